```python
import jax
import jax.numpy as jnp
from jax import lax
import numpy as np


D_MODEL = 1024
BATCH = 8
SEQ = 8192
DEPTH = 2

GRID_W = 64
CTX_LEN = 256
CONV_W = 4
CHUNK = 128
LN_EPS = 1e-6
LRU_W = D_MODEL
LRU_BLOCKS = 8
LRU_BS = LRU_W // LRU_BLOCKS
LRU_C = 8.0
SSD_INNER = 2 * D_MODEL
SSD_HEADDIM = 64
SSD_HEADS = SSD_INNER // SSD_HEADDIM
SSD_STATE = 128
SSD_GROUPS = 8
SSD_HPG = SSD_HEADS // SSD_GROUPS
SSD_GN = SSD_GROUPS * SSD_STATE
SSD_XBC = SSD_INNER + 2 * SSD_GN
MLSTM_W = D_MODEL
MLSTM_HEADS = 4
MLSTM_HD = MLSTM_W // MLSTM_HEADS
N_BRANCH = 3
D_FF = 4 * D_MODEL
IN_SPLITS = (LRU_W, LRU_W, SSD_INNER, SSD_XBC, SSD_HEADS, SSD_HEADS, MLSTM_W, MLSTM_W, MLSTM_W, MLSTM_W, 4 * MLSTM_HEADS, N_BRANCH * D_MODEL)
N_IN = sum(IN_SPLITS)
DN_ALPHA = (2 * DEPTH) ** 0.25
DN_BETA = (8 * DEPTH) ** -0.25

kernel_name = 'hybrid_lru_ssd_mlstm_dit_block'


def _split(t, sizes):
    idx = np.cumsum(sizes)[:-1].tolist()
    return jnp.split(t, idx, axis=-1)


def _layernorm(x, g=None, b=None):
    xf = x.astype(jnp.float32)
    mu = jnp.mean(xf, -1, keepdims=True)
    var = jnp.mean(jnp.square(xf - mu), -1, keepdims=True)
    y = (xf - mu) * lax.rsqrt(var + LN_EPS)
    if g is not None:
        y = y * g.astype(jnp.float32) + b.astype(jnp.float32)
    return y.astype(x.dtype)


def _modulate(x, shift, scale):
    return _layernorm(x) * (1 + scale) + shift


def _to_col_major(h, rows):
    b, s, d = h.shape
    return h.reshape(b, rows, GRID_W, d).swapaxes(1, 2).reshape(b, s, d)


def _from_col_major(h, rows):
    b, s, d = h.shape
    return h.reshape(b, GRID_W, rows, d).swapaxes(1, 2).reshape(b, s, d)


def _dwconv(x, w, b):
    ch = x.shape[-1]
    y = lax.conv_general_dilated(x, w[:, None, :], window_strides=(1,),
                                 padding=[(CONV_W // 2, CONV_W - 1 - CONV_W // 2)],
                                 dimension_numbers=('NWC', 'WIO', 'NWC'), feature_group_count=ch)
    return y + b


def _lin_scan(a, b, h0):
    def comb(l, r):
        return (l[0] * r[0], r[0] * l[1] + r[1])
    a_cum, h = lax.associative_scan(comb, (a, b), axis=1)
    h = h + a_cum * h0[:, None]
    return h, h[:, -1]


def _rglru(xa, ya, p, h0):
    dt = xa.dtype
    f32 = jnp.float32
    xc = _dwconv(xa, p['lru_conv_w'], p['lru_conv_b'])
    bsz, L, _ = xc.shape
    xb = xc.reshape(bsz, L, LRU_BLOCKS, LRU_BS)
    r = jax.nn.sigmoid(jnp.einsum('blnh,dnhk->dblnk', xb, p['lru_w_r']).reshape(2, bsz, L, LRU_W).astype(f32)
                       + p['lru_b_r'].astype(f32)[:, None, None])
    i = jax.nn.sigmoid(jnp.einsum('blnh,dnhk->dblnk', xb, p['lru_w_i']).reshape(2, bsz, L, LRU_W).astype(f32)
                       + p['lru_b_i'].astype(f32)[:, None, None])
    log_a = -LRU_C * r * jax.nn.softplus(-p['lru_lambda'].astype(f32))[:, None, None]
    a = jnp.exp(log_a)
    bin_ = jnp.sqrt(1.0 - jnp.exp(2.0 * log_a)) * i * xc.astype(f32)[None]
    h_f, s_f = _lin_scan(a[0], bin_[0], h0[0])
    h_b, s_b = _lin_scan(jnp.flip(a[1], 1), jnp.flip(bin_[1], 1), h0[1])
    y = (h_f + jnp.flip(h_b, 1)).astype(dt) * jax.nn.gelu(ya)
    return y, jnp.stack([s_f, s_b])


def _ssd_scan(x, dA, Bm, Cm, h0):
    bsz, L, G, R, P = x.shape
    N = Bm.shape[-1]
    nc = L // CHUNK
    x = x.reshape(bsz, nc, CHUNK, G, R, P)
    Bm = Bm.reshape(bsz, nc, CHUNK, G, N)
    Cm = Cm.reshape(bsz, nc, CHUNK, G, N)
    a_cs = jnp.cumsum(dA.reshape(bsz, nc, CHUNK, G, R).transpose(0, 3, 4, 1, 2), axis=-1)
    mask = jnp.tril(jnp.ones((CHUNK, CHUNK), bool))
    lmat = jnp.exp(jnp.where(mask, a_cs[..., :, None] - a_cs[..., None, :], -jnp.inf))
    cb = jnp.einsum('bcqgn,bcsgn->bgcqs', Cm, Bm)
    y_diag = jnp.einsum('bgrcqs,bcsgrp->bcqgrp', cb[:, :, None] * lmat, x)
    decay_st = jnp.exp(a_cs[..., -1:] - a_cs).transpose(0, 3, 4, 1, 2)
    states = jnp.einsum('bcsgn,bcsgrp->bcgrpn', Bm, x * decay_st[..., None])
    chunk_decay = jnp.exp(a_cs[..., -1]).transpose(0, 3, 1, 2)
    end_states, h_final = _lin_scan(chunk_decay[..., None, None], states, h0)
    enter = jnp.concatenate([h0[:, None], end_states[:, :-1]], axis=1)
    y_off = jnp.einsum('bcqgn,bcgrpn->bcqgrp', Cm, enter) * jnp.exp(a_cs).transpose(0, 3, 4, 1, 2)[..., None]
    return (y_diag + y_off).reshape(bsz, L, G, R, P), h_final


def _ssd(z, xbc, dt_f, dt_b, p, h0):
    dt = z.dtype
    f32 = jnp.float32
    xbc = jax.nn.silu(_dwconv(xbc, p['ssd_conv_w'], p['ssd_conv_b']))
    xs, Bm, Cm = _split(xbc, (SSD_INNER, SSD_GN, SSD_GN))
    bsz, L, _ = xs.shape
    xs = xs.astype(f32).reshape(bsz, L, SSD_GROUPS, SSD_HPG, SSD_HEADDIM)
    Bm = Bm.astype(f32).reshape(bsz, L, SSD_GROUPS, SSD_STATE)
    Cm = Cm.astype(f32).reshape(bsz, L, SSD_GROUPS, SSD_STATE)
    delta = jax.nn.softplus(jnp.stack([dt_f, dt_b]).astype(f32) + p['ssd_dt_bias'].astype(f32)[:, None, None])
    A = -jnp.exp(p['ssd_a_log'].astype(f32))
    dA = (delta * A[:, None, None]).reshape(2, bsz, L, SSD_GROUPS, SSD_HPG)
    xdt = xs[None] * delta.reshape(2, bsz, L, SSD_GROUPS, SSD_HPG)[..., None]
    y_f, s_f = _ssd_scan(xdt[0], dA[0], Bm, Cm, h0[0])
    y_b, s_b = _ssd_scan(jnp.flip(xdt[1], 1), jnp.flip(dA[1], 1), jnp.flip(Bm, 1), jnp.flip(Cm, 1), h0[1])
    y = y_f + jnp.flip(y_b, 1) + xs * p['ssd_d'].astype(f32).reshape(SSD_GROUPS, SSD_HPG, 1)
    y = y.reshape(bsz, L, SSD_INNER) * jax.nn.silu(z.astype(f32))
    yg = y.reshape(bsz, L, SSD_GROUPS, SSD_INNER // SSD_GROUPS)
    yg = yg * lax.rsqrt(jnp.mean(jnp.square(yg), -1, keepdims=True) + LN_EPS)
    y = yg.reshape(bsz, L, SSD_INNER) * p['ssd_norm_w'].astype(f32)
    return y.astype(dt), jnp.stack([s_f, s_b])


def _mlstm_scan(q, k, v, log_i, log_f, state):
    bsz, L, H, Dh = q.shape
    nc = L // CHUNK
    q = q.reshape(bsz, nc, CHUNK, H, Dh)
    k = k.reshape(bsz, nc, CHUNK, H, Dh)
    v = v.reshape(bsz, nc, CHUNK, H, Dh)
    li = log_i.reshape(bsz, nc, CHUNK, H).transpose(0, 3, 1, 2)
    b = jnp.cumsum(log_f.reshape(bsz, nc, CHUNK, H).transpose(0, 3, 1, 2), axis=-1)
    g = b[..., -1]
    w = g[..., None] - b + li
    m_loc = jnp.max(w, -1)
    kw = k * jnp.exp(w - m_loc[..., None]).transpose(0, 2, 3, 1)[..., None]
    c_loc = jnp.einsum('bcshd,bcshe->bchde', kw, v)
    n_loc = jnp.sum(kw, axis=2)

    def step(carry, inp):
        cb_, nb_, m_ = carry
        g_c, ml_c, cl_c, nl_c = inp
        m_new = jnp.maximum(g_c + m_, ml_c)
        s_old = jnp.exp(g_c + m_ - m_new)
        s_loc = jnp.exp(ml_c - m_new)
        c_new = s_old[..., None, None] * cb_ + s_loc[..., None, None] * cl_c
        n_new = s_old[..., None] * nb_ + s_loc[..., None] * nl_c
        return (c_new, n_new, m_new), (cb_, nb_, m_)

    xs = (jnp.moveaxis(g, 2, 0), jnp.moveaxis(m_loc, 2, 0), jnp.moveaxis(c_loc, 1, 0), jnp.moveaxis(n_loc, 1, 0))
    final, (c_in, n_in, m_in) = lax.scan(step, state, xs)
    c_in = jnp.moveaxis(c_in, 0, 1)
    n_in = jnp.moveaxis(n_in, 0, 1)
    m_in = m_in.transpose(1, 2, 0)
    mask = jnp.tril(jnp.ones((CHUNK, CHUNK), bool))
    log_d = jnp.where(mask, b[..., :, None] - b[..., None, :] + li[..., None, :], -jnp.inf)
    inter = b + m_in[..., None]
    m_t = jnp.maximum(inter, jnp.max(log_d, -1))
    dmat = jnp.exp(log_d - m_t[..., None])
    wi = jnp.exp(inter - m_t).transpose(0, 2, 3, 1)
    s = jnp.einsum('bcqhd,bcshd->bhcqs', q, k) * dmat
    num = jnp.einsum('bhcqs,bcshd->bcqhd', s, v) + wi[..., None] * jnp.einsum('bcqhd,bchde->bcqhe', q, c_in)
    den = jnp.sum(s, -1).transpose(0, 2, 3, 1) + wi * jnp.einsum('bcqhd,bchd->bcqh', q, n_in)
    floor = jnp.exp(-m_t.transpose(0, 2, 3, 1))
    h = num / jnp.maximum(jnp.abs(den), floor)[..., None]
    return h.reshape(bsz, L, H, Dh), final


def _mlstm(q, k, v, o, gates, p, s0):
    dt = q.dtype
    f32 = jnp.float32
    qk = jax.nn.silu(_dwconv(jnp.concatenate([q, k], -1), p['ml_conv_w'], p['ml_conv_b']))
    q, k = jnp.split(qk, 2, axis=-1)
    bsz, L, _ = q.shape
    shp = (bsz, L, MLSTM_HEADS, MLSTM_HD)
    q = q.astype(f32).reshape(shp) * (MLSTM_HD ** -0.5)
    k = k.astype(f32).reshape(shp)
    v = v.astype(f32).reshape(shp)
    g = gates.astype(f32).reshape(bsz, L, 2, 2, MLSTM_HEADS) + p['ml_gate_b'].astype(f32)
    c0, n0, m0 = s0
    h_f, (c_f, n_f, m_f) = _mlstm_scan(q, k, v, g[:, :, 0, 0], jax.nn.log_sigmoid(g[:, :, 0, 1]), (c0[0], n0[0], m0[0]))
    h_b, (c_b, n_b, m_b) = _mlstm_scan(jnp.flip(q, 1), jnp.flip(k, 1), jnp.flip(v, 1), jnp.flip(g[:, :, 1, 0], 1),
                                        jnp.flip(jax.nn.log_sigmoid(g[:, :, 1, 1]), 1), (c0[1], n0[1], m0[1]))
    h = h_f + jnp.flip(h_b, 1)
    mu = jnp.mean(h, -1, keepdims=True)
    var = jnp.mean(jnp.square(h - mu), -1, keepdims=True)
    h = (h - mu) * lax.rsqrt(var + LN_EPS) * p['ml_norm_w'].astype(f32).reshape(MLSTM_HEADS, MLSTM_HD)
    h = h.reshape(bsz, L, MLSTM_W) * jax.nn.sigmoid(o.astype(f32))
    return h.astype(dt), (jnp.stack([c_f, c_b]), jnp.stack([n_f, n_b]), jnp.stack([m_f, m_b]))


def _mixer(h, p, states):
    sa0, sb0, sc0 = states
    (lx, ly, sz, sxbc, sdtf, sdtb, mq, mk, mv, mo, mg, gates) = _split(h @ p['w_in'], IN_SPLITS)
    ya, sa = _rglru(lx, ly, p, sa0)
    yb, sb = _ssd(sz, sxbc, sdtf, sdtb, p, sb0)
    yc, sc = _mlstm(mq, mk, mv, mo, mg, p, sc0)
    return (ya, yb, yc, gates), (sa, sb, sc)


def _merge(br, p):
    ya, yb, yc, gates = br
    ga, gb, gc = jnp.split(jax.nn.sigmoid(gates.astype(jnp.float32)).astype(ya.dtype), N_BRANCH, axis=-1)
    m = ga * (ya @ p['w_br_a']) + gb * (yb @ p['w_br_b']) + gc * (yc @ p['w_br_c'])
    return m @ p['w_out'] + p['b_out']


def _mlp(h, p):
    u = jax.nn.relu(h @ p['w_ff1'] + p['b_ff1'])
    return jnp.square(u) @ p['w_ff2'] + p['b_ff2']


def _fwd_setup_inputs(seed: int = 0) -> dict:
    key = jax.random.key(seed)
    ks = iter(jax.random.split(key, 48))
    f32 = jnp.float32
    L = DEPTH

    def nrm(shape, s):
        return jax.random.normal(next(ks), shape, f32) * s

    x = nrm((BATCH, SEQ, D_MODEL), 1.0)
    c = nrm((BATCH, D_MODEL), 1.0)
    ctx = nrm((BATCH, CTX_LEN, D_MODEL), 1.0)
    c_ctx = nrm((D_MODEL,), 1.0)
    w_ada = nrm((L, D_MODEL, 6 * D_MODEL), 0.5 * D_MODEL ** -0.5)
    b_ada = nrm((L, 6 * D_MODEL), 0.01)
    w_in = nrm((L, D_MODEL, N_IN), D_MODEL ** -0.5)
    lru_conv_w = nrm((L, CONV_W, LRU_W), CONV_W ** -0.5)
    lru_conv_b = nrm((L, LRU_W), 0.01)
    lru_w_r = nrm((L, 2, LRU_BLOCKS, LRU_BS, LRU_BS), LRU_BS ** -0.5)
    lru_b_r = nrm((L, 2, LRU_W), 0.01)
    lru_w_i = nrm((L, 2, LRU_BLOCKS, LRU_BS, LRU_BS), LRU_BS ** -0.5)
    lru_b_i = nrm((L, 2, LRU_W), 0.01)
    a0 = jax.random.uniform(next(ks), (L, 2, LRU_W), f32, 0.9, 0.999) ** (1.0 / LRU_C)
    lru_lambda = jnp.log(a0) - jnp.log1p(-a0)
    ssd_conv_w = nrm((L, CONV_W, SSD_XBC), CONV_W ** -0.5)
    ssd_conv_b = nrm((L, SSD_XBC), 0.01)
    dt0 = jnp.exp(jax.random.uniform(next(ks), (L, 2, SSD_HEADS), f32, float(np.log(1e-3)), float(np.log(1e-1))))
    ssd_dt_bias = dt0 + jnp.log(-jnp.expm1(-dt0))
    ssd_a_log = jnp.log(jax.random.uniform(next(ks), (L, 2, SSD_HEADS), f32, 1.0, 16.0))
    ssd_d = 1.0 + nrm((L, SSD_HEADS), 0.01)
    ssd_norm_w = 1.0 + nrm((L, SSD_INNER), 0.01)
    ml_conv_w = nrm((L, CONV_W, 2 * MLSTM_W), CONV_W ** -0.5)
    ml_conv_b = nrm((L, 2 * MLSTM_W), 0.01)
    ig_b = nrm((L, 2, 1, MLSTM_HEADS), 0.1)
    fg_b = jnp.linspace(3.0, 6.0, MLSTM_HEADS, dtype=f32) + nrm((L, 2, 1, MLSTM_HEADS), 0.1)
    ml_gate_b = jnp.concatenate([ig_b, fg_b], axis=2)
    ml_norm_w = 1.0 + nrm((L, MLSTM_W), 0.01)
    w_br_a = nrm((L, LRU_W, D_MODEL), DN_BETA * LRU_W ** -0.5)
    w_br_b = nrm((L, SSD_INNER, D_MODEL), DN_BETA * SSD_INNER ** -0.5)
    w_br_c = nrm((L, MLSTM_W, D_MODEL), DN_BETA * MLSTM_W ** -0.5)
    w_out = nrm((L, D_MODEL, D_MODEL), DN_BETA * D_MODEL ** -0.5)
    b_out = nrm((L, D_MODEL), 0.01)
    ln1_g = 1.0 + nrm((L, D_MODEL), 0.01)
    ln1_b = nrm((L, D_MODEL), 0.01)
    w_ff1 = nrm((L, D_MODEL, D_FF), D_MODEL ** -0.5)
    b_ff1 = nrm((L, D_FF), 0.01)
    w_ff2 = nrm((L, D_FF, D_MODEL), DN_BETA * D_FF ** -0.5)
    b_ff2 = nrm((L, D_MODEL), 0.01)
    ln2_g = 1.0 + nrm((L, D_MODEL), 0.01)
    ln2_b = nrm((L, D_MODEL), 0.01)
    return {'x': x, 'c': c, 'ctx': ctx, 'c_ctx': c_ctx, 'w_ada': w_ada, 'b_ada': b_ada, 'w_in': w_in,
            'lru_conv_w': lru_conv_w, 'lru_conv_b': lru_conv_b, 'lru_w_r': lru_w_r, 'lru_b_r': lru_b_r,
            'lru_w_i': lru_w_i, 'lru_b_i': lru_b_i, 'lru_lambda': lru_lambda,
            'ssd_conv_w': ssd_conv_w, 'ssd_conv_b': ssd_conv_b, 'ssd_dt_bias': ssd_dt_bias, 'ssd_a_log': ssd_a_log,
            'ssd_d': ssd_d, 'ssd_norm_w': ssd_norm_w, 'ml_conv_w': ml_conv_w, 'ml_conv_b': ml_conv_b,
            'ml_gate_b': ml_gate_b, 'ml_norm_w': ml_norm_w, 'w_br_a': w_br_a, 'w_br_b': w_br_b, 'w_br_c': w_br_c,
            'w_out': w_out, 'b_out': b_out, 'ln1_g': ln1_g, 'ln1_b': ln1_b, 'w_ff1': w_ff1, 'b_ff1': b_ff1,
            'w_ff2': w_ff2, 'b_ff2': b_ff2, 'ln2_g': ln2_g, 'ln2_b': ln2_b}


def _fwd_reference(x, c, ctx, c_ctx, w_ada, b_ada, w_in, lru_conv_w, lru_conv_b, lru_w_r, lru_b_r, lru_w_i, lru_b_i,
              lru_lambda, ssd_conv_w, ssd_conv_b, ssd_dt_bias, ssd_a_log, ssd_d, ssd_norm_w, ml_conv_w, ml_conv_b,
              ml_gate_b, ml_norm_w, w_br_a, w_br_b, w_br_c, w_out, b_out, ln1_g, ln1_b, w_ff1, b_ff1, w_ff2, b_ff2,
              ln2_g, ln2_b):
    bsz, seq, _ = x.shape
    rows = seq // GRID_W
    f32 = jnp.float32
    ctx_init = (jnp.zeros((2, bsz, LRU_W), f32),
                jnp.zeros((2, bsz, SSD_GROUPS, SSD_HPG, SSD_HEADDIM, SSD_STATE), f32),
                (jnp.zeros((2, bsz, MLSTM_HEADS, MLSTM_HD, MLSTM_HD), f32),
                 jnp.zeros((2, bsz, MLSTM_HEADS, MLSTM_HD), f32),
                 jnp.zeros((2, bsz, MLSTM_HEADS), f32)))
    for l in range(DEPTH):
        p = {'w_in': w_in[l], 'lru_conv_w': lru_conv_w[l], 'lru_conv_b': lru_conv_b[l], 'lru_w_r': lru_w_r[l],
             'lru_b_r': lru_b_r[l], 'lru_w_i': lru_w_i[l], 'lru_b_i': lru_b_i[l], 'lru_lambda': lru_lambda[l],
             'ssd_conv_w': ssd_conv_w[l], 'ssd_conv_b': ssd_conv_b[l], 'ssd_dt_bias': ssd_dt_bias[l],
             'ssd_a_log': ssd_a_log[l], 'ssd_d': ssd_d[l], 'ssd_norm_w': ssd_norm_w[l],
             'ml_conv_w': ml_conv_w[l], 'ml_conv_b': ml_conv_b[l], 'ml_gate_b': ml_gate_b[l],
             'ml_norm_w': ml_norm_w[l], 'w_br_a': w_br_a[l], 'w_br_b': w_br_b[l], 'w_br_c': w_br_c[l],
             'w_out': w_out[l], 'b_out': b_out[l], 'w_ff1': w_ff1[l], 'b_ff1': b_ff1[l],
             'w_ff2': w_ff2[l], 'b_ff2': b_ff2[l]}
        col_major = (l % 2 == 1)
        mod_x = (jax.nn.silu(c) @ w_ada[l] + b_ada[l])[:, None, :]
        mod_c = jax.nn.silu(c_ctx) @ w_ada[l] + b_ada[l]
        sh1x, sc1x, g1x, sh2x, sc2x, g2x = jnp.split(mod_x, 6, axis=-1)
        sh1c, sc1c, g1c, sh2c, sc2c, g2c = jnp.split(mod_c, 6, axis=-1)
        br_c, ctx_states = _mixer(_modulate(ctx, sh1c, sc1c), p, ctx_init)
        hx = _modulate(x, sh1x, sc1x)
        if col_major:
            hx = _to_col_major(hx, rows)
        br_x, _ = _mixer(hx, p, ctx_states)
        ox = _merge(br_x, p)
        if col_major:
            ox = _from_col_major(ox, rows)
        x = _layernorm(DN_ALPHA * x + g1x * ox, ln1_g[l], ln1_b[l])
        x = _layernorm(DN_ALPHA * x + g2x * _mlp(_modulate(x, sh2x, sc2x), p), ln2_g[l], ln2_b[l])
        if l < DEPTH - 1:
            ctx = _layernorm(DN_ALPHA * ctx + g1c * _merge(br_c, p), ln1_g[l], ln1_b[l])
            ctx = _layernorm(DN_ALPHA * ctx + g2c * _mlp(_modulate(ctx, sh2c, sc2c), p), ln2_g[l], ln2_b[l])
    return x


import jax as _jax
import jax.numpy as _jnp

TWIN_FORMAT = 'train_step'
FWD_PARAMS = ['x', 'c', 'ctx', 'c_ctx', 'w_ada', 'b_ada', 'w_in', 'lru_conv_w', 'lru_conv_b', 'lru_w_r', 'lru_b_r', 'lru_w_i', 'lru_b_i', 'lru_lambda', 'ssd_conv_w', 'ssd_conv_b', 'ssd_dt_bias', 'ssd_a_log', 'ssd_d', 'ssd_norm_w', 'ml_conv_w', 'ml_conv_b', 'ml_gate_b', 'ml_norm_w', 'w_br_a', 'w_br_b', 'w_br_c', 'w_out', 'b_out', 'ln1_g', 'ln1_b', 'w_ff1', 'b_ff1', 'w_ff2', 'b_ff2', 'ln2_g', 'ln2_b']
TWIN_WEIGHTS = ['c_ctx', 'w_ada', 'b_ada', 'w_in', 'lru_conv_w', 'lru_conv_b', 'lru_w_r', 'lru_b_r', 'lru_w_i', 'lru_b_i', 'lru_lambda', 'ssd_conv_w', 'ssd_conv_b', 'ssd_dt_bias', 'ssd_a_log', 'ssd_d', 'ssd_norm_w', 'ml_conv_w', 'ml_conv_b', 'ml_gate_b', 'ml_norm_w', 'w_br_a', 'w_br_b', 'w_br_c', 'w_out', 'b_out', 'ln1_g', 'ln1_b', 'w_ff1', 'b_ff1', 'w_ff2', 'b_ff2', 'ln2_g', 'ln2_b']
TWIN_DIFF_INPUT = 'x'
TWIN_INPUTS = ['x', 'c', 'ctx', 'c_ctx', 'w_ada', 'b_ada', 'w_in', 'lru_conv_w', 'lru_conv_b', 'lru_w_r', 'lru_b_r', 'lru_w_i', 'lru_b_i', 'lru_lambda', 'ssd_conv_w', 'ssd_conv_b', 'ssd_dt_bias', 'ssd_a_log', 'ssd_d', 'ssd_norm_w', 'ml_conv_w', 'ml_conv_b', 'ml_gate_b', 'ml_norm_w', 'w_br_a', 'w_br_b', 'w_br_c', 'w_out', 'b_out', 'ln1_g', 'ln1_b', 'w_ff1', 'b_ff1', 'w_ff2', 'b_ff2', 'ln2_g', 'ln2_b', 'loss_target', 'm_c_ctx', 'm_w_ada', 'm_b_ada', 'm_w_in', 'm_lru_conv_w', 'm_lru_conv_b', 'm_lru_w_r', 'm_lru_b_r', 'm_lru_w_i', 'm_lru_b_i', 'm_lru_lambda', 'm_ssd_conv_w', 'm_ssd_conv_b', 'm_ssd_dt_bias', 'm_ssd_a_log', 'm_ssd_d', 'm_ssd_norm_w', 'm_ml_conv_w', 'm_ml_conv_b', 'm_ml_gate_b', 'm_ml_norm_w', 'm_w_br_a', 'm_w_br_b', 'm_w_br_c', 'm_w_out', 'm_b_out', 'm_ln1_g', 'm_ln1_b', 'm_w_ff1', 'm_b_ff1', 'm_w_ff2', 'm_b_ff2', 'm_ln2_g', 'm_ln2_b', 'v_c_ctx', 'v_w_ada', 'v_b_ada', 'v_w_in', 'v_lru_conv_w', 'v_lru_conv_b', 'v_lru_w_r', 'v_lru_b_r', 'v_lru_w_i', 'v_lru_b_i', 'v_lru_lambda', 'v_ssd_conv_w', 'v_ssd_conv_b', 'v_ssd_dt_bias', 'v_ssd_a_log', 'v_ssd_d', 'v_ssd_norm_w', 'v_ml_conv_w', 'v_ml_conv_b', 'v_ml_gate_b', 'v_ml_norm_w', 'v_w_br_a', 'v_w_br_b', 'v_w_br_c', 'v_w_out', 'v_b_out', 'v_ln1_g', 'v_ln1_b', 'v_w_ff1', 'v_b_ff1', 'v_w_ff2', 'v_b_ff2', 'v_ln2_g', 'v_ln2_b']
TWIN_OUTPUTS = ['loss', 'grad_x', 'grad_c_ctx', 'grad_w_ada', 'grad_b_ada', 'grad_w_in', 'grad_lru_conv_w', 'grad_lru_conv_b', 'grad_lru_w_r', 'grad_lru_b_r', 'grad_lru_w_i', 'grad_lru_b_i', 'grad_lru_lambda', 'grad_ssd_conv_w', 'grad_ssd_conv_b', 'grad_ssd_dt_bias', 'grad_ssd_a_log', 'grad_ssd_d', 'grad_ssd_norm_w', 'grad_ml_conv_w', 'grad_ml_conv_b', 'grad_ml_gate_b', 'grad_ml_norm_w', 'grad_w_br_a', 'grad_w_br_b', 'grad_w_br_c', 'grad_w_out', 'grad_b_out', 'grad_ln1_g', 'grad_ln1_b', 'grad_w_ff1', 'grad_b_ff1', 'grad_w_ff2', 'grad_b_ff2', 'grad_ln2_g', 'grad_ln2_b', 'delta_c_ctx', 'delta_w_ada', 'delta_b_ada', 'delta_w_in', 'delta_lru_conv_w', 'delta_lru_conv_b', 'delta_lru_w_r', 'delta_lru_b_r', 'delta_lru_w_i', 'delta_lru_b_i', 'delta_lru_lambda', 'delta_ssd_conv_w', 'delta_ssd_conv_b', 'delta_ssd_dt_bias', 'delta_ssd_a_log', 'delta_ssd_d', 'delta_ssd_norm_w', 'delta_ml_conv_w', 'delta_ml_conv_b', 'delta_ml_gate_b', 'delta_ml_norm_w', 'delta_w_br_a', 'delta_w_br_b', 'delta_w_br_c', 'delta_w_out', 'delta_b_out', 'delta_ln1_g', 'delta_ln1_b', 'delta_w_ff1', 'delta_b_ff1', 'delta_w_ff2', 'delta_b_ff2', 'delta_ln2_g', 'delta_ln2_b', 'new_m_c_ctx', 'new_m_w_ada', 'new_m_b_ada', 'new_m_w_in', 'new_m_lru_conv_w', 'new_m_lru_conv_b', 'new_m_lru_w_r', 'new_m_lru_b_r', 'new_m_lru_w_i', 'new_m_lru_b_i', 'new_m_lru_lambda', 'new_m_ssd_conv_w', 'new_m_ssd_conv_b', 'new_m_ssd_dt_bias', 'new_m_ssd_a_log', 'new_m_ssd_d', 'new_m_ssd_norm_w', 'new_m_ml_conv_w', 'new_m_ml_conv_b', 'new_m_ml_gate_b', 'new_m_ml_norm_w', 'new_m_w_br_a', 'new_m_w_br_b', 'new_m_w_br_c', 'new_m_w_out', 'new_m_b_out', 'new_m_ln1_g', 'new_m_ln1_b', 'new_m_w_ff1', 'new_m_b_ff1', 'new_m_w_ff2', 'new_m_b_ff2', 'new_m_ln2_g', 'new_m_ln2_b', 'new_v_c_ctx', 'new_v_w_ada', 'new_v_b_ada', 'new_v_w_in', 'new_v_lru_conv_w', 'new_v_lru_conv_b', 'new_v_lru_w_r', 'new_v_lru_b_r', 'new_v_lru_w_i', 'new_v_lru_b_i', 'new_v_lru_lambda', 'new_v_ssd_conv_w', 'new_v_ssd_conv_b', 'new_v_ssd_dt_bias', 'new_v_ssd_a_log', 'new_v_ssd_d', 'new_v_ssd_norm_w', 'new_v_ml_conv_w', 'new_v_ml_conv_b', 'new_v_ml_gate_b', 'new_v_ml_norm_w', 'new_v_w_br_a', 'new_v_w_br_b', 'new_v_w_br_c', 'new_v_w_out', 'new_v_b_out', 'new_v_ln1_g', 'new_v_ln1_b', 'new_v_w_ff1', 'new_v_b_ff1', 'new_v_w_ff2', 'new_v_b_ff2', 'new_v_ln2_g', 'new_v_ln2_b']
TWIN_LEAF_KINDS = {'loss': 'loss', 'grad_x': 'grad_x', 'grad_c_ctx': 'grad_w', 'grad_w_ada': 'grad_w', 'grad_b_ada': 'grad_w', 'grad_w_in': 'grad_w', 'grad_lru_conv_w': 'grad_w', 'grad_lru_conv_b': 'grad_w', 'grad_lru_w_r': 'grad_w', 'grad_lru_b_r': 'grad_w', 'grad_lru_w_i': 'grad_w', 'grad_lru_b_i': 'grad_w', 'grad_lru_lambda': 'grad_w', 'grad_ssd_conv_w': 'grad_w', 'grad_ssd_conv_b': 'grad_w', 'grad_ssd_dt_bias': 'grad_w', 'grad_ssd_a_log': 'grad_w', 'grad_ssd_d': 'grad_w', 'grad_ssd_norm_w': 'grad_w', 'grad_ml_conv_w': 'grad_w', 'grad_ml_conv_b': 'grad_w', 'grad_ml_gate_b': 'grad_w', 'grad_ml_norm_w': 'grad_w', 'grad_w_br_a': 'grad_w', 'grad_w_br_b': 'grad_w', 'grad_w_br_c': 'grad_w', 'grad_w_out': 'grad_w', 'grad_b_out': 'grad_w', 'grad_ln1_g': 'grad_w', 'grad_ln1_b': 'grad_w', 'grad_w_ff1': 'grad_w', 'grad_b_ff1': 'grad_w', 'grad_w_ff2': 'grad_w', 'grad_b_ff2': 'grad_w', 'grad_ln2_g': 'grad_w', 'grad_ln2_b': 'grad_w', 'delta_c_ctx': 'delta_w', 'delta_w_ada': 'delta_w', 'delta_b_ada': 'delta_w', 'delta_w_in': 'delta_w', 'delta_lru_conv_w': 'delta_w', 'delta_lru_conv_b': 'delta_w', 'delta_lru_w_r': 'delta_w', 'delta_lru_b_r': 'delta_w', 'delta_lru_w_i': 'delta_w', 'delta_lru_b_i': 'delta_w', 'delta_lru_lambda': 'delta_w', 'delta_ssd_conv_w': 'delta_w', 'delta_ssd_conv_b': 'delta_w', 'delta_ssd_dt_bias': 'delta_w', 'delta_ssd_a_log': 'delta_w', 'delta_ssd_d': 'delta_w', 'delta_ssd_norm_w': 'delta_w', 'delta_ml_conv_w': 'delta_w', 'delta_ml_conv_b': 'delta_w', 'delta_ml_gate_b': 'delta_w', 'delta_ml_norm_w': 'delta_w', 'delta_w_br_a': 'delta_w', 'delta_w_br_b': 'delta_w', 'delta_w_br_c': 'delta_w', 'delta_w_out': 'delta_w', 'delta_b_out': 'delta_w', 'delta_ln1_g': 'delta_w', 'delta_ln1_b': 'delta_w', 'delta_w_ff1': 'delta_w', 'delta_b_ff1': 'delta_w', 'delta_w_ff2': 'delta_w', 'delta_b_ff2': 'delta_w', 'delta_ln2_g': 'delta_w', 'delta_ln2_b': 'delta_w', 'new_m_c_ctx': 'new_m', 'new_m_w_ada': 'new_m', 'new_m_b_ada': 'new_m', 'new_m_w_in': 'new_m', 'new_m_lru_conv_w': 'new_m', 'new_m_lru_conv_b': 'new_m', 'new_m_lru_w_r': 'new_m', 'new_m_lru_b_r': 'new_m', 'new_m_lru_w_i': 'new_m', 'new_m_lru_b_i': 'new_m', 'new_m_lru_lambda': 'new_m', 'new_m_ssd_conv_w': 'new_m', 'new_m_ssd_conv_b': 'new_m', 'new_m_ssd_dt_bias': 'new_m', 'new_m_ssd_a_log': 'new_m', 'new_m_ssd_d': 'new_m', 'new_m_ssd_norm_w': 'new_m', 'new_m_ml_conv_w': 'new_m', 'new_m_ml_conv_b': 'new_m', 'new_m_ml_gate_b': 'new_m', 'new_m_ml_norm_w': 'new_m', 'new_m_w_br_a': 'new_m', 'new_m_w_br_b': 'new_m', 'new_m_w_br_c': 'new_m', 'new_m_w_out': 'new_m', 'new_m_b_out': 'new_m', 'new_m_ln1_g': 'new_m', 'new_m_ln1_b': 'new_m', 'new_m_w_ff1': 'new_m', 'new_m_b_ff1': 'new_m', 'new_m_w_ff2': 'new_m', 'new_m_b_ff2': 'new_m', 'new_m_ln2_g': 'new_m', 'new_m_ln2_b': 'new_m', 'new_v_c_ctx': 'new_v', 'new_v_w_ada': 'new_v', 'new_v_b_ada': 'new_v', 'new_v_w_in': 'new_v', 'new_v_lru_conv_w': 'new_v', 'new_v_lru_conv_b': 'new_v', 'new_v_lru_w_r': 'new_v', 'new_v_lru_b_r': 'new_v', 'new_v_lru_w_i': 'new_v', 'new_v_lru_b_i': 'new_v', 'new_v_lru_lambda': 'new_v', 'new_v_ssd_conv_w': 'new_v', 'new_v_ssd_conv_b': 'new_v', 'new_v_ssd_dt_bias': 'new_v', 'new_v_ssd_a_log': 'new_v', 'new_v_ssd_d': 'new_v', 'new_v_ssd_norm_w': 'new_v', 'new_v_ml_conv_w': 'new_v', 'new_v_ml_conv_b': 'new_v', 'new_v_ml_gate_b': 'new_v', 'new_v_ml_norm_w': 'new_v', 'new_v_w_br_a': 'new_v', 'new_v_w_br_b': 'new_v', 'new_v_w_br_c': 'new_v', 'new_v_w_out': 'new_v', 'new_v_b_out': 'new_v', 'new_v_ln1_g': 'new_v', 'new_v_ln1_b': 'new_v', 'new_v_w_ff1': 'new_v', 'new_v_b_ff1': 'new_v', 'new_v_w_ff2': 'new_v', 'new_v_b_ff2': 'new_v', 'new_v_ln2_g': 'new_v', 'new_v_ln2_b': 'new_v'}


def _forward(args):
    return _fwd_reference(*[args[k] for k in FWD_PARAMS])


def _output_shape():
    def fwd():
        inp = _fwd_setup_inputs(0)
        return _fwd_reference(*[inp[k] for k in FWD_PARAMS])
    out = _jax.eval_shape(fwd)
    return out.shape, out.dtype

N_MICROBATCH = 1
ADAM_LR = 0.001
ADAM_B1 = 0.9
ADAM_B2 = 0.999
ADAM_EPS = 1e-08
ADAM_WD = 0.01
ADAM_STEP = 10
PER_EXAMPLE_BATCH_AXIS = {'x': 0, 'c': 0, 'ctx': 0, 'loss_target': 0}
SHARED_INPUTS = []
_WEIGHT_DTYPES = {'c_ctx': _jnp.float32, 'w_ada': _jnp.float32, 'b_ada': _jnp.float32, 'w_in': _jnp.float32, 'lru_conv_w': _jnp.float32, 'lru_conv_b': _jnp.float32, 'lru_w_r': _jnp.float32, 'lru_b_r': _jnp.float32, 'lru_w_i': _jnp.float32, 'lru_b_i': _jnp.float32, 'lru_lambda': _jnp.float32, 'ssd_conv_w': _jnp.float32, 'ssd_conv_b': _jnp.float32, 'ssd_dt_bias': _jnp.float32, 'ssd_a_log': _jnp.float32, 'ssd_d': _jnp.float32, 'ssd_norm_w': _jnp.float32, 'ml_conv_w': _jnp.float32, 'ml_conv_b': _jnp.float32, 'ml_gate_b': _jnp.float32, 'ml_norm_w': _jnp.float32, 'w_br_a': _jnp.float32, 'w_br_b': _jnp.float32, 'w_br_c': _jnp.float32, 'w_out': _jnp.float32, 'b_out': _jnp.float32, 'ln1_g': _jnp.float32, 'ln1_b': _jnp.float32, 'w_ff1': _jnp.float32, 'b_ff1': _jnp.float32, 'w_ff2': _jnp.float32, 'b_ff2': _jnp.float32, 'ln2_g': _jnp.float32, 'ln2_b': _jnp.float32}
MOMENT_SCALE = {'c_ctx': 8.279368e-03, 'w_ada': 4.430122e-02, 'b_ada': 8.067958e-02, 'w_in': 9.126152e-03, 'lru_conv_w': 2.367110e-02, 'lru_conv_b': 5.913045e-02, 'lru_w_r': 1.305609e-03, 'lru_b_r': 2.224374e-03, 'lru_w_i': 2.675657e-03, 'lru_b_i': 5.123264e-03, 'lru_lambda': 5.897884e-03, 'ssd_conv_w': 3.802096e-03, 'ssd_conv_b': 5.256692e-03, 'ssd_dt_bias': 8.307288e-03, 'ssd_a_log': 1.709937e-02, 'ssd_d': 1.740004e-02, 'ssd_norm_w': 5.231804e-03, 'ml_conv_w': 9.500525e-04, 'ml_conv_b': 7.721483e-04, 'ml_gate_b': 8.224738e-03, 'ml_norm_w': 4.060747e-03, 'w_br_a': 4.770562e-02, 'w_br_b': 1.477889e-02, 'w_br_c': 8.674942e-03, 'w_out': 4.924661e-02, 'b_out': 6.611387e-02, 'ln1_g': 9.575141e-01, 'ln1_b': 4.595116e-01, 'w_ff1': 2.231533e-02, 'b_ff1': 2.083039e-02, 'w_ff2': 7.885646e-02, 'b_ff2': 5.829987e-02, 'ln2_g': 4.526483e+01, 'ln2_b': 2.106358e+00}


def _to_microbatches(a, axis):
    t = _jnp.moveaxis(a, axis, 0)
    t = t.reshape((N_MICROBATCH, t.shape[0] // N_MICROBATCH) + t.shape[1:])
    return _jnp.moveaxis(t, 1, axis + 1)


def setup_inputs(seed: int = 0) -> dict:
    inp = _fwd_setup_inputs(seed)
    key = _jax.random.fold_in(_jax.random.key(seed), 7919)
    shape, _ = _output_shape()
    out = dict(inp)
    out["loss_target"] = _jax.random.normal(_jax.random.fold_in(key, 0), shape, _jnp.float32)
    for i, name in enumerate(TWIN_WEIGHTS):
        w = inp[name].astype(_jnp.float32)
        if MOMENT_SCALE is None:
            s = _jnp.sqrt(_jnp.mean(_jnp.square(w)) + 1e-30)
        else:
            s = MOMENT_SCALE[name]
        km, kv = _jax.random.split(_jax.random.fold_in(key, i + 1))
        out[name] = w
        out["m_" + name] = s * _jax.random.normal(km, w.shape, _jnp.float32)
        out["v_" + name] = (s * s) * _jax.random.uniform(kv, w.shape, _jnp.float32, 0.5, 1.5)
    if N_MICROBATCH > 1:
        for name, axis in PER_EXAMPLE_BATCH_AXIS.items():
            out[name] = _to_microbatches(out[name], axis)
    return {'x': out['x'], 'c': out['c'], 'ctx': out['ctx'], 'c_ctx': out['c_ctx'], 'w_ada': out['w_ada'], 'b_ada': out['b_ada'], 'w_in': out['w_in'], 'lru_conv_w': out['lru_conv_w'], 'lru_conv_b': out['lru_conv_b'], 'lru_w_r': out['lru_w_r'], 'lru_b_r': out['lru_b_r'], 'lru_w_i': out['lru_w_i'], 'lru_b_i': out['lru_b_i'], 'lru_lambda': out['lru_lambda'], 'ssd_conv_w': out['ssd_conv_w'], 'ssd_conv_b': out['ssd_conv_b'], 'ssd_dt_bias': out['ssd_dt_bias'], 'ssd_a_log': out['ssd_a_log'], 'ssd_d': out['ssd_d'], 'ssd_norm_w': out['ssd_norm_w'], 'ml_conv_w': out['ml_conv_w'], 'ml_conv_b': out['ml_conv_b'], 'ml_gate_b': out['ml_gate_b'], 'ml_norm_w': out['ml_norm_w'], 'w_br_a': out['w_br_a'], 'w_br_b': out['w_br_b'], 'w_br_c': out['w_br_c'], 'w_out': out['w_out'], 'b_out': out['b_out'], 'ln1_g': out['ln1_g'], 'ln1_b': out['ln1_b'], 'w_ff1': out['w_ff1'], 'b_ff1': out['b_ff1'], 'w_ff2': out['w_ff2'], 'b_ff2': out['b_ff2'], 'ln2_g': out['ln2_g'], 'ln2_b': out['ln2_b'], 'loss_target': out['loss_target'], 'm_c_ctx': out['m_c_ctx'], 'm_w_ada': out['m_w_ada'], 'm_b_ada': out['m_b_ada'], 'm_w_in': out['m_w_in'], 'm_lru_conv_w': out['m_lru_conv_w'], 'm_lru_conv_b': out['m_lru_conv_b'], 'm_lru_w_r': out['m_lru_w_r'], 'm_lru_b_r': out['m_lru_b_r'], 'm_lru_w_i': out['m_lru_w_i'], 'm_lru_b_i': out['m_lru_b_i'], 'm_lru_lambda': out['m_lru_lambda'], 'm_ssd_conv_w': out['m_ssd_conv_w'], 'm_ssd_conv_b': out['m_ssd_conv_b'], 'm_ssd_dt_bias': out['m_ssd_dt_bias'], 'm_ssd_a_log': out['m_ssd_a_log'], 'm_ssd_d': out['m_ssd_d'], 'm_ssd_norm_w': out['m_ssd_norm_w'], 'm_ml_conv_w': out['m_ml_conv_w'], 'm_ml_conv_b': out['m_ml_conv_b'], 'm_ml_gate_b': out['m_ml_gate_b'], 'm_ml_norm_w': out['m_ml_norm_w'], 'm_w_br_a': out['m_w_br_a'], 'm_w_br_b': out['m_w_br_b'], 'm_w_br_c': out['m_w_br_c'], 'm_w_out': out['m_w_out'], 'm_b_out': out['m_b_out'], 'm_ln1_g': out['m_ln1_g'], 'm_ln1_b': out['m_ln1_b'], 'm_w_ff1': out['m_w_ff1'], 'm_b_ff1': out['m_b_ff1'], 'm_w_ff2': out['m_w_ff2'], 'm_b_ff2': out['m_b_ff2'], 'm_ln2_g': out['m_ln2_g'], 'm_ln2_b': out['m_ln2_b'], 'v_c_ctx': out['v_c_ctx'], 'v_w_ada': out['v_w_ada'], 'v_b_ada': out['v_b_ada'], 'v_w_in': out['v_w_in'], 'v_lru_conv_w': out['v_lru_conv_w'], 'v_lru_conv_b': out['v_lru_conv_b'], 'v_lru_w_r': out['v_lru_w_r'], 'v_lru_b_r': out['v_lru_b_r'], 'v_lru_w_i': out['v_lru_w_i'], 'v_lru_b_i': out['v_lru_b_i'], 'v_lru_lambda': out['v_lru_lambda'], 'v_ssd_conv_w': out['v_ssd_conv_w'], 'v_ssd_conv_b': out['v_ssd_conv_b'], 'v_ssd_dt_bias': out['v_ssd_dt_bias'], 'v_ssd_a_log': out['v_ssd_a_log'], 'v_ssd_d': out['v_ssd_d'], 'v_ssd_norm_w': out['v_ssd_norm_w'], 'v_ml_conv_w': out['v_ml_conv_w'], 'v_ml_conv_b': out['v_ml_conv_b'], 'v_ml_gate_b': out['v_ml_gate_b'], 'v_ml_norm_w': out['v_ml_norm_w'], 'v_w_br_a': out['v_w_br_a'], 'v_w_br_b': out['v_w_br_b'], 'v_w_br_c': out['v_w_br_c'], 'v_w_out': out['v_w_out'], 'v_b_out': out['v_b_out'], 'v_ln1_g': out['v_ln1_g'], 'v_ln1_b': out['v_ln1_b'], 'v_w_ff1': out['v_w_ff1'], 'v_b_ff1': out['v_b_ff1'], 'v_w_ff2': out['v_w_ff2'], 'v_b_ff2': out['v_b_ff2'], 'v_ln2_g': out['v_ln2_g'], 'v_ln2_b': out['v_ln2_b']}


def _loss(weights, diff, rest, loss_target):
    with _jax.named_scope("forward"):
        args = {**rest, TWIN_DIFF_INPUT: diff, **{k: w.astype(_WEIGHT_DTYPES[k]) for k, w in weights.items()}}
        y = _forward(args)
    with _jax.named_scope("loss_head"):
        err = _jnp.square(y.astype(_jnp.float32) - loss_target)
        return 0.5 * _jnp.sum(_jnp.mean(err, axis=-1)) if err.ndim else 0.5 * err


def _adamw(w, g, m, v):
    m = ADAM_B1 * m + (1.0 - ADAM_B1) * g
    v = ADAM_B2 * v + (1.0 - ADAM_B2) * _jnp.square(g)
    m_hat = m / (1.0 - ADAM_B1 ** ADAM_STEP)
    v_hat = v / (1.0 - ADAM_B2 ** ADAM_STEP)
    delta = -ADAM_LR * (m_hat / (_jnp.sqrt(v_hat) + ADAM_EPS) + ADAM_WD * w)
    return delta, m, v


def reference(x, c, ctx, c_ctx, w_ada, b_ada, w_in, lru_conv_w, lru_conv_b, lru_w_r, lru_b_r, lru_w_i, lru_b_i, lru_lambda, ssd_conv_w, ssd_conv_b, ssd_dt_bias, ssd_a_log, ssd_d, ssd_norm_w, ml_conv_w, ml_conv_b, ml_gate_b, ml_norm_w, w_br_a, w_br_b, w_br_c, w_out, b_out, ln1_g, ln1_b, w_ff1, b_ff1, w_ff2, b_ff2, ln2_g, ln2_b, loss_target, m_c_ctx, m_w_ada, m_b_ada, m_w_in, m_lru_conv_w, m_lru_conv_b, m_lru_w_r, m_lru_b_r, m_lru_w_i, m_lru_b_i, m_lru_lambda, m_ssd_conv_w, m_ssd_conv_b, m_ssd_dt_bias, m_ssd_a_log, m_ssd_d, m_ssd_norm_w, m_ml_conv_w, m_ml_conv_b, m_ml_gate_b, m_ml_norm_w, m_w_br_a, m_w_br_b, m_w_br_c, m_w_out, m_b_out, m_ln1_g, m_ln1_b, m_w_ff1, m_b_ff1, m_w_ff2, m_b_ff2, m_ln2_g, m_ln2_b, v_c_ctx, v_w_ada, v_b_ada, v_w_in, v_lru_conv_w, v_lru_conv_b, v_lru_w_r, v_lru_b_r, v_lru_w_i, v_lru_b_i, v_lru_lambda, v_ssd_conv_w, v_ssd_conv_b, v_ssd_dt_bias, v_ssd_a_log, v_ssd_d, v_ssd_norm_w, v_ml_conv_w, v_ml_conv_b, v_ml_gate_b, v_ml_norm_w, v_w_br_a, v_w_br_b, v_w_br_c, v_w_out, v_b_out, v_ln1_g, v_ln1_b, v_w_ff1, v_b_ff1, v_w_ff2, v_b_ff2, v_ln2_g, v_ln2_b):
    given = dict(x=x, c=c, ctx=ctx, c_ctx=c_ctx, w_ada=w_ada, b_ada=b_ada, w_in=w_in, lru_conv_w=lru_conv_w, lru_conv_b=lru_conv_b, lru_w_r=lru_w_r, lru_b_r=lru_b_r, lru_w_i=lru_w_i, lru_b_i=lru_b_i, lru_lambda=lru_lambda, ssd_conv_w=ssd_conv_w, ssd_conv_b=ssd_conv_b, ssd_dt_bias=ssd_dt_bias, ssd_a_log=ssd_a_log, ssd_d=ssd_d, ssd_norm_w=ssd_norm_w, ml_conv_w=ml_conv_w, ml_conv_b=ml_conv_b, ml_gate_b=ml_gate_b, ml_norm_w=ml_norm_w, w_br_a=w_br_a, w_br_b=w_br_b, w_br_c=w_br_c, w_out=w_out, b_out=b_out, ln1_g=ln1_g, ln1_b=ln1_b, w_ff1=w_ff1, b_ff1=b_ff1, w_ff2=w_ff2, b_ff2=b_ff2, ln2_g=ln2_g, ln2_b=ln2_b, loss_target=loss_target, m_c_ctx=m_c_ctx, m_w_ada=m_w_ada, m_b_ada=m_b_ada, m_w_in=m_w_in, m_lru_conv_w=m_lru_conv_w, m_lru_conv_b=m_lru_conv_b, m_lru_w_r=m_lru_w_r, m_lru_b_r=m_lru_b_r, m_lru_w_i=m_lru_w_i, m_lru_b_i=m_lru_b_i, m_lru_lambda=m_lru_lambda, m_ssd_conv_w=m_ssd_conv_w, m_ssd_conv_b=m_ssd_conv_b, m_ssd_dt_bias=m_ssd_dt_bias, m_ssd_a_log=m_ssd_a_log, m_ssd_d=m_ssd_d, m_ssd_norm_w=m_ssd_norm_w, m_ml_conv_w=m_ml_conv_w, m_ml_conv_b=m_ml_conv_b, m_ml_gate_b=m_ml_gate_b, m_ml_norm_w=m_ml_norm_w, m_w_br_a=m_w_br_a, m_w_br_b=m_w_br_b, m_w_br_c=m_w_br_c, m_w_out=m_w_out, m_b_out=m_b_out, m_ln1_g=m_ln1_g, m_ln1_b=m_ln1_b, m_w_ff1=m_w_ff1, m_b_ff1=m_b_ff1, m_w_ff2=m_w_ff2, m_b_ff2=m_b_ff2, m_ln2_g=m_ln2_g, m_ln2_b=m_ln2_b, v_c_ctx=v_c_ctx, v_w_ada=v_w_ada, v_b_ada=v_b_ada, v_w_in=v_w_in, v_lru_conv_w=v_lru_conv_w, v_lru_conv_b=v_lru_conv_b, v_lru_w_r=v_lru_w_r, v_lru_b_r=v_lru_b_r, v_lru_w_i=v_lru_w_i, v_lru_b_i=v_lru_b_i, v_lru_lambda=v_lru_lambda, v_ssd_conv_w=v_ssd_conv_w, v_ssd_conv_b=v_ssd_conv_b, v_ssd_dt_bias=v_ssd_dt_bias, v_ssd_a_log=v_ssd_a_log, v_ssd_d=v_ssd_d, v_ssd_norm_w=v_ssd_norm_w, v_ml_conv_w=v_ml_conv_w, v_ml_conv_b=v_ml_conv_b, v_ml_gate_b=v_ml_gate_b, v_ml_norm_w=v_ml_norm_w, v_w_br_a=v_w_br_a, v_w_br_b=v_w_br_b, v_w_br_c=v_w_br_c, v_w_out=v_w_out, v_b_out=v_b_out, v_ln1_g=v_ln1_g, v_ln1_b=v_ln1_b, v_w_ff1=v_w_ff1, v_b_ff1=v_b_ff1, v_w_ff2=v_w_ff2, v_b_ff2=v_b_ff2, v_ln2_g=v_ln2_g, v_ln2_b=v_ln2_b)
    weights = {n: given[n] for n in TWIN_WEIGHTS}
    shared = {n: given[n] for n in SHARED_INPUTS}
    per_example = {n: given[n] for n in ['x', 'c', 'ctx']}
    grad_fn = _jax.value_and_grad(_loss, argnums=(0, 1))

    def one_microbatch(ex, loss_target):
        ex = dict(ex)
        diff = ex.pop(TWIN_DIFF_INPUT)
        return grad_fn(weights, diff, {**shared, **ex}, loss_target)

    if N_MICROBATCH == 1:
        loss, (grad_w, grad_x) = one_microbatch(per_example, given["loss_target"])
    else:
        def body(carry, xs):
            loss_sum, grad_sum = carry
            l_k, (gw_k, gx_k) = one_microbatch(xs[0], xs[1])
            with _jax.named_scope("update"):
                return (loss_sum + l_k, _jax.tree.map(_jnp.add, grad_sum, gw_k)), gx_k

        init = (_jnp.zeros((), _jnp.float32), _jax.tree.map(_jnp.zeros_like, weights))
        (loss, grad_w), grad_x = _jax.lax.scan(body, init, (per_example, given["loss_target"]))
    with _jax.named_scope("update"):
        delta_w, new_m, new_v = {}, {}, {}
        for n in TWIN_WEIGHTS:
            delta_w[n], new_m[n], new_v[n] = _adamw(weights[n], grad_w[n], given["m_" + n], given["v_" + n])
    return (loss, grad_x, *[grad_w[n] for n in TWIN_WEIGHTS], *[delta_w[n] for n in TWIN_WEIGHTS],
            *[new_m[n] for n in TWIN_WEIGHTS], *[new_v[n] for n in TWIN_WEIGHTS])
```

```python
import functools

import numpy as np
import jax
import jax.numpy as jnp
from jax import lax
from jax.experimental import pallas as pl
from jax.experimental.pallas import tpu as pltpu

F32 = jnp.float32
BF16 = jnp.bfloat16

N_DEV = 8
D_MODEL = 1024
DEPTH = 2
GRID_W = 64
CHUNK = 128
LN_EPS = 1e-6
LRU_BLOCKS = 8
LRU_BS = 128
LRU_C = 8.0
SSD_INNER = 2048
SSD_GROUPS = 8
SSD_HPG = 4
SSD_HEADDIM = 64
SSD_STATE = 128
ML_HEADS = 4
ML_HD = 256
D_FF = 4096
DN_ALPHA = (2 * DEPTH) ** 0.25
ADAM_LR, ADAM_B1, ADAM_B2, ADAM_EPS, ADAM_WD, ADAM_STEP = 0.001, 0.9, 0.999, 1e-08, 0.01, 10

VMEM_CAP = 60 * 1024 * 1024
SUBLANES = 8
LANES = 128

_IN_MAIN = ((0, 8192), (8256, 12352), (12368, 15440))
_IN_MAIN_WIDTHS = (1024, 1024, 2048, 2048, 1024, 1024, 1024, 1024, 1024, 1024, 1024, 1024, 1024)
_IN_SMALL = ((8192, 8256), (12352, 12368))
_DT_LANE = 0
_MG_LANE = 64

_WEIGHTS = ['c_ctx', 'w_ada', 'b_ada', 'w_in', 'lru_conv_w', 'lru_conv_b', 'lru_w_r', 'lru_b_r', 'lru_w_i', 'lru_b_i',
            'lru_lambda', 'ssd_conv_w', 'ssd_conv_b', 'ssd_dt_bias', 'ssd_a_log', 'ssd_d', 'ssd_norm_w', 'ml_conv_w',
            'ml_conv_b', 'ml_gate_b', 'ml_norm_w', 'w_br_a', 'w_br_b', 'w_br_c', 'w_out', 'b_out', 'ln1_g', 'ln1_b',
            'w_ff1', 'b_ff1', 'w_ff2', 'b_ff2', 'ln2_g', 'ln2_b']
_BIG = {'w_ada': 2, 'w_in': 2, 'w_ff1': 2, 'w_br_a': 1, 'w_br_b': 1, 'w_br_c': 1, 'w_out': 1, 'w_ff2': 1}
_SMALL_SHARDED = ['lru_conv_w', 'lru_b_r', 'lru_b_i', 'lru_lambda', 'ssd_conv_w', 'ml_conv_w']
_REPLICATED = [n for n in _WEIGHTS if n not in _BIG and n not in _SMALL_SHARDED]


def _params(vmem_bytes):
    return pltpu.CompilerParams(vmem_limit_bytes=int(min(max(2 * vmem_bytes, 32 << 20), VMEM_CAP)))


def _row_tile(n_rows, bytes_per_row, budget=6 << 20, cap=512):
    t = cap
    while t > SUBLANES and (t * bytes_per_row > budget or n_rows % t):
        t //= 2
    assert n_rows % t == 0, (n_rows, t)
    return t


def _dg(a, b, ca, cb):
    return lax.dot_general(a.astype(BF16), b.astype(BF16), (((ca,), (cb,)), ((), ())), preferred_element_type=F32)


def _make_bdot(ca, cb):
    @jax.custom_vjp
    def f(a, b):
        return _dg(a, b, ca, cb)

    def fwd(a, b):
        return _dg(a, b, ca, cb), (a, b)

    def bwd(res, g):
        a, b = res
        da = _dg(g, b, 1, 1 - cb) if ca == 1 else _dg(b, g, 1 - cb, 1)
        db = _dg(a, g, 1 - ca, 0) if cb == 0 else _dg(g, a, 0, 1 - ca)
        return da, db

    f.defvjp(fwd, bwd)
    return f


_mm_nn = _make_bdot(1, 0)
_mm_nt = _make_bdot(1, 1)
_mm_tn = _make_bdot(0, 0)


@jax.custom_vjp
def _round_bf16(x):
    return x.astype(BF16).astype(F32)


_round_bf16.defvjp(lambda x: (_round_bf16(x), None), lambda _, g: (g,))


def _exact_dot(a, b):
    return jnp.dot(a, b, precision=lax.Precision.HIGHEST, preferred_element_type=F32)


def _layernorm_rows(x):
    mu = jnp.mean(x, -1, keepdims=True)
    var = jnp.mean(jnp.square(x - mu), -1, keepdims=True)
    return (x - mu) * lax.rsqrt(var + LN_EPS)


def _rowwise(name, f, rows, params, out_widths):
    rows, params = tuple(rows), tuple(params)
    nr, npar, no = len(rows), len(params), len(out_widths)
    n_rows = rows[0].shape[0]
    row_w = [r.shape[1] for r in rows]
    par_bytes = sum(int(np.prod(p.shape)) * 4 for p in params)
    tile = _row_tile(n_rows, 4 * (2 * sum(row_w) + 2 * sum(out_widths)))
    grid = (n_rows // tile,)

    def row_spec(w):
        return pl.BlockSpec((tile, w), lambda i: (i, 0))

    def par_spec(p):
        return pl.BlockSpec(p.shape, lambda i: (0, 0))

    vmem = 2 * tile * 4 * (2 * sum(row_w) + 3 * sum(out_widths)) + 4 * par_bytes

    def fwd_call(rows, params):
        def kern(*refs):
            outs = f(*[r[...] for r in refs[:nr + npar]])
            for r, o in zip(refs[nr + npar:], outs):
                r[...] = o

        return pl.pallas_call(
            kern, grid=grid, name=name + "_fwd",
            in_specs=[row_spec(w) for w in row_w] + [par_spec(p) for p in params],
            out_specs=[row_spec(w) for w in out_widths],
            out_shape=[jax.ShapeDtypeStruct((n_rows, w), F32) for w in out_widths],
            compiler_params=_params(vmem),
        )(*rows, *params)

    def bwd_call(rows, params, gouts):
        def kern(*refs):
            ins = [r[...] for r in refs[:nr + npar]]
            gs = tuple(r[...] for r in refs[nr + npar:nr + npar + no])
            grads = jax.vjp(f, *ins)[1](gs)
            drefs = refs[nr + npar + no:]
            for k in range(nr):
                drefs[k][...] = grads[k]

            @pl.when(pl.program_id(0) == 0)
            def _():
                for k in range(npar):
                    drefs[nr + k][...] = jnp.zeros_like(drefs[nr + k])

            for k in range(npar):
                drefs[nr + k][...] += grads[nr + k]

        res = pl.pallas_call(
            kern, grid=grid, name=name + "_bwd",
            in_specs=[row_spec(w) for w in row_w] + [par_spec(p) for p in params] + [row_spec(w) for w in out_widths],
            out_specs=[row_spec(w) for w in row_w] + [par_spec(p) for p in params],
            out_shape=[jax.ShapeDtypeStruct(r.shape, F32) for r in rows] + [jax.ShapeDtypeStruct(p.shape, F32) for p in params],
            compiler_params=_params(vmem),
        )(*rows, *params, *gouts)
        return tuple(res[:nr]), tuple(res[nr:])

    @jax.custom_vjp
    def op(rows, params):
        return tuple(fwd_call(rows, params))

    op.defvjp(lambda r, p: (tuple(fwd_call(r, p)), (r, p)), lambda res, g: bwd_call(res[0], res[1], g))
    return op(rows, params)


def _group_ranges(widths, tn):
    starts, s = [], 0
    for w in widths:
        assert w % tn == 0, (w, tn)
        starts.append((s // tn, (s + w) // tn))
        s += w
    return starts, s // tn


def _clamped(spec_shape, s, e, rows_first):
    if rows_first:
        return pl.BlockSpec(spec_shape, lambda i, j: (i, jnp.clip(j - s, 0, e - s - 1)))
    return pl.BlockSpec(spec_shape, lambda j, i: (i, jnp.clip(j - s, 0, e - s - 1)))


def _linear(name, a, w, widths=None):
    single = widths is None
    widths = (w.shape[1],) if single else tuple(widths)
    m, k = a.shape
    n = w.shape[1]
    tn = 128 if n < 256 else (256 if k > 2048 or n % 512 else 512)
    tm = _row_tile(m, 0, cap=512)
    ranges, nt = _group_ranges(widths, tn)
    mt = m // tm
    ng = len(widths)
    vmem = 2 * 4 * (tm * k + k * tn + ng * tm * tn) + 2 * tm * k + 4 * tm * k

    def fwd_call(a, w):
        def kern(a_ref, w_ref, *rest):
            outs, a_bf = rest[:ng], rest[ng]
            j = pl.program_id(1)

            @pl.when(j == 0)
            def _():
                a_bf[...] = a_ref[...].astype(BF16)

            res = jnp.dot(a_bf[...], w_ref[...].astype(BF16), preferred_element_type=F32)
            for o, (s, e) in zip(outs, ranges):
                @pl.when((j >= s) & (j < e))
                def _(o=o):
                    o[...] = res

        return pl.pallas_call(
            kern, grid=(mt, nt), name=name + "_fwd",
            in_specs=[pl.BlockSpec((tm, k), lambda i, j: (i, 0)), pl.BlockSpec((k, tn), lambda i, j: (0, j))],
            out_specs=[_clamped((tm, tn), s, e, True) for (s, e) in ranges],
            out_shape=[jax.ShapeDtypeStruct((m, wd), F32) for wd in widths],
            scratch_shapes=[pltpu.VMEM((tm, k), BF16)],
            compiler_params=_params(vmem),
        )(a, w)

    def dgrad_call(w, gouts):
        def kern(w_ref, *rest):
            gs, da = rest[:ng], rest[ng]
            j = pl.program_id(1)

            @pl.when(j == 0)
            def _():
                da[...] = jnp.zeros_like(da)

            wb = w_ref[...].astype(BF16)
            for g, (s, e) in zip(gs, ranges):
                @pl.when((j >= s) & (j < e))
                def _(g=g):
                    da[...] += lax.dot_general(g[...].astype(BF16), wb, (((1,), (1,)), ((), ())), preferred_element_type=F32)

        return pl.pallas_call(
            kern, grid=(mt, nt), name=name + "_dgrad",
            in_specs=[pl.BlockSpec((k, tn), lambda i, j: (0, j))] + [_clamped((tm, tn), s, e, True) for (s, e) in ranges],
            out_specs=pl.BlockSpec((tm, k), lambda i, j: (i, 0)),
            out_shape=jax.ShapeDtypeStruct((m, k), F32),
            compiler_params=_params(vmem),
        )(w, *gouts)

    def wgrad_call(a, gouts):
        def kern(a_ref, *rest):
            gs, dw = rest[:ng], rest[ng]
            j, i = pl.program_id(0), pl.program_id(1)

            @pl.when(i == 0)
            def _():
                dw[...] = jnp.zeros_like(dw)

            ab = a_ref[...].astype(BF16)
            for g, (s, e) in zip(gs, ranges):
                @pl.when((j >= s) & (j < e))
                def _(g=g):
                    dw[...] += lax.dot_general(ab, g[...].astype(BF16), (((0,), (0,)), ((), ())), preferred_element_type=F32)

        return pl.pallas_call(
            kern, grid=(nt, mt), name=name + "_wgrad",
            in_specs=[pl.BlockSpec((tm, k), lambda j, i: (i, 0))] + [_clamped((tm, tn), s, e, False) for (s, e) in ranges],
            out_specs=pl.BlockSpec((k, tn), lambda j, i: (0, j)),
            out_shape=jax.ShapeDtypeStruct((k, n), F32),
            compiler_params=_params(vmem),
        )(a, *gouts)

    @jax.custom_vjp
    def op(a, w):
        return tuple(fwd_call(a, w))

    op.defvjp(lambda a, w: (tuple(fwd_call(a, w)), (a, w)),
              lambda res, g: (dgrad_call(res[1], g), wgrad_call(res[0], g)))
    out = op(a, w)
    return out[0] if single else out


def _conv_taps(x_ext, w, n_ext):
    xm2 = pltpu.roll(x_ext, 2, 0)
    xm1 = pltpu.roll(x_ext, 1, 0)
    xp1 = pltpu.roll(x_ext, n_ext - 1, 0)
    return xm2, xm1, xp1


def _dwconv(name, x, w, b, act):
    n_rows, ch = x.shape
    tt = _row_tile(n_rows, 4 * 6 * ch, cap=256)
    nt = n_rows // tt
    n_ext = tt + 2 * SUBLANES
    per8 = tt // SUBLANES
    last8 = n_rows // SUBLANES - 1
    main = pl.BlockSpec((tt, ch), lambda i: (i, 0))
    prev = pl.BlockSpec((SUBLANES, ch), lambda i: (jnp.maximum(i * per8 - 1, 0), 0))
    nxt = pl.BlockSpec((SUBLANES, ch), lambda i: (jnp.minimum((i + 1) * per8, last8), 0))
    wspec = pl.BlockSpec((4, ch), lambda i: (0, 0))
    bspec = pl.BlockSpec((1, ch), lambda i: (0, 0))
    vmem = 4 * n_ext * ch * 14

    def ext(main_ref, prev_ref, next_ref):
        i = pl.program_id(0)
        p = jnp.where(i > 0, prev_ref[...], 0.0)
        q = jnp.where(i < nt - 1, next_ref[...], 0.0)
        return jnp.concatenate([p, main_ref[...], q], axis=0)

    def pre_of(x_ext, wv, bv):
        xm2, xm1, xp1 = _conv_taps(x_ext, wv, n_ext)
        pre = wv[0:1] * xm2 + wv[1:2] * xm1 + wv[2:3] * x_ext + wv[3:4] * xp1 + bv
        return pre, (xm2, xm1, xp1)

    def fwd_call(x, w, b):
        def kern(xm, xp, xn, w_ref, b_ref, o_ref):
            pre, _ = pre_of(ext(xm, xp, xn), w_ref[...], b_ref[...])
            pre = pre[SUBLANES:SUBLANES + tt]
            o_ref[...] = pre * jax.nn.sigmoid(pre) if act else pre

        return pl.pallas_call(
            kern, grid=(nt,), name=name + "_fwd", in_specs=[main, prev, nxt, wspec, bspec], out_specs=main,
            out_shape=jax.ShapeDtypeStruct((n_rows, ch), F32), compiler_params=_params(vmem),
        )(x, x, x, w, b)

    def bwd_call(x, w, b, dy):
        def kern(xm, xp, xn, gm, gp, gn, w_ref, b_ref, dx_ref, dw_ref, db_ref):
            wv = w_ref[...]
            x_ext = ext(xm, xp, xn)
            pre, (xm2, xm1, xp1) = pre_of(x_ext, wv, b_ref[...])
            dpre = ext(gm, gp, gn)
            if act:
                sg = jax.nn.sigmoid(pre)
                dpre = dpre * (sg + pre * sg * (1.0 - sg))
            dx = (wv[0:1] * pltpu.roll(dpre, n_ext - 2, 0) + wv[1:2] * pltpu.roll(dpre, n_ext - 1, 0)
                  + wv[2:3] * dpre + wv[3:4] * pltpu.roll(dpre, 1, 0))
            sl = slice(SUBLANES, SUBLANES + tt)
            dx_ref[...] = dx[sl]
            dm = dpre[sl]

            @pl.when(pl.program_id(0) == 0)
            def _():
                dw_ref[...] = jnp.zeros_like(dw_ref)
                db_ref[...] = jnp.zeros_like(db_ref)

            dw_ref[...] += jnp.concatenate(
                [jnp.sum(dm * t[sl], axis=0, keepdims=True) for t in (xm2, xm1, x_ext, xp1)], axis=0)
            db_ref[...] += jnp.sum(dm, axis=0, keepdims=True)

        return pl.pallas_call(
            kern, grid=(nt,), name=name + "_bwd", in_specs=[main, prev, nxt, main, prev, nxt, wspec, bspec],
            out_specs=[main, wspec, bspec],
            out_shape=[jax.ShapeDtypeStruct((n_rows, ch), F32), jax.ShapeDtypeStruct((4, ch), F32),
                       jax.ShapeDtypeStruct((1, ch), F32)],
            compiler_params=_params(vmem),
        )(x, x, x, dy, dy, dy, w, b)

    @jax.custom_vjp
    def op(x, w, b):
        return fwd_call(x, w, b)

    op.defvjp(lambda x, w, b: (fwd_call(x, w, b), (x, w, b)), lambda res, g: tuple(bwd_call(*res, g)))
    return op(x, w, b)


def _scan_groups(tt, ch, reverse, load, store, carry_ref):
    row = lax.broadcasted_iota(jnp.int32, (SUBLANES, ch), 0)
    ng = tt // SUBLANES

    def body(k, carry):
        g = (ng - 1 - k) if reverse else k
        sl = pl.ds(pl.multiple_of(g * SUBLANES, SUBLANES), SUBLANES)
        a, b, extra = load(sl)
        for s in (1, 2, 4):
            if reverse:
                a_sh, b_sh, valid = pltpu.roll(a, SUBLANES - s, 0), pltpu.roll(b, SUBLANES - s, 0), row < SUBLANES - s
            else:
                a_sh, b_sh, valid = pltpu.roll(a, s, 0), pltpu.roll(b, s, 0), row >= s
            b = jnp.where(valid, b + a * b_sh, b)
            a = jnp.where(valid, a * a_sh, a)
        h = b + a * carry
        if reverse:
            h_prev = jnp.where(row == SUBLANES - 1, carry, pltpu.roll(h, SUBLANES - 1, 0))
            last = h[0:1]
        else:
            h_prev = jnp.where(row == 0, carry, pltpu.roll(h, 1, 0))
            last = h[SUBLANES - 1:SUBLANES]
        store(sl, h, h_prev, extra)
        return jnp.broadcast_to(last, (SUBLANES, ch))

    carry_ref[...] = lax.fori_loop(0, ng, body, carry_ref[...])


def _lin_scan(name, a, b, h0, reverse):
    n_rows, ch = a.shape
    tt = _row_tile(n_rows, 0, cap=256)
    nt = n_rows // tt
    vmem = 2 * 4 * tt * ch * 5

    def tile_spec(rev):
        return pl.BlockSpec((tt, ch), (lambda i: (nt - 1 - i, 0)) if rev else (lambda i: (i, 0)))

    vec = pl.BlockSpec((1, ch), lambda i: (0, 0))

    def fwd_call(a, b, h0):
        def kern(a_ref, b_ref, h0_ref, h_ref, hp_ref, last_ref, carry):
            @pl.when(pl.program_id(0) == 0)
            def _():
                carry[...] = jnp.broadcast_to(h0_ref[...], carry.shape)

            def load(sl):
                return a_ref[sl, :], b_ref[sl, :], None

            def store(sl, h, h_prev, _):
                h_ref[sl, :] = h
                hp_ref[sl, :] = h_prev

            _scan_groups(tt, ch, reverse, load, store, carry)
            last_ref[...] = carry[0:1]

        return pl.pallas_call(
            kern, grid=(nt,), name=name + "_fwd", in_specs=[tile_spec(reverse), tile_spec(reverse), vec],
            out_specs=[tile_spec(reverse), tile_spec(reverse), vec],
            out_shape=[jax.ShapeDtypeStruct((n_rows, ch), F32)] * 2 + [jax.ShapeDtypeStruct((1, ch), F32)],
            scratch_shapes=[pltpu.VMEM((SUBLANES, ch), F32)], compiler_params=_params(vmem),
        )(a, b, h0)

    def bwd_call(a, h_prev, dh, dlast):
        rev = not reverse

        def kern(a_ref, hp_ref, dh_ref, dl_ref, da_ref, db_ref, d0_ref, carry):
            @pl.when(pl.program_id(0) == 0)
            def _():
                carry[...] = jnp.broadcast_to(dl_ref[...], carry.shape)

            def load(sl):
                av, dv = a_ref[sl, :], dh_ref[sl, :]
                return av, av * dv, dv

            def store(sl, u, u_next, dv):
                g = dv + u_next
                db_ref[sl, :] = g
                da_ref[sl, :] = g * hp_ref[sl, :]

            _scan_groups(tt, ch, rev, load, store, carry)
            d0_ref[...] = carry[0:1]

        return pl.pallas_call(
            kern, grid=(nt,), name=name + "_bwd", in_specs=[tile_spec(rev)] * 3 + [vec],
            out_specs=[tile_spec(rev), tile_spec(rev), vec],
            out_shape=[jax.ShapeDtypeStruct((n_rows, ch), F32)] * 2 + [jax.ShapeDtypeStruct((1, ch), F32)],
            scratch_shapes=[pltpu.VMEM((SUBLANES, ch), F32)], compiler_params=_params(vmem),
        )(a, h_prev, dh, dlast)

    @jax.custom_vjp
    def op(a, b, h0):
        h, _, last = fwd_call(a, b, h0)
        return h, last

    def op_fwd(a, b, h0):
        h, h_prev, last = fwd_call(a, b, h0)
        return (h, last), (a, h_prev)

    def op_bwd(res, g):
        da, db, d0 = bwd_call(res[0], res[1], g[0], g[1])
        return da, db, d0

    op.defvjp(op_fwd, op_bwd)
    return op(a, b, h0)


def _tri(reverse):
    q = lax.broadcasted_iota(jnp.int32, (CHUNK, CHUNK), 0)
    s = lax.broadcasted_iota(jnp.int32, (CHUNK, CHUNK), 1)
    return (q <= s) if reverse else (q >= s)


def _pick_col(x, lane):
    idx = lax.broadcasted_iota(jnp.int32, x.shape, 1)
    return jnp.sum(jnp.where(idx == lane, x, 0.0), axis=1, keepdims=True)


def _pick_row(x, row):
    idx = lax.broadcasted_iota(jnp.int32, x.shape, 0)
    return jnp.sum(jnp.where(idx == row, x, 0.0), axis=0, keepdims=True)


def _ssd_chunk(xs, bm, cm, small, bias_row, alog_row, state, g, direction, reverse):
    mask = _tri(reverse)
    last = 0 if reverse else CHUNK - 1
    delta_all = jax.nn.softplus(small + bias_row)
    da_all = delta_all * (-jnp.exp(alog_row))
    acs_all = _exact_dot(mask.astype(F32), da_all)
    acs_t = acs_all.T
    cb = _mm_nt(cm, bm)
    rowi = lax.broadcasted_iota(jnp.int32, (CHUNK, 1), 0)
    ys, new_states = [], []
    for r in range(SSD_HPG):
        lane = _DT_LANE + 32 * direction + SSD_HPG * g + r
        delta = _pick_col(delta_all, lane)
        a_col = _pick_col(acs_all, lane)
        a_row = _pick_row(acs_t, lane)
        x_r = xs[:, r * SSD_HEADDIM:(r + 1) * SSD_HEADDIM] * delta
        lmat = jnp.exp(jnp.where(mask, a_col - a_row, -jnp.inf))
        y_diag = _mm_nn(cb * lmat, x_r)
        tot = jnp.sum(jnp.where(rowi == last, a_col, 0.0), axis=0, keepdims=True)
        st = _mm_tn(x_r * jnp.exp(tot - a_col), bm)
        s_r = state[r * SSD_HEADDIM:(r + 1) * SSD_HEADDIM, :]
        y_off = _mm_nt(cm, s_r) * jnp.exp(a_col)
        ys.append(y_diag + y_off)
        new_states.append(jnp.exp(tot) * s_r + st)
    return jnp.concatenate(ys, axis=1), jnp.concatenate(new_states, axis=0)


def _ssd_scan(name, xs, bm, cm, small, bias_row, alog_row, s0, direction, reverse):
    n_rows = xs.shape[0]
    nc = n_rows // CHUNK
    gw = SSD_HPG * SSD_HEADDIM
    vmem = 4 * CHUNK * (gw + 3 * 128) * 8 + 4 * gw * 128 * 12 + (8 << 20)

    def specs(order):
        def cidx(c):
            c = (nc - 1 - c) if order else c
            return c

        return dict(
            xs=pl.BlockSpec((CHUNK, gw), lambda g, c: (cidx(c), g)),
            bc=pl.BlockSpec((CHUNK, SSD_STATE), lambda g, c: (cidx(c), g)),
            small=pl.BlockSpec((CHUNK, LANES), lambda g, c: (cidx(c), 0)),
            row=pl.BlockSpec((1, LANES), lambda g, c: (0, 0)),
            state=pl.BlockSpec((gw, SSD_STATE), lambda g, c: (g, 0)),
            enter=pl.BlockSpec((1, gw, SSD_STATE), lambda g, c: (cidx(c), g, 0)),
            dsmall=pl.BlockSpec((1, CHUNK, LANES), lambda g, c: (g, cidx(c), 0)),
        )

    def fwd_call(xs, bm, cm, small, bias_row, alog_row, s0):
        sp = specs(reverse)

        def kern(xs_r, bm_r, cm_r, sm_r, br_r, ar_r, s0_r, y_r, sf_r, se_r, st):
            @pl.when(pl.program_id(1) == 0)
            def _():
                st[...] = s0_r[...]

            s_in = st[...]
            se_r[0] = s_in
            y, s_new = _ssd_chunk(xs_r[...], bm_r[...], cm_r[...], sm_r[...], br_r[...], ar_r[...], s_in,
                                  pl.program_id(0), direction, reverse)
            y_r[...] = y
            st[...] = s_new
            sf_r[...] = s_new

        return pl.pallas_call(
            kern, grid=(SSD_GROUPS, nc), name=name + "_fwd",
            in_specs=[sp['xs'], sp['bc'], sp['bc'], sp['small'], sp['row'], sp['row'], sp['state']],
            out_specs=[sp['xs'], sp['state'], sp['enter']],
            out_shape=[jax.ShapeDtypeStruct((n_rows, SSD_INNER), F32),
                       jax.ShapeDtypeStruct((SSD_GROUPS * gw, SSD_STATE), F32),
                       jax.ShapeDtypeStruct((nc, SSD_GROUPS * gw, SSD_STATE), F32)],
            scratch_shapes=[pltpu.VMEM((gw, SSD_STATE), F32)], compiler_params=_params(vmem),
        )(xs, bm, cm, small, bias_row, alog_row, s0)

    def bwd_call(xs, bm, cm, small, bias_row, alog_row, enter, dy, dsf):
        sp = specs(not reverse)

        def kern(xs_r, bm_r, cm_r, sm_r, br_r, ar_r, se_r, dy_r, dsf_r, dxs_r, dbm_r, dcm_r, dsm_r, dbr_r, dar_r, ds0_r, ds):
            g = pl.program_id(0)

            @pl.when(pl.program_id(1) == 0)
            def _():
                ds[...] = dsf_r[...]

            @pl.when((g == 0) & (pl.program_id(1) == 0))
            def _():
                dbr_r[...] = jnp.zeros_like(dbr_r)
                dar_r[...] = jnp.zeros_like(dar_r)

            fn = functools.partial(_ssd_chunk, g=g, direction=direction, reverse=reverse)
            _, vjp = jax.vjp(fn, xs_r[...], bm_r[...], cm_r[...], sm_r[...], br_r[...], ar_r[...], se_r[0])
            dxs, dbm, dcm, dsm, dbr, dar, ds_in = vjp((dy_r[...], ds[...]))
            dxs_r[...] = dxs
            dbm_r[...] = dbm
            dcm_r[...] = dcm
            dsm_r[0] = dsm
            dbr_r[...] += dbr
            dar_r[...] += dar
            ds[...] = ds_in
            ds0_r[...] = ds_in

        res = pl.pallas_call(
            kern, grid=(SSD_GROUPS, nc), name=name + "_bwd",
            in_specs=[sp['xs'], sp['bc'], sp['bc'], sp['small'], sp['row'], sp['row'], sp['enter'], sp['xs'], sp['state']],
            out_specs=[sp['xs'], sp['bc'], sp['bc'], sp['dsmall'], sp['row'], sp['row'], sp['state']],
            out_shape=[jax.ShapeDtypeStruct(xs.shape, F32), jax.ShapeDtypeStruct(bm.shape, F32),
                       jax.ShapeDtypeStruct(cm.shape, F32), jax.ShapeDtypeStruct((SSD_GROUPS, n_rows, LANES), F32),
                       jax.ShapeDtypeStruct((1, LANES), F32), jax.ShapeDtypeStruct((1, LANES), F32),
                       jax.ShapeDtypeStruct(s0.shape, F32)],
            scratch_shapes=[pltpu.VMEM((gw, SSD_STATE), F32)], compiler_params=_params(vmem),
        )(xs, bm, cm, small, bias_row, alog_row, enter, dy, dsf)
        dxs, dbm, dcm, dsm, dbr, dar, ds0 = res
        return dxs, dbm, dcm, jnp.sum(dsm, axis=0), dbr, dar, ds0

    @jax.custom_vjp
    def op(*args):
        y, sf, _ = fwd_call(*args)
        return y, sf

    def op_fwd(*args):
        y, sf, enter = fwd_call(*args)
        return (y, sf), (args[:6], enter)

    op.defvjp(op_fwd, lambda res, g: bwd_call(*res[0], res[1], g[0], g[1]))
    return op(xs, bm, cm, small, bias_row, alog_row, s0)


def _ml_chunk(q, k, v, small, gate_row, c_st, n_st, m_st, h, direction, reverse):
    mask = _tri(reverse)
    last = 0 if reverse else CHUNK - 1
    gates = small + gate_row
    lane_i = _MG_LANE + 8 * direction + h
    lane_f = lane_i + ML_HEADS
    lf_all = jax.nn.log_sigmoid(gates)
    b_all = _exact_dot(mask.astype(F32), lf_all)
    b_col = _pick_col(b_all, lane_f)
    b_row = _pick_row(b_all.T, lane_f)
    li_col = _pick_col(gates, lane_i)
    li_row = _pick_row(gates.T, lane_i)
    rowi = lax.broadcasted_iota(jnp.int32, (CHUNK, 1), 0)
    g_tot = jnp.sum(jnp.where(rowi == last, b_col, 0.0), axis=0, keepdims=True)
    m_in = m_st[:, 0:1]
    q = q * (ML_HD ** -0.5)
    w = g_tot - b_col + li_col
    m_loc = jnp.max(w, axis=0, keepdims=True)
    kw = k * jnp.exp(w - m_loc)
    c_loc = _mm_tn(kw, v)
    n_loc = jnp.sum(kw, axis=0, keepdims=True)
    m_new = jnp.maximum(g_tot + m_in, m_loc)
    s_old = jnp.exp(g_tot + m_in - m_new)
    s_loc = jnp.exp(m_loc - m_new)
    c_new = s_old * c_st + s_loc * c_loc
    n_new = s_old * n_st + s_loc * n_loc
    log_d = jnp.where(mask, b_col - b_row + li_row, -jnp.inf)
    inter = b_col + m_in
    m_t = jnp.maximum(inter, jnp.max(log_d, axis=1, keepdims=True))
    dmat = jnp.exp(log_d - m_t)
    wi = jnp.exp(inter - m_t)
    s = _mm_nt(q, k) * dmat
    num = _mm_nn(s, v) + wi * _mm_nn(q, c_st)
    den = jnp.sum(s, axis=1, keepdims=True) + wi * jnp.sum(_round_bf16(q) * _round_bf16(n_st), axis=1, keepdims=True)
    out = num / jnp.maximum(jnp.abs(den), jnp.exp(-m_t))
    return out, c_new, n_new, jnp.broadcast_to(m_new, (1, LANES))


def _ml_scan(name, q, k, v, small, gate_row, c0, n0, m0, direction, reverse):
    n_rows = q.shape[0]
    nc = n_rows // CHUNK
    vmem = 4 * CHUNK * (4 * ML_HD + 128) * 8 + 4 * ML_HD * ML_HD * 12 + (8 << 20)

    def specs(order):
        def cidx(c):
            return (nc - 1 - c) if order else c

        return dict(
            qkv=pl.BlockSpec((CHUNK, ML_HD), lambda h, c: (cidx(c), h)),
            small=pl.BlockSpec((CHUNK, LANES), lambda h, c: (cidx(c), 0)),
            row=pl.BlockSpec((1, LANES), lambda h, c: (0, 0)),
            c=pl.BlockSpec((ML_HD, ML_HD), lambda h, c: (h, 0)),
            n=pl.BlockSpec((1, 1, ML_HD), lambda h, c: (h, 0, 0)),
            m=pl.BlockSpec((1, 1, LANES), lambda h, c: (h, 0, 0)),
            ec=pl.BlockSpec((1, ML_HD, ML_HD), lambda h, c: (cidx(c), h, 0)),
            en=pl.BlockSpec((1, 1, 1, ML_HD), lambda h, c: (cidx(c), h, 0, 0)),
            em=pl.BlockSpec((1, 1, 1, LANES), lambda h, c: (cidx(c), h, 0, 0)),
            dsmall=pl.BlockSpec((1, CHUNK, LANES), lambda h, c: (h, cidx(c), 0)),
        )

    st_shapes = [jax.ShapeDtypeStruct((ML_HEADS * ML_HD, ML_HD), F32), jax.ShapeDtypeStruct((ML_HEADS, 1, ML_HD), F32),
                 jax.ShapeDtypeStruct((ML_HEADS, 1, LANES), F32)]
    scratch = [pltpu.VMEM((ML_HD, ML_HD), F32), pltpu.VMEM((1, ML_HD), F32), pltpu.VMEM((1, LANES), F32)]

    def fwd_call(q, k, v, small, gate_row, c0, n0, m0):
        sp = specs(reverse)

        def kern(q_r, k_r, v_r, sm_r, gr_r, c0_r, n0_r, m0_r, o_r, cf_r, nf_r, mf_r, ec_r, en_r, em_r, cs, ns, ms):
            @pl.when(pl.program_id(1) == 0)
            def _():
                cs[...] = c0_r[...]
                ns[...] = n0_r[0]
                ms[...] = m0_r[0]

            c_in, n_in, m_in = cs[...], ns[...], ms[...]
            ec_r[0] = c_in
            en_r[0, 0] = n_in
            em_r[0, 0] = m_in
            out, c_new, n_new, m_new = _ml_chunk(q_r[...], k_r[...], v_r[...], sm_r[...], gr_r[...], c_in, n_in, m_in,
                                                 pl.program_id(0), direction, reverse)
            o_r[...] = out
            cs[...] = c_new
            ns[...] = n_new
            ms[...] = m_new
            cf_r[...] = c_new
            nf_r[0] = n_new
            mf_r[0] = m_new

        return pl.pallas_call(
            kern, grid=(ML_HEADS, nc), name=name + "_fwd",
            in_specs=[sp['qkv']] * 3 + [sp['small'], sp['row'], sp['c'], sp['n'], sp['m']],
            out_specs=[sp['qkv'], sp['c'], sp['n'], sp['m'], sp['ec'], sp['en'], sp['em']],
            out_shape=[jax.ShapeDtypeStruct((n_rows, ML_HEADS * ML_HD), F32)] + st_shapes + [
                jax.ShapeDtypeStruct((nc, ML_HEADS * ML_HD, ML_HD), F32),
                jax.ShapeDtypeStruct((nc, ML_HEADS, 1, ML_HD), F32), jax.ShapeDtypeStruct((nc, ML_HEADS, 1, LANES), F32)],
            scratch_shapes=scratch, compiler_params=_params(vmem),
        )(q, k, v, small, gate_row, c0, n0, m0)

    def bwd_call(q, k, v, small, gate_row, ec, en, em, do, dcf, dnf, dmf):
        sp = specs(not reverse)

        def kern(q_r, k_r, v_r, sm_r, gr_r, ec_r, en_r, em_r, do_r, dcf_r, dnf_r, dmf_r,
                 dq_r, dk_r, dv_r, dsm_r, dgr_r, dc0_r, dn0_r, dm0_r, dcs, dns, dms):
            h = pl.program_id(0)

            @pl.when(pl.program_id(1) == 0)
            def _():
                dcs[...] = dcf_r[...]
                dns[...] = dnf_r[0]
                dms[...] = dmf_r[0]

            @pl.when((h == 0) & (pl.program_id(1) == 0))
            def _():
                dgr_r[...] = jnp.zeros_like(dgr_r)

            fn = functools.partial(_ml_chunk, h=h, direction=direction, reverse=reverse)
            _, vjp = jax.vjp(fn, q_r[...], k_r[...], v_r[...], sm_r[...], gr_r[...], ec_r[0], en_r[0, 0], em_r[0, 0])
            dq, dk, dv, dsm, dgr, dc, dn, dm = vjp((do_r[...], dcs[...], dns[...], dms[...]))
            dq_r[...] = dq
            dk_r[...] = dk
            dv_r[...] = dv
            dsm_r[0] = dsm
            dgr_r[...] += dgr
            dm = jnp.broadcast_to(jnp.sum(dm, axis=1, keepdims=True), (1, LANES)) * (1.0 / LANES)
            dcs[...] = dc
            dns[...] = dn
            dms[...] = dm
            dc0_r[...] = dc
            dn0_r[0] = dn
            dm0_r[0] = dm

        res = pl.pallas_call(
            kern, grid=(ML_HEADS, nc), name=name + "_bwd",
            in_specs=[sp['qkv']] * 3 + [sp['small'], sp['row'], sp['ec'], sp['en'], sp['em'], sp['qkv'], sp['c'], sp['n'], sp['m']],
            out_specs=[sp['qkv']] * 3 + [sp['dsmall'], sp['row'], sp['c'], sp['n'], sp['m']],
            out_shape=[jax.ShapeDtypeStruct(q.shape, F32)] * 3 + [jax.ShapeDtypeStruct((ML_HEADS, n_rows, LANES), F32),
                                                                  jax.ShapeDtypeStruct((1, LANES), F32)] + st_shapes,
            scratch_shapes=scratch, compiler_params=_params(vmem),
        )(q, k, v, small, gate_row, ec, en, em, do, dcf, dnf, dmf)
        dq, dk, dv, dsm, dgr, dc0, dn0, dm0 = res
        return dq, dk, dv, jnp.sum(dsm, axis=0), dgr, dc0, dn0, dm0

    @jax.custom_vjp
    def op(*args):
        return tuple(fwd_call(*args)[:4])

    def op_fwd(*args):
        res = fwd_call(*args)
        return tuple(res[:4]), (args[:5], tuple(res[4:]))

    op.defvjp(op_fwd, lambda res, g: bwd_call(*res[0], *res[1], *g))
    return op(q, k, v, small, gate_row, c0, n0, m0)


def _f_modulate(x, shift, scale):
    return (_layernorm_rows(x) * (1.0 + scale) + shift,)


def _f_resid_ln(x, o, gate, bias, ln_g, ln_b):
    return (_layernorm_rows(DN_ALPHA * x + gate * (o + bias)) * ln_g + ln_b,)


def _f_lru_gates(xc, w_r, b_r, w_i, b_i, lam):
    outs = []
    for d in range(2):
        def blockdiag(w):
            return jnp.concatenate(
                [_mm_nn(xc[:, n * LRU_BS:(n + 1) * LRU_BS], w[(d * LRU_BLOCKS + n) * LRU_BS:(d * LRU_BLOCKS + n + 1) * LRU_BS, :])
                 for n in range(LRU_BLOCKS)], axis=1)

        r = jax.nn.sigmoid(blockdiag(w_r) + b_r[d:d + 1])
        i = jax.nn.sigmoid(blockdiag(w_i) + b_i[d:d + 1])
        log_a = -LRU_C * r * jax.nn.softplus(-lam[d:d + 1])
        outs += [jnp.exp(log_a), jnp.sqrt(1.0 - jnp.exp(2.0 * log_a)) * i * xc]
    return tuple(outs)


def _f_lru_out(h_f, h_b, ly):
    return ((h_f + h_b) * jax.nn.gelu(ly),)


def _f_ssd_post(y_f, y_b, xs, z, d_exp, norm_w):
    y = (y_f + y_b + xs * d_exp) * jax.nn.silu(z)
    gw = SSD_INNER // SSD_GROUPS
    parts = []
    for g in range(SSD_GROUPS):
        yg = y[:, g * gw:(g + 1) * gw]
        parts.append(yg * lax.rsqrt(jnp.mean(jnp.square(yg), -1, keepdims=True) + LN_EPS))
    return (jnp.concatenate(parts, axis=1) * norm_w,)


def _f_ml_post(h_f, h_b, o, norm_w):
    h = h_f + h_b
    parts = [_layernorm_rows(h[:, i * ML_HD:(i + 1) * ML_HD]) for i in range(ML_HEADS)]
    return (jnp.concatenate(parts, axis=1) * norm_w * jax.nn.sigmoid(o),)


def _f_merge(ga, gb, gc, pa, pb, pc):
    return (jax.nn.sigmoid(ga) * pa + jax.nn.sigmoid(gb) * pb + jax.nn.sigmoid(gc) * pc,)


def _f_relu2(pre, bias):
    return (jnp.square(jax.nn.relu(pre + bias)),)


def _lane_row(vec, start):
    return jnp.zeros((1, LANES), F32).at[0, start:start + vec.shape[0]].set(vec)


def _mixer(tag, h, p, states):
    (lru_s, ssd_s, ml_s) = states
    w_in = p['w_in']
    w_main = jnp.concatenate([w_in[:, s:e] for s, e in _IN_MAIN], axis=1)
    w_small = jnp.concatenate([w_in[:, s:e] for s, e in _IN_SMALL] + [jnp.zeros((D_MODEL, LANES - 80), F32)], axis=1)
    lx, ly, sz, xs, bm, cm, mq, mk, mv, mo, ga, gb, gc = _linear(tag + "in", h, w_main, _IN_MAIN_WIDTHS)
    small = _linear(tag + "insmall", h, w_small)

    xc = _dwconv(tag + "lruconv", lx, p['lru_conv_w'], p['lru_conv_b'][None], False)
    a_f, b_f, a_b, b_b = _rowwise(
        tag + "lrugate", _f_lru_gates, [xc],
        [p['lru_w_r'].reshape(2 * LRU_BLOCKS * LRU_BS, LRU_BS), p['lru_b_r'], p['lru_w_i'].reshape(2 * LRU_BLOCKS * LRU_BS, LRU_BS),
         p['lru_b_i'], p['lru_lambda']], [D_MODEL] * 4)
    h_f, s_f = _lin_scan(tag + "lruscanf", a_f, b_f, lru_s[0], False)
    h_b, s_b = _lin_scan(tag + "lruscanb", a_b, b_b, lru_s[1], True)
    (ya,) = _rowwise(tag + "lruout", _f_lru_out, [h_f, h_b, ly], [], [D_MODEL])

    cw, cb_ = p['ssd_conv_w'], p['ssd_conv_b'][None]
    xs_c = _dwconv(tag + "ssdconvx", xs, cw[:, :2048], cb_[:, :2048], True)
    bm_c = _dwconv(tag + "ssdconvb", bm, cw[:, 2048:3072], cb_[:, 2048:3072], True)
    cm_c = _dwconv(tag + "ssdconvc", cm, cw[:, 3072:], cb_[:, 3072:], True)
    ssd_new, ys = [], []
    for d in range(2):
        y_d, st_d = _ssd_scan(tag + "ssd%d" % d, xs_c, bm_c, cm_c, small, _lane_row(p['ssd_dt_bias'][d], _DT_LANE + 32 * d),
                              _lane_row(p['ssd_a_log'][d], _DT_LANE + 32 * d), ssd_s[d], d, d == 1)
        ys.append(y_d)
        ssd_new.append(st_d)
    (yb,) = _rowwise(tag + "ssdpost", _f_ssd_post, [ys[0], ys[1], xs_c, sz],
                     [jnp.repeat(p['ssd_d'], SSD_HEADDIM)[None], p['ssd_norm_w'][None]], [SSD_INNER])

    mw, mb = p['ml_conv_w'], p['ml_conv_b'][None]
    q_c = _dwconv(tag + "mlconvq", mq, mw[:, :1024], mb[:, :1024], True)
    k_c = _dwconv(tag + "mlconvk", mk, mw[:, 1024:], mb[:, 1024:], True)
    gate_row = _lane_row(p['ml_gate_b'].reshape(4 * ML_HEADS), _MG_LANE)
    ml_new, hs = [], []
    for d in range(2):
        o_d, c_d, n_d, m_d = _ml_scan(tag + "ml%d" % d, q_c, k_c, mv, small, gate_row, *ml_s[d], d, d == 1)
        hs.append(o_d)
        ml_new.append((c_d, n_d, m_d))
    (yc,) = _rowwise(tag + "mlpost", _f_ml_post, [hs[0], hs[1], mo], [p['ml_norm_w'][None]], [D_MODEL])
    return (ya, yb, yc, ga, gb, gc), ((s_f, s_b), tuple(ssd_new), tuple(ml_new))


def _merge(tag, br, p):
    ya, yb, yc, ga, gb, gc = br
    pa = _linear(tag + "bra", ya, p['w_br_a'])
    pb = _linear(tag + "brb", yb, p['w_br_b'])
    pc = _linear(tag + "brc", yc, p['w_br_c'])
    (m,) = _rowwise(tag + "merge", _f_merge, [ga, gb, gc, pa, pb, pc], [], [D_MODEL])
    return _linear(tag + "out", m, p['w_out'])


def _sublayers(tag, xin, o, mods, p, l):
    sh2, sc2, g1, g2 = mods
    (x1,) = _rowwise(tag + "ln1", _f_resid_ln, [xin, o], [g1, p['b_out'][None], p['ln1_g'][None], p['ln1_b'][None]], [D_MODEL])
    (h2,) = _rowwise(tag + "mod2", _f_modulate, [x1], [sh2, sc2], [D_MODEL])
    pre = _linear(tag + "ff1", h2, p['w_ff1'])
    (u2,) = _rowwise(tag + "relu2", _f_relu2, [pre], [p['b_ff1'][None]], [D_FF])
    o2 = _linear(tag + "ff2", u2, p['w_ff2'])
    (x2,) = _rowwise(tag + "ln2", _f_resid_ln, [x1, o2], [g2, p['b_ff2'][None], p['ln2_g'][None], p['ln2_b'][None]], [D_MODEL])
    return x2


def _to_col_major(h):
    s, d = h.shape
    return h.reshape(s // GRID_W, GRID_W, d).swapaxes(0, 1).reshape(s, d)


def _from_col_major(h):
    s, d = h.shape
    return h.reshape(GRID_W, s // GRID_W, d).swapaxes(0, 1).reshape(s, d)


def _forward(x, wts, cvec, ctx):
    zeros = lambda *s: jnp.zeros(s, F32)
    ctx_init = ((zeros(1, D_MODEL), zeros(1, D_MODEL)),
                (zeros(SSD_INNER, SSD_STATE), zeros(SSD_INNER, SSD_STATE)),
                tuple((zeros(ML_HEADS * ML_HD, ML_HD), zeros(ML_HEADS, 1, ML_HD), zeros(ML_HEADS, 1, LANES)) for _ in range(2)))
    for l in range(DEPTH):
        p = {n: (wts[n] if n == 'c_ctx' else wts[n][l]) for n in _WEIGHTS}
        tag = "l%d" % l
        cc = jnp.concatenate([cvec, p['c_ctx'][None], jnp.zeros((SUBLANES - 2, D_MODEL), F32)], axis=0)
        mod = _linear(tag + "ada", jax.nn.silu(cc), p['w_ada']) + p['b_ada'][None]
        sh1x, sc1x, g1x, sh2x, sc2x, g2x = [mod[0:1, i * D_MODEL:(i + 1) * D_MODEL] for i in range(6)]
        sh1c, sc1c, g1c, sh2c, sc2c, g2c = [mod[1:2, i * D_MODEL:(i + 1) * D_MODEL] for i in range(6)]
        (hc,) = _rowwise(tag + "cmod1", _f_modulate, [ctx], [sh1c, sc1c], [D_MODEL])
        br_c, ctx_states = _mixer(tag + "c", hc, p, ctx_init)
        (hx,) = _rowwise(tag + "xmod1", _f_modulate, [x], [sh1x, sc1x], [D_MODEL])
        if l % 2 == 1:
            hx = _to_col_major(hx)
        br_x, _ = _mixer(tag + "x", hx, p, ctx_states)
        ox = _merge(tag + "x", br_x, p)
        if l % 2 == 1:
            ox = _from_col_major(ox)
        x = _sublayers(tag + "x", x, ox, (sh2x, sc2x, g1x, g2x), p, l)
        if l < DEPTH - 1:
            ctx = _sublayers(tag + "c", ctx, _merge(tag + "c", br_c, p), (sh2c, sc2c, g1c, g2c), p, l)
    return x


def _loss_and_cotangent(y, target):
    n_rows, d = y.shape
    tt = _row_tile(n_rows, 0, cap=256)

    def kern(y_ref, t_ref, dy_ref, acc_ref):
        @pl.when(pl.program_id(0) == 0)
        def _():
            acc_ref[...] = jnp.zeros_like(acc_ref)

        err = y_ref[...] - t_ref[...]
        dy_ref[...] = err * (1.0 / d)
        acc_ref[...] += jnp.sum(jnp.square(err))

    spec = pl.BlockSpec((tt, d), lambda i: (i, 0))
    dy, acc = pl.pallas_call(
        kern, grid=(n_rows // tt,), name="loss", in_specs=[spec, spec],
        out_specs=[spec, pl.BlockSpec((SUBLANES, LANES), lambda i: (0, 0))],
        out_shape=[jax.ShapeDtypeStruct((n_rows, d), F32), jax.ShapeDtypeStruct((SUBLANES, LANES), F32)],
    )(y, target)
    return acc[0, 0] * (0.5 / d), dy


def _exchange(name, src, gather):
    slab = src.shape if gather else src.shape[1:]

    def body(src_ref, out_ref, send_sems, recv_sems, local_sem):
        x, y, c = lax.axis_index("x"), lax.axis_index("y"), lax.axis_index("c")
        me = 4 * x + 2 * y + c
        local = pltpu.make_async_copy(src_ref if gather else src_ref.at[me], out_ref.at[me], local_sem)
        local.start()
        copies = []
        for d in range(1, N_DEV):
            px, py, pc = lax.rem(x + (d >> 2), 2), lax.rem(y + ((d >> 1) & 1), 2), lax.rem(c + (d & 1), 2)
            peer = 4 * px + 2 * py + pc
            cp = pltpu.make_async_remote_copy(
                src_ref=src_ref if gather else src_ref.at[peer], dst_ref=out_ref.at[me],
                send_sem=send_sems.at[d - 1], recv_sem=recv_sems.at[d - 1],
                device_id=(px, py, pc), device_id_type=pl.DeviceIdType.MESH)
            cp.start()
            copies.append(cp)
        for cp in copies:
            cp.wait()
        local.wait()

    return pl.pallas_call(
        body, name=name, out_shape=jax.ShapeDtypeStruct((N_DEV,) + tuple(slab), src.dtype),
        in_specs=[pl.BlockSpec(memory_space=pl.ANY)], out_specs=pl.BlockSpec(memory_space=pl.ANY),
        scratch_shapes=[pltpu.SemaphoreType.DMA((N_DEV - 1,)), pltpu.SemaphoreType.DMA((N_DEV - 1,)), pltpu.SemaphoreType.DMA],
    )(src)


def _sum_parts(name, parts):
    n_parts, rows, cols = parts.shape
    tr = _row_tile(rows, 4 * cols * (n_parts + 1) * 2)

    def kern(p_ref, o_ref):
        acc = p_ref[0]
        for k in range(1, n_parts):
            acc = acc + p_ref[k]
        o_ref[...] = acc

    return pl.pallas_call(
        kern, grid=(rows // tr,), name=name, in_specs=[pl.BlockSpec((n_parts, tr, cols), lambda i: (0, i, 0))],
        out_specs=pl.BlockSpec((tr, cols), lambda i: (i, 0)), out_shape=jax.ShapeDtypeStruct((rows, cols), F32),
    )(parts)


def _adamw(name, w, m, v, parts):
    n_parts, rows, cols = parts.shape
    lanes = -(-cols // LANES) * LANES
    tr = _row_tile(rows, 4 * lanes * (n_parts + 7) * 2, budget=16 << 20)
    c1 = np.float32(1.0 - ADAM_B1 ** ADAM_STEP)
    c2 = np.float32(1.0 - ADAM_B2 ** ADAM_STEP)

    def kern(w_ref, m_ref, v_ref, p_ref, g_ref, d_ref, nm_ref, nv_ref):
        g = p_ref[0]
        for k in range(1, n_parts):
            g = g + p_ref[k]
        m_new = ADAM_B1 * m_ref[...] + (1.0 - ADAM_B1) * g
        v_new = ADAM_B2 * v_ref[...] + (1.0 - ADAM_B2) * jnp.square(g)
        g_ref[...] = g
        nm_ref[...] = m_new
        nv_ref[...] = v_new
        d_ref[...] = -ADAM_LR * ((m_new / c1) / (jnp.sqrt(v_new / c2) + ADAM_EPS) + ADAM_WD * w_ref[...])

    spec = pl.BlockSpec((tr, cols), lambda i: (i, 0))
    return pl.pallas_call(
        kern, grid=(rows // tr,), name=name,
        in_specs=[spec, spec, spec, pl.BlockSpec((n_parts, tr, cols), lambda i: (0, i, 0))], out_specs=[spec] * 4,
        out_shape=[jax.ShapeDtypeStruct((rows, cols), F32)] * 4,
        compiler_params=_params(4 * lanes * tr * (n_parts + 7) * 2),
    )(w, m, v, parts)


def _pack(arrays, row_multiple):
    flat = jnp.concatenate([a.reshape(-1) for a in arrays])
    n = flat.shape[0]
    per = row_multiple * LANES
    total = -(-n // per) * per
    return jnp.pad(flat, (0, total - n)).reshape(total // LANES, LANES)


def _unpack(packed, shapes):
    flat = packed.reshape(-1)
    out, off = [], 0
    for s in shapes:
        n = int(np.prod(s))
        out.append(flat[off:off + n].reshape(s))
        off += n
    return out


def _unshard(gathered, axis):
    nd, nl, r, c = gathered.shape
    if axis == 1:
        return gathered.transpose(1, 0, 2, 3).reshape(nl, nd * r, c)
    return gathered.transpose(1, 2, 0, 3).reshape(nl, r, nd * c)


def _reshard(full, axis):
    nl, r, c = full.shape
    if axis == 1:
        return full.reshape(nl, N_DEV, r // N_DEV, c).transpose(1, 0, 2, 3)
    return full.reshape(nl, r, N_DEV, c // N_DEV).transpose(2, 0, 1, 3)


def kernel(x, c, ctx, c_ctx, w_ada, b_ada, w_in, lru_conv_w, lru_conv_b, lru_w_r, lru_b_r, lru_w_i, lru_b_i, lru_lambda, ssd_conv_w, ssd_conv_b, ssd_dt_bias, ssd_a_log, ssd_d, ssd_norm_w, ml_conv_w, ml_conv_b, ml_gate_b, ml_norm_w, w_br_a, w_br_b, w_br_c, w_out, b_out, ln1_g, ln1_b, w_ff1, b_ff1, w_ff2, b_ff2, ln2_g, ln2_b, loss_target, m_c_ctx, m_w_ada, m_b_ada, m_w_in, m_lru_conv_w, m_lru_conv_b, m_lru_w_r, m_lru_b_r, m_lru_w_i, m_lru_b_i, m_lru_lambda, m_ssd_conv_w, m_ssd_conv_b, m_ssd_dt_bias, m_ssd_a_log, m_ssd_d, m_ssd_norm_w, m_ml_conv_w, m_ml_conv_b, m_ml_gate_b, m_ml_norm_w, m_w_br_a, m_w_br_b, m_w_br_c, m_w_out, m_b_out, m_ln1_g, m_ln1_b, m_w_ff1, m_b_ff1, m_w_ff2, m_b_ff2, m_ln2_g, m_ln2_b, v_c_ctx, v_w_ada, v_b_ada, v_w_in, v_lru_conv_w, v_lru_conv_b, v_lru_w_r, v_lru_b_r, v_lru_w_i, v_lru_b_i, v_lru_lambda, v_ssd_conv_w, v_ssd_conv_b, v_ssd_dt_bias, v_ssd_a_log, v_ssd_d, v_ssd_norm_w, v_ml_conv_w, v_ml_conv_b, v_ml_gate_b, v_ml_norm_w, v_w_br_a, v_w_br_b, v_w_br_c, v_w_out, v_b_out, v_ln1_g, v_ln1_b, v_w_ff1, v_b_ff1, v_w_ff2, v_b_ff2, v_ln2_g, v_ln2_b):
    a = dict(locals())
    me = 4 * lax.axis_index("x") + 2 * lax.axis_index("y") + lax.axis_index("c")

    wts = {n: a[n] for n in _REPLICATED}
    for n, axis in _BIG.items():
        wts[n] = _unshard(_exchange("gather_" + n, a[n].astype(BF16), True), axis).astype(F32)
    small_shapes = [a[n].shape for n in _SMALL_SHARDED]
    small_all = _exchange("gather_small", _pack([a[n] for n in _SMALL_SHARDED], SUBLANES), True)
    per_dev = [_unpack(small_all[k], small_shapes) for k in range(N_DEV)]
    for i, n in enumerate(_SMALL_SHARDED):
        wts[n] = jnp.concatenate([per_dev[k][i] for k in range(N_DEV)], axis=-1)

    y, vjp = jax.vjp(functools.partial(_forward, cvec=c, ctx=ctx[0]), x[0], wts)
    loss_local, dy = _loss_and_cotangent(y, loss_target[0])
    grad_x, grads = vjp(dy)
    loss = lax.psum(loss_local, ("x", "y", "c"))

    out = {}

    def put(n, res, shape):
        for kind, r in zip(("grad_", "delta_", "new_m_", "new_v_"), res):
            out[kind + n] = r.reshape(shape)

    for n, axis in _BIG.items():
        shp = a[n].shape
        rows, cols = shp[0] * shp[1], shp[2]
        parts = _exchange("scatter_" + n, _reshard(grads[n], axis), False).reshape(N_DEV, rows, cols)
        put(n, _adamw("adamw_" + n, a[n].reshape(rows, cols), a["m_" + n].reshape(rows, cols), a["v_" + n].reshape(rows, cols), parts), shp)

    rep_names = _REPLICATED + _SMALL_SHARDED
    chunk_rows = SUBLANES * N_DEV
    g_pack = _pack([grads[n] for n in rep_names], chunk_rows * N_DEV)
    rows = g_pack.shape[0]
    parts = _exchange("scatter_rep", g_pack.reshape(N_DEV, rows // N_DEV, LANES), False)
    mine = _sum_parts("sum_rep", parts)
    g_all = _exchange("gather_rep", mine, True).reshape(rows, LANES)
    g_full = _unpack(g_all, [grads[n].shape for n in rep_names])
    g_local = []
    for n, g in zip(rep_names, g_full):
        if n in _SMALL_SHARDED:
            width = a[n].shape[-1]
            g = lax.dynamic_slice_in_dim(g, me * width, width, axis=g.ndim - 1)
        g_local.append(g)
    shapes = [a[n].shape for n in rep_names]
    res = _adamw("adamw_rep", _pack([a[n] for n in rep_names], chunk_rows), _pack([a["m_" + n] for n in rep_names], chunk_rows),
                 _pack([a["v_" + n] for n in rep_names], chunk_rows), _pack(g_local, chunk_rows)[None])
    unpacked = [_unpack(r, shapes) for r in res]
    for i, n in enumerate(rep_names):
        put(n, [u[i] for u in unpacked], shapes[i])

    outs = [loss, grad_x[None]]
    for kind in ("grad_", "delta_", "new_m_", "new_v_"):
        outs += [out[kind + n] for n in _WEIGHTS]
    return tuple(outs)
```

```python
import functools

import numpy as np
import jax
import jax.numpy as jnp
from jax import lax
from jax.experimental import pallas as pl
from jax.experimental.pallas import tpu as pltpu

F32 = jnp.float32
BF16 = jnp.bfloat16

N_DEV = 8
D_MODEL = 1024
DEPTH = 2
GRID_W = 64
CHUNK = 128
LN_EPS = 1e-6
LRU_BLOCKS = 8
LRU_BS = 128
LRU_C = 8.0
SSD_INNER = 2048
SSD_GROUPS = 8
SSD_HPG = 4
SSD_HEADDIM = 64
SSD_STATE = 128
ML_HEADS = 4
ML_HD = 256
D_FF = 4096
DN_ALPHA = (2 * DEPTH) ** 0.25
ADAM_LR, ADAM_B1, ADAM_B2, ADAM_EPS, ADAM_WD, ADAM_STEP = 0.001, 0.9, 0.999, 1e-08, 0.01, 10

VMEM_CAP = 60 * 1024 * 1024
SUBLANES = 8
LANES = 128

_IN_MAIN = ((0, 8192), (8256, 12352), (12368, 15440))
_IN_MAIN_WIDTHS = (1024, 1024, 2048, 2048, 1024, 1024, 1024, 1024, 1024, 1024, 1024, 1024, 1024)
_IN_SMALL = ((8192, 8256), (12352, 12368))
_DT_LANE = 0
_MG_LANE = 64

_WEIGHTS = ['c_ctx', 'w_ada', 'b_ada', 'w_in', 'lru_conv_w', 'lru_conv_b', 'lru_w_r', 'lru_b_r', 'lru_w_i', 'lru_b_i',
            'lru_lambda', 'ssd_conv_w', 'ssd_conv_b', 'ssd_dt_bias', 'ssd_a_log', 'ssd_d', 'ssd_norm_w', 'ml_conv_w',
            'ml_conv_b', 'ml_gate_b', 'ml_norm_w', 'w_br_a', 'w_br_b', 'w_br_c', 'w_out', 'b_out', 'ln1_g', 'ln1_b',
            'w_ff1', 'b_ff1', 'w_ff2', 'b_ff2', 'ln2_g', 'ln2_b']
_BIG = {'w_ada': 2, 'w_in': 2, 'w_ff1': 2, 'w_br_a': 1, 'w_br_b': 1, 'w_br_c': 1, 'w_out': 1, 'w_ff2': 1}
_SMALL_SHARDED = ['lru_conv_w', 'lru_b_r', 'lru_b_i', 'lru_lambda', 'ssd_conv_w', 'ml_conv_w']
_REPLICATED = [n for n in _WEIGHTS if n not in _BIG and n not in _SMALL_SHARDED]


def _params(vmem_bytes):
    return pltpu.CompilerParams(vmem_limit_bytes=int(min(max(2 * vmem_bytes, 32 << 20), VMEM_CAP)))


def _row_tile(n_rows, bytes_per_row, budget=6 << 20, cap=512):
    t = cap
    while t > SUBLANES and (t * bytes_per_row > budget or n_rows % t):
        t //= 2
    assert n_rows % t == 0, (n_rows, t)
    return t


def _dg(a, b, ca, cb):
    return lax.dot_general(a.astype(BF16), b.astype(BF16), (((ca,), (cb,)), ((), ())), preferred_element_type=F32)


def _make_bdot(ca, cb):
    @jax.custom_vjp
    def f(a, b):
        return _dg(a, b, ca, cb)

    def fwd(a, b):
        return _dg(a, b, ca, cb), (a, b)

    def bwd(res, g):
        a, b = res
        da = _dg(g, b, 1, 1 - cb) if ca == 1 else _dg(b, g, 1 - cb, 1)
        db = _dg(a, g, 1 - ca, 0) if cb == 0 else _dg(g, a, 0, 1 - ca)
        return da, db

    f.defvjp(fwd, bwd)
    return f


_mm_nn = _make_bdot(1, 0)
_mm_nt = _make_bdot(1, 1)
_mm_tn = _make_bdot(0, 0)


@jax.custom_vjp
def _round_bf16(x):
    return x.astype(BF16).astype(F32)


_round_bf16.defvjp(lambda x: (_round_bf16(x), None), lambda _, g: (g,))


def _exact_dot(a, b):
    return jnp.dot(a, b, precision=lax.Precision.HIGHEST, preferred_element_type=F32)


def _layernorm_rows(x):
    mu = jnp.mean(x, -1, keepdims=True)
    var = jnp.mean(jnp.square(x - mu), -1, keepdims=True)
    return (x - mu) * lax.rsqrt(var + LN_EPS)


def _rowwise(name, f, rows, params, out_widths):
    rows, params = tuple(rows), tuple(params)
    nr, npar, no = len(rows), len(params), len(out_widths)
    n_rows = rows[0].shape[0]
    row_w = [r.shape[1] for r in rows]
    par_bytes = sum(int(np.prod(p.shape)) * 4 for p in params)
    tile = _row_tile(n_rows, 4 * (2 * sum(row_w) + 2 * sum(out_widths)))
    grid = (n_rows // tile,)

    def row_spec(w):
        return pl.BlockSpec((tile, w), lambda i: (i, 0))

    def par_spec(p):
        return pl.BlockSpec(p.shape, lambda i: (0, 0))

    vmem = 2 * tile * 4 * (2 * sum(row_w) + 3 * sum(out_widths)) + 4 * par_bytes

    def fwd_call(rows, params):
        def kern(*refs):
            outs = f(*[r[...] for r in refs[:nr + npar]])
            for r, o in zip(refs[nr + npar:], outs):
                r[...] = o

        return pl.pallas_call(
            kern, grid=grid, name=name + "_fwd",
            in_specs=[row_spec(w) for w in row_w] + [par_spec(p) for p in params],
            out_specs=[row_spec(w) for w in out_widths],
            out_shape=[jax.ShapeDtypeStruct((n_rows, w), F32) for w in out_widths],
            compiler_params=_params(vmem),
        )(*rows, *params)

    def bwd_call(rows, params, gouts):
        def kern(*refs):
            ins = [r[...] for r in refs[:nr + npar]]
            gs = tuple(r[...] for r in refs[nr + npar:nr + npar + no])
            grads = jax.vjp(f, *ins)[1](gs)
            drefs = refs[nr + npar + no:]
            for k in range(nr):
                drefs[k][...] = grads[k]

            @pl.when(pl.program_id(0) == 0)
            def _():
                for k in range(npar):
                    drefs[nr + k][...] = jnp.zeros_like(drefs[nr + k])

            for k in range(npar):
                drefs[nr + k][...] += grads[nr + k]

        res = pl.pallas_call(
            kern, grid=grid, name=name + "_bwd",
            in_specs=[row_spec(w) for w in row_w] + [par_spec(p) for p in params] + [row_spec(w) for w in out_widths],
            out_specs=[row_spec(w) for w in row_w] + [par_spec(p) for p in params],
            out_shape=[jax.ShapeDtypeStruct(r.shape, F32) for r in rows] + [jax.ShapeDtypeStruct(p.shape, F32) for p in params],
            compiler_params=_params(vmem),
        )(*rows, *params, *gouts)
        return tuple(res[:nr]), tuple(res[nr:])

    @jax.custom_vjp
    def op(rows, params):
        return tuple(fwd_call(rows, params))

    op.defvjp(lambda r, p: (tuple(fwd_call(r, p)), (r, p)), lambda res, g: bwd_call(res[0], res[1], g))
    return op(rows, params)


def _group_ranges(widths, tn):
    starts, s = [], 0
    for w in widths:
        assert w % tn == 0, (w, tn)
        starts.append((s // tn, (s + w) // tn))
        s += w
    return starts, s // tn


def _clamped(spec_shape, s, e, rows_first):
    if rows_first:
        return pl.BlockSpec(spec_shape, lambda i, j: (i, jnp.clip(j - s, 0, e - s - 1)))
    return pl.BlockSpec(spec_shape, lambda j, i: (jnp.where((j >= s) & (j < e), i, 0), jnp.clip(j - s, 0, e - s - 1)))


def _linear(name, a, w, widths=None):
    single = widths is None
    widths = (w.shape[1],) if single else tuple(widths)
    m, k = a.shape
    n = w.shape[1]
    tn = 128 if n < 256 else (256 if k > 2048 or n % 512 else 512)
    tm = _row_tile(m, 0, cap=512)
    ranges, nt = _group_ranges(widths, tn)
    mt = m // tm
    ng = len(widths)
    vmem = 2 * 4 * (tm * k + k * tn + ng * tm * tn) + 2 * tm * k + 4 * tm * k

    def fwd_call(a, w):
        def kern(a_ref, w_ref, *rest):
            outs, a_bf = rest[:ng], rest[ng]
            j = pl.program_id(1)

            @pl.when(j == 0)
            def _():
                a_bf[...] = a_ref[...].astype(BF16)

            res = jnp.dot(a_bf[...], w_ref[...], preferred_element_type=F32)
            for o, (s, e) in zip(outs, ranges):
                @pl.when((j >= s) & (j < e))
                def _(o=o):
                    o[...] = res

        return pl.pallas_call(
            kern, grid=(mt, nt), name=name + "_fwd",
            in_specs=[pl.BlockSpec((tm, k), lambda i, j: (i, 0)), pl.BlockSpec((k, tn), lambda i, j: (0, j))],
            out_specs=[_clamped((tm, tn), s, e, True) for (s, e) in ranges],
            out_shape=[jax.ShapeDtypeStruct((m, wd), F32) for wd in widths],
            scratch_shapes=[pltpu.VMEM((tm, k), BF16)],
            compiler_params=_params(vmem),
        )(a, w.astype(BF16))

    def dgrad_call(w, gouts):
        def kern(w_ref, *rest):
            gs, da = rest[:ng], rest[ng]
            j = pl.program_id(1)

            @pl.when(j == 0)
            def _():
                da[...] = jnp.zeros_like(da)

            for g, (s, e) in zip(gs, ranges):
                @pl.when((j >= s) & (j < e))
                def _(g=g):
                    da[...] += lax.dot_general(g[...].astype(BF16), w_ref[...], (((1,), (1,)), ((), ())), preferred_element_type=F32)

        return pl.pallas_call(
            kern, grid=(mt, nt), name=name + "_dgrad",
            in_specs=[pl.BlockSpec((k, tn), lambda i, j: (0, j))] + [_clamped((tm, tn), s, e, True) for (s, e) in ranges],
            out_specs=pl.BlockSpec((tm, k), lambda i, j: (i, 0)),
            out_shape=jax.ShapeDtypeStruct((m, k), F32),
            compiler_params=_params(vmem),
        )(w.astype(BF16), *gouts)

    def wgrad_call(a, gouts):
        def kern(a_ref, *rest):
            gs, dw = rest[:ng], rest[ng]
            j, i = pl.program_id(0), pl.program_id(1)

            @pl.when(i == 0)
            def _():
                dw[...] = jnp.zeros_like(dw)

            ab = a_ref[...].astype(BF16)
            for g, (s, e) in zip(gs, ranges):
                @pl.when((j >= s) & (j < e))
                def _(g=g):
                    dw[...] += lax.dot_general(ab, g[...].astype(BF16), (((0,), (0,)), ((), ())), preferred_element_type=F32)

        return pl.pallas_call(
            kern, grid=(nt, mt), name=name + "_wgrad",
            in_specs=[pl.BlockSpec((tm, k), lambda j, i: (i, 0))] + [_clamped((tm, tn), s, e, False) for (s, e) in ranges],
            out_specs=pl.BlockSpec((k, tn), lambda j, i: (0, j)),
            out_shape=jax.ShapeDtypeStruct((k, n), F32),
            compiler_params=_params(vmem),
        )(a, *gouts)

    @jax.custom_vjp
    def op(a, w):
        return tuple(fwd_call(a, w))

    op.defvjp(lambda a, w: (tuple(fwd_call(a, w)), (a, w)),
              lambda res, g: (dgrad_call(res[1], g), wgrad_call(res[0], g)))
    out = op(a, w)
    return out[0] if single else out


def _conv_taps(x_ext, w, n_ext):
    xm2 = pltpu.roll(x_ext, 2, 0)
    xm1 = pltpu.roll(x_ext, 1, 0)
    xp1 = pltpu.roll(x_ext, n_ext - 1, 0)
    return xm2, xm1, xp1


def _dwconv(name, x, w, b, act):
    n_rows, ch = x.shape
    tt = _row_tile(n_rows, 4 * 6 * ch, cap=256)
    nt = n_rows // tt
    n_ext = tt + 2 * SUBLANES
    per8 = tt // SUBLANES
    last8 = n_rows // SUBLANES - 1
    main = pl.BlockSpec((tt, ch), lambda i: (i, 0))
    prev = pl.BlockSpec((SUBLANES, ch), lambda i: (jnp.maximum(i * per8 - 1, 0), 0))
    nxt = pl.BlockSpec((SUBLANES, ch), lambda i: (jnp.minimum((i + 1) * per8, last8), 0))
    wspec = pl.BlockSpec((4, ch), lambda i: (0, 0))
    bspec = pl.BlockSpec((1, ch), lambda i: (0, 0))
    vmem = 4 * n_ext * ch * 14

    def ext(main_ref, prev_ref, next_ref):
        i = pl.program_id(0)
        p = jnp.where(i > 0, prev_ref[...], 0.0)
        q = jnp.where(i < nt - 1, next_ref[...], 0.0)
        return jnp.concatenate([p, main_ref[...], q], axis=0)

    def pre_of(x_ext, wv, bv):
        xm2, xm1, xp1 = _conv_taps(x_ext, wv, n_ext)
        pre = wv[0:1] * xm2 + wv[1:2] * xm1 + wv[2:3] * x_ext + wv[3:4] * xp1 + bv
        return pre, (xm2, xm1, xp1)

    def fwd_call(x, w, b):
        def kern(xm, xp, xn, w_ref, b_ref, o_ref):
            pre, _ = pre_of(ext(xm, xp, xn), w_ref[...], b_ref[...])
            pre = pre[SUBLANES:SUBLANES + tt]
            o_ref[...] = pre * jax.nn.sigmoid(pre) if act else pre

        return pl.pallas_call(
            kern, grid=(nt,), name=name + "_fwd", in_specs=[main, prev, nxt, wspec, bspec], out_specs=main,
            out_shape=jax.ShapeDtypeStruct((n_rows, ch), F32), compiler_params=_params(vmem),
        )(x, x, x, w, b)

    def bwd_call(x, w, b, dy):
        def kern(xm, xp, xn, gm, gp, gn, w_ref, b_ref, dx_ref, dw_ref, db_ref):
            wv = w_ref[...]
            x_ext = ext(xm, xp, xn)
            pre, (xm2, xm1, xp1) = pre_of(x_ext, wv, b_ref[...])
            dpre = ext(gm, gp, gn)
            if act:
                sg = jax.nn.sigmoid(pre)
                dpre = dpre * (sg + pre * sg * (1.0 - sg))
            dx = (wv[0:1] * pltpu.roll(dpre, n_ext - 2, 0) + wv[1:2] * pltpu.roll(dpre, n_ext - 1, 0)
                  + wv[2:3] * dpre + wv[3:4] * pltpu.roll(dpre, 1, 0))
            sl = slice(SUBLANES, SUBLANES + tt)
            dx_ref[...] = dx[sl]
            dm = dpre[sl]

            @pl.when(pl.program_id(0) == 0)
            def _():
                dw_ref[...] = jnp.zeros_like(dw_ref)
                db_ref[...] = jnp.zeros_like(db_ref)

            dw_ref[...] += jnp.concatenate(
                [jnp.sum(dm * t[sl], axis=0, keepdims=True) for t in (xm2, xm1, x_ext, xp1)], axis=0)
            db_ref[...] += jnp.sum(dm, axis=0, keepdims=True)

        return pl.pallas_call(
            kern, grid=(nt,), name=name + "_bwd", in_specs=[main, prev, nxt, main, prev, nxt, wspec, bspec],
            out_specs=[main, wspec, bspec],
            out_shape=[jax.ShapeDtypeStruct((n_rows, ch), F32), jax.ShapeDtypeStruct((4, ch), F32),
                       jax.ShapeDtypeStruct((1, ch), F32)],
            compiler_params=_params(vmem),
        )(x, x, x, dy, dy, dy, w, b)

    @jax.custom_vjp
    def op(x, w, b):
        return fwd_call(x, w, b)

    op.defvjp(lambda x, w, b: (fwd_call(x, w, b), (x, w, b)), lambda res, g: tuple(bwd_call(*res, g)))
    return op(x, w, b)


def _scan_groups(tt, ch, reverse, load, store, carry_ref):
    row = lax.broadcasted_iota(jnp.int32, (SUBLANES, ch), 0)
    ng = tt // SUBLANES

    def body(k, carry):
        g = (ng - 1 - k) if reverse else k
        sl = pl.ds(pl.multiple_of(g * SUBLANES, SUBLANES), SUBLANES)
        a, b, extra = load(sl)
        for s in (1, 2, 4):
            if reverse:
                a_sh, b_sh, valid = pltpu.roll(a, SUBLANES - s, 0), pltpu.roll(b, SUBLANES - s, 0), row < SUBLANES - s
            else:
                a_sh, b_sh, valid = pltpu.roll(a, s, 0), pltpu.roll(b, s, 0), row >= s
            b = jnp.where(valid, b + a * b_sh, b)
            a = jnp.where(valid, a * a_sh, a)
        h = b + a * carry
        if reverse:
            h_prev = jnp.where(row == SUBLANES - 1, carry, pltpu.roll(h, SUBLANES - 1, 0))
            last = h[0:1]
        else:
            h_prev = jnp.where(row == 0, carry, pltpu.roll(h, 1, 0))
            last = h[SUBLANES - 1:SUBLANES]
        store(sl, h, h_prev, extra)
        return jnp.broadcast_to(last, (SUBLANES, ch))

    carry_ref[...] = lax.fori_loop(0, ng, body, carry_ref[...])


def _lin_scan(name, a, b, h0, reverse):
    n_rows, ch = a.shape
    tt = _row_tile(n_rows, 0, cap=256)
    nt = n_rows // tt
    vmem = 2 * 4 * tt * ch * 5

    def tile_spec(rev):
        return pl.BlockSpec((tt, ch), (lambda i: (nt - 1 - i, 0)) if rev else (lambda i: (i, 0)))

    vec = pl.BlockSpec((1, ch), lambda i: (0, 0))

    def fwd_call(a, b, h0):
        def kern(a_ref, b_ref, h0_ref, h_ref, hp_ref, last_ref, carry):
            @pl.when(pl.program_id(0) == 0)
            def _():
                carry[...] = jnp.broadcast_to(h0_ref[...], carry.shape)

            def load(sl):
                return a_ref[sl, :], b_ref[sl, :], None

            def store(sl, h, h_prev, _):
                h_ref[sl, :] = h
                hp_ref[sl, :] = h_prev

            _scan_groups(tt, ch, reverse, load, store, carry)
            last_ref[...] = carry[0:1]

        return pl.pallas_call(
            kern, grid=(nt,), name=name + "_fwd", in_specs=[tile_spec(reverse), tile_spec(reverse), vec],
            out_specs=[tile_spec(reverse), tile_spec(reverse), vec],
            out_shape=[jax.ShapeDtypeStruct((n_rows, ch), F32)] * 2 + [jax.ShapeDtypeStruct((1, ch), F32)],
            scratch_shapes=[pltpu.VMEM((SUBLANES, ch), F32)], compiler_params=_params(vmem),
        )(a, b, h0)

    def bwd_call(a, h_prev, dh, dlast):
        rev = not reverse

        def kern(a_ref, hp_ref, dh_ref, dl_ref, da_ref, db_ref, d0_ref, carry):
            @pl.when(pl.program_id(0) == 0)
            def _():
                carry[...] = jnp.broadcast_to(dl_ref[...], carry.shape)

            def load(sl):
                av, dv = a_ref[sl, :], dh_ref[sl, :]
                return av, av * dv, dv

            def store(sl, u, u_next, dv):
                g = dv + u_next
                db_ref[sl, :] = g
                da_ref[sl, :] = g * hp_ref[sl, :]

            _scan_groups(tt, ch, rev, load, store, carry)
            d0_ref[...] = carry[0:1]

        return pl.pallas_call(
            kern, grid=(nt,), name=name + "_bwd", in_specs=[tile_spec(rev)] * 3 + [vec],
            out_specs=[tile_spec(rev), tile_spec(rev), vec],
            out_shape=[jax.ShapeDtypeStruct((n_rows, ch), F32)] * 2 + [jax.ShapeDtypeStruct((1, ch), F32)],
            scratch_shapes=[pltpu.VMEM((SUBLANES, ch), F32)], compiler_params=_params(vmem),
        )(a, h_prev, dh, dlast)

    @jax.custom_vjp
    def op(a, b, h0):
        h, _, last = fwd_call(a, b, h0)
        return h, last

    def op_fwd(a, b, h0):
        h, h_prev, last = fwd_call(a, b, h0)
        return (h, last), (a, h_prev)

    def op_bwd(res, g):
        da, db, d0 = bwd_call(res[0], res[1], g[0], g[1])
        return da, db, d0

    op.defvjp(op_fwd, op_bwd)
    return op(a, b, h0)


def _tri(reverse):
    q = lax.broadcasted_iota(jnp.int32, (CHUNK, CHUNK), 0)
    s = lax.broadcasted_iota(jnp.int32, (CHUNK, CHUNK), 1)
    return (q <= s) if reverse else (q >= s)


def _pick_col(x, lane):
    idx = lax.broadcasted_iota(jnp.int32, x.shape, 1)
    return jnp.sum(jnp.where(idx == lane, x, 0.0), axis=1, keepdims=True)


def _pick_row(x, row):
    idx = lax.broadcasted_iota(jnp.int32, x.shape, 0)
    return jnp.sum(jnp.where(idx == row, x, 0.0), axis=0, keepdims=True)


def _ssd_shared(small, bias_row, alog_row, reverse):
    delta_all = jax.nn.softplus(small + bias_row)
    acs_all = _exact_dot(_tri(reverse).astype(F32), delta_all * (-jnp.exp(alog_row)))
    return delta_all, acs_all, acs_all.T


def _ssd_group(xs, bm, cm, state, delta_all, acs_all, acs_t, g, direction, reverse):
    mask = _tri(reverse)
    last = 0 if reverse else CHUNK - 1
    cb = _mm_nt(cm, bm)
    rowi = lax.broadcasted_iota(jnp.int32, (CHUNK, 1), 0)
    ys, new_states = [], []
    for r in range(SSD_HPG):
        lane = _DT_LANE + 32 * direction + SSD_HPG * g + r
        delta = _pick_col(delta_all, lane)
        a_col = _pick_col(acs_all, lane)
        a_row = _pick_row(acs_t, lane)
        x_r = xs[:, r * SSD_HEADDIM:(r + 1) * SSD_HEADDIM] * delta
        lmat = jnp.exp(jnp.where(mask, a_col - a_row, -jnp.inf))
        y_diag = _mm_nn(cb * lmat, x_r)
        tot = jnp.sum(jnp.where(rowi == last, a_col, 0.0), axis=0, keepdims=True)
        st = _mm_tn(x_r * jnp.exp(tot - a_col), bm)
        s_r = state[r * SSD_HEADDIM:(r + 1) * SSD_HEADDIM, :]
        y_off = _mm_nt(cm, s_r) * jnp.exp(a_col)
        ys.append(y_diag + y_off)
        new_states.append(jnp.exp(tot) * s_r + st)
    return jnp.concatenate(ys, axis=1), jnp.concatenate(new_states, axis=0)


def _ssd_scan(name, xs, bm, cm, small, bias_row, alog_row, s0, direction, reverse):
    n_rows = xs.shape[0]
    nc = n_rows // CHUNK
    gw = SSD_HPG * SSD_HEADDIM
    vmem = 4 * CHUNK * (gw + 3 * 128) * 8 + 4 * gw * 128 * 12 + (8 << 20)

    n_state = SSD_GROUPS * gw
    shared_scratch = [pltpu.VMEM((CHUNK, LANES), F32), pltpu.VMEM((CHUNK, LANES), F32), pltpu.VMEM((LANES, CHUNK), F32)]

    def specs(order):
        def cidx(c):
            return (nc - 1 - c) if order else c

        return dict(
            xs=pl.BlockSpec((CHUNK, gw), lambda c, g: (cidx(c), g)),
            bc=pl.BlockSpec((CHUNK, SSD_STATE), lambda c, g: (cidx(c), g)),
            small=pl.BlockSpec((CHUNK, LANES), lambda c, g: (cidx(c), 0)),
            row=pl.BlockSpec((1, LANES), lambda c, g: (0, 0)),
            state=pl.BlockSpec((n_state, SSD_STATE), lambda c, g: (0, 0)),
            enter=pl.BlockSpec((1, gw, SSD_STATE), lambda c, g: (cidx(c), g, 0)),
        )

    def group_rows(g):
        return pl.ds(pl.multiple_of(g * gw, gw), gw)

    def fwd_call(xs, bm, cm, small, bias_row, alog_row, s0):
        sp = specs(reverse)

        def kern(xs_r, bm_r, cm_r, sm_r, br_r, ar_r, s0_r, y_r, sf_r, se_r, st, sh_d, sh_a, sh_t):
            c, g = pl.program_id(0), pl.program_id(1)

            @pl.when((c == 0) & (g == 0))
            def _():
                st[...] = s0_r[...]

            @pl.when(g == 0)
            def _():
                sh_d[...], sh_a[...], sh_t[...] = _ssd_shared(sm_r[...], br_r[...], ar_r[...], reverse)

            rows = group_rows(g)
            s_in = st[rows, :]
            se_r[0] = s_in
            y, s_new = _ssd_group(xs_r[...], bm_r[...], cm_r[...], s_in, sh_d[...], sh_a[...], sh_t[...], g, direction, reverse)
            y_r[...] = y
            st[rows, :] = s_new
            sf_r[rows, :] = s_new

        return pl.pallas_call(
            kern, grid=(nc, SSD_GROUPS), name=name + "_fwd",
            in_specs=[sp['xs'], sp['bc'], sp['bc'], sp['small'], sp['row'], sp['row'], sp['state']],
            out_specs=[sp['xs'], sp['state'], sp['enter']],
            out_shape=[jax.ShapeDtypeStruct((n_rows, SSD_INNER), F32), jax.ShapeDtypeStruct((n_state, SSD_STATE), F32),
                       jax.ShapeDtypeStruct((nc, n_state, SSD_STATE), F32)],
            scratch_shapes=[pltpu.VMEM((n_state, SSD_STATE), F32)] + shared_scratch, compiler_params=_params(vmem),
        )(xs, bm, cm, small, bias_row, alog_row, s0)

    def bwd_call(xs, bm, cm, small, bias_row, alog_row, enter, dy, dsf):
        sp = specs(not reverse)

        def kern(xs_r, bm_r, cm_r, sm_r, br_r, ar_r, se_r, dy_r, dsf_r, dxs_r, dbm_r, dcm_r, dsm_r, dbr_r, dar_r, ds0_r,
                 ds, sh_d, sh_a, sh_t, gd, ga, gt):
            c, g = pl.program_id(0), pl.program_id(1)

            @pl.when((c == 0) & (g == 0))
            def _():
                ds[...] = dsf_r[...]
                dbr_r[...] = jnp.zeros_like(dbr_r)
                dar_r[...] = jnp.zeros_like(dar_r)

            @pl.when(g == 0)
            def _():
                sh_d[...], sh_a[...], sh_t[...] = _ssd_shared(sm_r[...], br_r[...], ar_r[...], reverse)
                gd[...] = jnp.zeros_like(gd)
                ga[...] = jnp.zeros_like(ga)
                gt[...] = jnp.zeros_like(gt)

            rows = group_rows(g)
            fn = functools.partial(_ssd_group, g=g, direction=direction, reverse=reverse)
            _, vjp = jax.vjp(fn, xs_r[...], bm_r[...], cm_r[...], se_r[0], sh_d[...], sh_a[...], sh_t[...])
            dxs, dbm, dcm, ds_in, dd, da, dt = vjp((dy_r[...], ds[rows, :]))
            dxs_r[...] = dxs
            dbm_r[...] = dbm
            dcm_r[...] = dcm
            ds[rows, :] = ds_in
            ds0_r[rows, :] = ds_in
            gd[...] += dd
            ga[...] += da
            gt[...] += dt

            @pl.when(g == SSD_GROUPS - 1)
            def _():
                shared = functools.partial(_ssd_shared, reverse=reverse)
                dsm, dbr, dar = jax.vjp(shared, sm_r[...], br_r[...], ar_r[...])[1]((gd[...], ga[...], gt[...]))
                dsm_r[...] = dsm
                dbr_r[...] += dbr
                dar_r[...] += dar

        return pl.pallas_call(
            kern, grid=(nc, SSD_GROUPS), name=name + "_bwd",
            in_specs=[sp['xs'], sp['bc'], sp['bc'], sp['small'], sp['row'], sp['row'], sp['enter'], sp['xs'], sp['state']],
            out_specs=[sp['xs'], sp['bc'], sp['bc'], sp['small'], sp['row'], sp['row'], sp['state']],
            out_shape=[jax.ShapeDtypeStruct(xs.shape, F32), jax.ShapeDtypeStruct(bm.shape, F32),
                       jax.ShapeDtypeStruct(cm.shape, F32), jax.ShapeDtypeStruct((n_rows, LANES), F32),
                       jax.ShapeDtypeStruct((1, LANES), F32), jax.ShapeDtypeStruct((1, LANES), F32),
                       jax.ShapeDtypeStruct(s0.shape, F32)],
            scratch_shapes=[pltpu.VMEM((n_state, SSD_STATE), F32)] + shared_scratch + shared_scratch,
            compiler_params=_params(vmem),
        )(xs, bm, cm, small, bias_row, alog_row, enter, dy, dsf)

    @jax.custom_vjp
    def op(*args):
        y, sf, _ = fwd_call(*args)
        return y, sf

    def op_fwd(*args):
        y, sf, enter = fwd_call(*args)
        return (y, sf), (args[:6], enter)

    op.defvjp(op_fwd, lambda res, g: tuple(bwd_call(*res[0], res[1], g[0], g[1])))
    return op(xs, bm, cm, small, bias_row, alog_row, s0)


def _ml_shared(small, gate_row, reverse):
    gates = small + gate_row
    b_all = _exact_dot(_tri(reverse).astype(F32), jax.nn.log_sigmoid(gates))
    return gates, b_all, gates.T, b_all.T


def _ml_head(q, k, v, c_st, n_st, m_st, gates, b_all, gates_t, b_t, h, direction, reverse):
    mask = _tri(reverse)
    last = 0 if reverse else CHUNK - 1
    lane_i = _MG_LANE + 8 * direction + h
    lane_f = lane_i + ML_HEADS
    b_col = _pick_col(b_all, lane_f)
    b_row = _pick_row(b_t, lane_f)
    li_col = _pick_col(gates, lane_i)
    li_row = _pick_row(gates_t, lane_i)
    rowi = lax.broadcasted_iota(jnp.int32, (CHUNK, 1), 0)
    g_tot = jnp.sum(jnp.where(rowi == last, b_col, 0.0), axis=0, keepdims=True)
    m_in = m_st[:, 0:1]
    q = q * (ML_HD ** -0.5)
    w = g_tot - b_col + li_col
    m_loc = jnp.max(w, axis=0, keepdims=True)
    kw = k * jnp.exp(w - m_loc)
    c_loc = _mm_tn(kw, v)
    n_loc = jnp.sum(kw, axis=0, keepdims=True)
    m_new = jnp.maximum(g_tot + m_in, m_loc)
    s_old = jnp.exp(g_tot + m_in - m_new)
    s_loc = jnp.exp(m_loc - m_new)
    c_new = s_old * c_st + s_loc * c_loc
    n_new = s_old * n_st + s_loc * n_loc
    log_d = jnp.where(mask, b_col - b_row + li_row, -jnp.inf)
    inter = b_col + m_in
    m_t = jnp.maximum(inter, jnp.max(log_d, axis=1, keepdims=True))
    dmat = jnp.exp(log_d - m_t)
    wi = jnp.exp(inter - m_t)
    s = _mm_nt(q, k) * dmat
    num = _mm_nn(s, v) + wi * _mm_nn(q, c_st)
    den = jnp.sum(s, axis=1, keepdims=True) + wi * jnp.sum(_round_bf16(q) * _round_bf16(n_st), axis=1, keepdims=True)
    out = num / jnp.maximum(jnp.abs(den), jnp.exp(-m_t))
    return out, c_new, n_new, jnp.broadcast_to(m_new, (1, LANES))


def _ml_scan(name, q, k, v, small, gate_row, c0, n0, m0, direction, reverse):
    n_rows = q.shape[0]
    nc = n_rows // CHUNK
    vmem = 4 * CHUNK * (4 * ML_HD + 128) * 8 + 4 * ML_HD * ML_HD * 12 + (8 << 20)

    def specs(order):
        def cidx(c):
            return (nc - 1 - c) if order else c

        return dict(
            qkv=pl.BlockSpec((CHUNK, ML_HD), lambda c, h: (cidx(c), h)),
            small=pl.BlockSpec((CHUNK, LANES), lambda c, h: (cidx(c), 0)),
            row=pl.BlockSpec((1, LANES), lambda c, h: (0, 0)),
            c=pl.BlockSpec((ML_HEADS * ML_HD, ML_HD), lambda c, h: (0, 0)),
            n=pl.BlockSpec((ML_HEADS, 1, ML_HD), lambda c, h: (0, 0, 0)),
            m=pl.BlockSpec((ML_HEADS, 1, LANES), lambda c, h: (0, 0, 0)),
            ec=pl.BlockSpec((1, ML_HD, ML_HD), lambda c, h: (cidx(c), h, 0)),
            en=pl.BlockSpec((1, 1, 1, ML_HD), lambda c, h: (cidx(c), h, 0, 0)),
            em=pl.BlockSpec((1, 1, 1, LANES), lambda c, h: (cidx(c), h, 0, 0)),
        )

    st_shapes = [jax.ShapeDtypeStruct((ML_HEADS * ML_HD, ML_HD), F32), jax.ShapeDtypeStruct((ML_HEADS, 1, ML_HD), F32),
                 jax.ShapeDtypeStruct((ML_HEADS, 1, LANES), F32)]
    scratch = [pltpu.VMEM((ML_HEADS * ML_HD, ML_HD), F32), pltpu.VMEM((ML_HEADS, 1, ML_HD), F32),
               pltpu.VMEM((ML_HEADS, 1, LANES), F32)]
    shared_scratch = [pltpu.VMEM((CHUNK, LANES), F32), pltpu.VMEM((CHUNK, LANES), F32),
                      pltpu.VMEM((LANES, CHUNK), F32), pltpu.VMEM((LANES, CHUNK), F32)]

    def head_rows(h):
        return pl.ds(pl.multiple_of(h * ML_HD, ML_HD), ML_HD)

    def fwd_call(q, k, v, small, gate_row, c0, n0, m0):
        sp = specs(reverse)

        def kern(q_r, k_r, v_r, sm_r, gr_r, c0_r, n0_r, m0_r, o_r, cf_r, nf_r, mf_r, ec_r, en_r, em_r, cs, ns, ms, *sh):
            c, h = pl.program_id(0), pl.program_id(1)

            @pl.when((c == 0) & (h == 0))
            def _():
                cs[...] = c0_r[...]
                ns[...] = n0_r[...]
                ms[...] = m0_r[...]

            @pl.when(h == 0)
            def _():
                for ref, val in zip(sh, _ml_shared(sm_r[...], gr_r[...], reverse)):
                    ref[...] = val

            rows = head_rows(h)
            c_in, n_in, m_in = cs[rows, :], ns[h], ms[h]
            ec_r[0] = c_in
            en_r[0, 0] = n_in
            em_r[0, 0] = m_in
            out, c_new, n_new, m_new = _ml_head(q_r[...], k_r[...], v_r[...], c_in, n_in, m_in, *[r[...] for r in sh],
                                                h, direction, reverse)
            o_r[...] = out
            cs[rows, :] = c_new
            ns[h] = n_new
            ms[h] = m_new
            cf_r[rows, :] = c_new
            nf_r[h] = n_new
            mf_r[h] = m_new

        return pl.pallas_call(
            kern, grid=(nc, ML_HEADS), name=name + "_fwd",
            in_specs=[sp['qkv']] * 3 + [sp['small'], sp['row'], sp['c'], sp['n'], sp['m']],
            out_specs=[sp['qkv'], sp['c'], sp['n'], sp['m'], sp['ec'], sp['en'], sp['em']],
            out_shape=[jax.ShapeDtypeStruct((n_rows, ML_HEADS * ML_HD), F32)] + st_shapes + [
                jax.ShapeDtypeStruct((nc, ML_HEADS * ML_HD, ML_HD), F32),
                jax.ShapeDtypeStruct((nc, ML_HEADS, 1, ML_HD), F32), jax.ShapeDtypeStruct((nc, ML_HEADS, 1, LANES), F32)],
            scratch_shapes=scratch + shared_scratch, compiler_params=_params(vmem),
        )(q, k, v, small, gate_row, c0, n0, m0)

    def bwd_call(q, k, v, small, gate_row, ec, en, em, do, dcf, dnf, dmf):
        sp = specs(not reverse)
        n_sh = len(shared_scratch)

        def kern(q_r, k_r, v_r, sm_r, gr_r, ec_r, en_r, em_r, do_r, dcf_r, dnf_r, dmf_r,
                 dq_r, dk_r, dv_r, dsm_r, dgr_r, dc0_r, dn0_r, dm0_r, dcs, dns, dms, *rest):
            sh, gsh = rest[:n_sh], rest[n_sh:]
            c, h = pl.program_id(0), pl.program_id(1)

            @pl.when((c == 0) & (h == 0))
            def _():
                dcs[...] = dcf_r[...]
                dns[...] = dnf_r[...]
                dms[...] = dmf_r[...]
                dgr_r[...] = jnp.zeros_like(dgr_r)

            @pl.when(h == 0)
            def _():
                for ref, val in zip(sh, _ml_shared(sm_r[...], gr_r[...], reverse)):
                    ref[...] = val
                for ref in gsh:
                    ref[...] = jnp.zeros_like(ref)

            rows = head_rows(h)
            fn = functools.partial(_ml_head, h=h, direction=direction, reverse=reverse)
            _, vjp = jax.vjp(fn, q_r[...], k_r[...], v_r[...], ec_r[0], en_r[0, 0], em_r[0, 0], *[r[...] for r in sh])
            grads = vjp((do_r[...], dcs[rows, :], dns[h], dms[h]))
            dq, dk, dv, dc, dn, dm = grads[:6]
            dq_r[...] = dq
            dk_r[...] = dk
            dv_r[...] = dv
            for ref, val in zip(gsh, grads[6:]):
                ref[...] += val
            dm = jnp.broadcast_to(jnp.sum(dm, axis=1, keepdims=True), (1, LANES)) * (1.0 / LANES)
            dcs[rows, :] = dc
            dns[h] = dn
            dms[h] = dm
            dc0_r[rows, :] = dc
            dn0_r[h] = dn
            dm0_r[h] = dm

            @pl.when(h == ML_HEADS - 1)
            def _():
                shared = functools.partial(_ml_shared, reverse=reverse)
                dsm, dgr = jax.vjp(shared, sm_r[...], gr_r[...])[1](tuple(r[...] for r in gsh))
                dsm_r[...] = dsm
                dgr_r[...] += dgr

        return pl.pallas_call(
            kern, grid=(nc, ML_HEADS), name=name + "_bwd",
            in_specs=[sp['qkv']] * 3 + [sp['small'], sp['row'], sp['ec'], sp['en'], sp['em'], sp['qkv'], sp['c'], sp['n'], sp['m']],
            out_specs=[sp['qkv']] * 3 + [sp['small'], sp['row'], sp['c'], sp['n'], sp['m']],
            out_shape=[jax.ShapeDtypeStruct(q.shape, F32)] * 3 + [jax.ShapeDtypeStruct((n_rows, LANES), F32),
                                                                  jax.ShapeDtypeStruct((1, LANES), F32)] + st_shapes,
            scratch_shapes=scratch + shared_scratch + shared_scratch, compiler_params=_params(vmem),
        )(q, k, v, small, gate_row, ec, en, em, do, dcf, dnf, dmf)

    @jax.custom_vjp
    def op(*args):
        return tuple(fwd_call(*args)[:4])

    def op_fwd(*args):
        res = fwd_call(*args)
        return tuple(res[:4]), (args[:5], tuple(res[4:]))

    op.defvjp(op_fwd, lambda res, g: tuple(bwd_call(*res[0], *res[1], *g)))
    return op(q, k, v, small, gate_row, c0, n0, m0)


def _f_modulate(x, shift, scale):
    return (_layernorm_rows(x) * (1.0 + scale) + shift,)


def _f_resid_ln(x, o, gate, bias, ln_g, ln_b):
    return (_layernorm_rows(DN_ALPHA * x + gate * (o + bias)) * ln_g + ln_b,)


def _f_lru_gates(xc, w_r, b_r, w_i, b_i, lam):
    outs = []
    for d in range(2):
        def blockdiag(w):
            return jnp.concatenate(
                [_mm_nn(xc[:, n * LRU_BS:(n + 1) * LRU_BS], w[(d * LRU_BLOCKS + n) * LRU_BS:(d * LRU_BLOCKS + n + 1) * LRU_BS, :])
                 for n in range(LRU_BLOCKS)], axis=1)

        r = jax.nn.sigmoid(blockdiag(w_r) + b_r[d:d + 1])
        i = jax.nn.sigmoid(blockdiag(w_i) + b_i[d:d + 1])
        log_a = -LRU_C * r * jax.nn.softplus(-lam[d:d + 1])
        outs += [jnp.exp(log_a), jnp.sqrt(1.0 - jnp.exp(2.0 * log_a)) * i * xc]
    return tuple(outs)


def _f_lru_out(h_f, h_b, ly):
    return ((h_f + h_b) * jax.nn.gelu(ly),)


def _f_ssd_post(y_f, y_b, xs, z, d_exp, norm_w):
    y = (y_f + y_b + xs * d_exp) * jax.nn.silu(z)
    gw = SSD_INNER // SSD_GROUPS
    parts = []
    for g in range(SSD_GROUPS):
        yg = y[:, g * gw:(g + 1) * gw]
        parts.append(yg * lax.rsqrt(jnp.mean(jnp.square(yg), -1, keepdims=True) + LN_EPS))
    return (jnp.concatenate(parts, axis=1) * norm_w,)


def _f_ml_post(h_f, h_b, o, norm_w):
    h = h_f + h_b
    parts = [_layernorm_rows(h[:, i * ML_HD:(i + 1) * ML_HD]) for i in range(ML_HEADS)]
    return (jnp.concatenate(parts, axis=1) * norm_w * jax.nn.sigmoid(o),)


def _f_merge(ga, gb, gc, pa, pb, pc):
    return (jax.nn.sigmoid(ga) * pa + jax.nn.sigmoid(gb) * pb + jax.nn.sigmoid(gc) * pc,)


def _f_relu2(pre, bias):
    return (jnp.square(jax.nn.relu(pre + bias)),)


def _lane_row(vec, start):
    return jnp.pad(vec[None], ((0, 0), (start, LANES - start - vec.shape[0])))


def _mixer(tag, h, p, states):
    (lru_s, ssd_s, ml_s) = states
    lx, ly, sz, xs, bm, cm, mq, mk, mv, mo, ga, gb, gc = _linear(tag + "in", h, p['w_in_main'], _IN_MAIN_WIDTHS)
    small = _linear(tag + "insmall", h, p['w_in_small'])

    xc = _dwconv(tag + "lruconv", lx, p['lru_conv_w'], p['lru_conv_b'][None], False)
    a_f, b_f, a_b, b_b = _rowwise(
        tag + "lrugate", _f_lru_gates, [xc],
        [p['lru_w_r'].reshape(2 * LRU_BLOCKS * LRU_BS, LRU_BS), p['lru_b_r'], p['lru_w_i'].reshape(2 * LRU_BLOCKS * LRU_BS, LRU_BS),
         p['lru_b_i'], p['lru_lambda']], [D_MODEL] * 4)
    h_f, s_f = _lin_scan(tag + "lruscanf", a_f, b_f, lru_s[0], False)
    h_b, s_b = _lin_scan(tag + "lruscanb", a_b, b_b, lru_s[1], True)
    (ya,) = _rowwise(tag + "lruout", _f_lru_out, [h_f, h_b, ly], [], [D_MODEL])

    cw, cb_ = p['ssd_conv_w'], p['ssd_conv_b'][None]
    xs_c = _dwconv(tag + "ssdconvx", xs, cw[:, :2048], cb_[:, :2048], True)
    bm_c = _dwconv(tag + "ssdconvb", bm, cw[:, 2048:3072], cb_[:, 2048:3072], True)
    cm_c = _dwconv(tag + "ssdconvc", cm, cw[:, 3072:], cb_[:, 3072:], True)
    ssd_new, ys = [], []
    for d in range(2):
        y_d, st_d = _ssd_scan(tag + "ssd%d" % d, xs_c, bm_c, cm_c, small, _lane_row(p['ssd_dt_bias'][d], _DT_LANE + 32 * d),
                              _lane_row(p['ssd_a_log'][d], _DT_LANE + 32 * d), ssd_s[d], d, d == 1)
        ys.append(y_d)
        ssd_new.append(st_d)
    (yb,) = _rowwise(tag + "ssdpost", _f_ssd_post, [ys[0], ys[1], xs_c, sz],
                     [jnp.repeat(p['ssd_d'], SSD_HEADDIM)[None], p['ssd_norm_w'][None]], [SSD_INNER])

    mw, mb = p['ml_conv_w'], p['ml_conv_b'][None]
    q_c = _dwconv(tag + "mlconvq", mq, mw[:, :1024], mb[:, :1024], True)
    k_c = _dwconv(tag + "mlconvk", mk, mw[:, 1024:], mb[:, 1024:], True)
    gate_row = _lane_row(p['ml_gate_b'].reshape(4 * ML_HEADS), _MG_LANE)
    ml_new, hs = [], []
    for d in range(2):
        o_d, c_d, n_d, m_d = _ml_scan(tag + "ml%d" % d, q_c, k_c, mv, small, gate_row, *ml_s[d], d, d == 1)
        hs.append(o_d)
        ml_new.append((c_d, n_d, m_d))
    (yc,) = _rowwise(tag + "mlpost", _f_ml_post, [hs[0], hs[1], mo], [p['ml_norm_w'][None]], [D_MODEL])
    return (ya, yb, yc, ga, gb, gc), ((s_f, s_b), tuple(ssd_new), tuple(ml_new))


def _merge(tag, br, p):
    ya, yb, yc, ga, gb, gc = br
    pa = _linear(tag + "bra", ya, p['w_br_a'])
    pb = _linear(tag + "brb", yb, p['w_br_b'])
    pc = _linear(tag + "brc", yc, p['w_br_c'])
    (m,) = _rowwise(tag + "merge", _f_merge, [ga, gb, gc, pa, pb, pc], [], [D_MODEL])
    return _linear(tag + "out", m, p['w_out'])


def _sublayers(tag, xin, o, mods, p, l):
    sh2, sc2, g1, g2 = mods
    (x1,) = _rowwise(tag + "ln1", _f_resid_ln, [xin, o], [g1, p['b_out'][None], p['ln1_g'][None], p['ln1_b'][None]], [D_MODEL])
    (h2,) = _rowwise(tag + "mod2", _f_modulate, [x1], [sh2, sc2], [D_MODEL])
    pre = _linear(tag + "ff1", h2, p['w_ff1'])
    (u2,) = _rowwise(tag + "relu2", _f_relu2, [pre], [p['b_ff1'][None]], [D_FF])
    o2 = _linear(tag + "ff2", u2, p['w_ff2'])
    (x2,) = _rowwise(tag + "ln2", _f_resid_ln, [x1, o2], [g2, p['b_ff2'][None], p['ln2_g'][None], p['ln2_b'][None]], [D_MODEL])
    return x2


def _to_col_major(h):
    s, d = h.shape
    return h.reshape(s // GRID_W, GRID_W, d).swapaxes(0, 1).reshape(s, d)


def _from_col_major(h):
    s, d = h.shape
    return h.reshape(GRID_W, s // GRID_W, d).swapaxes(0, 1).reshape(s, d)


def _forward(x, wts, cvec, ctx):
    zeros = lambda *s: jnp.zeros(s, F32)
    ctx_init = ((zeros(1, D_MODEL), zeros(1, D_MODEL)),
                (zeros(SSD_INNER, SSD_STATE), zeros(SSD_INNER, SSD_STATE)),
                tuple((zeros(ML_HEADS * ML_HD, ML_HD), zeros(ML_HEADS, 1, ML_HD), zeros(ML_HEADS, 1, LANES)) for _ in range(2)))
    for l in range(DEPTH):
        p = {n: (wts[n] if n == 'c_ctx' else wts[n][l]) for n in wts}
        tag = "l%d" % l
        cc = jnp.concatenate([cvec, p['c_ctx'][None], jnp.zeros((SUBLANES - 2, D_MODEL), F32)], axis=0)
        mod = _linear(tag + "ada", jax.nn.silu(cc), p['w_ada']) + p['b_ada'][None]
        sh1x, sc1x, g1x, sh2x, sc2x, g2x = [mod[0:1, i * D_MODEL:(i + 1) * D_MODEL] for i in range(6)]
        sh1c, sc1c, g1c, sh2c, sc2c, g2c = [mod[1:2, i * D_MODEL:(i + 1) * D_MODEL] for i in range(6)]
        (hc,) = _rowwise(tag + "cmod1", _f_modulate, [ctx], [sh1c, sc1c], [D_MODEL])
        br_c, ctx_states = _mixer(tag + "c", hc, p, ctx_init)
        (hx,) = _rowwise(tag + "xmod1", _f_modulate, [x], [sh1x, sc1x], [D_MODEL])
        if l % 2 == 1:
            hx = _to_col_major(hx)
        br_x, _ = _mixer(tag + "x", hx, p, ctx_states)
        ox = _merge(tag + "x", br_x, p)
        if l % 2 == 1:
            ox = _from_col_major(ox)
        x = _sublayers(tag + "x", x, ox, (sh2x, sc2x, g1x, g2x), p, l)
        if l < DEPTH - 1:
            ctx = _sublayers(tag + "c", ctx, _merge(tag + "c", br_c, p), (sh2c, sc2c, g1c, g2c), p, l)
    return x


def _loss_and_cotangent(y, target):
    n_rows, d = y.shape
    tt = _row_tile(n_rows, 0, cap=256)

    def kern(y_ref, t_ref, dy_ref, acc_ref):
        @pl.when(pl.program_id(0) == 0)
        def _():
            acc_ref[...] = jnp.zeros_like(acc_ref)

        err = y_ref[...] - t_ref[...]
        dy_ref[...] = err * (1.0 / d)
        acc_ref[...] += jnp.sum(jnp.square(err))

    spec = pl.BlockSpec((tt, d), lambda i: (i, 0))
    dy, acc = pl.pallas_call(
        kern, grid=(n_rows // tt,), name="loss", in_specs=[spec, spec],
        out_specs=[spec, pl.BlockSpec((SUBLANES, LANES), lambda i: (0, 0))],
        out_shape=[jax.ShapeDtypeStruct((n_rows, d), F32), jax.ShapeDtypeStruct((SUBLANES, LANES), F32)],
    )(y, target)
    return acc[0, 0] * (0.5 / d), dy


def _exchange(name, src, gather):
    slab = src.shape if gather else src.shape[1:]

    def body(src_ref, out_ref, send_sems, recv_sems, local_sem):
        x, y, c = lax.axis_index("x"), lax.axis_index("y"), lax.axis_index("c")
        me = 4 * x + 2 * y + c
        local = pltpu.make_async_copy(src_ref if gather else src_ref.at[me], out_ref.at[me], local_sem)
        local.start()
        copies = []
        for d in range(1, N_DEV):
            px, py, pc = lax.rem(x + (d >> 2), 2), lax.rem(y + ((d >> 1) & 1), 2), lax.rem(c + (d & 1), 2)
            peer = 4 * px + 2 * py + pc
            cp = pltpu.make_async_remote_copy(
                src_ref=src_ref if gather else src_ref.at[peer], dst_ref=out_ref.at[me],
                send_sem=send_sems.at[d - 1], recv_sem=recv_sems.at[d - 1],
                device_id=(px, py, pc), device_id_type=pl.DeviceIdType.MESH)
            cp.start()
            copies.append(cp)
        for cp in copies:
            cp.wait()
        local.wait()

    return pl.pallas_call(
        body, name=name, out_shape=jax.ShapeDtypeStruct((N_DEV,) + tuple(slab), src.dtype),
        in_specs=[pl.BlockSpec(memory_space=pl.ANY)], out_specs=pl.BlockSpec(memory_space=pl.ANY),
        scratch_shapes=[pltpu.SemaphoreType.DMA((N_DEV - 1,)), pltpu.SemaphoreType.DMA((N_DEV - 1,)), pltpu.SemaphoreType.DMA],
    )(src)


_HBM = pl.BlockSpec(memory_space=pl.ANY)
_CHIPS = ((0, 0), (0, 1), (1, 0), (1, 1))


def _gather_two_level(name, src):
    def body(src_ref, out_ref, send_sems, recv_sems, local_sem):
        x, y, c = lax.axis_index("x"), lax.axis_index("y"), lax.axis_index("c")
        me, sibling = (x, y, c), (x, y, 1 - c)
        chips = [(1 - x, y), (x, 1 - y), (1 - x, 1 - y)]

        def slab(px, py, pc):
            return out_ref.at[4 * px + 2 * py + pc]

        def copy(k, block, to, src=None):
            return pltpu.make_async_remote_copy(
                src_ref=slab(*block) if src is None else src, dst_ref=slab(*block), send_sem=send_sems.at[k],
                recv_sem=recv_sems.at[k], device_id=to, device_id_type=pl.DeviceIdType.MESH)

        mine = pltpu.make_async_copy(src_ref, slab(*me), local_sem)
        mine.start()
        first = [copy(0, me, sibling, src=src_ref)] + [copy(1 + j, me, (*chip, c), src=src_ref) for j, chip in enumerate(chips)]
        for cp in first:
            cp.start()
        passed = [copy(4 + j, (*chip, c), sibling) for j, chip in enumerate(chips)]
        for j, chip in enumerate(chips):
            copy(1 + j, (*chip, c), me).wait_recv()
            passed[j].start()
        copy(0, sibling, me).wait_recv()
        for j, chip in enumerate(chips):
            copy(4 + j, (*chip, 1 - c), me).wait_recv()
        for cp in first + passed:
            cp.wait_send()
        mine.wait()

    return pl.pallas_call(
        body, name=name, out_shape=jax.ShapeDtypeStruct((N_DEV,) + tuple(src.shape), src.dtype),
        in_specs=[_HBM], out_specs=_HBM,
        scratch_shapes=[pltpu.SemaphoreType.DMA((N_DEV - 1,)), pltpu.SemaphoreType.DMA((N_DEV - 1,)), pltpu.SemaphoreType.DMA],
    )(src)


def _scatter_to_sibling(name, parts):
    def body(p_ref, out_ref, send_sems, recv_sems):
        x, y, c = lax.axis_index("x"), lax.axis_index("y"), lax.axis_index("c")
        copies = []
        for j, (px, py) in enumerate(_CHIPS):
            cp = pltpu.make_async_remote_copy(
                src_ref=p_ref.at[4 * px + 2 * py + (1 - c)], dst_ref=out_ref.at[j], send_sem=send_sems.at[j],
                recv_sem=recv_sems.at[j], device_id=(x, y, 1 - c), device_id_type=pl.DeviceIdType.MESH)
            cp.start()
            copies.append(cp)
        for cp in copies:
            cp.wait()

    return pl.pallas_call(
        body, name=name, out_shape=jax.ShapeDtypeStruct((4,) + tuple(parts.shape[1:]), parts.dtype),
        in_specs=[_HBM], out_specs=_HBM,
        scratch_shapes=[pltpu.SemaphoreType.DMA((4,)), pltpu.SemaphoreType.DMA((4,))],
    )(parts)


def _chip_sum(name, parts, from_sibling):
    _, rows, cols = parts.shape
    lanes = -(-cols // LANES) * LANES
    tr = _row_tile(rows, 4 * lanes * 4 * 2, budget=12 << 20)

    def kern(p_ref, s_ref, o_ref):
        c = lax.axis_index("c")
        o_ref[0] = jnp.where(c == 0, p_ref[0, 0], p_ref[0, 1]) + s_ref[0]

    return pl.pallas_call(
        kern, grid=(4, rows // tr), name=name,
        in_specs=[pl.BlockSpec((1, 2, tr, cols), lambda j, i: (j, 0, i, 0)), pl.BlockSpec((1, tr, cols), lambda j, i: (j, i, 0))],
        out_specs=pl.BlockSpec((1, tr, cols), lambda j, i: (j, i, 0)),
        out_shape=jax.ShapeDtypeStruct((4, rows, cols), F32),
        compiler_params=_params(4 * lanes * tr * 4 * 2),
    )(parts.reshape(4, 2, rows, cols), from_sibling)


def _scatter_across_chips(name, sums):
    def body(q_ref, out_ref, send_sems, recv_sems, local_sem):
        x, y, c = lax.axis_index("x"), lax.axis_index("y"), lax.axis_index("c")
        own = 2 * x + y
        local = pltpu.make_async_copy(q_ref.at[own], out_ref.at[own], local_sem)
        local.start()
        copies = []
        for d in range(1, 4):
            px, py = lax.rem(x + (d >> 1), 2), lax.rem(y + (d & 1), 2)
            cp = pltpu.make_async_remote_copy(
                src_ref=q_ref.at[2 * px + py], dst_ref=out_ref.at[own], send_sem=send_sems.at[d - 1],
                recv_sem=recv_sems.at[d - 1], device_id=(px, py, c), device_id_type=pl.DeviceIdType.MESH)
            cp.start()
            copies.append(cp)
        for cp in copies:
            cp.wait()
        local.wait()

    return pl.pallas_call(
        body, name=name, out_shape=jax.ShapeDtypeStruct(sums.shape, sums.dtype), in_specs=[_HBM], out_specs=_HBM,
        scratch_shapes=[pltpu.SemaphoreType.DMA((3,)), pltpu.SemaphoreType.DMA((3,)), pltpu.SemaphoreType.DMA],
    )(sums)


def _sum_parts(name, parts):
    n_parts, rows, cols = parts.shape
    tr = _row_tile(rows, 4 * cols * (n_parts + 1) * 2)

    def kern(p_ref, o_ref):
        acc = p_ref[0]
        for k in range(1, n_parts):
            acc = acc + p_ref[k]
        o_ref[...] = acc

    return pl.pallas_call(
        kern, grid=(rows // tr,), name=name, in_specs=[pl.BlockSpec((n_parts, tr, cols), lambda i: (0, i, 0))],
        out_specs=pl.BlockSpec((tr, cols), lambda i: (i, 0)), out_shape=jax.ShapeDtypeStruct((rows, cols), F32),
    )(parts)


def _adamw(name, w, m, v, parts):
    n_parts, rows, cols = parts.shape
    lanes = -(-cols // LANES) * LANES
    tr = _row_tile(rows, 4 * lanes * (n_parts + 7) * 2, budget=16 << 20)
    c1 = np.float32(1.0 - ADAM_B1 ** ADAM_STEP)
    c2 = np.float32(1.0 - ADAM_B2 ** ADAM_STEP)

    def kern(w_ref, m_ref, v_ref, p_ref, g_ref, d_ref, nm_ref, nv_ref):
        g = p_ref[0]
        for k in range(1, n_parts):
            g = g + p_ref[k]
        m_new = ADAM_B1 * m_ref[...] + (1.0 - ADAM_B1) * g
        v_new = ADAM_B2 * v_ref[...] + (1.0 - ADAM_B2) * jnp.square(g)
        g_ref[...] = g
        nm_ref[...] = m_new
        nv_ref[...] = v_new
        d_ref[...] = -ADAM_LR * ((m_new / c1) / (jnp.sqrt(v_new / c2) + ADAM_EPS) + ADAM_WD * w_ref[...])

    spec = pl.BlockSpec((tr, cols), lambda i: (i, 0))
    return pl.pallas_call(
        kern, grid=(rows // tr,), name=name,
        in_specs=[spec, spec, spec, pl.BlockSpec((n_parts, tr, cols), lambda i: (0, i, 0))], out_specs=[spec] * 4,
        out_shape=[jax.ShapeDtypeStruct((rows, cols), F32)] * 4,
        compiler_params=_params(4 * lanes * tr * (n_parts + 7) * 2),
    )(w, m, v, parts)


def _packed_rows(shape):
    return -(-int(np.prod(shape)) // (SUBLANES * LANES)) * SUBLANES


def _pack(arrays, row_multiple):
    parts = []
    for a in arrays:
        n = int(np.prod(a.shape))
        r = _packed_rows(a.shape)
        parts.append(jnp.pad(a.reshape(-1), (0, r * LANES - n)).reshape(r, LANES))
    rows = sum(p.shape[0] for p in parts)
    total = -(-rows // row_multiple) * row_multiple
    if total > rows:
        parts.append(jnp.zeros((total - rows, LANES), arrays[0].dtype))
    return jnp.concatenate(parts, axis=0)


def _unpack(packed, shapes):
    out, off = [], 0
    for s in shapes:
        r = _packed_rows(s)
        out.append(packed[off:off + r].reshape(-1)[:int(np.prod(s))].reshape(s))
        off += r
    return out


def _split_w_in(w_in):
    main = jnp.concatenate([w_in[:, :, s:e] for s, e in _IN_MAIN], axis=2)
    pad = jnp.zeros(w_in.shape[:2] + (LANES - 80,), w_in.dtype)
    small = jnp.concatenate([w_in[:, :, s:e] for s, e in _IN_SMALL] + [pad], axis=2)
    return main, small


def _join_w_in(main, small):
    return jnp.concatenate([main[:, :, 0:8192], small[:, :, 0:64], main[:, :, 8192:12288], small[:, :, 64:80],
                            main[:, :, 12288:15360]], axis=2)


def _unshard(gathered, axis):
    nd, nl, r, c = gathered.shape
    if axis == 1:
        return gathered.transpose(1, 0, 2, 3).reshape(nl, nd * r, c)
    return gathered.transpose(1, 2, 0, 3).reshape(nl, r, nd * c)


def _reshard(full, axis):
    nl, r, c = full.shape
    if axis == 1:
        return full.reshape(nl, N_DEV, r // N_DEV, c).transpose(1, 0, 2, 3)
    return full.reshape(nl, r, N_DEV, c // N_DEV).transpose(2, 0, 1, 3)


def kernel(x, c, ctx, c_ctx, w_ada, b_ada, w_in, lru_conv_w, lru_conv_b, lru_w_r, lru_b_r, lru_w_i, lru_b_i, lru_lambda, ssd_conv_w, ssd_conv_b, ssd_dt_bias, ssd_a_log, ssd_d, ssd_norm_w, ml_conv_w, ml_conv_b, ml_gate_b, ml_norm_w, w_br_a, w_br_b, w_br_c, w_out, b_out, ln1_g, ln1_b, w_ff1, b_ff1, w_ff2, b_ff2, ln2_g, ln2_b, loss_target, m_c_ctx, m_w_ada, m_b_ada, m_w_in, m_lru_conv_w, m_lru_conv_b, m_lru_w_r, m_lru_b_r, m_lru_w_i, m_lru_b_i, m_lru_lambda, m_ssd_conv_w, m_ssd_conv_b, m_ssd_dt_bias, m_ssd_a_log, m_ssd_d, m_ssd_norm_w, m_ml_conv_w, m_ml_conv_b, m_ml_gate_b, m_ml_norm_w, m_w_br_a, m_w_br_b, m_w_br_c, m_w_out, m_b_out, m_ln1_g, m_ln1_b, m_w_ff1, m_b_ff1, m_w_ff2, m_b_ff2, m_ln2_g, m_ln2_b, v_c_ctx, v_w_ada, v_b_ada, v_w_in, v_lru_conv_w, v_lru_conv_b, v_lru_w_r, v_lru_b_r, v_lru_w_i, v_lru_b_i, v_lru_lambda, v_ssd_conv_w, v_ssd_conv_b, v_ssd_dt_bias, v_ssd_a_log, v_ssd_d, v_ssd_norm_w, v_ml_conv_w, v_ml_conv_b, v_ml_gate_b, v_ml_norm_w, v_w_br_a, v_w_br_b, v_w_br_c, v_w_out, v_b_out, v_ln1_g, v_ln1_b, v_w_ff1, v_b_ff1, v_w_ff2, v_b_ff2, v_ln2_g, v_ln2_b):
    a = dict(locals())
    me = 4 * lax.axis_index("x") + 2 * lax.axis_index("y") + lax.axis_index("c")

    wts = {n: a[n] for n in _REPLICATED}
    for n, axis in _BIG.items():
        full = _unshard(_gather_two_level("gather_" + n, a[n].astype(BF16)), axis)
        if n == 'w_in':
            main, small = _split_w_in(full)
            wts['w_in_main'], wts['w_in_small'] = main.astype(F32), small.astype(F32)
        else:
            wts[n] = full.astype(F32)
    small_shapes = [a[n].shape for n in _SMALL_SHARDED]
    small_all = _exchange("gather_small", _pack([a[n] for n in _SMALL_SHARDED], SUBLANES), True)
    per_dev = [_unpack(small_all[k], small_shapes) for k in range(N_DEV)]
    for i, n in enumerate(_SMALL_SHARDED):
        wts[n] = jnp.concatenate([per_dev[k][i] for k in range(N_DEV)], axis=-1)

    y, vjp = jax.vjp(functools.partial(_forward, cvec=c, ctx=ctx[0]), x[0], wts)
    loss_local, dy = _loss_and_cotangent(y, loss_target[0])
    grad_x, grads = vjp(dy)
    grads['w_in'] = _join_w_in(grads.pop('w_in_main'), grads.pop('w_in_small'))
    loss = lax.psum(loss_local, ("x", "y", "c"))

    out = {}

    def put(n, res, shape):
        for kind, r in zip(("grad_", "delta_", "new_m_", "new_v_"), res):
            out[kind + n] = r.reshape(shape)

    for n, axis in _BIG.items():
        shp = a[n].shape
        rows, cols = shp[0] * shp[1], shp[2]
        parts = _reshard(grads[n], axis).reshape(N_DEV, rows, cols)
        chip = _chip_sum("chipsum_" + n, parts, _scatter_to_sibling("scatter_d2d_" + n, parts))
        parts = _scatter_across_chips("scatter_ici_" + n, chip)
        put(n, _adamw("adamw_" + n, a[n].reshape(rows, cols), a["m_" + n].reshape(rows, cols), a["v_" + n].reshape(rows, cols), parts), shp)

    rep_names = _REPLICATED + _SMALL_SHARDED
    chunk_rows = SUBLANES * N_DEV
    g_pack = _pack([grads[n] for n in rep_names], chunk_rows * N_DEV)
    rows = g_pack.shape[0]
    parts = _exchange("scatter_rep", g_pack.reshape(N_DEV, rows // N_DEV, LANES), False)
    mine = _sum_parts("sum_rep", parts)
    g_all = _exchange("gather_rep", mine, True).reshape(rows, LANES)
    g_full = _unpack(g_all, [grads[n].shape for n in rep_names])
    g_local = []
    for n, g in zip(rep_names, g_full):
        if n in _SMALL_SHARDED:
            width = a[n].shape[-1]
            g = lax.dynamic_slice_in_dim(g, me * width, width, axis=g.ndim - 1)
        g_local.append(g)
    shapes = [a[n].shape for n in rep_names]
    res = _adamw("adamw_rep", _pack([a[n] for n in rep_names], chunk_rows), _pack([a["m_" + n] for n in rep_names], chunk_rows),
                 _pack([a["v_" + n] for n in rep_names], chunk_rows), _pack(g_local, chunk_rows)[None])
    unpacked = [_unpack(r, shapes) for r in res]
    for i, n in enumerate(rep_names):
        put(n, [u[i] for u in unpacked], shapes[i])

    outs = [loss, grad_x[None]]
    for kind in ("grad_", "delta_", "new_m_", "new_v_"):
        outs += [out[kind + n] for n in _WEIGHTS]
    return tuple(outs)
```

```python
import functools

import numpy as np
import jax
import jax.numpy as jnp
from jax import lax
from jax.experimental import pallas as pl
from jax.experimental.pallas import tpu as pltpu

F32 = jnp.float32
BF16 = jnp.bfloat16

N_DEV = 8
D_MODEL = 1024
DEPTH = 2
GRID_W = 64
CHUNK = 128
LN_EPS = 1e-6
LRU_BLOCKS = 8
LRU_BS = 128
LRU_C = 8.0
SSD_INNER = 2048
SSD_GROUPS = 8
SSD_HPG = 4
SSD_HEADDIM = 64
SSD_STATE = 128
ML_HEADS = 4
ML_HD = 256
D_FF = 4096
DN_ALPHA = (2 * DEPTH) ** 0.25
ADAM_LR, ADAM_B1, ADAM_B2, ADAM_EPS, ADAM_WD, ADAM_STEP = 0.001, 0.9, 0.999, 1e-08, 0.01, 10

VMEM_CAP = 60 * 1024 * 1024
SUBLANES = 8
LANES = 128

_IN_MAIN = ((0, 8192), (8256, 12352), (12368, 15440))
_IN_MAIN_WIDTHS = (1024, 1024, 2048, 2048, 1024, 1024, 1024, 1024, 1024, 1024, 1024, 1024, 1024)
_IN_SMALL = ((8192, 8256), (12352, 12368))
_DT_LANE = 0
_MG_LANE = 64

_WEIGHTS = ['c_ctx', 'w_ada', 'b_ada', 'w_in', 'lru_conv_w', 'lru_conv_b', 'lru_w_r', 'lru_b_r', 'lru_w_i', 'lru_b_i',
            'lru_lambda', 'ssd_conv_w', 'ssd_conv_b', 'ssd_dt_bias', 'ssd_a_log', 'ssd_d', 'ssd_norm_w', 'ml_conv_w',
            'ml_conv_b', 'ml_gate_b', 'ml_norm_w', 'w_br_a', 'w_br_b', 'w_br_c', 'w_out', 'b_out', 'ln1_g', 'ln1_b',
            'w_ff1', 'b_ff1', 'w_ff2', 'b_ff2', 'ln2_g', 'ln2_b']
_BIG = {'w_ada': 2, 'w_in': 2, 'w_ff1': 2, 'w_br_a': 1, 'w_br_b': 1, 'w_br_c': 1, 'w_out': 1, 'w_ff2': 1}
_SMALL_SHARDED = ['lru_conv_w', 'lru_b_r', 'lru_b_i', 'lru_lambda', 'ssd_conv_w', 'ml_conv_w']
_REPLICATED = [n for n in _WEIGHTS if n not in _BIG and n not in _SMALL_SHARDED]


def _params(vmem_bytes):
    return pltpu.CompilerParams(vmem_limit_bytes=int(min(max(2 * vmem_bytes, 32 << 20), VMEM_CAP)))


def _row_tile(n_rows, bytes_per_row, budget=6 << 20, cap=512):
    t = cap
    while t > SUBLANES and (t * bytes_per_row > budget or n_rows % t):
        t //= 2
    assert n_rows % t == 0, (n_rows, t)
    return t


def _dg(a, b, ca, cb):
    return lax.dot_general(a.astype(BF16), b.astype(BF16), (((ca,), (cb,)), ((), ())), preferred_element_type=F32)


def _make_bdot(ca, cb):
    @jax.custom_vjp
    def f(a, b):
        return _dg(a, b, ca, cb)

    def fwd(a, b):
        return _dg(a, b, ca, cb), (a, b)

    def bwd(res, g):
        a, b = res
        da = _dg(g, b, 1, 1 - cb) if ca == 1 else _dg(b, g, 1 - cb, 1)
        db = _dg(a, g, 1 - ca, 0) if cb == 0 else _dg(g, a, 0, 1 - ca)
        return da, db

    f.defvjp(fwd, bwd)
    return f


_mm_nn = _make_bdot(1, 0)
_mm_nt = _make_bdot(1, 1)
_mm_tn = _make_bdot(0, 0)


@jax.custom_vjp
def _round_bf16(x):
    return x.astype(BF16).astype(F32)


_round_bf16.defvjp(lambda x: (_round_bf16(x), None), lambda _, g: (g,))


def _exact_dot(a, b):
    return jnp.dot(a, b, precision=lax.Precision.HIGHEST, preferred_element_type=F32)


def _layernorm_rows(x):
    mu = jnp.mean(x, -1, keepdims=True)
    var = jnp.mean(jnp.square(x - mu), -1, keepdims=True)
    return (x - mu) * lax.rsqrt(var + LN_EPS)


def _rowwise(name, f, rows, params, out_widths):
    rows, params = tuple(rows), tuple(params)
    nr, npar, no = len(rows), len(params), len(out_widths)
    n_rows = rows[0].shape[0]
    row_w = [r.shape[1] for r in rows]
    par_bytes = sum(int(np.prod(p.shape)) * 4 for p in params)
    tile = _row_tile(n_rows, 4 * (2 * sum(row_w) + 2 * sum(out_widths)))
    grid = (n_rows // tile,)

    def row_spec(w):
        return pl.BlockSpec((tile, w), lambda i: (i, 0))

    def par_spec(p):
        return pl.BlockSpec(p.shape, lambda i: (0, 0))

    vmem = 2 * tile * 4 * (2 * sum(row_w) + 3 * sum(out_widths)) + 4 * par_bytes

    def fwd_call(rows, params):
        def kern(*refs):
            outs = f(*[r[...] for r in refs[:nr + npar]])
            for r, o in zip(refs[nr + npar:], outs):
                r[...] = o

        return pl.pallas_call(
            kern, grid=grid, name=name + "_fwd",
            in_specs=[row_spec(w) for w in row_w] + [par_spec(p) for p in params],
            out_specs=[row_spec(w) for w in out_widths],
            out_shape=[jax.ShapeDtypeStruct((n_rows, w), F32) for w in out_widths],
            compiler_params=_params(vmem),
        )(*rows, *params)

    def bwd_call(rows, params, gouts):
        def kern(*refs):
            ins = [r[...] for r in refs[:nr + npar]]
            gs = tuple(r[...] for r in refs[nr + npar:nr + npar + no])
            grads = jax.vjp(f, *ins)[1](gs)
            drefs = refs[nr + npar + no:]
            for k in range(nr):
                drefs[k][...] = grads[k]

            @pl.when(pl.program_id(0) == 0)
            def _():
                for k in range(npar):
                    drefs[nr + k][...] = jnp.zeros_like(drefs[nr + k])

            for k in range(npar):
                drefs[nr + k][...] += grads[nr + k]

        res = pl.pallas_call(
            kern, grid=grid, name=name + "_bwd",
            in_specs=[row_spec(w) for w in row_w] + [par_spec(p) for p in params] + [row_spec(w) for w in out_widths],
            out_specs=[row_spec(w) for w in row_w] + [par_spec(p) for p in params],
            out_shape=[jax.ShapeDtypeStruct(r.shape, F32) for r in rows] + [jax.ShapeDtypeStruct(p.shape, F32) for p in params],
            compiler_params=_params(vmem),
        )(*rows, *params, *gouts)
        return tuple(res[:nr]), tuple(res[nr:])

    @jax.custom_vjp
    def op(rows, params):
        return tuple(fwd_call(rows, params))

    op.defvjp(lambda r, p: (tuple(fwd_call(r, p)), (r, p)), lambda res, g: bwd_call(res[0], res[1], g))
    return op(rows, params)


def _group_ranges(widths, tn):
    starts, s = [], 0
    for w in widths:
        assert w % tn == 0, (w, tn)
        starts.append((s // tn, (s + w) // tn))
        s += w
    return starts, s // tn


def _group_tile(refs, ranges, row_tile, col_tile, i, j):
    out = []
    for ref, (s, e) in zip(refs, ranges):
        cols = pl.ds(pl.multiple_of((j - s) * col_tile, col_tile), col_tile)
        out.append(((j >= s) & (j < e), ref, cols))
    return [(p, lambda r=r, c=c: r.at[pl.ds(pl.multiple_of(i * row_tile, row_tile), row_tile), c]) for p, r, c in out]


def _linear(name, a, w, widths=None):
    single = widths is None
    widths = (w.shape[1],) if single else tuple(widths)
    m, k = a.shape
    n = w.shape[1]
    ng = len(widths)
    tn = 128 if n < 256 else (256 if k > 2048 or n % 512 else 512)
    tm = _row_tile(m, 0, cap=1024 if k <= 2048 else 512)
    ranges, nt = _group_ranges(widths, tn)
    mt = m // tm
    tn_w = 1024 if (k <= 1024 and all(wd % 1024 == 0 for wd in widths)) else tn
    tm_w = _row_tile(m, 0, cap=512)
    ranges_w, nt_w = _group_ranges(widths, tn_w)
    mt_w = m // tm_w
    hbm = pl.BlockSpec(memory_space=pl.ANY)

    def fwd_call(a, w):
        n_steps = mt * nt

        def kern(a_ref, w_ref, *rest):
            outs, a_bf, obuf, osem = rest[:ng], rest[ng], rest[ng + 1], rest[ng + 2]
            i, j = pl.program_id(0), pl.program_id(1)
            step = i * nt + j
            slot = lax.rem(step, 2)

            def drain(sl):
                pltpu.make_async_copy(obuf.at[sl], outs[0].at[pl.ds(0, tm), pl.ds(0, tn)], osem.at[sl]).wait()

            @pl.when(j == 0)
            def _():
                a_bf[...] = a_ref[...].astype(BF16)

            @pl.when(step >= 2)
            def _():
                drain(slot)

            obuf[slot] = jnp.dot(a_bf[...], w_ref[...], preferred_element_type=F32)
            for pred, window in _group_tile(outs, ranges, tm, tn, i, j):
                @pl.when(pred)
                def _(window=window):
                    pltpu.make_async_copy(obuf.at[slot], window(), osem.at[slot]).start()

            @pl.when(step == n_steps - 1)
            def _():
                drain(slot)
                if n_steps > 1:
                    drain(1 - slot)

        return pl.pallas_call(
            kern, grid=(mt, nt), name=name + "_fwd",
            in_specs=[pl.BlockSpec((tm, k), lambda i, j: (i, 0)), pl.BlockSpec((k, tn), lambda i, j: (0, j))],
            out_specs=[hbm] * ng,
            out_shape=[jax.ShapeDtypeStruct((m, wd), F32) for wd in widths],
            scratch_shapes=[pltpu.VMEM((tm, k), BF16), pltpu.VMEM((2, tm, tn), F32), pltpu.SemaphoreType.DMA((2,))],
            compiler_params=_params(10 * tm * k + 4 * k * tn + 8 * tm * tn),
        )(a, w.astype(BF16))

    def prefetched(gs, gbuf, gsem, rngs, row_tile, col_tile, step, n_steps, tile_of):
        slot = lax.rem(step, 2)

        def start(s_idx, sl):
            ii, jj = tile_of(s_idx)
            for pred, window in _group_tile(gs, rngs, row_tile, col_tile, ii, jj):
                @pl.when(pred)
                def _(window=window):
                    pltpu.make_async_copy(window(), gbuf.at[sl], gsem.at[sl]).start()

        @pl.when(step == 0)
        def _():
            start(step, slot)

        @pl.when(step + 1 < n_steps)
        def _():
            start(step + 1, 1 - slot)

        pltpu.make_async_copy(gs[0].at[pl.ds(0, row_tile), pl.ds(0, col_tile)], gbuf.at[slot], gsem.at[slot]).wait()
        return slot

    def dgrad_call(w, gouts):
        def kern(w_ref, *rest):
            gs, da, gbuf, gsem = rest[:ng], rest[ng], rest[ng + 1], rest[ng + 2]
            i, j = pl.program_id(0), pl.program_id(1)
            slot = prefetched(gs, gbuf, gsem, ranges, tm, tn, i * nt + j, mt * nt, lambda s: (s // nt, lax.rem(s, nt)))

            @pl.when(j == 0)
            def _():
                da[...] = jnp.zeros_like(da)

            da[...] += lax.dot_general(gbuf[slot].astype(BF16), w_ref[...], (((1,), (1,)), ((), ())), preferred_element_type=F32)

        return pl.pallas_call(
            kern, grid=(mt, nt), name=name + "_dgrad",
            in_specs=[pl.BlockSpec((k, tn), lambda i, j: (0, j))] + [hbm] * ng,
            out_specs=pl.BlockSpec((tm, k), lambda i, j: (i, 0)),
            out_shape=jax.ShapeDtypeStruct((m, k), F32),
            scratch_shapes=[pltpu.VMEM((2, tm, tn), F32), pltpu.SemaphoreType.DMA((2,))],
            compiler_params=_params(12 * tm * k + 4 * k * tn + 10 * tm * tn),
        )(w.astype(BF16), *gouts)

    def wgrad_call(a, gouts):
        def kern(a_ref, *rest):
            gs, dw, gbuf, gsem = rest[:ng], rest[ng], rest[ng + 1], rest[ng + 2]
            j, i = pl.program_id(0), pl.program_id(1)
            slot = prefetched(gs, gbuf, gsem, ranges_w, tm_w, tn_w, j * mt_w + i, mt_w * nt_w,
                              lambda s: (lax.rem(s, mt_w), s // mt_w))

            @pl.when(i == 0)
            def _():
                dw[...] = jnp.zeros_like(dw)

            dw[...] += lax.dot_general(a_ref[...].astype(BF16), gbuf[slot].astype(BF16), (((0,), (0,)), ((), ())),
                                       preferred_element_type=F32)

        return pl.pallas_call(
            kern, grid=(nt_w, mt_w), name=name + "_wgrad",
            in_specs=[pl.BlockSpec((tm_w, k), lambda j, i: (i, 0))] + [hbm] * ng,
            out_specs=pl.BlockSpec((k, tn_w), lambda j, i: (0, j)),
            out_shape=jax.ShapeDtypeStruct((k, n), F32),
            scratch_shapes=[pltpu.VMEM((2, tm_w, tn_w), F32), pltpu.SemaphoreType.DMA((2,))],
            compiler_params=_params(12 * tm_w * k + 12 * k * tn_w + 10 * tm_w * tn_w),
        )(a, *gouts)

    @jax.custom_vjp
    def op(a, w):
        return tuple(fwd_call(a, w))

    op.defvjp(lambda a, w: (tuple(fwd_call(a, w)), (a, w)),
              lambda res, g: (dgrad_call(res[1], g), wgrad_call(res[0], g)))
    out = op(a, w)
    return out[0] if single else out


def _conv_taps(x_ext, w, n_ext):
    xm2 = pltpu.roll(x_ext, 2, 0)
    xm1 = pltpu.roll(x_ext, 1, 0)
    xp1 = pltpu.roll(x_ext, n_ext - 1, 0)
    return xm2, xm1, xp1


def _dwconv(name, x, w, b, act):
    n_rows, ch = x.shape
    tt = _row_tile(n_rows, 4 * 6 * ch, cap=256)
    nt = n_rows // tt
    n_ext = tt + 2 * SUBLANES
    per8 = tt // SUBLANES
    last8 = n_rows // SUBLANES - 1
    main = pl.BlockSpec((tt, ch), lambda i: (i, 0))
    prev = pl.BlockSpec((SUBLANES, ch), lambda i: (jnp.maximum(i * per8 - 1, 0), 0))
    nxt = pl.BlockSpec((SUBLANES, ch), lambda i: (jnp.minimum((i + 1) * per8, last8), 0))
    wspec = pl.BlockSpec((4, ch), lambda i: (0, 0))
    bspec = pl.BlockSpec((1, ch), lambda i: (0, 0))
    vmem = 4 * n_ext * ch * 14

    def ext(main_ref, prev_ref, next_ref):
        i = pl.program_id(0)
        p = jnp.where(i > 0, prev_ref[...], 0.0)
        q = jnp.where(i < nt - 1, next_ref[...], 0.0)
        return jnp.concatenate([p, main_ref[...], q], axis=0)

    def pre_of(x_ext, wv, bv):
        xm2, xm1, xp1 = _conv_taps(x_ext, wv, n_ext)
        pre = wv[0:1] * xm2 + wv[1:2] * xm1 + wv[2:3] * x_ext + wv[3:4] * xp1 + bv
        return pre, (xm2, xm1, xp1)

    def fwd_call(x, w, b):
        def kern(xm, xp, xn, w_ref, b_ref, o_ref):
            pre, _ = pre_of(ext(xm, xp, xn), w_ref[...], b_ref[...])
            pre = pre[SUBLANES:SUBLANES + tt]
            o_ref[...] = pre * jax.nn.sigmoid(pre) if act else pre

        return pl.pallas_call(
            kern, grid=(nt,), name=name + "_fwd", in_specs=[main, prev, nxt, wspec, bspec], out_specs=main,
            out_shape=jax.ShapeDtypeStruct((n_rows, ch), F32), compiler_params=_params(vmem),
        )(x, x, x, w, b)

    def bwd_call(x, w, b, dy):
        def kern(xm, xp, xn, gm, gp, gn, w_ref, b_ref, dx_ref, dw_ref, db_ref):
            wv = w_ref[...]
            x_ext = ext(xm, xp, xn)
            pre, (xm2, xm1, xp1) = pre_of(x_ext, wv, b_ref[...])
            dpre = ext(gm, gp, gn)
            if act:
                sg = jax.nn.sigmoid(pre)
                dpre = dpre * (sg + pre * sg * (1.0 - sg))
            dx = (wv[0:1] * pltpu.roll(dpre, n_ext - 2, 0) + wv[1:2] * pltpu.roll(dpre, n_ext - 1, 0)
                  + wv[2:3] * dpre + wv[3:4] * pltpu.roll(dpre, 1, 0))
            sl = slice(SUBLANES, SUBLANES + tt)
            dx_ref[...] = dx[sl]
            dm = dpre[sl]

            @pl.when(pl.program_id(0) == 0)
            def _():
                dw_ref[...] = jnp.zeros_like(dw_ref)
                db_ref[...] = jnp.zeros_like(db_ref)

            dw_ref[...] += jnp.concatenate(
                [jnp.sum(dm * t[sl], axis=0, keepdims=True) for t in (xm2, xm1, x_ext, xp1)], axis=0)
            db_ref[...] += jnp.sum(dm, axis=0, keepdims=True)

        return pl.pallas_call(
            kern, grid=(nt,), name=name + "_bwd", in_specs=[main, prev, nxt, main, prev, nxt, wspec, bspec],
            out_specs=[main, wspec, bspec],
            out_shape=[jax.ShapeDtypeStruct((n_rows, ch), F32), jax.ShapeDtypeStruct((4, ch), F32),
                       jax.ShapeDtypeStruct((1, ch), F32)],
            compiler_params=_params(vmem),
        )(x, x, x, dy, dy, dy, w, b)

    @jax.custom_vjp
    def op(x, w, b):
        return fwd_call(x, w, b)

    op.defvjp(lambda x, w, b: (fwd_call(x, w, b), (x, w, b)), lambda res, g: tuple(bwd_call(*res, g)))
    return op(x, w, b)


def _scan_groups(tt, ch, reverse, load, store, carry_ref):
    row = lax.broadcasted_iota(jnp.int32, (SUBLANES, ch), 0)
    ng = tt // SUBLANES

    def body(k, carry):
        g = (ng - 1 - k) if reverse else k
        sl = pl.ds(pl.multiple_of(g * SUBLANES, SUBLANES), SUBLANES)
        a, b, extra = load(sl)
        for s in (1, 2, 4):
            if reverse:
                a_sh, b_sh, valid = pltpu.roll(a, SUBLANES - s, 0), pltpu.roll(b, SUBLANES - s, 0), row < SUBLANES - s
            else:
                a_sh, b_sh, valid = pltpu.roll(a, s, 0), pltpu.roll(b, s, 0), row >= s
            b = jnp.where(valid, b + a * b_sh, b)
            a = jnp.where(valid, a * a_sh, a)
        h = b + a * carry
        if reverse:
            h_prev = jnp.where(row == SUBLANES - 1, carry, pltpu.roll(h, SUBLANES - 1, 0))
            last = h[0:1]
        else:
            h_prev = jnp.where(row == 0, carry, pltpu.roll(h, 1, 0))
            last = h[SUBLANES - 1:SUBLANES]
        store(sl, h, h_prev, extra)
        return jnp.broadcast_to(last, (SUBLANES, ch))

    carry_ref[...] = lax.fori_loop(0, ng, body, carry_ref[...])


def _lin_scan(name, a, b, h0, reverse):
    n_rows, ch = a.shape
    tt = _row_tile(n_rows, 0, cap=256)
    nt = n_rows // tt
    vmem = 2 * 4 * tt * ch * 5

    def tile_spec(rev):
        return pl.BlockSpec((tt, ch), (lambda i: (nt - 1 - i, 0)) if rev else (lambda i: (i, 0)))

    vec = pl.BlockSpec((1, ch), lambda i: (0, 0))

    def fwd_call(a, b, h0):
        def kern(a_ref, b_ref, h0_ref, h_ref, hp_ref, last_ref, carry):
            @pl.when(pl.program_id(0) == 0)
            def _():
                carry[...] = jnp.broadcast_to(h0_ref[...], carry.shape)

            def load(sl):
                return a_ref[sl, :], b_ref[sl, :], None

            def store(sl, h, h_prev, _):
                h_ref[sl, :] = h
                hp_ref[sl, :] = h_prev

            _scan_groups(tt, ch, reverse, load, store, carry)
            last_ref[...] = carry[0:1]

        return pl.pallas_call(
            kern, grid=(nt,), name=name + "_fwd", in_specs=[tile_spec(reverse), tile_spec(reverse), vec],
            out_specs=[tile_spec(reverse), tile_spec(reverse), vec],
            out_shape=[jax.ShapeDtypeStruct((n_rows, ch), F32)] * 2 + [jax.ShapeDtypeStruct((1, ch), F32)],
            scratch_shapes=[pltpu.VMEM((SUBLANES, ch), F32)], compiler_params=_params(vmem),
        )(a, b, h0)

    def bwd_call(a, h_prev, dh, dlast):
        rev = not reverse

        def kern(a_ref, hp_ref, dh_ref, dl_ref, da_ref, db_ref, d0_ref, carry):
            @pl.when(pl.program_id(0) == 0)
            def _():
                carry[...] = jnp.broadcast_to(dl_ref[...], carry.shape)

            def load(sl):
                av, dv = a_ref[sl, :], dh_ref[sl, :]
                return av, av * dv, dv

            def store(sl, u, u_next, dv):
                g = dv + u_next
                db_ref[sl, :] = g
                da_ref[sl, :] = g * hp_ref[sl, :]

            _scan_groups(tt, ch, rev, load, store, carry)
            d0_ref[...] = carry[0:1]

        return pl.pallas_call(
            kern, grid=(nt,), name=name + "_bwd", in_specs=[tile_spec(rev)] * 3 + [vec],
            out_specs=[tile_spec(rev), tile_spec(rev), vec],
            out_shape=[jax.ShapeDtypeStruct((n_rows, ch), F32)] * 2 + [jax.ShapeDtypeStruct((1, ch), F32)],
            scratch_shapes=[pltpu.VMEM((SUBLANES, ch), F32)], compiler_params=_params(vmem),
        )(a, h_prev, dh, dlast)

    @jax.custom_vjp
    def op(a, b, h0):
        h, _, last = fwd_call(a, b, h0)
        return h, last

    def op_fwd(a, b, h0):
        h, h_prev, last = fwd_call(a, b, h0)
        return (h, last), (a, h_prev)

    def op_bwd(res, g):
        da, db, d0 = bwd_call(res[0], res[1], g[0], g[1])
        return da, db, d0

    op.defvjp(op_fwd, op_bwd)
    return op(a, b, h0)


def _tri(reverse):
    q = lax.broadcasted_iota(jnp.int32, (CHUNK, CHUNK), 0)
    s = lax.broadcasted_iota(jnp.int32, (CHUNK, CHUNK), 1)
    return (q <= s) if reverse else (q >= s)


def _pick_col(x, lane):
    idx = lax.broadcasted_iota(jnp.int32, x.shape, 1)
    return jnp.sum(jnp.where(idx == lane, x, 0.0), axis=1, keepdims=True)


def _pick_row(x, row):
    idx = lax.broadcasted_iota(jnp.int32, x.shape, 0)
    return jnp.sum(jnp.where(idx == row, x, 0.0), axis=0, keepdims=True)


def _ssd_shared(small, bias_row, alog_row, reverse):
    delta_all = jax.nn.softplus(small + bias_row)
    acs_all = _exact_dot(_tri(reverse).astype(F32), delta_all * (-jnp.exp(alog_row)))
    return delta_all, acs_all, acs_all.T


def _ssd_group(xs, bm, cm, state, delta_all, acs_all, acs_t, g, direction, reverse):
    mask = _tri(reverse)
    last = 0 if reverse else CHUNK - 1
    cb = _mm_nt(cm, bm)
    rowi = lax.broadcasted_iota(jnp.int32, (CHUNK, 1), 0)
    ys, new_states = [], []
    for r in range(SSD_HPG):
        lane = _DT_LANE + 32 * direction + SSD_HPG * g + r
        delta = _pick_col(delta_all, lane)
        a_col = _pick_col(acs_all, lane)
        a_row = _pick_row(acs_t, lane)
        x_r = xs[:, r * SSD_HEADDIM:(r + 1) * SSD_HEADDIM] * delta
        lmat = jnp.exp(jnp.where(mask, a_col - a_row, -jnp.inf))
        y_diag = _mm_nn(cb * lmat, x_r)
        tot = jnp.sum(jnp.where(rowi == last, a_col, 0.0), axis=0, keepdims=True)
        st = _mm_tn(x_r * jnp.exp(tot - a_col), bm)
        s_r = state[r * SSD_HEADDIM:(r + 1) * SSD_HEADDIM, :]
        y_off = _mm_nt(cm, s_r) * jnp.exp(a_col)
        ys.append(y_diag + y_off)
        new_states.append(jnp.exp(tot) * s_r + st)
    return jnp.concatenate(ys, axis=1), jnp.concatenate(new_states, axis=0)


def _ssd_scan(name, xs, bm, cm, small, bias_row, alog_row, s0, direction, reverse):
    n_rows = xs.shape[0]
    nc = n_rows // CHUNK
    gw = SSD_HPG * SSD_HEADDIM
    vmem = 4 * CHUNK * (gw + 3 * 128) * 8 + 4 * gw * 128 * 12 + (8 << 20)

    n_state = SSD_GROUPS * gw
    shared_scratch = [pltpu.VMEM((CHUNK, LANES), F32), pltpu.VMEM((CHUNK, LANES), F32), pltpu.VMEM((LANES, CHUNK), F32)]

    def specs(order, gps=1):
        def cidx(c):
            return (nc - 1 - c) if order else c

        return dict(
            xs=pl.BlockSpec((CHUNK, gps * gw), lambda c, g: (cidx(c), g)),
            bc=pl.BlockSpec((CHUNK, gps * SSD_STATE), lambda c, g: (cidx(c), g)),
            small=pl.BlockSpec((CHUNK, LANES), lambda c, g: (cidx(c), 0)),
            row=pl.BlockSpec((1, LANES), lambda c, g: (0, 0)),
            state=pl.BlockSpec((n_state, SSD_STATE), lambda c, g: (0, 0)),
            enter=pl.BlockSpec((1, gps * gw, SSD_STATE), lambda c, g: (cidx(c), g, 0)),
        )

    def group_rows(g, gps=1):
        return pl.ds(pl.multiple_of(g * gps * gw, gps * gw), gps * gw)

    def fwd_call(xs, bm, cm, small, bias_row, alog_row, s0):
        gps = 2
        sp = specs(reverse, gps)

        def kern(xs_r, bm_r, cm_r, sm_r, br_r, ar_r, s0_r, y_r, sf_r, se_r, st, sh_d, sh_a, sh_t):
            c, g = pl.program_id(0), pl.program_id(1)

            @pl.when((c == 0) & (g == 0))
            def _():
                st[...] = s0_r[...]

            @pl.when(g == 0)
            def _():
                sh_d[...], sh_a[...], sh_t[...] = _ssd_shared(sm_r[...], br_r[...], ar_r[...], reverse)

            rows = group_rows(g, gps)
            s_in = st[rows, :]
            se_r[0] = s_in
            xs_v, bm_v, cm_v = xs_r[...], bm_r[...], cm_r[...]
            ys, s_new = [], []
            for u in range(gps):
                y_u, s_u = _ssd_group(xs_v[:, u * gw:(u + 1) * gw], bm_v[:, u * SSD_STATE:(u + 1) * SSD_STATE],
                                      cm_v[:, u * SSD_STATE:(u + 1) * SSD_STATE], s_in[u * gw:(u + 1) * gw],
                                      sh_d[...], sh_a[...], sh_t[...], gps * g + u, direction, reverse)
                ys.append(y_u)
                s_new.append(s_u)
            s_new = jnp.concatenate(s_new, axis=0)
            y_r[...] = jnp.concatenate(ys, axis=1)
            st[rows, :] = s_new
            sf_r[rows, :] = s_new

        return pl.pallas_call(
            kern, grid=(nc, SSD_GROUPS // gps), name=name + "_fwd",
            in_specs=[sp['xs'], sp['bc'], sp['bc'], sp['small'], sp['row'], sp['row'], sp['state']],
            out_specs=[sp['xs'], sp['state'], sp['enter']],
            out_shape=[jax.ShapeDtypeStruct((n_rows, SSD_INNER), F32), jax.ShapeDtypeStruct((n_state, SSD_STATE), F32),
                       jax.ShapeDtypeStruct((nc, n_state, SSD_STATE), F32)],
            scratch_shapes=[pltpu.VMEM((n_state, SSD_STATE), F32)] + shared_scratch, compiler_params=_params(vmem),
        )(xs, bm, cm, small, bias_row, alog_row, s0)

    def bwd_call(xs, bm, cm, small, bias_row, alog_row, enter, dy, dsf):
        sp = specs(not reverse)

        def kern(xs_r, bm_r, cm_r, sm_r, br_r, ar_r, se_r, dy_r, dsf_r, dxs_r, dbm_r, dcm_r, dsm_r, dbr_r, dar_r, ds0_r,
                 ds, sh_d, sh_a, sh_t, gd, ga, gt):
            c, g = pl.program_id(0), pl.program_id(1)

            @pl.when((c == 0) & (g == 0))
            def _():
                ds[...] = dsf_r[...]
                dbr_r[...] = jnp.zeros_like(dbr_r)
                dar_r[...] = jnp.zeros_like(dar_r)

            @pl.when(g == 0)
            def _():
                sh_d[...], sh_a[...], sh_t[...] = _ssd_shared(sm_r[...], br_r[...], ar_r[...], reverse)
                gd[...] = jnp.zeros_like(gd)
                ga[...] = jnp.zeros_like(ga)
                gt[...] = jnp.zeros_like(gt)

            rows = group_rows(g)
            fn = functools.partial(_ssd_group, g=g, direction=direction, reverse=reverse)
            _, vjp = jax.vjp(fn, xs_r[...], bm_r[...], cm_r[...], se_r[0], sh_d[...], sh_a[...], sh_t[...])
            dxs, dbm, dcm, ds_in, dd, da, dt = vjp((dy_r[...], ds[rows, :]))
            dxs_r[...] = dxs
            dbm_r[...] = dbm
            dcm_r[...] = dcm
            ds[rows, :] = ds_in
            ds0_r[rows, :] = ds_in
            gd[...] += dd
            ga[...] += da
            gt[...] += dt

            @pl.when(g == SSD_GROUPS - 1)
            def _():
                shared = functools.partial(_ssd_shared, reverse=reverse)
                dsm, dbr, dar = jax.vjp(shared, sm_r[...], br_r[...], ar_r[...])[1]((gd[...], ga[...], gt[...]))
                dsm_r[...] = dsm
                dbr_r[...] += dbr
                dar_r[...] += dar

        return pl.pallas_call(
            kern, grid=(nc, SSD_GROUPS), name=name + "_bwd",
            in_specs=[sp['xs'], sp['bc'], sp['bc'], sp['small'], sp['row'], sp['row'], sp['enter'], sp['xs'], sp['state']],
            out_specs=[sp['xs'], sp['bc'], sp['bc'], sp['small'], sp['row'], sp['row'], sp['state']],
            out_shape=[jax.ShapeDtypeStruct(xs.shape, F32), jax.ShapeDtypeStruct(bm.shape, F32),
                       jax.ShapeDtypeStruct(cm.shape, F32), jax.ShapeDtypeStruct((n_rows, LANES), F32),
                       jax.ShapeDtypeStruct((1, LANES), F32), jax.ShapeDtypeStruct((1, LANES), F32),
                       jax.ShapeDtypeStruct(s0.shape, F32)],
            scratch_shapes=[pltpu.VMEM((n_state, SSD_STATE), F32)] + shared_scratch + shared_scratch,
            compiler_params=_params(vmem),
        )(xs, bm, cm, small, bias_row, alog_row, enter, dy, dsf)

    @jax.custom_vjp
    def op(*args):
        y, sf, _ = fwd_call(*args)
        return y, sf

    def op_fwd(*args):
        y, sf, enter = fwd_call(*args)
        return (y, sf), (args[:6], enter)

    op.defvjp(op_fwd, lambda res, g: tuple(bwd_call(*res[0], res[1], g[0], g[1])))
    return op(xs, bm, cm, small, bias_row, alog_row, s0)


def _ml_shared(small, gate_row, reverse):
    gates = small + gate_row
    b_all = _exact_dot(_tri(reverse).astype(F32), jax.nn.log_sigmoid(gates))
    return gates, b_all, gates.T, b_all.T


def _ml_head(q, k, v, c_st, n_st, m_st, gates, b_all, gates_t, b_t, h, direction, reverse):
    mask = _tri(reverse)
    last = 0 if reverse else CHUNK - 1
    lane_i = _MG_LANE + 8 * direction + h
    lane_f = lane_i + ML_HEADS
    b_col = _pick_col(b_all, lane_f)
    b_row = _pick_row(b_t, lane_f)
    li_col = _pick_col(gates, lane_i)
    li_row = _pick_row(gates_t, lane_i)
    rowi = lax.broadcasted_iota(jnp.int32, (CHUNK, 1), 0)
    g_tot = jnp.sum(jnp.where(rowi == last, b_col, 0.0), axis=0, keepdims=True)
    m_in = m_st[:, 0:1]
    q = q * (ML_HD ** -0.5)
    w = g_tot - b_col + li_col
    m_loc = jnp.max(w, axis=0, keepdims=True)
    kw = k * jnp.exp(w - m_loc)
    c_loc = _mm_tn(kw, v)
    n_loc = jnp.sum(kw, axis=0, keepdims=True)
    m_new = jnp.maximum(g_tot + m_in, m_loc)
    s_old = jnp.exp(g_tot + m_in - m_new)
    s_loc = jnp.exp(m_loc - m_new)
    c_new = s_old * c_st + s_loc * c_loc
    n_new = s_old * n_st + s_loc * n_loc
    log_d = jnp.where(mask, b_col - b_row + li_row, -jnp.inf)
    inter = b_col + m_in
    m_t = jnp.maximum(inter, jnp.max(log_d, axis=1, keepdims=True))
    dmat = jnp.exp(log_d - m_t)
    wi = jnp.exp(inter - m_t)
    s = _mm_nt(q, k) * dmat
    num = _mm_nn(s, v) + wi * _mm_nn(q, c_st)
    den = jnp.sum(s, axis=1, keepdims=True) + wi * jnp.sum(_round_bf16(q) * _round_bf16(n_st), axis=1, keepdims=True)
    out = num / jnp.maximum(jnp.abs(den), jnp.exp(-m_t))
    return out, c_new, n_new, jnp.broadcast_to(m_new, (1, LANES))


def _ml_scan(name, q, k, v, small, gate_row, c0, n0, m0, direction, reverse):
    n_rows = q.shape[0]
    nc = n_rows // CHUNK
    vmem = 4 * CHUNK * (4 * ML_HD + 128) * 8 + 4 * ML_HD * ML_HD * 12 + (8 << 20)

    def specs(order):
        def cidx(c):
            return (nc - 1 - c) if order else c

        return dict(
            qkv=pl.BlockSpec((CHUNK, ML_HD), lambda c, h: (cidx(c), h)),
            small=pl.BlockSpec((CHUNK, LANES), lambda c, h: (cidx(c), 0)),
            row=pl.BlockSpec((1, LANES), lambda c, h: (0, 0)),
            c=pl.BlockSpec((ML_HEADS * ML_HD, ML_HD), lambda c, h: (0, 0)),
            n=pl.BlockSpec((ML_HEADS, 1, ML_HD), lambda c, h: (0, 0, 0)),
            m=pl.BlockSpec((ML_HEADS, 1, LANES), lambda c, h: (0, 0, 0)),
            ec=pl.BlockSpec((1, ML_HD, ML_HD), lambda c, h: (cidx(c), h, 0)),
            en=pl.BlockSpec((1, 1, 1, ML_HD), lambda c, h: (cidx(c), h, 0, 0)),
            em=pl.BlockSpec((1, 1, 1, LANES), lambda c, h: (cidx(c), h, 0, 0)),
        )

    st_shapes = [jax.ShapeDtypeStruct((ML_HEADS * ML_HD, ML_HD), F32), jax.ShapeDtypeStruct((ML_HEADS, 1, ML_HD), F32),
                 jax.ShapeDtypeStruct((ML_HEADS, 1, LANES), F32)]
    scratch = [pltpu.VMEM((ML_HEADS * ML_HD, ML_HD), F32), pltpu.VMEM((ML_HEADS, 1, ML_HD), F32),
               pltpu.VMEM((ML_HEADS, 1, LANES), F32)]
    shared_scratch = [pltpu.VMEM((CHUNK, LANES), F32), pltpu.VMEM((CHUNK, LANES), F32),
                      pltpu.VMEM((LANES, CHUNK), F32), pltpu.VMEM((LANES, CHUNK), F32)]

    def head_rows(h):
        return pl.ds(pl.multiple_of(h * ML_HD, ML_HD), ML_HD)

    def fwd_call(q, k, v, small, gate_row, c0, n0, m0):
        sp = specs(reverse)

        def kern(q_r, k_r, v_r, sm_r, gr_r, c0_r, n0_r, m0_r, o_r, cf_r, nf_r, mf_r, ec_r, en_r, em_r, cs, ns, ms, *sh):
            c, h = pl.program_id(0), pl.program_id(1)

            @pl.when((c == 0) & (h == 0))
            def _():
                cs[...] = c0_r[...]
                ns[...] = n0_r[...]
                ms[...] = m0_r[...]

            @pl.when(h == 0)
            def _():
                for ref, val in zip(sh, _ml_shared(sm_r[...], gr_r[...], reverse)):
                    ref[...] = val

            rows = head_rows(h)
            c_in, n_in, m_in = cs[rows, :], ns[h], ms[h]
            ec_r[0] = c_in
            en_r[0, 0] = n_in
            em_r[0, 0] = m_in
            out, c_new, n_new, m_new = _ml_head(q_r[...], k_r[...], v_r[...], c_in, n_in, m_in, *[r[...] for r in sh],
                                                h, direction, reverse)
            o_r[...] = out
            cs[rows, :] = c_new
            ns[h] = n_new
            ms[h] = m_new
            cf_r[rows, :] = c_new
            nf_r[h] = n_new
            mf_r[h] = m_new

        return pl.pallas_call(
            kern, grid=(nc, ML_HEADS), name=name + "_fwd",
            in_specs=[sp['qkv']] * 3 + [sp['small'], sp['row'], sp['c'], sp['n'], sp['m']],
            out_specs=[sp['qkv'], sp['c'], sp['n'], sp['m'], sp['ec'], sp['en'], sp['em']],
            out_shape=[jax.ShapeDtypeStruct((n_rows, ML_HEADS * ML_HD), F32)] + st_shapes + [
                jax.ShapeDtypeStruct((nc, ML_HEADS * ML_HD, ML_HD), F32),
                jax.ShapeDtypeStruct((nc, ML_HEADS, 1, ML_HD), F32), jax.ShapeDtypeStruct((nc, ML_HEADS, 1, LANES), F32)],
            scratch_shapes=scratch + shared_scratch, compiler_params=_params(vmem),
        )(q, k, v, small, gate_row, c0, n0, m0)

    def bwd_call(q, k, v, small, gate_row, ec, en, em, do, dcf, dnf, dmf):
        sp = specs(not reverse)
        n_sh = len(shared_scratch)

        def kern(q_r, k_r, v_r, sm_r, gr_r, ec_r, en_r, em_r, do_r, dcf_r, dnf_r, dmf_r,
                 dq_r, dk_r, dv_r, dsm_r, dgr_r, dc0_r, dn0_r, dm0_r, dcs, dns, dms, *rest):
            sh, gsh = rest[:n_sh], rest[n_sh:]
            c, h = pl.program_id(0), pl.program_id(1)

            @pl.when((c == 0) & (h == 0))
            def _():
                dcs[...] = dcf_r[...]
                dns[...] = dnf_r[...]
                dms[...] = dmf_r[...]
                dgr_r[...] = jnp.zeros_like(dgr_r)

            @pl.when(h == 0)
            def _():
                for ref, val in zip(sh, _ml_shared(sm_r[...], gr_r[...], reverse)):
                    ref[...] = val
                for ref in gsh:
                    ref[...] = jnp.zeros_like(ref)

            rows = head_rows(h)
            fn = functools.partial(_ml_head, h=h, direction=direction, reverse=reverse)
            _, vjp = jax.vjp(fn, q_r[...], k_r[...], v_r[...], ec_r[0], en_r[0, 0], em_r[0, 0], *[r[...] for r in sh])
            grads = vjp((do_r[...], dcs[rows, :], dns[h], dms[h]))
            dq, dk, dv, dc, dn, dm = grads[:6]
            dq_r[...] = dq
            dk_r[...] = dk
            dv_r[...] = dv
            for ref, val in zip(gsh, grads[6:]):
                ref[...] += val
            dm = jnp.broadcast_to(jnp.sum(dm, axis=1, keepdims=True), (1, LANES)) * (1.0 / LANES)
            dcs[rows, :] = dc
            dns[h] = dn
            dms[h] = dm
            dc0_r[rows, :] = dc
            dn0_r[h] = dn
            dm0_r[h] = dm

            @pl.when(h == ML_HEADS - 1)
            def _():
                shared = functools.partial(_ml_shared, reverse=reverse)
                dsm, dgr = jax.vjp(shared, sm_r[...], gr_r[...])[1](tuple(r[...] for r in gsh))
                dsm_r[...] = dsm
                dgr_r[...] += dgr

        return pl.pallas_call(
            kern, grid=(nc, ML_HEADS), name=name + "_bwd",
            in_specs=[sp['qkv']] * 3 + [sp['small'], sp['row'], sp['ec'], sp['en'], sp['em'], sp['qkv'], sp['c'], sp['n'], sp['m']],
            out_specs=[sp['qkv']] * 3 + [sp['small'], sp['row'], sp['c'], sp['n'], sp['m']],
            out_shape=[jax.ShapeDtypeStruct(q.shape, F32)] * 3 + [jax.ShapeDtypeStruct((n_rows, LANES), F32),
                                                                  jax.ShapeDtypeStruct((1, LANES), F32)] + st_shapes,
            scratch_shapes=scratch + shared_scratch + shared_scratch, compiler_params=_params(vmem),
        )(q, k, v, small, gate_row, ec, en, em, do, dcf, dnf, dmf)

    @jax.custom_vjp
    def op(*args):
        return tuple(fwd_call(*args)[:4])

    def op_fwd(*args):
        res = fwd_call(*args)
        return tuple(res[:4]), (args[:5], tuple(res[4:]))

    op.defvjp(op_fwd, lambda res, g: tuple(bwd_call(*res[0], *res[1], *g)))
    return op(q, k, v, small, gate_row, c0, n0, m0)


def _f_modulate(x, shift, scale):
    return (_layernorm_rows(x) * (1.0 + scale) + shift,)


def _f_resid_ln(x, o, gate, bias, ln_g, ln_b):
    return (_layernorm_rows(DN_ALPHA * x + gate * (o + bias)) * ln_g + ln_b,)


def _f_lru_gates(xc, w_r, b_r, w_i, b_i, lam):
    outs = []
    for d in range(2):
        def blockdiag(w):
            return jnp.concatenate(
                [_mm_nn(xc[:, n * LRU_BS:(n + 1) * LRU_BS], w[(d * LRU_BLOCKS + n) * LRU_BS:(d * LRU_BLOCKS + n + 1) * LRU_BS, :])
                 for n in range(LRU_BLOCKS)], axis=1)

        r = jax.nn.sigmoid(blockdiag(w_r) + b_r[d:d + 1])
        i = jax.nn.sigmoid(blockdiag(w_i) + b_i[d:d + 1])
        log_a = -LRU_C * r * jax.nn.softplus(-lam[d:d + 1])
        outs += [jnp.exp(log_a), jnp.sqrt(1.0 - jnp.exp(2.0 * log_a)) * i * xc]
    return tuple(outs)


def _f_lru_out(h_f, h_b, ly):
    return ((h_f + h_b) * jax.nn.gelu(ly),)


def _f_ssd_post(y_f, y_b, xs, z, d_exp, norm_w):
    y = (y_f + y_b + xs * d_exp) * jax.nn.silu(z)
    gw = SSD_INNER // SSD_GROUPS
    parts = []
    for g in range(SSD_GROUPS):
        yg = y[:, g * gw:(g + 1) * gw]
        parts.append(yg * lax.rsqrt(jnp.mean(jnp.square(yg), -1, keepdims=True) + LN_EPS))
    return (jnp.concatenate(parts, axis=1) * norm_w,)


def _f_ml_post(h_f, h_b, o, norm_w):
    h = h_f + h_b
    parts = [_layernorm_rows(h[:, i * ML_HD:(i + 1) * ML_HD]) for i in range(ML_HEADS)]
    return (jnp.concatenate(parts, axis=1) * norm_w * jax.nn.sigmoid(o),)


def _f_merge(ga, gb, gc, pa, pb, pc):
    return (jax.nn.sigmoid(ga) * pa + jax.nn.sigmoid(gb) * pb + jax.nn.sigmoid(gc) * pc,)


def _f_relu2(pre, bias):
    return (jnp.square(jax.nn.relu(pre + bias)),)


def _lane_row(vec, start):
    return jnp.pad(vec[None], ((0, 0), (start, LANES - start - vec.shape[0])))


def _mixer(tag, h, p, states):
    (lru_s, ssd_s, ml_s) = states
    lx, ly, sz, xs, bm, cm, mq, mk, mv, mo, ga, gb, gc = _linear(tag + "in", h, p['w_in_main'], _IN_MAIN_WIDTHS)
    small = _linear(tag + "insmall", h, p['w_in_small'])

    xc = _dwconv(tag + "lruconv", lx, p['lru_conv_w'], p['lru_conv_b'][None], False)
    a_f, b_f, a_b, b_b = _rowwise(
        tag + "lrugate", _f_lru_gates, [xc],
        [p['lru_w_r'].reshape(2 * LRU_BLOCKS * LRU_BS, LRU_BS), p['lru_b_r'], p['lru_w_i'].reshape(2 * LRU_BLOCKS * LRU_BS, LRU_BS),
         p['lru_b_i'], p['lru_lambda']], [D_MODEL] * 4)
    h_f, s_f = _lin_scan(tag + "lruscanf", a_f, b_f, lru_s[0], False)
    h_b, s_b = _lin_scan(tag + "lruscanb", a_b, b_b, lru_s[1], True)
    (ya,) = _rowwise(tag + "lruout", _f_lru_out, [h_f, h_b, ly], [], [D_MODEL])

    cw, cb_ = p['ssd_conv_w'], p['ssd_conv_b'][None]
    xs_c = _dwconv(tag + "ssdconvx", xs, cw[:, :2048], cb_[:, :2048], True)
    bm_c = _dwconv(tag + "ssdconvb", bm, cw[:, 2048:3072], cb_[:, 2048:3072], True)
    cm_c = _dwconv(tag + "ssdconvc", cm, cw[:, 3072:], cb_[:, 3072:], True)
    ssd_new, ys = [], []
    for d in range(2):
        y_d, st_d = _ssd_scan(tag + "ssd%d" % d, xs_c, bm_c, cm_c, small, _lane_row(p['ssd_dt_bias'][d], _DT_LANE + 32 * d),
                              _lane_row(p['ssd_a_log'][d], _DT_LANE + 32 * d), ssd_s[d], d, d == 1)
        ys.append(y_d)
        ssd_new.append(st_d)
    (yb,) = _rowwise(tag + "ssdpost", _f_ssd_post, [ys[0], ys[1], xs_c, sz],
                     [jnp.repeat(p['ssd_d'], SSD_HEADDIM)[None], p['ssd_norm_w'][None]], [SSD_INNER])

    mw, mb = p['ml_conv_w'], p['ml_conv_b'][None]
    q_c = _dwconv(tag + "mlconvq", mq, mw[:, :1024], mb[:, :1024], True)
    k_c = _dwconv(tag + "mlconvk", mk, mw[:, 1024:], mb[:, 1024:], True)
    gate_row = _lane_row(p['ml_gate_b'].reshape(4 * ML_HEADS), _MG_LANE)
    ml_new, hs = [], []
    for d in range(2):
        o_d, c_d, n_d, m_d = _ml_scan(tag + "ml%d" % d, q_c, k_c, mv, small, gate_row, *ml_s[d], d, d == 1)
        hs.append(o_d)
        ml_new.append((c_d, n_d, m_d))
    (yc,) = _rowwise(tag + "mlpost", _f_ml_post, [hs[0], hs[1], mo], [p['ml_norm_w'][None]], [D_MODEL])
    return (ya, yb, yc, ga, gb, gc), ((s_f, s_b), tuple(ssd_new), tuple(ml_new))


def _merge(tag, br, p):
    ya, yb, yc, ga, gb, gc = br
    pa = _linear(tag + "bra", ya, p['w_br_a'])
    pb = _linear(tag + "brb", yb, p['w_br_b'])
    pc = _linear(tag + "brc", yc, p['w_br_c'])
    (m,) = _rowwise(tag + "merge", _f_merge, [ga, gb, gc, pa, pb, pc], [], [D_MODEL])
    return _linear(tag + "out", m, p['w_out'])


def _sublayers(tag, xin, o, mods, p, l):
    sh2, sc2, g1, g2 = mods
    (x1,) = _rowwise(tag + "ln1", _f_resid_ln, [xin, o], [g1, p['b_out'][None], p['ln1_g'][None], p['ln1_b'][None]], [D_MODEL])
    (h2,) = _rowwise(tag + "mod2", _f_modulate, [x1], [sh2, sc2], [D_MODEL])
    pre = _linear(tag + "ff1", h2, p['w_ff1'])
    (u2,) = _rowwise(tag + "relu2", _f_relu2, [pre], [p['b_ff1'][None]], [D_FF])
    o2 = _linear(tag + "ff2", u2, p['w_ff2'])
    (x2,) = _rowwise(tag + "ln2", _f_resid_ln, [x1, o2], [g2, p['b_ff2'][None], p['ln2_g'][None], p['ln2_b'][None]], [D_MODEL])
    return x2


def _to_col_major(h):
    s, d = h.shape
    return h.reshape(s // GRID_W, GRID_W, d).swapaxes(0, 1).reshape(s, d)


def _from_col_major(h):
    s, d = h.shape
    return h.reshape(GRID_W, s // GRID_W, d).swapaxes(0, 1).reshape(s, d)


def _forward(x, wts, cvec, ctx):
    zeros = lambda *s: jnp.zeros(s, F32)
    ctx_init = ((zeros(1, D_MODEL), zeros(1, D_MODEL)),
                (zeros(SSD_INNER, SSD_STATE), zeros(SSD_INNER, SSD_STATE)),
                tuple((zeros(ML_HEADS * ML_HD, ML_HD), zeros(ML_HEADS, 1, ML_HD), zeros(ML_HEADS, 1, LANES)) for _ in range(2)))
    for l in range(DEPTH):
        p = {n: (wts[n] if n == 'c_ctx' else wts[n][l]) for n in wts}
        tag = "l%d" % l
        cc = jnp.concatenate([cvec, p['c_ctx'][None], jnp.zeros((SUBLANES - 2, D_MODEL), F32)], axis=0)
        mod = _linear(tag + "ada", jax.nn.silu(cc), p['w_ada']) + p['b_ada'][None]
        sh1x, sc1x, g1x, sh2x, sc2x, g2x = [mod[0:1, i * D_MODEL:(i + 1) * D_MODEL] for i in range(6)]
        sh1c, sc1c, g1c, sh2c, sc2c, g2c = [mod[1:2, i * D_MODEL:(i + 1) * D_MODEL] for i in range(6)]
        (hc,) = _rowwise(tag + "cmod1", _f_modulate, [ctx], [sh1c, sc1c], [D_MODEL])
        br_c, ctx_states = _mixer(tag + "c", hc, p, ctx_init)
        (hx,) = _rowwise(tag + "xmod1", _f_modulate, [x], [sh1x, sc1x], [D_MODEL])
        if l % 2 == 1:
            hx = _to_col_major(hx)
        br_x, _ = _mixer(tag + "x", hx, p, ctx_states)
        ox = _merge(tag + "x", br_x, p)
        if l % 2 == 1:
            ox = _from_col_major(ox)
        x = _sublayers(tag + "x", x, ox, (sh2x, sc2x, g1x, g2x), p, l)
        if l < DEPTH - 1:
            ctx = _sublayers(tag + "c", ctx, _merge(tag + "c", br_c, p), (sh2c, sc2c, g1c, g2c), p, l)
    return x


def _loss_and_cotangent(y, target):
    n_rows, d = y.shape
    tt = _row_tile(n_rows, 0, cap=256)

    def kern(y_ref, t_ref, dy_ref, acc_ref):
        @pl.when(pl.program_id(0) == 0)
        def _():
            acc_ref[...] = jnp.zeros_like(acc_ref)

        err = y_ref[...] - t_ref[...]
        dy_ref[...] = err * (1.0 / d)
        acc_ref[...] += jnp.sum(jnp.square(err))

    spec = pl.BlockSpec((tt, d), lambda i: (i, 0))
    dy, acc = pl.pallas_call(
        kern, grid=(n_rows // tt,), name="loss", in_specs=[spec, spec],
        out_specs=[spec, pl.BlockSpec((SUBLANES, LANES), lambda i: (0, 0))],
        out_shape=[jax.ShapeDtypeStruct((n_rows, d), F32), jax.ShapeDtypeStruct((SUBLANES, LANES), F32)],
    )(y, target)
    return acc[0, 0] * (0.5 / d), dy


def _exchange(name, src, gather):
    slab = src.shape if gather else src.shape[1:]

    def body(src_ref, out_ref, send_sems, recv_sems, local_sem):
        x, y, c = lax.axis_index("x"), lax.axis_index("y"), lax.axis_index("c")
        me = 4 * x + 2 * y + c
        local = pltpu.make_async_copy(src_ref if gather else src_ref.at[me], out_ref.at[me], local_sem)
        local.start()
        copies = []
        for d in range(1, N_DEV):
            px, py, pc = lax.rem(x + (d >> 2), 2), lax.rem(y + ((d >> 1) & 1), 2), lax.rem(c + (d & 1), 2)
            peer = 4 * px + 2 * py + pc
            cp = pltpu.make_async_remote_copy(
                src_ref=src_ref if gather else src_ref.at[peer], dst_ref=out_ref.at[me],
                send_sem=send_sems.at[d - 1], recv_sem=recv_sems.at[d - 1],
                device_id=(px, py, pc), device_id_type=pl.DeviceIdType.MESH)
            cp.start()
            copies.append(cp)
        for cp in copies:
            cp.wait()
        local.wait()

    return pl.pallas_call(
        body, name=name, out_shape=jax.ShapeDtypeStruct((N_DEV,) + tuple(slab), src.dtype),
        in_specs=[pl.BlockSpec(memory_space=pl.ANY)], out_specs=pl.BlockSpec(memory_space=pl.ANY),
        scratch_shapes=[pltpu.SemaphoreType.DMA((N_DEV - 1,)), pltpu.SemaphoreType.DMA((N_DEV - 1,)), pltpu.SemaphoreType.DMA],
    )(src)


_HBM = pl.BlockSpec(memory_space=pl.ANY)
_CHIPS = ((0, 0), (0, 1), (1, 0), (1, 1))


def _gather_two_level(name, src):
    def body(src_ref, out_ref, send_sems, recv_sems, local_sem):
        x, y, c = lax.axis_index("x"), lax.axis_index("y"), lax.axis_index("c")
        me, sibling = (x, y, c), (x, y, 1 - c)
        chips = [(1 - x, y), (x, 1 - y), (1 - x, 1 - y)]

        def slab(px, py, pc):
            return out_ref.at[4 * px + 2 * py + pc]

        def copy(k, block, to, src=None):
            return pltpu.make_async_remote_copy(
                src_ref=slab(*block) if src is None else src, dst_ref=slab(*block), send_sem=send_sems.at[k],
                recv_sem=recv_sems.at[k], device_id=to, device_id_type=pl.DeviceIdType.MESH)

        mine = pltpu.make_async_copy(src_ref, slab(*me), local_sem)
        mine.start()
        first = [copy(0, me, sibling, src=src_ref)] + [copy(1 + j, me, (*chip, c), src=src_ref) for j, chip in enumerate(chips)]
        for cp in first:
            cp.start()
        passed = [copy(4 + j, (*chip, c), sibling) for j, chip in enumerate(chips)]
        for j, chip in enumerate(chips):
            copy(1 + j, (*chip, c), me).wait_recv()
            passed[j].start()
        copy(0, sibling, me).wait_recv()
        for j, chip in enumerate(chips):
            copy(4 + j, (*chip, 1 - c), me).wait_recv()
        for cp in first + passed:
            cp.wait_send()
        mine.wait()

    return pl.pallas_call(
        body, name=name, out_shape=jax.ShapeDtypeStruct((N_DEV,) + tuple(src.shape), src.dtype),
        in_specs=[_HBM], out_specs=_HBM,
        scratch_shapes=[pltpu.SemaphoreType.DMA((N_DEV - 1,)), pltpu.SemaphoreType.DMA((N_DEV - 1,)), pltpu.SemaphoreType.DMA],
    )(src)


def _scatter_to_sibling(name, parts):
    def body(p_ref, out_ref, send_sems, recv_sems):
        x, y, c = lax.axis_index("x"), lax.axis_index("y"), lax.axis_index("c")
        copies = []
        for j, (px, py) in enumerate(_CHIPS):
            cp = pltpu.make_async_remote_copy(
                src_ref=p_ref.at[4 * px + 2 * py + (1 - c)], dst_ref=out_ref.at[j], send_sem=send_sems.at[j],
                recv_sem=recv_sems.at[j], device_id=(x, y, 1 - c), device_id_type=pl.DeviceIdType.MESH)
            cp.start()
            copies.append(cp)
        for cp in copies:
            cp.wait()

    return pl.pallas_call(
        body, name=name, out_shape=jax.ShapeDtypeStruct((4,) + tuple(parts.shape[1:]), parts.dtype),
        in_specs=[_HBM], out_specs=_HBM,
        scratch_shapes=[pltpu.SemaphoreType.DMA((4,)), pltpu.SemaphoreType.DMA((4,))],
    )(parts)


def _chip_sum(name, parts, from_sibling):
    _, rows, cols = parts.shape
    lanes = -(-cols // LANES) * LANES
    tr = _row_tile(rows, 4 * lanes * 4 * 2, budget=12 << 20)

    def kern(p_ref, s_ref, o_ref):
        c = lax.axis_index("c")
        o_ref[0] = jnp.where(c == 0, p_ref[0, 0], p_ref[0, 1]) + s_ref[0]

    return pl.pallas_call(
        kern, grid=(4, rows // tr), name=name,
        in_specs=[pl.BlockSpec((1, 2, tr, cols), lambda j, i: (j, 0, i, 0)), pl.BlockSpec((1, tr, cols), lambda j, i: (j, i, 0))],
        out_specs=pl.BlockSpec((1, tr, cols), lambda j, i: (j, i, 0)),
        out_shape=jax.ShapeDtypeStruct((4, rows, cols), F32),
        compiler_params=_params(4 * lanes * tr * 4 * 2),
    )(parts.reshape(4, 2, rows, cols), from_sibling)


def _scatter_across_chips(name, sums):
    def body(q_ref, out_ref, send_sems, recv_sems, local_sem):
        x, y, c = lax.axis_index("x"), lax.axis_index("y"), lax.axis_index("c")
        own = 2 * x + y
        local = pltpu.make_async_copy(q_ref.at[own], out_ref.at[own], local_sem)
        local.start()
        copies = []
        for d in range(1, 4):
            px, py = lax.rem(x + (d >> 1), 2), lax.rem(y + (d & 1), 2)
            cp = pltpu.make_async_remote_copy(
                src_ref=q_ref.at[2 * px + py], dst_ref=out_ref.at[own], send_sem=send_sems.at[d - 1],
                recv_sem=recv_sems.at[d - 1], device_id=(px, py, c), device_id_type=pl.DeviceIdType.MESH)
            cp.start()
            copies.append(cp)
        for cp in copies:
            cp.wait()
        local.wait()

    return pl.pallas_call(
        body, name=name, out_shape=jax.ShapeDtypeStruct(sums.shape, sums.dtype), in_specs=[_HBM], out_specs=_HBM,
        scratch_shapes=[pltpu.SemaphoreType.DMA((3,)), pltpu.SemaphoreType.DMA((3,)), pltpu.SemaphoreType.DMA],
    )(sums)


def _sum_parts(name, parts):
    n_parts, rows, cols = parts.shape
    tr = _row_tile(rows, 4 * cols * (n_parts + 1) * 2)

    def kern(p_ref, o_ref):
        acc = p_ref[0]
        for k in range(1, n_parts):
            acc = acc + p_ref[k]
        o_ref[...] = acc

    return pl.pallas_call(
        kern, grid=(rows // tr,), name=name, in_specs=[pl.BlockSpec((n_parts, tr, cols), lambda i: (0, i, 0))],
        out_specs=pl.BlockSpec((tr, cols), lambda i: (i, 0)), out_shape=jax.ShapeDtypeStruct((rows, cols), F32),
    )(parts)


def _adamw(name, w, m, v, parts):
    n_parts, rows, cols = parts.shape
    lanes = -(-cols // LANES) * LANES
    tr = _row_tile(rows, 4 * lanes * (n_parts + 7) * 2, budget=16 << 20)
    c1 = np.float32(1.0 - ADAM_B1 ** ADAM_STEP)
    c2 = np.float32(1.0 - ADAM_B2 ** ADAM_STEP)

    def kern(w_ref, m_ref, v_ref, p_ref, g_ref, d_ref, nm_ref, nv_ref):
        g = p_ref[0]
        for k in range(1, n_parts):
            g = g + p_ref[k]
        m_new = ADAM_B1 * m_ref[...] + (1.0 - ADAM_B1) * g
        v_new = ADAM_B2 * v_ref[...] + (1.0 - ADAM_B2) * jnp.square(g)
        g_ref[...] = g
        nm_ref[...] = m_new
        nv_ref[...] = v_new
        d_ref[...] = -ADAM_LR * ((m_new / c1) / (jnp.sqrt(v_new / c2) + ADAM_EPS) + ADAM_WD * w_ref[...])

    spec = pl.BlockSpec((tr, cols), lambda i: (i, 0))
    return pl.pallas_call(
        kern, grid=(rows // tr,), name=name,
        in_specs=[spec, spec, spec, pl.BlockSpec((n_parts, tr, cols), lambda i: (0, i, 0))], out_specs=[spec] * 4,
        out_shape=[jax.ShapeDtypeStruct((rows, cols), F32)] * 4,
        compiler_params=_params(4 * lanes * tr * (n_parts + 7) * 2),
    )(w, m, v, parts)


def _packed_rows(shape):
    return -(-int(np.prod(shape)) // (SUBLANES * LANES)) * SUBLANES


def _pack(arrays, row_multiple):
    parts = []
    for a in arrays:
        n = int(np.prod(a.shape))
        r = _packed_rows(a.shape)
        parts.append(jnp.pad(a.reshape(-1), (0, r * LANES - n)).reshape(r, LANES))
    rows = sum(p.shape[0] for p in parts)
    total = -(-rows // row_multiple) * row_multiple
    if total > rows:
        parts.append(jnp.zeros((total - rows, LANES), arrays[0].dtype))
    return jnp.concatenate(parts, axis=0)


def _unpack(packed, shapes):
    out, off = [], 0
    for s in shapes:
        r = _packed_rows(s)
        out.append(packed[off:off + r].reshape(-1)[:int(np.prod(s))].reshape(s))
        off += r
    return out


def _split_w_in(w_in):
    main = jnp.concatenate([w_in[:, :, s:e] for s, e in _IN_MAIN], axis=2)
    pad = jnp.zeros(w_in.shape[:2] + (LANES - 80,), w_in.dtype)
    small = jnp.concatenate([w_in[:, :, s:e] for s, e in _IN_SMALL] + [pad], axis=2)
    return main, small


def _join_w_in(main, small):
    return jnp.concatenate([main[:, :, 0:8192], small[:, :, 0:64], main[:, :, 8192:12288], small[:, :, 64:80],
                            main[:, :, 12288:15360]], axis=2)


def _unshard(gathered, axis):
    nd, nl, r, c = gathered.shape
    if axis == 1:
        return gathered.transpose(1, 0, 2, 3).reshape(nl, nd * r, c)
    return gathered.transpose(1, 2, 0, 3).reshape(nl, r, nd * c)


def _reshard(full, axis):
    nl, r, c = full.shape
    if axis == 1:
        return full.reshape(nl, N_DEV, r // N_DEV, c).transpose(1, 0, 2, 3)
    return full.reshape(nl, r, N_DEV, c // N_DEV).transpose(2, 0, 1, 3)


def kernel(x, c, ctx, c_ctx, w_ada, b_ada, w_in, lru_conv_w, lru_conv_b, lru_w_r, lru_b_r, lru_w_i, lru_b_i, lru_lambda, ssd_conv_w, ssd_conv_b, ssd_dt_bias, ssd_a_log, ssd_d, ssd_norm_w, ml_conv_w, ml_conv_b, ml_gate_b, ml_norm_w, w_br_a, w_br_b, w_br_c, w_out, b_out, ln1_g, ln1_b, w_ff1, b_ff1, w_ff2, b_ff2, ln2_g, ln2_b, loss_target, m_c_ctx, m_w_ada, m_b_ada, m_w_in, m_lru_conv_w, m_lru_conv_b, m_lru_w_r, m_lru_b_r, m_lru_w_i, m_lru_b_i, m_lru_lambda, m_ssd_conv_w, m_ssd_conv_b, m_ssd_dt_bias, m_ssd_a_log, m_ssd_d, m_ssd_norm_w, m_ml_conv_w, m_ml_conv_b, m_ml_gate_b, m_ml_norm_w, m_w_br_a, m_w_br_b, m_w_br_c, m_w_out, m_b_out, m_ln1_g, m_ln1_b, m_w_ff1, m_b_ff1, m_w_ff2, m_b_ff2, m_ln2_g, m_ln2_b, v_c_ctx, v_w_ada, v_b_ada, v_w_in, v_lru_conv_w, v_lru_conv_b, v_lru_w_r, v_lru_b_r, v_lru_w_i, v_lru_b_i, v_lru_lambda, v_ssd_conv_w, v_ssd_conv_b, v_ssd_dt_bias, v_ssd_a_log, v_ssd_d, v_ssd_norm_w, v_ml_conv_w, v_ml_conv_b, v_ml_gate_b, v_ml_norm_w, v_w_br_a, v_w_br_b, v_w_br_c, v_w_out, v_b_out, v_ln1_g, v_ln1_b, v_w_ff1, v_b_ff1, v_w_ff2, v_b_ff2, v_ln2_g, v_ln2_b):
    a = dict(locals())
    me = 4 * lax.axis_index("x") + 2 * lax.axis_index("y") + lax.axis_index("c")

    wts = {n: a[n] for n in _REPLICATED}
    for n, axis in _BIG.items():
        full = _unshard(_gather_two_level("gather_" + n, a[n].astype(BF16)), axis)
        if n == 'w_in':
            main, small = _split_w_in(full)
            wts['w_in_main'], wts['w_in_small'] = main.astype(F32), small.astype(F32)
        else:
            wts[n] = full.astype(F32)
    small_shapes = [a[n].shape for n in _SMALL_SHARDED]
    small_all = _exchange("gather_small", _pack([a[n] for n in _SMALL_SHARDED], SUBLANES), True)
    per_dev = [_unpack(small_all[k], small_shapes) for k in range(N_DEV)]
    for i, n in enumerate(_SMALL_SHARDED):
        wts[n] = jnp.concatenate([per_dev[k][i] for k in range(N_DEV)], axis=-1)

    y, vjp = jax.vjp(functools.partial(_forward, cvec=c, ctx=ctx[0]), x[0], wts)
    loss_local, dy = _loss_and_cotangent(y, loss_target[0])
    grad_x, grads = vjp(dy)
    grads['w_in'] = _join_w_in(grads.pop('w_in_main'), grads.pop('w_in_small'))
    loss = lax.psum(loss_local, ("x", "y", "c"))

    out = {}

    def put(n, res, shape):
        for kind, r in zip(("grad_", "delta_", "new_m_", "new_v_"), res):
            out[kind + n] = r.reshape(shape)

    for n, axis in _BIG.items():
        shp = a[n].shape
        rows, cols = shp[0] * shp[1], shp[2]
        parts = _reshard(grads[n], axis).reshape(N_DEV, rows, cols)
        chip = _chip_sum("chipsum_" + n, parts, _scatter_to_sibling("scatter_d2d_" + n, parts))
        parts = _scatter_across_chips("scatter_ici_" + n, chip)
        put(n, _adamw("adamw_" + n, a[n].reshape(rows, cols), a["m_" + n].reshape(rows, cols), a["v_" + n].reshape(rows, cols), parts), shp)

    rep_names = _REPLICATED + _SMALL_SHARDED
    chunk_rows = SUBLANES * N_DEV
    g_pack = _pack([grads[n] for n in rep_names], chunk_rows * N_DEV)
    rows = g_pack.shape[0]
    parts = _exchange("scatter_rep", g_pack.reshape(N_DEV, rows // N_DEV, LANES), False)
    mine = _sum_parts("sum_rep", parts)
    g_all = _exchange("gather_rep", mine, True).reshape(rows, LANES)
    g_full = _unpack(g_all, [grads[n].shape for n in rep_names])
    g_local = []
    for n, g in zip(rep_names, g_full):
        if n in _SMALL_SHARDED:
            width = a[n].shape[-1]
            g = lax.dynamic_slice_in_dim(g, me * width, width, axis=g.ndim - 1)
        g_local.append(g)
    shapes = [a[n].shape for n in rep_names]
    res = _adamw("adamw_rep", _pack([a[n] for n in rep_names], chunk_rows), _pack([a["m_" + n] for n in rep_names], chunk_rows),
                 _pack([a["v_" + n] for n in rep_names], chunk_rows), _pack(g_local, chunk_rows)[None])
    unpacked = [_unpack(r, shapes) for r in res]
    for i, n in enumerate(rep_names):
        put(n, [u[i] for u in unpacked], shapes[i])

    outs = [loss, grad_x[None]]
    for kind in ("grad_", "delta_", "new_m_", "new_v_"):
        outs += [out[kind + n] for n in _WEIGHTS]
    return tuple(outs)
```

```python
import functools

import numpy as np
import jax
import jax.numpy as jnp
from jax import lax
from jax.experimental import pallas as pl
from jax.experimental.pallas import tpu as pltpu

F32 = jnp.float32
BF16 = jnp.bfloat16

N_DEV = 8
D_MODEL = 1024
DEPTH = 2
GRID_W = 64
CHUNK = 128
LN_EPS = 1e-6
LRU_BLOCKS = 8
LRU_BS = 128
LRU_C = 8.0
SSD_INNER = 2048
SSD_GROUPS = 8
SSD_HPG = 4
SSD_HEADDIM = 64
SSD_STATE = 128
ML_HEADS = 4
ML_HD = 256
D_FF = 4096
DN_ALPHA = (2 * DEPTH) ** 0.25
ADAM_LR, ADAM_B1, ADAM_B2, ADAM_EPS, ADAM_WD, ADAM_STEP = 0.001, 0.9, 0.999, 1e-08, 0.01, 10

VMEM_CAP = 60 * 1024 * 1024
SUBLANES = 8
LANES = 128

_IN_MAIN = ((0, 8192), (8256, 12352), (12368, 15440))
_IN_MAIN_WIDTHS = (1024, 1024, 2048, 2048, 1024, 1024, 1024, 1024, 1024, 1024, 1024, 1024, 1024)
_IN_SMALL = ((8192, 8256), (12352, 12368))
_DT_LANE = 0
_MG_LANE = 64

_WEIGHTS = ['c_ctx', 'w_ada', 'b_ada', 'w_in', 'lru_conv_w', 'lru_conv_b', 'lru_w_r', 'lru_b_r', 'lru_w_i', 'lru_b_i',
            'lru_lambda', 'ssd_conv_w', 'ssd_conv_b', 'ssd_dt_bias', 'ssd_a_log', 'ssd_d', 'ssd_norm_w', 'ml_conv_w',
            'ml_conv_b', 'ml_gate_b', 'ml_norm_w', 'w_br_a', 'w_br_b', 'w_br_c', 'w_out', 'b_out', 'ln1_g', 'ln1_b',
            'w_ff1', 'b_ff1', 'w_ff2', 'b_ff2', 'ln2_g', 'ln2_b']
_BIG = {'w_ada': 2, 'w_in': 2, 'w_ff1': 2, 'w_br_a': 1, 'w_br_b': 1, 'w_br_c': 1, 'w_out': 1, 'w_ff2': 1}
_SMALL_SHARDED = ['lru_conv_w', 'lru_b_r', 'lru_b_i', 'lru_lambda', 'ssd_conv_w', 'ml_conv_w']
_REPLICATED = [n for n in _WEIGHTS if n not in _BIG and n not in _SMALL_SHARDED]


def _params(vmem_bytes):
    return pltpu.CompilerParams(vmem_limit_bytes=int(min(max(2 * vmem_bytes, 32 << 20), VMEM_CAP)))


def _row_tile(n_rows, bytes_per_row, budget=6 << 20, cap=512):
    t = cap
    while t > SUBLANES and (t * bytes_per_row > budget or n_rows % t):
        t //= 2
    assert n_rows % t == 0, (n_rows, t)
    return t


def _dg(a, b, ca, cb):
    return lax.dot_general(a.astype(BF16), b.astype(BF16), (((ca,), (cb,)), ((), ())), preferred_element_type=F32)


def _make_bdot(ca, cb):
    @jax.custom_vjp
    def f(a, b):
        return _dg(a, b, ca, cb)

    def fwd(a, b):
        return _dg(a, b, ca, cb), (a, b)

    def bwd(res, g):
        a, b = res
        da = _dg(g, b, 1, 1 - cb) if ca == 1 else _dg(b, g, 1 - cb, 1)
        db = _dg(a, g, 1 - ca, 0) if cb == 0 else _dg(g, a, 0, 1 - ca)
        return da, db

    f.defvjp(fwd, bwd)
    return f


_mm_nn = _make_bdot(1, 0)
_mm_nt = _make_bdot(1, 1)
_mm_tn = _make_bdot(0, 0)


@jax.custom_vjp
def _round_bf16(x):
    return x.astype(BF16).astype(F32)


_round_bf16.defvjp(lambda x: (_round_bf16(x), None), lambda _, g: (g,))


def _exact_dot(a, b):
    return jnp.dot(a, b, precision=lax.Precision.HIGHEST, preferred_element_type=F32)


def _layernorm_rows(x):
    mu = jnp.mean(x, -1, keepdims=True)
    var = jnp.mean(jnp.square(x - mu), -1, keepdims=True)
    return (x - mu) * lax.rsqrt(var + LN_EPS)


def _rowwise(name, f, rows, params, out_widths):
    rows, params = tuple(rows), tuple(params)
    nr, npar, no = len(rows), len(params), len(out_widths)
    n_rows = rows[0].shape[0]
    row_w = [r.shape[1] for r in rows]
    par_bytes = sum(int(np.prod(p.shape)) * 4 for p in params)
    tile = _row_tile(n_rows, 4 * (2 * sum(row_w) + 2 * sum(out_widths)))
    grid = (n_rows // tile,)

    def row_spec(w):
        return pl.BlockSpec((tile, w), lambda i: (i, 0))

    def par_spec(p):
        return pl.BlockSpec(p.shape, lambda i: (0, 0))

    vmem = 2 * tile * 4 * (2 * sum(row_w) + 3 * sum(out_widths)) + 4 * par_bytes

    def fwd_call(rows, params):
        def kern(*refs):
            outs = f(*[r[...] for r in refs[:nr + npar]])
            for r, o in zip(refs[nr + npar:], outs):
                r[...] = o

        return pl.pallas_call(
            kern, grid=grid, name=name + "_fwd",
            in_specs=[row_spec(w) for w in row_w] + [par_spec(p) for p in params],
            out_specs=[row_spec(w) for w in out_widths],
            out_shape=[jax.ShapeDtypeStruct((n_rows, w), F32) for w in out_widths],
            compiler_params=_params(vmem),
        )(*rows, *params)

    def bwd_call(rows, params, gouts):
        def kern(*refs):
            ins = [r[...] for r in refs[:nr + npar]]
            gs = tuple(r[...] for r in refs[nr + npar:nr + npar + no])
            grads = jax.vjp(f, *ins)[1](gs)
            drefs = refs[nr + npar + no:]
            for k in range(nr):
                drefs[k][...] = grads[k]

            @pl.when(pl.program_id(0) == 0)
            def _():
                for k in range(npar):
                    drefs[nr + k][...] = jnp.zeros_like(drefs[nr + k])

            for k in range(npar):
                drefs[nr + k][...] += grads[nr + k]

        res = pl.pallas_call(
            kern, grid=grid, name=name + "_bwd",
            in_specs=[row_spec(w) for w in row_w] + [par_spec(p) for p in params] + [row_spec(w) for w in out_widths],
            out_specs=[row_spec(w) for w in row_w] + [par_spec(p) for p in params],
            out_shape=[jax.ShapeDtypeStruct(r.shape, F32) for r in rows] + [jax.ShapeDtypeStruct(p.shape, F32) for p in params],
            compiler_params=_params(vmem),
        )(*rows, *params, *gouts)
        return tuple(res[:nr]), tuple(res[nr:])

    @jax.custom_vjp
    def op(rows, params):
        return tuple(fwd_call(rows, params))

    op.defvjp(lambda r, p: (tuple(fwd_call(r, p)), (r, p)), lambda res, g: bwd_call(res[0], res[1], g))
    return op(rows, params)


def _group_ranges(widths, tn):
    starts, s = [], 0
    for w in widths:
        assert w % tn == 0, (w, tn)
        starts.append((s // tn, (s + w) // tn))
        s += w
    return starts, s // tn


def _group_tile(refs, ranges, row_tile, col_tile, i, j):
    out = []
    for ref, (s, e) in zip(refs, ranges):
        cols = pl.ds(pl.multiple_of((j - s) * col_tile, col_tile), col_tile)
        out.append(((j >= s) & (j < e), ref, cols))
    return [(p, lambda r=r, c=c: r.at[pl.ds(pl.multiple_of(i * row_tile, row_tile), row_tile), c]) for p, r, c in out]


def _linear(name, a, w, widths=None):
    single = widths is None
    widths = (w.shape[1],) if single else tuple(widths)
    m, k = a.shape
    n = w.shape[1]
    ng = len(widths)
    tn = 128 if n < 256 else (256 if k > 2048 or n % 512 else 512)
    tm = _row_tile(m, 0, cap=1024 if k <= 2048 else 512)
    ranges, nt = _group_ranges(widths, tn)
    mt = m // tm
    tn_w = 1024 if (k <= 1024 and all(wd % 1024 == 0 for wd in widths)) else tn
    tm_w = _row_tile(m, 0, cap=512)
    ranges_w, nt_w = _group_ranges(widths, tn_w)
    mt_w = m // tm_w
    hbm = pl.BlockSpec(memory_space=pl.ANY)

    def fwd_call(a, w):
        n_steps = mt * nt

        def kern(a_ref, w_ref, *rest):
            outs, a_bf, obuf, osem = rest[:ng], rest[ng], rest[ng + 1], rest[ng + 2]
            i, j = pl.program_id(0), pl.program_id(1)
            step = i * nt + j
            slot = lax.rem(step, 2)

            def drain(sl):
                pltpu.make_async_copy(obuf.at[sl], outs[0].at[pl.ds(0, tm), pl.ds(0, tn)], osem.at[sl]).wait()

            @pl.when(j == 0)
            def _():
                a_bf[...] = a_ref[...].astype(BF16)

            @pl.when(step >= 2)
            def _():
                drain(slot)

            obuf[slot] = jnp.dot(a_bf[...], w_ref[...], preferred_element_type=F32)
            for pred, window in _group_tile(outs, ranges, tm, tn, i, j):
                @pl.when(pred)
                def _(window=window):
                    pltpu.make_async_copy(obuf.at[slot], window(), osem.at[slot]).start()

            @pl.when(step == n_steps - 1)
            def _():
                drain(slot)
                if n_steps > 1:
                    drain(1 - slot)

        return pl.pallas_call(
            kern, grid=(mt, nt), name=name + "_fwd",
            in_specs=[pl.BlockSpec((tm, k), lambda i, j: (i, 0)), pl.BlockSpec((k, tn), lambda i, j: (0, j))],
            out_specs=[hbm] * ng,
            out_shape=[jax.ShapeDtypeStruct((m, wd), F32) for wd in widths],
            scratch_shapes=[pltpu.VMEM((tm, k), BF16), pltpu.VMEM((2, tm, tn), F32), pltpu.SemaphoreType.DMA((2,))],
            compiler_params=_params(10 * tm * k + 4 * k * tn + 8 * tm * tn),
        )(a, w.astype(BF16))

    def prefetched(gs, gbuf, gsem, rngs, row_tile, col_tile, step, n_steps, tile_of):
        slot = lax.rem(step, 2)

        def start(s_idx, sl):
            ii, jj = tile_of(s_idx)
            for pred, window in _group_tile(gs, rngs, row_tile, col_tile, ii, jj):
                @pl.when(pred)
                def _(window=window):
                    pltpu.make_async_copy(window(), gbuf.at[sl], gsem.at[sl]).start()

        @pl.when(step == 0)
        def _():
            start(step, slot)

        @pl.when(step + 1 < n_steps)
        def _():
            start(step + 1, 1 - slot)

        pltpu.make_async_copy(gs[0].at[pl.ds(0, row_tile), pl.ds(0, col_tile)], gbuf.at[slot], gsem.at[slot]).wait()
        return slot

    def dgrad_call(w, gouts):
        def kern(w_ref, *rest):
            gs, da, gbuf, gsem = rest[:ng], rest[ng], rest[ng + 1], rest[ng + 2]
            i, j = pl.program_id(0), pl.program_id(1)
            slot = prefetched(gs, gbuf, gsem, ranges, tm, tn, i * nt + j, mt * nt, lambda s: (s // nt, lax.rem(s, nt)))

            @pl.when(j == 0)
            def _():
                da[...] = jnp.zeros_like(da)

            da[...] += lax.dot_general(gbuf[slot].astype(BF16), w_ref[...], (((1,), (1,)), ((), ())), preferred_element_type=F32)

        return pl.pallas_call(
            kern, grid=(mt, nt), name=name + "_dgrad",
            in_specs=[pl.BlockSpec((k, tn), lambda i, j: (0, j))] + [hbm] * ng,
            out_specs=pl.BlockSpec((tm, k), lambda i, j: (i, 0)),
            out_shape=jax.ShapeDtypeStruct((m, k), F32),
            scratch_shapes=[pltpu.VMEM((2, tm, tn), F32), pltpu.SemaphoreType.DMA((2,))],
            compiler_params=_params(12 * tm * k + 4 * k * tn + 10 * tm * tn),
        )(w.astype(BF16), *gouts)

    def wgrad_call(a, gouts):
        def kern(a_ref, *rest):
            gs, dw, gbuf, gsem = rest[:ng], rest[ng], rest[ng + 1], rest[ng + 2]
            j, i = pl.program_id(0), pl.program_id(1)
            slot = prefetched(gs, gbuf, gsem, ranges_w, tm_w, tn_w, j * mt_w + i, mt_w * nt_w,
                              lambda s: (lax.rem(s, mt_w), s // mt_w))

            @pl.when(i == 0)
            def _():
                dw[...] = jnp.zeros_like(dw)

            dw[...] += lax.dot_general(a_ref[...].astype(BF16), gbuf[slot].astype(BF16), (((0,), (0,)), ((), ())),
                                       preferred_element_type=F32)

        return pl.pallas_call(
            kern, grid=(nt_w, mt_w), name=name + "_wgrad",
            in_specs=[pl.BlockSpec((tm_w, k), lambda j, i: (i, 0))] + [hbm] * ng,
            out_specs=pl.BlockSpec((k, tn_w), lambda j, i: (0, j)),
            out_shape=jax.ShapeDtypeStruct((k, n), F32),
            scratch_shapes=[pltpu.VMEM((2, tm_w, tn_w), F32), pltpu.SemaphoreType.DMA((2,))],
            compiler_params=_params(12 * tm_w * k + 12 * k * tn_w + 10 * tm_w * tn_w),
        )(a, *gouts)

    @jax.custom_vjp
    def op(a, w):
        return tuple(fwd_call(a, w))

    op.defvjp(lambda a, w: (tuple(fwd_call(a, w)), (a, w)),
              lambda res, g: (dgrad_call(res[1], g), wgrad_call(res[0], g)))
    out = op(a, w)
    return out[0] if single else out


def _conv_taps(x_ext, w, n_ext):
    xm2 = pltpu.roll(x_ext, 2, 0)
    xm1 = pltpu.roll(x_ext, 1, 0)
    xp1 = pltpu.roll(x_ext, n_ext - 1, 0)
    return xm2, xm1, xp1


def _dwconv(name, x, w, b, act):
    n_rows, ch = x.shape
    tt = _row_tile(n_rows, 4 * 6 * ch, cap=256)
    nt = n_rows // tt
    n_ext = tt + 2 * SUBLANES
    per8 = tt // SUBLANES
    last8 = n_rows // SUBLANES - 1
    main = pl.BlockSpec((tt, ch), lambda i: (i, 0))
    prev = pl.BlockSpec((SUBLANES, ch), lambda i: (jnp.maximum(i * per8 - 1, 0), 0))
    nxt = pl.BlockSpec((SUBLANES, ch), lambda i: (jnp.minimum((i + 1) * per8, last8), 0))
    wspec = pl.BlockSpec((4, ch), lambda i: (0, 0))
    bspec = pl.BlockSpec((1, ch), lambda i: (0, 0))
    vmem = 4 * n_ext * ch * 14

    def ext(main_ref, prev_ref, next_ref):
        i = pl.program_id(0)
        p = jnp.where(i > 0, prev_ref[...], 0.0)
        q = jnp.where(i < nt - 1, next_ref[...], 0.0)
        return jnp.concatenate([p, main_ref[...], q], axis=0)

    def pre_of(x_ext, wv, bv):
        xm2, xm1, xp1 = _conv_taps(x_ext, wv, n_ext)
        pre = wv[0:1] * xm2 + wv[1:2] * xm1 + wv[2:3] * x_ext + wv[3:4] * xp1 + bv
        return pre, (xm2, xm1, xp1)

    def fwd_call(x, w, b):
        def kern(xm, xp, xn, w_ref, b_ref, o_ref):
            pre, _ = pre_of(ext(xm, xp, xn), w_ref[...], b_ref[...])
            pre = pre[SUBLANES:SUBLANES + tt]
            o_ref[...] = pre * jax.nn.sigmoid(pre) if act else pre

        return pl.pallas_call(
            kern, grid=(nt,), name=name + "_fwd", in_specs=[main, prev, nxt, wspec, bspec], out_specs=main,
            out_shape=jax.ShapeDtypeStruct((n_rows, ch), F32), compiler_params=_params(vmem),
        )(x, x, x, w, b)

    def bwd_call(x, w, b, dy):
        def kern(xm, xp, xn, gm, gp, gn, w_ref, b_ref, dx_ref, dw_ref, db_ref):
            wv = w_ref[...]
            x_ext = ext(xm, xp, xn)
            pre, (xm2, xm1, xp1) = pre_of(x_ext, wv, b_ref[...])
            dpre = ext(gm, gp, gn)
            if act:
                sg = jax.nn.sigmoid(pre)
                dpre = dpre * (sg + pre * sg * (1.0 - sg))
            dx = (wv[0:1] * pltpu.roll(dpre, n_ext - 2, 0) + wv[1:2] * pltpu.roll(dpre, n_ext - 1, 0)
                  + wv[2:3] * dpre + wv[3:4] * pltpu.roll(dpre, 1, 0))
            sl = slice(SUBLANES, SUBLANES + tt)
            dx_ref[...] = dx[sl]
            dm = dpre[sl]

            @pl.when(pl.program_id(0) == 0)
            def _():
                dw_ref[...] = jnp.zeros_like(dw_ref)
                db_ref[...] = jnp.zeros_like(db_ref)

            dw_ref[...] += jnp.concatenate(
                [jnp.sum(dm * t[sl], axis=0, keepdims=True) for t in (xm2, xm1, x_ext, xp1)], axis=0)
            db_ref[...] += jnp.sum(dm, axis=0, keepdims=True)

        return pl.pallas_call(
            kern, grid=(nt,), name=name + "_bwd", in_specs=[main, prev, nxt, main, prev, nxt, wspec, bspec],
            out_specs=[main, wspec, bspec],
            out_shape=[jax.ShapeDtypeStruct((n_rows, ch), F32), jax.ShapeDtypeStruct((4, ch), F32),
                       jax.ShapeDtypeStruct((1, ch), F32)],
            compiler_params=_params(vmem),
        )(x, x, x, dy, dy, dy, w, b)

    @jax.custom_vjp
    def op(x, w, b):
        return fwd_call(x, w, b)

    op.defvjp(lambda x, w, b: (fwd_call(x, w, b), (x, w, b)), lambda res, g: tuple(bwd_call(*res, g)))
    return op(x, w, b)


def _scan_groups(tt, ch, reverse, load, store, carry_ref):
    row = lax.broadcasted_iota(jnp.int32, (SUBLANES, ch), 0)
    ng = tt // SUBLANES

    def body(k, carry):
        g = (ng - 1 - k) if reverse else k
        sl = pl.ds(pl.multiple_of(g * SUBLANES, SUBLANES), SUBLANES)
        a, b, extra = load(sl)
        for s in (1, 2, 4):
            if reverse:
                a_sh, b_sh, valid = pltpu.roll(a, SUBLANES - s, 0), pltpu.roll(b, SUBLANES - s, 0), row < SUBLANES - s
            else:
                a_sh, b_sh, valid = pltpu.roll(a, s, 0), pltpu.roll(b, s, 0), row >= s
            b = jnp.where(valid, b + a * b_sh, b)
            a = jnp.where(valid, a * a_sh, a)
        h = b + a * carry
        if reverse:
            h_prev = jnp.where(row == SUBLANES - 1, carry, pltpu.roll(h, SUBLANES - 1, 0))
            last = h[0:1]
        else:
            h_prev = jnp.where(row == 0, carry, pltpu.roll(h, 1, 0))
            last = h[SUBLANES - 1:SUBLANES]
        store(sl, h, h_prev, extra)
        return jnp.broadcast_to(last, (SUBLANES, ch))

    carry_ref[...] = lax.fori_loop(0, ng, body, carry_ref[...])


def _lin_scan(name, a, b, h0, reverse):
    n_rows, ch = a.shape
    tt = _row_tile(n_rows, 0, cap=256)
    nt = n_rows // tt
    vmem = 2 * 4 * tt * ch * 5

    def tile_spec(rev):
        return pl.BlockSpec((tt, ch), (lambda i: (nt - 1 - i, 0)) if rev else (lambda i: (i, 0)))

    vec = pl.BlockSpec((1, ch), lambda i: (0, 0))

    def fwd_call(a, b, h0):
        def kern(a_ref, b_ref, h0_ref, h_ref, hp_ref, last_ref, carry):
            @pl.when(pl.program_id(0) == 0)
            def _():
                carry[...] = jnp.broadcast_to(h0_ref[...], carry.shape)

            def load(sl):
                return a_ref[sl, :], b_ref[sl, :], None

            def store(sl, h, h_prev, _):
                h_ref[sl, :] = h
                hp_ref[sl, :] = h_prev

            _scan_groups(tt, ch, reverse, load, store, carry)
            last_ref[...] = carry[0:1]

        return pl.pallas_call(
            kern, grid=(nt,), name=name + "_fwd", in_specs=[tile_spec(reverse), tile_spec(reverse), vec],
            out_specs=[tile_spec(reverse), tile_spec(reverse), vec],
            out_shape=[jax.ShapeDtypeStruct((n_rows, ch), F32)] * 2 + [jax.ShapeDtypeStruct((1, ch), F32)],
            scratch_shapes=[pltpu.VMEM((SUBLANES, ch), F32)], compiler_params=_params(vmem),
        )(a, b, h0)

    def bwd_call(a, h_prev, dh, dlast):
        rev = not reverse

        def kern(a_ref, hp_ref, dh_ref, dl_ref, da_ref, db_ref, d0_ref, carry):
            @pl.when(pl.program_id(0) == 0)
            def _():
                carry[...] = jnp.broadcast_to(dl_ref[...], carry.shape)

            def load(sl):
                av, dv = a_ref[sl, :], dh_ref[sl, :]
                return av, av * dv, dv

            def store(sl, u, u_next, dv):
                g = dv + u_next
                db_ref[sl, :] = g
                da_ref[sl, :] = g * hp_ref[sl, :]

            _scan_groups(tt, ch, rev, load, store, carry)
            d0_ref[...] = carry[0:1]

        return pl.pallas_call(
            kern, grid=(nt,), name=name + "_bwd", in_specs=[tile_spec(rev)] * 3 + [vec],
            out_specs=[tile_spec(rev), tile_spec(rev), vec],
            out_shape=[jax.ShapeDtypeStruct((n_rows, ch), F32)] * 2 + [jax.ShapeDtypeStruct((1, ch), F32)],
            scratch_shapes=[pltpu.VMEM((SUBLANES, ch), F32)], compiler_params=_params(vmem),
        )(a, h_prev, dh, dlast)

    @jax.custom_vjp
    def op(a, b, h0):
        h, _, last = fwd_call(a, b, h0)
        return h, last

    def op_fwd(a, b, h0):
        h, h_prev, last = fwd_call(a, b, h0)
        return (h, last), (a, h_prev)

    def op_bwd(res, g):
        da, db, d0 = bwd_call(res[0], res[1], g[0], g[1])
        return da, db, d0

    op.defvjp(op_fwd, op_bwd)
    return op(a, b, h0)


def _tri(reverse):
    q = lax.broadcasted_iota(jnp.int32, (CHUNK, CHUNK), 0)
    s = lax.broadcasted_iota(jnp.int32, (CHUNK, CHUNK), 1)
    return (q <= s) if reverse else (q >= s)


def _pick_col(x, lane):
    idx = lax.broadcasted_iota(jnp.int32, x.shape, 1)
    return jnp.sum(jnp.where(idx == lane, x, 0.0), axis=1, keepdims=True)


def _pick_row(x, row):
    idx = lax.broadcasted_iota(jnp.int32, x.shape, 0)
    return jnp.sum(jnp.where(idx == row, x, 0.0), axis=0, keepdims=True)


def _ssd_shared(small, bias_row, alog_row, reverse):
    delta_all = jax.nn.softplus(small + bias_row)
    acs_all = _exact_dot(_tri(reverse).astype(F32), delta_all * (-jnp.exp(alog_row)))
    return delta_all, acs_all, acs_all.T


def _ssd_group(xs, bm, cm, state, delta_all, acs_all, acs_t, g, direction, reverse):
    mask = _tri(reverse)
    last = 0 if reverse else CHUNK - 1
    hd = SSD_HEADDIM
    rowi = lax.broadcasted_iota(jnp.int32, (CHUNK, 1), 0)
    a_cols, a_rows, deltas, tots = [], [], [], []
    for r in range(SSD_HPG):
        lane = _DT_LANE + 32 * direction + SSD_HPG * g + r
        a_col = _pick_col(acs_all, lane)
        a_cols.append(a_col)
        a_rows.append(_pick_row(acs_t, lane))
        deltas.append(_pick_col(delta_all, lane))
        tots.append(jnp.sum(jnp.where(rowi == last, a_col, 0.0), axis=0, keepdims=True))

    def wide(cols, rows):
        return jnp.concatenate([jnp.broadcast_to(c, (rows, hd)) for c in cols], axis=1)

    a_w = wide(a_cols, CHUNK)
    x_w = xs * wide(deltas, CHUNK)
    st = _mm_tn(x_w * jnp.exp(wide(tots, 1) - a_w), bm)
    y_off = _mm_nt(cm, state) * jnp.exp(a_w)
    grow = jnp.concatenate([jnp.broadcast_to(jnp.exp(t), (hd, 1)) for t in tots], axis=0)
    cb = _mm_nt(cm, bm)
    m_cat = jnp.concatenate([cb * jnp.exp(jnp.where(mask, a_cols[r] - a_rows[r], -jnp.inf)) for r in range(SSD_HPG)], axis=1)
    lane_head = lax.broadcasted_iota(jnp.int32, (1, SSD_HPG * hd), 1) // hd
    x_bd = jnp.concatenate([jnp.where(lane_head == r, x_w, 0.0) for r in range(SSD_HPG)], axis=0)
    return _mm_nn(m_cat, x_bd) + y_off, grow * state + st


def _ssd_scan(name, xs, bm, cm, small, bias_row, alog_row, s0, direction, reverse):
    n_rows = xs.shape[0]
    nc = n_rows // CHUNK
    gw = SSD_HPG * SSD_HEADDIM
    vmem = 4 * CHUNK * (gw + 3 * 128) * 8 + 4 * gw * 128 * 12 + (8 << 20)

    n_state = SSD_GROUPS * gw
    shared_scratch = [pltpu.VMEM((CHUNK, LANES), F32), pltpu.VMEM((CHUNK, LANES), F32), pltpu.VMEM((LANES, CHUNK), F32)]

    def specs(order, gps=1):
        def cidx(c):
            return (nc - 1 - c) if order else c

        return dict(
            xs=pl.BlockSpec((CHUNK, gps * gw), lambda c, g: (cidx(c), g)),
            bc=pl.BlockSpec((CHUNK, gps * SSD_STATE), lambda c, g: (cidx(c), g)),
            small=pl.BlockSpec((CHUNK, LANES), lambda c, g: (cidx(c), 0)),
            row=pl.BlockSpec((1, LANES), lambda c, g: (0, 0)),
            state=pl.BlockSpec((n_state, SSD_STATE), lambda c, g: (0, 0)),
            enter=pl.BlockSpec((1, gps * gw, SSD_STATE), lambda c, g: (cidx(c), g, 0)),
        )

    def group_rows(g, gps=1):
        return pl.ds(pl.multiple_of(g * gps * gw, gps * gw), gps * gw)

    def fwd_call(xs, bm, cm, small, bias_row, alog_row, s0):
        gps = 2
        sp = specs(reverse, gps)

        def kern(xs_r, bm_r, cm_r, sm_r, br_r, ar_r, s0_r, y_r, sf_r, se_r, st, sh_d, sh_a, sh_t):
            c, g = pl.program_id(0), pl.program_id(1)

            @pl.when((c == 0) & (g == 0))
            def _():
                st[...] = s0_r[...]

            @pl.when(g == 0)
            def _():
                sh_d[...], sh_a[...], sh_t[...] = _ssd_shared(sm_r[...], br_r[...], ar_r[...], reverse)

            rows = group_rows(g, gps)
            s_in = st[rows, :]
            se_r[0] = s_in
            xs_v, bm_v, cm_v = xs_r[...], bm_r[...], cm_r[...]
            ys, s_new = [], []
            for u in range(gps):
                y_u, s_u = _ssd_group(xs_v[:, u * gw:(u + 1) * gw], bm_v[:, u * SSD_STATE:(u + 1) * SSD_STATE],
                                      cm_v[:, u * SSD_STATE:(u + 1) * SSD_STATE], s_in[u * gw:(u + 1) * gw],
                                      sh_d[...], sh_a[...], sh_t[...], gps * g + u, direction, reverse)
                ys.append(y_u)
                s_new.append(s_u)
            s_new = jnp.concatenate(s_new, axis=0)
            y_r[...] = jnp.concatenate(ys, axis=1)
            st[rows, :] = s_new
            sf_r[rows, :] = s_new

        return pl.pallas_call(
            kern, grid=(nc, SSD_GROUPS // gps), name=name + "_fwd",
            in_specs=[sp['xs'], sp['bc'], sp['bc'], sp['small'], sp['row'], sp['row'], sp['state']],
            out_specs=[sp['xs'], sp['state'], sp['enter']],
            out_shape=[jax.ShapeDtypeStruct((n_rows, SSD_INNER), F32), jax.ShapeDtypeStruct((n_state, SSD_STATE), F32),
                       jax.ShapeDtypeStruct((nc, n_state, SSD_STATE), F32)],
            scratch_shapes=[pltpu.VMEM((n_state, SSD_STATE), F32)] + shared_scratch, compiler_params=_params(vmem),
        )(xs, bm, cm, small, bias_row, alog_row, s0)

    def bwd_call(xs, bm, cm, small, bias_row, alog_row, enter, dy, dsf):
        sp = specs(not reverse)

        def kern(xs_r, bm_r, cm_r, sm_r, br_r, ar_r, se_r, dy_r, dsf_r, dxs_r, dbm_r, dcm_r, dsm_r, dbr_r, dar_r, ds0_r,
                 ds, sh_d, sh_a, sh_t, gd, ga, gt):
            c, g = pl.program_id(0), pl.program_id(1)

            @pl.when((c == 0) & (g == 0))
            def _():
                ds[...] = dsf_r[...]
                dbr_r[...] = jnp.zeros_like(dbr_r)
                dar_r[...] = jnp.zeros_like(dar_r)

            @pl.when(g == 0)
            def _():
                sh_d[...], sh_a[...], sh_t[...] = _ssd_shared(sm_r[...], br_r[...], ar_r[...], reverse)
                gd[...] = jnp.zeros_like(gd)
                ga[...] = jnp.zeros_like(ga)
                gt[...] = jnp.zeros_like(gt)

            rows = group_rows(g)
            fn = functools.partial(_ssd_group, g=g, direction=direction, reverse=reverse)
            _, vjp = jax.vjp(fn, xs_r[...], bm_r[...], cm_r[...], se_r[0], sh_d[...], sh_a[...], sh_t[...])
            dxs, dbm, dcm, ds_in, dd, da, dt = vjp((dy_r[...], ds[rows, :]))
            dxs_r[...] = dxs
            dbm_r[...] = dbm
            dcm_r[...] = dcm
            ds[rows, :] = ds_in
            ds0_r[rows, :] = ds_in
            gd[...] += dd
            ga[...] += da
            gt[...] += dt

            @pl.when(g == SSD_GROUPS - 1)
            def _():
                shared = functools.partial(_ssd_shared, reverse=reverse)
                dsm, dbr, dar = jax.vjp(shared, sm_r[...], br_r[...], ar_r[...])[1]((gd[...], ga[...], gt[...]))
                dsm_r[...] = dsm
                dbr_r[...] += dbr
                dar_r[...] += dar

        return pl.pallas_call(
            kern, grid=(nc, SSD_GROUPS), name=name + "_bwd",
            in_specs=[sp['xs'], sp['bc'], sp['bc'], sp['small'], sp['row'], sp['row'], sp['enter'], sp['xs'], sp['state']],
            out_specs=[sp['xs'], sp['bc'], sp['bc'], sp['small'], sp['row'], sp['row'], sp['state']],
            out_shape=[jax.ShapeDtypeStruct(xs.shape, F32), jax.ShapeDtypeStruct(bm.shape, F32),
                       jax.ShapeDtypeStruct(cm.shape, F32), jax.ShapeDtypeStruct((n_rows, LANES), F32),
                       jax.ShapeDtypeStruct((1, LANES), F32), jax.ShapeDtypeStruct((1, LANES), F32),
                       jax.ShapeDtypeStruct(s0.shape, F32)],
            scratch_shapes=[pltpu.VMEM((n_state, SSD_STATE), F32)] + shared_scratch + shared_scratch,
            compiler_params=_params(vmem),
        )(xs, bm, cm, small, bias_row, alog_row, enter, dy, dsf)

    @jax.custom_vjp
    def op(*args):
        y, sf, _ = fwd_call(*args)
        return y, sf

    def op_fwd(*args):
        y, sf, enter = fwd_call(*args)
        return (y, sf), (args[:6], enter)

    op.defvjp(op_fwd, lambda res, g: tuple(bwd_call(*res[0], res[1], g[0], g[1])))
    return op(xs, bm, cm, small, bias_row, alog_row, s0)


def _ml_shared(small, gate_row, reverse):
    gates = small + gate_row
    b_all = _exact_dot(_tri(reverse).astype(F32), jax.nn.log_sigmoid(gates))
    return gates, b_all, gates.T, b_all.T


def _ml_head(q, k, v, c_st, n_st, m_st, gates, b_all, gates_t, b_t, h, direction, reverse):
    mask = _tri(reverse)
    last = 0 if reverse else CHUNK - 1
    lane_i = _MG_LANE + 8 * direction + h
    lane_f = lane_i + ML_HEADS
    b_col = _pick_col(b_all, lane_f)
    b_row = _pick_row(b_t, lane_f)
    li_col = _pick_col(gates, lane_i)
    li_row = _pick_row(gates_t, lane_i)
    rowi = lax.broadcasted_iota(jnp.int32, (CHUNK, 1), 0)
    g_tot = jnp.sum(jnp.where(rowi == last, b_col, 0.0), axis=0, keepdims=True)
    m_in = m_st[:, 0:1]
    q = q * (ML_HD ** -0.5)
    w = g_tot - b_col + li_col
    m_loc = jnp.max(w, axis=0, keepdims=True)
    kw = k * jnp.exp(w - m_loc)
    c_loc = _mm_tn(kw, v)
    n_loc = jnp.sum(kw, axis=0, keepdims=True)
    m_new = jnp.maximum(g_tot + m_in, m_loc)
    s_old = jnp.exp(g_tot + m_in - m_new)
    s_loc = jnp.exp(m_loc - m_new)
    c_new = s_old * c_st + s_loc * c_loc
    n_new = s_old * n_st + s_loc * n_loc
    log_d = jnp.where(mask, b_col - b_row + li_row, -jnp.inf)
    inter = b_col + m_in
    m_t = jnp.maximum(inter, jnp.max(log_d, axis=1, keepdims=True))
    dmat = jnp.exp(log_d - m_t)
    wi = jnp.exp(inter - m_t)
    s = _mm_nt(q, k) * dmat
    num = _mm_nn(s, v) + wi * _mm_nn(q, c_st)
    den = jnp.sum(s, axis=1, keepdims=True) + wi * jnp.sum(_round_bf16(q) * _round_bf16(n_st), axis=1, keepdims=True)
    out = num / jnp.maximum(jnp.abs(den), jnp.exp(-m_t))
    return out, c_new, n_new, jnp.broadcast_to(m_new, (1, LANES))


def _ml_scan(name, q, k, v, small, gate_row, c0, n0, m0, direction, reverse):
    n_rows = q.shape[0]
    nc = n_rows // CHUNK
    vmem = 4 * CHUNK * (4 * ML_HD + 128) * 8 + 4 * ML_HD * ML_HD * 12 + (8 << 20)

    def specs(order):
        def cidx(c):
            return (nc - 1 - c) if order else c

        return dict(
            qkv=pl.BlockSpec((CHUNK, ML_HD), lambda c, h: (cidx(c), h)),
            small=pl.BlockSpec((CHUNK, LANES), lambda c, h: (cidx(c), 0)),
            row=pl.BlockSpec((1, LANES), lambda c, h: (0, 0)),
            c=pl.BlockSpec((ML_HEADS * ML_HD, ML_HD), lambda c, h: (0, 0)),
            n=pl.BlockSpec((ML_HEADS, 1, ML_HD), lambda c, h: (0, 0, 0)),
            m=pl.BlockSpec((ML_HEADS, 1, LANES), lambda c, h: (0, 0, 0)),
            ec=pl.BlockSpec((1, ML_HD, ML_HD), lambda c, h: (cidx(c), h, 0)),
            en=pl.BlockSpec((1, 1, 1, ML_HD), lambda c, h: (cidx(c), h, 0, 0)),
            em=pl.BlockSpec((1, 1, 1, LANES), lambda c, h: (cidx(c), h, 0, 0)),
        )

    st_shapes = [jax.ShapeDtypeStruct((ML_HEADS * ML_HD, ML_HD), F32), jax.ShapeDtypeStruct((ML_HEADS, 1, ML_HD), F32),
                 jax.ShapeDtypeStruct((ML_HEADS, 1, LANES), F32)]
    scratch = [pltpu.VMEM((ML_HEADS * ML_HD, ML_HD), F32), pltpu.VMEM((ML_HEADS, 1, ML_HD), F32),
               pltpu.VMEM((ML_HEADS, 1, LANES), F32)]
    shared_scratch = [pltpu.VMEM((CHUNK, LANES), F32), pltpu.VMEM((CHUNK, LANES), F32),
                      pltpu.VMEM((LANES, CHUNK), F32), pltpu.VMEM((LANES, CHUNK), F32)]

    def head_rows(h):
        return pl.ds(pl.multiple_of(h * ML_HD, ML_HD), ML_HD)

    def fwd_call(q, k, v, small, gate_row, c0, n0, m0):
        sp = specs(reverse)

        def kern(q_r, k_r, v_r, sm_r, gr_r, c0_r, n0_r, m0_r, o_r, cf_r, nf_r, mf_r, ec_r, en_r, em_r, cs, ns, ms, *sh):
            c, h = pl.program_id(0), pl.program_id(1)

            @pl.when((c == 0) & (h == 0))
            def _():
                cs[...] = c0_r[...]
                ns[...] = n0_r[...]
                ms[...] = m0_r[...]

            @pl.when(h == 0)
            def _():
                for ref, val in zip(sh, _ml_shared(sm_r[...], gr_r[...], reverse)):
                    ref[...] = val

            rows = head_rows(h)
            c_in, n_in, m_in = cs[rows, :], ns[h], ms[h]
            ec_r[0] = c_in
            en_r[0, 0] = n_in
            em_r[0, 0] = m_in
            out, c_new, n_new, m_new = _ml_head(q_r[...], k_r[...], v_r[...], c_in, n_in, m_in, *[r[...] for r in sh],
                                                h, direction, reverse)
            o_r[...] = out
            cs[rows, :] = c_new
            ns[h] = n_new
            ms[h] = m_new
            cf_r[rows, :] = c_new
            nf_r[h] = n_new
            mf_r[h] = m_new

        return pl.pallas_call(
            kern, grid=(nc, ML_HEADS), name=name + "_fwd",
            in_specs=[sp['qkv']] * 3 + [sp['small'], sp['row'], sp['c'], sp['n'], sp['m']],
            out_specs=[sp['qkv'], sp['c'], sp['n'], sp['m'], sp['ec'], sp['en'], sp['em']],
            out_shape=[jax.ShapeDtypeStruct((n_rows, ML_HEADS * ML_HD), F32)] + st_shapes + [
                jax.ShapeDtypeStruct((nc, ML_HEADS * ML_HD, ML_HD), F32),
                jax.ShapeDtypeStruct((nc, ML_HEADS, 1, ML_HD), F32), jax.ShapeDtypeStruct((nc, ML_HEADS, 1, LANES), F32)],
            scratch_shapes=scratch + shared_scratch, compiler_params=_params(vmem),
        )(q, k, v, small, gate_row, c0, n0, m0)

    def bwd_call(q, k, v, small, gate_row, ec, en, em, do, dcf, dnf, dmf):
        sp = specs(not reverse)
        n_sh = len(shared_scratch)

        def kern(q_r, k_r, v_r, sm_r, gr_r, ec_r, en_r, em_r, do_r, dcf_r, dnf_r, dmf_r,
                 dq_r, dk_r, dv_r, dsm_r, dgr_r, dc0_r, dn0_r, dm0_r, dcs, dns, dms, *rest):
            sh, gsh = rest[:n_sh], rest[n_sh:]
            c, h = pl.program_id(0), pl.program_id(1)

            @pl.when((c == 0) & (h == 0))
            def _():
                dcs[...] = dcf_r[...]
                dns[...] = dnf_r[...]
                dms[...] = dmf_r[...]
                dgr_r[...] = jnp.zeros_like(dgr_r)

            @pl.when(h == 0)
            def _():
                for ref, val in zip(sh, _ml_shared(sm_r[...], gr_r[...], reverse)):
                    ref[...] = val
                for ref in gsh:
                    ref[...] = jnp.zeros_like(ref)

            rows = head_rows(h)
            fn = functools.partial(_ml_head, h=h, direction=direction, reverse=reverse)
            _, vjp = jax.vjp(fn, q_r[...], k_r[...], v_r[...], ec_r[0], en_r[0, 0], em_r[0, 0], *[r[...] for r in sh])
            grads = vjp((do_r[...], dcs[rows, :], dns[h], dms[h]))
            dq, dk, dv, dc, dn, dm = grads[:6]
            dq_r[...] = dq
            dk_r[...] = dk
            dv_r[...] = dv
            for ref, val in zip(gsh, grads[6:]):
                ref[...] += val
            dm = jnp.broadcast_to(jnp.sum(dm, axis=1, keepdims=True), (1, LANES)) * (1.0 / LANES)
            dcs[rows, :] = dc
            dns[h] = dn
            dms[h] = dm
            dc0_r[rows, :] = dc
            dn0_r[h] = dn
            dm0_r[h] = dm

            @pl.when(h == ML_HEADS - 1)
            def _():
                shared = functools.partial(_ml_shared, reverse=reverse)
                dsm, dgr = jax.vjp(shared, sm_r[...], gr_r[...])[1](tuple(r[...] for r in gsh))
                dsm_r[...] = dsm
                dgr_r[...] += dgr

        return pl.pallas_call(
            kern, grid=(nc, ML_HEADS), name=name + "_bwd",
            in_specs=[sp['qkv']] * 3 + [sp['small'], sp['row'], sp['ec'], sp['en'], sp['em'], sp['qkv'], sp['c'], sp['n'], sp['m']],
            out_specs=[sp['qkv']] * 3 + [sp['small'], sp['row'], sp['c'], sp['n'], sp['m']],
            out_shape=[jax.ShapeDtypeStruct(q.shape, F32)] * 3 + [jax.ShapeDtypeStruct((n_rows, LANES), F32),
                                                                  jax.ShapeDtypeStruct((1, LANES), F32)] + st_shapes,
            scratch_shapes=scratch + shared_scratch + shared_scratch, compiler_params=_params(vmem),
        )(q, k, v, small, gate_row, ec, en, em, do, dcf, dnf, dmf)

    @jax.custom_vjp
    def op(*args):
        return tuple(fwd_call(*args)[:4])

    def op_fwd(*args):
        res = fwd_call(*args)
        return tuple(res[:4]), (args[:5], tuple(res[4:]))

    op.defvjp(op_fwd, lambda res, g: tuple(bwd_call(*res[0], *res[1], *g)))
    return op(q, k, v, small, gate_row, c0, n0, m0)


def _f_modulate(x, shift, scale):
    return (_layernorm_rows(x) * (1.0 + scale) + shift,)


def _f_resid_ln(x, o, gate, bias, ln_g, ln_b):
    return (_layernorm_rows(DN_ALPHA * x + gate * (o + bias)) * ln_g + ln_b,)


def _f_lru_gates(xc, w_r, b_r, w_i, b_i, lam):
    outs = []
    for d in range(2):
        def blockdiag(w):
            return jnp.concatenate(
                [_mm_nn(xc[:, n * LRU_BS:(n + 1) * LRU_BS], w[(d * LRU_BLOCKS + n) * LRU_BS:(d * LRU_BLOCKS + n + 1) * LRU_BS, :])
                 for n in range(LRU_BLOCKS)], axis=1)

        r = jax.nn.sigmoid(blockdiag(w_r) + b_r[d:d + 1])
        i = jax.nn.sigmoid(blockdiag(w_i) + b_i[d:d + 1])
        log_a = -LRU_C * r * jax.nn.softplus(-lam[d:d + 1])
        outs += [jnp.exp(log_a), jnp.sqrt(1.0 - jnp.exp(2.0 * log_a)) * i * xc]
    return tuple(outs)


def _f_lru_out(h_f, h_b, ly):
    return ((h_f + h_b) * jax.nn.gelu(ly),)


def _f_ssd_post(y_f, y_b, xs, z, d_exp, norm_w):
    y = (y_f + y_b + xs * d_exp) * jax.nn.silu(z)
    gw = SSD_INNER // SSD_GROUPS
    parts = []
    for g in range(SSD_GROUPS):
        yg = y[:, g * gw:(g + 1) * gw]
        parts.append(yg * lax.rsqrt(jnp.mean(jnp.square(yg), -1, keepdims=True) + LN_EPS))
    return (jnp.concatenate(parts, axis=1) * norm_w,)


def _f_ml_post(h_f, h_b, o, norm_w):
    h = h_f + h_b
    parts = [_layernorm_rows(h[:, i * ML_HD:(i + 1) * ML_HD]) for i in range(ML_HEADS)]
    return (jnp.concatenate(parts, axis=1) * norm_w * jax.nn.sigmoid(o),)


def _f_merge(ga, gb, gc, pa, pb, pc):
    return (jax.nn.sigmoid(ga) * pa + jax.nn.sigmoid(gb) * pb + jax.nn.sigmoid(gc) * pc,)


def _f_relu2(pre, bias):
    return (jnp.square(jax.nn.relu(pre + bias)),)


def _lane_row(vec, start):
    return jnp.pad(vec[None], ((0, 0), (start, LANES - start - vec.shape[0])))


def _mixer(tag, h, p, states):
    (lru_s, ssd_s, ml_s) = states
    lx, ly, sz, xs, bm, cm, mq, mk, mv, mo, ga, gb, gc = _linear(tag + "in", h, p['w_in_main'], _IN_MAIN_WIDTHS)
    small = _linear(tag + "insmall", h, p['w_in_small'])

    xc = _dwconv(tag + "lruconv", lx, p['lru_conv_w'], p['lru_conv_b'][None], False)
    a_f, b_f, a_b, b_b = _rowwise(
        tag + "lrugate", _f_lru_gates, [xc],
        [p['lru_w_r'].reshape(2 * LRU_BLOCKS * LRU_BS, LRU_BS), p['lru_b_r'], p['lru_w_i'].reshape(2 * LRU_BLOCKS * LRU_BS, LRU_BS),
         p['lru_b_i'], p['lru_lambda']], [D_MODEL] * 4)
    h_f, s_f = _lin_scan(tag + "lruscanf", a_f, b_f, lru_s[0], False)
    h_b, s_b = _lin_scan(tag + "lruscanb", a_b, b_b, lru_s[1], True)
    (ya,) = _rowwise(tag + "lruout", _f_lru_out, [h_f, h_b, ly], [], [D_MODEL])

    cw, cb_ = p['ssd_conv_w'], p['ssd_conv_b'][None]
    xs_c = _dwconv(tag + "ssdconvx", xs, cw[:, :2048], cb_[:, :2048], True)
    bm_c = _dwconv(tag + "ssdconvb", bm, cw[:, 2048:3072], cb_[:, 2048:3072], True)
    cm_c = _dwconv(tag + "ssdconvc", cm, cw[:, 3072:], cb_[:, 3072:], True)
    ssd_new, ys = [], []
    for d in range(2):
        y_d, st_d = _ssd_scan(tag + "ssd%d" % d, xs_c, bm_c, cm_c, small, _lane_row(p['ssd_dt_bias'][d], _DT_LANE + 32 * d),
                              _lane_row(p['ssd_a_log'][d], _DT_LANE + 32 * d), ssd_s[d], d, d == 1)
        ys.append(y_d)
        ssd_new.append(st_d)
    (yb,) = _rowwise(tag + "ssdpost", _f_ssd_post, [ys[0], ys[1], xs_c, sz],
                     [jnp.repeat(p['ssd_d'], SSD_HEADDIM)[None], p['ssd_norm_w'][None]], [SSD_INNER])

    mw, mb = p['ml_conv_w'], p['ml_conv_b'][None]
    q_c = _dwconv(tag + "mlconvq", mq, mw[:, :1024], mb[:, :1024], True)
    k_c = _dwconv(tag + "mlconvk", mk, mw[:, 1024:], mb[:, 1024:], True)
    gate_row = _lane_row(p['ml_gate_b'].reshape(4 * ML_HEADS), _MG_LANE)
    ml_new, hs = [], []
    for d in range(2):
        o_d, c_d, n_d, m_d = _ml_scan(tag + "ml%d" % d, q_c, k_c, mv, small, gate_row, *ml_s[d], d, d == 1)
        hs.append(o_d)
        ml_new.append((c_d, n_d, m_d))
    (yc,) = _rowwise(tag + "mlpost", _f_ml_post, [hs[0], hs[1], mo], [p['ml_norm_w'][None]], [D_MODEL])
    return (ya, yb, yc, ga, gb, gc), ((s_f, s_b), tuple(ssd_new), tuple(ml_new))


def _merge(tag, br, p):
    ya, yb, yc, ga, gb, gc = br
    pa = _linear(tag + "bra", ya, p['w_br_a'])
    pb = _linear(tag + "brb", yb, p['w_br_b'])
    pc = _linear(tag + "brc", yc, p['w_br_c'])
    (m,) = _rowwise(tag + "merge", _f_merge, [ga, gb, gc, pa, pb, pc], [], [D_MODEL])
    return _linear(tag + "out", m, p['w_out'])


def _sublayers(tag, xin, o, mods, p, l):
    sh2, sc2, g1, g2 = mods
    (x1,) = _rowwise(tag + "ln1", _f_resid_ln, [xin, o], [g1, p['b_out'][None], p['ln1_g'][None], p['ln1_b'][None]], [D_MODEL])
    (h2,) = _rowwise(tag + "mod2", _f_modulate, [x1], [sh2, sc2], [D_MODEL])
    pre = _linear(tag + "ff1", h2, p['w_ff1'])
    (u2,) = _rowwise(tag + "relu2", _f_relu2, [pre], [p['b_ff1'][None]], [D_FF])
    o2 = _linear(tag + "ff2", u2, p['w_ff2'])
    (x2,) = _rowwise(tag + "ln2", _f_resid_ln, [x1, o2], [g2, p['b_ff2'][None], p['ln2_g'][None], p['ln2_b'][None]], [D_MODEL])
    return x2


def _to_col_major(h):
    s, d = h.shape
    return h.reshape(s // GRID_W, GRID_W, d).swapaxes(0, 1).reshape(s, d)


def _from_col_major(h):
    s, d = h.shape
    return h.reshape(GRID_W, s // GRID_W, d).swapaxes(0, 1).reshape(s, d)


def _forward(x, wts, cvec, ctx):
    zeros = lambda *s: jnp.zeros(s, F32)
    ctx_init = ((zeros(1, D_MODEL), zeros(1, D_MODEL)),
                (zeros(SSD_INNER, SSD_STATE), zeros(SSD_INNER, SSD_STATE)),
                tuple((zeros(ML_HEADS * ML_HD, ML_HD), zeros(ML_HEADS, 1, ML_HD), zeros(ML_HEADS, 1, LANES)) for _ in range(2)))
    for l in range(DEPTH):
        p = {n: (wts[n] if n == 'c_ctx' else wts[n][l]) for n in wts}
        tag = "l%d" % l
        cc = jnp.concatenate([cvec, p['c_ctx'][None], jnp.zeros((SUBLANES - 2, D_MODEL), F32)], axis=0)
        mod = _linear(tag + "ada", jax.nn.silu(cc), p['w_ada']) + p['b_ada'][None]
        sh1x, sc1x, g1x, sh2x, sc2x, g2x = [mod[0:1, i * D_MODEL:(i + 1) * D_MODEL] for i in range(6)]
        sh1c, sc1c, g1c, sh2c, sc2c, g2c = [mod[1:2, i * D_MODEL:(i + 1) * D_MODEL] for i in range(6)]
        (hc,) = _rowwise(tag + "cmod1", _f_modulate, [ctx], [sh1c, sc1c], [D_MODEL])
        br_c, ctx_states = _mixer(tag + "c", hc, p, ctx_init)
        (hx,) = _rowwise(tag + "xmod1", _f_modulate, [x], [sh1x, sc1x], [D_MODEL])
        if l % 2 == 1:
            hx = _to_col_major(hx)
        br_x, _ = _mixer(tag + "x", hx, p, ctx_states)
        ox = _merge(tag + "x", br_x, p)
        if l % 2 == 1:
            ox = _from_col_major(ox)
        x = _sublayers(tag + "x", x, ox, (sh2x, sc2x, g1x, g2x), p, l)
        if l < DEPTH - 1:
            ctx = _sublayers(tag + "c", ctx, _merge(tag + "c", br_c, p), (sh2c, sc2c, g1c, g2c), p, l)
    return x


def _loss_and_cotangent(y, target):
    n_rows, d = y.shape
    tt = _row_tile(n_rows, 0, cap=256)

    def kern(y_ref, t_ref, dy_ref, acc_ref):
        @pl.when(pl.program_id(0) == 0)
        def _():
            acc_ref[...] = jnp.zeros_like(acc_ref)

        err = y_ref[...] - t_ref[...]
        dy_ref[...] = err * (1.0 / d)
        acc_ref[...] += jnp.sum(jnp.square(err))

    spec = pl.BlockSpec((tt, d), lambda i: (i, 0))
    dy, acc = pl.pallas_call(
        kern, grid=(n_rows // tt,), name="loss", in_specs=[spec, spec],
        out_specs=[spec, pl.BlockSpec((SUBLANES, LANES), lambda i: (0, 0))],
        out_shape=[jax.ShapeDtypeStruct((n_rows, d), F32), jax.ShapeDtypeStruct((SUBLANES, LANES), F32)],
    )(y, target)
    return acc[0, 0] * (0.5 / d), dy


def _exchange(name, src, gather):
    slab = src.shape if gather else src.shape[1:]

    def body(src_ref, out_ref, send_sems, recv_sems, local_sem):
        x, y, c = lax.axis_index("x"), lax.axis_index("y"), lax.axis_index("c")
        me = 4 * x + 2 * y + c
        local = pltpu.make_async_copy(src_ref if gather else src_ref.at[me], out_ref.at[me], local_sem)
        local.start()
        copies = []
        for d in range(1, N_DEV):
            px, py, pc = lax.rem(x + (d >> 2), 2), lax.rem(y + ((d >> 1) & 1), 2), lax.rem(c + (d & 1), 2)
            peer = 4 * px + 2 * py + pc
            cp = pltpu.make_async_remote_copy(
                src_ref=src_ref if gather else src_ref.at[peer], dst_ref=out_ref.at[me],
                send_sem=send_sems.at[d - 1], recv_sem=recv_sems.at[d - 1],
                device_id=(px, py, pc), device_id_type=pl.DeviceIdType.MESH)
            cp.start()
            copies.append(cp)
        for cp in copies:
            cp.wait()
        local.wait()

    return pl.pallas_call(
        body, name=name, out_shape=jax.ShapeDtypeStruct((N_DEV,) + tuple(slab), src.dtype),
        in_specs=[pl.BlockSpec(memory_space=pl.ANY)], out_specs=pl.BlockSpec(memory_space=pl.ANY),
        scratch_shapes=[pltpu.SemaphoreType.DMA((N_DEV - 1,)), pltpu.SemaphoreType.DMA((N_DEV - 1,)), pltpu.SemaphoreType.DMA],
    )(src)


_HBM = pl.BlockSpec(memory_space=pl.ANY)
_CHIPS = ((0, 0), (0, 1), (1, 0), (1, 1))


def _gather_two_level(name, src):
    def body(src_ref, out_ref, send_sems, recv_sems, local_sem):
        x, y, c = lax.axis_index("x"), lax.axis_index("y"), lax.axis_index("c")
        me, sibling = (x, y, c), (x, y, 1 - c)
        chips = [(1 - x, y), (x, 1 - y), (1 - x, 1 - y)]

        def slab(px, py, pc):
            return out_ref.at[4 * px + 2 * py + pc]

        def copy(k, block, to, src=None):
            return pltpu.make_async_remote_copy(
                src_ref=slab(*block) if src is None else src, dst_ref=slab(*block), send_sem=send_sems.at[k],
                recv_sem=recv_sems.at[k], device_id=to, device_id_type=pl.DeviceIdType.MESH)

        mine = pltpu.make_async_copy(src_ref, slab(*me), local_sem)
        mine.start()
        first = [copy(0, me, sibling, src=src_ref)] + [copy(1 + j, me, (*chip, c), src=src_ref) for j, chip in enumerate(chips)]
        for cp in first:
            cp.start()
        passed = [copy(4 + j, (*chip, c), sibling) for j, chip in enumerate(chips)]
        for j, chip in enumerate(chips):
            copy(1 + j, (*chip, c), me).wait_recv()
            passed[j].start()
        copy(0, sibling, me).wait_recv()
        for j, chip in enumerate(chips):
            copy(4 + j, (*chip, 1 - c), me).wait_recv()
        for cp in first + passed:
            cp.wait_send()
        mine.wait()

    return pl.pallas_call(
        body, name=name, out_shape=jax.ShapeDtypeStruct((N_DEV,) + tuple(src.shape), src.dtype),
        in_specs=[_HBM], out_specs=_HBM,
        scratch_shapes=[pltpu.SemaphoreType.DMA((N_DEV - 1,)), pltpu.SemaphoreType.DMA((N_DEV - 1,)), pltpu.SemaphoreType.DMA],
    )(src)


def _scatter_to_sibling(name, parts):
    def body(p_ref, out_ref, send_sems, recv_sems):
        x, y, c = lax.axis_index("x"), lax.axis_index("y"), lax.axis_index("c")
        copies = []
        for j, (px, py) in enumerate(_CHIPS):
            cp = pltpu.make_async_remote_copy(
                src_ref=p_ref.at[4 * px + 2 * py + (1 - c)], dst_ref=out_ref.at[j], send_sem=send_sems.at[j],
                recv_sem=recv_sems.at[j], device_id=(x, y, 1 - c), device_id_type=pl.DeviceIdType.MESH)
            cp.start()
            copies.append(cp)
        for cp in copies:
            cp.wait()

    return pl.pallas_call(
        body, name=name, out_shape=jax.ShapeDtypeStruct((4,) + tuple(parts.shape[1:]), parts.dtype),
        in_specs=[_HBM], out_specs=_HBM,
        scratch_shapes=[pltpu.SemaphoreType.DMA((4,)), pltpu.SemaphoreType.DMA((4,))],
    )(parts)


def _chip_sum(name, parts, from_sibling):
    _, rows, cols = parts.shape
    lanes = -(-cols // LANES) * LANES
    tr = _row_tile(rows, 4 * lanes * 4 * 2, budget=12 << 20)

    def kern(p_ref, s_ref, o_ref):
        c = lax.axis_index("c")
        o_ref[0] = (jnp.where(c == 0, p_ref[0, 0], p_ref[0, 1]) + s_ref[0]).astype(o_ref.dtype)

    return pl.pallas_call(
        kern, grid=(4, rows // tr), name=name,
        in_specs=[pl.BlockSpec((1, 2, tr, cols), lambda j, i: (j, 0, i, 0)), pl.BlockSpec((1, tr, cols), lambda j, i: (j, i, 0))],
        out_specs=pl.BlockSpec((1, tr, cols), lambda j, i: (j, i, 0)),
        out_shape=jax.ShapeDtypeStruct((4, rows, cols), BF16),
        compiler_params=_params(4 * lanes * tr * 4 * 2),
    )(parts.reshape(4, 2, rows, cols), from_sibling)


def _scatter_across_chips(name, sums):
    def body(q_ref, out_ref, send_sems, recv_sems, local_sem):
        x, y, c = lax.axis_index("x"), lax.axis_index("y"), lax.axis_index("c")
        own = 2 * x + y
        local = pltpu.make_async_copy(q_ref.at[own], out_ref.at[own], local_sem)
        local.start()
        copies = []
        for d in range(1, 4):
            px, py = lax.rem(x + (d >> 1), 2), lax.rem(y + (d & 1), 2)
            cp = pltpu.make_async_remote_copy(
                src_ref=q_ref.at[2 * px + py], dst_ref=out_ref.at[own], send_sem=send_sems.at[d - 1],
                recv_sem=recv_sems.at[d - 1], device_id=(px, py, c), device_id_type=pl.DeviceIdType.MESH)
            cp.start()
            copies.append(cp)
        for cp in copies:
            cp.wait()
        local.wait()

    return pl.pallas_call(
        body, name=name, out_shape=jax.ShapeDtypeStruct(sums.shape, sums.dtype), in_specs=[_HBM], out_specs=_HBM,
        scratch_shapes=[pltpu.SemaphoreType.DMA((3,)), pltpu.SemaphoreType.DMA((3,)), pltpu.SemaphoreType.DMA],
    )(sums)


def _sum_parts(name, parts):
    n_parts, rows, cols = parts.shape
    tr = _row_tile(rows, 4 * cols * (n_parts + 1) * 2)

    def kern(p_ref, o_ref):
        acc = p_ref[0]
        for k in range(1, n_parts):
            acc = acc + p_ref[k]
        o_ref[...] = acc

    return pl.pallas_call(
        kern, grid=(rows // tr,), name=name, in_specs=[pl.BlockSpec((n_parts, tr, cols), lambda i: (0, i, 0))],
        out_specs=pl.BlockSpec((tr, cols), lambda i: (i, 0)), out_shape=jax.ShapeDtypeStruct((rows, cols), F32),
    )(parts)


def _adamw(name, w, m, v, parts):
    n_parts, rows, cols = parts.shape
    lanes = -(-cols // LANES) * LANES
    tr = _row_tile(rows, 4 * lanes * (n_parts + 7) * 2, budget=16 << 20)
    c1 = np.float32(1.0 - ADAM_B1 ** ADAM_STEP)
    c2 = np.float32(1.0 - ADAM_B2 ** ADAM_STEP)

    def kern(w_ref, m_ref, v_ref, p_ref, g_ref, d_ref, nm_ref, nv_ref):
        g = p_ref[0].astype(F32)
        for k in range(1, n_parts):
            g = g + p_ref[k].astype(F32)
        m_new = ADAM_B1 * m_ref[...] + (1.0 - ADAM_B1) * g
        v_new = ADAM_B2 * v_ref[...] + (1.0 - ADAM_B2) * jnp.square(g)
        g_ref[...] = g
        nm_ref[...] = m_new
        nv_ref[...] = v_new
        d_ref[...] = -ADAM_LR * ((m_new / c1) / (jnp.sqrt(v_new / c2) + ADAM_EPS) + ADAM_WD * w_ref[...])

    spec = pl.BlockSpec((tr, cols), lambda i: (i, 0))
    return pl.pallas_call(
        kern, grid=(rows // tr,), name=name,
        in_specs=[spec, spec, spec, pl.BlockSpec((n_parts, tr, cols), lambda i: (0, i, 0))], out_specs=[spec] * 4,
        out_shape=[jax.ShapeDtypeStruct((rows, cols), F32)] * 4,
        compiler_params=_params(4 * lanes * tr * (n_parts + 7) * 2),
    )(w, m, v, parts)


def _packed_rows(shape):
    return -(-int(np.prod(shape)) // (SUBLANES * LANES)) * SUBLANES


def _pack(arrays, row_multiple):
    parts = []
    for a in arrays:
        n = int(np.prod(a.shape))
        r = _packed_rows(a.shape)
        parts.append(jnp.pad(a.reshape(-1), (0, r * LANES - n)).reshape(r, LANES))
    rows = sum(p.shape[0] for p in parts)
    total = -(-rows // row_multiple) * row_multiple
    if total > rows:
        parts.append(jnp.zeros((total - rows, LANES), arrays[0].dtype))
    return jnp.concatenate(parts, axis=0)


def _unpack(packed, shapes):
    out, off = [], 0
    for s in shapes:
        r = _packed_rows(s)
        out.append(packed[off:off + r].reshape(-1)[:int(np.prod(s))].reshape(s))
        off += r
    return out


def _split_w_in(w_in):
    main = jnp.concatenate([w_in[:, :, s:e] for s, e in _IN_MAIN], axis=2)
    pad = jnp.zeros(w_in.shape[:2] + (LANES - 80,), w_in.dtype)
    small = jnp.concatenate([w_in[:, :, s:e] for s, e in _IN_SMALL] + [pad], axis=2)
    return main, small


def _join_w_in(main, small):
    return jnp.concatenate([main[:, :, 0:8192], small[:, :, 0:64], main[:, :, 8192:12288], small[:, :, 64:80],
                            main[:, :, 12288:15360]], axis=2)


def _unshard(gathered, axis):
    nd, nl, r, c = gathered.shape
    if axis == 1:
        return gathered.transpose(1, 0, 2, 3).reshape(nl, nd * r, c)
    return gathered.transpose(1, 2, 0, 3).reshape(nl, r, nd * c)


def _reshard(full, axis):
    nl, r, c = full.shape
    if axis == 1:
        return full.reshape(nl, N_DEV, r // N_DEV, c).transpose(1, 0, 2, 3)
    return full.reshape(nl, r, N_DEV, c // N_DEV).transpose(2, 0, 1, 3)


def kernel(x, c, ctx, c_ctx, w_ada, b_ada, w_in, lru_conv_w, lru_conv_b, lru_w_r, lru_b_r, lru_w_i, lru_b_i, lru_lambda, ssd_conv_w, ssd_conv_b, ssd_dt_bias, ssd_a_log, ssd_d, ssd_norm_w, ml_conv_w, ml_conv_b, ml_gate_b, ml_norm_w, w_br_a, w_br_b, w_br_c, w_out, b_out, ln1_g, ln1_b, w_ff1, b_ff1, w_ff2, b_ff2, ln2_g, ln2_b, loss_target, m_c_ctx, m_w_ada, m_b_ada, m_w_in, m_lru_conv_w, m_lru_conv_b, m_lru_w_r, m_lru_b_r, m_lru_w_i, m_lru_b_i, m_lru_lambda, m_ssd_conv_w, m_ssd_conv_b, m_ssd_dt_bias, m_ssd_a_log, m_ssd_d, m_ssd_norm_w, m_ml_conv_w, m_ml_conv_b, m_ml_gate_b, m_ml_norm_w, m_w_br_a, m_w_br_b, m_w_br_c, m_w_out, m_b_out, m_ln1_g, m_ln1_b, m_w_ff1, m_b_ff1, m_w_ff2, m_b_ff2, m_ln2_g, m_ln2_b, v_c_ctx, v_w_ada, v_b_ada, v_w_in, v_lru_conv_w, v_lru_conv_b, v_lru_w_r, v_lru_b_r, v_lru_w_i, v_lru_b_i, v_lru_lambda, v_ssd_conv_w, v_ssd_conv_b, v_ssd_dt_bias, v_ssd_a_log, v_ssd_d, v_ssd_norm_w, v_ml_conv_w, v_ml_conv_b, v_ml_gate_b, v_ml_norm_w, v_w_br_a, v_w_br_b, v_w_br_c, v_w_out, v_b_out, v_ln1_g, v_ln1_b, v_w_ff1, v_b_ff1, v_w_ff2, v_b_ff2, v_ln2_g, v_ln2_b):
    a = dict(locals())
    me = 4 * lax.axis_index("x") + 2 * lax.axis_index("y") + lax.axis_index("c")

    wts = {n: a[n] for n in _REPLICATED}
    for n, axis in _BIG.items():
        full = _unshard(_gather_two_level("gather_" + n, a[n].astype(BF16)), axis)
        if n == 'w_in':
            main, small = _split_w_in(full)
            wts['w_in_main'], wts['w_in_small'] = main.astype(F32), small.astype(F32)
        else:
            wts[n] = full.astype(F32)
    small_shapes = [a[n].shape for n in _SMALL_SHARDED]
    small_all = _exchange("gather_small", _pack([a[n] for n in _SMALL_SHARDED], SUBLANES), True)
    per_dev = [_unpack(small_all[k], small_shapes) for k in range(N_DEV)]
    for i, n in enumerate(_SMALL_SHARDED):
        wts[n] = jnp.concatenate([per_dev[k][i] for k in range(N_DEV)], axis=-1)

    y, vjp = jax.vjp(functools.partial(_forward, cvec=c, ctx=ctx[0]), x[0], wts)
    loss_local, dy = _loss_and_cotangent(y, loss_target[0])
    grad_x, grads = vjp(dy)
    grads['w_in'] = _join_w_in(grads.pop('w_in_main'), grads.pop('w_in_small'))
    loss = lax.psum(loss_local, ("x", "y", "c"))

    out = {}

    def put(n, res, shape):
        for kind, r in zip(("grad_", "delta_", "new_m_", "new_v_"), res):
            out[kind + n] = r.reshape(shape)

    for n, axis in _BIG.items():
        shp = a[n].shape
        rows, cols = shp[0] * shp[1], shp[2]
        parts = _reshard(grads[n], axis).reshape(N_DEV, rows, cols)
        chip = _chip_sum("chipsum_" + n, parts, _scatter_to_sibling("scatter_d2d_" + n, parts))
        parts = _scatter_across_chips("scatter_ici_" + n, chip)
        put(n, _adamw("adamw_" + n, a[n].reshape(rows, cols), a["m_" + n].reshape(rows, cols), a["v_" + n].reshape(rows, cols), parts), shp)

    rep_names = _REPLICATED + _SMALL_SHARDED
    chunk_rows = SUBLANES * N_DEV
    g_pack = _pack([grads[n] for n in rep_names], chunk_rows * N_DEV)
    rows = g_pack.shape[0]
    parts = _exchange("scatter_rep", g_pack.reshape(N_DEV, rows // N_DEV, LANES), False)
    mine = _sum_parts("sum_rep", parts)
    g_all = _exchange("gather_rep", mine, True).reshape(rows, LANES)
    g_full = _unpack(g_all, [grads[n].shape for n in rep_names])
    g_local = []
    for n, g in zip(rep_names, g_full):
        if n in _SMALL_SHARDED:
            width = a[n].shape[-1]
            g = lax.dynamic_slice_in_dim(g, me * width, width, axis=g.ndim - 1)
        g_local.append(g)
    shapes = [a[n].shape for n in rep_names]
    res = _adamw("adamw_rep", _pack([a[n] for n in rep_names], chunk_rows), _pack([a["m_" + n] for n in rep_names], chunk_rows),
                 _pack([a["v_" + n] for n in rep_names], chunk_rows), _pack(g_local, chunk_rows)[None])
    unpacked = [_unpack(r, shapes) for r in res]
    for i, n in enumerate(rep_names):
        put(n, [u[i] for u in unpacked], shapes[i])

    outs = [loss, grad_x[None]]
    for kind in ("grad_", "delta_", "new_m_", "new_v_"):
        outs += [out[kind + n] for n in _WEIGHTS]
    return tuple(outs)
```

```python
import functools

import numpy as np
import jax
import jax.numpy as jnp
from jax import lax
from jax.experimental import pallas as pl
from jax.experimental.pallas import tpu as pltpu

F32 = jnp.float32
BF16 = jnp.bfloat16

N_DEV = 8
D_MODEL = 1024
DEPTH = 2
GRID_W = 64
CHUNK = 128
LN_EPS = 1e-6
LRU_BLOCKS = 8
LRU_BS = 128
LRU_C = 8.0
SSD_INNER = 2048
SSD_GROUPS = 8
SSD_HPG = 4
SSD_HEADDIM = 64
SSD_STATE = 128
ML_HEADS = 4
ML_HD = 256
D_FF = 4096
DN_ALPHA = (2 * DEPTH) ** 0.25
ADAM_LR, ADAM_B1, ADAM_B2, ADAM_EPS, ADAM_WD, ADAM_STEP = 0.001, 0.9, 0.999, 1e-08, 0.01, 10

VMEM_CAP = 60 * 1024 * 1024
SUBLANES = 8
LANES = 128

_IN_MAIN = ((0, 8192), (8256, 12352), (12368, 15440))
_IN_MAIN_WIDTHS = (1024, 1024, 2048, 2048, 1024, 1024, 1024, 1024, 1024, 1024, 1024, 1024, 1024)
_IN_SMALL = ((8192, 8256), (12352, 12368))
_DT_LANE = 0
_MG_LANE = 64

_WEIGHTS = ['c_ctx', 'w_ada', 'b_ada', 'w_in', 'lru_conv_w', 'lru_conv_b', 'lru_w_r', 'lru_b_r', 'lru_w_i', 'lru_b_i',
            'lru_lambda', 'ssd_conv_w', 'ssd_conv_b', 'ssd_dt_bias', 'ssd_a_log', 'ssd_d', 'ssd_norm_w', 'ml_conv_w',
            'ml_conv_b', 'ml_gate_b', 'ml_norm_w', 'w_br_a', 'w_br_b', 'w_br_c', 'w_out', 'b_out', 'ln1_g', 'ln1_b',
            'w_ff1', 'b_ff1', 'w_ff2', 'b_ff2', 'ln2_g', 'ln2_b']
_BIG = {'w_ada': 2, 'w_in': 2, 'w_ff1': 2, 'w_br_a': 1, 'w_br_b': 1, 'w_br_c': 1, 'w_out': 1, 'w_ff2': 1}
_SMALL_SHARDED = ['lru_conv_w', 'lru_b_r', 'lru_b_i', 'lru_lambda', 'ssd_conv_w', 'ml_conv_w']
_REPLICATED = [n for n in _WEIGHTS if n not in _BIG and n not in _SMALL_SHARDED]


def _params(vmem_bytes):
    return pltpu.CompilerParams(vmem_limit_bytes=int(min(max(2 * vmem_bytes, 32 << 20), VMEM_CAP)))


def _row_tile(n_rows, bytes_per_row, budget=6 << 20, cap=512):
    t = cap
    while t > SUBLANES and (t * bytes_per_row > budget or n_rows % t):
        t //= 2
    assert n_rows % t == 0, (n_rows, t)
    return t


def _dg(a, b, ca, cb):
    return lax.dot_general(a.astype(BF16), b.astype(BF16), (((ca,), (cb,)), ((), ())), preferred_element_type=F32)


def _make_bdot(ca, cb):
    @jax.custom_vjp
    def f(a, b):
        return _dg(a, b, ca, cb)

    def fwd(a, b):
        return _dg(a, b, ca, cb), (a, b)

    def bwd(res, g):
        a, b = res
        da = _dg(g, b, 1, 1 - cb) if ca == 1 else _dg(b, g, 1 - cb, 1)
        db = _dg(a, g, 1 - ca, 0) if cb == 0 else _dg(g, a, 0, 1 - ca)
        return da, db

    f.defvjp(fwd, bwd)
    return f


_mm_nn = _make_bdot(1, 0)
_mm_nt = _make_bdot(1, 1)
_mm_tn = _make_bdot(0, 0)


@jax.custom_vjp
def _round_bf16(x):
    return x.astype(BF16).astype(F32)


_round_bf16.defvjp(lambda x: (_round_bf16(x), None), lambda _, g: (g,))


def _exact_dot(a, b):
    return jnp.dot(a, b, precision=lax.Precision.HIGHEST, preferred_element_type=F32)


def _layernorm_rows(x):
    mu = jnp.mean(x, -1, keepdims=True)
    var = jnp.mean(jnp.square(x - mu), -1, keepdims=True)
    return (x - mu) * lax.rsqrt(var + LN_EPS)


def _rowwise_calls(name, f, rows, params, out_widths, out_dtype=F32):
    nr, npar, no = len(rows), len(params), len(out_widths)
    n_rows = rows[0].shape[0]
    row_w = [r.shape[1] for r in rows]
    par_bytes = sum(int(np.prod(p.shape)) * 4 for p in params)
    tile = _row_tile(n_rows, 4 * (2 * sum(row_w) + 2 * sum(out_widths)))
    grid = (n_rows // tile,)

    def row_spec(w):
        return pl.BlockSpec((tile, w), lambda i: (i, 0))

    def par_spec(p):
        return pl.BlockSpec(p.shape, lambda i: (0, 0))

    vmem = 2 * tile * 4 * (2 * sum(row_w) + 3 * sum(out_widths)) + 4 * par_bytes

    def fwd_call(rows, params):
        def kern(*refs):
            outs = f(*[r[...] for r in refs[:nr + npar]])
            for r, o in zip(refs[nr + npar:], outs):
                r[...] = o.astype(out_dtype)

        return pl.pallas_call(
            kern, grid=grid, name=name + "_fwd",
            in_specs=[row_spec(w) for w in row_w] + [par_spec(p) for p in params],
            out_specs=[row_spec(w) for w in out_widths],
            out_shape=[jax.ShapeDtypeStruct((n_rows, w), out_dtype) for w in out_widths],
            compiler_params=_params(vmem),
        )(*rows, *params)

    def bwd_call(rows, params, gouts):
        def kern(*refs):
            ins = [r[...] for r in refs[:nr + npar]]
            gs = tuple(r[...] for r in refs[nr + npar:nr + npar + no])
            grads = jax.vjp(f, *ins)[1](gs)
            drefs = refs[nr + npar + no:]
            for k in range(nr):
                drefs[k][...] = grads[k]

            @pl.when(pl.program_id(0) == 0)
            def _():
                for k in range(npar):
                    drefs[nr + k][...] = jnp.zeros_like(drefs[nr + k])

            for k in range(npar):
                drefs[nr + k][...] += grads[nr + k]

        res = pl.pallas_call(
            kern, grid=grid, name=name + "_bwd",
            in_specs=[row_spec(w) for w in row_w] + [par_spec(p) for p in params] + [row_spec(w) for w in out_widths],
            out_specs=[row_spec(w) for w in row_w] + [par_spec(p) for p in params],
            out_shape=[jax.ShapeDtypeStruct(r.shape, F32) for r in rows] + [jax.ShapeDtypeStruct(p.shape, F32) for p in params],
            compiler_params=_params(vmem),
        )(*rows, *params, *gouts)
        return tuple(res[:nr]), tuple(res[nr:])

    return fwd_call, bwd_call


def _rowwise(name, f, rows, params, out_widths):
    rows, params = tuple(rows), tuple(params)
    fwd_call, bwd_call = _rowwise_calls(name, f, rows, params, out_widths)

    @jax.custom_vjp
    def op(rows, params):
        return tuple(fwd_call(rows, params))

    op.defvjp(lambda r, p: (tuple(fwd_call(r, p)), (r, p)), lambda res, g: bwd_call(res[0], res[1], g))
    return op(rows, params)


def _rowwise_linear(name, f, rows, params, weights):
    rows, params = tuple(rows), tuple(params)
    ws = tuple(w for w, _ in weights)
    m, k = rows[0].shape[0], ws[0].shape[0]
    row_fwd, row_bwd = _rowwise_calls(name, f, rows, params, [k], BF16)
    lin = [_linear_calls(name + "lin%d" % i, m, k, w.shape[1], wd, BF16) for i, (w, wd) in enumerate(weights)]
    counts = [len(c[3]) for c in lin]

    def fwd(rows, params, ws):
        (a,) = row_fwd(rows, params)
        outs = []
        for (fwd_call, _, _, _), w in zip(lin, ws):
            outs += list(fwd_call(a, w))
        return tuple(outs), a

    @jax.custom_vjp
    def op(rows, params, ws):
        return fwd(rows, params, ws)[0]

    def op_fwd(rows, params, ws):
        outs, a = fwd(rows, params, ws)
        return outs, (rows, params, ws, a)

    def op_bwd(res, g):
        rows, params, ws, a = res
        da, dws, off = None, [], 0
        for (_, dgrad_call, wgrad_call, _), w, cnt in zip(lin, ws, counts):
            gk = g[off:off + cnt]
            off += cnt
            d = dgrad_call(w, gk)
            da = d if da is None else da + d
            dws.append(wgrad_call(a, gk))
        drows, dparams = row_bwd(rows, params, (da,))
        return drows, dparams, tuple(dws)

    op.defvjp(op_fwd, op_bwd)
    return op(rows, params, ws)


def _group_ranges(widths, tn):
    starts, s = [], 0
    for w in widths:
        assert w % tn == 0, (w, tn)
        starts.append((s // tn, (s + w) // tn))
        s += w
    return starts, s // tn


def _group_tile(refs, ranges, row_tile, col_tile, i, j):
    out = []
    for ref, (s, e) in zip(refs, ranges):
        cols = pl.ds(pl.multiple_of((j - s) * col_tile, col_tile), col_tile)
        out.append(((j >= s) & (j < e), ref, cols))
    return [(p, lambda r=r, c=c: r.at[pl.ds(pl.multiple_of(i * row_tile, row_tile), row_tile), c]) for p, r, c in out]


def _linear_calls(name, m, k, n, widths, a_dtype):
    widths = (n,) if widths is None else tuple(widths)
    ng = len(widths)
    cast_a = a_dtype != BF16
    tn = 128 if n < 256 else (256 if k > 2048 or n % 512 else 512)
    tm = _row_tile(m, 0, cap=1024 if k <= 2048 else 512)
    ranges, nt = _group_ranges(widths, tn)
    mt = m // tm
    tn_w = 1024 if (k <= 1024 and all(wd % 1024 == 0 for wd in widths)) else (512 if all(wd % 512 == 0 for wd in widths) else tn)
    tm_w = _row_tile(m, 0, cap=512)
    ranges_w, nt_w = _group_ranges(widths, tn_w)
    mt_w = m // tm_w
    hbm = pl.BlockSpec(memory_space=pl.ANY)

    def fwd_call(a, w):
        n_steps = mt * nt

        def kern(a_ref, w_ref, *rest):
            outs, obuf, osem = rest[:ng], rest[ng], rest[ng + 1]
            a_bf = rest[ng + 2] if cast_a else a_ref
            i, j = pl.program_id(0), pl.program_id(1)
            step = i * nt + j
            slot = lax.rem(step, 2)

            def drain(sl):
                pltpu.make_async_copy(obuf.at[sl], outs[0].at[pl.ds(0, tm), pl.ds(0, tn)], osem.at[sl]).wait()

            if cast_a:
                @pl.when(j == 0)
                def _():
                    a_bf[...] = a_ref[...].astype(BF16)

            @pl.when(step >= 2)
            def _():
                drain(slot)

            obuf[slot] = jnp.dot(a_bf[...], w_ref[...], preferred_element_type=F32)
            for pred, window in _group_tile(outs, ranges, tm, tn, i, j):
                @pl.when(pred)
                def _(window=window):
                    pltpu.make_async_copy(obuf.at[slot], window(), osem.at[slot]).start()

            @pl.when(step == n_steps - 1)
            def _():
                drain(slot)
                if n_steps > 1:
                    drain(1 - slot)

        return pl.pallas_call(
            kern, grid=(mt, nt), name=name + "_fwd",
            in_specs=[pl.BlockSpec((tm, k), lambda i, j: (i, 0)), pl.BlockSpec((k, tn), lambda i, j: (0, j))],
            out_specs=[hbm] * ng,
            out_shape=[jax.ShapeDtypeStruct((m, wd), F32) for wd in widths],
            scratch_shapes=[pltpu.VMEM((2, tm, tn), F32), pltpu.SemaphoreType.DMA((2,))]
            + ([pltpu.VMEM((tm, k), BF16)] if cast_a else []),
            compiler_params=_params(10 * tm * k + 4 * k * tn + 8 * tm * tn),
        )(a, w.astype(BF16))

    def prefetched(gs, gbuf, gsem, rngs, row_tile, col_tile, step, n_steps, tile_of):
        slot = lax.rem(step, 2)

        def start(s_idx, sl):
            ii, jj = tile_of(s_idx)
            for pred, window in _group_tile(gs, rngs, row_tile, col_tile, ii, jj):
                @pl.when(pred)
                def _(window=window):
                    pltpu.make_async_copy(window(), gbuf.at[sl], gsem.at[sl]).start()

        @pl.when(step == 0)
        def _():
            start(step, slot)

        @pl.when(step + 1 < n_steps)
        def _():
            start(step + 1, 1 - slot)

        pltpu.make_async_copy(gs[0].at[pl.ds(0, row_tile), pl.ds(0, col_tile)], gbuf.at[slot], gsem.at[slot]).wait()
        return slot

    def dgrad_call(w, gouts):
        def kern(w_ref, *rest):
            gs, da, gbuf, gsem = rest[:ng], rest[ng], rest[ng + 1], rest[ng + 2]
            i, j = pl.program_id(0), pl.program_id(1)
            slot = prefetched(gs, gbuf, gsem, ranges, tm, tn, i * nt + j, mt * nt, lambda s: (s // nt, lax.rem(s, nt)))

            @pl.when(j == 0)
            def _():
                da[...] = jnp.zeros_like(da)

            da[...] += lax.dot_general(gbuf[slot].astype(BF16), w_ref[...], (((1,), (1,)), ((), ())), preferred_element_type=F32)

        return pl.pallas_call(
            kern, grid=(mt, nt), name=name + "_dgrad",
            in_specs=[pl.BlockSpec((k, tn), lambda i, j: (0, j))] + [hbm] * ng,
            out_specs=pl.BlockSpec((tm, k), lambda i, j: (i, 0)),
            out_shape=jax.ShapeDtypeStruct((m, k), F32),
            scratch_shapes=[pltpu.VMEM((2, tm, tn), F32), pltpu.SemaphoreType.DMA((2,))],
            compiler_params=_params(12 * tm * k + 4 * k * tn + 10 * tm * tn),
        )(w.astype(BF16), *gouts)

    def wgrad_call(a, gouts):
        def kern(a_ref, *rest):
            gs, dw, gbuf, gsem = rest[:ng], rest[ng], rest[ng + 1], rest[ng + 2]
            j, i = pl.program_id(0), pl.program_id(1)
            slot = prefetched(gs, gbuf, gsem, ranges_w, tm_w, tn_w, j * mt_w + i, mt_w * nt_w,
                              lambda s: (lax.rem(s, mt_w), s // mt_w))

            @pl.when(i == 0)
            def _():
                dw[...] = jnp.zeros_like(dw)

            dw[...] += lax.dot_general(a_ref[...].astype(BF16), gbuf[slot].astype(BF16), (((0,), (0,)), ((), ())),
                                       preferred_element_type=F32)

        return pl.pallas_call(
            kern, grid=(nt_w, mt_w), name=name + "_wgrad",
            in_specs=[pl.BlockSpec((tm_w, k), lambda j, i: (i, 0))] + [hbm] * ng,
            out_specs=pl.BlockSpec((k, tn_w), lambda j, i: (0, j)),
            out_shape=jax.ShapeDtypeStruct((k, n), F32),
            scratch_shapes=[pltpu.VMEM((2, tm_w, tn_w), F32), pltpu.SemaphoreType.DMA((2,))],
            compiler_params=_params(12 * tm_w * k + 12 * k * tn_w + 10 * tm_w * tn_w),
        )(a, *gouts)

    return fwd_call, dgrad_call, wgrad_call, widths


def _linear(name, a, w, widths=None):
    fwd_call, dgrad_call, wgrad_call, _ = _linear_calls(name, a.shape[0], a.shape[1], w.shape[1], widths, a.dtype)

    @jax.custom_vjp
    def op(a, w):
        return tuple(fwd_call(a, w))

    op.defvjp(lambda a, w: (tuple(fwd_call(a, w)), (a, w)),
              lambda res, g: (dgrad_call(res[1], g), wgrad_call(res[0], g)))
    out = op(a, w)
    return out[0] if widths is None else out


def _conv_taps(x_ext, w, n_ext):
    xm2 = pltpu.roll(x_ext, 2, 0)
    xm1 = pltpu.roll(x_ext, 1, 0)
    xp1 = pltpu.roll(x_ext, n_ext - 1, 0)
    return xm2, xm1, xp1


def _dwconv(name, x, w, b, act):
    n_rows, ch = x.shape
    tt = _row_tile(n_rows, 4 * 6 * ch, cap=256)
    nt = n_rows // tt
    n_ext = tt + 2 * SUBLANES
    per8 = tt // SUBLANES
    last8 = n_rows // SUBLANES - 1
    main = pl.BlockSpec((tt, ch), lambda i: (i, 0))
    prev = pl.BlockSpec((SUBLANES, ch), lambda i: (jnp.maximum(i * per8 - 1, 0), 0))
    nxt = pl.BlockSpec((SUBLANES, ch), lambda i: (jnp.minimum((i + 1) * per8, last8), 0))
    wspec = pl.BlockSpec((4, ch), lambda i: (0, 0))
    bspec = pl.BlockSpec((1, ch), lambda i: (0, 0))
    vmem = 4 * n_ext * ch * 14

    def ext(main_ref, prev_ref, next_ref):
        i = pl.program_id(0)
        p = jnp.where(i > 0, prev_ref[...], 0.0)
        q = jnp.where(i < nt - 1, next_ref[...], 0.0)
        return jnp.concatenate([p, main_ref[...], q], axis=0)

    def pre_of(x_ext, wv, bv):
        xm2, xm1, xp1 = _conv_taps(x_ext, wv, n_ext)
        pre = wv[0:1] * xm2 + wv[1:2] * xm1 + wv[2:3] * x_ext + wv[3:4] * xp1 + bv
        return pre, (xm2, xm1, xp1)

    def fwd_call(x, w, b):
        def kern(xm, xp, xn, w_ref, b_ref, o_ref):
            pre, _ = pre_of(ext(xm, xp, xn), w_ref[...], b_ref[...])
            pre = pre[SUBLANES:SUBLANES + tt]
            o_ref[...] = pre * jax.nn.sigmoid(pre) if act else pre

        return pl.pallas_call(
            kern, grid=(nt,), name=name + "_fwd", in_specs=[main, prev, nxt, wspec, bspec], out_specs=main,
            out_shape=jax.ShapeDtypeStruct((n_rows, ch), F32), compiler_params=_params(vmem),
        )(x, x, x, w, b)

    def bwd_call(x, w, b, dy):
        def kern(xm, xp, xn, gm, gp, gn, w_ref, b_ref, dx_ref, dw_ref, db_ref):
            wv = w_ref[...]
            x_ext = ext(xm, xp, xn)
            pre, (xm2, xm1, xp1) = pre_of(x_ext, wv, b_ref[...])
            dpre = ext(gm, gp, gn)
            if act:
                sg = jax.nn.sigmoid(pre)
                dpre = dpre * (sg + pre * sg * (1.0 - sg))
            dx = (wv[0:1] * pltpu.roll(dpre, n_ext - 2, 0) + wv[1:2] * pltpu.roll(dpre, n_ext - 1, 0)
                  + wv[2:3] * dpre + wv[3:4] * pltpu.roll(dpre, 1, 0))
            sl = slice(SUBLANES, SUBLANES + tt)
            dx_ref[...] = dx[sl]
            dm = dpre[sl]

            @pl.when(pl.program_id(0) == 0)
            def _():
                dw_ref[...] = jnp.zeros_like(dw_ref)
                db_ref[...] = jnp.zeros_like(db_ref)

            dw_ref[...] += jnp.concatenate(
                [jnp.sum(dm * t[sl], axis=0, keepdims=True) for t in (xm2, xm1, x_ext, xp1)], axis=0)
            db_ref[...] += jnp.sum(dm, axis=0, keepdims=True)

        return pl.pallas_call(
            kern, grid=(nt,), name=name + "_bwd", in_specs=[main, prev, nxt, main, prev, nxt, wspec, bspec],
            out_specs=[main, wspec, bspec],
            out_shape=[jax.ShapeDtypeStruct((n_rows, ch), F32), jax.ShapeDtypeStruct((4, ch), F32),
                       jax.ShapeDtypeStruct((1, ch), F32)],
            compiler_params=_params(vmem),
        )(x, x, x, dy, dy, dy, w, b)

    @jax.custom_vjp
    def op(x, w, b):
        return fwd_call(x, w, b)

    op.defvjp(lambda x, w, b: (fwd_call(x, w, b), (x, w, b)), lambda res, g: tuple(bwd_call(*res, g)))
    return op(x, w, b)


def _scan_groups(tt, ch, reverse, load, store, carry_ref):
    row = lax.broadcasted_iota(jnp.int32, (SUBLANES, ch), 0)
    ng = tt // SUBLANES

    def body(k, carry):
        g = (ng - 1 - k) if reverse else k
        sl = pl.ds(pl.multiple_of(g * SUBLANES, SUBLANES), SUBLANES)
        a, b, extra = load(sl)
        for s in (1, 2, 4):
            if reverse:
                a_sh, b_sh, valid = pltpu.roll(a, SUBLANES - s, 0), pltpu.roll(b, SUBLANES - s, 0), row < SUBLANES - s
            else:
                a_sh, b_sh, valid = pltpu.roll(a, s, 0), pltpu.roll(b, s, 0), row >= s
            b = jnp.where(valid, b + a * b_sh, b)
            a = jnp.where(valid, a * a_sh, a)
        h = b + a * carry
        if reverse:
            h_prev = jnp.where(row == SUBLANES - 1, carry, pltpu.roll(h, SUBLANES - 1, 0))
            last = h[0:1]
        else:
            h_prev = jnp.where(row == 0, carry, pltpu.roll(h, 1, 0))
            last = h[SUBLANES - 1:SUBLANES]
        store(sl, h, h_prev, extra)
        return jnp.broadcast_to(last, (SUBLANES, ch))

    carry_ref[...] = lax.fori_loop(0, ng, body, carry_ref[...])


def _lin_scan(name, a, b, h0, reverse):
    n_rows, ch = a.shape
    tt = _row_tile(n_rows, 0, cap=256)
    nt = n_rows // tt
    vmem = 2 * 4 * tt * ch * 5

    def tile_spec(rev):
        return pl.BlockSpec((tt, ch), (lambda i: (nt - 1 - i, 0)) if rev else (lambda i: (i, 0)))

    vec = pl.BlockSpec((1, ch), lambda i: (0, 0))

    def fwd_call(a, b, h0):
        def kern(a_ref, b_ref, h0_ref, h_ref, hp_ref, last_ref, carry):
            @pl.when(pl.program_id(0) == 0)
            def _():
                carry[...] = jnp.broadcast_to(h0_ref[...], carry.shape)

            def load(sl):
                return a_ref[sl, :], b_ref[sl, :], None

            def store(sl, h, h_prev, _):
                h_ref[sl, :] = h
                hp_ref[sl, :] = h_prev

            _scan_groups(tt, ch, reverse, load, store, carry)
            last_ref[...] = carry[0:1]

        return pl.pallas_call(
            kern, grid=(nt,), name=name + "_fwd", in_specs=[tile_spec(reverse), tile_spec(reverse), vec],
            out_specs=[tile_spec(reverse), tile_spec(reverse), vec],
            out_shape=[jax.ShapeDtypeStruct((n_rows, ch), F32)] * 2 + [jax.ShapeDtypeStruct((1, ch), F32)],
            scratch_shapes=[pltpu.VMEM((SUBLANES, ch), F32)], compiler_params=_params(vmem),
        )(a, b, h0)

    def bwd_call(a, h_prev, dh, dlast):
        rev = not reverse

        def kern(a_ref, hp_ref, dh_ref, dl_ref, da_ref, db_ref, d0_ref, carry):
            @pl.when(pl.program_id(0) == 0)
            def _():
                carry[...] = jnp.broadcast_to(dl_ref[...], carry.shape)

            def load(sl):
                av, dv = a_ref[sl, :], dh_ref[sl, :]
                return av, av * dv, dv

            def store(sl, u, u_next, dv):
                g = dv + u_next
                db_ref[sl, :] = g
                da_ref[sl, :] = g * hp_ref[sl, :]

            _scan_groups(tt, ch, rev, load, store, carry)
            d0_ref[...] = carry[0:1]

        return pl.pallas_call(
            kern, grid=(nt,), name=name + "_bwd", in_specs=[tile_spec(rev)] * 3 + [vec],
            out_specs=[tile_spec(rev), tile_spec(rev), vec],
            out_shape=[jax.ShapeDtypeStruct((n_rows, ch), F32)] * 2 + [jax.ShapeDtypeStruct((1, ch), F32)],
            scratch_shapes=[pltpu.VMEM((SUBLANES, ch), F32)], compiler_params=_params(vmem),
        )(a, h_prev, dh, dlast)

    @jax.custom_vjp
    def op(a, b, h0):
        h, _, last = fwd_call(a, b, h0)
        return h, last

    def op_fwd(a, b, h0):
        h, h_prev, last = fwd_call(a, b, h0)
        return (h, last), (a, h_prev)

    def op_bwd(res, g):
        da, db, d0 = bwd_call(res[0], res[1], g[0], g[1])
        return da, db, d0

    op.defvjp(op_fwd, op_bwd)
    return op(a, b, h0)


def _tri(reverse):
    q = lax.broadcasted_iota(jnp.int32, (CHUNK, CHUNK), 0)
    s = lax.broadcasted_iota(jnp.int32, (CHUNK, CHUNK), 1)
    return (q <= s) if reverse else (q >= s)


def _pick_col(x, lane):
    idx = lax.broadcasted_iota(jnp.int32, x.shape, 1)
    return jnp.sum(jnp.where(idx == lane, x, 0.0), axis=1, keepdims=True)


def _pick_row(x, row):
    idx = lax.broadcasted_iota(jnp.int32, x.shape, 0)
    return jnp.sum(jnp.where(idx == row, x, 0.0), axis=0, keepdims=True)


def _ssd_shared(small, bias_row, alog_row, reverse):
    delta_all = jax.nn.softplus(small + bias_row)
    acs_all = _exact_dot(_tri(reverse).astype(F32), delta_all * (-jnp.exp(alog_row)))
    return delta_all, acs_all, acs_all.T


def _ssd_group(xs, bm, cm, state, delta_all, acs_all, acs_t, g, direction, reverse):
    mask = _tri(reverse)
    last = 0 if reverse else CHUNK - 1
    hd = SSD_HEADDIM
    rowi = lax.broadcasted_iota(jnp.int32, (CHUNK, 1), 0)
    a_cols, a_rows, deltas, tots = [], [], [], []
    for r in range(SSD_HPG):
        lane = _DT_LANE + 32 * direction + SSD_HPG * g + r
        a_col = _pick_col(acs_all, lane)
        a_cols.append(a_col)
        a_rows.append(_pick_row(acs_t, lane))
        deltas.append(_pick_col(delta_all, lane))
        tots.append(jnp.sum(jnp.where(rowi == last, a_col, 0.0), axis=0, keepdims=True))

    def wide(cols, rows):
        return jnp.concatenate([jnp.broadcast_to(c, (rows, hd)) for c in cols], axis=1)

    a_w = wide(a_cols, CHUNK)
    x_w = xs * wide(deltas, CHUNK)
    st = _mm_tn(x_w * jnp.exp(wide(tots, 1) - a_w), bm)
    y_off = _mm_nt(cm, state) * jnp.exp(a_w)
    grow = jnp.concatenate([jnp.broadcast_to(jnp.exp(t), (hd, 1)) for t in tots], axis=0)
    cb = _mm_nt(cm, bm)
    m_cat = jnp.concatenate([cb * jnp.exp(jnp.where(mask, a_cols[r] - a_rows[r], -jnp.inf)) for r in range(SSD_HPG)], axis=1)
    lane_head = lax.broadcasted_iota(jnp.int32, (1, SSD_HPG * hd), 1) // hd
    x_bd = jnp.concatenate([jnp.where(lane_head == r, x_w, 0.0) for r in range(SSD_HPG)], axis=0)
    return _mm_nn(m_cat, x_bd) + y_off, grow * state + st


def _ssd_scan(name, xs, bm, cm, small, bias_row, alog_row, s0, direction, reverse):
    n_rows = xs.shape[0]
    nc = n_rows // CHUNK
    gw = SSD_HPG * SSD_HEADDIM
    vmem = 4 * CHUNK * (gw + 3 * 128) * 8 + 4 * gw * 128 * 12 + (8 << 20)

    n_state = SSD_GROUPS * gw
    shared_scratch = [pltpu.VMEM((CHUNK, LANES), F32), pltpu.VMEM((CHUNK, LANES), F32), pltpu.VMEM((LANES, CHUNK), F32)]

    def specs(order, gps=1):
        def cidx(c):
            return (nc - 1 - c) if order else c

        return dict(
            xs=pl.BlockSpec((CHUNK, gps * gw), lambda c, g: (cidx(c), g)),
            bc=pl.BlockSpec((CHUNK, gps * SSD_STATE), lambda c, g: (cidx(c), g)),
            small=pl.BlockSpec((CHUNK, LANES), lambda c, g: (cidx(c), 0)),
            row=pl.BlockSpec((1, LANES), lambda c, g: (0, 0)),
            state=pl.BlockSpec((n_state, SSD_STATE), lambda c, g: (0, 0)),
            enter=pl.BlockSpec((1, gps * gw, SSD_STATE), lambda c, g: (cidx(c), g, 0)),
        )

    gps = 2

    def group_rows(g):
        return pl.ds(pl.multiple_of(g * gps * gw, gps * gw), gps * gw)

    def step_fn(g):
        def fn(xs_v, bm_v, cm_v, st_v, d_all, a_all, a_t):
            ys, sts = [], []
            for u in range(gps):
                y_u, s_u = _ssd_group(xs_v[:, u * gw:(u + 1) * gw], bm_v[:, u * SSD_STATE:(u + 1) * SSD_STATE],
                                      cm_v[:, u * SSD_STATE:(u + 1) * SSD_STATE], st_v[u * gw:(u + 1) * gw],
                                      d_all, a_all, a_t, gps * g + u, direction, reverse)
                ys.append(y_u)
                sts.append(s_u)
            return jnp.concatenate(ys, axis=1), jnp.concatenate(sts, axis=0)

        return fn

    def fwd_call(xs, bm, cm, small, bias_row, alog_row, s0):
        sp = specs(reverse, gps)

        def kern(xs_r, bm_r, cm_r, sm_r, br_r, ar_r, s0_r, y_r, sf_r, se_r, st, sh_d, sh_a, sh_t):
            c, g = pl.program_id(0), pl.program_id(1)

            @pl.when((c == 0) & (g == 0))
            def _():
                st[...] = s0_r[...]

            @pl.when(g == 0)
            def _():
                sh_d[...], sh_a[...], sh_t[...] = _ssd_shared(sm_r[...], br_r[...], ar_r[...], reverse)

            rows = group_rows(g)
            s_in = st[rows, :]
            se_r[0] = s_in
            y_r[...], s_new = step_fn(g)(xs_r[...], bm_r[...], cm_r[...], s_in, sh_d[...], sh_a[...], sh_t[...])
            st[rows, :] = s_new
            sf_r[rows, :] = s_new

        return pl.pallas_call(
            kern, grid=(nc, SSD_GROUPS // gps), name=name + "_fwd",
            in_specs=[sp['xs'], sp['bc'], sp['bc'], sp['small'], sp['row'], sp['row'], sp['state']],
            out_specs=[sp['xs'], sp['state'], sp['enter']],
            out_shape=[jax.ShapeDtypeStruct((n_rows, SSD_INNER), F32), jax.ShapeDtypeStruct((n_state, SSD_STATE), F32),
                       jax.ShapeDtypeStruct((nc, n_state, SSD_STATE), F32)],
            scratch_shapes=[pltpu.VMEM((n_state, SSD_STATE), F32)] + shared_scratch, compiler_params=_params(vmem),
        )(xs, bm, cm, small, bias_row, alog_row, s0)

    def bwd_call(xs, bm, cm, small, bias_row, alog_row, enter, dy, dsf):
        sp = specs(not reverse, gps)

        def kern(xs_r, bm_r, cm_r, sm_r, br_r, ar_r, se_r, dy_r, dsf_r, dxs_r, dbm_r, dcm_r, dsm_r, dbr_r, dar_r, ds0_r,
                 ds, sh_d, sh_a, sh_t, gd, ga, gt):
            c, g = pl.program_id(0), pl.program_id(1)

            @pl.when((c == 0) & (g == 0))
            def _():
                ds[...] = dsf_r[...]
                dbr_r[...] = jnp.zeros_like(dbr_r)
                dar_r[...] = jnp.zeros_like(dar_r)

            @pl.when(g == 0)
            def _():
                sh_d[...], sh_a[...], sh_t[...] = _ssd_shared(sm_r[...], br_r[...], ar_r[...], reverse)
                gd[...] = jnp.zeros_like(gd)
                ga[...] = jnp.zeros_like(ga)
                gt[...] = jnp.zeros_like(gt)

            rows = group_rows(g)
            _, vjp = jax.vjp(step_fn(g), xs_r[...], bm_r[...], cm_r[...], se_r[0], sh_d[...], sh_a[...], sh_t[...])
            dxs, dbm, dcm, ds_in, dd, da, dt = vjp((dy_r[...], ds[rows, :]))
            dxs_r[...] = dxs
            dbm_r[...] = dbm
            dcm_r[...] = dcm
            ds[rows, :] = ds_in
            ds0_r[rows, :] = ds_in
            gd[...] += dd
            ga[...] += da
            gt[...] += dt

            @pl.when(g == SSD_GROUPS // gps - 1)
            def _():
                shared = functools.partial(_ssd_shared, reverse=reverse)
                dsm, dbr, dar = jax.vjp(shared, sm_r[...], br_r[...], ar_r[...])[1]((gd[...], ga[...], gt[...]))
                dsm_r[...] = dsm
                dbr_r[...] += dbr
                dar_r[...] += dar

        return pl.pallas_call(
            kern, grid=(nc, SSD_GROUPS // gps), name=name + "_bwd",
            in_specs=[sp['xs'], sp['bc'], sp['bc'], sp['small'], sp['row'], sp['row'], sp['enter'], sp['xs'], sp['state']],
            out_specs=[sp['xs'], sp['bc'], sp['bc'], sp['small'], sp['row'], sp['row'], sp['state']],
            out_shape=[jax.ShapeDtypeStruct(xs.shape, F32), jax.ShapeDtypeStruct(bm.shape, F32),
                       jax.ShapeDtypeStruct(cm.shape, F32), jax.ShapeDtypeStruct((n_rows, LANES), F32),
                       jax.ShapeDtypeStruct((1, LANES), F32), jax.ShapeDtypeStruct((1, LANES), F32),
                       jax.ShapeDtypeStruct(s0.shape, F32)],
            scratch_shapes=[pltpu.VMEM((n_state, SSD_STATE), F32)] + shared_scratch + shared_scratch,
            compiler_params=_params(vmem),
        )(xs, bm, cm, small, bias_row, alog_row, enter, dy, dsf)

    @jax.custom_vjp
    def op(*args):
        y, sf, _ = fwd_call(*args)
        return y, sf

    def op_fwd(*args):
        y, sf, enter = fwd_call(*args)
        return (y, sf), (args[:6], enter)

    op.defvjp(op_fwd, lambda res, g: tuple(bwd_call(*res[0], res[1], g[0], g[1])))
    return op(xs, bm, cm, small, bias_row, alog_row, s0)


def _ml_shared(small, gate_row, reverse):
    gates = small + gate_row
    b_all = _exact_dot(_tri(reverse).astype(F32), jax.nn.log_sigmoid(gates))
    return gates, b_all, gates.T, b_all.T


def _ml_head(q, k, v, c_st, n_st, m_st, gates, b_all, gates_t, b_t, h, direction, reverse):
    mask = _tri(reverse)
    last = 0 if reverse else CHUNK - 1
    lane_i = _MG_LANE + 8 * direction + h
    lane_f = lane_i + ML_HEADS
    b_col = _pick_col(b_all, lane_f)
    b_row = _pick_row(b_t, lane_f)
    li_col = _pick_col(gates, lane_i)
    li_row = _pick_row(gates_t, lane_i)
    rowi = lax.broadcasted_iota(jnp.int32, (CHUNK, 1), 0)
    g_tot = jnp.sum(jnp.where(rowi == last, b_col, 0.0), axis=0, keepdims=True)
    m_in = m_st[:, 0:1]
    q = q * (ML_HD ** -0.5)
    w = g_tot - b_col + li_col
    m_loc = lax.stop_gradient(jnp.max(w, axis=0, keepdims=True))
    kw = k * jnp.exp(w - m_loc)
    c_loc = _mm_tn(kw, v)
    n_loc = jnp.sum(kw, axis=0, keepdims=True)
    m_new = lax.stop_gradient(jnp.maximum(g_tot + m_in, m_loc))
    s_old = jnp.exp(g_tot + m_in - m_new)
    s_loc = jnp.exp(m_loc - m_new)
    c_new = s_old * c_st + s_loc * c_loc
    n_new = s_old * n_st + s_loc * n_loc
    log_d = jnp.where(mask, b_col - b_row + li_row, -jnp.inf)
    inter = b_col + m_in
    m_t = lax.stop_gradient(jnp.maximum(inter, jnp.max(log_d, axis=1, keepdims=True)))
    dmat = jnp.exp(log_d - m_t)
    wi = jnp.exp(inter - m_t)
    s = _mm_nt(q, k) * dmat
    num = _mm_nn(s, v) + wi * _mm_nn(q, c_st)
    den = jnp.sum(s, axis=1, keepdims=True) + wi * jnp.sum(_round_bf16(q) * _round_bf16(n_st), axis=1, keepdims=True)
    out = num / jnp.maximum(jnp.abs(den), jnp.exp(-m_t))
    return out, c_new, n_new, jnp.broadcast_to(m_new, (1, LANES))


def _ml_scan(name, q, k, v, small, gate_row, c0, n0, m0, direction, reverse):
    n_rows = q.shape[0]
    nc = n_rows // CHUNK
    vmem = 4 * CHUNK * (4 * ML_HD + 128) * 8 + 4 * ML_HD * ML_HD * 12 + (8 << 20)

    def specs(order):
        def cidx(c):
            return (nc - 1 - c) if order else c

        return dict(
            qkv=pl.BlockSpec((CHUNK, ML_HD), lambda c, h: (cidx(c), h)),
            small=pl.BlockSpec((CHUNK, LANES), lambda c, h: (cidx(c), 0)),
            row=pl.BlockSpec((1, LANES), lambda c, h: (0, 0)),
            c=pl.BlockSpec((ML_HEADS * ML_HD, ML_HD), lambda c, h: (0, 0)),
            n=pl.BlockSpec((ML_HEADS, 1, ML_HD), lambda c, h: (0, 0, 0)),
            m=pl.BlockSpec((ML_HEADS, 1, LANES), lambda c, h: (0, 0, 0)),
            ec=pl.BlockSpec((1, ML_HD, ML_HD), lambda c, h: (cidx(c), h, 0)),
            en=pl.BlockSpec((1, 1, 1, ML_HD), lambda c, h: (cidx(c), h, 0, 0)),
            em=pl.BlockSpec((1, 1, 1, LANES), lambda c, h: (cidx(c), h, 0, 0)),
        )

    st_shapes = [jax.ShapeDtypeStruct((ML_HEADS * ML_HD, ML_HD), F32), jax.ShapeDtypeStruct((ML_HEADS, 1, ML_HD), F32),
                 jax.ShapeDtypeStruct((ML_HEADS, 1, LANES), F32)]
    scratch = [pltpu.VMEM((ML_HEADS * ML_HD, ML_HD), F32), pltpu.VMEM((ML_HEADS, 1, ML_HD), F32),
               pltpu.VMEM((ML_HEADS, 1, LANES), F32)]
    shared_scratch = [pltpu.VMEM((CHUNK, LANES), F32), pltpu.VMEM((CHUNK, LANES), F32),
                      pltpu.VMEM((LANES, CHUNK), F32), pltpu.VMEM((LANES, CHUNK), F32)]

    def head_rows(h):
        return pl.ds(pl.multiple_of(h * ML_HD, ML_HD), ML_HD)

    def fwd_call(q, k, v, small, gate_row, c0, n0, m0):
        sp = specs(reverse)

        def kern(q_r, k_r, v_r, sm_r, gr_r, c0_r, n0_r, m0_r, o_r, cf_r, nf_r, mf_r, ec_r, en_r, em_r, cs, ns, ms, *sh):
            c, h = pl.program_id(0), pl.program_id(1)

            @pl.when((c == 0) & (h == 0))
            def _():
                cs[...] = c0_r[...]
                ns[...] = n0_r[...]
                ms[...] = m0_r[...]

            @pl.when(h == 0)
            def _():
                for ref, val in zip(sh, _ml_shared(sm_r[...], gr_r[...], reverse)):
                    ref[...] = val

            rows = head_rows(h)
            c_in, n_in, m_in = cs[rows, :], ns[h], ms[h]
            ec_r[0] = c_in
            en_r[0, 0] = n_in
            em_r[0, 0] = m_in
            out, c_new, n_new, m_new = _ml_head(q_r[...], k_r[...], v_r[...], c_in, n_in, m_in, *[r[...] for r in sh],
                                                h, direction, reverse)
            o_r[...] = out
            cs[rows, :] = c_new
            ns[h] = n_new
            ms[h] = m_new
            cf_r[rows, :] = c_new
            nf_r[h] = n_new
            mf_r[h] = m_new

        return pl.pallas_call(
            kern, grid=(nc, ML_HEADS), name=name + "_fwd",
            in_specs=[sp['qkv']] * 3 + [sp['small'], sp['row'], sp['c'], sp['n'], sp['m']],
            out_specs=[sp['qkv'], sp['c'], sp['n'], sp['m'], sp['ec'], sp['en'], sp['em']],
            out_shape=[jax.ShapeDtypeStruct((n_rows, ML_HEADS * ML_HD), F32)] + st_shapes + [
                jax.ShapeDtypeStruct((nc, ML_HEADS * ML_HD, ML_HD), F32),
                jax.ShapeDtypeStruct((nc, ML_HEADS, 1, ML_HD), F32), jax.ShapeDtypeStruct((nc, ML_HEADS, 1, LANES), F32)],
            scratch_shapes=scratch + shared_scratch, compiler_params=_params(vmem),
        )(q, k, v, small, gate_row, c0, n0, m0)

    def bwd_call(q, k, v, small, gate_row, ec, en, em, do, dcf, dnf, dmf):
        sp = specs(not reverse)
        n_sh = len(shared_scratch)

        def kern(q_r, k_r, v_r, sm_r, gr_r, ec_r, en_r, em_r, do_r, dcf_r, dnf_r, dmf_r,
                 dq_r, dk_r, dv_r, dsm_r, dgr_r, dc0_r, dn0_r, dm0_r, dcs, dns, dms, *rest):
            sh, gsh = rest[:n_sh], rest[n_sh:]
            c, h = pl.program_id(0), pl.program_id(1)

            @pl.when((c == 0) & (h == 0))
            def _():
                dcs[...] = dcf_r[...]
                dns[...] = dnf_r[...]
                dms[...] = dmf_r[...]
                dgr_r[...] = jnp.zeros_like(dgr_r)

            @pl.when(h == 0)
            def _():
                for ref, val in zip(sh, _ml_shared(sm_r[...], gr_r[...], reverse)):
                    ref[...] = val
                for ref in gsh:
                    ref[...] = jnp.zeros_like(ref)

            rows = head_rows(h)
            fn = functools.partial(_ml_head, h=h, direction=direction, reverse=reverse)
            _, vjp = jax.vjp(fn, q_r[...], k_r[...], v_r[...], ec_r[0], en_r[0, 0], em_r[0, 0], *[r[...] for r in sh])
            grads = vjp((do_r[...], dcs[rows, :], dns[h], dms[h]))
            dq, dk, dv, dc, dn, dm = grads[:6]
            dq_r[...] = dq
            dk_r[...] = dk
            dv_r[...] = dv
            for ref, val in zip(gsh, grads[6:]):
                ref[...] += val
            dm = jnp.broadcast_to(jnp.sum(dm, axis=1, keepdims=True), (1, LANES)) * (1.0 / LANES)
            dcs[rows, :] = dc
            dns[h] = dn
            dms[h] = dm
            dc0_r[rows, :] = dc
            dn0_r[h] = dn
            dm0_r[h] = dm

            @pl.when(h == ML_HEADS - 1)
            def _():
                shared = functools.partial(_ml_shared, reverse=reverse)
                dsm, dgr = jax.vjp(shared, sm_r[...], gr_r[...])[1](tuple(r[...] for r in gsh))
                dsm_r[...] = dsm
                dgr_r[...] += dgr

        return pl.pallas_call(
            kern, grid=(nc, ML_HEADS), name=name + "_bwd",
            in_specs=[sp['qkv']] * 3 + [sp['small'], sp['row'], sp['ec'], sp['en'], sp['em'], sp['qkv'], sp['c'], sp['n'], sp['m']],
            out_specs=[sp['qkv']] * 3 + [sp['small'], sp['row'], sp['c'], sp['n'], sp['m']],
            out_shape=[jax.ShapeDtypeStruct(q.shape, F32)] * 3 + [jax.ShapeDtypeStruct((n_rows, LANES), F32),
                                                                  jax.ShapeDtypeStruct((1, LANES), F32)] + st_shapes,
            scratch_shapes=scratch + shared_scratch + shared_scratch, compiler_params=_params(vmem),
        )(q, k, v, small, gate_row, ec, en, em, do, dcf, dnf, dmf)

    @jax.custom_vjp
    def op(*args):
        return tuple(fwd_call(*args)[:4])

    def op_fwd(*args):
        res = fwd_call(*args)
        return tuple(res[:4]), (args[:5], tuple(res[4:]))

    op.defvjp(op_fwd, lambda res, g: tuple(bwd_call(*res[0], *res[1], *g)))
    return op(q, k, v, small, gate_row, c0, n0, m0)


def _f_modulate(x, shift, scale):
    return (_layernorm_rows(x) * (1.0 + scale) + shift,)


def _f_resid_ln(x, o, gate, bias, ln_g, ln_b):
    return (_layernorm_rows(DN_ALPHA * x + gate * (o + bias)) * ln_g + ln_b,)


def _f_lru_gates(xc, w_r, b_r, w_i, b_i, lam):
    outs = []
    for d in range(2):
        def blockdiag(w):
            return jnp.concatenate(
                [_mm_nn(xc[:, n * LRU_BS:(n + 1) * LRU_BS], w[(d * LRU_BLOCKS + n) * LRU_BS:(d * LRU_BLOCKS + n + 1) * LRU_BS, :])
                 for n in range(LRU_BLOCKS)], axis=1)

        r = jax.nn.sigmoid(blockdiag(w_r) + b_r[d:d + 1])
        i = jax.nn.sigmoid(blockdiag(w_i) + b_i[d:d + 1])
        log_a = -LRU_C * r * jax.nn.softplus(-lam[d:d + 1])
        outs += [jnp.exp(log_a), jnp.sqrt(1.0 - jnp.exp(2.0 * log_a)) * i * xc]
    return tuple(outs)


def _f_lru_out(h_f, h_b, ly):
    return ((h_f + h_b) * jax.nn.gelu(ly),)


def _f_ssd_post(y_f, y_b, xs, z, d_exp, norm_w):
    y = (y_f + y_b + xs * d_exp) * jax.nn.silu(z)
    gw = SSD_INNER // SSD_GROUPS
    parts = []
    for g in range(SSD_GROUPS):
        yg = y[:, g * gw:(g + 1) * gw]
        parts.append(yg * lax.rsqrt(jnp.mean(jnp.square(yg), -1, keepdims=True) + LN_EPS))
    return (jnp.concatenate(parts, axis=1) * norm_w,)


def _f_ml_post(h_f, h_b, o, norm_w):
    h = h_f + h_b
    parts = [_layernorm_rows(h[:, i * ML_HD:(i + 1) * ML_HD]) for i in range(ML_HEADS)]
    return (jnp.concatenate(parts, axis=1) * norm_w * jax.nn.sigmoid(o),)


def _f_merge(ga, gb, gc, pa, pb, pc):
    return (jax.nn.sigmoid(ga) * pa + jax.nn.sigmoid(gb) * pb + jax.nn.sigmoid(gc) * pc,)


def _f_relu2(pre, bias):
    return (jnp.square(jax.nn.relu(pre + bias)),)


def _lane_row(vec, start):
    return jnp.pad(vec[None], ((0, 0), (start, LANES - start - vec.shape[0])))


def _mixer(tag, x_tok, shift, scale, p, states):
    (lru_s, ssd_s, ml_s) = states
    lx, ly, sz, xs, bm, cm, mq, mk, mv, mo, ga, gb, gc, small = _rowwise_linear(
        tag + "in", _f_modulate, [x_tok], [shift, scale], [(p['w_in_main'], _IN_MAIN_WIDTHS), (p['w_in_small'], None)])

    xc = _dwconv(tag + "lruconv", lx, p['lru_conv_w'], p['lru_conv_b'][None], False)
    a_f, b_f, a_b, b_b = _rowwise(
        tag + "lrugate", _f_lru_gates, [xc],
        [p['lru_w_r'].reshape(2 * LRU_BLOCKS * LRU_BS, LRU_BS), p['lru_b_r'], p['lru_w_i'].reshape(2 * LRU_BLOCKS * LRU_BS, LRU_BS),
         p['lru_b_i'], p['lru_lambda']], [D_MODEL] * 4)
    h_f, s_f = _lin_scan(tag + "lruscanf", a_f, b_f, lru_s[0], False)
    h_b, s_b = _lin_scan(tag + "lruscanb", a_b, b_b, lru_s[1], True)
    (pa,) = _rowwise_linear(tag + "bra", _f_lru_out, [h_f, h_b, ly], [], [(p['w_br_a'], None)])

    cw, cb_ = p['ssd_conv_w'], p['ssd_conv_b'][None]
    xs_c = _dwconv(tag + "ssdconvx", xs, cw[:, :2048], cb_[:, :2048], True)
    bm_c = _dwconv(tag + "ssdconvb", bm, cw[:, 2048:3072], cb_[:, 2048:3072], True)
    cm_c = _dwconv(tag + "ssdconvc", cm, cw[:, 3072:], cb_[:, 3072:], True)
    ssd_new, ys = [], []
    for d in range(2):
        y_d, st_d = _ssd_scan(tag + "ssd%d" % d, xs_c, bm_c, cm_c, small, _lane_row(p['ssd_dt_bias'][d], _DT_LANE + 32 * d),
                              _lane_row(p['ssd_a_log'][d], _DT_LANE + 32 * d), ssd_s[d], d, d == 1)
        ys.append(y_d)
        ssd_new.append(st_d)
    (pb,) = _rowwise_linear(tag + "brb", _f_ssd_post, [ys[0], ys[1], xs_c, sz],
                            [jnp.repeat(p['ssd_d'], SSD_HEADDIM)[None], p['ssd_norm_w'][None]], [(p['w_br_b'], None)])

    mw, mb = p['ml_conv_w'], p['ml_conv_b'][None]
    q_c = _dwconv(tag + "mlconvq", mq, mw[:, :1024], mb[:, :1024], True)
    k_c = _dwconv(tag + "mlconvk", mk, mw[:, 1024:], mb[:, 1024:], True)
    gate_row = _lane_row(p['ml_gate_b'].reshape(4 * ML_HEADS), _MG_LANE)
    ml_new, hs = [], []
    for d in range(2):
        o_d, c_d, n_d, m_d = _ml_scan(tag + "ml%d" % d, q_c, k_c, mv, small, gate_row, *ml_s[d], d, d == 1)
        hs.append(o_d)
        ml_new.append((c_d, n_d, m_d))
    (pc,) = _rowwise_linear(tag + "brc", _f_ml_post, [hs[0], hs[1], mo], [p['ml_norm_w'][None]], [(p['w_br_c'], None)])
    return (ga, gb, gc, pa, pb, pc), ((s_f, s_b), tuple(ssd_new), tuple(ml_new))


def _merge(tag, br, p):
    return _rowwise_linear(tag + "out", _f_merge, list(br), [], [(p['w_out'], None)])[0]


def _sublayers(tag, xin, o, mods, p, l):
    sh2, sc2, g1, g2 = mods
    (x1,) = _rowwise(tag + "ln1", _f_resid_ln, [xin, o], [g1, p['b_out'][None], p['ln1_g'][None], p['ln1_b'][None]], [D_MODEL])
    (pre,) = _rowwise_linear(tag + "ff1", _f_modulate, [x1], [sh2, sc2], [(p['w_ff1'], None)])
    (o2,) = _rowwise_linear(tag + "ff2", _f_relu2, [pre], [p['b_ff1'][None]], [(p['w_ff2'], None)])
    (x2,) = _rowwise(tag + "ln2", _f_resid_ln, [x1, o2], [g2, p['b_ff2'][None], p['ln2_g'][None], p['ln2_b'][None]], [D_MODEL])
    return x2


def _to_col_major(h):
    s, d = h.shape
    return h.reshape(s // GRID_W, GRID_W, d).swapaxes(0, 1).reshape(s, d)


def _from_col_major(h):
    s, d = h.shape
    return h.reshape(GRID_W, s // GRID_W, d).swapaxes(0, 1).reshape(s, d)


def _forward(x, wts, cvec, ctx):
    zeros = lambda *s: jnp.zeros(s, F32)
    ctx_init = ((zeros(1, D_MODEL), zeros(1, D_MODEL)),
                (zeros(SSD_INNER, SSD_STATE), zeros(SSD_INNER, SSD_STATE)),
                tuple((zeros(ML_HEADS * ML_HD, ML_HD), zeros(ML_HEADS, 1, ML_HD), zeros(ML_HEADS, 1, LANES)) for _ in range(2)))
    for l in range(DEPTH):
        p = {n: (wts[n] if n == 'c_ctx' else wts[n][l]) for n in wts}
        tag = "l%d" % l
        cc = jnp.concatenate([cvec, p['c_ctx'][None], jnp.zeros((SUBLANES - 2, D_MODEL), F32)], axis=0)
        mod = _linear(tag + "ada", jax.nn.silu(cc), p['w_ada']) + p['b_ada'][None]
        sh1x, sc1x, g1x, sh2x, sc2x, g2x = [mod[0:1, i * D_MODEL:(i + 1) * D_MODEL] for i in range(6)]
        sh1c, sc1c, g1c, sh2c, sc2c, g2c = [mod[1:2, i * D_MODEL:(i + 1) * D_MODEL] for i in range(6)]
        br_c, ctx_states = _mixer(tag + "c", ctx, sh1c, sc1c, p, ctx_init)
        br_x, _ = _mixer(tag + "x", _to_col_major(x) if l % 2 == 1 else x, sh1x, sc1x, p, ctx_states)
        ox = _merge(tag + "x", br_x, p)
        if l % 2 == 1:
            ox = _from_col_major(ox)
        x = _sublayers(tag + "x", x, ox, (sh2x, sc2x, g1x, g2x), p, l)
        if l < DEPTH - 1:
            ctx = _sublayers(tag + "c", ctx, _merge(tag + "c", br_c, p), (sh2c, sc2c, g1c, g2c), p, l)
    return x


def _loss_and_cotangent(y, target):
    n_rows, d = y.shape
    tt = _row_tile(n_rows, 0, cap=256)

    def kern(y_ref, t_ref, dy_ref, acc_ref):
        @pl.when(pl.program_id(0) == 0)
        def _():
            acc_ref[...] = jnp.zeros_like(acc_ref)

        err = y_ref[...] - t_ref[...]
        dy_ref[...] = err * (1.0 / d)
        acc_ref[...] += jnp.sum(jnp.square(err))

    spec = pl.BlockSpec((tt, d), lambda i: (i, 0))
    dy, acc = pl.pallas_call(
        kern, grid=(n_rows // tt,), name="loss", in_specs=[spec, spec],
        out_specs=[spec, pl.BlockSpec((SUBLANES, LANES), lambda i: (0, 0))],
        out_shape=[jax.ShapeDtypeStruct((n_rows, d), F32), jax.ShapeDtypeStruct((SUBLANES, LANES), F32)],
    )(y, target)
    return acc[0, 0] * (0.5 / d), dy


def _exchange(name, src, gather):
    slab = src.shape if gather else src.shape[1:]

    def body(src_ref, out_ref, send_sems, recv_sems, local_sem):
        x, y, c = lax.axis_index("x"), lax.axis_index("y"), lax.axis_index("c")
        me = 4 * x + 2 * y + c
        local = pltpu.make_async_copy(src_ref if gather else src_ref.at[me], out_ref.at[me], local_sem)
        local.start()
        copies = []
        for d in range(1, N_DEV):
            px, py, pc = lax.rem(x + (d >> 2), 2), lax.rem(y + ((d >> 1) & 1), 2), lax.rem(c + (d & 1), 2)
            peer = 4 * px + 2 * py + pc
            cp = pltpu.make_async_remote_copy(
                src_ref=src_ref if gather else src_ref.at[peer], dst_ref=out_ref.at[me],
                send_sem=send_sems.at[d - 1], recv_sem=recv_sems.at[d - 1],
                device_id=(px, py, pc), device_id_type=pl.DeviceIdType.MESH)
            cp.start()
            copies.append(cp)
        for cp in copies:
            cp.wait()
        local.wait()

    return pl.pallas_call(
        body, name=name, out_shape=jax.ShapeDtypeStruct((N_DEV,) + tuple(slab), src.dtype),
        in_specs=[pl.BlockSpec(memory_space=pl.ANY)], out_specs=pl.BlockSpec(memory_space=pl.ANY),
        scratch_shapes=[pltpu.SemaphoreType.DMA((N_DEV - 1,)), pltpu.SemaphoreType.DMA((N_DEV - 1,)), pltpu.SemaphoreType.DMA],
    )(src)


_HBM = pl.BlockSpec(memory_space=pl.ANY)
_CHIPS = ((0, 0), (0, 1), (1, 0), (1, 1))


def _gather_two_level(name, src):
    def body(src_ref, out_ref, send_sems, recv_sems, local_sem):
        x, y, c = lax.axis_index("x"), lax.axis_index("y"), lax.axis_index("c")
        me, sibling = (x, y, c), (x, y, 1 - c)
        chips = [(1 - x, y), (x, 1 - y), (1 - x, 1 - y)]

        def slab(px, py, pc):
            return out_ref.at[4 * px + 2 * py + pc]

        def copy(k, block, to, src=None):
            return pltpu.make_async_remote_copy(
                src_ref=slab(*block) if src is None else src, dst_ref=slab(*block), send_sem=send_sems.at[k],
                recv_sem=recv_sems.at[k], device_id=to, device_id_type=pl.DeviceIdType.MESH)

        mine = pltpu.make_async_copy(src_ref, slab(*me), local_sem)
        mine.start()
        first = [copy(0, me, sibling, src=src_ref)] + [copy(1 + j, me, (*chip, c), src=src_ref) for j, chip in enumerate(chips)]
        for cp in first:
            cp.start()
        passed = [copy(4 + j, (*chip, c), sibling) for j, chip in enumerate(chips)]
        for j, chip in enumerate(chips):
            copy(1 + j, (*chip, c), me).wait_recv()
            passed[j].start()
        copy(0, sibling, me).wait_recv()
        for j, chip in enumerate(chips):
            copy(4 + j, (*chip, 1 - c), me).wait_recv()
        for cp in first + passed:
            cp.wait_send()
        mine.wait()

    return pl.pallas_call(
        body, name=name, out_shape=jax.ShapeDtypeStruct((N_DEV,) + tuple(src.shape), src.dtype),
        in_specs=[_HBM], out_specs=_HBM,
        scratch_shapes=[pltpu.SemaphoreType.DMA((N_DEV - 1,)), pltpu.SemaphoreType.DMA((N_DEV - 1,)), pltpu.SemaphoreType.DMA],
    )(src)


def _scatter_to_sibling(name, parts):
    def body(p_ref, out_ref, send_sems, recv_sems):
        x, y, c = lax.axis_index("x"), lax.axis_index("y"), lax.axis_index("c")
        copies = []
        for j, (px, py) in enumerate(_CHIPS):
            cp = pltpu.make_async_remote_copy(
                src_ref=p_ref.at[4 * px + 2 * py + (1 - c)], dst_ref=out_ref.at[j], send_sem=send_sems.at[j],
                recv_sem=recv_sems.at[j], device_id=(x, y, 1 - c), device_id_type=pl.DeviceIdType.MESH)
            cp.start()
            copies.append(cp)
        for cp in copies:
            cp.wait()

    return pl.pallas_call(
        body, name=name, out_shape=jax.ShapeDtypeStruct((4,) + tuple(parts.shape[1:]), parts.dtype),
        in_specs=[_HBM], out_specs=_HBM,
        scratch_shapes=[pltpu.SemaphoreType.DMA((4,)), pltpu.SemaphoreType.DMA((4,))],
    )(parts)


def _chip_sum(name, parts, from_sibling):
    _, rows, cols = parts.shape
    lanes = -(-cols // LANES) * LANES
    tr = _row_tile(rows, 4 * lanes * 4 * 2, budget=12 << 20)

    def kern(p_ref, s_ref, o_ref):
        c = lax.axis_index("c")
        o_ref[0] = (jnp.where(c == 0, p_ref[0, 0], p_ref[0, 1]) + s_ref[0]).astype(o_ref.dtype)

    return pl.pallas_call(
        kern, grid=(4, rows // tr), name=name,
        in_specs=[pl.BlockSpec((1, 2, tr, cols), lambda j, i: (j, 0, i, 0)), pl.BlockSpec((1, tr, cols), lambda j, i: (j, i, 0))],
        out_specs=pl.BlockSpec((1, tr, cols), lambda j, i: (j, i, 0)),
        out_shape=jax.ShapeDtypeStruct((4, rows, cols), BF16),
        compiler_params=_params(4 * lanes * tr * 4 * 2),
    )(parts.reshape(4, 2, rows, cols), from_sibling)


def _scatter_across_chips(name, sums):
    def body(q_ref, out_ref, send_sems, recv_sems, local_sem):
        x, y, c = lax.axis_index("x"), lax.axis_index("y"), lax.axis_index("c")
        own = 2 * x + y
        local = pltpu.make_async_copy(q_ref.at[own], out_ref.at[own], local_sem)
        local.start()
        copies = []
        for d in range(1, 4):
            px, py = lax.rem(x + (d >> 1), 2), lax.rem(y + (d & 1), 2)
            cp = pltpu.make_async_remote_copy(
                src_ref=q_ref.at[2 * px + py], dst_ref=out_ref.at[own], send_sem=send_sems.at[d - 1],
                recv_sem=recv_sems.at[d - 1], device_id=(px, py, c), device_id_type=pl.DeviceIdType.MESH)
            cp.start()
            copies.append(cp)
        for cp in copies:
            cp.wait()
        local.wait()

    return pl.pallas_call(
        body, name=name, out_shape=jax.ShapeDtypeStruct(sums.shape, sums.dtype), in_specs=[_HBM], out_specs=_HBM,
        scratch_shapes=[pltpu.SemaphoreType.DMA((3,)), pltpu.SemaphoreType.DMA((3,)), pltpu.SemaphoreType.DMA],
    )(sums)


def _sum_parts(name, parts):
    n_parts, rows, cols = parts.shape
    tr = _row_tile(rows, 4 * cols * (n_parts + 1) * 2)

    def kern(p_ref, o_ref):
        acc = p_ref[0]
        for k in range(1, n_parts):
            acc = acc + p_ref[k]
        o_ref[...] = acc

    return pl.pallas_call(
        kern, grid=(rows // tr,), name=name, in_specs=[pl.BlockSpec((n_parts, tr, cols), lambda i: (0, i, 0))],
        out_specs=pl.BlockSpec((tr, cols), lambda i: (i, 0)), out_shape=jax.ShapeDtypeStruct((rows, cols), F32),
    )(parts)


def _adamw(name, w, m, v, parts):
    n_parts, rows, cols = parts.shape
    lanes = -(-cols // LANES) * LANES
    tr = _row_tile(rows, 4 * lanes * (n_parts + 7) * 2, budget=16 << 20)
    c1 = np.float32(1.0 - ADAM_B1 ** ADAM_STEP)
    c2 = np.float32(1.0 - ADAM_B2 ** ADAM_STEP)

    def kern(w_ref, m_ref, v_ref, p_ref, g_ref, d_ref, nm_ref, nv_ref):
        g = p_ref[0].astype(F32)
        for k in range(1, n_parts):
            g = g + p_ref[k].astype(F32)
        m_new = ADAM_B1 * m_ref[...] + (1.0 - ADAM_B1) * g
        v_new = ADAM_B2 * v_ref[...] + (1.0 - ADAM_B2) * jnp.square(g)
        g_ref[...] = g
        nm_ref[...] = m_new
        nv_ref[...] = v_new
        d_ref[...] = -ADAM_LR * ((m_new / c1) / (jnp.sqrt(v_new / c2) + ADAM_EPS) + ADAM_WD * w_ref[...])

    spec = pl.BlockSpec((tr, cols), lambda i: (i, 0))
    return pl.pallas_call(
        kern, grid=(rows // tr,), name=name,
        in_specs=[spec, spec, spec, pl.BlockSpec((n_parts, tr, cols), lambda i: (0, i, 0))], out_specs=[spec] * 4,
        out_shape=[jax.ShapeDtypeStruct((rows, cols), F32)] * 4,
        compiler_params=_params(4 * lanes * tr * (n_parts + 7) * 2),
    )(w, m, v, parts)


def _packed_rows(shape):
    return -(-int(np.prod(shape)) // (SUBLANES * LANES)) * SUBLANES


def _pack(arrays, row_multiple):
    parts = []
    for a in arrays:
        n = int(np.prod(a.shape))
        r = _packed_rows(a.shape)
        parts.append(jnp.pad(a.reshape(-1), (0, r * LANES - n)).reshape(r, LANES))
    rows = sum(p.shape[0] for p in parts)
    total = -(-rows // row_multiple) * row_multiple
    if total > rows:
        parts.append(jnp.zeros((total - rows, LANES), arrays[0].dtype))
    return jnp.concatenate(parts, axis=0)


def _unpack(packed, shapes):
    out, off = [], 0
    for s in shapes:
        r = _packed_rows(s)
        out.append(packed[off:off + r].reshape(-1)[:int(np.prod(s))].reshape(s))
        off += r
    return out


def _split_w_in(w_in):
    main = jnp.concatenate([w_in[:, :, s:e] for s, e in _IN_MAIN], axis=2)
    pad = jnp.zeros(w_in.shape[:2] + (LANES - 80,), w_in.dtype)
    small = jnp.concatenate([w_in[:, :, s:e] for s, e in _IN_SMALL] + [pad], axis=2)
    return main, small


def _join_w_in(main, small):
    return jnp.concatenate([main[:, :, 0:8192], small[:, :, 0:64], main[:, :, 8192:12288], small[:, :, 64:80],
                            main[:, :, 12288:15360]], axis=2)


def _unshard(gathered, axis):
    nd, nl, r, c = gathered.shape
    if axis == 1:
        return gathered.transpose(1, 0, 2, 3).reshape(nl, nd * r, c)
    return gathered.transpose(1, 2, 0, 3).reshape(nl, r, nd * c)


def _reshard(full, axis):
    nl, r, c = full.shape
    if axis == 1:
        return full.reshape(nl, N_DEV, r // N_DEV, c).transpose(1, 0, 2, 3)
    return full.reshape(nl, r, N_DEV, c // N_DEV).transpose(2, 0, 1, 3)


def kernel(x, c, ctx, c_ctx, w_ada, b_ada, w_in, lru_conv_w, lru_conv_b, lru_w_r, lru_b_r, lru_w_i, lru_b_i, lru_lambda, ssd_conv_w, ssd_conv_b, ssd_dt_bias, ssd_a_log, ssd_d, ssd_norm_w, ml_conv_w, ml_conv_b, ml_gate_b, ml_norm_w, w_br_a, w_br_b, w_br_c, w_out, b_out, ln1_g, ln1_b, w_ff1, b_ff1, w_ff2, b_ff2, ln2_g, ln2_b, loss_target, m_c_ctx, m_w_ada, m_b_ada, m_w_in, m_lru_conv_w, m_lru_conv_b, m_lru_w_r, m_lru_b_r, m_lru_w_i, m_lru_b_i, m_lru_lambda, m_ssd_conv_w, m_ssd_conv_b, m_ssd_dt_bias, m_ssd_a_log, m_ssd_d, m_ssd_norm_w, m_ml_conv_w, m_ml_conv_b, m_ml_gate_b, m_ml_norm_w, m_w_br_a, m_w_br_b, m_w_br_c, m_w_out, m_b_out, m_ln1_g, m_ln1_b, m_w_ff1, m_b_ff1, m_w_ff2, m_b_ff2, m_ln2_g, m_ln2_b, v_c_ctx, v_w_ada, v_b_ada, v_w_in, v_lru_conv_w, v_lru_conv_b, v_lru_w_r, v_lru_b_r, v_lru_w_i, v_lru_b_i, v_lru_lambda, v_ssd_conv_w, v_ssd_conv_b, v_ssd_dt_bias, v_ssd_a_log, v_ssd_d, v_ssd_norm_w, v_ml_conv_w, v_ml_conv_b, v_ml_gate_b, v_ml_norm_w, v_w_br_a, v_w_br_b, v_w_br_c, v_w_out, v_b_out, v_ln1_g, v_ln1_b, v_w_ff1, v_b_ff1, v_w_ff2, v_b_ff2, v_ln2_g, v_ln2_b):
    a = dict(locals())
    me = 4 * lax.axis_index("x") + 2 * lax.axis_index("y") + lax.axis_index("c")

    wts = {n: a[n] for n in _REPLICATED}
    for n, axis in _BIG.items():
        full = _unshard(_gather_two_level("gather_" + n, a[n].astype(BF16)), axis)
        if n == 'w_in':
            main, small = _split_w_in(full)
            wts['w_in_main'], wts['w_in_small'] = main.astype(F32), small.astype(F32)
        else:
            wts[n] = full.astype(F32)
    small_shapes = [a[n].shape for n in _SMALL_SHARDED]
    small_all = _exchange("gather_small", _pack([a[n] for n in _SMALL_SHARDED], SUBLANES), True)
    per_dev = [_unpack(small_all[k], small_shapes) for k in range(N_DEV)]
    for i, n in enumerate(_SMALL_SHARDED):
        wts[n] = jnp.concatenate([per_dev[k][i] for k in range(N_DEV)], axis=-1)

    y, vjp = jax.vjp(functools.partial(_forward, cvec=c, ctx=ctx[0]), x[0], wts)
    loss_local, dy = _loss_and_cotangent(y, loss_target[0])
    grad_x, grads = vjp(dy)
    grads['w_in'] = _join_w_in(grads.pop('w_in_main'), grads.pop('w_in_small'))
    loss = lax.psum(loss_local, ("x", "y", "c"))

    out = {}

    def put(n, res, shape):
        for kind, r in zip(("grad_", "delta_", "new_m_", "new_v_"), res):
            out[kind + n] = r.reshape(shape)

    for n, axis in _BIG.items():
        shp = a[n].shape
        rows, cols = shp[0] * shp[1], shp[2]
        parts = _reshard(grads[n], axis).reshape(N_DEV, rows, cols)
        chip = _chip_sum("chipsum_" + n, parts, _scatter_to_sibling("scatter_d2d_" + n, parts))
        parts = _scatter_across_chips("scatter_ici_" + n, chip)
        put(n, _adamw("adamw_" + n, a[n].reshape(rows, cols), a["m_" + n].reshape(rows, cols), a["v_" + n].reshape(rows, cols), parts), shp)

    rep_names = _REPLICATED + _SMALL_SHARDED
    chunk_rows = SUBLANES * N_DEV
    g_pack = _pack([grads[n] for n in rep_names], chunk_rows * N_DEV)
    rows = g_pack.shape[0]
    parts = _exchange("scatter_rep", g_pack.reshape(N_DEV, rows // N_DEV, LANES), False)
    mine = _sum_parts("sum_rep", parts)
    g_all = _exchange("gather_rep", mine, True).reshape(rows, LANES)
    g_full = _unpack(g_all, [grads[n].shape for n in rep_names])
    g_local = []
    for n, g in zip(rep_names, g_full):
        if n in _SMALL_SHARDED:
            width = a[n].shape[-1]
            g = lax.dynamic_slice_in_dim(g, me * width, width, axis=g.ndim - 1)
        g_local.append(g)
    shapes = [a[n].shape for n in rep_names]
    res = _adamw("adamw_rep", _pack([a[n] for n in rep_names], chunk_rows), _pack([a["m_" + n] for n in rep_names], chunk_rows),
                 _pack([a["v_" + n] for n in rep_names], chunk_rows), _pack(g_local, chunk_rows)[None])
    unpacked = [_unpack(r, shapes) for r in res]
    for i, n in enumerate(rep_names):
        put(n, [u[i] for u in unpacked], shapes[i])

    outs = [loss, grad_x[None]]
    for kind in ("grad_", "delta_", "new_m_", "new_v_"):
        outs += [out[kind + n] for n in _WEIGHTS]
    return tuple(outs)
```

```python
import functools

import numpy as np
import jax
import jax.numpy as jnp
from jax import lax
from jax.experimental import pallas as pl
from jax.experimental.pallas import tpu as pltpu

F32 = jnp.float32
BF16 = jnp.bfloat16

N_DEV = 8
D_MODEL = 1024
DEPTH = 2
GRID_W = 64
CHUNK = 128
LN_EPS = 1e-6
LRU_BLOCKS = 8
LRU_BS = 128
LRU_C = 8.0
SSD_INNER = 2048
SSD_GROUPS = 8
SSD_HPG = 4
SSD_HEADDIM = 64
SSD_STATE = 128
ML_HEADS = 4
ML_HD = 256
D_FF = 4096
DN_ALPHA = (2 * DEPTH) ** 0.25
ADAM_LR, ADAM_B1, ADAM_B2, ADAM_EPS, ADAM_WD, ADAM_STEP = 0.001, 0.9, 0.999, 1e-08, 0.01, 10

VMEM_CAP = 60 * 1024 * 1024
SUBLANES = 8
LANES = 128

_IN_MAIN = ((0, 8192), (8256, 12352), (12368, 15440))
_IN_MAIN_WIDTHS = (1024, 1024, 2048, 2048, 1024, 1024, 1024, 1024, 1024, 1024, 1024, 1024, 1024)
_IN_SMALL = ((8192, 8256), (12352, 12368))
_DT_LANE = 0
_MG_LANE = 64

_WEIGHTS = ['c_ctx', 'w_ada', 'b_ada', 'w_in', 'lru_conv_w', 'lru_conv_b', 'lru_w_r', 'lru_b_r', 'lru_w_i', 'lru_b_i',
            'lru_lambda', 'ssd_conv_w', 'ssd_conv_b', 'ssd_dt_bias', 'ssd_a_log', 'ssd_d', 'ssd_norm_w', 'ml_conv_w',
            'ml_conv_b', 'ml_gate_b', 'ml_norm_w', 'w_br_a', 'w_br_b', 'w_br_c', 'w_out', 'b_out', 'ln1_g', 'ln1_b',
            'w_ff1', 'b_ff1', 'w_ff2', 'b_ff2', 'ln2_g', 'ln2_b']
_BIG = {'w_ada': 2, 'w_in': 2, 'w_ff1': 2, 'w_br_a': 1, 'w_br_b': 1, 'w_br_c': 1, 'w_out': 1, 'w_ff2': 1}
_SMALL_SHARDED = ['lru_conv_w', 'lru_b_r', 'lru_b_i', 'lru_lambda', 'ssd_conv_w', 'ml_conv_w']
_REPLICATED = [n for n in _WEIGHTS if n not in _BIG and n not in _SMALL_SHARDED]


def _params(vmem_bytes):
    return pltpu.CompilerParams(vmem_limit_bytes=int(min(max(2 * vmem_bytes, 32 << 20), VMEM_CAP)))


def _row_tile(n_rows, bytes_per_row, budget=6 << 20, cap=512):
    t = cap
    while t > SUBLANES and (t * bytes_per_row > budget or n_rows % t):
        t //= 2
    assert n_rows % t == 0, (n_rows, t)
    return t


def _dg(a, b, ca, cb):
    return lax.dot_general(a.astype(BF16), b.astype(BF16), (((ca,), (cb,)), ((), ())), preferred_element_type=F32)


def _make_bdot(ca, cb):
    @jax.custom_vjp
    def f(a, b):
        return _dg(a, b, ca, cb)

    def fwd(a, b):
        return _dg(a, b, ca, cb), (a, b)

    def bwd(res, g):
        a, b = res
        da = _dg(g, b, 1, 1 - cb) if ca == 1 else _dg(b, g, 1 - cb, 1)
        db = _dg(a, g, 1 - ca, 0) if cb == 0 else _dg(g, a, 0, 1 - ca)
        return da, db

    f.defvjp(fwd, bwd)
    return f


_mm_nn = _make_bdot(1, 0)
_mm_nt = _make_bdot(1, 1)
_mm_tn = _make_bdot(0, 0)


@jax.custom_vjp
def _round_bf16(x):
    return x.astype(BF16).astype(F32)


_round_bf16.defvjp(lambda x: (_round_bf16(x), None), lambda _, g: (g,))


def _exact_dot(a, b):
    return jnp.dot(a, b, precision=lax.Precision.HIGHEST, preferred_element_type=F32)


def _layernorm_rows(x):
    mu = jnp.mean(x, -1, keepdims=True)
    var = jnp.mean(jnp.square(x - mu), -1, keepdims=True)
    return (x - mu) * lax.rsqrt(var + LN_EPS)


def _rowwise_calls(name, f, rows, params, out_widths, out_dtype=F32, to_linear=()):
    drow_dtypes = [BF16 if i in to_linear else F32 for i in range(len(rows))]
    nr, npar, no = len(rows), len(params), len(out_widths)
    n_rows = rows[0].shape[0]
    row_w = [r.shape[1] for r in rows]
    par_bytes = sum(int(np.prod(p.shape)) * 4 for p in params)
    tile = _row_tile(n_rows, 4 * (2 * sum(row_w) + 2 * sum(out_widths)))
    grid = (n_rows // tile,)

    def row_spec(w):
        return pl.BlockSpec((tile, w), lambda i: (i, 0))

    def par_spec(p):
        return pl.BlockSpec(p.shape, lambda i: (0, 0))

    vmem = 2 * tile * 4 * (2 * sum(row_w) + 3 * sum(out_widths)) + 4 * par_bytes

    def fwd_call(rows, params):
        def kern(*refs):
            outs = f(*[r[...] for r in refs[:nr + npar]])
            for r, o in zip(refs[nr + npar:], outs):
                r[...] = o.astype(out_dtype)

        return pl.pallas_call(
            kern, grid=grid, name=name + "_fwd",
            in_specs=[row_spec(w) for w in row_w] + [par_spec(p) for p in params],
            out_specs=[row_spec(w) for w in out_widths],
            out_shape=[jax.ShapeDtypeStruct((n_rows, w), out_dtype) for w in out_widths],
            compiler_params=_params(vmem),
        )(*rows, *params)

    def bwd_call(rows, params, gouts):
        def kern(*refs):
            ins = [r[...] for r in refs[:nr + npar]]
            gs = tuple(r[...] for r in refs[nr + npar:nr + npar + no])
            grads = jax.vjp(f, *ins)[1](gs)
            drefs = refs[nr + npar + no:]
            for k in range(nr):
                drefs[k][...] = grads[k].astype(drefs[k].dtype)

            @pl.when(pl.program_id(0) == 0)
            def _():
                for k in range(npar):
                    drefs[nr + k][...] = jnp.zeros_like(drefs[nr + k])

            for k in range(npar):
                drefs[nr + k][...] += grads[nr + k]

        res = pl.pallas_call(
            kern, grid=grid, name=name + "_bwd",
            in_specs=[row_spec(w) for w in row_w] + [par_spec(p) for p in params] + [row_spec(w) for w in out_widths],
            out_specs=[row_spec(w) for w in row_w] + [par_spec(p) for p in params],
            out_shape=[jax.ShapeDtypeStruct(r.shape, dt) for r, dt in zip(rows, drow_dtypes)]
            + [jax.ShapeDtypeStruct(p.shape, F32) for p in params],
            compiler_params=_params(vmem),
        )(*rows, *params, *gouts)
        return tuple(r.astype(F32) for r in res[:nr]), tuple(res[nr:])

    return fwd_call, bwd_call


def _rowwise(name, f, rows, params, out_widths, to_linear=()):
    rows, params = tuple(rows), tuple(params)
    fwd_call, bwd_call = _rowwise_calls(name, f, rows, params, out_widths, to_linear=to_linear)

    @jax.custom_vjp
    def op(rows, params):
        return tuple(fwd_call(rows, params))

    op.defvjp(lambda r, p: (tuple(fwd_call(r, p)), (r, p)), lambda res, g: bwd_call(res[0], res[1], g))
    return op(rows, params)


def _rowwise_linear(name, f, rows, params, weights, to_linear=()):
    rows, params = tuple(rows), tuple(params)
    ws = tuple(w for w, _ in weights)
    m, k = rows[0].shape[0], ws[0].shape[0]
    row_fwd, row_bwd = _rowwise_calls(name, f, rows, params, [k], BF16, to_linear)
    lin = [_linear_calls(name + "lin%d" % i, m, k, w.shape[1], wd, BF16) for i, (w, wd) in enumerate(weights)]
    counts = [len(c[3]) for c in lin]

    def fwd(rows, params, ws):
        (a,) = row_fwd(rows, params)
        outs = []
        for (fwd_call, _, _, _), w in zip(lin, ws):
            outs += list(fwd_call(a, w))
        return tuple(outs), a

    @jax.custom_vjp
    def op(rows, params, ws):
        return fwd(rows, params, ws)[0]

    def op_fwd(rows, params, ws):
        outs, a = fwd(rows, params, ws)
        return outs, (rows, params, ws, a)

    def op_bwd(res, g):
        rows, params, ws, a = res
        da, dws, off = None, [], 0
        for (_, dgrad_call, wgrad_call, _), w, cnt in zip(lin, ws, counts):
            gk = g[off:off + cnt]
            off += cnt
            d = dgrad_call(w, gk)
            da = d if da is None else da + d
            dws.append(wgrad_call(a, gk))
        drows, dparams = row_bwd(rows, params, (da,))
        return drows, dparams, tuple(dws)

    op.defvjp(op_fwd, op_bwd)
    return op(rows, params, ws)


def _group_ranges(widths, tn):
    starts, s = [], 0
    for w in widths:
        assert w % tn == 0, (w, tn)
        starts.append((s // tn, (s + w) // tn))
        s += w
    return starts, s // tn


def _group_tile(refs, ranges, row_tile, col_tile, i, j):
    out = []
    for ref, (s, e) in zip(refs, ranges):
        cols = pl.ds(pl.multiple_of((j - s) * col_tile, col_tile), col_tile)
        out.append(((j >= s) & (j < e), ref, cols))
    return [(p, lambda r=r, c=c: r.at[pl.ds(pl.multiple_of(i * row_tile, row_tile), row_tile), c]) for p, r, c in out]


def _linear_calls(name, m, k, n, widths, a_dtype):
    widths = (n,) if widths is None else tuple(widths)
    ng = len(widths)
    cast_a = a_dtype != BF16
    tn = 128 if n < 256 else (256 if k > 2048 or n % 512 else 512)
    tm = _row_tile(m, 0, cap=1024 if k <= 2048 else 512)
    ranges, nt = _group_ranges(widths, tn)
    mt = m // tm
    tn_w = 1024 if (k <= 1024 and all(wd % 1024 == 0 for wd in widths)) else (512 if all(wd % 512 == 0 for wd in widths) else tn)
    tm_w = _row_tile(m, 0, cap=512)
    ranges_w, nt_w = _group_ranges(widths, tn_w)
    mt_w = m // tm_w
    hbm = pl.BlockSpec(memory_space=pl.ANY)

    def fwd_call(a, w):
        n_steps = mt * nt

        def kern(a_ref, w_ref, *rest):
            outs, obuf, osem = rest[:ng], rest[ng], rest[ng + 1]
            a_bf = rest[ng + 2] if cast_a else a_ref
            i, j = pl.program_id(0), pl.program_id(1)
            step = i * nt + j
            slot = lax.rem(step, 2)

            def drain(sl):
                pltpu.make_async_copy(obuf.at[sl], outs[0].at[pl.ds(0, tm), pl.ds(0, tn)], osem.at[sl]).wait()

            if cast_a:
                @pl.when(j == 0)
                def _():
                    a_bf[...] = a_ref[...].astype(BF16)

            @pl.when(step >= 2)
            def _():
                drain(slot)

            obuf[slot] = jnp.dot(a_bf[...], w_ref[...], preferred_element_type=F32)
            for pred, window in _group_tile(outs, ranges, tm, tn, i, j):
                @pl.when(pred)
                def _(window=window):
                    pltpu.make_async_copy(obuf.at[slot], window(), osem.at[slot]).start()

            @pl.when(step == n_steps - 1)
            def _():
                drain(slot)
                if n_steps > 1:
                    drain(1 - slot)

        return pl.pallas_call(
            kern, grid=(mt, nt), name=name + "_fwd",
            in_specs=[pl.BlockSpec((tm, k), lambda i, j: (i, 0)), pl.BlockSpec((k, tn), lambda i, j: (0, j))],
            out_specs=[hbm] * ng,
            out_shape=[jax.ShapeDtypeStruct((m, wd), F32) for wd in widths],
            scratch_shapes=[pltpu.VMEM((2, tm, tn), F32), pltpu.SemaphoreType.DMA((2,))]
            + ([pltpu.VMEM((tm, k), BF16)] if cast_a else []),
            compiler_params=_params(10 * tm * k + 4 * k * tn + 8 * tm * tn),
        )(a, w.astype(BF16))

    def prefetched(gs, gbuf, gsem, rngs, row_tile, col_tile, step, n_steps, tile_of):
        slot = lax.rem(step, 2)

        def start(s_idx, sl):
            ii, jj = tile_of(s_idx)
            for pred, window in _group_tile(gs, rngs, row_tile, col_tile, ii, jj):
                @pl.when(pred)
                def _(window=window):
                    pltpu.make_async_copy(window(), gbuf.at[sl], gsem.at[sl]).start()

        @pl.when(step == 0)
        def _():
            start(step, slot)

        @pl.when(step + 1 < n_steps)
        def _():
            start(step + 1, 1 - slot)

        pltpu.make_async_copy(gs[0].at[pl.ds(0, row_tile), pl.ds(0, col_tile)], gbuf.at[slot], gsem.at[slot]).wait()
        return slot

    def dgrad_call(w, gouts):
        def kern(w_ref, *rest):
            gs, da, gbuf, gsem = rest[:ng], rest[ng], rest[ng + 1], rest[ng + 2]
            i, j = pl.program_id(0), pl.program_id(1)
            slot = prefetched(gs, gbuf, gsem, ranges, tm, tn, i * nt + j, mt * nt, lambda s: (s // nt, lax.rem(s, nt)))

            @pl.when(j == 0)
            def _():
                da[...] = jnp.zeros_like(da)

            da[...] += lax.dot_general(gbuf[slot].astype(BF16), w_ref[...], (((1,), (1,)), ((), ())), preferred_element_type=F32)

        return pl.pallas_call(
            kern, grid=(mt, nt), name=name + "_dgrad",
            in_specs=[pl.BlockSpec((k, tn), lambda i, j: (0, j))] + [hbm] * ng,
            out_specs=pl.BlockSpec((tm, k), lambda i, j: (i, 0)),
            out_shape=jax.ShapeDtypeStruct((m, k), F32),
            scratch_shapes=[pltpu.VMEM((2, tm, tn), BF16), pltpu.SemaphoreType.DMA((2,))],
            compiler_params=_params(12 * tm * k + 4 * k * tn + 10 * tm * tn),
        )(w.astype(BF16), *[g.astype(BF16) for g in gouts])

    def wgrad_call(a, gouts):
        def kern(a_ref, *rest):
            gs, dw, gbuf, gsem = rest[:ng], rest[ng], rest[ng + 1], rest[ng + 2]
            j, i = pl.program_id(0), pl.program_id(1)
            slot = prefetched(gs, gbuf, gsem, ranges_w, tm_w, tn_w, j * mt_w + i, mt_w * nt_w,
                              lambda s: (lax.rem(s, mt_w), s // mt_w))

            @pl.when(i == 0)
            def _():
                dw[...] = jnp.zeros_like(dw)

            dw[...] += lax.dot_general(a_ref[...].astype(BF16), gbuf[slot].astype(BF16), (((0,), (0,)), ((), ())),
                                       preferred_element_type=F32)

        return pl.pallas_call(
            kern, grid=(nt_w, mt_w), name=name + "_wgrad",
            in_specs=[pl.BlockSpec((tm_w, k), lambda j, i: (i, 0))] + [hbm] * ng,
            out_specs=pl.BlockSpec((k, tn_w), lambda j, i: (0, j)),
            out_shape=jax.ShapeDtypeStruct((k, n), F32),
            scratch_shapes=[pltpu.VMEM((2, tm_w, tn_w), BF16), pltpu.SemaphoreType.DMA((2,))],
            compiler_params=_params(12 * tm_w * k + 12 * k * tn_w + 10 * tm_w * tn_w),
        )(a, *[g.astype(BF16) for g in gouts])

    return fwd_call, dgrad_call, wgrad_call, widths


def _linear(name, a, w, widths=None):
    fwd_call, dgrad_call, wgrad_call, _ = _linear_calls(name, a.shape[0], a.shape[1], w.shape[1], widths, a.dtype)

    @jax.custom_vjp
    def op(a, w):
        return tuple(fwd_call(a, w))

    op.defvjp(lambda a, w: (tuple(fwd_call(a, w)), (a, w)),
              lambda res, g: (dgrad_call(res[1], g), wgrad_call(res[0], g)))
    out = op(a, w)
    return out[0] if widths is None else out


def _conv_taps(x_ext, w, n_ext):
    xm2 = pltpu.roll(x_ext, 2, 0)
    xm1 = pltpu.roll(x_ext, 1, 0)
    xp1 = pltpu.roll(x_ext, n_ext - 1, 0)
    return xm2, xm1, xp1


def _dwconv(name, x, w, b, act):
    n_rows, ch = x.shape
    tt = _row_tile(n_rows, 4 * 6 * ch, cap=256)
    nt = n_rows // tt
    n_ext = tt + 2 * SUBLANES
    per8 = tt // SUBLANES
    last8 = n_rows // SUBLANES - 1
    main = pl.BlockSpec((tt, ch), lambda i: (i, 0))
    prev = pl.BlockSpec((SUBLANES, ch), lambda i: (jnp.maximum(i * per8 - 1, 0), 0))
    nxt = pl.BlockSpec((SUBLANES, ch), lambda i: (jnp.minimum((i + 1) * per8, last8), 0))
    wspec = pl.BlockSpec((4, ch), lambda i: (0, 0))
    bspec = pl.BlockSpec((1, ch), lambda i: (0, 0))
    vmem = 4 * n_ext * ch * 14

    def ext(main_ref, prev_ref, next_ref):
        i = pl.program_id(0)
        p = jnp.where(i > 0, prev_ref[...], 0.0)
        q = jnp.where(i < nt - 1, next_ref[...], 0.0)
        return jnp.concatenate([p, main_ref[...], q], axis=0)

    def pre_of(x_ext, wv, bv):
        xm2, xm1, xp1 = _conv_taps(x_ext, wv, n_ext)
        pre = wv[0:1] * xm2 + wv[1:2] * xm1 + wv[2:3] * x_ext + wv[3:4] * xp1 + bv
        return pre, (xm2, xm1, xp1)

    def fwd_call(x, w, b):
        def kern(xm, xp, xn, w_ref, b_ref, o_ref):
            pre, _ = pre_of(ext(xm, xp, xn), w_ref[...], b_ref[...])
            pre = pre[SUBLANES:SUBLANES + tt]
            o_ref[...] = pre * jax.nn.sigmoid(pre) if act else pre

        return pl.pallas_call(
            kern, grid=(nt,), name=name + "_fwd", in_specs=[main, prev, nxt, wspec, bspec], out_specs=main,
            out_shape=jax.ShapeDtypeStruct((n_rows, ch), F32), compiler_params=_params(vmem),
        )(x, x, x, w, b)

    def bwd_call(x, w, b, dy):
        def kern(xm, xp, xn, gm, gp, gn, w_ref, b_ref, dx_ref, dw_ref, db_ref):
            wv = w_ref[...]
            x_ext = ext(xm, xp, xn)
            pre, (xm2, xm1, xp1) = pre_of(x_ext, wv, b_ref[...])
            dpre = ext(gm, gp, gn)
            if act:
                sg = jax.nn.sigmoid(pre)
                dpre = dpre * (sg + pre * sg * (1.0 - sg))
            dx = (wv[0:1] * pltpu.roll(dpre, n_ext - 2, 0) + wv[1:2] * pltpu.roll(dpre, n_ext - 1, 0)
                  + wv[2:3] * dpre + wv[3:4] * pltpu.roll(dpre, 1, 0))
            sl = slice(SUBLANES, SUBLANES + tt)
            dx_ref[...] = dx[sl].astype(dx_ref.dtype)
            dm = dpre[sl]

            @pl.when(pl.program_id(0) == 0)
            def _():
                dw_ref[...] = jnp.zeros_like(dw_ref)
                db_ref[...] = jnp.zeros_like(db_ref)

            dw_ref[...] += jnp.concatenate(
                [jnp.sum(dm * t[sl], axis=0, keepdims=True) for t in (xm2, xm1, x_ext, xp1)], axis=0)
            db_ref[...] += jnp.sum(dm, axis=0, keepdims=True)

        return pl.pallas_call(
            kern, grid=(nt,), name=name + "_bwd", in_specs=[main, prev, nxt, main, prev, nxt, wspec, bspec],
            out_specs=[main, wspec, bspec],
            out_shape=[jax.ShapeDtypeStruct((n_rows, ch), BF16), jax.ShapeDtypeStruct((4, ch), F32),
                       jax.ShapeDtypeStruct((1, ch), F32)],
            compiler_params=_params(vmem),
        )(x, x, x, dy, dy, dy, w, b)

    @jax.custom_vjp
    def op(x, w, b):
        return fwd_call(x, w, b)

    def op_bwd(res, g):
        dx, dw, db = bwd_call(*res, g)
        return dx.astype(F32), dw, db

    op.defvjp(lambda x, w, b: (fwd_call(x, w, b), (x, w, b)), op_bwd)
    return op(x, w, b)


def _scan_groups(tt, ch, reverse, load, store, carry_ref):
    row = lax.broadcasted_iota(jnp.int32, (SUBLANES, ch), 0)
    ng = tt // SUBLANES

    def body(k, carry):
        g = (ng - 1 - k) if reverse else k
        sl = pl.ds(pl.multiple_of(g * SUBLANES, SUBLANES), SUBLANES)
        a, b, extra = load(sl)
        for s in (1, 2, 4):
            if reverse:
                a_sh, b_sh, valid = pltpu.roll(a, SUBLANES - s, 0), pltpu.roll(b, SUBLANES - s, 0), row < SUBLANES - s
            else:
                a_sh, b_sh, valid = pltpu.roll(a, s, 0), pltpu.roll(b, s, 0), row >= s
            b = jnp.where(valid, b + a * b_sh, b)
            a = jnp.where(valid, a * a_sh, a)
        h = b + a * carry
        if reverse:
            h_prev = jnp.where(row == SUBLANES - 1, carry, pltpu.roll(h, SUBLANES - 1, 0))
            last = h[0:1]
        else:
            h_prev = jnp.where(row == 0, carry, pltpu.roll(h, 1, 0))
            last = h[SUBLANES - 1:SUBLANES]
        store(sl, h, h_prev, extra)
        return jnp.broadcast_to(last, (SUBLANES, ch))

    carry_ref[...] = lax.fori_loop(0, ng, body, carry_ref[...])


def _lin_scan(name, a, b, h0, reverse):
    n_rows, ch = a.shape
    tt = _row_tile(n_rows, 0, cap=256)
    nt = n_rows // tt
    vmem = 2 * 4 * tt * ch * 5

    def tile_spec(rev):
        return pl.BlockSpec((tt, ch), (lambda i: (nt - 1 - i, 0)) if rev else (lambda i: (i, 0)))

    vec = pl.BlockSpec((1, ch), lambda i: (0, 0))

    def fwd_call(a, b, h0):
        def kern(a_ref, b_ref, h0_ref, h_ref, hp_ref, last_ref, carry):
            @pl.when(pl.program_id(0) == 0)
            def _():
                carry[...] = jnp.broadcast_to(h0_ref[...], carry.shape)

            def load(sl):
                return a_ref[sl, :], b_ref[sl, :], None

            def store(sl, h, h_prev, _):
                h_ref[sl, :] = h
                hp_ref[sl, :] = h_prev

            _scan_groups(tt, ch, reverse, load, store, carry)
            last_ref[...] = carry[0:1]

        return pl.pallas_call(
            kern, grid=(nt,), name=name + "_fwd", in_specs=[tile_spec(reverse), tile_spec(reverse), vec],
            out_specs=[tile_spec(reverse), tile_spec(reverse), vec],
            out_shape=[jax.ShapeDtypeStruct((n_rows, ch), F32)] * 2 + [jax.ShapeDtypeStruct((1, ch), F32)],
            scratch_shapes=[pltpu.VMEM((SUBLANES, ch), F32)], compiler_params=_params(vmem),
        )(a, b, h0)

    def bwd_call(a, h_prev, dh, dlast):
        rev = not reverse

        def kern(a_ref, hp_ref, dh_ref, dl_ref, da_ref, db_ref, d0_ref, carry):
            @pl.when(pl.program_id(0) == 0)
            def _():
                carry[...] = jnp.broadcast_to(dl_ref[...], carry.shape)

            def load(sl):
                av, dv = a_ref[sl, :], dh_ref[sl, :]
                return av, av * dv, dv

            def store(sl, u, u_next, dv):
                g = dv + u_next
                db_ref[sl, :] = g
                da_ref[sl, :] = g * hp_ref[sl, :]

            _scan_groups(tt, ch, rev, load, store, carry)
            d0_ref[...] = carry[0:1]

        return pl.pallas_call(
            kern, grid=(nt,), name=name + "_bwd", in_specs=[tile_spec(rev)] * 3 + [vec],
            out_specs=[tile_spec(rev), tile_spec(rev), vec],
            out_shape=[jax.ShapeDtypeStruct((n_rows, ch), F32)] * 2 + [jax.ShapeDtypeStruct((1, ch), F32)],
            scratch_shapes=[pltpu.VMEM((SUBLANES, ch), F32)], compiler_params=_params(vmem),
        )(a, h_prev, dh, dlast)

    @jax.custom_vjp
    def op(a, b, h0):
        h, _, last = fwd_call(a, b, h0)
        return h, last

    def op_fwd(a, b, h0):
        h, h_prev, last = fwd_call(a, b, h0)
        return (h, last), (a, h_prev)

    def op_bwd(res, g):
        da, db, d0 = bwd_call(res[0], res[1], g[0], g[1])
        return da, db, d0

    op.defvjp(op_fwd, op_bwd)
    return op(a, b, h0)


def _tri(reverse):
    q = lax.broadcasted_iota(jnp.int32, (CHUNK, CHUNK), 0)
    s = lax.broadcasted_iota(jnp.int32, (CHUNK, CHUNK), 1)
    return (q <= s) if reverse else (q >= s)


def _pick_col(x, lane):
    idx = lax.broadcasted_iota(jnp.int32, x.shape, 1)
    return jnp.sum(jnp.where(idx == lane, x, 0.0), axis=1, keepdims=True)


def _pick_row(x, row):
    idx = lax.broadcasted_iota(jnp.int32, x.shape, 0)
    return jnp.sum(jnp.where(idx == row, x, 0.0), axis=0, keepdims=True)


def _ssd_shared(small, bias_row, alog_row, reverse):
    delta_all = jax.nn.softplus(small + bias_row)
    acs_all = _exact_dot(_tri(reverse).astype(F32), delta_all * (-jnp.exp(alog_row)))
    return delta_all, acs_all, acs_all.T


def _ssd_group(xs, bm, cm, state, delta_all, acs_all, acs_t, g, direction, reverse):
    mask = _tri(reverse)
    last = 0 if reverse else CHUNK - 1
    hd = SSD_HEADDIM
    rowi = lax.broadcasted_iota(jnp.int32, (CHUNK, 1), 0)
    a_cols, a_rows, deltas, tots = [], [], [], []
    for r in range(SSD_HPG):
        lane = _DT_LANE + 32 * direction + SSD_HPG * g + r
        a_col = _pick_col(acs_all, lane)
        a_cols.append(a_col)
        a_rows.append(_pick_row(acs_t, lane))
        deltas.append(_pick_col(delta_all, lane))
        tots.append(jnp.sum(jnp.where(rowi == last, a_col, 0.0), axis=0, keepdims=True))

    def wide(cols, rows):
        return jnp.concatenate([jnp.broadcast_to(c, (rows, hd)) for c in cols], axis=1)

    a_w = wide(a_cols, CHUNK)
    x_w = xs * wide(deltas, CHUNK)
    st = _mm_tn(x_w * jnp.exp(wide(tots, 1) - a_w), bm)
    y_off = _mm_nt(cm, state) * jnp.exp(a_w)
    grow = jnp.concatenate([jnp.broadcast_to(jnp.exp(t), (hd, 1)) for t in tots], axis=0)
    cb = _mm_nt(cm, bm)
    m_cat = jnp.concatenate([cb * jnp.exp(jnp.where(mask, a_cols[r] - a_rows[r], -jnp.inf)) for r in range(SSD_HPG)], axis=1)
    lane_head = lax.broadcasted_iota(jnp.int32, (1, SSD_HPG * hd), 1) // hd
    x_bd = jnp.concatenate([jnp.where(lane_head == r, x_w, 0.0) for r in range(SSD_HPG)], axis=0)
    return _mm_nn(m_cat, x_bd) + y_off, grow * state + st


def _ssd_scan(name, xs, bm, cm, small, bias_row, alog_row, s0, direction, reverse):
    n_rows = xs.shape[0]
    nc = n_rows // CHUNK
    gw = SSD_HPG * SSD_HEADDIM
    vmem = 4 * CHUNK * (gw + 3 * 128) * 8 + 4 * gw * 128 * 12 + (8 << 20)

    n_state = SSD_GROUPS * gw
    shared_scratch = [pltpu.VMEM((CHUNK, LANES), F32), pltpu.VMEM((CHUNK, LANES), F32), pltpu.VMEM((LANES, CHUNK), F32)]

    def specs(order, gps=1):
        def cidx(c):
            return (nc - 1 - c) if order else c

        return dict(
            xs=pl.BlockSpec((CHUNK, gps * gw), lambda c, g: (cidx(c), g)),
            bc=pl.BlockSpec((CHUNK, gps * SSD_STATE), lambda c, g: (cidx(c), g)),
            small=pl.BlockSpec((CHUNK, LANES), lambda c, g: (cidx(c), 0)),
            row=pl.BlockSpec((1, LANES), lambda c, g: (0, 0)),
            state=pl.BlockSpec((n_state, SSD_STATE), lambda c, g: (0, 0)),
            enter=pl.BlockSpec((1, gps * gw, SSD_STATE), lambda c, g: (cidx(c), g, 0)),
        )

    gps_fwd, gps_bwd = 2, 1

    def group_rows(g, gps):
        return pl.ds(pl.multiple_of(g * gps * gw, gps * gw), gps * gw)

    def step_fn(g, gps):
        def fn(xs_v, bm_v, cm_v, st_v, d_all, a_all, a_t):
            ys, sts = [], []
            for u in range(gps):
                y_u, s_u = _ssd_group(xs_v[:, u * gw:(u + 1) * gw], bm_v[:, u * SSD_STATE:(u + 1) * SSD_STATE],
                                      cm_v[:, u * SSD_STATE:(u + 1) * SSD_STATE], st_v[u * gw:(u + 1) * gw],
                                      d_all, a_all, a_t, gps * g + u, direction, reverse)
                ys.append(y_u)
                sts.append(s_u)
            return jnp.concatenate(ys, axis=1), jnp.concatenate(sts, axis=0)

        return fn

    def fwd_call(xs, bm, cm, small, bias_row, alog_row, s0):
        gps = gps_fwd
        sp = specs(reverse, gps)

        def kern(xs_r, bm_r, cm_r, sm_r, br_r, ar_r, s0_r, y_r, sf_r, se_r, st, sh_d, sh_a, sh_t):
            c, g = pl.program_id(0), pl.program_id(1)

            @pl.when((c == 0) & (g == 0))
            def _():
                st[...] = s0_r[...]

            @pl.when(g == 0)
            def _():
                sh_d[...], sh_a[...], sh_t[...] = _ssd_shared(sm_r[...], br_r[...], ar_r[...], reverse)

            rows = group_rows(g, gps)
            s_in = st[rows, :]
            se_r[0] = s_in
            y_r[...], s_new = step_fn(g, gps)(xs_r[...], bm_r[...], cm_r[...], s_in, sh_d[...], sh_a[...], sh_t[...])
            st[rows, :] = s_new
            sf_r[rows, :] = s_new

        return pl.pallas_call(
            kern, grid=(nc, SSD_GROUPS // gps), name=name + "_fwd",
            in_specs=[sp['xs'], sp['bc'], sp['bc'], sp['small'], sp['row'], sp['row'], sp['state']],
            out_specs=[sp['xs'], sp['state'], sp['enter']],
            out_shape=[jax.ShapeDtypeStruct((n_rows, SSD_INNER), F32), jax.ShapeDtypeStruct((n_state, SSD_STATE), F32),
                       jax.ShapeDtypeStruct((nc, n_state, SSD_STATE), F32)],
            scratch_shapes=[pltpu.VMEM((n_state, SSD_STATE), F32)] + shared_scratch, compiler_params=_params(vmem),
        )(xs, bm, cm, small, bias_row, alog_row, s0)

    def bwd_call(xs, bm, cm, small, bias_row, alog_row, enter, dy, dsf):
        gps = gps_bwd
        sp = specs(not reverse, gps)

        def kern(xs_r, bm_r, cm_r, sm_r, br_r, ar_r, se_r, dy_r, dsf_r, dxs_r, dbm_r, dcm_r, dsm_r, dbr_r, dar_r, ds0_r,
                 ds, sh_d, sh_a, sh_t, gd, ga, gt):
            c, g = pl.program_id(0), pl.program_id(1)

            @pl.when((c == 0) & (g == 0))
            def _():
                ds[...] = dsf_r[...]
                dbr_r[...] = jnp.zeros_like(dbr_r)
                dar_r[...] = jnp.zeros_like(dar_r)

            @pl.when(g == 0)
            def _():
                sh_d[...], sh_a[...], sh_t[...] = _ssd_shared(sm_r[...], br_r[...], ar_r[...], reverse)
                gd[...] = jnp.zeros_like(gd)
                ga[...] = jnp.zeros_like(ga)
                gt[...] = jnp.zeros_like(gt)

            rows = group_rows(g, gps)
            _, vjp = jax.vjp(step_fn(g, gps), xs_r[...], bm_r[...], cm_r[...], se_r[0], sh_d[...], sh_a[...], sh_t[...])
            dxs, dbm, dcm, ds_in, dd, da, dt = vjp((dy_r[...], ds[rows, :]))
            dxs_r[...] = dxs
            dbm_r[...] = dbm
            dcm_r[...] = dcm
            ds[rows, :] = ds_in
            ds0_r[rows, :] = ds_in
            gd[...] += dd
            ga[...] += da
            gt[...] += dt

            @pl.when(g == SSD_GROUPS // gps - 1)
            def _():
                shared = functools.partial(_ssd_shared, reverse=reverse)
                dsm, dbr, dar = jax.vjp(shared, sm_r[...], br_r[...], ar_r[...])[1]((gd[...], ga[...], gt[...]))
                dsm_r[...] = dsm
                dbr_r[...] += dbr
                dar_r[...] += dar

        return pl.pallas_call(
            kern, grid=(nc, SSD_GROUPS // gps), name=name + "_bwd",
            in_specs=[sp['xs'], sp['bc'], sp['bc'], sp['small'], sp['row'], sp['row'], sp['enter'], sp['xs'], sp['state']],
            out_specs=[sp['xs'], sp['bc'], sp['bc'], sp['small'], sp['row'], sp['row'], sp['state']],
            out_shape=[jax.ShapeDtypeStruct(xs.shape, F32), jax.ShapeDtypeStruct(bm.shape, F32),
                       jax.ShapeDtypeStruct(cm.shape, F32), jax.ShapeDtypeStruct((n_rows, LANES), F32),
                       jax.ShapeDtypeStruct((1, LANES), F32), jax.ShapeDtypeStruct((1, LANES), F32),
                       jax.ShapeDtypeStruct(s0.shape, F32)],
            scratch_shapes=[pltpu.VMEM((n_state, SSD_STATE), F32)] + shared_scratch + shared_scratch,
            compiler_params=_params(vmem),
        )(xs, bm, cm, small, bias_row, alog_row, enter, dy, dsf)

    @jax.custom_vjp
    def op(*args):
        y, sf, _ = fwd_call(*args)
        return y, sf

    def op_fwd(*args):
        y, sf, enter = fwd_call(*args)
        return (y, sf), (args[:6], enter)

    op.defvjp(op_fwd, lambda res, g: tuple(bwd_call(*res[0], res[1], g[0], g[1])))
    return op(xs, bm, cm, small, bias_row, alog_row, s0)


def _ml_shared(small, gate_row, reverse):
    gates = small + gate_row
    b_all = _exact_dot(_tri(reverse).astype(F32), jax.nn.log_sigmoid(gates))
    return gates, b_all, gates.T, b_all.T


def _ml_head(q, k, v, c_st, n_st, m_st, gates, b_all, gates_t, b_t, h, direction, reverse):
    mask = _tri(reverse)
    last = 0 if reverse else CHUNK - 1
    lane_i = _MG_LANE + 8 * direction + h
    lane_f = lane_i + ML_HEADS
    b_col = _pick_col(b_all, lane_f)
    b_row = _pick_row(b_t, lane_f)
    li_col = _pick_col(gates, lane_i)
    li_row = _pick_row(gates_t, lane_i)
    rowi = lax.broadcasted_iota(jnp.int32, (CHUNK, 1), 0)
    g_tot = jnp.sum(jnp.where(rowi == last, b_col, 0.0), axis=0, keepdims=True)
    m_in = m_st[:, 0:1]
    q = q * (ML_HD ** -0.5)
    w = g_tot - b_col + li_col
    m_loc = lax.stop_gradient(jnp.max(w, axis=0, keepdims=True))
    kw = k * jnp.exp(w - m_loc)
    c_loc = _mm_tn(kw, v)
    n_loc = jnp.sum(kw, axis=0, keepdims=True)
    m_new = lax.stop_gradient(jnp.maximum(g_tot + m_in, m_loc))
    s_old = jnp.exp(g_tot + m_in - m_new)
    s_loc = jnp.exp(m_loc - m_new)
    c_new = s_old * c_st + s_loc * c_loc
    n_new = s_old * n_st + s_loc * n_loc
    log_d = jnp.where(mask, b_col - b_row + li_row, -jnp.inf)
    inter = b_col + m_in
    m_t = lax.stop_gradient(jnp.maximum(inter, jnp.max(log_d, axis=1, keepdims=True)))
    dmat = jnp.exp(log_d - m_t)
    wi = jnp.exp(inter - m_t)
    s = _mm_nt(q, k) * dmat
    num = _mm_nn(s, v) + wi * _mm_nn(q, c_st)
    den = jnp.sum(s, axis=1, keepdims=True) + wi * jnp.sum(_round_bf16(q) * _round_bf16(n_st), axis=1, keepdims=True)
    out = num / jnp.maximum(jnp.abs(den), jnp.exp(-m_t))
    return out, c_new, n_new, jnp.broadcast_to(m_new, (1, LANES))


def _ml_scan(name, q, k, v, small, gate_row, c0, n0, m0, direction, reverse):
    n_rows = q.shape[0]
    nc = n_rows // CHUNK
    vmem = 4 * CHUNK * (4 * ML_HD + 128) * 8 + 4 * ML_HD * ML_HD * 12 + (8 << 20)

    def specs(order):
        def cidx(c):
            return (nc - 1 - c) if order else c

        return dict(
            qkv=pl.BlockSpec((CHUNK, ML_HD), lambda c, h: (cidx(c), h)),
            small=pl.BlockSpec((CHUNK, LANES), lambda c, h: (cidx(c), 0)),
            row=pl.BlockSpec((1, LANES), lambda c, h: (0, 0)),
            c=pl.BlockSpec((ML_HEADS * ML_HD, ML_HD), lambda c, h: (0, 0)),
            n=pl.BlockSpec((ML_HEADS, 1, ML_HD), lambda c, h: (0, 0, 0)),
            m=pl.BlockSpec((ML_HEADS, 1, LANES), lambda c, h: (0, 0, 0)),
            ec=pl.BlockSpec((1, ML_HD, ML_HD), lambda c, h: (cidx(c), h, 0)),
            en=pl.BlockSpec((1, 1, 1, ML_HD), lambda c, h: (cidx(c), h, 0, 0)),
            em=pl.BlockSpec((1, 1, 1, LANES), lambda c, h: (cidx(c), h, 0, 0)),
        )

    st_shapes = [jax.ShapeDtypeStruct((ML_HEADS * ML_HD, ML_HD), F32), jax.ShapeDtypeStruct((ML_HEADS, 1, ML_HD), F32),
                 jax.ShapeDtypeStruct((ML_HEADS, 1, LANES), F32)]
    scratch = [pltpu.VMEM((ML_HEADS * ML_HD, ML_HD), F32), pltpu.VMEM((ML_HEADS, 1, ML_HD), F32),
               pltpu.VMEM((ML_HEADS, 1, LANES), F32)]
    shared_scratch = [pltpu.VMEM((CHUNK, LANES), F32), pltpu.VMEM((CHUNK, LANES), F32),
                      pltpu.VMEM((LANES, CHUNK), F32), pltpu.VMEM((LANES, CHUNK), F32)]

    def head_rows(h):
        return pl.ds(pl.multiple_of(h * ML_HD, ML_HD), ML_HD)

    def fwd_call(q, k, v, small, gate_row, c0, n0, m0):
        sp = specs(reverse)

        def kern(q_r, k_r, v_r, sm_r, gr_r, c0_r, n0_r, m0_r, o_r, cf_r, nf_r, mf_r, ec_r, en_r, em_r, cs, ns, ms, *sh):
            c, h = pl.program_id(0), pl.program_id(1)

            @pl.when((c == 0) & (h == 0))
            def _():
                cs[...] = c0_r[...]
                ns[...] = n0_r[...]
                ms[...] = m0_r[...]

            @pl.when(h == 0)
            def _():
                for ref, val in zip(sh, _ml_shared(sm_r[...], gr_r[...], reverse)):
                    ref[...] = val

            rows = head_rows(h)
            c_in, n_in, m_in = cs[rows, :], ns[h], ms[h]
            ec_r[0] = c_in
            en_r[0, 0] = n_in
            em_r[0, 0] = m_in
            out, c_new, n_new, m_new = _ml_head(q_r[...], k_r[...], v_r[...], c_in, n_in, m_in, *[r[...] for r in sh],
                                                h, direction, reverse)
            o_r[...] = out
            cs[rows, :] = c_new
            ns[h] = n_new
            ms[h] = m_new
            cf_r[rows, :] = c_new
            nf_r[h] = n_new
            mf_r[h] = m_new

        return pl.pallas_call(
            kern, grid=(nc, ML_HEADS), name=name + "_fwd",
            in_specs=[sp['qkv']] * 3 + [sp['small'], sp['row'], sp['c'], sp['n'], sp['m']],
            out_specs=[sp['qkv'], sp['c'], sp['n'], sp['m'], sp['ec'], sp['en'], sp['em']],
            out_shape=[jax.ShapeDtypeStruct((n_rows, ML_HEADS * ML_HD), F32)] + st_shapes + [
                jax.ShapeDtypeStruct((nc, ML_HEADS * ML_HD, ML_HD), F32),
                jax.ShapeDtypeStruct((nc, ML_HEADS, 1, ML_HD), F32), jax.ShapeDtypeStruct((nc, ML_HEADS, 1, LANES), F32)],
            scratch_shapes=scratch + shared_scratch, compiler_params=_params(vmem),
        )(q, k, v, small, gate_row, c0, n0, m0)

    def bwd_call(q, k, v, small, gate_row, ec, en, em, do, dcf, dnf, dmf):
        sp = specs(not reverse)
        n_sh = len(shared_scratch)

        def kern(q_r, k_r, v_r, sm_r, gr_r, ec_r, en_r, em_r, do_r, dcf_r, dnf_r, dmf_r,
                 dq_r, dk_r, dv_r, dsm_r, dgr_r, dc0_r, dn0_r, dm0_r, dcs, dns, dms, *rest):
            sh, gsh = rest[:n_sh], rest[n_sh:]
            c, h = pl.program_id(0), pl.program_id(1)

            @pl.when((c == 0) & (h == 0))
            def _():
                dcs[...] = dcf_r[...]
                dns[...] = dnf_r[...]
                dms[...] = dmf_r[...]
                dgr_r[...] = jnp.zeros_like(dgr_r)

            @pl.when(h == 0)
            def _():
                for ref, val in zip(sh, _ml_shared(sm_r[...], gr_r[...], reverse)):
                    ref[...] = val
                for ref in gsh:
                    ref[...] = jnp.zeros_like(ref)

            rows = head_rows(h)
            fn = functools.partial(_ml_head, h=h, direction=direction, reverse=reverse)
            _, vjp = jax.vjp(fn, q_r[...], k_r[...], v_r[...], ec_r[0], en_r[0, 0], em_r[0, 0], *[r[...] for r in sh])
            grads = vjp((do_r[...], dcs[rows, :], dns[h], dms[h]))
            dq, dk, dv, dc, dn, dm = grads[:6]
            dq_r[...] = dq
            dk_r[...] = dk
            dv_r[...] = dv
            for ref, val in zip(gsh, grads[6:]):
                ref[...] += val
            dm = jnp.broadcast_to(jnp.sum(dm, axis=1, keepdims=True), (1, LANES)) * (1.0 / LANES)
            dcs[rows, :] = dc
            dns[h] = dn
            dms[h] = dm
            dc0_r[rows, :] = dc
            dn0_r[h] = dn
            dm0_r[h] = dm

            @pl.when(h == ML_HEADS - 1)
            def _():
                shared = functools.partial(_ml_shared, reverse=reverse)
                dsm, dgr = jax.vjp(shared, sm_r[...], gr_r[...])[1](tuple(r[...] for r in gsh))
                dsm_r[...] = dsm
                dgr_r[...] += dgr

        return pl.pallas_call(
            kern, grid=(nc, ML_HEADS), name=name + "_bwd",
            in_specs=[sp['qkv']] * 3 + [sp['small'], sp['row'], sp['ec'], sp['en'], sp['em'], sp['qkv'], sp['c'], sp['n'], sp['m']],
            out_specs=[sp['qkv']] * 3 + [sp['small'], sp['row'], sp['c'], sp['n'], sp['m']],
            out_shape=[jax.ShapeDtypeStruct(q.shape, F32)] * 3 + [jax.ShapeDtypeStruct((n_rows, LANES), F32),
                                                                  jax.ShapeDtypeStruct((1, LANES), F32)] + st_shapes,
            scratch_shapes=scratch + shared_scratch + shared_scratch, compiler_params=_params(vmem),
        )(q, k, v, small, gate_row, ec, en, em, do, dcf, dnf, dmf)

    @jax.custom_vjp
    def op(*args):
        return tuple(fwd_call(*args)[:4])

    def op_fwd(*args):
        res = fwd_call(*args)
        return tuple(res[:4]), (args[:5], tuple(res[4:]))

    op.defvjp(op_fwd, lambda res, g: tuple(bwd_call(*res[0], *res[1], *g)))
    return op(q, k, v, small, gate_row, c0, n0, m0)


def _f_modulate(x, shift, scale):
    return (_layernorm_rows(x) * (1.0 + scale) + shift,)


def _f_resid_ln(x, o, gate, bias, ln_g, ln_b):
    return (_layernorm_rows(DN_ALPHA * x + gate * (o + bias)) * ln_g + ln_b,)


def _f_lru_gates(xc, w_r, b_r, w_i, b_i, lam):
    outs = []
    for d in range(2):
        def blockdiag(w):
            return jnp.concatenate(
                [_mm_nn(xc[:, n * LRU_BS:(n + 1) * LRU_BS], w[(d * LRU_BLOCKS + n) * LRU_BS:(d * LRU_BLOCKS + n + 1) * LRU_BS, :])
                 for n in range(LRU_BLOCKS)], axis=1)

        r = jax.nn.sigmoid(blockdiag(w_r) + b_r[d:d + 1])
        i = jax.nn.sigmoid(blockdiag(w_i) + b_i[d:d + 1])
        log_a = -LRU_C * r * jax.nn.softplus(-lam[d:d + 1])
        outs += [jnp.exp(log_a), jnp.sqrt(1.0 - jnp.exp(2.0 * log_a)) * i * xc]
    return tuple(outs)


def _f_lru_out(h_f, h_b, ly):
    return ((h_f + h_b) * jax.nn.gelu(ly),)


def _f_ssd_post(y_f, y_b, xs, z, d_exp, norm_w):
    y = (y_f + y_b + xs * d_exp) * jax.nn.silu(z)
    gw = SSD_INNER // SSD_GROUPS
    parts = []
    for g in range(SSD_GROUPS):
        yg = y[:, g * gw:(g + 1) * gw]
        parts.append(yg * lax.rsqrt(jnp.mean(jnp.square(yg), -1, keepdims=True) + LN_EPS))
    return (jnp.concatenate(parts, axis=1) * norm_w,)


def _f_ml_post(h_f, h_b, o, norm_w):
    h = h_f + h_b
    parts = [_layernorm_rows(h[:, i * ML_HD:(i + 1) * ML_HD]) for i in range(ML_HEADS)]
    return (jnp.concatenate(parts, axis=1) * norm_w * jax.nn.sigmoid(o),)


def _f_merge(ga, gb, gc, pa, pb, pc):
    return (jax.nn.sigmoid(ga) * pa + jax.nn.sigmoid(gb) * pb + jax.nn.sigmoid(gc) * pc,)


def _f_relu2(pre, bias):
    return (jnp.square(jax.nn.relu(pre + bias)),)


def _lane_row(vec, start):
    return jnp.pad(vec[None], ((0, 0), (start, LANES - start - vec.shape[0])))


def _mixer(tag, x_tok, shift, scale, p, states):
    (lru_s, ssd_s, ml_s) = states
    lx, ly, sz, xs, bm, cm, mq, mk, mv, mo, ga, gb, gc, small = _rowwise_linear(
        tag + "in", _f_modulate, [x_tok], [shift, scale], [(p['w_in_main'], _IN_MAIN_WIDTHS), (p['w_in_small'], None)])

    xc = _dwconv(tag + "lruconv", lx, p['lru_conv_w'], p['lru_conv_b'][None], False)
    a_f, b_f, a_b, b_b = _rowwise(
        tag + "lrugate", _f_lru_gates, [xc],
        [p['lru_w_r'].reshape(2 * LRU_BLOCKS * LRU_BS, LRU_BS), p['lru_b_r'], p['lru_w_i'].reshape(2 * LRU_BLOCKS * LRU_BS, LRU_BS),
         p['lru_b_i'], p['lru_lambda']], [D_MODEL] * 4)
    h_f, s_f = _lin_scan(tag + "lruscanf", a_f, b_f, lru_s[0], False)
    h_b, s_b = _lin_scan(tag + "lruscanb", a_b, b_b, lru_s[1], True)
    (pa,) = _rowwise_linear(tag + "bra", _f_lru_out, [h_f, h_b, ly], [], [(p['w_br_a'], None)], to_linear=(2,))

    cw, cb_ = p['ssd_conv_w'], p['ssd_conv_b'][None]
    xs_c = _dwconv(tag + "ssdconvx", xs, cw[:, :2048], cb_[:, :2048], True)
    bm_c = _dwconv(tag + "ssdconvb", bm, cw[:, 2048:3072], cb_[:, 2048:3072], True)
    cm_c = _dwconv(tag + "ssdconvc", cm, cw[:, 3072:], cb_[:, 3072:], True)
    ssd_new, ys = [], []
    for d in range(2):
        y_d, st_d = _ssd_scan(tag + "ssd%d" % d, xs_c, bm_c, cm_c, small, _lane_row(p['ssd_dt_bias'][d], _DT_LANE + 32 * d),
                              _lane_row(p['ssd_a_log'][d], _DT_LANE + 32 * d), ssd_s[d], d, d == 1)
        ys.append(y_d)
        ssd_new.append(st_d)
    (pb,) = _rowwise_linear(tag + "brb", _f_ssd_post, [ys[0], ys[1], xs_c, sz],
                            [jnp.repeat(p['ssd_d'], SSD_HEADDIM)[None], p['ssd_norm_w'][None]], [(p['w_br_b'], None)], to_linear=(3,))

    mw, mb = p['ml_conv_w'], p['ml_conv_b'][None]
    q_c = _dwconv(tag + "mlconvq", mq, mw[:, :1024], mb[:, :1024], True)
    k_c = _dwconv(tag + "mlconvk", mk, mw[:, 1024:], mb[:, 1024:], True)
    gate_row = _lane_row(p['ml_gate_b'].reshape(4 * ML_HEADS), _MG_LANE)
    ml_new, hs = [], []
    for d in range(2):
        o_d, c_d, n_d, m_d = _ml_scan(tag + "ml%d" % d, q_c, k_c, mv, small, gate_row, *ml_s[d], d, d == 1)
        hs.append(o_d)
        ml_new.append((c_d, n_d, m_d))
    (pc,) = _rowwise_linear(tag + "brc", _f_ml_post, [hs[0], hs[1], mo], [p['ml_norm_w'][None]], [(p['w_br_c'], None)], to_linear=(2,))
    return (ga, gb, gc, pa, pb, pc), ((s_f, s_b), tuple(ssd_new), tuple(ml_new))


def _merge(tag, br, p):
    return _rowwise_linear(tag + "out", _f_merge, list(br), [], [(p['w_out'], None)], to_linear=tuple(range(6)))[0]


def _sublayers(tag, xin, o, mods, p, l):
    sh2, sc2, g1, g2 = mods
    (x1,) = _rowwise(tag + "ln1", _f_resid_ln, [xin, o], [g1, p['b_out'][None], p['ln1_g'][None], p['ln1_b'][None]], [D_MODEL], to_linear=(1,))
    (pre,) = _rowwise_linear(tag + "ff1", _f_modulate, [x1], [sh2, sc2], [(p['w_ff1'], None)])
    (o2,) = _rowwise_linear(tag + "ff2", _f_relu2, [pre], [p['b_ff1'][None]], [(p['w_ff2'], None)], to_linear=(0,))
    (x2,) = _rowwise(tag + "ln2", _f_resid_ln, [x1, o2], [g2, p['b_ff2'][None], p['ln2_g'][None], p['ln2_b'][None]], [D_MODEL], to_linear=(1,))
    return x2


def _to_col_major(h):
    s, d = h.shape
    return h.reshape(s // GRID_W, GRID_W, d).swapaxes(0, 1).reshape(s, d)


def _from_col_major(h):
    s, d = h.shape
    return h.reshape(GRID_W, s // GRID_W, d).swapaxes(0, 1).reshape(s, d)


def _forward(x, wts, cvec, ctx):
    zeros = lambda *s: jnp.zeros(s, F32)
    ctx_init = ((zeros(1, D_MODEL), zeros(1, D_MODEL)),
                (zeros(SSD_INNER, SSD_STATE), zeros(SSD_INNER, SSD_STATE)),
                tuple((zeros(ML_HEADS * ML_HD, ML_HD), zeros(ML_HEADS, 1, ML_HD), zeros(ML_HEADS, 1, LANES)) for _ in range(2)))
    for l in range(DEPTH):
        p = {n: (wts[n] if n == 'c_ctx' else wts[n][l]) for n in wts}
        tag = "l%d" % l
        cc = jnp.concatenate([cvec, p['c_ctx'][None], jnp.zeros((SUBLANES - 2, D_MODEL), F32)], axis=0)
        mod = _linear(tag + "ada", jax.nn.silu(cc), p['w_ada']) + p['b_ada'][None]
        sh1x, sc1x, g1x, sh2x, sc2x, g2x = [mod[0:1, i * D_MODEL:(i + 1) * D_MODEL] for i in range(6)]
        sh1c, sc1c, g1c, sh2c, sc2c, g2c = [mod[1:2, i * D_MODEL:(i + 1) * D_MODEL] for i in range(6)]
        br_c, ctx_states = _mixer(tag + "c", ctx, sh1c, sc1c, p, ctx_init)
        br_x, _ = _mixer(tag + "x", _to_col_major(x) if l % 2 == 1 else x, sh1x, sc1x, p, ctx_states)
        ox = _merge(tag + "x", br_x, p)
        if l % 2 == 1:
            ox = _from_col_major(ox)
        x = _sublayers(tag + "x", x, ox, (sh2x, sc2x, g1x, g2x), p, l)
        if l < DEPTH - 1:
            ctx = _sublayers(tag + "c", ctx, _merge(tag + "c", br_c, p), (sh2c, sc2c, g1c, g2c), p, l)
    return x


def _loss_and_cotangent(y, target):
    n_rows, d = y.shape
    tt = _row_tile(n_rows, 0, cap=256)

    def kern(y_ref, t_ref, dy_ref, acc_ref):
        @pl.when(pl.program_id(0) == 0)
        def _():
            acc_ref[...] = jnp.zeros_like(acc_ref)

        err = y_ref[...] - t_ref[...]
        dy_ref[...] = err * (1.0 / d)
        acc_ref[...] += jnp.sum(jnp.square(err))

    spec = pl.BlockSpec((tt, d), lambda i: (i, 0))
    dy, acc = pl.pallas_call(
        kern, grid=(n_rows // tt,), name="loss", in_specs=[spec, spec],
        out_specs=[spec, pl.BlockSpec((SUBLANES, LANES), lambda i: (0, 0))],
        out_shape=[jax.ShapeDtypeStruct((n_rows, d), F32), jax.ShapeDtypeStruct((SUBLANES, LANES), F32)],
    )(y, target)
    return acc[0, 0] * (0.5 / d), dy


def _exchange(name, src, gather):
    slab = src.shape if gather else src.shape[1:]

    def body(src_ref, out_ref, send_sems, recv_sems, local_sem):
        x, y, c = lax.axis_index("x"), lax.axis_index("y"), lax.axis_index("c")
        me = 4 * x + 2 * y + c
        local = pltpu.make_async_copy(src_ref if gather else src_ref.at[me], out_ref.at[me], local_sem)
        local.start()
        copies = []
        for d in range(1, N_DEV):
            px, py, pc = lax.rem(x + (d >> 2), 2), lax.rem(y + ((d >> 1) & 1), 2), lax.rem(c + (d & 1), 2)
            peer = 4 * px + 2 * py + pc
            cp = pltpu.make_async_remote_copy(
                src_ref=src_ref if gather else src_ref.at[peer], dst_ref=out_ref.at[me],
                send_sem=send_sems.at[d - 1], recv_sem=recv_sems.at[d - 1],
                device_id=(px, py, pc), device_id_type=pl.DeviceIdType.MESH)
            cp.start()
            copies.append(cp)
        for cp in copies:
            cp.wait()
        local.wait()

    return pl.pallas_call(
        body, name=name, out_shape=jax.ShapeDtypeStruct((N_DEV,) + tuple(slab), src.dtype),
        in_specs=[pl.BlockSpec(memory_space=pl.ANY)], out_specs=pl.BlockSpec(memory_space=pl.ANY),
        scratch_shapes=[pltpu.SemaphoreType.DMA((N_DEV - 1,)), pltpu.SemaphoreType.DMA((N_DEV - 1,)), pltpu.SemaphoreType.DMA],
    )(src)


_HBM = pl.BlockSpec(memory_space=pl.ANY)
_CHIPS = ((0, 0), (0, 1), (1, 0), (1, 1))


def _gather_two_level(name, src):
    def body(src_ref, out_ref, send_sems, recv_sems, local_sem):
        x, y, c = lax.axis_index("x"), lax.axis_index("y"), lax.axis_index("c")
        me, sibling = (x, y, c), (x, y, 1 - c)
        chips = [(1 - x, y), (x, 1 - y), (1 - x, 1 - y)]

        def slab(px, py, pc):
            return out_ref.at[4 * px + 2 * py + pc]

        def copy(k, block, to, src=None):
            return pltpu.make_async_remote_copy(
                src_ref=slab(*block) if src is None else src, dst_ref=slab(*block), send_sem=send_sems.at[k],
                recv_sem=recv_sems.at[k], device_id=to, device_id_type=pl.DeviceIdType.MESH)

        mine = pltpu.make_async_copy(src_ref, slab(*me), local_sem)
        mine.start()
        first = [copy(0, me, sibling, src=src_ref)] + [copy(1 + j, me, (*chip, c), src=src_ref) for j, chip in enumerate(chips)]
        for cp in first:
            cp.start()
        passed = [copy(4 + j, (*chip, c), sibling) for j, chip in enumerate(chips)]
        for j, chip in enumerate(chips):
            copy(1 + j, (*chip, c), me).wait_recv()
            passed[j].start()
        copy(0, sibling, me).wait_recv()
        for j, chip in enumerate(chips):
            copy(4 + j, (*chip, 1 - c), me).wait_recv()
        for cp in first + passed:
            cp.wait_send()
        mine.wait()

    return pl.pallas_call(
        body, name=name, out_shape=jax.ShapeDtypeStruct((N_DEV,) + tuple(src.shape), src.dtype),
        in_specs=[_HBM], out_specs=_HBM,
        scratch_shapes=[pltpu.SemaphoreType.DMA((N_DEV - 1,)), pltpu.SemaphoreType.DMA((N_DEV - 1,)), pltpu.SemaphoreType.DMA],
    )(src)


def _scatter_to_sibling(name, parts):
    def body(p_ref, out_ref, send_sems, recv_sems):
        x, y, c = lax.axis_index("x"), lax.axis_index("y"), lax.axis_index("c")
        copies = []
        for j, (px, py) in enumerate(_CHIPS):
            cp = pltpu.make_async_remote_copy(
                src_ref=p_ref.at[4 * px + 2 * py + (1 - c)], dst_ref=out_ref.at[j], send_sem=send_sems.at[j],
                recv_sem=recv_sems.at[j], device_id=(x, y, 1 - c), device_id_type=pl.DeviceIdType.MESH)
            cp.start()
            copies.append(cp)
        for cp in copies:
            cp.wait()

    return pl.pallas_call(
        body, name=name, out_shape=jax.ShapeDtypeStruct((4,) + tuple(parts.shape[1:]), parts.dtype),
        in_specs=[_HBM], out_specs=_HBM,
        scratch_shapes=[pltpu.SemaphoreType.DMA((4,)), pltpu.SemaphoreType.DMA((4,))],
    )(parts)


def _chip_sum(name, parts, from_sibling):
    _, rows, cols = parts.shape
    lanes = -(-cols // LANES) * LANES
    tr = _row_tile(rows, 4 * lanes * 4 * 2, budget=12 << 20)

    def kern(p_ref, s_ref, o_ref):
        c = lax.axis_index("c")
        o_ref[0] = (jnp.where(c == 0, p_ref[0, 0], p_ref[0, 1]) + s_ref[0]).astype(o_ref.dtype)

    return pl.pallas_call(
        kern, grid=(4, rows // tr), name=name,
        in_specs=[pl.BlockSpec((1, 2, tr, cols), lambda j, i: (j, 0, i, 0)), pl.BlockSpec((1, tr, cols), lambda j, i: (j, i, 0))],
        out_specs=pl.BlockSpec((1, tr, cols), lambda j, i: (j, i, 0)),
        out_shape=jax.ShapeDtypeStruct((4, rows, cols), BF16),
        compiler_params=_params(4 * lanes * tr * 4 * 2),
    )(parts.reshape(4, 2, rows, cols), from_sibling)


def _scatter_across_chips(name, sums):
    def body(q_ref, out_ref, send_sems, recv_sems, local_sem):
        x, y, c = lax.axis_index("x"), lax.axis_index("y"), lax.axis_index("c")
        own = 2 * x + y
        local = pltpu.make_async_copy(q_ref.at[own], out_ref.at[own], local_sem)
        local.start()
        copies = []
        for d in range(1, 4):
            px, py = lax.rem(x + (d >> 1), 2), lax.rem(y + (d & 1), 2)
            cp = pltpu.make_async_remote_copy(
                src_ref=q_ref.at[2 * px + py], dst_ref=out_ref.at[own], send_sem=send_sems.at[d - 1],
                recv_sem=recv_sems.at[d - 1], device_id=(px, py, c), device_id_type=pl.DeviceIdType.MESH)
            cp.start()
            copies.append(cp)
        for cp in copies:
            cp.wait()
        local.wait()

    return pl.pallas_call(
        body, name=name, out_shape=jax.ShapeDtypeStruct(sums.shape, sums.dtype), in_specs=[_HBM], out_specs=_HBM,
        scratch_shapes=[pltpu.SemaphoreType.DMA((3,)), pltpu.SemaphoreType.DMA((3,)), pltpu.SemaphoreType.DMA],
    )(sums)


def _sum_parts(name, parts):
    n_parts, rows, cols = parts.shape
    tr = _row_tile(rows, 4 * cols * (n_parts + 1) * 2)

    def kern(p_ref, o_ref):
        acc = p_ref[0]
        for k in range(1, n_parts):
            acc = acc + p_ref[k]
        o_ref[...] = acc

    return pl.pallas_call(
        kern, grid=(rows // tr,), name=name, in_specs=[pl.BlockSpec((n_parts, tr, cols), lambda i: (0, i, 0))],
        out_specs=pl.BlockSpec((tr, cols), lambda i: (i, 0)), out_shape=jax.ShapeDtypeStruct((rows, cols), F32),
    )(parts)


def _adamw(name, w, m, v, parts):
    n_parts, rows, cols = parts.shape
    lanes = -(-cols // LANES) * LANES
    tr = _row_tile(rows, 4 * lanes * (n_parts + 7) * 2, budget=16 << 20)
    c1 = np.float32(1.0 - ADAM_B1 ** ADAM_STEP)
    c2 = np.float32(1.0 - ADAM_B2 ** ADAM_STEP)

    def kern(w_ref, m_ref, v_ref, p_ref, g_ref, d_ref, nm_ref, nv_ref):
        g = p_ref[0].astype(F32)
        for k in range(1, n_parts):
            g = g + p_ref[k].astype(F32)
        m_new = ADAM_B1 * m_ref[...] + (1.0 - ADAM_B1) * g
        v_new = ADAM_B2 * v_ref[...] + (1.0 - ADAM_B2) * jnp.square(g)
        g_ref[...] = g
        nm_ref[...] = m_new
        nv_ref[...] = v_new
        d_ref[...] = -ADAM_LR * ((m_new / c1) / (jnp.sqrt(v_new / c2) + ADAM_EPS) + ADAM_WD * w_ref[...])

    spec = pl.BlockSpec((tr, cols), lambda i: (i, 0))
    return pl.pallas_call(
        kern, grid=(rows // tr,), name=name,
        in_specs=[spec, spec, spec, pl.BlockSpec((n_parts, tr, cols), lambda i: (0, i, 0))], out_specs=[spec] * 4,
        out_shape=[jax.ShapeDtypeStruct((rows, cols), F32)] * 4,
        compiler_params=_params(4 * lanes * tr * (n_parts + 7) * 2),
    )(w, m, v, parts)


def _packed_rows(shape):
    return -(-int(np.prod(shape)) // (SUBLANES * LANES)) * SUBLANES


def _pack(arrays, row_multiple):
    parts = []
    for a in arrays:
        n = int(np.prod(a.shape))
        r = _packed_rows(a.shape)
        parts.append(jnp.pad(a.reshape(-1), (0, r * LANES - n)).reshape(r, LANES))
    rows = sum(p.shape[0] for p in parts)
    total = -(-rows // row_multiple) * row_multiple
    if total > rows:
        parts.append(jnp.zeros((total - rows, LANES), arrays[0].dtype))
    return jnp.concatenate(parts, axis=0)


def _unpack(packed, shapes):
    out, off = [], 0
    for s in shapes:
        r = _packed_rows(s)
        out.append(packed[off:off + r].reshape(-1)[:int(np.prod(s))].reshape(s))
        off += r
    return out


def _split_w_in(w_in):
    main = jnp.concatenate([w_in[:, :, s:e] for s, e in _IN_MAIN], axis=2)
    pad = jnp.zeros(w_in.shape[:2] + (LANES - 80,), w_in.dtype)
    small = jnp.concatenate([w_in[:, :, s:e] for s, e in _IN_SMALL] + [pad], axis=2)
    return main, small


def _join_w_in(main, small):
    return jnp.concatenate([main[:, :, 0:8192], small[:, :, 0:64], main[:, :, 8192:12288], small[:, :, 64:80],
                            main[:, :, 12288:15360]], axis=2)


def _unshard(gathered, axis):
    nd, nl, r, c = gathered.shape
    if axis == 1:
        return gathered.transpose(1, 0, 2, 3).reshape(nl, nd * r, c)
    return gathered.transpose(1, 2, 0, 3).reshape(nl, r, nd * c)


def _reshard(full, axis):
    nl, r, c = full.shape
    if axis == 1:
        return full.reshape(nl, N_DEV, r // N_DEV, c).transpose(1, 0, 2, 3)
    return full.reshape(nl, r, N_DEV, c // N_DEV).transpose(2, 0, 1, 3)


def kernel(x, c, ctx, c_ctx, w_ada, b_ada, w_in, lru_conv_w, lru_conv_b, lru_w_r, lru_b_r, lru_w_i, lru_b_i, lru_lambda, ssd_conv_w, ssd_conv_b, ssd_dt_bias, ssd_a_log, ssd_d, ssd_norm_w, ml_conv_w, ml_conv_b, ml_gate_b, ml_norm_w, w_br_a, w_br_b, w_br_c, w_out, b_out, ln1_g, ln1_b, w_ff1, b_ff1, w_ff2, b_ff2, ln2_g, ln2_b, loss_target, m_c_ctx, m_w_ada, m_b_ada, m_w_in, m_lru_conv_w, m_lru_conv_b, m_lru_w_r, m_lru_b_r, m_lru_w_i, m_lru_b_i, m_lru_lambda, m_ssd_conv_w, m_ssd_conv_b, m_ssd_dt_bias, m_ssd_a_log, m_ssd_d, m_ssd_norm_w, m_ml_conv_w, m_ml_conv_b, m_ml_gate_b, m_ml_norm_w, m_w_br_a, m_w_br_b, m_w_br_c, m_w_out, m_b_out, m_ln1_g, m_ln1_b, m_w_ff1, m_b_ff1, m_w_ff2, m_b_ff2, m_ln2_g, m_ln2_b, v_c_ctx, v_w_ada, v_b_ada, v_w_in, v_lru_conv_w, v_lru_conv_b, v_lru_w_r, v_lru_b_r, v_lru_w_i, v_lru_b_i, v_lru_lambda, v_ssd_conv_w, v_ssd_conv_b, v_ssd_dt_bias, v_ssd_a_log, v_ssd_d, v_ssd_norm_w, v_ml_conv_w, v_ml_conv_b, v_ml_gate_b, v_ml_norm_w, v_w_br_a, v_w_br_b, v_w_br_c, v_w_out, v_b_out, v_ln1_g, v_ln1_b, v_w_ff1, v_b_ff1, v_w_ff2, v_b_ff2, v_ln2_g, v_ln2_b):
    a = dict(locals())
    me = 4 * lax.axis_index("x") + 2 * lax.axis_index("y") + lax.axis_index("c")

    wts = {n: a[n] for n in _REPLICATED}
    for n, axis in _BIG.items():
        full = _unshard(_gather_two_level("gather_" + n, a[n].astype(BF16)), axis)
        if n == 'w_in':
            main, small = _split_w_in(full)
            wts['w_in_main'], wts['w_in_small'] = main.astype(F32), small.astype(F32)
        else:
            wts[n] = full.astype(F32)
    small_shapes = [a[n].shape for n in _SMALL_SHARDED]
    small_all = _exchange("gather_small", _pack([a[n] for n in _SMALL_SHARDED], SUBLANES), True)
    per_dev = [_unpack(small_all[k], small_shapes) for k in range(N_DEV)]
    for i, n in enumerate(_SMALL_SHARDED):
        wts[n] = jnp.concatenate([per_dev[k][i] for k in range(N_DEV)], axis=-1)

    y, vjp = jax.vjp(functools.partial(_forward, cvec=c, ctx=ctx[0]), x[0], wts)
    loss_local, dy = _loss_and_cotangent(y, loss_target[0])
    grad_x, grads = vjp(dy)
    grads['w_in'] = _join_w_in(grads.pop('w_in_main'), grads.pop('w_in_small'))
    loss = lax.psum(loss_local, ("x", "y", "c"))

    out = {}

    def put(n, res, shape):
        for kind, r in zip(("grad_", "delta_", "new_m_", "new_v_"), res):
            out[kind + n] = r.reshape(shape)

    for n, axis in _BIG.items():
        shp = a[n].shape
        rows, cols = shp[0] * shp[1], shp[2]
        parts = _reshard(grads[n], axis).reshape(N_DEV, rows, cols)
        chip = _chip_sum("chipsum_" + n, parts, _scatter_to_sibling("scatter_d2d_" + n, parts))
        parts = _scatter_across_chips("scatter_ici_" + n, chip)
        put(n, _adamw("adamw_" + n, a[n].reshape(rows, cols), a["m_" + n].reshape(rows, cols), a["v_" + n].reshape(rows, cols), parts), shp)

    rep_names = _REPLICATED + _SMALL_SHARDED
    chunk_rows = SUBLANES * N_DEV
    g_pack = _pack([grads[n] for n in rep_names], chunk_rows * N_DEV)
    rows = g_pack.shape[0]
    parts = _exchange("scatter_rep", g_pack.reshape(N_DEV, rows // N_DEV, LANES), False)
    mine = _sum_parts("sum_rep", parts)
    g_all = _exchange("gather_rep", mine, True).reshape(rows, LANES)
    g_full = _unpack(g_all, [grads[n].shape for n in rep_names])
    g_local = []
    for n, g in zip(rep_names, g_full):
        if n in _SMALL_SHARDED:
            width = a[n].shape[-1]
            g = lax.dynamic_slice_in_dim(g, me * width, width, axis=g.ndim - 1)
        g_local.append(g)
    shapes = [a[n].shape for n in rep_names]
    res = _adamw("adamw_rep", _pack([a[n] for n in rep_names], chunk_rows), _pack([a["m_" + n] for n in rep_names], chunk_rows),
                 _pack([a["v_" + n] for n in rep_names], chunk_rows), _pack(g_local, chunk_rows)[None])
    unpacked = [_unpack(r, shapes) for r in res]
    for i, n in enumerate(rep_names):
        put(n, [u[i] for u in unpacked], shapes[i])

    outs = [loss, grad_x[None]]
    for kind in ("grad_", "delta_", "new_m_", "new_v_"):
        outs += [out[kind + n] for n in _WEIGHTS]
    return tuple(outs)
```

```python
import functools

import numpy as np
import jax
import jax.numpy as jnp
from jax import lax
from jax.experimental import pallas as pl
from jax.experimental.pallas import tpu as pltpu

F32 = jnp.float32
BF16 = jnp.bfloat16

N_DEV = 8
D_MODEL = 1024
DEPTH = 2
GRID_W = 64
CHUNK = 128
LN_EPS = 1e-6
LRU_BLOCKS = 8
LRU_BS = 128
LRU_C = 8.0
SSD_INNER = 2048
SSD_GROUPS = 8
SSD_HPG = 4
SSD_HEADDIM = 64
SSD_STATE = 128
ML_HEADS = 4
ML_HD = 256
D_FF = 4096
DN_ALPHA = (2 * DEPTH) ** 0.25
ADAM_LR, ADAM_B1, ADAM_B2, ADAM_EPS, ADAM_WD, ADAM_STEP = 0.001, 0.9, 0.999, 1e-08, 0.01, 10

VMEM_CAP = 60 * 1024 * 1024
SUBLANES = 8
LANES = 128

_IN_MAIN = ((0, 8192), (8256, 12352), (12368, 15440))
_IN_MAIN_WIDTHS = (1024, 1024, 2048, 2048, 1024, 1024, 1024, 1024, 1024, 1024, 1024, 1024, 1024)
_IN_SMALL = ((8192, 8256), (12352, 12368))
_DT_LANE = 0
_MG_LANE = 64

_WEIGHTS = ['c_ctx', 'w_ada', 'b_ada', 'w_in', 'lru_conv_w', 'lru_conv_b', 'lru_w_r', 'lru_b_r', 'lru_w_i', 'lru_b_i',
            'lru_lambda', 'ssd_conv_w', 'ssd_conv_b', 'ssd_dt_bias', 'ssd_a_log', 'ssd_d', 'ssd_norm_w', 'ml_conv_w',
            'ml_conv_b', 'ml_gate_b', 'ml_norm_w', 'w_br_a', 'w_br_b', 'w_br_c', 'w_out', 'b_out', 'ln1_g', 'ln1_b',
            'w_ff1', 'b_ff1', 'w_ff2', 'b_ff2', 'ln2_g', 'ln2_b']
_BIG = {'w_ada': 2, 'w_in': 2, 'w_ff1': 2, 'w_br_a': 1, 'w_br_b': 1, 'w_br_c': 1, 'w_out': 1, 'w_ff2': 1}
_SMALL_SHARDED = ['lru_conv_w', 'lru_b_r', 'lru_b_i', 'lru_lambda', 'ssd_conv_w', 'ml_conv_w']
_REPLICATED = [n for n in _WEIGHTS if n not in _BIG and n not in _SMALL_SHARDED]


def _params(vmem_bytes):
    return pltpu.CompilerParams(vmem_limit_bytes=int(min(max(2 * vmem_bytes, 32 << 20), VMEM_CAP)))


def _row_tile(n_rows, bytes_per_row, budget=6 << 20, cap=512):
    t = cap
    while t > SUBLANES and (t * bytes_per_row > budget or n_rows % t):
        t //= 2
    assert n_rows % t == 0, (n_rows, t)
    return t


def _dg(a, b, ca, cb):
    return lax.dot_general(a.astype(BF16), b.astype(BF16), (((ca,), (cb,)), ((), ())), preferred_element_type=F32)


def _make_bdot(ca, cb):
    @jax.custom_vjp
    def f(a, b):
        return _dg(a, b, ca, cb)

    def fwd(a, b):
        return _dg(a, b, ca, cb), (a, b)

    def bwd(res, g):
        a, b = res
        da = _dg(g, b, 1, 1 - cb) if ca == 1 else _dg(b, g, 1 - cb, 1)
        db = _dg(a, g, 1 - ca, 0) if cb == 0 else _dg(g, a, 0, 1 - ca)
        return da, db

    f.defvjp(fwd, bwd)
    return f


_mm_nn = _make_bdot(1, 0)
_mm_nt = _make_bdot(1, 1)
_mm_tn = _make_bdot(0, 0)


@jax.custom_vjp
def _round_bf16(x):
    return x.astype(BF16).astype(F32)


_round_bf16.defvjp(lambda x: (_round_bf16(x), None), lambda _, g: (g,))


def _exact_dot(a, b):
    return jnp.dot(a, b, precision=lax.Precision.HIGHEST, preferred_element_type=F32)


def _layernorm_rows(x):
    mu = jnp.mean(x, -1, keepdims=True)
    var = jnp.mean(jnp.square(x - mu), -1, keepdims=True)
    return (x - mu) * lax.rsqrt(var + LN_EPS)


def _rowwise_calls(name, f, rows, params, out_widths, out_dtype=F32, to_linear=()):
    drow_dtypes = [BF16 if i in to_linear else F32 for i in range(len(rows))]
    nr, npar, no = len(rows), len(params), len(out_widths)
    n_rows = rows[0].shape[0]
    row_w = [r.shape[1] for r in rows]
    par_bytes = sum(int(np.prod(p.shape)) * 4 for p in params)
    tile = _row_tile(n_rows, 4 * (2 * sum(row_w) + 2 * sum(out_widths)))
    grid = (n_rows // tile,)

    def row_spec(w):
        return pl.BlockSpec((tile, w), lambda i: (i, 0))

    def par_spec(p):
        return pl.BlockSpec(p.shape, lambda i: (0, 0))

    vmem = 2 * tile * 4 * (2 * sum(row_w) + 3 * sum(out_widths)) + 4 * par_bytes

    def fwd_call(rows, params):
        def kern(*refs):
            outs = f(*[r[...] for r in refs[:nr + npar]])
            for r, o in zip(refs[nr + npar:], outs):
                r[...] = o.astype(out_dtype)

        return pl.pallas_call(
            kern, grid=grid, name=name + "_fwd",
            in_specs=[row_spec(w) for w in row_w] + [par_spec(p) for p in params],
            out_specs=[row_spec(w) for w in out_widths],
            out_shape=[jax.ShapeDtypeStruct((n_rows, w), out_dtype) for w in out_widths],
            compiler_params=_params(vmem),
        )(*rows, *params)

    def bwd_call(rows, params, gouts):
        def kern(*refs):
            ins = [r[...] for r in refs[:nr + npar]]
            gs = tuple(r[...] for r in refs[nr + npar:nr + npar + no])
            grads = jax.vjp(f, *ins)[1](gs)
            drefs = refs[nr + npar + no:]
            for k in range(nr):
                drefs[k][...] = grads[k].astype(drefs[k].dtype)

            @pl.when(pl.program_id(0) == 0)
            def _():
                for k in range(npar):
                    drefs[nr + k][...] = jnp.zeros_like(drefs[nr + k])

            for k in range(npar):
                drefs[nr + k][...] += grads[nr + k]

        res = pl.pallas_call(
            kern, grid=grid, name=name + "_bwd",
            in_specs=[row_spec(w) for w in row_w] + [par_spec(p) for p in params] + [row_spec(w) for w in out_widths],
            out_specs=[row_spec(w) for w in row_w] + [par_spec(p) for p in params],
            out_shape=[jax.ShapeDtypeStruct(r.shape, dt) for r, dt in zip(rows, drow_dtypes)]
            + [jax.ShapeDtypeStruct(p.shape, F32) for p in params],
            compiler_params=_params(vmem),
        )(*rows, *params, *gouts)
        return tuple(r.astype(F32) for r in res[:nr]), tuple(res[nr:])

    return fwd_call, bwd_call


def _rowwise(name, f, rows, params, out_widths, to_linear=()):
    rows, params = tuple(rows), tuple(params)
    fwd_call, bwd_call = _rowwise_calls(name, f, rows, params, out_widths, to_linear=to_linear)

    @jax.custom_vjp
    def op(rows, params):
        return tuple(fwd_call(rows, params))

    op.defvjp(lambda r, p: (tuple(fwd_call(r, p)), (r, p)), lambda res, g: bwd_call(res[0], res[1], g))
    return op(rows, params)


def _rowwise_linear(name, f, rows, params, weights, to_linear=()):
    rows, params = tuple(rows), tuple(params)
    ws = tuple(w for w, _ in weights)
    m, k = rows[0].shape[0], ws[0].shape[0]
    row_fwd, row_bwd = _rowwise_calls(name, f, rows, params, [k], BF16, to_linear)
    lin = [_linear_calls(name + "lin%d" % i, m, k, w.shape[1], wd, BF16) for i, (w, wd) in enumerate(weights)]
    counts = [len(c[3]) for c in lin]

    def fwd(rows, params, ws):
        (a,) = row_fwd(rows, params)
        outs = []
        for (fwd_call, _, _, _), w in zip(lin, ws):
            outs += list(fwd_call(a, w))
        return tuple(outs), a

    @jax.custom_vjp
    def op(rows, params, ws):
        return fwd(rows, params, ws)[0]

    def op_fwd(rows, params, ws):
        outs, a = fwd(rows, params, ws)
        return outs, (rows, params, ws, a)

    def op_bwd(res, g):
        rows, params, ws, a = res
        da, dws, off = None, [], 0
        for (_, dgrad_call, wgrad_call, _), w, cnt in zip(lin, ws, counts):
            gk = g[off:off + cnt]
            off += cnt
            d = dgrad_call(w, gk)
            da = d if da is None else da + d
            dws.append(wgrad_call(a, gk))
        drows, dparams = row_bwd(rows, params, (da,))
        return drows, dparams, tuple(dws)

    op.defvjp(op_fwd, op_bwd)
    return op(rows, params, ws)


def _group_ranges(widths, tn):
    starts, s = [], 0
    for w in widths:
        assert w % tn == 0, (w, tn)
        starts.append((s // tn, (s + w) // tn))
        s += w
    return starts, s // tn


def _group_tile(refs, ranges, row_tile, col_tile, i, j):
    out = []
    for ref, (s, e) in zip(refs, ranges):
        cols = pl.ds(pl.multiple_of((j - s) * col_tile, col_tile), col_tile)
        out.append(((j >= s) & (j < e), ref, cols))
    return [(p, lambda r=r, c=c: r.at[pl.ds(pl.multiple_of(i * row_tile, row_tile), row_tile), c]) for p, r, c in out]


def _linear_calls(name, m, k, n, widths, a_dtype):
    widths = (n,) if widths is None else tuple(widths)
    ng = len(widths)
    cast_a = a_dtype != BF16
    tn = 128 if n < 256 else (256 if k > 2048 or n % 512 else 512)
    tm = _row_tile(m, 0, cap=1024 if k <= 2048 else 512)
    ranges, nt = _group_ranges(widths, tn)
    mt = m // tm
    tn_w = 1024 if (k <= 1024 and all(wd % 1024 == 0 for wd in widths)) else (512 if all(wd % 512 == 0 for wd in widths) else tn)
    tm_w = _row_tile(m, 0, cap=512)
    ranges_w, nt_w = _group_ranges(widths, tn_w)
    mt_w = m // tm_w
    hbm = pl.BlockSpec(memory_space=pl.ANY)

    def fwd_call(a, w):
        n_steps = mt * nt

        def kern(a_ref, w_ref, *rest):
            outs, obuf, osem = rest[:ng], rest[ng], rest[ng + 1]
            a_bf = rest[ng + 2] if cast_a else a_ref
            i, j = pl.program_id(0), pl.program_id(1)
            step = i * nt + j
            slot = lax.rem(step, 2)

            def drain(sl):
                pltpu.make_async_copy(obuf.at[sl], outs[0].at[pl.ds(0, tm), pl.ds(0, tn)], osem.at[sl]).wait()

            if cast_a:
                @pl.when(j == 0)
                def _():
                    a_bf[...] = a_ref[...].astype(BF16)

            @pl.when(step >= 2)
            def _():
                drain(slot)

            obuf[slot] = jnp.dot(a_bf[...], w_ref[...], preferred_element_type=F32)
            for pred, window in _group_tile(outs, ranges, tm, tn, i, j):
                @pl.when(pred)
                def _(window=window):
                    pltpu.make_async_copy(obuf.at[slot], window(), osem.at[slot]).start()

            @pl.when(step == n_steps - 1)
            def _():
                drain(slot)
                if n_steps > 1:
                    drain(1 - slot)

        return pl.pallas_call(
            kern, grid=(mt, nt), name=name + "_fwd",
            in_specs=[pl.BlockSpec((tm, k), lambda i, j: (i, 0)), pl.BlockSpec((k, tn), lambda i, j: (0, j))],
            out_specs=[hbm] * ng,
            out_shape=[jax.ShapeDtypeStruct((m, wd), F32) for wd in widths],
            scratch_shapes=[pltpu.VMEM((2, tm, tn), F32), pltpu.SemaphoreType.DMA((2,))]
            + ([pltpu.VMEM((tm, k), BF16)] if cast_a else []),
            compiler_params=_params(10 * tm * k + 4 * k * tn + 8 * tm * tn),
        )(a, w.astype(BF16))

    def prefetched(gs, gbuf, gsem, rngs, row_tile, col_tile, step, n_steps, tile_of):
        slot = lax.rem(step, 2)

        def start(s_idx, sl):
            ii, jj = tile_of(s_idx)
            for pred, window in _group_tile(gs, rngs, row_tile, col_tile, ii, jj):
                @pl.when(pred)
                def _(window=window):
                    pltpu.make_async_copy(window(), gbuf.at[sl], gsem.at[sl]).start()

        @pl.when(step == 0)
        def _():
            start(step, slot)

        @pl.when(step + 1 < n_steps)
        def _():
            start(step + 1, 1 - slot)

        pltpu.make_async_copy(gs[0].at[pl.ds(0, row_tile), pl.ds(0, col_tile)], gbuf.at[slot], gsem.at[slot]).wait()
        return slot

    def dgrad_call(w, gouts):
        def kern(w_ref, *rest):
            gs, da, gbuf, gsem = rest[:ng], rest[ng], rest[ng + 1], rest[ng + 2]
            i, j = pl.program_id(0), pl.program_id(1)
            slot = prefetched(gs, gbuf, gsem, ranges, tm, tn, i * nt + j, mt * nt, lambda s: (s // nt, lax.rem(s, nt)))

            @pl.when(j == 0)
            def _():
                da[...] = jnp.zeros_like(da)

            da[...] += lax.dot_general(gbuf[slot].astype(BF16), w_ref[...], (((1,), (1,)), ((), ())), preferred_element_type=F32)

        return pl.pallas_call(
            kern, grid=(mt, nt), name=name + "_dgrad",
            in_specs=[pl.BlockSpec((k, tn), lambda i, j: (0, j))] + [hbm] * ng,
            out_specs=pl.BlockSpec((tm, k), lambda i, j: (i, 0)),
            out_shape=jax.ShapeDtypeStruct((m, k), F32),
            scratch_shapes=[pltpu.VMEM((2, tm, tn), BF16), pltpu.SemaphoreType.DMA((2,))],
            compiler_params=_params(12 * tm * k + 4 * k * tn + 10 * tm * tn),
        )(w.astype(BF16), *[g.astype(BF16) for g in gouts])

    def wgrad_call(a, gouts):
        def kern(a_ref, *rest):
            gs, dw, gbuf, gsem = rest[:ng], rest[ng], rest[ng + 1], rest[ng + 2]
            j, i = pl.program_id(0), pl.program_id(1)
            slot = prefetched(gs, gbuf, gsem, ranges_w, tm_w, tn_w, j * mt_w + i, mt_w * nt_w,
                              lambda s: (lax.rem(s, mt_w), s // mt_w))

            @pl.when(i == 0)
            def _():
                dw[...] = jnp.zeros_like(dw)

            dw[...] += lax.dot_general(a_ref[...].astype(BF16), gbuf[slot].astype(BF16), (((0,), (0,)), ((), ())),
                                       preferred_element_type=F32)

        return pl.pallas_call(
            kern, grid=(nt_w, mt_w), name=name + "_wgrad",
            in_specs=[pl.BlockSpec((tm_w, k), lambda j, i: (i, 0))] + [hbm] * ng,
            out_specs=pl.BlockSpec((k, tn_w), lambda j, i: (0, j)),
            out_shape=jax.ShapeDtypeStruct((k, n), F32),
            scratch_shapes=[pltpu.VMEM((2, tm_w, tn_w), BF16), pltpu.SemaphoreType.DMA((2,))],
            compiler_params=_params(12 * tm_w * k + 12 * k * tn_w + 10 * tm_w * tn_w),
        )(a, *[g.astype(BF16) for g in gouts])

    return fwd_call, dgrad_call, wgrad_call, widths


def _linear(name, a, w, widths=None):
    fwd_call, dgrad_call, wgrad_call, _ = _linear_calls(name, a.shape[0], a.shape[1], w.shape[1], widths, a.dtype)

    @jax.custom_vjp
    def op(a, w):
        return tuple(fwd_call(a, w))

    op.defvjp(lambda a, w: (tuple(fwd_call(a, w)), (a, w)),
              lambda res, g: (dgrad_call(res[1], g), wgrad_call(res[0], g)))
    out = op(a, w)
    return out[0] if widths is None else out


def _conv_taps(x_ext, w, n_ext):
    xm2 = pltpu.roll(x_ext, 2, 0)
    xm1 = pltpu.roll(x_ext, 1, 0)
    xp1 = pltpu.roll(x_ext, n_ext - 1, 0)
    return xm2, xm1, xp1


def _dwconv(name, x, w, b, act):
    n_rows, ch = x.shape
    tt = _row_tile(n_rows, 4 * 6 * ch, cap=256)
    nt = n_rows // tt
    n_ext = tt + 2 * SUBLANES
    per8 = tt // SUBLANES
    last8 = n_rows // SUBLANES - 1
    main = pl.BlockSpec((tt, ch), lambda i: (i, 0))
    prev = pl.BlockSpec((SUBLANES, ch), lambda i: (jnp.maximum(i * per8 - 1, 0), 0))
    nxt = pl.BlockSpec((SUBLANES, ch), lambda i: (jnp.minimum((i + 1) * per8, last8), 0))
    wspec = pl.BlockSpec((4, ch), lambda i: (0, 0))
    bspec = pl.BlockSpec((1, ch), lambda i: (0, 0))
    vmem = 4 * n_ext * ch * 14

    def ext(main_ref, prev_ref, next_ref):
        i = pl.program_id(0)
        p = jnp.where(i > 0, prev_ref[...], 0.0)
        q = jnp.where(i < nt - 1, next_ref[...], 0.0)
        return jnp.concatenate([p, main_ref[...], q], axis=0)

    def pre_of(x_ext, wv, bv):
        xm2, xm1, xp1 = _conv_taps(x_ext, wv, n_ext)
        pre = wv[0:1] * xm2 + wv[1:2] * xm1 + wv[2:3] * x_ext + wv[3:4] * xp1 + bv
        return pre, (xm2, xm1, xp1)

    def fwd_call(x, w, b):
        def kern(xm, xp, xn, w_ref, b_ref, o_ref):
            pre, _ = pre_of(ext(xm, xp, xn), w_ref[...], b_ref[...])
            pre = pre[SUBLANES:SUBLANES + tt]
            o_ref[...] = pre * jax.nn.sigmoid(pre) if act else pre

        return pl.pallas_call(
            kern, grid=(nt,), name=name + "_fwd", in_specs=[main, prev, nxt, wspec, bspec], out_specs=main,
            out_shape=jax.ShapeDtypeStruct((n_rows, ch), F32), compiler_params=_params(vmem),
        )(x, x, x, w, b)

    def bwd_call(x, w, b, dy):
        def kern(xm, xp, xn, gm, gp, gn, w_ref, b_ref, dx_ref, dw_ref, db_ref):
            wv = w_ref[...]
            x_ext = ext(xm, xp, xn)
            pre, (xm2, xm1, xp1) = pre_of(x_ext, wv, b_ref[...])
            dpre = ext(gm, gp, gn)
            if act:
                sg = jax.nn.sigmoid(pre)
                dpre = dpre * (sg + pre * sg * (1.0 - sg))
            dx = (wv[0:1] * pltpu.roll(dpre, n_ext - 2, 0) + wv[1:2] * pltpu.roll(dpre, n_ext - 1, 0)
                  + wv[2:3] * dpre + wv[3:4] * pltpu.roll(dpre, 1, 0))
            sl = slice(SUBLANES, SUBLANES + tt)
            dx_ref[...] = dx[sl].astype(dx_ref.dtype)
            dm = dpre[sl]

            @pl.when(pl.program_id(0) == 0)
            def _():
                dw_ref[...] = jnp.zeros_like(dw_ref)
                db_ref[...] = jnp.zeros_like(db_ref)

            dw_ref[...] += jnp.concatenate(
                [jnp.sum(dm * t[sl], axis=0, keepdims=True) for t in (xm2, xm1, x_ext, xp1)], axis=0)
            db_ref[...] += jnp.sum(dm, axis=0, keepdims=True)

        return pl.pallas_call(
            kern, grid=(nt,), name=name + "_bwd", in_specs=[main, prev, nxt, main, prev, nxt, wspec, bspec],
            out_specs=[main, wspec, bspec],
            out_shape=[jax.ShapeDtypeStruct((n_rows, ch), BF16), jax.ShapeDtypeStruct((4, ch), F32),
                       jax.ShapeDtypeStruct((1, ch), F32)],
            compiler_params=_params(vmem),
        )(x, x, x, dy, dy, dy, w, b)

    @jax.custom_vjp
    def op(x, w, b):
        return fwd_call(x, w, b)

    def op_bwd(res, g):
        dx, dw, db = bwd_call(*res, g)
        return dx.astype(F32), dw, db

    op.defvjp(lambda x, w, b: (fwd_call(x, w, b), (x, w, b)), op_bwd)
    return op(x, w, b)


def _scan_groups(tt, ch, reverse, load, store, carry_ref):
    row = lax.broadcasted_iota(jnp.int32, (SUBLANES, ch), 0)
    ng = tt // SUBLANES

    def body(k, carry):
        g = (ng - 1 - k) if reverse else k
        sl = pl.ds(pl.multiple_of(g * SUBLANES, SUBLANES), SUBLANES)
        a, b, extra = load(sl)
        for s in (1, 2, 4):
            if reverse:
                a_sh, b_sh, valid = pltpu.roll(a, SUBLANES - s, 0), pltpu.roll(b, SUBLANES - s, 0), row < SUBLANES - s
            else:
                a_sh, b_sh, valid = pltpu.roll(a, s, 0), pltpu.roll(b, s, 0), row >= s
            b = jnp.where(valid, b + a * b_sh, b)
            a = jnp.where(valid, a * a_sh, a)
        h = b + a * carry
        if reverse:
            h_prev = jnp.where(row == SUBLANES - 1, carry, pltpu.roll(h, SUBLANES - 1, 0))
            last = h[0:1]
        else:
            h_prev = jnp.where(row == 0, carry, pltpu.roll(h, 1, 0))
            last = h[SUBLANES - 1:SUBLANES]
        store(sl, h, h_prev, extra)
        return jnp.broadcast_to(last, (SUBLANES, ch))

    carry_ref[...] = lax.fori_loop(0, ng, body, carry_ref[...])


def _lin_scan(name, a, b, h0, reverse):
    n_rows, ch = a.shape
    tt = _row_tile(n_rows, 0, cap=256)
    nt = n_rows // tt
    vmem = 2 * 4 * tt * ch * 5

    def tile_spec(rev):
        return pl.BlockSpec((tt, ch), (lambda i: (nt - 1 - i, 0)) if rev else (lambda i: (i, 0)))

    vec = pl.BlockSpec((1, ch), lambda i: (0, 0))

    def fwd_call(a, b, h0):
        def kern(a_ref, b_ref, h0_ref, h_ref, hp_ref, last_ref, carry):
            @pl.when(pl.program_id(0) == 0)
            def _():
                carry[...] = jnp.broadcast_to(h0_ref[...], carry.shape)

            def load(sl):
                return a_ref[sl, :], b_ref[sl, :], None

            def store(sl, h, h_prev, _):
                h_ref[sl, :] = h
                hp_ref[sl, :] = h_prev

            _scan_groups(tt, ch, reverse, load, store, carry)
            last_ref[...] = carry[0:1]

        return pl.pallas_call(
            kern, grid=(nt,), name=name + "_fwd", in_specs=[tile_spec(reverse), tile_spec(reverse), vec],
            out_specs=[tile_spec(reverse), tile_spec(reverse), vec],
            out_shape=[jax.ShapeDtypeStruct((n_rows, ch), F32)] * 2 + [jax.ShapeDtypeStruct((1, ch), F32)],
            scratch_shapes=[pltpu.VMEM((SUBLANES, ch), F32)], compiler_params=_params(vmem),
        )(a, b, h0)

    def bwd_call(a, h_prev, dh, dlast):
        rev = not reverse

        def kern(a_ref, hp_ref, dh_ref, dl_ref, da_ref, db_ref, d0_ref, carry):
            @pl.when(pl.program_id(0) == 0)
            def _():
                carry[...] = jnp.broadcast_to(dl_ref[...], carry.shape)

            def load(sl):
                av, dv = a_ref[sl, :], dh_ref[sl, :]
                return av, av * dv, dv

            def store(sl, u, u_next, dv):
                g = dv + u_next
                db_ref[sl, :] = g
                da_ref[sl, :] = g * hp_ref[sl, :]

            _scan_groups(tt, ch, rev, load, store, carry)
            d0_ref[...] = carry[0:1]

        return pl.pallas_call(
            kern, grid=(nt,), name=name + "_bwd", in_specs=[tile_spec(rev)] * 3 + [vec],
            out_specs=[tile_spec(rev), tile_spec(rev), vec],
            out_shape=[jax.ShapeDtypeStruct((n_rows, ch), F32)] * 2 + [jax.ShapeDtypeStruct((1, ch), F32)],
            scratch_shapes=[pltpu.VMEM((SUBLANES, ch), F32)], compiler_params=_params(vmem),
        )(a, h_prev, dh, dlast)

    @jax.custom_vjp
    def op(a, b, h0):
        h, _, last = fwd_call(a, b, h0)
        return h, last

    def op_fwd(a, b, h0):
        h, h_prev, last = fwd_call(a, b, h0)
        return (h, last), (a, h_prev)

    def op_bwd(res, g):
        da, db, d0 = bwd_call(res[0], res[1], g[0], g[1])
        return da, db, d0

    op.defvjp(op_fwd, op_bwd)
    return op(a, b, h0)


def _tri(reverse):
    q = lax.broadcasted_iota(jnp.int32, (CHUNK, CHUNK), 0)
    s = lax.broadcasted_iota(jnp.int32, (CHUNK, CHUNK), 1)
    return (q <= s) if reverse else (q >= s)


def _pick_col(x, lane):
    idx = lax.broadcasted_iota(jnp.int32, x.shape, 1)
    return jnp.sum(jnp.where(idx == lane, x, 0.0), axis=1, keepdims=True)


def _pick_row(x, row):
    idx = lax.broadcasted_iota(jnp.int32, x.shape, 0)
    return jnp.sum(jnp.where(idx == row, x, 0.0), axis=0, keepdims=True)


def _ssd_shared(small, bias_row, alog_row, reverse):
    delta_all = jax.nn.softplus(small + bias_row)
    acs_all = _exact_dot(_tri(reverse).astype(F32), delta_all * (-jnp.exp(alog_row)))
    return delta_all, acs_all, acs_all.T


def _ssd_group(xs, bm, cm, state, delta_all, acs_all, acs_t, g, direction, reverse):
    mask = _tri(reverse)
    last = 0 if reverse else CHUNK - 1
    hd = SSD_HEADDIM
    rowi = lax.broadcasted_iota(jnp.int32, (CHUNK, 1), 0)
    a_cols, a_rows, deltas, tots = [], [], [], []
    for r in range(SSD_HPG):
        lane = _DT_LANE + 32 * direction + SSD_HPG * g + r
        a_col = _pick_col(acs_all, lane)
        a_cols.append(a_col)
        a_rows.append(_pick_row(acs_t, lane))
        deltas.append(_pick_col(delta_all, lane))
        tots.append(jnp.sum(jnp.where(rowi == last, a_col, 0.0), axis=0, keepdims=True))

    def wide(cols, rows):
        return jnp.concatenate([jnp.broadcast_to(c, (rows, hd)) for c in cols], axis=1)

    a_w = wide(a_cols, CHUNK)
    x_w = xs * wide(deltas, CHUNK)
    st = _mm_tn(x_w * jnp.exp(wide(tots, 1) - a_w), bm)
    y_off = _mm_nt(cm, state) * jnp.exp(a_w)
    grow = jnp.concatenate([jnp.broadcast_to(jnp.exp(t), (hd, 1)) for t in tots], axis=0)
    cb = _mm_nt(cm, bm)
    m_cat = jnp.concatenate([cb * jnp.exp(jnp.where(mask, a_cols[r] - a_rows[r], -jnp.inf)) for r in range(SSD_HPG)], axis=1)
    lane_head = lax.broadcasted_iota(jnp.int32, (1, SSD_HPG * hd), 1) // hd
    x_bd = jnp.concatenate([jnp.where(lane_head == r, x_w, 0.0) for r in range(SSD_HPG)], axis=0)
    return _mm_nn(m_cat, x_bd) + y_off, grow * state + st


def _ssd_scan(name, xs, bm, cm, small, bias_row, alog_row, s0, direction, reverse):
    n_rows = xs.shape[0]
    nc = n_rows // CHUNK
    gw = SSD_HPG * SSD_HEADDIM
    vmem = 4 * CHUNK * (gw + 3 * 128) * 8 + 4 * gw * 128 * 12 + (8 << 20)

    n_state = SSD_GROUPS * gw
    shared_scratch = [pltpu.VMEM((CHUNK, LANES), F32), pltpu.VMEM((CHUNK, LANES), F32), pltpu.VMEM((LANES, CHUNK), F32)]

    def specs(order, gps=1):
        def cidx(c):
            return (nc - 1 - c) if order else c

        return dict(
            xs=pl.BlockSpec((CHUNK, gps * gw), lambda c, g: (cidx(c), g)),
            bc=pl.BlockSpec((CHUNK, gps * SSD_STATE), lambda c, g: (cidx(c), g)),
            small=pl.BlockSpec((CHUNK, LANES), lambda c, g: (cidx(c), 0)),
            row=pl.BlockSpec((1, LANES), lambda c, g: (0, 0)),
            state=pl.BlockSpec((n_state, SSD_STATE), lambda c, g: (0, 0)),
            enter=pl.BlockSpec((1, gps * gw, SSD_STATE), lambda c, g: (cidx(c), g, 0)),
        )

    gps_fwd, gps_bwd = 2, 1

    def group_rows(g, gps):
        return pl.ds(pl.multiple_of(g * gps * gw, gps * gw), gps * gw)

    def step_fn(g, gps):
        def fn(xs_v, bm_v, cm_v, st_v, d_all, a_all, a_t):
            ys, sts = [], []
            for u in range(gps):
                y_u, s_u = _ssd_group(xs_v[:, u * gw:(u + 1) * gw], bm_v[:, u * SSD_STATE:(u + 1) * SSD_STATE],
                                      cm_v[:, u * SSD_STATE:(u + 1) * SSD_STATE], st_v[u * gw:(u + 1) * gw],
                                      d_all, a_all, a_t, gps * g + u, direction, reverse)
                ys.append(y_u)
                sts.append(s_u)
            return jnp.concatenate(ys, axis=1), jnp.concatenate(sts, axis=0)

        return fn

    def fwd_call(xs, bm, cm, small, bias_row, alog_row, s0):
        gps = gps_fwd
        sp = specs(reverse, gps)

        def kern(xs_r, bm_r, cm_r, sm_r, br_r, ar_r, s0_r, y_r, sf_r, se_r, st, sh_d, sh_a, sh_t):
            c, g = pl.program_id(0), pl.program_id(1)

            @pl.when((c == 0) & (g == 0))
            def _():
                st[...] = s0_r[...]

            @pl.when(g == 0)
            def _():
                sh_d[...], sh_a[...], sh_t[...] = _ssd_shared(sm_r[...], br_r[...], ar_r[...], reverse)

            rows = group_rows(g, gps)
            s_in = st[rows, :]
            se_r[0] = s_in
            y_r[...], s_new = step_fn(g, gps)(xs_r[...], bm_r[...], cm_r[...], s_in, sh_d[...], sh_a[...], sh_t[...])
            st[rows, :] = s_new
            sf_r[rows, :] = s_new

        return pl.pallas_call(
            kern, grid=(nc, SSD_GROUPS // gps), name=name + "_fwd",
            in_specs=[sp['xs'], sp['bc'], sp['bc'], sp['small'], sp['row'], sp['row'], sp['state']],
            out_specs=[sp['xs'], sp['state'], sp['enter']],
            out_shape=[jax.ShapeDtypeStruct((n_rows, SSD_INNER), F32), jax.ShapeDtypeStruct((n_state, SSD_STATE), F32),
                       jax.ShapeDtypeStruct((nc, n_state, SSD_STATE), F32)],
            scratch_shapes=[pltpu.VMEM((n_state, SSD_STATE), F32)] + shared_scratch, compiler_params=_params(vmem),
        )(xs, bm, cm, small, bias_row, alog_row, s0)

    def bwd_call(xs, bm, cm, small, bias_row, alog_row, enter, dy, dsf):
        gps = gps_bwd
        sp = specs(not reverse, gps)

        def kern(xs_r, bm_r, cm_r, sm_r, br_r, ar_r, se_r, dy_r, dsf_r, dxs_r, dbm_r, dcm_r, dsm_r, dbr_r, dar_r, ds0_r,
                 ds, sh_d, sh_a, sh_t, gd, ga, gt):
            c, g = pl.program_id(0), pl.program_id(1)

            @pl.when((c == 0) & (g == 0))
            def _():
                ds[...] = dsf_r[...]
                dbr_r[...] = jnp.zeros_like(dbr_r)
                dar_r[...] = jnp.zeros_like(dar_r)

            @pl.when(g == 0)
            def _():
                sh_d[...], sh_a[...], sh_t[...] = _ssd_shared(sm_r[...], br_r[...], ar_r[...], reverse)
                gd[...] = jnp.zeros_like(gd)
                ga[...] = jnp.zeros_like(ga)
                gt[...] = jnp.zeros_like(gt)

            rows = group_rows(g, gps)
            _, vjp = jax.vjp(step_fn(g, gps), xs_r[...], bm_r[...], cm_r[...], se_r[0], sh_d[...], sh_a[...], sh_t[...])
            dxs, dbm, dcm, ds_in, dd, da, dt = vjp((dy_r[...], ds[rows, :]))
            dxs_r[...] = dxs
            dbm_r[...] = dbm
            dcm_r[...] = dcm
            ds[rows, :] = ds_in
            ds0_r[rows, :] = ds_in
            gd[...] += dd
            ga[...] += da
            gt[...] += dt

            @pl.when(g == SSD_GROUPS // gps - 1)
            def _():
                shared = functools.partial(_ssd_shared, reverse=reverse)
                dsm, dbr, dar = jax.vjp(shared, sm_r[...], br_r[...], ar_r[...])[1]((gd[...], ga[...], gt[...]))
                dsm_r[...] = dsm
                dbr_r[...] += dbr
                dar_r[...] += dar

        return pl.pallas_call(
            kern, grid=(nc, SSD_GROUPS // gps), name=name + "_bwd",
            in_specs=[sp['xs'], sp['bc'], sp['bc'], sp['small'], sp['row'], sp['row'], sp['enter'], sp['xs'], sp['state']],
            out_specs=[sp['xs'], sp['bc'], sp['bc'], sp['small'], sp['row'], sp['row'], sp['state']],
            out_shape=[jax.ShapeDtypeStruct(xs.shape, F32), jax.ShapeDtypeStruct(bm.shape, F32),
                       jax.ShapeDtypeStruct(cm.shape, F32), jax.ShapeDtypeStruct((n_rows, LANES), F32),
                       jax.ShapeDtypeStruct((1, LANES), F32), jax.ShapeDtypeStruct((1, LANES), F32),
                       jax.ShapeDtypeStruct(s0.shape, F32)],
            scratch_shapes=[pltpu.VMEM((n_state, SSD_STATE), F32)] + shared_scratch + shared_scratch,
            compiler_params=_params(vmem),
        )(xs, bm, cm, small, bias_row, alog_row, enter, dy, dsf)

    @jax.custom_vjp
    def op(*args):
        y, sf, _ = fwd_call(*args)
        return y, sf

    def op_fwd(*args):
        y, sf, enter = fwd_call(*args)
        return (y, sf), (args[:6], enter)

    op.defvjp(op_fwd, lambda res, g: tuple(bwd_call(*res[0], res[1], g[0], g[1])))
    return op(xs, bm, cm, small, bias_row, alog_row, s0)


def _ml_shared(small, gate_row, reverse):
    gates = small + gate_row
    b_all = _exact_dot(_tri(reverse).astype(F32), jax.nn.log_sigmoid(gates))
    return gates, b_all, gates.T, b_all.T


def _ml_head(q, k, v, c_st, n_st, m_st, gates, b_all, gates_t, b_t, h, direction, reverse):
    mask = _tri(reverse)
    last = 0 if reverse else CHUNK - 1
    lane_i = _MG_LANE + 8 * direction + h
    lane_f = lane_i + ML_HEADS
    b_col = _pick_col(b_all, lane_f)
    b_row = _pick_row(b_t, lane_f)
    li_col = _pick_col(gates, lane_i)
    li_row = _pick_row(gates_t, lane_i)
    rowi = lax.broadcasted_iota(jnp.int32, (CHUNK, 1), 0)
    g_tot = jnp.sum(jnp.where(rowi == last, b_col, 0.0), axis=0, keepdims=True)
    m_in = m_st[:, 0:1]
    q = q * (ML_HD ** -0.5)
    w = g_tot - b_col + li_col
    m_loc = lax.stop_gradient(jnp.max(w, axis=0, keepdims=True))
    kw = k * jnp.exp(w - m_loc)
    c_loc = _mm_tn(kw, v)
    n_loc = jnp.sum(kw, axis=0, keepdims=True)
    m_new = lax.stop_gradient(jnp.maximum(g_tot + m_in, m_loc))
    s_old = jnp.exp(g_tot + m_in - m_new)
    s_loc = jnp.exp(m_loc - m_new)
    c_new = s_old * c_st + s_loc * c_loc
    n_new = s_old * n_st + s_loc * n_loc
    log_d = jnp.where(mask, b_col - b_row + li_row, -jnp.inf)
    inter = b_col + m_in
    m_t = lax.stop_gradient(jnp.maximum(inter, jnp.max(log_d, axis=1, keepdims=True)))
    dmat = jnp.exp(log_d - m_t)
    wi = jnp.exp(inter - m_t)
    s = _mm_nt(q, k) * dmat
    num = _mm_nn(s, v) + wi * _mm_nn(q, c_st)
    den = jnp.sum(s, axis=1, keepdims=True) + wi * jnp.sum(_round_bf16(q) * _round_bf16(n_st), axis=1, keepdims=True)
    out = num / jnp.maximum(jnp.abs(den), jnp.exp(-m_t))
    return out, c_new, n_new, jnp.broadcast_to(m_new, (1, LANES))


def _ml_scan(name, q, k, v, small, gate_row, c0, n0, m0, direction, reverse):
    n_rows = q.shape[0]
    nc = n_rows // CHUNK
    vmem = 4 * CHUNK * (4 * ML_HD + 128) * 8 + 4 * ML_HD * ML_HD * 12 + (8 << 20)

    def specs(order):
        def cidx(c):
            return (nc - 1 - c) if order else c

        return dict(
            qkv=pl.BlockSpec((CHUNK, ML_HD), lambda c, h: (cidx(c), h)),
            small=pl.BlockSpec((CHUNK, LANES), lambda c, h: (cidx(c), 0)),
            row=pl.BlockSpec((1, LANES), lambda c, h: (0, 0)),
            c=pl.BlockSpec((ML_HEADS * ML_HD, ML_HD), lambda c, h: (0, 0)),
            n=pl.BlockSpec((ML_HEADS, 1, ML_HD), lambda c, h: (0, 0, 0)),
            m=pl.BlockSpec((ML_HEADS, 1, LANES), lambda c, h: (0, 0, 0)),
            ec=pl.BlockSpec((1, ML_HD, ML_HD), lambda c, h: (cidx(c), h, 0)),
            en=pl.BlockSpec((1, 1, 1, ML_HD), lambda c, h: (cidx(c), h, 0, 0)),
            em=pl.BlockSpec((1, 1, 1, LANES), lambda c, h: (cidx(c), h, 0, 0)),
        )

    st_shapes = [jax.ShapeDtypeStruct((ML_HEADS * ML_HD, ML_HD), F32), jax.ShapeDtypeStruct((ML_HEADS, 1, ML_HD), F32),
                 jax.ShapeDtypeStruct((ML_HEADS, 1, LANES), F32)]
    scratch = [pltpu.VMEM((ML_HEADS * ML_HD, ML_HD), F32), pltpu.VMEM((ML_HEADS, 1, ML_HD), F32),
               pltpu.VMEM((ML_HEADS, 1, LANES), F32)]
    shared_scratch = [pltpu.VMEM((CHUNK, LANES), F32), pltpu.VMEM((CHUNK, LANES), F32),
                      pltpu.VMEM((LANES, CHUNK), F32), pltpu.VMEM((LANES, CHUNK), F32)]

    def head_rows(h):
        return pl.ds(pl.multiple_of(h * ML_HD, ML_HD), ML_HD)

    def fwd_call(q, k, v, small, gate_row, c0, n0, m0):
        sp = specs(reverse)

        def kern(q_r, k_r, v_r, sm_r, gr_r, c0_r, n0_r, m0_r, o_r, cf_r, nf_r, mf_r, ec_r, en_r, em_r, cs, ns, ms, *sh):
            c, h = pl.program_id(0), pl.program_id(1)

            @pl.when((c == 0) & (h == 0))
            def _():
                cs[...] = c0_r[...]
                ns[...] = n0_r[...]
                ms[...] = m0_r[...]

            @pl.when(h == 0)
            def _():
                for ref, val in zip(sh, _ml_shared(sm_r[...], gr_r[...], reverse)):
                    ref[...] = val

            rows = head_rows(h)
            c_in, n_in, m_in = cs[rows, :], ns[h], ms[h]
            ec_r[0] = c_in
            en_r[0, 0] = n_in
            em_r[0, 0] = m_in
            out, c_new, n_new, m_new = _ml_head(q_r[...], k_r[...], v_r[...], c_in, n_in, m_in, *[r[...] for r in sh],
                                                h, direction, reverse)
            o_r[...] = out
            cs[rows, :] = c_new
            ns[h] = n_new
            ms[h] = m_new
            cf_r[rows, :] = c_new
            nf_r[h] = n_new
            mf_r[h] = m_new

        return pl.pallas_call(
            kern, grid=(nc, ML_HEADS), name=name + "_fwd",
            in_specs=[sp['qkv']] * 3 + [sp['small'], sp['row'], sp['c'], sp['n'], sp['m']],
            out_specs=[sp['qkv'], sp['c'], sp['n'], sp['m'], sp['ec'], sp['en'], sp['em']],
            out_shape=[jax.ShapeDtypeStruct((n_rows, ML_HEADS * ML_HD), F32)] + st_shapes + [
                jax.ShapeDtypeStruct((nc, ML_HEADS * ML_HD, ML_HD), F32),
                jax.ShapeDtypeStruct((nc, ML_HEADS, 1, ML_HD), F32), jax.ShapeDtypeStruct((nc, ML_HEADS, 1, LANES), F32)],
            scratch_shapes=scratch + shared_scratch, compiler_params=_params(vmem),
        )(q, k, v, small, gate_row, c0, n0, m0)

    def bwd_call(q, k, v, small, gate_row, ec, en, em, do, dcf, dnf, dmf):
        sp = specs(not reverse)
        n_sh = len(shared_scratch)

        def kern(q_r, k_r, v_r, sm_r, gr_r, ec_r, en_r, em_r, do_r, dcf_r, dnf_r, dmf_r,
                 dq_r, dk_r, dv_r, dsm_r, dgr_r, dc0_r, dn0_r, dm0_r, dcs, dns, dms, *rest):
            sh, gsh = rest[:n_sh], rest[n_sh:]
            c, h = pl.program_id(0), pl.program_id(1)

            @pl.when((c == 0) & (h == 0))
            def _():
                dcs[...] = dcf_r[...]
                dns[...] = dnf_r[...]
                dms[...] = dmf_r[...]
                dgr_r[...] = jnp.zeros_like(dgr_r)

            @pl.when(h == 0)
            def _():
                for ref, val in zip(sh, _ml_shared(sm_r[...], gr_r[...], reverse)):
                    ref[...] = val
                for ref in gsh:
                    ref[...] = jnp.zeros_like(ref)

            rows = head_rows(h)
            fn = functools.partial(_ml_head, h=h, direction=direction, reverse=reverse)
            _, vjp = jax.vjp(fn, q_r[...], k_r[...], v_r[...], ec_r[0], en_r[0, 0], em_r[0, 0], *[r[...] for r in sh])
            grads = vjp((do_r[...], dcs[rows, :], dns[h], dms[h]))
            dq, dk, dv, dc, dn, dm = grads[:6]
            dq_r[...] = dq
            dk_r[...] = dk
            dv_r[...] = dv
            for ref, val in zip(gsh, grads[6:]):
                ref[...] += val
            dm = jnp.broadcast_to(jnp.sum(dm, axis=1, keepdims=True), (1, LANES)) * (1.0 / LANES)
            dcs[rows, :] = dc
            dns[h] = dn
            dms[h] = dm
            dc0_r[rows, :] = dc
            dn0_r[h] = dn
            dm0_r[h] = dm

            @pl.when(h == ML_HEADS - 1)
            def _():
                shared = functools.partial(_ml_shared, reverse=reverse)
                dsm, dgr = jax.vjp(shared, sm_r[...], gr_r[...])[1](tuple(r[...] for r in gsh))
                dsm_r[...] = dsm
                dgr_r[...] += dgr

        return pl.pallas_call(
            kern, grid=(nc, ML_HEADS), name=name + "_bwd",
            in_specs=[sp['qkv']] * 3 + [sp['small'], sp['row'], sp['ec'], sp['en'], sp['em'], sp['qkv'], sp['c'], sp['n'], sp['m']],
            out_specs=[sp['qkv']] * 3 + [sp['small'], sp['row'], sp['c'], sp['n'], sp['m']],
            out_shape=[jax.ShapeDtypeStruct(q.shape, F32)] * 3 + [jax.ShapeDtypeStruct((n_rows, LANES), F32),
                                                                  jax.ShapeDtypeStruct((1, LANES), F32)] + st_shapes,
            scratch_shapes=scratch + shared_scratch + shared_scratch, compiler_params=_params(vmem),
        )(q, k, v, small, gate_row, ec, en, em, do, dcf, dnf, dmf)

    @jax.custom_vjp
    def op(*args):
        return tuple(fwd_call(*args)[:4])

    def op_fwd(*args):
        res = fwd_call(*args)
        return tuple(res[:4]), (args[:5], tuple(res[4:]))

    op.defvjp(op_fwd, lambda res, g: tuple(bwd_call(*res[0], *res[1], *g)))
    return op(q, k, v, small, gate_row, c0, n0, m0)


def _f_modulate(x, shift, scale):
    return (_layernorm_rows(x) * (1.0 + scale) + shift,)


def _f_resid_ln(x, o, gate, bias, ln_g, ln_b):
    return (_layernorm_rows(DN_ALPHA * x + gate * (o + bias)) * ln_g + ln_b,)


def _f_lru_gates(xc, w_r, b_r, w_i, b_i, lam):
    outs = []
    for d in range(2):
        def blockdiag(w):
            return jnp.concatenate(
                [_mm_nn(xc[:, n * LRU_BS:(n + 1) * LRU_BS], w[(d * LRU_BLOCKS + n) * LRU_BS:(d * LRU_BLOCKS + n + 1) * LRU_BS, :])
                 for n in range(LRU_BLOCKS)], axis=1)

        r = jax.nn.sigmoid(blockdiag(w_r) + b_r[d:d + 1])
        i = jax.nn.sigmoid(blockdiag(w_i) + b_i[d:d + 1])
        log_a = -LRU_C * r * jax.nn.softplus(-lam[d:d + 1])
        outs += [jnp.exp(log_a), jnp.sqrt(1.0 - jnp.exp(2.0 * log_a)) * i * xc]
    return tuple(outs)


def _f_lru_out(h_f, h_b, ly):
    return ((h_f + h_b) * jax.nn.gelu(ly),)


def _f_ssd_post(y_f, y_b, xs, z, d_exp, norm_w):
    y = (y_f + y_b + xs * d_exp) * jax.nn.silu(z)
    gw = SSD_INNER // SSD_GROUPS
    parts = []
    for g in range(SSD_GROUPS):
        yg = y[:, g * gw:(g + 1) * gw]
        parts.append(yg * lax.rsqrt(jnp.mean(jnp.square(yg), -1, keepdims=True) + LN_EPS))
    return (jnp.concatenate(parts, axis=1) * norm_w,)


def _f_ml_post(h_f, h_b, o, norm_w):
    h = h_f + h_b
    parts = [_layernorm_rows(h[:, i * ML_HD:(i + 1) * ML_HD]) for i in range(ML_HEADS)]
    return (jnp.concatenate(parts, axis=1) * norm_w * jax.nn.sigmoid(o),)


def _f_merge(ga, gb, gc, pa, pb, pc):
    return (jax.nn.sigmoid(ga) * pa + jax.nn.sigmoid(gb) * pb + jax.nn.sigmoid(gc) * pc,)


def _f_relu2(pre, bias):
    return (jnp.square(jax.nn.relu(pre + bias)),)


def _lane_row(vec, start):
    return jnp.pad(vec[None], ((0, 0), (start, LANES - start - vec.shape[0])))


def _mixer(tag, x_tok, shift, scale, p, states):
    (lru_s, ssd_s, ml_s) = states
    lx, ly, sz, xs, bm, cm, mq, mk, mv, mo, ga, gb, gc, small = _rowwise_linear(
        tag + "in", _f_modulate, [x_tok], [shift, scale], [(p['w_in_main'], _IN_MAIN_WIDTHS), (p['w_in_small'], None)])

    xc = _dwconv(tag + "lruconv", lx, p['lru_conv_w'], p['lru_conv_b'][None], False)
    a_f, b_f, a_b, b_b = _rowwise(
        tag + "lrugate", _f_lru_gates, [xc],
        [p['lru_w_r'].reshape(2 * LRU_BLOCKS * LRU_BS, LRU_BS), p['lru_b_r'], p['lru_w_i'].reshape(2 * LRU_BLOCKS * LRU_BS, LRU_BS),
         p['lru_b_i'], p['lru_lambda']], [D_MODEL] * 4)
    h_f, s_f = _lin_scan(tag + "lruscanf", a_f, b_f, lru_s[0], False)
    h_b, s_b = _lin_scan(tag + "lruscanb", a_b, b_b, lru_s[1], True)
    (pa,) = _rowwise_linear(tag + "bra", _f_lru_out, [h_f, h_b, ly], [], [(p['w_br_a'], None)], to_linear=(2,))

    cw, cb_ = p['ssd_conv_w'], p['ssd_conv_b'][None]
    xs_c = _dwconv(tag + "ssdconvx", xs, cw[:, :2048], cb_[:, :2048], True)
    bm_c = _dwconv(tag + "ssdconvb", bm, cw[:, 2048:3072], cb_[:, 2048:3072], True)
    cm_c = _dwconv(tag + "ssdconvc", cm, cw[:, 3072:], cb_[:, 3072:], True)
    ssd_new, ys = [], []
    for d in range(2):
        y_d, st_d = _ssd_scan(tag + "ssd%d" % d, xs_c, bm_c, cm_c, small, _lane_row(p['ssd_dt_bias'][d], _DT_LANE + 32 * d),
                              _lane_row(p['ssd_a_log'][d], _DT_LANE + 32 * d), ssd_s[d], d, d == 1)
        ys.append(y_d)
        ssd_new.append(st_d)
    (pb,) = _rowwise_linear(tag + "brb", _f_ssd_post, [ys[0], ys[1], xs_c, sz],
                            [jnp.repeat(p['ssd_d'], SSD_HEADDIM)[None], p['ssd_norm_w'][None]], [(p['w_br_b'], None)], to_linear=(3,))

    mw, mb = p['ml_conv_w'], p['ml_conv_b'][None]
    q_c = _dwconv(tag + "mlconvq", mq, mw[:, :1024], mb[:, :1024], True)
    k_c = _dwconv(tag + "mlconvk", mk, mw[:, 1024:], mb[:, 1024:], True)
    gate_row = _lane_row(p['ml_gate_b'].reshape(4 * ML_HEADS), _MG_LANE)
    ml_new, hs = [], []
    for d in range(2):
        o_d, c_d, n_d, m_d = _ml_scan(tag + "ml%d" % d, q_c, k_c, mv, small, gate_row, *ml_s[d], d, d == 1)
        hs.append(o_d)
        ml_new.append((c_d, n_d, m_d))
    (pc,) = _rowwise_linear(tag + "brc", _f_ml_post, [hs[0], hs[1], mo], [p['ml_norm_w'][None]], [(p['w_br_c'], None)], to_linear=(2,))
    return (ga, gb, gc, pa, pb, pc), ((s_f, s_b), tuple(ssd_new), tuple(ml_new))


def _merge(tag, br, p):
    return _rowwise_linear(tag + "out", _f_merge, list(br), [], [(p['w_out'], None)], to_linear=tuple(range(6)))[0]


def _sublayers(tag, xin, o, mods, p, l):
    sh2, sc2, g1, g2 = mods
    (x1,) = _rowwise(tag + "ln1", _f_resid_ln, [xin, o], [g1, p['b_out'][None], p['ln1_g'][None], p['ln1_b'][None]], [D_MODEL], to_linear=(1,))
    (pre,) = _rowwise_linear(tag + "ff1", _f_modulate, [x1], [sh2, sc2], [(p['w_ff1'], None)])
    (o2,) = _rowwise_linear(tag + "ff2", _f_relu2, [pre], [p['b_ff1'][None]], [(p['w_ff2'], None)], to_linear=(0,))
    (x2,) = _rowwise(tag + "ln2", _f_resid_ln, [x1, o2], [g2, p['b_ff2'][None], p['ln2_g'][None], p['ln2_b'][None]], [D_MODEL], to_linear=(1,))
    return x2


def _to_col_major(h):
    s, d = h.shape
    return h.reshape(s // GRID_W, GRID_W, d).swapaxes(0, 1).reshape(s, d)


def _from_col_major(h):
    s, d = h.shape
    return h.reshape(GRID_W, s // GRID_W, d).swapaxes(0, 1).reshape(s, d)


def _forward(x, wts, mods, ctx):
    zeros = lambda *s: jnp.zeros(s, F32)
    ctx_init = ((zeros(1, D_MODEL), zeros(1, D_MODEL)),
                (zeros(SSD_INNER, SSD_STATE), zeros(SSD_INNER, SSD_STATE)),
                tuple((zeros(ML_HEADS * ML_HD, ML_HD), zeros(ML_HEADS, 1, ML_HD), zeros(ML_HEADS, 1, LANES)) for _ in range(2)))
    for l in range(DEPTH):
        p = {n: wts[n][l] for n in wts}
        tag = "l%d" % l
        sh1x, sc1x, g1x, sh2x, sc2x, g2x = [mods[l][0][:, i * D_MODEL:(i + 1) * D_MODEL] for i in range(6)]
        sh1c, sc1c, g1c, sh2c, sc2c, g2c = [mods[l][1][:, i * D_MODEL:(i + 1) * D_MODEL] for i in range(6)]
        br_c, ctx_states = _mixer(tag + "c", ctx, sh1c, sc1c, p, ctx_init)
        br_x, _ = _mixer(tag + "x", _to_col_major(x) if l % 2 == 1 else x, sh1x, sc1x, p, ctx_states)
        ox = _merge(tag + "x", br_x, p)
        if l % 2 == 1:
            ox = _from_col_major(ox)
        x = _sublayers(tag + "x", x, ox, (sh2x, sc2x, g1x, g2x), p, l)
        if l < DEPTH - 1:
            ctx = _sublayers(tag + "c", ctx, _merge(tag + "c", br_c, p), (sh2c, sc2c, g1c, g2c), p, l)
    return x


_ADA_ROWS = 2 * SUBLANES


def _ada_forward(c, c_ctx, w_ada, b_ada, me):
    c_all = _exchange("gather_c", jnp.broadcast_to(c, (SUBLANES, D_MODEL)), True)[:, 0]

    def rows_of(c_ctx_):
        pad = jnp.zeros((_ADA_ROWS - N_DEV - 1, D_MODEL), F32)
        return jax.nn.silu(jnp.concatenate([c_all, c_ctx_[None], pad], axis=0))

    rows, vjp_rows = jax.vjp(rows_of, c_ctx)
    cols, vjp_cols = jax.vjp(lambda r, w: jnp.stack([_linear("ada%d" % l, r, w[l]) for l in range(DEPTH)]), rows, w_ada)
    full = _exchange("gather_mod", cols, True).transpose(1, 2, 0, 3).reshape(DEPTH, _ADA_ROWS, 6 * D_MODEL) + b_ada[:, None, :]
    mods = [(lax.dynamic_slice_in_dim(full[l], me, 1, axis=0), full[l][N_DEV:N_DEV + 1]) for l in range(DEPTH)]
    return mods, (vjp_rows, vjp_cols)


def _ada_backward(saved, dmods):
    vjp_rows, vjp_cols = saved
    wcol = 6 * D_MODEL // N_DEV
    pad = jnp.zeros((SUBLANES - 2, 6 * D_MODEL), F32)
    both = jnp.stack([jnp.concatenate([dx, dc, pad], axis=0) for dx, dc in dmods])
    send = both.reshape(DEPTH, SUBLANES, N_DEV, wcol).transpose(2, 0, 1, 3)
    recv = _exchange("scatter_dmod", send, False)
    ctx_row = recv[0, :, 1]
    for k in range(1, N_DEV):
        ctx_row = ctx_row + recv[k, :, 1]
    g = jnp.concatenate([recv[:, :, 0].transpose(1, 0, 2), ctx_row[:, None],
                         jnp.zeros((DEPTH, _ADA_ROWS - N_DEV - 1, wcol), F32)], axis=1)
    d_rows, d_w = vjp_cols(g)
    (d_c_ctx,) = vjp_rows(d_rows)
    d_b = jnp.stack([(dx + dc)[0] for dx, dc in dmods])
    return d_w, d_b, d_c_ctx


def _loss_and_cotangent(y, target):
    n_rows, d = y.shape
    tt = _row_tile(n_rows, 0, cap=256)

    def kern(y_ref, t_ref, dy_ref, acc_ref):
        @pl.when(pl.program_id(0) == 0)
        def _():
            acc_ref[...] = jnp.zeros_like(acc_ref)

        err = y_ref[...] - t_ref[...]
        dy_ref[...] = err * (1.0 / d)
        acc_ref[...] += jnp.sum(jnp.square(err))

    spec = pl.BlockSpec((tt, d), lambda i: (i, 0))
    dy, acc = pl.pallas_call(
        kern, grid=(n_rows // tt,), name="loss", in_specs=[spec, spec],
        out_specs=[spec, pl.BlockSpec((SUBLANES, LANES), lambda i: (0, 0))],
        out_shape=[jax.ShapeDtypeStruct((n_rows, d), F32), jax.ShapeDtypeStruct((SUBLANES, LANES), F32)],
    )(y, target)
    return acc[0, 0] * (0.5 / d), dy


def _exchange(name, src, gather):
    slab = src.shape if gather else src.shape[1:]

    def body(src_ref, out_ref, send_sems, recv_sems, local_sem):
        x, y, c = lax.axis_index("x"), lax.axis_index("y"), lax.axis_index("c")
        me = 4 * x + 2 * y + c
        local = pltpu.make_async_copy(src_ref if gather else src_ref.at[me], out_ref.at[me], local_sem)
        local.start()
        copies = []
        for d in range(1, N_DEV):
            px, py, pc = lax.rem(x + (d >> 2), 2), lax.rem(y + ((d >> 1) & 1), 2), lax.rem(c + (d & 1), 2)
            peer = 4 * px + 2 * py + pc
            cp = pltpu.make_async_remote_copy(
                src_ref=src_ref if gather else src_ref.at[peer], dst_ref=out_ref.at[me],
                send_sem=send_sems.at[d - 1], recv_sem=recv_sems.at[d - 1],
                device_id=(px, py, pc), device_id_type=pl.DeviceIdType.MESH)
            cp.start()
            copies.append(cp)
        for cp in copies:
            cp.wait()
        local.wait()

    return pl.pallas_call(
        body, name=name, out_shape=jax.ShapeDtypeStruct((N_DEV,) + tuple(slab), src.dtype),
        in_specs=[pl.BlockSpec(memory_space=pl.ANY)], out_specs=pl.BlockSpec(memory_space=pl.ANY),
        scratch_shapes=[pltpu.SemaphoreType.DMA((N_DEV - 1,)), pltpu.SemaphoreType.DMA((N_DEV - 1,)), pltpu.SemaphoreType.DMA],
    )(src)


_HBM = pl.BlockSpec(memory_space=pl.ANY)
_CHIPS = ((0, 0), (0, 1), (1, 0), (1, 1))


def _gather_two_level(name, src):
    def body(src_ref, out_ref, send_sems, recv_sems, local_sem):
        x, y, c = lax.axis_index("x"), lax.axis_index("y"), lax.axis_index("c")
        me, sibling = (x, y, c), (x, y, 1 - c)
        chips = [(1 - x, y), (x, 1 - y), (1 - x, 1 - y)]

        def slab(px, py, pc):
            return out_ref.at[4 * px + 2 * py + pc]

        def copy(k, block, to, src=None):
            return pltpu.make_async_remote_copy(
                src_ref=slab(*block) if src is None else src, dst_ref=slab(*block), send_sem=send_sems.at[k],
                recv_sem=recv_sems.at[k], device_id=to, device_id_type=pl.DeviceIdType.MESH)

        mine = pltpu.make_async_copy(src_ref, slab(*me), local_sem)
        mine.start()
        first = [copy(0, me, sibling, src=src_ref)] + [copy(1 + j, me, (*chip, c), src=src_ref) for j, chip in enumerate(chips)]
        for cp in first:
            cp.start()
        passed = [copy(4 + j, (*chip, c), sibling) for j, chip in enumerate(chips)]
        for j, chip in enumerate(chips):
            copy(1 + j, (*chip, c), me).wait_recv()
            passed[j].start()
        copy(0, sibling, me).wait_recv()
        for j, chip in enumerate(chips):
            copy(4 + j, (*chip, 1 - c), me).wait_recv()
        for cp in first + passed:
            cp.wait_send()
        mine.wait()

    return pl.pallas_call(
        body, name=name, out_shape=jax.ShapeDtypeStruct((N_DEV,) + tuple(src.shape), src.dtype),
        in_specs=[_HBM], out_specs=_HBM,
        scratch_shapes=[pltpu.SemaphoreType.DMA((N_DEV - 1,)), pltpu.SemaphoreType.DMA((N_DEV - 1,)), pltpu.SemaphoreType.DMA],
    )(src)


def _scatter_to_sibling(name, parts):
    def body(p_ref, out_ref, send_sems, recv_sems):
        x, y, c = lax.axis_index("x"), lax.axis_index("y"), lax.axis_index("c")
        copies = []
        for j, (px, py) in enumerate(_CHIPS):
            cp = pltpu.make_async_remote_copy(
                src_ref=p_ref.at[4 * px + 2 * py + (1 - c)], dst_ref=out_ref.at[j], send_sem=send_sems.at[j],
                recv_sem=recv_sems.at[j], device_id=(x, y, 1 - c), device_id_type=pl.DeviceIdType.MESH)
            cp.start()
            copies.append(cp)
        for cp in copies:
            cp.wait()

    return pl.pallas_call(
        body, name=name, out_shape=jax.ShapeDtypeStruct((4,) + tuple(parts.shape[1:]), parts.dtype),
        in_specs=[_HBM], out_specs=_HBM,
        scratch_shapes=[pltpu.SemaphoreType.DMA((4,)), pltpu.SemaphoreType.DMA((4,))],
    )(parts)


def _chip_sum(name, parts, from_sibling):
    _, rows, cols = parts.shape
    lanes = -(-cols // LANES) * LANES
    tr = _row_tile(rows, 4 * lanes * 4 * 2, budget=12 << 20)

    def kern(p_ref, s_ref, o_ref):
        c = lax.axis_index("c")
        o_ref[0] = (jnp.where(c == 0, p_ref[0, 0], p_ref[0, 1]) + s_ref[0]).astype(o_ref.dtype)

    return pl.pallas_call(
        kern, grid=(4, rows // tr), name=name,
        in_specs=[pl.BlockSpec((1, 2, tr, cols), lambda j, i: (j, 0, i, 0)), pl.BlockSpec((1, tr, cols), lambda j, i: (j, i, 0))],
        out_specs=pl.BlockSpec((1, tr, cols), lambda j, i: (j, i, 0)),
        out_shape=jax.ShapeDtypeStruct((4, rows, cols), BF16),
        compiler_params=_params(4 * lanes * tr * 4 * 2),
    )(parts.reshape(4, 2, rows, cols), from_sibling)


def _scatter_across_chips(name, sums):
    def body(q_ref, out_ref, send_sems, recv_sems, local_sem):
        x, y, c = lax.axis_index("x"), lax.axis_index("y"), lax.axis_index("c")
        own = 2 * x + y
        local = pltpu.make_async_copy(q_ref.at[own], out_ref.at[own], local_sem)
        local.start()
        copies = []
        for d in range(1, 4):
            px, py = lax.rem(x + (d >> 1), 2), lax.rem(y + (d & 1), 2)
            cp = pltpu.make_async_remote_copy(
                src_ref=q_ref.at[2 * px + py], dst_ref=out_ref.at[own], send_sem=send_sems.at[d - 1],
                recv_sem=recv_sems.at[d - 1], device_id=(px, py, c), device_id_type=pl.DeviceIdType.MESH)
            cp.start()
            copies.append(cp)
        for cp in copies:
            cp.wait()
        local.wait()

    return pl.pallas_call(
        body, name=name, out_shape=jax.ShapeDtypeStruct(sums.shape, sums.dtype), in_specs=[_HBM], out_specs=_HBM,
        scratch_shapes=[pltpu.SemaphoreType.DMA((3,)), pltpu.SemaphoreType.DMA((3,)), pltpu.SemaphoreType.DMA],
    )(sums)


def _sum_parts(name, parts):
    n_parts, rows, cols = parts.shape
    tr = _row_tile(rows, 4 * cols * (n_parts + 1) * 2)

    def kern(p_ref, o_ref):
        acc = p_ref[0]
        for k in range(1, n_parts):
            acc = acc + p_ref[k]
        o_ref[...] = acc

    return pl.pallas_call(
        kern, grid=(rows // tr,), name=name, in_specs=[pl.BlockSpec((n_parts, tr, cols), lambda i: (0, i, 0))],
        out_specs=pl.BlockSpec((tr, cols), lambda i: (i, 0)), out_shape=jax.ShapeDtypeStruct((rows, cols), F32),
    )(parts)


def _adamw(name, w, m, v, parts):
    n_parts, rows, cols = parts.shape
    lanes = -(-cols // LANES) * LANES
    tr = _row_tile(rows, 4 * lanes * (n_parts + 7) * 2, budget=16 << 20)
    c1 = np.float32(1.0 - ADAM_B1 ** ADAM_STEP)
    c2 = np.float32(1.0 - ADAM_B2 ** ADAM_STEP)

    def kern(w_ref, m_ref, v_ref, p_ref, g_ref, d_ref, nm_ref, nv_ref):
        g = p_ref[0].astype(F32)
        for k in range(1, n_parts):
            g = g + p_ref[k].astype(F32)
        m_new = ADAM_B1 * m_ref[...] + (1.0 - ADAM_B1) * g
        v_new = ADAM_B2 * v_ref[...] + (1.0 - ADAM_B2) * jnp.square(g)
        g_ref[...] = g
        nm_ref[...] = m_new
        nv_ref[...] = v_new
        d_ref[...] = -ADAM_LR * ((m_new / c1) / (jnp.sqrt(v_new / c2) + ADAM_EPS) + ADAM_WD * w_ref[...])

    spec = pl.BlockSpec((tr, cols), lambda i: (i, 0))
    return pl.pallas_call(
        kern, grid=(rows // tr,), name=name,
        in_specs=[spec, spec, spec, pl.BlockSpec((n_parts, tr, cols), lambda i: (0, i, 0))], out_specs=[spec] * 4,
        out_shape=[jax.ShapeDtypeStruct((rows, cols), F32)] * 4,
        compiler_params=_params(4 * lanes * tr * (n_parts + 7) * 2),
    )(w, m, v, parts)


def _packed_rows(shape):
    return -(-int(np.prod(shape)) // (SUBLANES * LANES)) * SUBLANES


def _pack(arrays, row_multiple):
    parts = []
    for a in arrays:
        n = int(np.prod(a.shape))
        r = _packed_rows(a.shape)
        parts.append(jnp.pad(a.reshape(-1), (0, r * LANES - n)).reshape(r, LANES))
    rows = sum(p.shape[0] for p in parts)
    total = -(-rows // row_multiple) * row_multiple
    if total > rows:
        parts.append(jnp.zeros((total - rows, LANES), arrays[0].dtype))
    return jnp.concatenate(parts, axis=0)


def _unpack(packed, shapes):
    out, off = [], 0
    for s in shapes:
        r = _packed_rows(s)
        out.append(packed[off:off + r].reshape(-1)[:int(np.prod(s))].reshape(s))
        off += r
    return out


def _split_w_in(w_in):
    main = jnp.concatenate([w_in[:, :, s:e] for s, e in _IN_MAIN], axis=2)
    pad = jnp.zeros(w_in.shape[:2] + (LANES - 80,), w_in.dtype)
    small = jnp.concatenate([w_in[:, :, s:e] for s, e in _IN_SMALL] + [pad], axis=2)
    return main, small


def _join_w_in(main, small):
    return jnp.concatenate([main[:, :, 0:8192], small[:, :, 0:64], main[:, :, 8192:12288], small[:, :, 64:80],
                            main[:, :, 12288:15360]], axis=2)


def _unshard(gathered, axis):
    nd, nl, r, c = gathered.shape
    if axis == 1:
        return gathered.transpose(1, 0, 2, 3).reshape(nl, nd * r, c)
    return gathered.transpose(1, 2, 0, 3).reshape(nl, r, nd * c)


def _reshard(full, axis):
    nl, r, c = full.shape
    if axis == 1:
        return full.reshape(nl, N_DEV, r // N_DEV, c).transpose(1, 0, 2, 3)
    return full.reshape(nl, r, N_DEV, c // N_DEV).transpose(2, 0, 1, 3)


def kernel(x, c, ctx, c_ctx, w_ada, b_ada, w_in, lru_conv_w, lru_conv_b, lru_w_r, lru_b_r, lru_w_i, lru_b_i, lru_lambda, ssd_conv_w, ssd_conv_b, ssd_dt_bias, ssd_a_log, ssd_d, ssd_norm_w, ml_conv_w, ml_conv_b, ml_gate_b, ml_norm_w, w_br_a, w_br_b, w_br_c, w_out, b_out, ln1_g, ln1_b, w_ff1, b_ff1, w_ff2, b_ff2, ln2_g, ln2_b, loss_target, m_c_ctx, m_w_ada, m_b_ada, m_w_in, m_lru_conv_w, m_lru_conv_b, m_lru_w_r, m_lru_b_r, m_lru_w_i, m_lru_b_i, m_lru_lambda, m_ssd_conv_w, m_ssd_conv_b, m_ssd_dt_bias, m_ssd_a_log, m_ssd_d, m_ssd_norm_w, m_ml_conv_w, m_ml_conv_b, m_ml_gate_b, m_ml_norm_w, m_w_br_a, m_w_br_b, m_w_br_c, m_w_out, m_b_out, m_ln1_g, m_ln1_b, m_w_ff1, m_b_ff1, m_w_ff2, m_b_ff2, m_ln2_g, m_ln2_b, v_c_ctx, v_w_ada, v_b_ada, v_w_in, v_lru_conv_w, v_lru_conv_b, v_lru_w_r, v_lru_b_r, v_lru_w_i, v_lru_b_i, v_lru_lambda, v_ssd_conv_w, v_ssd_conv_b, v_ssd_dt_bias, v_ssd_a_log, v_ssd_d, v_ssd_norm_w, v_ml_conv_w, v_ml_conv_b, v_ml_gate_b, v_ml_norm_w, v_w_br_a, v_w_br_b, v_w_br_c, v_w_out, v_b_out, v_ln1_g, v_ln1_b, v_w_ff1, v_b_ff1, v_w_ff2, v_b_ff2, v_ln2_g, v_ln2_b):
    a = dict(locals())
    me = 4 * lax.axis_index("x") + 2 * lax.axis_index("y") + lax.axis_index("c")

    wts = {n: a[n] for n in _REPLICATED if n not in ('c_ctx', 'b_ada')}
    for n, axis in _BIG.items():
        if n == 'w_ada':
            continue
        full = _unshard(_gather_two_level("gather_" + n, a[n].astype(BF16)), axis)
        if n == 'w_in':
            main, small = _split_w_in(full)
            wts['w_in_main'], wts['w_in_small'] = main.astype(F32), small.astype(F32)
        else:
            wts[n] = full.astype(F32)
    small_shapes = [a[n].shape for n in _SMALL_SHARDED]
    small_all = _exchange("gather_small", _pack([a[n] for n in _SMALL_SHARDED], SUBLANES), True)
    per_dev = [_unpack(small_all[k], small_shapes) for k in range(N_DEV)]
    for i, n in enumerate(_SMALL_SHARDED):
        wts[n] = jnp.concatenate([per_dev[k][i] for k in range(N_DEV)], axis=-1)

    mods, ada_saved = _ada_forward(c, c_ctx, w_ada, b_ada, me)
    y, vjp = jax.vjp(functools.partial(_forward, ctx=ctx[0]), x[0], wts, mods)
    loss_local, dy = _loss_and_cotangent(y, loss_target[0])
    grad_x, grads, dmods = vjp(dy)
    grads['w_in'] = _join_w_in(grads.pop('w_in_main'), grads.pop('w_in_small'))
    grad_w_ada, grads['b_ada'], grads['c_ctx'] = _ada_backward(ada_saved, dmods)
    loss = lax.psum(loss_local, ("x", "y", "c"))

    out = {}

    def put(n, res, shape):
        for kind, r in zip(("grad_", "delta_", "new_m_", "new_v_"), res):
            out[kind + n] = r.reshape(shape)

    for n, axis in _BIG.items():
        shp = a[n].shape
        rows, cols = shp[0] * shp[1], shp[2]
        if n == 'w_ada':
            parts = grad_w_ada.reshape(1, rows, cols)
        else:
            parts = _reshard(grads[n], axis).reshape(N_DEV, rows, cols)
            chip = _chip_sum("chipsum_" + n, parts, _scatter_to_sibling("scatter_d2d_" + n, parts))
            parts = _scatter_across_chips("scatter_ici_" + n, chip)
        put(n, _adamw("adamw_" + n, a[n].reshape(rows, cols), a["m_" + n].reshape(rows, cols), a["v_" + n].reshape(rows, cols), parts), shp)

    rep_names = _REPLICATED + _SMALL_SHARDED
    chunk_rows = SUBLANES * N_DEV
    g_pack = _pack([grads[n] for n in rep_names], chunk_rows * N_DEV)
    rows = g_pack.shape[0]
    parts = _exchange("scatter_rep", g_pack.reshape(N_DEV, rows // N_DEV, LANES), False)
    mine = _sum_parts("sum_rep", parts)
    g_all = _exchange("gather_rep", mine, True).reshape(rows, LANES)
    g_full = _unpack(g_all, [grads[n].shape for n in rep_names])
    g_local = []
    for n, g in zip(rep_names, g_full):
        if n in _SMALL_SHARDED:
            width = a[n].shape[-1]
            g = lax.dynamic_slice_in_dim(g, me * width, width, axis=g.ndim - 1)
        g_local.append(g)
    shapes = [a[n].shape for n in rep_names]
    res = _adamw("adamw_rep", _pack([a[n] for n in rep_names], chunk_rows), _pack([a["m_" + n] for n in rep_names], chunk_rows),
                 _pack([a["v_" + n] for n in rep_names], chunk_rows), _pack(g_local, chunk_rows)[None])
    unpacked = [_unpack(r, shapes) for r in res]
    for i, n in enumerate(rep_names):
        put(n, [u[i] for u in unpacked], shapes[i])

    outs = [loss, grad_x[None]]
    for kind in ("grad_", "delta_", "new_m_", "new_v_"):
        outs += [out[kind + n] for n in _WEIGHTS]
    return tuple(outs)
```

```python
import functools

import numpy as np
import jax
import jax.numpy as jnp
from jax import lax
from jax.experimental import pallas as pl
from jax.experimental.pallas import tpu as pltpu

F32 = jnp.float32
BF16 = jnp.bfloat16

N_DEV = 8
D_MODEL = 1024
DEPTH = 2
GRID_W = 64
CHUNK = 128
LN_EPS = 1e-6
LRU_BLOCKS = 8
LRU_BS = 128
LRU_C = 8.0
SSD_INNER = 2048
SSD_GROUPS = 8
SSD_HPG = 4
SSD_HEADDIM = 64
SSD_STATE = 128
ML_HEADS = 4
ML_HD = 256
D_FF = 4096
DN_ALPHA = (2 * DEPTH) ** 0.25
ADAM_LR, ADAM_B1, ADAM_B2, ADAM_EPS, ADAM_WD, ADAM_STEP = 0.001, 0.9, 0.999, 1e-08, 0.01, 10

VMEM_CAP = 60 * 1024 * 1024
SUBLANES = 8
LANES = 128

_IN_MAIN = ((0, 8192), (8256, 12352), (12368, 15440))
_IN_MAIN_WIDTHS = (1024, 1024, 2048, 2048, 1024, 1024, 1024, 1024, 1024, 1024, 1024, 1024, 1024)
_IN_SMALL = ((8192, 8256), (12352, 12368))
_DT_LANE = 0
_MG_LANE = 64

_WEIGHTS = ['c_ctx', 'w_ada', 'b_ada', 'w_in', 'lru_conv_w', 'lru_conv_b', 'lru_w_r', 'lru_b_r', 'lru_w_i', 'lru_b_i',
            'lru_lambda', 'ssd_conv_w', 'ssd_conv_b', 'ssd_dt_bias', 'ssd_a_log', 'ssd_d', 'ssd_norm_w', 'ml_conv_w',
            'ml_conv_b', 'ml_gate_b', 'ml_norm_w', 'w_br_a', 'w_br_b', 'w_br_c', 'w_out', 'b_out', 'ln1_g', 'ln1_b',
            'w_ff1', 'b_ff1', 'w_ff2', 'b_ff2', 'ln2_g', 'ln2_b']
_BIG = {'w_ada': 2, 'w_in': 2, 'w_ff1': 2, 'w_br_a': 1, 'w_br_b': 1, 'w_br_c': 1, 'w_out': 1, 'w_ff2': 1}
_SMALL_SHARDED = ['lru_conv_w', 'lru_b_r', 'lru_b_i', 'lru_lambda', 'ssd_conv_w', 'ml_conv_w']
_REPLICATED = [n for n in _WEIGHTS if n not in _BIG and n not in _SMALL_SHARDED]


def _params(vmem_bytes):
    return pltpu.CompilerParams(vmem_limit_bytes=int(min(max(2 * vmem_bytes, 32 << 20), VMEM_CAP)))


def _row_tile(n_rows, bytes_per_row, budget=6 << 20, cap=512):
    t = cap
    while t > SUBLANES and (t * bytes_per_row > budget or n_rows % t):
        t //= 2
    assert n_rows % t == 0, (n_rows, t)
    return t


def _dg(a, b, ca, cb):
    return lax.dot_general(a.astype(BF16), b.astype(BF16), (((ca,), (cb,)), ((), ())), preferred_element_type=F32)


def _make_bdot(ca, cb):
    @jax.custom_vjp
    def f(a, b):
        return _dg(a, b, ca, cb)

    def fwd(a, b):
        return _dg(a, b, ca, cb), (a, b)

    def bwd(res, g):
        a, b = res
        da = _dg(g, b, 1, 1 - cb) if ca == 1 else _dg(b, g, 1 - cb, 1)
        db = _dg(a, g, 1 - ca, 0) if cb == 0 else _dg(g, a, 0, 1 - ca)
        return da, db

    f.defvjp(fwd, bwd)
    return f


_mm_nn = _make_bdot(1, 0)
_mm_nt = _make_bdot(1, 1)
_mm_tn = _make_bdot(0, 0)


@jax.custom_vjp
def _round_bf16(x):
    return x.astype(BF16).astype(F32)


_round_bf16.defvjp(lambda x: (_round_bf16(x), None), lambda _, g: (g,))


def _exact_dot(a, b):
    return jnp.dot(a, b, precision=lax.Precision.HIGHEST, preferred_element_type=F32)


def _layernorm_rows(x):
    mu = jnp.mean(x, -1, keepdims=True)
    var = jnp.mean(jnp.square(x - mu), -1, keepdims=True)
    return (x - mu) * lax.rsqrt(var + LN_EPS)


def _rowwise_calls(name, f, rows, params, out_widths, out_dtype=F32, to_linear=()):
    drow_dtypes = [BF16 if i in to_linear else F32 for i in range(len(rows))]
    nr, npar, no = len(rows), len(params), len(out_widths)
    n_rows = rows[0].shape[0]
    row_w = [r.shape[1] for r in rows]
    par_bytes = sum(int(np.prod(p.shape)) * 4 for p in params)
    tile = _row_tile(n_rows, 4 * (2 * sum(row_w) + 2 * sum(out_widths)))
    grid = (n_rows // tile,)

    def row_spec(w):
        return pl.BlockSpec((tile, w), lambda i: (i, 0))

    def par_spec(p):
        return pl.BlockSpec(p.shape, lambda i: (0, 0))

    vmem = 2 * tile * 4 * (2 * sum(row_w) + 3 * sum(out_widths)) + 4 * par_bytes

    def fwd_call(rows, params):
        def kern(*refs):
            outs = f(*[r[...] for r in refs[:nr + npar]])
            for r, o in zip(refs[nr + npar:], outs):
                r[...] = o.astype(out_dtype)

        return pl.pallas_call(
            kern, grid=grid, name=name + "_fwd",
            in_specs=[row_spec(w) for w in row_w] + [par_spec(p) for p in params],
            out_specs=[row_spec(w) for w in out_widths],
            out_shape=[jax.ShapeDtypeStruct((n_rows, w), out_dtype) for w in out_widths],
            compiler_params=_params(vmem),
        )(*rows, *params)

    def bwd_call(rows, params, gouts):
        def kern(*refs):
            ins = [r[...] for r in refs[:nr + npar]]
            gs = tuple(r[...] for r in refs[nr + npar:nr + npar + no])
            grads = jax.vjp(f, *ins)[1](gs)
            drefs = refs[nr + npar + no:]
            for k in range(nr):
                drefs[k][...] = grads[k].astype(drefs[k].dtype)

            @pl.when(pl.program_id(0) == 0)
            def _():
                for k in range(npar):
                    drefs[nr + k][...] = jnp.zeros_like(drefs[nr + k])

            for k in range(npar):
                drefs[nr + k][...] += grads[nr + k]

        res = pl.pallas_call(
            kern, grid=grid, name=name + "_bwd",
            in_specs=[row_spec(w) for w in row_w] + [par_spec(p) for p in params] + [row_spec(w) for w in out_widths],
            out_specs=[row_spec(w) for w in row_w] + [par_spec(p) for p in params],
            out_shape=[jax.ShapeDtypeStruct(r.shape, dt) for r, dt in zip(rows, drow_dtypes)]
            + [jax.ShapeDtypeStruct(p.shape, F32) for p in params],
            compiler_params=_params(vmem),
        )(*rows, *params, *gouts)
        return tuple(r.astype(F32) for r in res[:nr]), tuple(res[nr:])

    return fwd_call, bwd_call


def _rowwise(name, f, rows, params, out_widths, to_linear=()):
    rows, params = tuple(rows), tuple(params)
    fwd_call, bwd_call = _rowwise_calls(name, f, rows, params, out_widths, to_linear=to_linear)

    @jax.custom_vjp
    def op(rows, params):
        return tuple(fwd_call(rows, params))

    op.defvjp(lambda r, p: (tuple(fwd_call(r, p)), (r, p)), lambda res, g: bwd_call(res[0], res[1], g))
    return op(rows, params)


def _rowwise_linear(name, f, rows, params, weights, to_linear=()):
    rows, params = tuple(rows), tuple(params)
    ws = tuple(w for w, _ in weights)
    m, k = rows[0].shape[0], ws[0].shape[0]
    row_fwd, row_bwd = _rowwise_calls(name, f, rows, params, [k], BF16, to_linear)
    lin = [_linear_calls(name + "lin%d" % i, m, k, w.shape[1], wd, BF16) for i, (w, wd) in enumerate(weights)]
    counts = [len(c[3]) for c in lin]

    def fwd(rows, params, ws):
        (a,) = row_fwd(rows, params)
        outs = []
        for (fwd_call, _, _, _), w in zip(lin, ws):
            outs += list(fwd_call(a, w))
        return tuple(outs), a

    @jax.custom_vjp
    def op(rows, params, ws):
        return fwd(rows, params, ws)[0]

    def op_fwd(rows, params, ws):
        outs, a = fwd(rows, params, ws)
        return outs, (rows, params, ws, a)

    def op_bwd(res, g):
        rows, params, ws, a = res
        da, dws, off = None, [], 0
        for (_, dgrad_call, wgrad_call, _), w, cnt in zip(lin, ws, counts):
            gk = g[off:off + cnt]
            off += cnt
            d = dgrad_call(w, gk)
            da = d if da is None else da + d
            dws.append(wgrad_call(a, gk))
        drows, dparams = row_bwd(rows, params, (da,))
        return drows, dparams, tuple(dws)

    op.defvjp(op_fwd, op_bwd)
    return op(rows, params, ws)


def _group_ranges(widths, tn):
    starts, s = [], 0
    for w in widths:
        assert w % tn == 0, (w, tn)
        starts.append((s // tn, (s + w) // tn))
        s += w
    return starts, s // tn


def _group_tile(refs, ranges, row_tile, col_tile, i, j):
    out = []
    for ref, (s, e) in zip(refs, ranges):
        cols = pl.ds(pl.multiple_of((j - s) * col_tile, col_tile), col_tile)
        out.append(((j >= s) & (j < e), ref, cols))
    return [(p, lambda r=r, c=c: r.at[pl.ds(pl.multiple_of(i * row_tile, row_tile), row_tile), c]) for p, r, c in out]


def _linear_calls(name, m, k, n, widths, a_dtype):
    widths = (n,) if widths is None else tuple(widths)
    ng = len(widths)
    cast_a = a_dtype != BF16
    tn = 128 if n < 256 else (256 if k > 2048 or n % 512 else 512)
    tm = _row_tile(m, 0, cap=1024 if k <= 2048 else 512)
    ranges, nt = _group_ranges(widths, tn)
    mt = m // tm
    tn_w = 1024 if (k <= 1024 and all(wd % 1024 == 0 for wd in widths)) else (512 if all(wd % 512 == 0 for wd in widths) else tn)
    tm_w = _row_tile(m, 0, cap=512)
    ranges_w, nt_w = _group_ranges(widths, tn_w)
    mt_w = m // tm_w
    hbm = pl.BlockSpec(memory_space=pl.ANY)

    def fwd_call(a, w):
        n_steps = mt * nt

        def kern(a_ref, w_ref, *rest):
            outs, obuf, osem = rest[:ng], rest[ng], rest[ng + 1]
            a_bf = rest[ng + 2] if cast_a else a_ref
            i, j = pl.program_id(0), pl.program_id(1)
            step = i * nt + j
            slot = lax.rem(step, 2)

            def drain(sl):
                pltpu.make_async_copy(obuf.at[sl], outs[0].at[pl.ds(0, tm), pl.ds(0, tn)], osem.at[sl]).wait()

            if cast_a:
                @pl.when(j == 0)
                def _():
                    a_bf[...] = a_ref[...].astype(BF16)

            @pl.when(step >= 2)
            def _():
                drain(slot)

            obuf[slot] = jnp.dot(a_bf[...], w_ref[...], preferred_element_type=F32)
            for pred, window in _group_tile(outs, ranges, tm, tn, i, j):
                @pl.when(pred)
                def _(window=window):
                    pltpu.make_async_copy(obuf.at[slot], window(), osem.at[slot]).start()

            @pl.when(step == n_steps - 1)
            def _():
                drain(slot)
                if n_steps > 1:
                    drain(1 - slot)

        return pl.pallas_call(
            kern, grid=(mt, nt), name=name + "_fwd",
            in_specs=[pl.BlockSpec((tm, k), lambda i, j: (i, 0)), pl.BlockSpec((k, tn), lambda i, j: (0, j))],
            out_specs=[hbm] * ng,
            out_shape=[jax.ShapeDtypeStruct((m, wd), F32) for wd in widths],
            scratch_shapes=[pltpu.VMEM((2, tm, tn), F32), pltpu.SemaphoreType.DMA((2,))]
            + ([pltpu.VMEM((tm, k), BF16)] if cast_a else []),
            compiler_params=_params(10 * tm * k + 4 * k * tn + 8 * tm * tn),
        )(a, w.astype(BF16))

    def prefetched(gs, gbuf, gsem, rngs, row_tile, col_tile, step, n_steps, tile_of):
        slot = lax.rem(step, 2)

        def start(s_idx, sl):
            ii, jj = tile_of(s_idx)
            for pred, window in _group_tile(gs, rngs, row_tile, col_tile, ii, jj):
                @pl.when(pred)
                def _(window=window):
                    pltpu.make_async_copy(window(), gbuf.at[sl], gsem.at[sl]).start()

        @pl.when(step == 0)
        def _():
            start(step, slot)

        @pl.when(step + 1 < n_steps)
        def _():
            start(step + 1, 1 - slot)

        pltpu.make_async_copy(gs[0].at[pl.ds(0, row_tile), pl.ds(0, col_tile)], gbuf.at[slot], gsem.at[slot]).wait()
        return slot

    def dgrad_call(w, gouts):
        def kern(w_ref, *rest):
            gs, da, gbuf, gsem = rest[:ng], rest[ng], rest[ng + 1], rest[ng + 2]
            i, j = pl.program_id(0), pl.program_id(1)
            slot = prefetched(gs, gbuf, gsem, ranges, tm, tn, i * nt + j, mt * nt, lambda s: (s // nt, lax.rem(s, nt)))

            @pl.when(j == 0)
            def _():
                da[...] = jnp.zeros_like(da)

            da[...] += lax.dot_general(gbuf[slot].astype(BF16), w_ref[...], (((1,), (1,)), ((), ())), preferred_element_type=F32)

        return pl.pallas_call(
            kern, grid=(mt, nt), name=name + "_dgrad",
            in_specs=[pl.BlockSpec((k, tn), lambda i, j: (0, j))] + [hbm] * ng,
            out_specs=pl.BlockSpec((tm, k), lambda i, j: (i, 0)),
            out_shape=jax.ShapeDtypeStruct((m, k), F32),
            scratch_shapes=[pltpu.VMEM((2, tm, tn), BF16), pltpu.SemaphoreType.DMA((2,))],
            compiler_params=_params(12 * tm * k + 4 * k * tn + 10 * tm * tn),
        )(w.astype(BF16), *[g.astype(BF16) for g in gouts])

    def wgrad_call(a, gouts):
        def kern(a_ref, *rest):
            gs, dw, gbuf, gsem = rest[:ng], rest[ng], rest[ng + 1], rest[ng + 2]
            j, i = pl.program_id(0), pl.program_id(1)
            slot = prefetched(gs, gbuf, gsem, ranges_w, tm_w, tn_w, j * mt_w + i, mt_w * nt_w,
                              lambda s: (lax.rem(s, mt_w), s // mt_w))

            @pl.when(i == 0)
            def _():
                dw[...] = jnp.zeros_like(dw)

            dw[...] += lax.dot_general(a_ref[...].astype(BF16), gbuf[slot].astype(BF16), (((0,), (0,)), ((), ())),
                                       preferred_element_type=F32)

        return pl.pallas_call(
            kern, grid=(nt_w, mt_w), name=name + "_wgrad",
            in_specs=[pl.BlockSpec((tm_w, k), lambda j, i: (i, 0))] + [hbm] * ng,
            out_specs=pl.BlockSpec((k, tn_w), lambda j, i: (0, j)),
            out_shape=jax.ShapeDtypeStruct((k, n), F32),
            scratch_shapes=[pltpu.VMEM((2, tm_w, tn_w), BF16), pltpu.SemaphoreType.DMA((2,))],
            compiler_params=_params(12 * tm_w * k + 12 * k * tn_w + 10 * tm_w * tn_w),
        )(a, *[g.astype(BF16) for g in gouts])

    return fwd_call, dgrad_call, wgrad_call, widths


def _linear(name, a, w, widths=None):
    fwd_call, dgrad_call, wgrad_call, _ = _linear_calls(name, a.shape[0], a.shape[1], w.shape[1], widths, a.dtype)

    @jax.custom_vjp
    def op(a, w):
        return tuple(fwd_call(a, w))

    op.defvjp(lambda a, w: (tuple(fwd_call(a, w)), (a, w)),
              lambda res, g: (dgrad_call(res[1], g), wgrad_call(res[0], g)))
    out = op(a, w)
    return out[0] if widths is None else out


def _conv_taps(x_ext, w, n_ext):
    xm2 = pltpu.roll(x_ext, 2, 0)
    xm1 = pltpu.roll(x_ext, 1, 0)
    xp1 = pltpu.roll(x_ext, n_ext - 1, 0)
    return xm2, xm1, xp1


def _dwconv(name, x, w, b, act):
    n_rows, ch = x.shape
    tt = _row_tile(n_rows, 4 * 6 * ch, cap=256)
    nt = n_rows // tt
    n_ext = tt + 2 * SUBLANES
    per8 = tt // SUBLANES
    last8 = n_rows // SUBLANES - 1
    main = pl.BlockSpec((tt, ch), lambda i: (i, 0))
    prev = pl.BlockSpec((SUBLANES, ch), lambda i: (jnp.maximum(i * per8 - 1, 0), 0))
    nxt = pl.BlockSpec((SUBLANES, ch), lambda i: (jnp.minimum((i + 1) * per8, last8), 0))
    wspec = pl.BlockSpec((4, ch), lambda i: (0, 0))
    bspec = pl.BlockSpec((1, ch), lambda i: (0, 0))
    vmem = 4 * n_ext * ch * 14

    def ext(main_ref, prev_ref, next_ref):
        i = pl.program_id(0)
        p = jnp.where(i > 0, prev_ref[...], 0.0)
        q = jnp.where(i < nt - 1, next_ref[...], 0.0)
        return jnp.concatenate([p, main_ref[...], q], axis=0)

    def pre_of(x_ext, wv, bv):
        xm2, xm1, xp1 = _conv_taps(x_ext, wv, n_ext)
        pre = wv[0:1] * xm2 + wv[1:2] * xm1 + wv[2:3] * x_ext + wv[3:4] * xp1 + bv
        return pre, (xm2, xm1, xp1)

    def fwd_call(x, w, b):
        def kern(xm, xp, xn, w_ref, b_ref, o_ref):
            pre, _ = pre_of(ext(xm, xp, xn), w_ref[...], b_ref[...])
            pre = pre[SUBLANES:SUBLANES + tt]
            o_ref[...] = pre * jax.nn.sigmoid(pre) if act else pre

        return pl.pallas_call(
            kern, grid=(nt,), name=name + "_fwd", in_specs=[main, prev, nxt, wspec, bspec], out_specs=main,
            out_shape=jax.ShapeDtypeStruct((n_rows, ch), F32), compiler_params=_params(vmem),
        )(x, x, x, w, b)

    def bwd_call(x, w, b, dy):
        def kern(xm, xp, xn, gm, gp, gn, w_ref, b_ref, dx_ref, dw_ref, db_ref):
            wv = w_ref[...]
            x_ext = ext(xm, xp, xn)
            pre, (xm2, xm1, xp1) = pre_of(x_ext, wv, b_ref[...])
            dpre = ext(gm, gp, gn)
            if act:
                sg = jax.nn.sigmoid(pre)
                dpre = dpre * (sg + pre * sg * (1.0 - sg))
            dx = (wv[0:1] * pltpu.roll(dpre, n_ext - 2, 0) + wv[1:2] * pltpu.roll(dpre, n_ext - 1, 0)
                  + wv[2:3] * dpre + wv[3:4] * pltpu.roll(dpre, 1, 0))
            sl = slice(SUBLANES, SUBLANES + tt)
            dx_ref[...] = dx[sl].astype(dx_ref.dtype)
            dm = dpre[sl]

            @pl.when(pl.program_id(0) == 0)
            def _():
                dw_ref[...] = jnp.zeros_like(dw_ref)
                db_ref[...] = jnp.zeros_like(db_ref)

            dw_ref[...] += jnp.concatenate(
                [jnp.sum(dm * t[sl], axis=0, keepdims=True) for t in (xm2, xm1, x_ext, xp1)], axis=0)
            db_ref[...] += jnp.sum(dm, axis=0, keepdims=True)

        return pl.pallas_call(
            kern, grid=(nt,), name=name + "_bwd", in_specs=[main, prev, nxt, main, prev, nxt, wspec, bspec],
            out_specs=[main, wspec, bspec],
            out_shape=[jax.ShapeDtypeStruct((n_rows, ch), BF16), jax.ShapeDtypeStruct((4, ch), F32),
                       jax.ShapeDtypeStruct((1, ch), F32)],
            compiler_params=_params(vmem),
        )(x, x, x, dy, dy, dy, w, b)

    @jax.custom_vjp
    def op(x, w, b):
        return fwd_call(x, w, b)

    def op_bwd(res, g):
        dx, dw, db = bwd_call(*res, g)
        return dx.astype(F32), dw, db

    op.defvjp(lambda x, w, b: (fwd_call(x, w, b), (x, w, b)), op_bwd)
    return op(x, w, b)


def _scan_groups(tt, ch, reverse, load, store, carry_ref):
    row = lax.broadcasted_iota(jnp.int32, (SUBLANES, ch), 0)
    ng = tt // SUBLANES

    def body(k, carry):
        g = (ng - 1 - k) if reverse else k
        sl = pl.ds(pl.multiple_of(g * SUBLANES, SUBLANES), SUBLANES)
        a, b, extra = load(sl)
        for s in (1, 2, 4):
            if reverse:
                a_sh, b_sh, valid = pltpu.roll(a, SUBLANES - s, 0), pltpu.roll(b, SUBLANES - s, 0), row < SUBLANES - s
            else:
                a_sh, b_sh, valid = pltpu.roll(a, s, 0), pltpu.roll(b, s, 0), row >= s
            b = jnp.where(valid, b + a * b_sh, b)
            a = jnp.where(valid, a * a_sh, a)
        h = b + a * carry
        if reverse:
            h_prev = jnp.where(row == SUBLANES - 1, carry, pltpu.roll(h, SUBLANES - 1, 0))
            last = h[0:1]
        else:
            h_prev = jnp.where(row == 0, carry, pltpu.roll(h, 1, 0))
            last = h[SUBLANES - 1:SUBLANES]
        store(sl, h, h_prev, extra)
        return jnp.broadcast_to(last, (SUBLANES, ch))

    carry_ref[...] = lax.fori_loop(0, ng, body, carry_ref[...])


def _lin_scan(name, a, b, h0, reverse):
    n_rows, ch = a.shape
    tt = _row_tile(n_rows, 0, cap=256)
    nt = n_rows // tt
    vmem = 2 * 4 * tt * ch * 5

    def tile_spec(rev):
        return pl.BlockSpec((tt, ch), (lambda i: (nt - 1 - i, 0)) if rev else (lambda i: (i, 0)))

    vec = pl.BlockSpec((1, ch), lambda i: (0, 0))

    def fwd_call(a, b, h0):
        def kern(a_ref, b_ref, h0_ref, h_ref, hp_ref, last_ref, carry):
            @pl.when(pl.program_id(0) == 0)
            def _():
                carry[...] = jnp.broadcast_to(h0_ref[...], carry.shape)

            def load(sl):
                return a_ref[sl, :], b_ref[sl, :], None

            def store(sl, h, h_prev, _):
                h_ref[sl, :] = h
                hp_ref[sl, :] = h_prev

            _scan_groups(tt, ch, reverse, load, store, carry)
            last_ref[...] = carry[0:1]

        return pl.pallas_call(
            kern, grid=(nt,), name=name + "_fwd", in_specs=[tile_spec(reverse), tile_spec(reverse), vec],
            out_specs=[tile_spec(reverse), tile_spec(reverse), vec],
            out_shape=[jax.ShapeDtypeStruct((n_rows, ch), F32)] * 2 + [jax.ShapeDtypeStruct((1, ch), F32)],
            scratch_shapes=[pltpu.VMEM((SUBLANES, ch), F32)], compiler_params=_params(vmem),
        )(a, b, h0)

    def bwd_call(a, h_prev, dh, dlast):
        rev = not reverse

        def kern(a_ref, hp_ref, dh_ref, dl_ref, da_ref, db_ref, d0_ref, carry):
            @pl.when(pl.program_id(0) == 0)
            def _():
                carry[...] = jnp.broadcast_to(dl_ref[...], carry.shape)

            def load(sl):
                av, dv = a_ref[sl, :], dh_ref[sl, :]
                return av, av * dv, dv

            def store(sl, u, u_next, dv):
                g = dv + u_next
                db_ref[sl, :] = g
                da_ref[sl, :] = g * hp_ref[sl, :]

            _scan_groups(tt, ch, rev, load, store, carry)
            d0_ref[...] = carry[0:1]

        return pl.pallas_call(
            kern, grid=(nt,), name=name + "_bwd", in_specs=[tile_spec(rev)] * 3 + [vec],
            out_specs=[tile_spec(rev), tile_spec(rev), vec],
            out_shape=[jax.ShapeDtypeStruct((n_rows, ch), F32)] * 2 + [jax.ShapeDtypeStruct((1, ch), F32)],
            scratch_shapes=[pltpu.VMEM((SUBLANES, ch), F32)], compiler_params=_params(vmem),
        )(a, h_prev, dh, dlast)

    @jax.custom_vjp
    def op(a, b, h0):
        h, _, last = fwd_call(a, b, h0)
        return h, last

    def op_fwd(a, b, h0):
        h, h_prev, last = fwd_call(a, b, h0)
        return (h, last), (a, h_prev)

    def op_bwd(res, g):
        da, db, d0 = bwd_call(res[0], res[1], g[0], g[1])
        return da, db, d0

    op.defvjp(op_fwd, op_bwd)
    return op(a, b, h0)


def _tri(reverse):
    q = lax.broadcasted_iota(jnp.int32, (CHUNK, CHUNK), 0)
    s = lax.broadcasted_iota(jnp.int32, (CHUNK, CHUNK), 1)
    return (q <= s) if reverse else (q >= s)


def _pick_col(x, lane):
    idx = lax.broadcasted_iota(jnp.int32, x.shape, 1)
    return jnp.sum(jnp.where(idx == lane, x, 0.0), axis=1, keepdims=True)


def _pick_row(x, row):
    idx = lax.broadcasted_iota(jnp.int32, x.shape, 0)
    return jnp.sum(jnp.where(idx == row, x, 0.0), axis=0, keepdims=True)


def _ssd_shared(small, bias_row, alog_row, reverse):
    delta_all = jax.nn.softplus(small + bias_row)
    acs_all = _exact_dot(_tri(reverse).astype(F32), delta_all * (-jnp.exp(alog_row)))
    return delta_all, acs_all, acs_all.T


def _ssd_group(xs, bm, cm, state, delta_all, acs_all, acs_t, g, direction, reverse):
    mask = _tri(reverse)
    last = 0 if reverse else CHUNK - 1
    hd = SSD_HEADDIM
    rowi = lax.broadcasted_iota(jnp.int32, (CHUNK, 1), 0)
    a_cols, a_rows, deltas, tots = [], [], [], []
    for r in range(SSD_HPG):
        lane = _DT_LANE + 32 * direction + SSD_HPG * g + r
        a_col = _pick_col(acs_all, lane)
        a_cols.append(a_col)
        a_rows.append(_pick_row(acs_t, lane))
        deltas.append(_pick_col(delta_all, lane))
        tots.append(jnp.sum(jnp.where(rowi == last, a_col, 0.0), axis=0, keepdims=True))

    def wide(cols, rows):
        return jnp.concatenate([jnp.broadcast_to(c, (rows, hd)) for c in cols], axis=1)

    a_w = wide(a_cols, CHUNK)
    x_w = xs * wide(deltas, CHUNK)
    st = _mm_tn(x_w * jnp.exp(wide(tots, 1) - a_w), bm)
    y_off = _mm_nt(cm, state) * jnp.exp(a_w)
    grow = jnp.concatenate([jnp.broadcast_to(jnp.exp(t), (hd, 1)) for t in tots], axis=0)
    cb = _mm_nt(cm, bm)
    m_cat = jnp.concatenate([cb * jnp.exp(jnp.where(mask, a_cols[r] - a_rows[r], -jnp.inf)) for r in range(SSD_HPG)], axis=1)
    lane_head = lax.broadcasted_iota(jnp.int32, (1, SSD_HPG * hd), 1) // hd
    x_bd = jnp.concatenate([jnp.where(lane_head == r, x_w, 0.0) for r in range(SSD_HPG)], axis=0)
    return _mm_nn(m_cat, x_bd) + y_off, grow * state + st


def _ssd_calls(name, xs, bm, cm, s0, direction, reverse):
    n_rows = xs.shape[0]
    nc = n_rows // CHUNK
    gw = SSD_HPG * SSD_HEADDIM
    vmem = 4 * CHUNK * (gw + 3 * 128) * 8 + 4 * gw * 128 * 12 + (8 << 20)

    n_state = SSD_GROUPS * gw
    shared_scratch = [pltpu.VMEM((CHUNK, LANES), F32), pltpu.VMEM((CHUNK, LANES), F32), pltpu.VMEM((LANES, CHUNK), F32)]

    def specs(order, gps=1):
        def cidx(c):
            return (nc - 1 - c) if order else c

        return dict(
            xs=pl.BlockSpec((CHUNK, gps * gw), lambda c, g: (cidx(c), g)),
            bc=pl.BlockSpec((CHUNK, gps * SSD_STATE), lambda c, g: (cidx(c), g)),
            small=pl.BlockSpec((CHUNK, LANES), lambda c, g: (cidx(c), 0)),
            row=pl.BlockSpec((1, LANES), lambda c, g: (0, 0)),
            state=pl.BlockSpec((n_state, SSD_STATE), lambda c, g: (0, 0)),
            enter=pl.BlockSpec((1, gps * gw, SSD_STATE), lambda c, g: (cidx(c), g, 0)),
        )

    gps_fwd, gps_bwd = 2, 1

    def group_rows(g, gps):
        return pl.ds(pl.multiple_of(g * gps * gw, gps * gw), gps * gw)

    def step_fn(g, gps):
        def fn(xs_v, bm_v, cm_v, st_v, d_all, a_all, a_t):
            ys, sts = [], []
            for u in range(gps):
                y_u, s_u = _ssd_group(xs_v[:, u * gw:(u + 1) * gw], bm_v[:, u * SSD_STATE:(u + 1) * SSD_STATE],
                                      cm_v[:, u * SSD_STATE:(u + 1) * SSD_STATE], st_v[u * gw:(u + 1) * gw],
                                      d_all, a_all, a_t, gps * g + u, direction, reverse)
                ys.append(y_u)
                sts.append(s_u)
            return jnp.concatenate(ys, axis=1), jnp.concatenate(sts, axis=0)

        return fn

    def fwd_call(xs, bm, cm, small, bias_row, alog_row, s0):
        gps = gps_fwd
        sp = specs(reverse, gps)

        def kern(xs_r, bm_r, cm_r, sm_r, br_r, ar_r, s0_r, y_r, sf_r, se_r, st, sh_d, sh_a, sh_t):
            c, g = pl.program_id(0), pl.program_id(1)

            @pl.when((c == 0) & (g == 0))
            def _():
                st[...] = s0_r[...]

            @pl.when(g == 0)
            def _():
                sh_d[...], sh_a[...], sh_t[...] = _ssd_shared(sm_r[...], br_r[...], ar_r[...], reverse)

            rows = group_rows(g, gps)
            s_in = st[rows, :]
            se_r[0] = s_in
            y_r[...], s_new = step_fn(g, gps)(xs_r[...], bm_r[...], cm_r[...], s_in, sh_d[...], sh_a[...], sh_t[...])
            st[rows, :] = s_new
            sf_r[rows, :] = s_new

        return pl.pallas_call(
            kern, grid=(nc, SSD_GROUPS // gps), name=name + "_fwd",
            in_specs=[sp['xs'], sp['bc'], sp['bc'], sp['small'], sp['row'], sp['row'], sp['state']],
            out_specs=[sp['xs'], sp['state'], sp['enter']],
            out_shape=[jax.ShapeDtypeStruct((n_rows, SSD_INNER), F32), jax.ShapeDtypeStruct((n_state, SSD_STATE), F32),
                       jax.ShapeDtypeStruct((nc, n_state, SSD_STATE), F32)],
            scratch_shapes=[pltpu.VMEM((n_state, SSD_STATE), F32)] + shared_scratch, compiler_params=_params(vmem),
        )(xs, bm, cm, small, bias_row, alog_row, s0)

    def bwd_call(xs, bm, cm, small, bias_row, alog_row, enter, dy, dsf, acc=()):
        gps = gps_bwd
        sp = specs(not reverse, gps)

        def kern(*refs):
            xs_r, bm_r, cm_r, sm_r, br_r, ar_r, se_r, dy_r, dsf_r = refs[:9]
            acc_r = refs[9:9 + len(acc)]
            dxs_r, dbm_r, dcm_r, dsm_r, dbr_r, dar_r, ds0_r, ds, sh_d, sh_a, sh_t, gd, ga, gt = refs[9 + len(acc):]
            c, g = pl.program_id(0), pl.program_id(1)

            @pl.when((c == 0) & (g == 0))
            def _():
                ds[...] = dsf_r[...]
                dbr_r[...] = jnp.zeros_like(dbr_r)
                dar_r[...] = jnp.zeros_like(dar_r)

            @pl.when(g == 0)
            def _():
                sh_d[...], sh_a[...], sh_t[...] = _ssd_shared(sm_r[...], br_r[...], ar_r[...], reverse)
                gd[...] = jnp.zeros_like(gd)
                ga[...] = jnp.zeros_like(ga)
                gt[...] = jnp.zeros_like(gt)

            rows = group_rows(g, gps)
            _, vjp = jax.vjp(step_fn(g, gps), xs_r[...], bm_r[...], cm_r[...], se_r[0], sh_d[...], sh_a[...], sh_t[...])
            dxs, dbm, dcm, ds_in, dd, da, dt = vjp((dy_r[...], ds[rows, :]))
            dxs_r[...] = dxs + acc_r[0][...] if acc else dxs
            dbm_r[...] = dbm + acc_r[1][...] if acc else dbm
            dcm_r[...] = dcm + acc_r[2][...] if acc else dcm
            ds[rows, :] = ds_in
            ds0_r[rows, :] = ds_in
            gd[...] += dd
            ga[...] += da
            gt[...] += dt

            @pl.when(g == SSD_GROUPS // gps - 1)
            def _():
                shared = functools.partial(_ssd_shared, reverse=reverse)
                dsm, dbr, dar = jax.vjp(shared, sm_r[...], br_r[...], ar_r[...])[1]((gd[...], ga[...], gt[...]))
                dsm_r[...] = dsm + acc_r[3][...] if acc else dsm
                dbr_r[...] += dbr
                dar_r[...] += dar

        return pl.pallas_call(
            kern, grid=(nc, SSD_GROUPS // gps), name=name + "_bwd",
            in_specs=[sp['xs'], sp['bc'], sp['bc'], sp['small'], sp['row'], sp['row'], sp['enter'], sp['xs'], sp['state']]
            + ([sp['xs'], sp['bc'], sp['bc'], sp['small']] if acc else []),
            out_specs=[sp['xs'], sp['bc'], sp['bc'], sp['small'], sp['row'], sp['row'], sp['state']],
            out_shape=[jax.ShapeDtypeStruct(xs.shape, F32), jax.ShapeDtypeStruct(bm.shape, F32),
                       jax.ShapeDtypeStruct(cm.shape, F32), jax.ShapeDtypeStruct((n_rows, LANES), F32),
                       jax.ShapeDtypeStruct((1, LANES), F32), jax.ShapeDtypeStruct((1, LANES), F32),
                       jax.ShapeDtypeStruct(s0.shape, F32)],
            scratch_shapes=[pltpu.VMEM((n_state, SSD_STATE), F32)] + shared_scratch + shared_scratch,
            compiler_params=_params(vmem),
        )(xs, bm, cm, small, bias_row, alog_row, enter, dy, dsf, *acc)

    return fwd_call, bwd_call


def _ssd_pair(name, xs, bm, cm, small, rows_f, rows_b, s0_f, s0_b):
    calls = [_ssd_calls(name + "%d" % d, xs, bm, cm, s0_f, d, d == 1) for d in range(2)]

    def run_fwd(xs, bm, cm, small, rows_f, rows_b, s0_f, s0_b):
        y_f, sf_f, en_f = calls[0][0](xs, bm, cm, small, *rows_f, s0_f)
        y_b, sf_b, en_b = calls[1][0](xs, bm, cm, small, *rows_b, s0_b)
        return (y_f, y_b, sf_f, sf_b), (en_f, en_b)

    @jax.custom_vjp
    def op(*args):
        return run_fwd(*args)[0]

    def op_fwd(*args):
        outs, enters = run_fwd(*args)
        return outs, (args[:6], enters)

    def op_bwd(res, g):
        (xs, bm, cm, small, rows_f, rows_b), (en_f, en_b) = res
        dy_f, dy_b, dsf_f, dsf_b = g
        dxs, dbm, dcm, dsm, dbr_f, dar_f, ds0_f = calls[0][1](xs, bm, cm, small, *rows_f, en_f, dy_f, dsf_f)
        dxs, dbm, dcm, dsm, dbr_b, dar_b, ds0_b = calls[1][1](xs, bm, cm, small, *rows_b, en_b, dy_b, dsf_b, acc=(dxs, dbm, dcm, dsm))
        return dxs, dbm, dcm, dsm, (dbr_f, dar_f), (dbr_b, dar_b), ds0_f, ds0_b

    op.defvjp(op_fwd, op_bwd)
    return op(xs, bm, cm, small, tuple(rows_f), tuple(rows_b), s0_f, s0_b)


def _ml_shared(small, gate_row, reverse):
    gates = small + gate_row
    b_all = _exact_dot(_tri(reverse).astype(F32), jax.nn.log_sigmoid(gates))
    return gates, b_all, gates.T, b_all.T


def _ml_head(q, k, v, c_st, n_st, m_st, gates, b_all, gates_t, b_t, h, direction, reverse):
    mask = _tri(reverse)
    last = 0 if reverse else CHUNK - 1
    lane_i = _MG_LANE + 8 * direction + h
    lane_f = lane_i + ML_HEADS
    b_col = _pick_col(b_all, lane_f)
    b_row = _pick_row(b_t, lane_f)
    li_col = _pick_col(gates, lane_i)
    li_row = _pick_row(gates_t, lane_i)
    rowi = lax.broadcasted_iota(jnp.int32, (CHUNK, 1), 0)
    g_tot = jnp.sum(jnp.where(rowi == last, b_col, 0.0), axis=0, keepdims=True)
    m_in = m_st[:, 0:1]
    q = q * (ML_HD ** -0.5)
    w = g_tot - b_col + li_col
    m_loc = lax.stop_gradient(jnp.max(w, axis=0, keepdims=True))
    kw = k * jnp.exp(w - m_loc)
    c_loc = _mm_tn(kw, v)
    n_loc = jnp.sum(kw, axis=0, keepdims=True)
    m_new = lax.stop_gradient(jnp.maximum(g_tot + m_in, m_loc))
    s_old = jnp.exp(g_tot + m_in - m_new)
    s_loc = jnp.exp(m_loc - m_new)
    c_new = s_old * c_st + s_loc * c_loc
    n_new = s_old * n_st + s_loc * n_loc
    log_d = jnp.where(mask, b_col - b_row + li_row, -jnp.inf)
    inter = b_col + m_in
    m_t = lax.stop_gradient(jnp.maximum(inter, jnp.max(log_d, axis=1, keepdims=True)))
    dmat = jnp.exp(log_d - m_t)
    wi = jnp.exp(inter - m_t)
    s = _mm_nt(q, k) * dmat
    num = _mm_nn(s, v) + wi * _mm_nn(q, c_st)
    den = jnp.sum(s, axis=1, keepdims=True) + wi * jnp.sum(_round_bf16(q) * _round_bf16(n_st), axis=1, keepdims=True)
    out = num / jnp.maximum(jnp.abs(den), jnp.exp(-m_t))
    return out, c_new, n_new, jnp.broadcast_to(m_new, (1, LANES))


def _ml_calls(name, q, direction, reverse):
    n_rows = q.shape[0]
    nc = n_rows // CHUNK
    vmem = 4 * CHUNK * (4 * ML_HD + 128) * 8 + 4 * ML_HD * ML_HD * 12 + (8 << 20)

    def specs(order):
        def cidx(c):
            return (nc - 1 - c) if order else c

        return dict(
            qkv=pl.BlockSpec((CHUNK, ML_HD), lambda c, h: (cidx(c), h)),
            small=pl.BlockSpec((CHUNK, LANES), lambda c, h: (cidx(c), 0)),
            row=pl.BlockSpec((1, LANES), lambda c, h: (0, 0)),
            c=pl.BlockSpec((ML_HEADS * ML_HD, ML_HD), lambda c, h: (0, 0)),
            n=pl.BlockSpec((ML_HEADS, 1, ML_HD), lambda c, h: (0, 0, 0)),
            m=pl.BlockSpec((ML_HEADS, 1, LANES), lambda c, h: (0, 0, 0)),
            ec=pl.BlockSpec((1, ML_HD, ML_HD), lambda c, h: (cidx(c), h, 0)),
            en=pl.BlockSpec((1, 1, 1, ML_HD), lambda c, h: (cidx(c), h, 0, 0)),
            em=pl.BlockSpec((1, 1, 1, LANES), lambda c, h: (cidx(c), h, 0, 0)),
        )

    st_shapes = [jax.ShapeDtypeStruct((ML_HEADS * ML_HD, ML_HD), F32), jax.ShapeDtypeStruct((ML_HEADS, 1, ML_HD), F32),
                 jax.ShapeDtypeStruct((ML_HEADS, 1, LANES), F32)]
    scratch = [pltpu.VMEM((ML_HEADS * ML_HD, ML_HD), F32), pltpu.VMEM((ML_HEADS, 1, ML_HD), F32),
               pltpu.VMEM((ML_HEADS, 1, LANES), F32)]
    shared_scratch = [pltpu.VMEM((CHUNK, LANES), F32), pltpu.VMEM((CHUNK, LANES), F32),
                      pltpu.VMEM((LANES, CHUNK), F32), pltpu.VMEM((LANES, CHUNK), F32)]

    def head_rows(h):
        return pl.ds(pl.multiple_of(h * ML_HD, ML_HD), ML_HD)

    def fwd_call(q, k, v, small, gate_row, c0, n0, m0):
        sp = specs(reverse)

        def kern(q_r, k_r, v_r, sm_r, gr_r, c0_r, n0_r, m0_r, o_r, cf_r, nf_r, mf_r, ec_r, en_r, em_r, cs, ns, ms, *sh):
            c, h = pl.program_id(0), pl.program_id(1)

            @pl.when((c == 0) & (h == 0))
            def _():
                cs[...] = c0_r[...]
                ns[...] = n0_r[...]
                ms[...] = m0_r[...]

            @pl.when(h == 0)
            def _():
                for ref, val in zip(sh, _ml_shared(sm_r[...], gr_r[...], reverse)):
                    ref[...] = val

            rows = head_rows(h)
            c_in, n_in, m_in = cs[rows, :], ns[h], ms[h]
            ec_r[0] = c_in
            en_r[0, 0] = n_in
            em_r[0, 0] = m_in
            out, c_new, n_new, m_new = _ml_head(q_r[...], k_r[...], v_r[...], c_in, n_in, m_in, *[r[...] for r in sh],
                                                h, direction, reverse)
            o_r[...] = out
            cs[rows, :] = c_new
            ns[h] = n_new
            ms[h] = m_new
            cf_r[rows, :] = c_new
            nf_r[h] = n_new
            mf_r[h] = m_new

        return pl.pallas_call(
            kern, grid=(nc, ML_HEADS), name=name + "_fwd",
            in_specs=[sp['qkv']] * 3 + [sp['small'], sp['row'], sp['c'], sp['n'], sp['m']],
            out_specs=[sp['qkv'], sp['c'], sp['n'], sp['m'], sp['ec'], sp['en'], sp['em']],
            out_shape=[jax.ShapeDtypeStruct((n_rows, ML_HEADS * ML_HD), F32)] + st_shapes + [
                jax.ShapeDtypeStruct((nc, ML_HEADS * ML_HD, ML_HD), F32),
                jax.ShapeDtypeStruct((nc, ML_HEADS, 1, ML_HD), F32), jax.ShapeDtypeStruct((nc, ML_HEADS, 1, LANES), F32)],
            scratch_shapes=scratch + shared_scratch, compiler_params=_params(vmem),
        )(q, k, v, small, gate_row, c0, n0, m0)

    def bwd_call(q, k, v, small, gate_row, ec, en, em, do, dcf, dnf, dmf, acc=()):
        sp = specs(not reverse)
        n_sh = len(shared_scratch)

        def kern(*refs):
            q_r, k_r, v_r, sm_r, gr_r, ec_r, en_r, em_r, do_r, dcf_r, dnf_r, dmf_r = refs[:12]
            acc_r = refs[12:12 + len(acc)]
            dq_r, dk_r, dv_r, dsm_r, dgr_r, dc0_r, dn0_r, dm0_r, dcs, dns, dms = refs[12 + len(acc):23 + len(acc)]
            rest = refs[23 + len(acc):]
            sh, gsh = rest[:n_sh], rest[n_sh:]
            c, h = pl.program_id(0), pl.program_id(1)

            @pl.when((c == 0) & (h == 0))
            def _():
                dcs[...] = dcf_r[...]
                dns[...] = dnf_r[...]
                dms[...] = dmf_r[...]
                dgr_r[...] = jnp.zeros_like(dgr_r)

            @pl.when(h == 0)
            def _():
                for ref, val in zip(sh, _ml_shared(sm_r[...], gr_r[...], reverse)):
                    ref[...] = val
                for ref in gsh:
                    ref[...] = jnp.zeros_like(ref)

            rows = head_rows(h)
            fn = functools.partial(_ml_head, h=h, direction=direction, reverse=reverse)
            _, vjp = jax.vjp(fn, q_r[...], k_r[...], v_r[...], ec_r[0], en_r[0, 0], em_r[0, 0], *[r[...] for r in sh])
            grads = vjp((do_r[...], dcs[rows, :], dns[h], dms[h]))
            dq, dk, dv, dc, dn, dm = grads[:6]
            dq_r[...] = dq + acc_r[0][...] if acc else dq
            dk_r[...] = dk + acc_r[1][...] if acc else dk
            dv_r[...] = dv + acc_r[2][...] if acc else dv
            for ref, val in zip(gsh, grads[6:]):
                ref[...] += val
            dm = jnp.broadcast_to(jnp.sum(dm, axis=1, keepdims=True), (1, LANES)) * (1.0 / LANES)
            dcs[rows, :] = dc
            dns[h] = dn
            dms[h] = dm
            dc0_r[rows, :] = dc
            dn0_r[h] = dn
            dm0_r[h] = dm

            @pl.when(h == ML_HEADS - 1)
            def _():
                shared = functools.partial(_ml_shared, reverse=reverse)
                dsm, dgr = jax.vjp(shared, sm_r[...], gr_r[...])[1](tuple(r[...] for r in gsh))
                dsm_r[...] = dsm + acc_r[3][...] if acc else dsm
                dgr_r[...] += dgr

        return pl.pallas_call(
            kern, grid=(nc, ML_HEADS), name=name + "_bwd",
            in_specs=[sp['qkv']] * 3 + [sp['small'], sp['row'], sp['ec'], sp['en'], sp['em'], sp['qkv'], sp['c'], sp['n'], sp['m']]
            + ([sp['qkv']] * 3 + [sp['small']] if acc else []),
            out_specs=[sp['qkv']] * 3 + [sp['small'], sp['row'], sp['c'], sp['n'], sp['m']],
            out_shape=[jax.ShapeDtypeStruct(q.shape, F32)] * 3 + [jax.ShapeDtypeStruct((n_rows, LANES), F32),
                                                                  jax.ShapeDtypeStruct((1, LANES), F32)] + st_shapes,
            scratch_shapes=scratch + shared_scratch + shared_scratch, compiler_params=_params(vmem),
        )(q, k, v, small, gate_row, ec, en, em, do, dcf, dnf, dmf, *acc)

    return fwd_call, bwd_call


def _ml_pair(name, q, k, v, small, gate_row, state_f, state_b):
    calls = [_ml_calls(name + "%d" % d, q, d, d == 1) for d in range(2)]

    def run_fwd(q, k, v, small, gate_row, state_f, state_b):
        res_f = calls[0][0](q, k, v, small, gate_row, *state_f)
        res_b = calls[1][0](q, k, v, small, gate_row, *state_b)
        return (res_f[0], res_b[0], tuple(res_f[1:4]), tuple(res_b[1:4])), (tuple(res_f[4:]), tuple(res_b[4:]))

    @jax.custom_vjp
    def op(*args):
        return run_fwd(*args)[0]

    def op_fwd(*args):
        outs, enters = run_fwd(*args)
        return outs, (args[:5], enters)

    def op_bwd(res, g):
        (q, k, v, small, gate_row), (en_f, en_b) = res
        do_f, do_b, ds_f, ds_b = g
        dq, dk, dv, dsm, dgr_f, *d0_f = calls[0][1](q, k, v, small, gate_row, *en_f, do_f, *ds_f)
        dq, dk, dv, dsm, dgr_b, *d0_b = calls[1][1](q, k, v, small, gate_row, *en_b, do_b, *ds_b, acc=(dq, dk, dv, dsm))
        return dq, dk, dv, dsm, dgr_f + dgr_b, tuple(d0_f), tuple(d0_b)

    op.defvjp(op_fwd, op_bwd)
    return op(q, k, v, small, gate_row, tuple(state_f), tuple(state_b))


def _f_modulate(x, shift, scale):
    return (_layernorm_rows(x) * (1.0 + scale) + shift,)


def _f_resid_ln(x, o, gate, bias, ln_g, ln_b):
    return (_layernorm_rows(DN_ALPHA * x + gate * (o + bias)) * ln_g + ln_b,)


def _f_lru_gates(xc, w_r, b_r, w_i, b_i, lam):
    outs = []
    for d in range(2):
        def blockdiag(w):
            return jnp.concatenate(
                [_mm_nn(xc[:, n * LRU_BS:(n + 1) * LRU_BS], w[(d * LRU_BLOCKS + n) * LRU_BS:(d * LRU_BLOCKS + n + 1) * LRU_BS, :])
                 for n in range(LRU_BLOCKS)], axis=1)

        r = jax.nn.sigmoid(blockdiag(w_r) + b_r[d:d + 1])
        i = jax.nn.sigmoid(blockdiag(w_i) + b_i[d:d + 1])
        log_a = -LRU_C * r * jax.nn.softplus(-lam[d:d + 1])
        outs += [jnp.exp(log_a), jnp.sqrt(1.0 - jnp.exp(2.0 * log_a)) * i * xc]
    return tuple(outs)


def _f_lru_out(h_f, h_b, ly):
    return ((h_f + h_b) * jax.nn.gelu(ly),)


def _f_ssd_post(y_f, y_b, xs, z, d_exp, norm_w):
    y = (y_f + y_b + xs * d_exp) * jax.nn.silu(z)
    gw = SSD_INNER // SSD_GROUPS
    parts = []
    for g in range(SSD_GROUPS):
        yg = y[:, g * gw:(g + 1) * gw]
        parts.append(yg * lax.rsqrt(jnp.mean(jnp.square(yg), -1, keepdims=True) + LN_EPS))
    return (jnp.concatenate(parts, axis=1) * norm_w,)


def _f_ml_post(h_f, h_b, o, norm_w):
    h = h_f + h_b
    parts = [_layernorm_rows(h[:, i * ML_HD:(i + 1) * ML_HD]) for i in range(ML_HEADS)]
    return (jnp.concatenate(parts, axis=1) * norm_w * jax.nn.sigmoid(o),)


def _f_merge(ga, gb, gc, pa, pb, pc):
    return (jax.nn.sigmoid(ga) * pa + jax.nn.sigmoid(gb) * pb + jax.nn.sigmoid(gc) * pc,)


def _f_relu2(pre, bias):
    return (jnp.square(jax.nn.relu(pre + bias)),)


def _lane_row(vec, start):
    return jnp.pad(vec[None], ((0, 0), (start, LANES - start - vec.shape[0])))


def _mixer(tag, x_tok, shift, scale, p, states):
    (lru_s, ssd_s, ml_s) = states
    lx, ly, sz, xs, bm, cm, mq, mk, mv, mo, ga, gb, gc, small = _rowwise_linear(
        tag + "in", _f_modulate, [x_tok], [shift, scale], [(p['w_in_main'], _IN_MAIN_WIDTHS), (p['w_in_small'], None)])

    xc = _dwconv(tag + "lruconv", lx, p['lru_conv_w'], p['lru_conv_b'][None], False)
    a_f, b_f, a_b, b_b = _rowwise(
        tag + "lrugate", _f_lru_gates, [xc],
        [p['lru_w_r'].reshape(2 * LRU_BLOCKS * LRU_BS, LRU_BS), p['lru_b_r'], p['lru_w_i'].reshape(2 * LRU_BLOCKS * LRU_BS, LRU_BS),
         p['lru_b_i'], p['lru_lambda']], [D_MODEL] * 4)
    h_f, s_f = _lin_scan(tag + "lruscanf", a_f, b_f, lru_s[0], False)
    h_b, s_b = _lin_scan(tag + "lruscanb", a_b, b_b, lru_s[1], True)
    (pa,) = _rowwise_linear(tag + "bra", _f_lru_out, [h_f, h_b, ly], [], [(p['w_br_a'], None)], to_linear=(2,))

    cw, cb_ = p['ssd_conv_w'], p['ssd_conv_b'][None]
    xs_c = _dwconv(tag + "ssdconvx", xs, cw[:, :2048], cb_[:, :2048], True)
    bm_c = _dwconv(tag + "ssdconvb", bm, cw[:, 2048:3072], cb_[:, 2048:3072], True)
    cm_c = _dwconv(tag + "ssdconvc", cm, cw[:, 3072:], cb_[:, 3072:], True)
    dir_rows = [(_lane_row(p['ssd_dt_bias'][d], _DT_LANE + 32 * d), _lane_row(p['ssd_a_log'][d], _DT_LANE + 32 * d)) for d in range(2)]
    *ys, st_f, st_b = _ssd_pair(tag + "ssd", xs_c, bm_c, cm_c, small, dir_rows[0], dir_rows[1], ssd_s[0], ssd_s[1])
    ssd_new = (st_f, st_b)
    (pb,) = _rowwise_linear(tag + "brb", _f_ssd_post, [ys[0], ys[1], xs_c, sz],
                            [jnp.repeat(p['ssd_d'], SSD_HEADDIM)[None], p['ssd_norm_w'][None]], [(p['w_br_b'], None)], to_linear=(3,))

    mw, mb = p['ml_conv_w'], p['ml_conv_b'][None]
    q_c = _dwconv(tag + "mlconvq", mq, mw[:, :1024], mb[:, :1024], True)
    k_c = _dwconv(tag + "mlconvk", mk, mw[:, 1024:], mb[:, 1024:], True)
    gate_row = _lane_row(p['ml_gate_b'].reshape(4 * ML_HEADS), _MG_LANE)
    *hs, ml_f, ml_b = _ml_pair(tag + "ml", q_c, k_c, mv, small, gate_row, ml_s[0], ml_s[1])
    ml_new = (ml_f, ml_b)
    (pc,) = _rowwise_linear(tag + "brc", _f_ml_post, [hs[0], hs[1], mo], [p['ml_norm_w'][None]], [(p['w_br_c'], None)], to_linear=(2,))
    return (ga, gb, gc, pa, pb, pc), ((s_f, s_b), tuple(ssd_new), tuple(ml_new))


def _merge(tag, br, p):
    return _rowwise_linear(tag + "out", _f_merge, list(br), [], [(p['w_out'], None)], to_linear=tuple(range(6)))[0]


def _sublayers(tag, xin, o, mods, p, l):
    sh2, sc2, g1, g2 = mods
    (x1,) = _rowwise(tag + "ln1", _f_resid_ln, [xin, o], [g1, p['b_out'][None], p['ln1_g'][None], p['ln1_b'][None]], [D_MODEL], to_linear=(1,))
    (pre,) = _rowwise_linear(tag + "ff1", _f_modulate, [x1], [sh2, sc2], [(p['w_ff1'], None)])
    (o2,) = _rowwise_linear(tag + "ff2", _f_relu2, [pre], [p['b_ff1'][None]], [(p['w_ff2'], None)], to_linear=(0,))
    (x2,) = _rowwise(tag + "ln2", _f_resid_ln, [x1, o2], [g2, p['b_ff2'][None], p['ln2_g'][None], p['ln2_b'][None]], [D_MODEL], to_linear=(1,))
    return x2


def _to_col_major(h):
    s, d = h.shape
    return h.reshape(s // GRID_W, GRID_W, d).swapaxes(0, 1).reshape(s, d)


def _from_col_major(h):
    s, d = h.shape
    return h.reshape(GRID_W, s // GRID_W, d).swapaxes(0, 1).reshape(s, d)


def _forward(x, wts, mods, ctx):
    zeros = lambda *s: jnp.zeros(s, F32)
    ctx_init = ((zeros(1, D_MODEL), zeros(1, D_MODEL)),
                (zeros(SSD_INNER, SSD_STATE), zeros(SSD_INNER, SSD_STATE)),
                tuple((zeros(ML_HEADS * ML_HD, ML_HD), zeros(ML_HEADS, 1, ML_HD), zeros(ML_HEADS, 1, LANES)) for _ in range(2)))
    for l in range(DEPTH):
        p = {n: wts[n][l] for n in wts}
        tag = "l%d" % l
        sh1x, sc1x, g1x, sh2x, sc2x, g2x = [mods[l][0][:, i * D_MODEL:(i + 1) * D_MODEL] for i in range(6)]
        sh1c, sc1c, g1c, sh2c, sc2c, g2c = [mods[l][1][:, i * D_MODEL:(i + 1) * D_MODEL] for i in range(6)]
        br_c, ctx_states = _mixer(tag + "c", ctx, sh1c, sc1c, p, ctx_init)
        br_x, _ = _mixer(tag + "x", _to_col_major(x) if l % 2 == 1 else x, sh1x, sc1x, p, ctx_states)
        ox = _merge(tag + "x", br_x, p)
        if l % 2 == 1:
            ox = _from_col_major(ox)
        x = _sublayers(tag + "x", x, ox, (sh2x, sc2x, g1x, g2x), p, l)
        if l < DEPTH - 1:
            ctx = _sublayers(tag + "c", ctx, _merge(tag + "c", br_c, p), (sh2c, sc2c, g1c, g2c), p, l)
    return x


_ADA_ROWS = 2 * SUBLANES


def _ada_forward(c, c_ctx, w_ada, b_ada, me):
    c_all = _exchange("gather_c", jnp.broadcast_to(c, (SUBLANES, D_MODEL)), True)[:, 0]

    def rows_of(c_ctx_):
        pad = jnp.zeros((_ADA_ROWS - N_DEV - 1, D_MODEL), F32)
        return jax.nn.silu(jnp.concatenate([c_all, c_ctx_[None], pad], axis=0))

    rows, vjp_rows = jax.vjp(rows_of, c_ctx)
    cols, vjp_cols = jax.vjp(lambda r, w: jnp.stack([_linear("ada%d" % l, r, w[l]) for l in range(DEPTH)]), rows, w_ada)
    full = _exchange("gather_mod", cols, True).transpose(1, 2, 0, 3).reshape(DEPTH, _ADA_ROWS, 6 * D_MODEL) + b_ada[:, None, :]
    mods = [(lax.dynamic_slice_in_dim(full[l], me, 1, axis=0), full[l][N_DEV:N_DEV + 1]) for l in range(DEPTH)]
    return mods, (vjp_rows, vjp_cols)


def _ada_backward(saved, dmods):
    vjp_rows, vjp_cols = saved
    wcol = 6 * D_MODEL // N_DEV
    pad = jnp.zeros((SUBLANES - 2, 6 * D_MODEL), F32)
    both = jnp.stack([jnp.concatenate([dx, dc, pad], axis=0) for dx, dc in dmods])
    send = both.reshape(DEPTH, SUBLANES, N_DEV, wcol).transpose(2, 0, 1, 3)
    recv = _exchange("scatter_dmod", send, False)
    ctx_row = recv[0, :, 1]
    for k in range(1, N_DEV):
        ctx_row = ctx_row + recv[k, :, 1]
    g = jnp.concatenate([recv[:, :, 0].transpose(1, 0, 2), ctx_row[:, None],
                         jnp.zeros((DEPTH, _ADA_ROWS - N_DEV - 1, wcol), F32)], axis=1)
    d_rows, d_w = vjp_cols(g)
    (d_c_ctx,) = vjp_rows(d_rows)
    d_b = jnp.stack([(dx + dc)[0] for dx, dc in dmods])
    return d_w, d_b, d_c_ctx


def _loss_and_cotangent(y, target):
    n_rows, d = y.shape
    tt = _row_tile(n_rows, 0, cap=256)

    def kern(y_ref, t_ref, dy_ref, acc_ref):
        @pl.when(pl.program_id(0) == 0)
        def _():
            acc_ref[...] = jnp.zeros_like(acc_ref)

        err = y_ref[...] - t_ref[...]
        dy_ref[...] = err * (1.0 / d)
        acc_ref[...] += jnp.sum(jnp.square(err))

    spec = pl.BlockSpec((tt, d), lambda i: (i, 0))
    dy, acc = pl.pallas_call(
        kern, grid=(n_rows // tt,), name="loss", in_specs=[spec, spec],
        out_specs=[spec, pl.BlockSpec((SUBLANES, LANES), lambda i: (0, 0))],
        out_shape=[jax.ShapeDtypeStruct((n_rows, d), F32), jax.ShapeDtypeStruct((SUBLANES, LANES), F32)],
    )(y, target)
    return acc[0, 0] * (0.5 / d), dy


def _exchange(name, src, gather):
    slab = src.shape if gather else src.shape[1:]

    def body(src_ref, out_ref, send_sems, recv_sems, local_sem):
        x, y, c = lax.axis_index("x"), lax.axis_index("y"), lax.axis_index("c")
        me = 4 * x + 2 * y + c
        local = pltpu.make_async_copy(src_ref if gather else src_ref.at[me], out_ref.at[me], local_sem)
        local.start()
        copies = []
        for d in range(1, N_DEV):
            px, py, pc = lax.rem(x + (d >> 2), 2), lax.rem(y + ((d >> 1) & 1), 2), lax.rem(c + (d & 1), 2)
            peer = 4 * px + 2 * py + pc
            cp = pltpu.make_async_remote_copy(
                src_ref=src_ref if gather else src_ref.at[peer], dst_ref=out_ref.at[me],
                send_sem=send_sems.at[d - 1], recv_sem=recv_sems.at[d - 1],
                device_id=(px, py, pc), device_id_type=pl.DeviceIdType.MESH)
            cp.start()
            copies.append(cp)
        for cp in copies:
            cp.wait()
        local.wait()

    return pl.pallas_call(
        body, name=name, out_shape=jax.ShapeDtypeStruct((N_DEV,) + tuple(slab), src.dtype),
        in_specs=[pl.BlockSpec(memory_space=pl.ANY)], out_specs=pl.BlockSpec(memory_space=pl.ANY),
        scratch_shapes=[pltpu.SemaphoreType.DMA((N_DEV - 1,)), pltpu.SemaphoreType.DMA((N_DEV - 1,)), pltpu.SemaphoreType.DMA],
    )(src)


_HBM = pl.BlockSpec(memory_space=pl.ANY)
_CHIPS = ((0, 0), (0, 1), (1, 0), (1, 1))


def _gather_two_level(name, srcs):
    n = len(srcs)

    def body(*refs):
        src_refs, out_refs = refs[:n], refs[n:2 * n]
        send_sems, recv_sems, local_sems = refs[2 * n:]
        x, y, c = lax.axis_index("x"), lax.axis_index("y"), lax.axis_index("c")
        me, sibling = (x, y, c), (x, y, 1 - c)
        chips = [(1 - x, y), (x, 1 - y), (1 - x, 1 - y)]

        def slab(a, px, py, pc):
            return out_refs[a].at[4 * px + 2 * py + pc]

        def copy(a, k, block, to, own=False):
            return pltpu.make_async_remote_copy(
                src_ref=src_refs[a] if own else slab(a, *block), dst_ref=slab(a, *block), send_sem=send_sems.at[a, k],
                recv_sem=recv_sems.at[a, k], device_id=to, device_id_type=pl.DeviceIdType.MESH)

        mine = [pltpu.make_async_copy(src_refs[a], slab(a, *me), local_sems.at[a]) for a in range(n)]
        first = []
        for a in range(n):
            mine[a].start()
            first += [copy(a, 0, me, sibling, own=True)] + [copy(a, 1 + j, me, (*chip, c), own=True) for j, chip in enumerate(chips)]
        for cp in first:
            cp.start()
        passed = []
        for j, chip in enumerate(chips):
            for a in range(n):
                copy(a, 1 + j, (*chip, c), me).wait_recv()
                passed.append(copy(a, 4 + j, (*chip, c), sibling))
                passed[-1].start()
        for a in range(n):
            copy(a, 0, sibling, me).wait_recv()
        for j, chip in enumerate(chips):
            for a in range(n):
                copy(a, 4 + j, (*chip, 1 - c), me).wait_recv()
        for cp in first + passed:
            cp.wait_send()
        for cp in mine:
            cp.wait()

    return pl.pallas_call(
        body, name=name, out_shape=[jax.ShapeDtypeStruct((N_DEV,) + tuple(s.shape), s.dtype) for s in srcs],
        in_specs=[_HBM] * n, out_specs=[_HBM] * n,
        scratch_shapes=[pltpu.SemaphoreType.DMA((n, N_DEV - 1)), pltpu.SemaphoreType.DMA((n, N_DEV - 1)), pltpu.SemaphoreType.DMA((n,))],
    )(*srcs)


def _scatter_to_sibling(name, parts_list):
    n = len(parts_list)

    def body(*refs):
        p_refs, out_refs = refs[:n], refs[n:2 * n]
        send_sems, recv_sems = refs[2 * n:]
        x, y, c = lax.axis_index("x"), lax.axis_index("y"), lax.axis_index("c")
        copies = []
        for a in range(n):
            for j, (px, py) in enumerate(_CHIPS):
                cp = pltpu.make_async_remote_copy(
                    src_ref=p_refs[a].at[4 * px + 2 * py + (1 - c)], dst_ref=out_refs[a].at[j], send_sem=send_sems.at[a, j],
                    recv_sem=recv_sems.at[a, j], device_id=(x, y, 1 - c), device_id_type=pl.DeviceIdType.MESH)
                cp.start()
                copies.append(cp)
        for cp in copies:
            cp.wait()

    return pl.pallas_call(
        body, name=name, out_shape=[jax.ShapeDtypeStruct((4,) + tuple(p.shape[1:]), p.dtype) for p in parts_list],
        in_specs=[_HBM] * n, out_specs=[_HBM] * n,
        scratch_shapes=[pltpu.SemaphoreType.DMA((n, 4)), pltpu.SemaphoreType.DMA((n, 4))],
    )(*parts_list)


def _chip_sum(name, parts, from_sibling):
    _, rows, cols = parts.shape
    lanes = -(-cols // LANES) * LANES
    tr = _row_tile(rows, 4 * lanes * 4 * 2, budget=12 << 20)

    def kern(p_ref, s_ref, o_ref):
        c = lax.axis_index("c")
        o_ref[0] = (jnp.where(c == 0, p_ref[0, 0], p_ref[0, 1]) + s_ref[0]).astype(o_ref.dtype)

    return pl.pallas_call(
        kern, grid=(4, rows // tr), name=name,
        in_specs=[pl.BlockSpec((1, 2, tr, cols), lambda j, i: (j, 0, i, 0)), pl.BlockSpec((1, tr, cols), lambda j, i: (j, i, 0))],
        out_specs=pl.BlockSpec((1, tr, cols), lambda j, i: (j, i, 0)),
        out_shape=jax.ShapeDtypeStruct((4, rows, cols), BF16),
        compiler_params=_params(4 * lanes * tr * 4 * 2),
    )(parts.reshape(4, 2, rows, cols), from_sibling)


def _scatter_across_chips(name, sums_list):
    n = len(sums_list)

    def body(*refs):
        q_refs, out_refs = refs[:n], refs[n:2 * n]
        send_sems, recv_sems, local_sems = refs[2 * n:]
        x, y, c = lax.axis_index("x"), lax.axis_index("y"), lax.axis_index("c")
        own = 2 * x + y
        copies = []
        for a in range(n):
            local = pltpu.make_async_copy(q_refs[a].at[own], out_refs[a].at[own], local_sems.at[a])
            local.start()
            copies.append(local)
            for d in range(1, 4):
                px, py = lax.rem(x + (d >> 1), 2), lax.rem(y + (d & 1), 2)
                cp = pltpu.make_async_remote_copy(
                    src_ref=q_refs[a].at[2 * px + py], dst_ref=out_refs[a].at[own], send_sem=send_sems.at[a, d - 1],
                    recv_sem=recv_sems.at[a, d - 1], device_id=(px, py, c), device_id_type=pl.DeviceIdType.MESH)
                cp.start()
                copies.append(cp)
        for cp in copies:
            cp.wait()

    return pl.pallas_call(
        body, name=name, out_shape=[jax.ShapeDtypeStruct(s.shape, s.dtype) for s in sums_list],
        in_specs=[_HBM] * n, out_specs=[_HBM] * n,
        scratch_shapes=[pltpu.SemaphoreType.DMA((n, 3)), pltpu.SemaphoreType.DMA((n, 3)), pltpu.SemaphoreType.DMA((n,))],
    )(*sums_list)


def _sum_parts(name, parts):
    n_parts, rows, cols = parts.shape
    tr = _row_tile(rows, 4 * cols * (n_parts + 1) * 2)

    def kern(p_ref, o_ref):
        acc = p_ref[0]
        for k in range(1, n_parts):
            acc = acc + p_ref[k]
        o_ref[...] = acc

    return pl.pallas_call(
        kern, grid=(rows // tr,), name=name, in_specs=[pl.BlockSpec((n_parts, tr, cols), lambda i: (0, i, 0))],
        out_specs=pl.BlockSpec((tr, cols), lambda i: (i, 0)), out_shape=jax.ShapeDtypeStruct((rows, cols), F32),
    )(parts)


def _adamw(name, w, m, v, parts):
    n_parts, rows, cols = parts.shape
    lanes = -(-cols // LANES) * LANES
    tr = _row_tile(rows, 4 * lanes * (n_parts + 7) * 2, budget=16 << 20)
    c1 = np.float32(1.0 - ADAM_B1 ** ADAM_STEP)
    c2 = np.float32(1.0 - ADAM_B2 ** ADAM_STEP)

    def kern(w_ref, m_ref, v_ref, p_ref, g_ref, d_ref, nm_ref, nv_ref):
        g = p_ref[0].astype(F32)
        for k in range(1, n_parts):
            g = g + p_ref[k].astype(F32)
        m_new = ADAM_B1 * m_ref[...] + (1.0 - ADAM_B1) * g
        v_new = ADAM_B2 * v_ref[...] + (1.0 - ADAM_B2) * jnp.square(g)
        g_ref[...] = g
        nm_ref[...] = m_new
        nv_ref[...] = v_new
        d_ref[...] = -ADAM_LR * ((m_new / c1) / (jnp.sqrt(v_new / c2) + ADAM_EPS) + ADAM_WD * w_ref[...])

    spec = pl.BlockSpec((tr, cols), lambda i: (i, 0))
    return pl.pallas_call(
        kern, grid=(rows // tr,), name=name,
        in_specs=[spec, spec, spec, pl.BlockSpec((n_parts, tr, cols), lambda i: (0, i, 0))], out_specs=[spec] * 4,
        out_shape=[jax.ShapeDtypeStruct((rows, cols), F32)] * 4,
        compiler_params=_params(4 * lanes * tr * (n_parts + 7) * 2),
    )(w, m, v, parts)


def _packed_rows(shape):
    return -(-int(np.prod(shape)) // (SUBLANES * LANES)) * SUBLANES


def _pack(arrays, row_multiple):
    parts = []
    for a in arrays:
        n = int(np.prod(a.shape))
        r = _packed_rows(a.shape)
        parts.append(jnp.pad(a.reshape(-1), (0, r * LANES - n)).reshape(r, LANES))
    rows = sum(p.shape[0] for p in parts)
    total = -(-rows // row_multiple) * row_multiple
    if total > rows:
        parts.append(jnp.zeros((total - rows, LANES), arrays[0].dtype))
    return jnp.concatenate(parts, axis=0)


def _unpack(packed, shapes):
    out, off = [], 0
    for s in shapes:
        r = _packed_rows(s)
        out.append(packed[off:off + r].reshape(-1)[:int(np.prod(s))].reshape(s))
        off += r
    return out


def _split_w_in(w_in):
    main = jnp.concatenate([w_in[:, :, s:e] for s, e in _IN_MAIN], axis=2)
    pad = jnp.zeros(w_in.shape[:2] + (LANES - 80,), w_in.dtype)
    small = jnp.concatenate([w_in[:, :, s:e] for s, e in _IN_SMALL] + [pad], axis=2)
    return main, small


def _join_w_in(main, small):
    return jnp.concatenate([main[:, :, 0:8192], small[:, :, 0:64], main[:, :, 8192:12288], small[:, :, 64:80],
                            main[:, :, 12288:15360]], axis=2)


def _unshard(gathered, axis):
    nd, nl, r, c = gathered.shape
    if axis == 1:
        return gathered.transpose(1, 0, 2, 3).reshape(nl, nd * r, c)
    return gathered.transpose(1, 2, 0, 3).reshape(nl, r, nd * c)


def _reshard(full, axis):
    nl, r, c = full.shape
    if axis == 1:
        return full.reshape(nl, N_DEV, r // N_DEV, c).transpose(1, 0, 2, 3)
    return full.reshape(nl, r, N_DEV, c // N_DEV).transpose(2, 0, 1, 3)


def kernel(x, c, ctx, c_ctx, w_ada, b_ada, w_in, lru_conv_w, lru_conv_b, lru_w_r, lru_b_r, lru_w_i, lru_b_i, lru_lambda, ssd_conv_w, ssd_conv_b, ssd_dt_bias, ssd_a_log, ssd_d, ssd_norm_w, ml_conv_w, ml_conv_b, ml_gate_b, ml_norm_w, w_br_a, w_br_b, w_br_c, w_out, b_out, ln1_g, ln1_b, w_ff1, b_ff1, w_ff2, b_ff2, ln2_g, ln2_b, loss_target, m_c_ctx, m_w_ada, m_b_ada, m_w_in, m_lru_conv_w, m_lru_conv_b, m_lru_w_r, m_lru_b_r, m_lru_w_i, m_lru_b_i, m_lru_lambda, m_ssd_conv_w, m_ssd_conv_b, m_ssd_dt_bias, m_ssd_a_log, m_ssd_d, m_ssd_norm_w, m_ml_conv_w, m_ml_conv_b, m_ml_gate_b, m_ml_norm_w, m_w_br_a, m_w_br_b, m_w_br_c, m_w_out, m_b_out, m_ln1_g, m_ln1_b, m_w_ff1, m_b_ff1, m_w_ff2, m_b_ff2, m_ln2_g, m_ln2_b, v_c_ctx, v_w_ada, v_b_ada, v_w_in, v_lru_conv_w, v_lru_conv_b, v_lru_w_r, v_lru_b_r, v_lru_w_i, v_lru_b_i, v_lru_lambda, v_ssd_conv_w, v_ssd_conv_b, v_ssd_dt_bias, v_ssd_a_log, v_ssd_d, v_ssd_norm_w, v_ml_conv_w, v_ml_conv_b, v_ml_gate_b, v_ml_norm_w, v_w_br_a, v_w_br_b, v_w_br_c, v_w_out, v_b_out, v_ln1_g, v_ln1_b, v_w_ff1, v_b_ff1, v_w_ff2, v_b_ff2, v_ln2_g, v_ln2_b):
    a = dict(locals())
    me = 4 * lax.axis_index("x") + 2 * lax.axis_index("y") + lax.axis_index("c")

    wts = {n: a[n] for n in _REPLICATED if n not in ('c_ctx', 'b_ada')}
    exchanged = [n for n in _BIG if n != 'w_ada']
    gathered = _gather_two_level("gather_weights", [a[n].astype(BF16) for n in exchanged])
    for n, g in zip(exchanged, gathered):
        full = _unshard(g, _BIG[n])
        if n == 'w_in':
            main, small = _split_w_in(full)
            wts['w_in_main'], wts['w_in_small'] = main.astype(F32), small.astype(F32)
        else:
            wts[n] = full.astype(F32)
    small_shapes = [a[n].shape for n in _SMALL_SHARDED]
    small_all = _exchange("gather_small", _pack([a[n] for n in _SMALL_SHARDED], SUBLANES), True)
    per_dev = [_unpack(small_all[k], small_shapes) for k in range(N_DEV)]
    for i, n in enumerate(_SMALL_SHARDED):
        wts[n] = jnp.concatenate([per_dev[k][i] for k in range(N_DEV)], axis=-1)

    mods, ada_saved = _ada_forward(c, c_ctx, w_ada, b_ada, me)
    y, vjp = jax.vjp(functools.partial(_forward, ctx=ctx[0]), x[0], wts, mods)
    loss_local, dy = _loss_and_cotangent(y, loss_target[0])
    grad_x, grads, dmods = vjp(dy)
    grads['w_in'] = _join_w_in(grads.pop('w_in_main'), grads.pop('w_in_small'))
    grad_w_ada, grads['b_ada'], grads['c_ctx'] = _ada_backward(ada_saved, dmods)
    loss = lax.psum(loss_local, ("x", "y", "c"))

    out = {}

    def put(n, res, shape):
        for kind, r in zip(("grad_", "delta_", "new_m_", "new_v_"), res):
            out[kind + n] = r.reshape(shape)

    flat = {n: (a[n].shape[0] * a[n].shape[1], a[n].shape[2]) for n in _BIG}
    by_dest = [_reshard(grads[n], _BIG[n]).reshape(N_DEV, *flat[n]) for n in exchanged]
    from_sibling = _scatter_to_sibling("scatter_d2d", by_dest)
    chip_sums = [_chip_sum("chipsum_" + n, p, s) for n, p, s in zip(exchanged, by_dest, from_sibling)]
    summed = dict(zip(exchanged, _scatter_across_chips("scatter_ici", chip_sums)))
    summed['w_ada'] = grad_w_ada.reshape(1, *flat['w_ada'])
    for n in _BIG:
        shp = a[n].shape
        rows, cols = flat[n]
        parts = summed[n]
        put(n, _adamw("adamw_" + n, a[n].reshape(rows, cols), a["m_" + n].reshape(rows, cols), a["v_" + n].reshape(rows, cols), parts), shp)

    rep_names = _REPLICATED + _SMALL_SHARDED
    chunk_rows = SUBLANES * N_DEV
    g_pack = _pack([grads[n] for n in rep_names], chunk_rows * N_DEV)
    rows = g_pack.shape[0]
    parts = _exchange("scatter_rep", g_pack.reshape(N_DEV, rows // N_DEV, LANES), False)
    mine = _sum_parts("sum_rep", parts)
    g_all = _exchange("gather_rep", mine, True).reshape(rows, LANES)
    g_full = _unpack(g_all, [grads[n].shape for n in rep_names])
    g_local = []
    for n, g in zip(rep_names, g_full):
        if n in _SMALL_SHARDED:
            width = a[n].shape[-1]
            g = lax.dynamic_slice_in_dim(g, me * width, width, axis=g.ndim - 1)
        g_local.append(g)
    shapes = [a[n].shape for n in rep_names]
    res = _adamw("adamw_rep", _pack([a[n] for n in rep_names], chunk_rows), _pack([a["m_" + n] for n in rep_names], chunk_rows),
                 _pack([a["v_" + n] for n in rep_names], chunk_rows), _pack(g_local, chunk_rows)[None])
    unpacked = [_unpack(r, shapes) for r in res]
    for i, n in enumerate(rep_names):
        put(n, [u[i] for u in unpacked], shapes[i])

    outs = [loss, grad_x[None]]
    for kind in ("grad_", "delta_", "new_m_", "new_v_"):
        outs += [out[kind + n] for n in _WEIGHTS]
    return tuple(outs)
```

```python
import functools

import numpy as np
import jax
import jax.numpy as jnp
from jax import lax
from jax.experimental import pallas as pl
from jax.experimental.pallas import tpu as pltpu

F32 = jnp.float32
BF16 = jnp.bfloat16

N_DEV = 8
D_MODEL = 1024
DEPTH = 2
GRID_W = 64
CHUNK = 128
LN_EPS = 1e-6
LRU_BLOCKS = 8
LRU_BS = 128
LRU_C = 8.0
SSD_INNER = 2048
SSD_GROUPS = 8
SSD_HPG = 4
SSD_HEADDIM = 64
SSD_STATE = 128
ML_HEADS = 4
ML_HD = 256
D_FF = 4096
DN_ALPHA = (2 * DEPTH) ** 0.25
ADAM_LR, ADAM_B1, ADAM_B2, ADAM_EPS, ADAM_WD, ADAM_STEP = 0.001, 0.9, 0.999, 1e-08, 0.01, 10

VMEM_CAP = 60 * 1024 * 1024
SUBLANES = 8
LANES = 128

_IN_MAIN = ((0, 8192), (8256, 12352), (12368, 15440))
_IN_MAIN_WIDTHS = (1024, 1024, 2048, 2048, 1024, 1024, 1024, 1024, 1024, 1024, 1024, 1024, 1024)
_IN_SMALL = ((8192, 8256), (12352, 12368))
_DT_LANE = 0
_MG_LANE = 64

_WEIGHTS = ['c_ctx', 'w_ada', 'b_ada', 'w_in', 'lru_conv_w', 'lru_conv_b', 'lru_w_r', 'lru_b_r', 'lru_w_i', 'lru_b_i',
            'lru_lambda', 'ssd_conv_w', 'ssd_conv_b', 'ssd_dt_bias', 'ssd_a_log', 'ssd_d', 'ssd_norm_w', 'ml_conv_w',
            'ml_conv_b', 'ml_gate_b', 'ml_norm_w', 'w_br_a', 'w_br_b', 'w_br_c', 'w_out', 'b_out', 'ln1_g', 'ln1_b',
            'w_ff1', 'b_ff1', 'w_ff2', 'b_ff2', 'ln2_g', 'ln2_b']
_BIG = {'w_ada': 2, 'w_in': 2, 'w_ff1': 2, 'w_br_a': 1, 'w_br_b': 1, 'w_br_c': 1, 'w_out': 1, 'w_ff2': 1}
_SMALL_SHARDED = ['lru_conv_w', 'lru_b_r', 'lru_b_i', 'lru_lambda', 'ssd_conv_w', 'ml_conv_w']
_REPLICATED = [n for n in _WEIGHTS if n not in _BIG and n not in _SMALL_SHARDED]


def _params(vmem_bytes):
    return pltpu.CompilerParams(vmem_limit_bytes=int(min(max(2 * vmem_bytes, 32 << 20), VMEM_CAP)))


def _row_tile(n_rows, bytes_per_row, budget=6 << 20, cap=512):
    t = cap
    while t > SUBLANES and (t * bytes_per_row > budget or n_rows % t):
        t //= 2
    assert n_rows % t == 0, (n_rows, t)
    return t


def _dg(a, b, ca, cb):
    return lax.dot_general(a.astype(BF16), b.astype(BF16), (((ca,), (cb,)), ((), ())), preferred_element_type=F32)


def _make_bdot(ca, cb):
    @jax.custom_vjp
    def f(a, b):
        return _dg(a, b, ca, cb)

    def fwd(a, b):
        return _dg(a, b, ca, cb), (a, b)

    def bwd(res, g):
        a, b = res
        da = _dg(g, b, 1, 1 - cb) if ca == 1 else _dg(b, g, 1 - cb, 1)
        db = _dg(a, g, 1 - ca, 0) if cb == 0 else _dg(g, a, 0, 1 - ca)
        return da, db

    f.defvjp(fwd, bwd)
    return f


_mm_nn = _make_bdot(1, 0)
_mm_nt = _make_bdot(1, 1)
_mm_tn = _make_bdot(0, 0)


@jax.custom_vjp
def _round_bf16(x):
    return x.astype(BF16).astype(F32)


_round_bf16.defvjp(lambda x: (_round_bf16(x), None), lambda _, g: (g,))


def _exact_dot(a, b):
    return jnp.dot(a, b, precision=lax.Precision.HIGHEST, preferred_element_type=F32)


def _layernorm_rows(x):
    mu = jnp.mean(x, -1, keepdims=True)
    var = jnp.mean(jnp.square(x - mu), -1, keepdims=True)
    return (x - mu) * lax.rsqrt(var + LN_EPS)


def _rowwise_calls(name, f, rows, params, out_widths, out_dtype=F32, to_linear=()):
    drow_dtypes = [BF16 if i in to_linear else F32 for i in range(len(rows))]
    nr, npar, no = len(rows), len(params), len(out_widths)
    n_rows = rows[0].shape[0]
    row_w = [r.shape[1] for r in rows]
    par_bytes = sum(int(np.prod(p.shape)) * 4 for p in params)
    tile = _row_tile(n_rows, 4 * (2 * sum(row_w) + 2 * sum(out_widths)), budget=12 << 20)
    grid = (n_rows // tile,)

    def row_spec(w):
        return pl.BlockSpec((tile, w), lambda i: (i, 0))

    def par_spec(p):
        return pl.BlockSpec(p.shape, lambda i: (0, 0))

    vmem = 2 * tile * 4 * (2 * sum(row_w) + 3 * sum(out_widths)) + 4 * par_bytes

    def fwd_call(rows, params):
        def kern(*refs):
            outs = f(*[r[...] for r in refs[:nr + npar]])
            for r, o in zip(refs[nr + npar:], outs):
                r[...] = o.astype(out_dtype)

        return pl.pallas_call(
            kern, grid=grid, name=name + "_fwd",
            in_specs=[row_spec(w) for w in row_w] + [par_spec(p) for p in params],
            out_specs=[row_spec(w) for w in out_widths],
            out_shape=[jax.ShapeDtypeStruct((n_rows, w), out_dtype) for w in out_widths],
            compiler_params=_params(vmem),
        )(*rows, *params)

    def bwd_call(rows, params, gouts):
        def kern(*refs):
            ins = [r[...] for r in refs[:nr + npar]]
            gs = tuple(r[...] for r in refs[nr + npar:nr + npar + no])
            grads = jax.vjp(f, *ins)[1](gs)
            drefs = refs[nr + npar + no:]
            for k in range(nr):
                drefs[k][...] = grads[k].astype(drefs[k].dtype)

            @pl.when(pl.program_id(0) == 0)
            def _():
                for k in range(npar):
                    drefs[nr + k][...] = jnp.zeros_like(drefs[nr + k])

            for k in range(npar):
                drefs[nr + k][...] += grads[nr + k]

        res = pl.pallas_call(
            kern, grid=grid, name=name + "_bwd",
            in_specs=[row_spec(w) for w in row_w] + [par_spec(p) for p in params] + [row_spec(w) for w in out_widths],
            out_specs=[row_spec(w) for w in row_w] + [par_spec(p) for p in params],
            out_shape=[jax.ShapeDtypeStruct(r.shape, dt) for r, dt in zip(rows, drow_dtypes)]
            + [jax.ShapeDtypeStruct(p.shape, F32) for p in params],
            compiler_params=_params(vmem),
        )(*rows, *params, *gouts)
        return tuple(r.astype(F32) for r in res[:nr]), tuple(res[nr:])

    return fwd_call, bwd_call


def _rowwise(name, f, rows, params, out_widths, to_linear=()):
    rows, params = tuple(rows), tuple(params)
    fwd_call, bwd_call = _rowwise_calls(name, f, rows, params, out_widths, to_linear=to_linear)

    @jax.custom_vjp
    def op(rows, params):
        return tuple(fwd_call(rows, params))

    op.defvjp(lambda r, p: (tuple(fwd_call(r, p)), (r, p)), lambda res, g: bwd_call(res[0], res[1], g))
    return op(rows, params)


def _rowwise_linear(name, f, rows, params, weights, to_linear=()):
    rows, params = tuple(rows), tuple(params)
    ws = tuple(w for w, _ in weights)
    m, k = rows[0].shape[0], ws[0].shape[0]
    row_fwd, row_bwd = _rowwise_calls(name, f, rows, params, [k], BF16, to_linear)
    lin = [_linear_calls(name + "lin%d" % i, m, k, w.shape[1], wd, BF16) for i, (w, wd) in enumerate(weights)]
    counts = [len(c[3]) for c in lin]

    def fwd(rows, params, ws):
        (a,) = row_fwd(rows, params)
        outs = []
        for (fwd_call, _, _, _), w in zip(lin, ws):
            outs += list(fwd_call(a, w))
        return tuple(outs), a

    @jax.custom_vjp
    def op(rows, params, ws):
        return fwd(rows, params, ws)[0]

    def op_fwd(rows, params, ws):
        outs, a = fwd(rows, params, ws)
        return outs, (rows, params, ws, a)

    def op_bwd(res, g):
        rows, params, ws, a = res
        da, dws, off = None, [], 0
        for (_, dgrad_call, wgrad_call, _), w, cnt in zip(lin, ws, counts):
            gk = g[off:off + cnt]
            off += cnt
            d = dgrad_call(w, gk)
            da = d if da is None else da + d
            dws.append(wgrad_call(a, gk))
        drows, dparams = row_bwd(rows, params, (da,))
        return drows, dparams, tuple(dws)

    op.defvjp(op_fwd, op_bwd)
    return op(rows, params, ws)


def _group_ranges(widths, tn):
    starts, s = [], 0
    for w in widths:
        assert w % tn == 0, (w, tn)
        starts.append((s // tn, (s + w) // tn))
        s += w
    return starts, s // tn


def _group_tile(refs, ranges, row_tile, col_tile, i, j):
    out = []
    for ref, (s, e) in zip(refs, ranges):
        cols = pl.ds(pl.multiple_of((j - s) * col_tile, col_tile), col_tile)
        out.append(((j >= s) & (j < e), ref, cols))
    return [(p, lambda r=r, c=c: r.at[pl.ds(pl.multiple_of(i * row_tile, row_tile), row_tile), c]) for p, r, c in out]


def _linear_calls(name, m, k, n, widths, a_dtype):
    widths = (n,) if widths is None else tuple(widths)
    ng = len(widths)
    cast_a = a_dtype != BF16
    wide = k <= 1024 and all(wd % 1024 == 0 for wd in widths)
    tn = 128 if n < 256 else (1024 if wide else (256 if k > 2048 or n % 512 else 512))
    tm = _row_tile(m, 0, cap=1024 if k <= 2048 else 512)
    ranges, nt = _group_ranges(widths, tn)
    mt = m // tm
    tn_w = 1024 if wide else (512 if all(wd % 512 == 0 for wd in widths) else tn)
    tm_w = _row_tile(m, 0, cap=512)
    ranges_w, nt_w = _group_ranges(widths, tn_w)
    mt_w = m // tm_w
    hbm = pl.BlockSpec(memory_space=pl.ANY)

    def fwd_call(a, w):
        n_steps = mt * nt

        def kern(a_ref, w_ref, *rest):
            outs, obuf, osem = rest[:ng], rest[ng], rest[ng + 1]
            a_bf = rest[ng + 2] if cast_a else a_ref
            i, j = pl.program_id(0), pl.program_id(1)
            step = i * nt + j
            slot = lax.rem(step, 2)

            def drain(sl):
                pltpu.make_async_copy(obuf.at[sl], outs[0].at[pl.ds(0, tm), pl.ds(0, tn)], osem.at[sl]).wait()

            if cast_a:
                @pl.when(j == 0)
                def _():
                    a_bf[...] = a_ref[...].astype(BF16)

            @pl.when(step >= 2)
            def _():
                drain(slot)

            obuf[slot] = jnp.dot(a_bf[...], w_ref[...], preferred_element_type=F32)
            for pred, window in _group_tile(outs, ranges, tm, tn, i, j):
                @pl.when(pred)
                def _(window=window):
                    pltpu.make_async_copy(obuf.at[slot], window(), osem.at[slot]).start()

            @pl.when(step == n_steps - 1)
            def _():
                drain(slot)
                if n_steps > 1:
                    drain(1 - slot)

        return pl.pallas_call(
            kern, grid=(mt, nt), name=name + "_fwd",
            in_specs=[pl.BlockSpec((tm, k), lambda i, j: (i, 0)), pl.BlockSpec((k, tn), lambda i, j: (0, j))],
            out_specs=[hbm] * ng,
            out_shape=[jax.ShapeDtypeStruct((m, wd), F32) for wd in widths],
            scratch_shapes=[pltpu.VMEM((2, tm, tn), F32), pltpu.SemaphoreType.DMA((2,))]
            + ([pltpu.VMEM((tm, k), BF16)] if cast_a else []),
            compiler_params=_params(10 * tm * k + 4 * k * tn + 8 * tm * tn),
        )(a, w.astype(BF16))

    def prefetched(gs, gbuf, gsem, rngs, row_tile, col_tile, step, n_steps, tile_of):
        slot = lax.rem(step, 2)

        def start(s_idx, sl):
            ii, jj = tile_of(s_idx)
            for pred, window in _group_tile(gs, rngs, row_tile, col_tile, ii, jj):
                @pl.when(pred)
                def _(window=window):
                    pltpu.make_async_copy(window(), gbuf.at[sl], gsem.at[sl]).start()

        @pl.when(step == 0)
        def _():
            start(step, slot)

        @pl.when(step + 1 < n_steps)
        def _():
            start(step + 1, 1 - slot)

        pltpu.make_async_copy(gs[0].at[pl.ds(0, row_tile), pl.ds(0, col_tile)], gbuf.at[slot], gsem.at[slot]).wait()
        return slot

    def dgrad_call(w, gouts):
        def kern(w_ref, *rest):
            gs, da, gbuf, gsem = rest[:ng], rest[ng], rest[ng + 1], rest[ng + 2]
            i, j = pl.program_id(0), pl.program_id(1)
            slot = prefetched(gs, gbuf, gsem, ranges, tm, tn, i * nt + j, mt * nt, lambda s: (s // nt, lax.rem(s, nt)))

            @pl.when(j == 0)
            def _():
                da[...] = jnp.zeros_like(da)

            da[...] += lax.dot_general(gbuf[slot].astype(BF16), w_ref[...], (((1,), (1,)), ((), ())), preferred_element_type=F32)

        return pl.pallas_call(
            kern, grid=(mt, nt), name=name + "_dgrad",
            in_specs=[pl.BlockSpec((k, tn), lambda i, j: (0, j))] + [hbm] * ng,
            out_specs=pl.BlockSpec((tm, k), lambda i, j: (i, 0)),
            out_shape=jax.ShapeDtypeStruct((m, k), F32),
            scratch_shapes=[pltpu.VMEM((2, tm, tn), BF16), pltpu.SemaphoreType.DMA((2,))],
            compiler_params=_params(12 * tm * k + 4 * k * tn + 10 * tm * tn),
        )(w.astype(BF16), *[g.astype(BF16) for g in gouts])

    def wgrad_call(a, gouts):
        def kern(a_ref, *rest):
            gs, dw, gbuf, gsem = rest[:ng], rest[ng], rest[ng + 1], rest[ng + 2]
            j, i = pl.program_id(0), pl.program_id(1)
            slot = prefetched(gs, gbuf, gsem, ranges_w, tm_w, tn_w, j * mt_w + i, mt_w * nt_w,
                              lambda s: (lax.rem(s, mt_w), s // mt_w))

            @pl.when(i == 0)
            def _():
                dw[...] = jnp.zeros_like(dw)

            dw[...] += lax.dot_general(a_ref[...].astype(BF16), gbuf[slot].astype(BF16), (((0,), (0,)), ((), ())),
                                       preferred_element_type=F32)

        return pl.pallas_call(
            kern, grid=(nt_w, mt_w), name=name + "_wgrad",
            in_specs=[pl.BlockSpec((tm_w, k), lambda j, i: (i, 0))] + [hbm] * ng,
            out_specs=pl.BlockSpec((k, tn_w), lambda j, i: (0, j)),
            out_shape=jax.ShapeDtypeStruct((k, n), F32),
            scratch_shapes=[pltpu.VMEM((2, tm_w, tn_w), BF16), pltpu.SemaphoreType.DMA((2,))],
            compiler_params=_params(12 * tm_w * k + 12 * k * tn_w + 10 * tm_w * tn_w),
        )(a, *[g.astype(BF16) for g in gouts])

    return fwd_call, dgrad_call, wgrad_call, widths


def _linear(name, a, w, widths=None):
    fwd_call, dgrad_call, wgrad_call, _ = _linear_calls(name, a.shape[0], a.shape[1], w.shape[1], widths, a.dtype)

    @jax.custom_vjp
    def op(a, w):
        return tuple(fwd_call(a, w))

    op.defvjp(lambda a, w: (tuple(fwd_call(a, w)), (a, w)),
              lambda res, g: (dgrad_call(res[1], g), wgrad_call(res[0], g)))
    out = op(a, w)
    return out[0] if widths is None else out


def _conv_taps(x_ext, w, n_ext):
    xm2 = pltpu.roll(x_ext, 2, 0)
    xm1 = pltpu.roll(x_ext, 1, 0)
    xp1 = pltpu.roll(x_ext, n_ext - 1, 0)
    return xm2, xm1, xp1


def _dwconv(name, x, w, b, act):
    n_rows, ch = x.shape
    tt = _row_tile(n_rows, 4 * 6 * ch, cap=256)
    nt = n_rows // tt
    n_ext = tt + 2 * SUBLANES
    per8 = tt // SUBLANES
    last8 = n_rows // SUBLANES - 1
    main = pl.BlockSpec((tt, ch), lambda i: (i, 0))
    prev = pl.BlockSpec((SUBLANES, ch), lambda i: (jnp.maximum(i * per8 - 1, 0), 0))
    nxt = pl.BlockSpec((SUBLANES, ch), lambda i: (jnp.minimum((i + 1) * per8, last8), 0))
    wspec = pl.BlockSpec((4, ch), lambda i: (0, 0))
    bspec = pl.BlockSpec((1, ch), lambda i: (0, 0))
    vmem = 4 * n_ext * ch * 14

    def ext(main_ref, prev_ref, next_ref):
        i = pl.program_id(0)
        p = jnp.where(i > 0, prev_ref[...], 0.0)
        q = jnp.where(i < nt - 1, next_ref[...], 0.0)
        return jnp.concatenate([p, main_ref[...], q], axis=0)

    def pre_of(x_ext, wv, bv):
        xm2, xm1, xp1 = _conv_taps(x_ext, wv, n_ext)
        pre = wv[0:1] * xm2 + wv[1:2] * xm1 + wv[2:3] * x_ext + wv[3:4] * xp1 + bv
        return pre, (xm2, xm1, xp1)

    def fwd_call(x, w, b):
        def kern(xm, xp, xn, w_ref, b_ref, o_ref):
            pre, _ = pre_of(ext(xm, xp, xn), w_ref[...], b_ref[...])
            pre = pre[SUBLANES:SUBLANES + tt]
            o_ref[...] = pre * jax.nn.sigmoid(pre) if act else pre

        return pl.pallas_call(
            kern, grid=(nt,), name=name + "_fwd", in_specs=[main, prev, nxt, wspec, bspec], out_specs=main,
            out_shape=jax.ShapeDtypeStruct((n_rows, ch), F32), compiler_params=_params(vmem),
        )(x, x, x, w, b)

    def bwd_call(x, w, b, dy):
        def kern(xm, xp, xn, gm, gp, gn, w_ref, b_ref, dx_ref, dw_ref, db_ref):
            wv = w_ref[...]
            x_ext = ext(xm, xp, xn)
            pre, (xm2, xm1, xp1) = pre_of(x_ext, wv, b_ref[...])
            dpre = ext(gm, gp, gn)
            if act:
                sg = jax.nn.sigmoid(pre)
                dpre = dpre * (sg + pre * sg * (1.0 - sg))
            dx = (wv[0:1] * pltpu.roll(dpre, n_ext - 2, 0) + wv[1:2] * pltpu.roll(dpre, n_ext - 1, 0)
                  + wv[2:3] * dpre + wv[3:4] * pltpu.roll(dpre, 1, 0))
            sl = slice(SUBLANES, SUBLANES + tt)
            dx_ref[...] = dx[sl].astype(dx_ref.dtype)
            dm = dpre[sl]

            @pl.when(pl.program_id(0) == 0)
            def _():
                dw_ref[...] = jnp.zeros_like(dw_ref)
                db_ref[...] = jnp.zeros_like(db_ref)

            dw_ref[...] += jnp.concatenate(
                [jnp.sum(dm * t[sl], axis=0, keepdims=True) for t in (xm2, xm1, x_ext, xp1)], axis=0)
            db_ref[...] += jnp.sum(dm, axis=0, keepdims=True)

        return pl.pallas_call(
            kern, grid=(nt,), name=name + "_bwd", in_specs=[main, prev, nxt, main, prev, nxt, wspec, bspec],
            out_specs=[main, wspec, bspec],
            out_shape=[jax.ShapeDtypeStruct((n_rows, ch), BF16), jax.ShapeDtypeStruct((4, ch), F32),
                       jax.ShapeDtypeStruct((1, ch), F32)],
            compiler_params=_params(vmem),
        )(x, x, x, dy, dy, dy, w, b)

    @jax.custom_vjp
    def op(x, w, b):
        return fwd_call(x, w, b)

    def op_bwd(res, g):
        dx, dw, db = bwd_call(*res, g)
        return dx.astype(F32), dw, db

    op.defvjp(lambda x, w, b: (fwd_call(x, w, b), (x, w, b)), op_bwd)
    return op(x, w, b)


def _scan_groups(tt, ch, reverse, load, store, carry_ref):
    row = lax.broadcasted_iota(jnp.int32, (SUBLANES, ch), 0)
    ng = tt // SUBLANES

    def body(k, carry):
        g = (ng - 1 - k) if reverse else k
        sl = pl.ds(pl.multiple_of(g * SUBLANES, SUBLANES), SUBLANES)
        a, b, extra = load(sl)
        for s in (1, 2, 4):
            if reverse:
                a_sh, b_sh, valid = pltpu.roll(a, SUBLANES - s, 0), pltpu.roll(b, SUBLANES - s, 0), row < SUBLANES - s
            else:
                a_sh, b_sh, valid = pltpu.roll(a, s, 0), pltpu.roll(b, s, 0), row >= s
            b = jnp.where(valid, b + a * b_sh, b)
            a = jnp.where(valid, a * a_sh, a)
        h = b + a * carry
        if reverse:
            h_prev = jnp.where(row == SUBLANES - 1, carry, pltpu.roll(h, SUBLANES - 1, 0))
            last = h[0:1]
        else:
            h_prev = jnp.where(row == 0, carry, pltpu.roll(h, 1, 0))
            last = h[SUBLANES - 1:SUBLANES]
        store(sl, h, h_prev, extra)
        return jnp.broadcast_to(last, (SUBLANES, ch))

    carry_ref[...] = lax.fori_loop(0, ng, body, carry_ref[...])


def _lin_scan(name, a, b, h0, reverse):
    n_rows, ch = a.shape
    tt = _row_tile(n_rows, 0, cap=256)
    nt = n_rows // tt
    vmem = 2 * 4 * tt * ch * 5

    def tile_spec(rev):
        return pl.BlockSpec((tt, ch), (lambda i: (nt - 1 - i, 0)) if rev else (lambda i: (i, 0)))

    vec = pl.BlockSpec((1, ch), lambda i: (0, 0))

    def fwd_call(a, b, h0):
        def kern(a_ref, b_ref, h0_ref, h_ref, hp_ref, last_ref, carry):
            @pl.when(pl.program_id(0) == 0)
            def _():
                carry[...] = jnp.broadcast_to(h0_ref[...], carry.shape)

            def load(sl):
                return a_ref[sl, :], b_ref[sl, :], None

            def store(sl, h, h_prev, _):
                h_ref[sl, :] = h
                hp_ref[sl, :] = h_prev

            _scan_groups(tt, ch, reverse, load, store, carry)
            last_ref[...] = carry[0:1]

        return pl.pallas_call(
            kern, grid=(nt,), name=name + "_fwd", in_specs=[tile_spec(reverse), tile_spec(reverse), vec],
            out_specs=[tile_spec(reverse), tile_spec(reverse), vec],
            out_shape=[jax.ShapeDtypeStruct((n_rows, ch), F32)] * 2 + [jax.ShapeDtypeStruct((1, ch), F32)],
            scratch_shapes=[pltpu.VMEM((SUBLANES, ch), F32)], compiler_params=_params(vmem),
        )(a, b, h0)

    def bwd_call(a, h_prev, dh, dlast):
        rev = not reverse

        def kern(a_ref, hp_ref, dh_ref, dl_ref, da_ref, db_ref, d0_ref, carry):
            @pl.when(pl.program_id(0) == 0)
            def _():
                carry[...] = jnp.broadcast_to(dl_ref[...], carry.shape)

            def load(sl):
                av, dv = a_ref[sl, :], dh_ref[sl, :]
                return av, av * dv, dv

            def store(sl, u, u_next, dv):
                g = dv + u_next
                db_ref[sl, :] = g
                da_ref[sl, :] = g * hp_ref[sl, :]

            _scan_groups(tt, ch, rev, load, store, carry)
            d0_ref[...] = carry[0:1]

        return pl.pallas_call(
            kern, grid=(nt,), name=name + "_bwd", in_specs=[tile_spec(rev)] * 3 + [vec],
            out_specs=[tile_spec(rev), tile_spec(rev), vec],
            out_shape=[jax.ShapeDtypeStruct((n_rows, ch), F32)] * 2 + [jax.ShapeDtypeStruct((1, ch), F32)],
            scratch_shapes=[pltpu.VMEM((SUBLANES, ch), F32)], compiler_params=_params(vmem),
        )(a, h_prev, dh, dlast)

    @jax.custom_vjp
    def op(a, b, h0):
        h, _, last = fwd_call(a, b, h0)
        return h, last

    def op_fwd(a, b, h0):
        h, h_prev, last = fwd_call(a, b, h0)
        return (h, last), (a, h_prev)

    def op_bwd(res, g):
        da, db, d0 = bwd_call(res[0], res[1], g[0], g[1])
        return da, db, d0

    op.defvjp(op_fwd, op_bwd)
    return op(a, b, h0)


def _tri(reverse):
    q = lax.broadcasted_iota(jnp.int32, (CHUNK, CHUNK), 0)
    s = lax.broadcasted_iota(jnp.int32, (CHUNK, CHUNK), 1)
    return (q <= s) if reverse else (q >= s)


def _pick_col(x, lane):
    idx = lax.broadcasted_iota(jnp.int32, x.shape, 1)
    return jnp.sum(jnp.where(idx == lane, x, 0.0), axis=1, keepdims=True)


def _pick_row(x, row):
    idx = lax.broadcasted_iota(jnp.int32, x.shape, 0)
    return jnp.sum(jnp.where(idx == row, x, 0.0), axis=0, keepdims=True)


def _ssd_shared(small, bias_row, alog_row, reverse):
    delta_all = jax.nn.softplus(small + bias_row)
    acs_all = _exact_dot(_tri(reverse).astype(F32), delta_all * (-jnp.exp(alog_row)))
    return delta_all, acs_all, acs_all.T


def _ssd_group(xs, bm, cm, state, delta_all, acs_all, acs_t, g, direction, reverse):
    mask = _tri(reverse)
    last = 0 if reverse else CHUNK - 1
    hd = SSD_HEADDIM
    rowi = lax.broadcasted_iota(jnp.int32, (CHUNK, 1), 0)
    a_cols, a_rows, deltas, tots = [], [], [], []
    for r in range(SSD_HPG):
        lane = _DT_LANE + 32 * direction + SSD_HPG * g + r
        a_col = _pick_col(acs_all, lane)
        a_cols.append(a_col)
        a_rows.append(_pick_row(acs_t, lane))
        deltas.append(_pick_col(delta_all, lane))
        tots.append(jnp.sum(jnp.where(rowi == last, a_col, 0.0), axis=0, keepdims=True))

    def wide(cols, rows):
        return jnp.concatenate([jnp.broadcast_to(c, (rows, hd)) for c in cols], axis=1)

    a_w = wide(a_cols, CHUNK)
    x_w = xs * wide(deltas, CHUNK)
    st = _mm_tn(x_w * jnp.exp(wide(tots, 1) - a_w), bm)
    y_off = _mm_nt(cm, state) * jnp.exp(a_w)
    grow = jnp.concatenate([jnp.broadcast_to(jnp.exp(t), (hd, 1)) for t in tots], axis=0)
    cb = _mm_nt(cm, bm)
    m_cat = jnp.concatenate([cb * jnp.exp(jnp.where(mask, a_cols[r] - a_rows[r], -jnp.inf)) for r in range(SSD_HPG)], axis=1)
    lane_head = lax.broadcasted_iota(jnp.int32, (1, SSD_HPG * hd), 1) // hd
    x_bd = jnp.concatenate([jnp.where(lane_head == r, x_w, 0.0) for r in range(SSD_HPG)], axis=0)
    return _mm_nn(m_cat, x_bd) + y_off, grow * state + st


def _ssd_calls(name, xs, bm, cm, s0, direction, reverse):
    n_rows = xs.shape[0]
    nc = n_rows // CHUNK
    gw = SSD_HPG * SSD_HEADDIM
    vmem = 4 * CHUNK * (gw + 3 * 128) * 8 + 4 * gw * 128 * 12 + (8 << 20)

    n_state = SSD_GROUPS * gw
    shared_scratch = [pltpu.VMEM((CHUNK, LANES), F32), pltpu.VMEM((CHUNK, LANES), F32), pltpu.VMEM((LANES, CHUNK), F32)]

    def specs(order, gps=1):
        def cidx(c):
            return (nc - 1 - c) if order else c

        return dict(
            xs=pl.BlockSpec((CHUNK, gps * gw), lambda c, g: (cidx(c), g)),
            bc=pl.BlockSpec((CHUNK, gps * SSD_STATE), lambda c, g: (cidx(c), g)),
            small=pl.BlockSpec((CHUNK, LANES), lambda c, g: (cidx(c), 0)),
            row=pl.BlockSpec((1, LANES), lambda c, g: (0, 0)),
            state=pl.BlockSpec((n_state, SSD_STATE), lambda c, g: (0, 0)),
            enter=pl.BlockSpec((1, gps * gw, SSD_STATE), lambda c, g: (cidx(c), g, 0)),
        )

    gps_fwd, gps_bwd = 2, 1

    def group_rows(g, gps):
        return pl.ds(pl.multiple_of(g * gps * gw, gps * gw), gps * gw)

    def step_fn(g, gps):
        def fn(xs_v, bm_v, cm_v, st_v, d_all, a_all, a_t):
            ys, sts = [], []
            for u in range(gps):
                y_u, s_u = _ssd_group(xs_v[:, u * gw:(u + 1) * gw], bm_v[:, u * SSD_STATE:(u + 1) * SSD_STATE],
                                      cm_v[:, u * SSD_STATE:(u + 1) * SSD_STATE], st_v[u * gw:(u + 1) * gw],
                                      d_all, a_all, a_t, gps * g + u, direction, reverse)
                ys.append(y_u)
                sts.append(s_u)
            return jnp.concatenate(ys, axis=1), jnp.concatenate(sts, axis=0)

        return fn

    def fwd_call(xs, bm, cm, small, bias_row, alog_row, s0):
        gps = gps_fwd
        sp = specs(reverse, gps)

        def kern(xs_r, bm_r, cm_r, sm_r, br_r, ar_r, s0_r, y_r, sf_r, se_r, st, sh_d, sh_a, sh_t):
            c, g = pl.program_id(0), pl.program_id(1)

            @pl.when((c == 0) & (g == 0))
            def _():
                st[...] = s0_r[...]

            @pl.when(g == 0)
            def _():
                sh_d[...], sh_a[...], sh_t[...] = _ssd_shared(sm_r[...], br_r[...], ar_r[...], reverse)

            rows = group_rows(g, gps)
            s_in = st[rows, :]
            se_r[0] = s_in
            y_r[...], s_new = step_fn(g, gps)(xs_r[...], bm_r[...], cm_r[...], s_in, sh_d[...], sh_a[...], sh_t[...])
            st[rows, :] = s_new
            sf_r[rows, :] = s_new

        return pl.pallas_call(
            kern, grid=(nc, SSD_GROUPS // gps), name=name + "_fwd",
            in_specs=[sp['xs'], sp['bc'], sp['bc'], sp['small'], sp['row'], sp['row'], sp['state']],
            out_specs=[sp['xs'], sp['state'], sp['enter']],
            out_shape=[jax.ShapeDtypeStruct((n_rows, SSD_INNER), F32), jax.ShapeDtypeStruct((n_state, SSD_STATE), F32),
                       jax.ShapeDtypeStruct((nc, n_state, SSD_STATE), F32)],
            scratch_shapes=[pltpu.VMEM((n_state, SSD_STATE), F32)] + shared_scratch, compiler_params=_params(vmem),
        )(xs, bm, cm, small, bias_row, alog_row, s0)

    def bwd_call(xs, bm, cm, small, bias_row, alog_row, enter, dy, dsf, acc=()):
        gps = gps_bwd
        sp = specs(not reverse, gps)

        def kern(*refs):
            xs_r, bm_r, cm_r, sm_r, br_r, ar_r, se_r, dy_r, dsf_r = refs[:9]
            acc_r = refs[9:9 + len(acc)]
            dxs_r, dbm_r, dcm_r, dsm_r, dbr_r, dar_r, ds0_r, ds, sh_d, sh_a, sh_t, gd, ga, gt = refs[9 + len(acc):]
            c, g = pl.program_id(0), pl.program_id(1)

            @pl.when((c == 0) & (g == 0))
            def _():
                ds[...] = dsf_r[...]
                dbr_r[...] = jnp.zeros_like(dbr_r)
                dar_r[...] = jnp.zeros_like(dar_r)

            @pl.when(g == 0)
            def _():
                sh_d[...], sh_a[...], sh_t[...] = _ssd_shared(sm_r[...], br_r[...], ar_r[...], reverse)
                gd[...] = jnp.zeros_like(gd)
                ga[...] = jnp.zeros_like(ga)
                gt[...] = jnp.zeros_like(gt)

            rows = group_rows(g, gps)
            _, vjp = jax.vjp(step_fn(g, gps), xs_r[...], bm_r[...], cm_r[...], se_r[0], sh_d[...], sh_a[...], sh_t[...])
            dxs, dbm, dcm, ds_in, dd, da, dt = vjp((dy_r[...], ds[rows, :]))
            dxs_r[...] = dxs + acc_r[0][...] if acc else dxs
            dbm_r[...] = dbm + acc_r[1][...] if acc else dbm
            dcm_r[...] = dcm + acc_r[2][...] if acc else dcm
            ds[rows, :] = ds_in
            ds0_r[rows, :] = ds_in
            gd[...] += dd
            ga[...] += da
            gt[...] += dt

            @pl.when(g == SSD_GROUPS // gps - 1)
            def _():
                shared = functools.partial(_ssd_shared, reverse=reverse)
                dsm, dbr, dar = jax.vjp(shared, sm_r[...], br_r[...], ar_r[...])[1]((gd[...], ga[...], gt[...]))
                dsm_r[...] = dsm + acc_r[3][...] if acc else dsm
                dbr_r[...] += dbr
                dar_r[...] += dar

        return pl.pallas_call(
            kern, grid=(nc, SSD_GROUPS // gps), name=name + "_bwd",
            in_specs=[sp['xs'], sp['bc'], sp['bc'], sp['small'], sp['row'], sp['row'], sp['enter'], sp['xs'], sp['state']]
            + ([sp['xs'], sp['bc'], sp['bc'], sp['small']] if acc else []),
            out_specs=[sp['xs'], sp['bc'], sp['bc'], sp['small'], sp['row'], sp['row'], sp['state']],
            out_shape=[jax.ShapeDtypeStruct(xs.shape, F32), jax.ShapeDtypeStruct(bm.shape, F32),
                       jax.ShapeDtypeStruct(cm.shape, F32), jax.ShapeDtypeStruct((n_rows, LANES), F32),
                       jax.ShapeDtypeStruct((1, LANES), F32), jax.ShapeDtypeStruct((1, LANES), F32),
                       jax.ShapeDtypeStruct(s0.shape, F32)],
            scratch_shapes=[pltpu.VMEM((n_state, SSD_STATE), F32)] + shared_scratch + shared_scratch,
            compiler_params=_params(vmem),
        )(xs, bm, cm, small, bias_row, alog_row, enter, dy, dsf, *acc)

    return fwd_call, bwd_call


def _ssd_pair(name, xs, bm, cm, small, rows_f, rows_b, s0_f, s0_b):
    calls = [_ssd_calls(name + "%d" % d, xs, bm, cm, s0_f, d, d == 1) for d in range(2)]

    def run_fwd(xs, bm, cm, small, rows_f, rows_b, s0_f, s0_b):
        y_f, sf_f, en_f = calls[0][0](xs, bm, cm, small, *rows_f, s0_f)
        y_b, sf_b, en_b = calls[1][0](xs, bm, cm, small, *rows_b, s0_b)
        return (y_f, y_b, sf_f, sf_b), (en_f, en_b)

    @jax.custom_vjp
    def op(*args):
        return run_fwd(*args)[0]

    def op_fwd(*args):
        outs, enters = run_fwd(*args)
        return outs, (args[:6], enters)

    def op_bwd(res, g):
        (xs, bm, cm, small, rows_f, rows_b), (en_f, en_b) = res
        dy_f, dy_b, dsf_f, dsf_b = g
        dxs, dbm, dcm, dsm, dbr_f, dar_f, ds0_f = calls[0][1](xs, bm, cm, small, *rows_f, en_f, dy_f, dsf_f)
        dxs, dbm, dcm, dsm, dbr_b, dar_b, ds0_b = calls[1][1](xs, bm, cm, small, *rows_b, en_b, dy_b, dsf_b, acc=(dxs, dbm, dcm, dsm))
        return dxs, dbm, dcm, dsm, (dbr_f, dar_f), (dbr_b, dar_b), ds0_f, ds0_b

    op.defvjp(op_fwd, op_bwd)
    return op(xs, bm, cm, small, tuple(rows_f), tuple(rows_b), s0_f, s0_b)


def _ml_shared(small, gate_row, reverse):
    gates = small + gate_row
    b_all = _exact_dot(_tri(reverse).astype(F32), jax.nn.log_sigmoid(gates))
    return gates, b_all, gates.T, b_all.T


def _ml_head(q, k, v, c_st, n_st, m_st, gates, b_all, gates_t, b_t, h, direction, reverse):
    mask = _tri(reverse)
    last = 0 if reverse else CHUNK - 1
    lane_i = _MG_LANE + 8 * direction + h
    lane_f = lane_i + ML_HEADS
    b_col = _pick_col(b_all, lane_f)
    b_row = _pick_row(b_t, lane_f)
    li_col = _pick_col(gates, lane_i)
    li_row = _pick_row(gates_t, lane_i)
    rowi = lax.broadcasted_iota(jnp.int32, (CHUNK, 1), 0)
    g_tot = jnp.sum(jnp.where(rowi == last, b_col, 0.0), axis=0, keepdims=True)
    m_in = m_st[:, 0:1]
    q = q * (ML_HD ** -0.5)
    w = g_tot - b_col + li_col
    m_loc = lax.stop_gradient(jnp.max(w, axis=0, keepdims=True))
    kw = k * jnp.exp(w - m_loc)
    c_loc = _mm_tn(kw, v)
    n_loc = jnp.sum(kw, axis=0, keepdims=True)
    m_new = lax.stop_gradient(jnp.maximum(g_tot + m_in, m_loc))
    s_old = jnp.exp(g_tot + m_in - m_new)
    s_loc = jnp.exp(m_loc - m_new)
    c_new = s_old * c_st + s_loc * c_loc
    n_new = s_old * n_st + s_loc * n_loc
    log_d = jnp.where(mask, b_col - b_row + li_row, -jnp.inf)
    inter = b_col + m_in
    m_t = lax.stop_gradient(jnp.maximum(inter, jnp.max(log_d, axis=1, keepdims=True)))
    dmat = jnp.exp(log_d - m_t)
    wi = jnp.exp(inter - m_t)
    s = _mm_nt(q, k) * dmat
    num = _mm_nn(s, v) + wi * _mm_nn(q, c_st)
    den = jnp.sum(s, axis=1, keepdims=True) + wi * jnp.sum(_round_bf16(q) * _round_bf16(n_st), axis=1, keepdims=True)
    out = num / jnp.maximum(jnp.abs(den), jnp.exp(-m_t))
    return out, c_new, n_new, jnp.broadcast_to(m_new, (1, LANES))


def _ml_calls(name, q, direction, reverse):
    n_rows = q.shape[0]
    nc = n_rows // CHUNK
    vmem = 4 * CHUNK * (4 * ML_HD + 128) * 8 + 4 * ML_HD * ML_HD * 12 + (8 << 20)

    def specs(order):
        def cidx(c):
            return (nc - 1 - c) if order else c

        return dict(
            qkv=pl.BlockSpec((CHUNK, ML_HD), lambda c, h: (cidx(c), h)),
            small=pl.BlockSpec((CHUNK, LANES), lambda c, h: (cidx(c), 0)),
            row=pl.BlockSpec((1, LANES), lambda c, h: (0, 0)),
            c=pl.BlockSpec((ML_HEADS * ML_HD, ML_HD), lambda c, h: (0, 0)),
            n=pl.BlockSpec((ML_HEADS, 1, ML_HD), lambda c, h: (0, 0, 0)),
            m=pl.BlockSpec((ML_HEADS, 1, LANES), lambda c, h: (0, 0, 0)),
            ec=pl.BlockSpec((1, ML_HD, ML_HD), lambda c, h: (cidx(c), h, 0)),
            en=pl.BlockSpec((1, 1, 1, ML_HD), lambda c, h: (cidx(c), h, 0, 0)),
            em=pl.BlockSpec((1, 1, 1, LANES), lambda c, h: (cidx(c), h, 0, 0)),
        )

    st_shapes = [jax.ShapeDtypeStruct((ML_HEADS * ML_HD, ML_HD), F32), jax.ShapeDtypeStruct((ML_HEADS, 1, ML_HD), F32),
                 jax.ShapeDtypeStruct((ML_HEADS, 1, LANES), F32)]
    scratch = [pltpu.VMEM((ML_HEADS * ML_HD, ML_HD), F32), pltpu.VMEM((ML_HEADS, 1, ML_HD), F32),
               pltpu.VMEM((ML_HEADS, 1, LANES), F32)]
    shared_scratch = [pltpu.VMEM((CHUNK, LANES), F32), pltpu.VMEM((CHUNK, LANES), F32),
                      pltpu.VMEM((LANES, CHUNK), F32), pltpu.VMEM((LANES, CHUNK), F32)]

    def head_rows(h):
        return pl.ds(pl.multiple_of(h * ML_HD, ML_HD), ML_HD)

    def fwd_call(q, k, v, small, gate_row, c0, n0, m0):
        sp = specs(reverse)

        def kern(q_r, k_r, v_r, sm_r, gr_r, c0_r, n0_r, m0_r, o_r, cf_r, nf_r, mf_r, ec_r, en_r, em_r, cs, ns, ms, *sh):
            c, h = pl.program_id(0), pl.program_id(1)

            @pl.when((c == 0) & (h == 0))
            def _():
                cs[...] = c0_r[...]
                ns[...] = n0_r[...]
                ms[...] = m0_r[...]

            @pl.when(h == 0)
            def _():
                for ref, val in zip(sh, _ml_shared(sm_r[...], gr_r[...], reverse)):
                    ref[...] = val

            rows = head_rows(h)
            c_in, n_in, m_in = cs[rows, :], ns[h], ms[h]
            ec_r[0] = c_in
            en_r[0, 0] = n_in
            em_r[0, 0] = m_in
            out, c_new, n_new, m_new = _ml_head(q_r[...], k_r[...], v_r[...], c_in, n_in, m_in, *[r[...] for r in sh],
                                                h, direction, reverse)
            o_r[...] = out
            cs[rows, :] = c_new
            ns[h] = n_new
            ms[h] = m_new
            cf_r[rows, :] = c_new
            nf_r[h] = n_new
            mf_r[h] = m_new

        return pl.pallas_call(
            kern, grid=(nc, ML_HEADS), name=name + "_fwd",
            in_specs=[sp['qkv']] * 3 + [sp['small'], sp['row'], sp['c'], sp['n'], sp['m']],
            out_specs=[sp['qkv'], sp['c'], sp['n'], sp['m'], sp['ec'], sp['en'], sp['em']],
            out_shape=[jax.ShapeDtypeStruct((n_rows, ML_HEADS * ML_HD), F32)] + st_shapes + [
                jax.ShapeDtypeStruct((nc, ML_HEADS * ML_HD, ML_HD), F32),
                jax.ShapeDtypeStruct((nc, ML_HEADS, 1, ML_HD), F32), jax.ShapeDtypeStruct((nc, ML_HEADS, 1, LANES), F32)],
            scratch_shapes=scratch + shared_scratch, compiler_params=_params(vmem),
        )(q, k, v, small, gate_row, c0, n0, m0)

    def bwd_call(q, k, v, small, gate_row, ec, en, em, do, dcf, dnf, dmf, acc=()):
        sp = specs(not reverse)
        n_sh = len(shared_scratch)

        def kern(*refs):
            q_r, k_r, v_r, sm_r, gr_r, ec_r, en_r, em_r, do_r, dcf_r, dnf_r, dmf_r = refs[:12]
            acc_r = refs[12:12 + len(acc)]
            dq_r, dk_r, dv_r, dsm_r, dgr_r, dc0_r, dn0_r, dm0_r, dcs, dns, dms = refs[12 + len(acc):23 + len(acc)]
            rest = refs[23 + len(acc):]
            sh, gsh = rest[:n_sh], rest[n_sh:]
            c, h = pl.program_id(0), pl.program_id(1)

            @pl.when((c == 0) & (h == 0))
            def _():
                dcs[...] = dcf_r[...]
                dns[...] = dnf_r[...]
                dms[...] = dmf_r[...]
                dgr_r[...] = jnp.zeros_like(dgr_r)

            @pl.when(h == 0)
            def _():
                for ref, val in zip(sh, _ml_shared(sm_r[...], gr_r[...], reverse)):
                    ref[...] = val
                for ref in gsh:
                    ref[...] = jnp.zeros_like(ref)

            rows = head_rows(h)
            fn = functools.partial(_ml_head, h=h, direction=direction, reverse=reverse)
            _, vjp = jax.vjp(fn, q_r[...], k_r[...], v_r[...], ec_r[0], en_r[0, 0], em_r[0, 0], *[r[...] for r in sh])
            grads = vjp((do_r[...], dcs[rows, :], dns[h], dms[h]))
            dq, dk, dv, dc, dn, dm = grads[:6]
            dq_r[...] = dq + acc_r[0][...] if acc else dq
            dk_r[...] = dk + acc_r[1][...] if acc else dk
            dv_r[...] = dv + acc_r[2][...] if acc else dv
            for ref, val in zip(gsh, grads[6:]):
                ref[...] += val
            dm = jnp.broadcast_to(jnp.sum(dm, axis=1, keepdims=True), (1, LANES)) * (1.0 / LANES)
            dcs[rows, :] = dc
            dns[h] = dn
            dms[h] = dm
            dc0_r[rows, :] = dc
            dn0_r[h] = dn
            dm0_r[h] = dm

            @pl.when(h == ML_HEADS - 1)
            def _():
                shared = functools.partial(_ml_shared, reverse=reverse)
                dsm, dgr = jax.vjp(shared, sm_r[...], gr_r[...])[1](tuple(r[...] for r in gsh))
                dsm_r[...] = dsm + acc_r[3][...] if acc else dsm
                dgr_r[...] += dgr

        return pl.pallas_call(
            kern, grid=(nc, ML_HEADS), name=name + "_bwd",
            in_specs=[sp['qkv']] * 3 + [sp['small'], sp['row'], sp['ec'], sp['en'], sp['em'], sp['qkv'], sp['c'], sp['n'], sp['m']]
            + ([sp['qkv']] * 3 + [sp['small']] if acc else []),
            out_specs=[sp['qkv']] * 3 + [sp['small'], sp['row'], sp['c'], sp['n'], sp['m']],
            out_shape=[jax.ShapeDtypeStruct(q.shape, F32)] * 3 + [jax.ShapeDtypeStruct((n_rows, LANES), F32),
                                                                  jax.ShapeDtypeStruct((1, LANES), F32)] + st_shapes,
            scratch_shapes=scratch + shared_scratch + shared_scratch, compiler_params=_params(vmem),
        )(q, k, v, small, gate_row, ec, en, em, do, dcf, dnf, dmf, *acc)

    return fwd_call, bwd_call


def _ml_pair(name, q, k, v, small, gate_row, state_f, state_b):
    calls = [_ml_calls(name + "%d" % d, q, d, d == 1) for d in range(2)]

    def run_fwd(q, k, v, small, gate_row, state_f, state_b):
        res_f = calls[0][0](q, k, v, small, gate_row, *state_f)
        res_b = calls[1][0](q, k, v, small, gate_row, *state_b)
        return (res_f[0], res_b[0], tuple(res_f[1:4]), tuple(res_b[1:4])), (tuple(res_f[4:]), tuple(res_b[4:]))

    @jax.custom_vjp
    def op(*args):
        return run_fwd(*args)[0]

    def op_fwd(*args):
        outs, enters = run_fwd(*args)
        return outs, (args[:5], enters)

    def op_bwd(res, g):
        (q, k, v, small, gate_row), (en_f, en_b) = res
        do_f, do_b, ds_f, ds_b = g
        dq, dk, dv, dsm, dgr_f, *d0_f = calls[0][1](q, k, v, small, gate_row, *en_f, do_f, *ds_f)
        dq, dk, dv, dsm, dgr_b, *d0_b = calls[1][1](q, k, v, small, gate_row, *en_b, do_b, *ds_b, acc=(dq, dk, dv, dsm))
        return dq, dk, dv, dsm, dgr_f + dgr_b, tuple(d0_f), tuple(d0_b)

    op.defvjp(op_fwd, op_bwd)
    return op(q, k, v, small, gate_row, tuple(state_f), tuple(state_b))


def _f_modulate(x, shift, scale):
    return (_layernorm_rows(x) * (1.0 + scale) + shift,)


def _f_resid_ln(x, o, gate, bias, ln_g, ln_b):
    return (_layernorm_rows(DN_ALPHA * x + gate * (o + bias)) * ln_g + ln_b,)


def _f_lru_gates(xc, w_r, b_r, w_i, b_i, lam):
    outs = []
    for d in range(2):
        def blockdiag(w):
            return jnp.concatenate(
                [_mm_nn(xc[:, n * LRU_BS:(n + 1) * LRU_BS], w[(d * LRU_BLOCKS + n) * LRU_BS:(d * LRU_BLOCKS + n + 1) * LRU_BS, :])
                 for n in range(LRU_BLOCKS)], axis=1)

        r = jax.nn.sigmoid(blockdiag(w_r) + b_r[d:d + 1])
        i = jax.nn.sigmoid(blockdiag(w_i) + b_i[d:d + 1])
        log_a = -LRU_C * r * jax.nn.softplus(-lam[d:d + 1])
        outs += [jnp.exp(log_a), jnp.sqrt(1.0 - jnp.exp(2.0 * log_a)) * i * xc]
    return tuple(outs)


def _f_lru_out(h_f, h_b, ly):
    return ((h_f + h_b) * jax.nn.gelu(ly),)


def _f_ssd_post(y_f, y_b, xs, z, d_exp, norm_w):
    y = (y_f + y_b + xs * d_exp) * jax.nn.silu(z)
    gw = SSD_INNER // SSD_GROUPS
    parts = []
    for g in range(SSD_GROUPS):
        yg = y[:, g * gw:(g + 1) * gw]
        parts.append(yg * lax.rsqrt(jnp.mean(jnp.square(yg), -1, keepdims=True) + LN_EPS))
    return (jnp.concatenate(parts, axis=1) * norm_w,)


def _f_ml_post(h_f, h_b, o, norm_w):
    h = h_f + h_b
    parts = [_layernorm_rows(h[:, i * ML_HD:(i + 1) * ML_HD]) for i in range(ML_HEADS)]
    return (jnp.concatenate(parts, axis=1) * norm_w * jax.nn.sigmoid(o),)


def _f_merge(ga, gb, gc, pa, pb, pc):
    return (jax.nn.sigmoid(ga) * pa + jax.nn.sigmoid(gb) * pb + jax.nn.sigmoid(gc) * pc,)


def _f_relu2(pre, bias):
    return (jnp.square(jax.nn.relu(pre + bias)),)


def _lane_row(vec, start):
    return jnp.pad(vec[None], ((0, 0), (start, LANES - start - vec.shape[0])))


def _mixer(tag, x_tok, shift, scale, p, states):
    (lru_s, ssd_s, ml_s) = states
    lx, ly, sz, xs, bm, cm, mq, mk, mv, mo, ga, gb, gc, small = _rowwise_linear(
        tag + "in", _f_modulate, [x_tok], [shift, scale], [(p['w_in_main'], _IN_MAIN_WIDTHS), (p['w_in_small'], None)])

    xc = _dwconv(tag + "lruconv", lx, p['lru_conv_w'], p['lru_conv_b'][None], False)
    a_f, b_f, a_b, b_b = _rowwise(
        tag + "lrugate", _f_lru_gates, [xc],
        [p['lru_w_r'].reshape(2 * LRU_BLOCKS * LRU_BS, LRU_BS), p['lru_b_r'], p['lru_w_i'].reshape(2 * LRU_BLOCKS * LRU_BS, LRU_BS),
         p['lru_b_i'], p['lru_lambda']], [D_MODEL] * 4)
    h_f, s_f = _lin_scan(tag + "lruscanf", a_f, b_f, lru_s[0], False)
    h_b, s_b = _lin_scan(tag + "lruscanb", a_b, b_b, lru_s[1], True)
    (pa,) = _rowwise_linear(tag + "bra", _f_lru_out, [h_f, h_b, ly], [], [(p['w_br_a'], None)], to_linear=(2,))

    cw, cb_ = p['ssd_conv_w'], p['ssd_conv_b'][None]
    xs_c = _dwconv(tag + "ssdconvx", xs, cw[:, :2048], cb_[:, :2048], True)
    bm_c = _dwconv(tag + "ssdconvb", bm, cw[:, 2048:3072], cb_[:, 2048:3072], True)
    cm_c = _dwconv(tag + "ssdconvc", cm, cw[:, 3072:], cb_[:, 3072:], True)
    dir_rows = [(_lane_row(p['ssd_dt_bias'][d], _DT_LANE + 32 * d), _lane_row(p['ssd_a_log'][d], _DT_LANE + 32 * d)) for d in range(2)]
    *ys, st_f, st_b = _ssd_pair(tag + "ssd", xs_c, bm_c, cm_c, small, dir_rows[0], dir_rows[1], ssd_s[0], ssd_s[1])
    ssd_new = (st_f, st_b)
    (pb,) = _rowwise_linear(tag + "brb", _f_ssd_post, [ys[0], ys[1], xs_c, sz],
                            [jnp.repeat(p['ssd_d'], SSD_HEADDIM)[None], p['ssd_norm_w'][None]], [(p['w_br_b'], None)], to_linear=(3,))

    mw, mb = p['ml_conv_w'], p['ml_conv_b'][None]
    q_c = _dwconv(tag + "mlconvq", mq, mw[:, :1024], mb[:, :1024], True)
    k_c = _dwconv(tag + "mlconvk", mk, mw[:, 1024:], mb[:, 1024:], True)
    gate_row = _lane_row(p['ml_gate_b'].reshape(4 * ML_HEADS), _MG_LANE)
    *hs, ml_f, ml_b = _ml_pair(tag + "ml", q_c, k_c, mv, small, gate_row, ml_s[0], ml_s[1])
    ml_new = (ml_f, ml_b)
    (pc,) = _rowwise_linear(tag + "brc", _f_ml_post, [hs[0], hs[1], mo], [p['ml_norm_w'][None]], [(p['w_br_c'], None)], to_linear=(2,))
    return (ga, gb, gc, pa, pb, pc), ((s_f, s_b), tuple(ssd_new), tuple(ml_new))


def _merge(tag, br, p):
    return _rowwise_linear(tag + "out", _f_merge, list(br), [], [(p['w_out'], None)], to_linear=tuple(range(6)))[0]


def _sublayers(tag, xin, o, mods, p, l):
    sh2, sc2, g1, g2 = mods
    (x1,) = _rowwise(tag + "ln1", _f_resid_ln, [xin, o], [g1, p['b_out'][None], p['ln1_g'][None], p['ln1_b'][None]], [D_MODEL], to_linear=(1,))
    (pre,) = _rowwise_linear(tag + "ff1", _f_modulate, [x1], [sh2, sc2], [(p['w_ff1'], None)])
    (o2,) = _rowwise_linear(tag + "ff2", _f_relu2, [pre], [p['b_ff1'][None]], [(p['w_ff2'], None)], to_linear=(0,))
    (x2,) = _rowwise(tag + "ln2", _f_resid_ln, [x1, o2], [g2, p['b_ff2'][None], p['ln2_g'][None], p['ln2_b'][None]], [D_MODEL], to_linear=(1,))
    return x2


def _to_col_major(h):
    s, d = h.shape
    return h.reshape(s // GRID_W, GRID_W, d).swapaxes(0, 1).reshape(s, d)


def _from_col_major(h):
    s, d = h.shape
    return h.reshape(GRID_W, s // GRID_W, d).swapaxes(0, 1).reshape(s, d)


def _forward(x, wts, mods, ctx):
    zeros = lambda *s: jnp.zeros(s, F32)
    ctx_init = ((zeros(1, D_MODEL), zeros(1, D_MODEL)),
                (zeros(SSD_INNER, SSD_STATE), zeros(SSD_INNER, SSD_STATE)),
                tuple((zeros(ML_HEADS * ML_HD, ML_HD), zeros(ML_HEADS, 1, ML_HD), zeros(ML_HEADS, 1, LANES)) for _ in range(2)))
    for l in range(DEPTH):
        p = {n: wts[n][l] for n in wts}
        tag = "l%d" % l
        sh1x, sc1x, g1x, sh2x, sc2x, g2x = [mods[l][0][:, i * D_MODEL:(i + 1) * D_MODEL] for i in range(6)]
        sh1c, sc1c, g1c, sh2c, sc2c, g2c = [mods[l][1][:, i * D_MODEL:(i + 1) * D_MODEL] for i in range(6)]
        br_c, ctx_states = _mixer(tag + "c", ctx, sh1c, sc1c, p, ctx_init)
        br_x, _ = _mixer(tag + "x", _to_col_major(x) if l % 2 == 1 else x, sh1x, sc1x, p, ctx_states)
        ox = _merge(tag + "x", br_x, p)
        if l % 2 == 1:
            ox = _from_col_major(ox)
        x = _sublayers(tag + "x", x, ox, (sh2x, sc2x, g1x, g2x), p, l)
        if l < DEPTH - 1:
            ctx = _sublayers(tag + "c", ctx, _merge(tag + "c", br_c, p), (sh2c, sc2c, g1c, g2c), p, l)
    return x


_ADA_ROWS = 2 * SUBLANES


def _ada_forward(c, c_ctx, w_ada, b_ada, me):
    c_all = _exchange("gather_c", jnp.broadcast_to(c, (SUBLANES, D_MODEL)), True)[:, 0]

    def rows_of(c_ctx_):
        pad = jnp.zeros((_ADA_ROWS - N_DEV - 1, D_MODEL), F32)
        return jax.nn.silu(jnp.concatenate([c_all, c_ctx_[None], pad], axis=0))

    rows, vjp_rows = jax.vjp(rows_of, c_ctx)
    cols, vjp_cols = jax.vjp(lambda r, w: jnp.stack([_linear("ada%d" % l, r, w[l]) for l in range(DEPTH)]), rows, w_ada)
    full = _exchange("gather_mod", cols, True).transpose(1, 2, 0, 3).reshape(DEPTH, _ADA_ROWS, 6 * D_MODEL) + b_ada[:, None, :]
    mods = [(lax.dynamic_slice_in_dim(full[l], me, 1, axis=0), full[l][N_DEV:N_DEV + 1]) for l in range(DEPTH)]
    return mods, (vjp_rows, vjp_cols)


def _ada_backward(saved, dmods):
    vjp_rows, vjp_cols = saved
    wcol = 6 * D_MODEL // N_DEV
    pad = jnp.zeros((SUBLANES - 2, 6 * D_MODEL), F32)
    both = jnp.stack([jnp.concatenate([dx, dc, pad], axis=0) for dx, dc in dmods])
    send = both.reshape(DEPTH, SUBLANES, N_DEV, wcol).transpose(2, 0, 1, 3)
    recv = _exchange("scatter_dmod", send, False)
    ctx_row = recv[0, :, 1]
    for k in range(1, N_DEV):
        ctx_row = ctx_row + recv[k, :, 1]
    g = jnp.concatenate([recv[:, :, 0].transpose(1, 0, 2), ctx_row[:, None],
                         jnp.zeros((DEPTH, _ADA_ROWS - N_DEV - 1, wcol), F32)], axis=1)
    d_rows, d_w = vjp_cols(g)
    (d_c_ctx,) = vjp_rows(d_rows)
    d_b = jnp.stack([(dx + dc)[0] for dx, dc in dmods])
    return d_w, d_b, d_c_ctx


def _loss_and_cotangent(y, target):
    n_rows, d = y.shape
    tt = _row_tile(n_rows, 0, cap=256)

    def kern(y_ref, t_ref, dy_ref, acc_ref):
        @pl.when(pl.program_id(0) == 0)
        def _():
            acc_ref[...] = jnp.zeros_like(acc_ref)

        err = y_ref[...] - t_ref[...]
        dy_ref[...] = err * (1.0 / d)
        acc_ref[...] += jnp.sum(jnp.square(err))

    spec = pl.BlockSpec((tt, d), lambda i: (i, 0))
    dy, acc = pl.pallas_call(
        kern, grid=(n_rows // tt,), name="loss", in_specs=[spec, spec],
        out_specs=[spec, pl.BlockSpec((SUBLANES, LANES), lambda i: (0, 0))],
        out_shape=[jax.ShapeDtypeStruct((n_rows, d), F32), jax.ShapeDtypeStruct((SUBLANES, LANES), F32)],
    )(y, target)
    return acc[0, 0] * (0.5 / d), dy


def _exchange(name, src, gather):
    slab = src.shape if gather else src.shape[1:]

    def body(src_ref, out_ref, send_sems, recv_sems, local_sem):
        x, y, c = lax.axis_index("x"), lax.axis_index("y"), lax.axis_index("c")
        me = 4 * x + 2 * y + c
        local = pltpu.make_async_copy(src_ref if gather else src_ref.at[me], out_ref.at[me], local_sem)
        local.start()
        copies = []
        for d in range(1, N_DEV):
            px, py, pc = lax.rem(x + (d >> 2), 2), lax.rem(y + ((d >> 1) & 1), 2), lax.rem(c + (d & 1), 2)
            peer = 4 * px + 2 * py + pc
            cp = pltpu.make_async_remote_copy(
                src_ref=src_ref if gather else src_ref.at[peer], dst_ref=out_ref.at[me],
                send_sem=send_sems.at[d - 1], recv_sem=recv_sems.at[d - 1],
                device_id=(px, py, pc), device_id_type=pl.DeviceIdType.MESH)
            cp.start()
            copies.append(cp)
        for cp in copies:
            cp.wait()
        local.wait()

    return pl.pallas_call(
        body, name=name, out_shape=jax.ShapeDtypeStruct((N_DEV,) + tuple(slab), src.dtype),
        in_specs=[pl.BlockSpec(memory_space=pl.ANY)], out_specs=pl.BlockSpec(memory_space=pl.ANY),
        scratch_shapes=[pltpu.SemaphoreType.DMA((N_DEV - 1,)), pltpu.SemaphoreType.DMA((N_DEV - 1,)), pltpu.SemaphoreType.DMA],
    )(src)


_HBM = pl.BlockSpec(memory_space=pl.ANY)
_CHIPS = ((0, 0), (0, 1), (1, 0), (1, 1))


def _gather_two_level(name, srcs):
    n = len(srcs)

    def body(*refs):
        src_refs, out_refs = refs[:n], refs[n:2 * n]
        send_sems, recv_sems, local_sems = refs[2 * n:]
        x, y, c = lax.axis_index("x"), lax.axis_index("y"), lax.axis_index("c")
        me, sibling = (x, y, c), (x, y, 1 - c)
        chips = [(1 - x, y), (x, 1 - y), (1 - x, 1 - y)]

        def slab(a, px, py, pc):
            return out_refs[a].at[4 * px + 2 * py + pc]

        def copy(a, k, block, to, own=False):
            return pltpu.make_async_remote_copy(
                src_ref=src_refs[a] if own else slab(a, *block), dst_ref=slab(a, *block), send_sem=send_sems.at[a, k],
                recv_sem=recv_sems.at[a, k], device_id=to, device_id_type=pl.DeviceIdType.MESH)

        mine = [pltpu.make_async_copy(src_refs[a], slab(a, *me), local_sems.at[a]) for a in range(n)]
        first = []
        for a in range(n):
            mine[a].start()
            first += [copy(a, 0, me, sibling, own=True)] + [copy(a, 1 + j, me, (*chip, c), own=True) for j, chip in enumerate(chips)]
        for cp in first:
            cp.start()
        passed = []
        for j, chip in enumerate(chips):
            for a in range(n):
                copy(a, 1 + j, (*chip, c), me).wait_recv()
                passed.append(copy(a, 4 + j, (*chip, c), sibling))
                passed[-1].start()
        for a in range(n):
            copy(a, 0, sibling, me).wait_recv()
        for j, chip in enumerate(chips):
            for a in range(n):
                copy(a, 4 + j, (*chip, 1 - c), me).wait_recv()
        for cp in first + passed:
            cp.wait_send()
        for cp in mine:
            cp.wait()

    return pl.pallas_call(
        body, name=name, out_shape=[jax.ShapeDtypeStruct((N_DEV,) + tuple(s.shape), s.dtype) for s in srcs],
        in_specs=[_HBM] * n, out_specs=[_HBM] * n,
        scratch_shapes=[pltpu.SemaphoreType.DMA((n, N_DEV - 1)), pltpu.SemaphoreType.DMA((n, N_DEV - 1)), pltpu.SemaphoreType.DMA((n,))],
    )(*srcs)


def _scatter_to_sibling(name, parts_list):
    n = len(parts_list)

    def body(*refs):
        p_refs, out_refs = refs[:n], refs[n:2 * n]
        send_sems, recv_sems = refs[2 * n:]
        x, y, c = lax.axis_index("x"), lax.axis_index("y"), lax.axis_index("c")
        copies = []
        for a in range(n):
            for j, (px, py) in enumerate(_CHIPS):
                cp = pltpu.make_async_remote_copy(
                    src_ref=p_refs[a].at[4 * px + 2 * py + (1 - c)], dst_ref=out_refs[a].at[j], send_sem=send_sems.at[a, j],
                    recv_sem=recv_sems.at[a, j], device_id=(x, y, 1 - c), device_id_type=pl.DeviceIdType.MESH)
                cp.start()
                copies.append(cp)
        for cp in copies:
            cp.wait()

    return pl.pallas_call(
        body, name=name, out_shape=[jax.ShapeDtypeStruct((4,) + tuple(p.shape[1:]), p.dtype) for p in parts_list],
        in_specs=[_HBM] * n, out_specs=[_HBM] * n,
        scratch_shapes=[pltpu.SemaphoreType.DMA((n, 4)), pltpu.SemaphoreType.DMA((n, 4))],
    )(*parts_list)


def _chip_sum(name, parts, from_sibling):
    _, rows, cols = parts.shape
    lanes = -(-cols // LANES) * LANES
    tr = _row_tile(rows, 4 * lanes * 4 * 2, budget=12 << 20)

    def kern(p_ref, s_ref, o_ref):
        c = lax.axis_index("c")
        o_ref[0] = (jnp.where(c == 0, p_ref[0, 0], p_ref[0, 1]) + s_ref[0]).astype(o_ref.dtype)

    return pl.pallas_call(
        kern, grid=(4, rows // tr), name=name,
        in_specs=[pl.BlockSpec((1, 2, tr, cols), lambda j, i: (j, 0, i, 0)), pl.BlockSpec((1, tr, cols), lambda j, i: (j, i, 0))],
        out_specs=pl.BlockSpec((1, tr, cols), lambda j, i: (j, i, 0)),
        out_shape=jax.ShapeDtypeStruct((4, rows, cols), BF16),
        compiler_params=_params(4 * lanes * tr * 4 * 2),
    )(parts.reshape(4, 2, rows, cols), from_sibling)


def _scatter_across_chips(name, sums_list):
    n = len(sums_list)

    def body(*refs):
        q_refs, out_refs = refs[:n], refs[n:2 * n]
        send_sems, recv_sems, local_sems = refs[2 * n:]
        x, y, c = lax.axis_index("x"), lax.axis_index("y"), lax.axis_index("c")
        own = 2 * x + y
        copies = []
        for a in range(n):
            local = pltpu.make_async_copy(q_refs[a].at[own], out_refs[a].at[own], local_sems.at[a])
            local.start()
            copies.append(local)
            for d in range(1, 4):
                px, py = lax.rem(x + (d >> 1), 2), lax.rem(y + (d & 1), 2)
                cp = pltpu.make_async_remote_copy(
                    src_ref=q_refs[a].at[2 * px + py], dst_ref=out_refs[a].at[own], send_sem=send_sems.at[a, d - 1],
                    recv_sem=recv_sems.at[a, d - 1], device_id=(px, py, c), device_id_type=pl.DeviceIdType.MESH)
                cp.start()
                copies.append(cp)
        for cp in copies:
            cp.wait()

    return pl.pallas_call(
        body, name=name, out_shape=[jax.ShapeDtypeStruct(s.shape, s.dtype) for s in sums_list],
        in_specs=[_HBM] * n, out_specs=[_HBM] * n,
        scratch_shapes=[pltpu.SemaphoreType.DMA((n, 3)), pltpu.SemaphoreType.DMA((n, 3)), pltpu.SemaphoreType.DMA((n,))],
    )(*sums_list)


def _sum_parts(name, parts):
    n_parts, rows, cols = parts.shape
    tr = _row_tile(rows, 4 * cols * (n_parts + 1) * 2)

    def kern(p_ref, o_ref):
        acc = p_ref[0]
        for k in range(1, n_parts):
            acc = acc + p_ref[k]
        o_ref[...] = acc

    return pl.pallas_call(
        kern, grid=(rows // tr,), name=name, in_specs=[pl.BlockSpec((n_parts, tr, cols), lambda i: (0, i, 0))],
        out_specs=pl.BlockSpec((tr, cols), lambda i: (i, 0)), out_shape=jax.ShapeDtypeStruct((rows, cols), F32),
    )(parts)


def _adamw(name, w, m, v, parts):
    n_parts, rows, cols = parts.shape
    lanes = -(-cols // LANES) * LANES
    tr = _row_tile(rows, 4 * lanes * (n_parts + 7) * 2, budget=16 << 20)
    c1 = np.float32(1.0 - ADAM_B1 ** ADAM_STEP)
    c2 = np.float32(1.0 - ADAM_B2 ** ADAM_STEP)

    def kern(w_ref, m_ref, v_ref, p_ref, g_ref, d_ref, nm_ref, nv_ref):
        g = p_ref[0].astype(F32)
        for k in range(1, n_parts):
            g = g + p_ref[k].astype(F32)
        m_new = ADAM_B1 * m_ref[...] + (1.0 - ADAM_B1) * g
        v_new = ADAM_B2 * v_ref[...] + (1.0 - ADAM_B2) * jnp.square(g)
        g_ref[...] = g
        nm_ref[...] = m_new
        nv_ref[...] = v_new
        d_ref[...] = -ADAM_LR * ((m_new / c1) / (jnp.sqrt(v_new / c2) + ADAM_EPS) + ADAM_WD * w_ref[...])

    spec = pl.BlockSpec((tr, cols), lambda i: (i, 0))
    return pl.pallas_call(
        kern, grid=(rows // tr,), name=name,
        in_specs=[spec, spec, spec, pl.BlockSpec((n_parts, tr, cols), lambda i: (0, i, 0))], out_specs=[spec] * 4,
        out_shape=[jax.ShapeDtypeStruct((rows, cols), F32)] * 4,
        compiler_params=_params(4 * lanes * tr * (n_parts + 7) * 2),
    )(w, m, v, parts)


def _packed_rows(shape):
    return -(-int(np.prod(shape)) // (SUBLANES * LANES)) * SUBLANES


def _pack(arrays, row_multiple):
    parts = []
    for a in arrays:
        n = int(np.prod(a.shape))
        r = _packed_rows(a.shape)
        parts.append(jnp.pad(a.reshape(-1), (0, r * LANES - n)).reshape(r, LANES))
    rows = sum(p.shape[0] for p in parts)
    total = -(-rows // row_multiple) * row_multiple
    if total > rows:
        parts.append(jnp.zeros((total - rows, LANES), arrays[0].dtype))
    return jnp.concatenate(parts, axis=0)


def _unpack(packed, shapes):
    out, off = [], 0
    for s in shapes:
        r = _packed_rows(s)
        out.append(packed[off:off + r].reshape(-1)[:int(np.prod(s))].reshape(s))
        off += r
    return out


def _split_w_in(w_in):
    main = jnp.concatenate([w_in[:, :, s:e] for s, e in _IN_MAIN], axis=2)
    pad = jnp.zeros(w_in.shape[:2] + (LANES - 80,), w_in.dtype)
    small = jnp.concatenate([w_in[:, :, s:e] for s, e in _IN_SMALL] + [pad], axis=2)
    return main, small


def _join_w_in(main, small):
    return jnp.concatenate([main[:, :, 0:8192], small[:, :, 0:64], main[:, :, 8192:12288], small[:, :, 64:80],
                            main[:, :, 12288:15360]], axis=2)


def _unshard(gathered, axis):
    nd, nl, r, c = gathered.shape
    if axis == 1:
        return gathered.transpose(1, 0, 2, 3).reshape(nl, nd * r, c)
    return gathered.transpose(1, 2, 0, 3).reshape(nl, r, nd * c)


def _reshard(full, axis):
    nl, r, c = full.shape
    if axis == 1:
        return full.reshape(nl, N_DEV, r // N_DEV, c).transpose(1, 0, 2, 3)
    return full.reshape(nl, r, N_DEV, c // N_DEV).transpose(2, 0, 1, 3)


def kernel(x, c, ctx, c_ctx, w_ada, b_ada, w_in, lru_conv_w, lru_conv_b, lru_w_r, lru_b_r, lru_w_i, lru_b_i, lru_lambda, ssd_conv_w, ssd_conv_b, ssd_dt_bias, ssd_a_log, ssd_d, ssd_norm_w, ml_conv_w, ml_conv_b, ml_gate_b, ml_norm_w, w_br_a, w_br_b, w_br_c, w_out, b_out, ln1_g, ln1_b, w_ff1, b_ff1, w_ff2, b_ff2, ln2_g, ln2_b, loss_target, m_c_ctx, m_w_ada, m_b_ada, m_w_in, m_lru_conv_w, m_lru_conv_b, m_lru_w_r, m_lru_b_r, m_lru_w_i, m_lru_b_i, m_lru_lambda, m_ssd_conv_w, m_ssd_conv_b, m_ssd_dt_bias, m_ssd_a_log, m_ssd_d, m_ssd_norm_w, m_ml_conv_w, m_ml_conv_b, m_ml_gate_b, m_ml_norm_w, m_w_br_a, m_w_br_b, m_w_br_c, m_w_out, m_b_out, m_ln1_g, m_ln1_b, m_w_ff1, m_b_ff1, m_w_ff2, m_b_ff2, m_ln2_g, m_ln2_b, v_c_ctx, v_w_ada, v_b_ada, v_w_in, v_lru_conv_w, v_lru_conv_b, v_lru_w_r, v_lru_b_r, v_lru_w_i, v_lru_b_i, v_lru_lambda, v_ssd_conv_w, v_ssd_conv_b, v_ssd_dt_bias, v_ssd_a_log, v_ssd_d, v_ssd_norm_w, v_ml_conv_w, v_ml_conv_b, v_ml_gate_b, v_ml_norm_w, v_w_br_a, v_w_br_b, v_w_br_c, v_w_out, v_b_out, v_ln1_g, v_ln1_b, v_w_ff1, v_b_ff1, v_w_ff2, v_b_ff2, v_ln2_g, v_ln2_b):
    a = dict(locals())
    me = 4 * lax.axis_index("x") + 2 * lax.axis_index("y") + lax.axis_index("c")

    wts = {n: a[n] for n in _REPLICATED if n not in ('c_ctx', 'b_ada')}
    exchanged = [n for n in _BIG if n != 'w_ada']
    gathered = _gather_two_level("gather_weights", [a[n].astype(BF16) for n in exchanged])
    for n, g in zip(exchanged, gathered):
        full = _unshard(g, _BIG[n])
        if n == 'w_in':
            main, small = _split_w_in(full)
            wts['w_in_main'], wts['w_in_small'] = main.astype(F32), small.astype(F32)
        else:
            wts[n] = full.astype(F32)
    small_shapes = [a[n].shape for n in _SMALL_SHARDED]
    small_all = _exchange("gather_small", _pack([a[n] for n in _SMALL_SHARDED], SUBLANES), True)
    per_dev = [_unpack(small_all[k], small_shapes) for k in range(N_DEV)]
    for i, n in enumerate(_SMALL_SHARDED):
        wts[n] = jnp.concatenate([per_dev[k][i] for k in range(N_DEV)], axis=-1)

    mods, ada_saved = _ada_forward(c, c_ctx, w_ada, b_ada, me)
    y, vjp = jax.vjp(functools.partial(_forward, ctx=ctx[0]), x[0], wts, mods)
    loss_local, dy = _loss_and_cotangent(y, loss_target[0])
    grad_x, grads, dmods = vjp(dy)
    grads['w_in'] = _join_w_in(grads.pop('w_in_main'), grads.pop('w_in_small'))
    grad_w_ada, grads['b_ada'], grads['c_ctx'] = _ada_backward(ada_saved, dmods)
    loss = lax.psum(loss_local, ("x", "y", "c"))

    out = {}

    def put(n, res, shape):
        for kind, r in zip(("grad_", "delta_", "new_m_", "new_v_"), res):
            out[kind + n] = r.reshape(shape)

    flat = {n: (a[n].shape[0] * a[n].shape[1], a[n].shape[2]) for n in _BIG}
    by_dest = [_reshard(grads[n], _BIG[n]).reshape(N_DEV, *flat[n]) for n in exchanged]
    from_sibling = _scatter_to_sibling("scatter_d2d", by_dest)
    chip_sums = [_chip_sum("chipsum_" + n, p, s) for n, p, s in zip(exchanged, by_dest, from_sibling)]
    summed = dict(zip(exchanged, _scatter_across_chips("scatter_ici", chip_sums)))
    summed['w_ada'] = grad_w_ada.reshape(1, *flat['w_ada'])
    for n in _BIG:
        shp = a[n].shape
        rows, cols = flat[n]
        parts = summed[n]
        put(n, _adamw("adamw_" + n, a[n].reshape(rows, cols), a["m_" + n].reshape(rows, cols), a["v_" + n].reshape(rows, cols), parts), shp)

    rep_names = _REPLICATED + _SMALL_SHARDED
    chunk_rows = SUBLANES * N_DEV
    g_pack = _pack([grads[n] for n in rep_names], chunk_rows * N_DEV)
    rows = g_pack.shape[0]
    parts = _exchange("scatter_rep", g_pack.reshape(N_DEV, rows // N_DEV, LANES), False)
    mine = _sum_parts("sum_rep", parts)
    g_all = _exchange("gather_rep", mine, True).reshape(rows, LANES)
    g_full = _unpack(g_all, [grads[n].shape for n in rep_names])
    g_local = []
    for n, g in zip(rep_names, g_full):
        if n in _SMALL_SHARDED:
            width = a[n].shape[-1]
            g = lax.dynamic_slice_in_dim(g, me * width, width, axis=g.ndim - 1)
        g_local.append(g)
    shapes = [a[n].shape for n in rep_names]
    res = _adamw("adamw_rep", _pack([a[n] for n in rep_names], chunk_rows), _pack([a["m_" + n] for n in rep_names], chunk_rows),
                 _pack([a["v_" + n] for n in rep_names], chunk_rows), _pack(g_local, chunk_rows)[None])
    unpacked = [_unpack(r, shapes) for r in res]
    for i, n in enumerate(rep_names):
        put(n, [u[i] for u in unpacked], shapes[i])

    outs = [loss, grad_x[None]]
    for kind in ("grad_", "delta_", "new_m_", "new_v_"):
        outs += [out[kind + n] for n in _WEIGHTS]
    return tuple(outs)
```

```python
import functools

import numpy as np
import jax
import jax.numpy as jnp
from jax import lax
from jax.experimental import pallas as pl
from jax.experimental.pallas import tpu as pltpu

F32 = jnp.float32
BF16 = jnp.bfloat16

N_DEV = 8
D_MODEL = 1024
DEPTH = 2
GRID_W = 64
CHUNK = 128
LN_EPS = 1e-6
LRU_BLOCKS = 8
LRU_BS = 128
LRU_C = 8.0
SSD_INNER = 2048
SSD_GROUPS = 8
SSD_HPG = 4
SSD_HEADDIM = 64
SSD_STATE = 128
ML_HEADS = 4
ML_HD = 256
D_FF = 4096
DN_ALPHA = (2 * DEPTH) ** 0.25
ADAM_LR, ADAM_B1, ADAM_B2, ADAM_EPS, ADAM_WD, ADAM_STEP = 0.001, 0.9, 0.999, 1e-08, 0.01, 10

VMEM_CAP = 60 * 1024 * 1024
SUBLANES = 8
LANES = 128

_IN_MAIN = ((0, 8192), (8256, 12352), (12368, 15440))
_IN_MAIN_WIDTHS = (1024, 1024, 2048, 2048, 1024, 1024, 1024, 1024, 1024, 1024, 1024, 1024, 1024)
_IN_SMALL = ((8192, 8256), (12352, 12368))
_DT_LANE = 0
_MG_LANE = 64

_WEIGHTS = ['c_ctx', 'w_ada', 'b_ada', 'w_in', 'lru_conv_w', 'lru_conv_b', 'lru_w_r', 'lru_b_r', 'lru_w_i', 'lru_b_i',
            'lru_lambda', 'ssd_conv_w', 'ssd_conv_b', 'ssd_dt_bias', 'ssd_a_log', 'ssd_d', 'ssd_norm_w', 'ml_conv_w',
            'ml_conv_b', 'ml_gate_b', 'ml_norm_w', 'w_br_a', 'w_br_b', 'w_br_c', 'w_out', 'b_out', 'ln1_g', 'ln1_b',
            'w_ff1', 'b_ff1', 'w_ff2', 'b_ff2', 'ln2_g', 'ln2_b']
_BIG = {'w_ada': 2, 'w_in': 2, 'w_ff1': 2, 'w_br_a': 1, 'w_br_b': 1, 'w_br_c': 1, 'w_out': 1, 'w_ff2': 1}
_SMALL_SHARDED = ['lru_conv_w', 'lru_b_r', 'lru_b_i', 'lru_lambda', 'ssd_conv_w', 'ml_conv_w']
_REPLICATED = [n for n in _WEIGHTS if n not in _BIG and n not in _SMALL_SHARDED]


def _params(vmem_bytes):
    return pltpu.CompilerParams(vmem_limit_bytes=int(min(max(2 * vmem_bytes, 32 << 20), VMEM_CAP)))


def _row_tile(n_rows, bytes_per_row, budget=6 << 20, cap=512):
    t = cap
    while t > SUBLANES and (t * bytes_per_row > budget or n_rows % t):
        t //= 2
    assert n_rows % t == 0, (n_rows, t)
    return t


def _dg(a, b, ca, cb):
    return lax.dot_general(a.astype(BF16), b.astype(BF16), (((ca,), (cb,)), ((), ())), preferred_element_type=F32)


def _make_bdot(ca, cb):
    @jax.custom_vjp
    def f(a, b):
        return _dg(a, b, ca, cb)

    def fwd(a, b):
        return _dg(a, b, ca, cb), (a, b)

    def bwd(res, g):
        a, b = res
        da = _dg(g, b, 1, 1 - cb) if ca == 1 else _dg(b, g, 1 - cb, 1)
        db = _dg(a, g, 1 - ca, 0) if cb == 0 else _dg(g, a, 0, 1 - ca)
        return da, db

    f.defvjp(fwd, bwd)
    return f


_mm_nn = _make_bdot(1, 0)
_mm_nt = _make_bdot(1, 1)
_mm_tn = _make_bdot(0, 0)


@jax.custom_vjp
def _round_bf16(x):
    return x.astype(BF16).astype(F32)


_round_bf16.defvjp(lambda x: (_round_bf16(x), None), lambda _, g: (g,))


def _exact_dot(a, b):
    return jnp.dot(a, b, precision=lax.Precision.HIGHEST, preferred_element_type=F32)


def _layernorm_rows(x):
    mu = jnp.mean(x, -1, keepdims=True)
    var = jnp.mean(jnp.square(x - mu), -1, keepdims=True)
    return (x - mu) * lax.rsqrt(var + LN_EPS)


def _rowwise_calls(name, f, rows, params, out_widths, out_dtype=F32, to_linear=()):
    drow_dtypes = [BF16 if i in to_linear else F32 for i in range(len(rows))]
    nr, npar, no = len(rows), len(params), len(out_widths)
    n_rows = rows[0].shape[0]
    row_w = [r.shape[1] for r in rows]
    par_bytes = sum(int(np.prod(p.shape)) * 4 for p in params)
    tile = _row_tile(n_rows, 4 * (2 * sum(row_w) + 2 * sum(out_widths)), budget=12 << 20)
    grid = (n_rows // tile,)

    def row_spec(w):
        return pl.BlockSpec((tile, w), lambda i: (i, 0))

    def par_spec(p):
        return pl.BlockSpec(p.shape, lambda i: (0, 0))

    vmem = 2 * tile * 4 * (2 * sum(row_w) + 3 * sum(out_widths)) + 4 * par_bytes

    def fwd_call(rows, params):
        def kern(*refs):
            outs = f(*[r[...] for r in refs[:nr + npar]])
            for r, o in zip(refs[nr + npar:], outs):
                r[...] = o.astype(out_dtype)

        return pl.pallas_call(
            kern, grid=grid, name=name + "_fwd",
            in_specs=[row_spec(w) for w in row_w] + [par_spec(p) for p in params],
            out_specs=[row_spec(w) for w in out_widths],
            out_shape=[jax.ShapeDtypeStruct((n_rows, w), out_dtype) for w in out_widths],
            compiler_params=_params(vmem),
        )(*rows, *params)

    def bwd_call(rows, params, gouts):
        def kern(*refs):
            ins = [r[...] for r in refs[:nr + npar]]
            gs = tuple(r[...] for r in refs[nr + npar:nr + npar + no])
            grads = jax.vjp(f, *ins)[1](gs)
            drefs = refs[nr + npar + no:]
            for k in range(nr):
                drefs[k][...] = grads[k].astype(drefs[k].dtype)

            @pl.when(pl.program_id(0) == 0)
            def _():
                for k in range(npar):
                    drefs[nr + k][...] = jnp.zeros_like(drefs[nr + k])

            for k in range(npar):
                drefs[nr + k][...] += grads[nr + k]

        res = pl.pallas_call(
            kern, grid=grid, name=name + "_bwd",
            in_specs=[row_spec(w) for w in row_w] + [par_spec(p) for p in params] + [row_spec(w) for w in out_widths],
            out_specs=[row_spec(w) for w in row_w] + [par_spec(p) for p in params],
            out_shape=[jax.ShapeDtypeStruct(r.shape, dt) for r, dt in zip(rows, drow_dtypes)]
            + [jax.ShapeDtypeStruct(p.shape, F32) for p in params],
            compiler_params=_params(vmem),
        )(*rows, *params, *gouts)
        return tuple(r.astype(F32) for r in res[:nr]), tuple(res[nr:])

    return fwd_call, bwd_call


def _rowwise(name, f, rows, params, out_widths, to_linear=()):
    rows, params = tuple(rows), tuple(params)
    fwd_call, bwd_call = _rowwise_calls(name, f, rows, params, out_widths, to_linear=to_linear)

    @jax.custom_vjp
    def op(rows, params):
        return tuple(fwd_call(rows, params))

    op.defvjp(lambda r, p: (tuple(fwd_call(r, p)), (r, p)), lambda res, g: bwd_call(res[0], res[1], g))
    return op(rows, params)


def _rowwise_linear(name, f, rows, params, weights, to_linear=()):
    rows, params = tuple(rows), tuple(params)
    ws = tuple(w for w, _ in weights)
    m, k = rows[0].shape[0], ws[0].shape[0]
    row_fwd, row_bwd = _rowwise_calls(name, f, rows, params, [k], BF16, to_linear)
    lin = [_linear_calls(name + "lin%d" % i, m, k, w.shape[1], wd, BF16) for i, (w, wd) in enumerate(weights)]
    counts = [len(c[3]) for c in lin]

    def fwd(rows, params, ws):
        (a,) = row_fwd(rows, params)
        outs = []
        for (fwd_call, _, _, _), w in zip(lin, ws):
            outs += list(fwd_call(a, w))
        return tuple(outs), a

    @jax.custom_vjp
    def op(rows, params, ws):
        return fwd(rows, params, ws)[0]

    def op_fwd(rows, params, ws):
        outs, a = fwd(rows, params, ws)
        return outs, (rows, params, ws, a)

    def op_bwd(res, g):
        rows, params, ws, a = res
        da, dws, off = None, [], 0
        for (_, dgrad_call, wgrad_call, _), w, cnt in zip(lin, ws, counts):
            gk = g[off:off + cnt]
            off += cnt
            d = dgrad_call(w, gk)
            da = d if da is None else da + d
            dws.append(wgrad_call(a, gk))
        drows, dparams = row_bwd(rows, params, (da,))
        return drows, dparams, tuple(dws)

    op.defvjp(op_fwd, op_bwd)
    return op(rows, params, ws)


def _group_ranges(widths, tn):
    starts, s = [], 0
    for w in widths:
        assert w % tn == 0, (w, tn)
        starts.append((s // tn, (s + w) // tn))
        s += w
    return starts, s // tn


def _group_tile(refs, ranges, row_tile, col_tile, i, j):
    out = []
    for ref, (s, e) in zip(refs, ranges):
        cols = pl.ds(pl.multiple_of((j - s) * col_tile, col_tile), col_tile)
        out.append(((j >= s) & (j < e), ref, cols))
    return [(p, lambda r=r, c=c: r.at[pl.ds(pl.multiple_of(i * row_tile, row_tile), row_tile), c]) for p, r, c in out]


def _linear_calls(name, m, k, n, widths, a_dtype):
    widths = (n,) if widths is None else tuple(widths)
    ng = len(widths)
    cast_a = a_dtype != BF16
    wide = all(wd % 1024 == 0 for wd in widths)
    tn = 128 if n < 256 else (1024 if wide else (256 if k > 2048 or n % 512 else 512))
    tm = _row_tile(m, 0, cap=1024 if k <= 2048 else 512)
    ranges, nt = _group_ranges(widths, tn)
    mt = m // tm
    tn_w = 1024 if (wide and k <= 1024) else (512 if all(wd % 512 == 0 for wd in widths) else min(tn, 256))
    tm_w = _row_tile(m, 0, cap=1024 if k <= 1024 else 512)
    ranges_w, nt_w = _group_ranges(widths, tn_w)
    mt_w = m // tm_w
    hbm = pl.BlockSpec(memory_space=pl.ANY)

    def fwd_call(a, w):
        n_steps = mt * nt

        def kern(a_ref, w_ref, *rest):
            outs, obuf, osem = rest[:ng], rest[ng], rest[ng + 1]
            a_bf = rest[ng + 2] if cast_a else a_ref
            i, j = pl.program_id(0), pl.program_id(1)
            step = i * nt + j
            slot = lax.rem(step, 2)

            def drain(sl):
                pltpu.make_async_copy(obuf.at[sl], outs[0].at[pl.ds(0, tm), pl.ds(0, tn)], osem.at[sl]).wait()

            if cast_a:
                @pl.when(j == 0)
                def _():
                    a_bf[...] = a_ref[...].astype(BF16)

            @pl.when(step >= 2)
            def _():
                drain(slot)

            obuf[slot] = jnp.dot(a_bf[...], w_ref[...], preferred_element_type=F32)
            for pred, window in _group_tile(outs, ranges, tm, tn, i, j):
                @pl.when(pred)
                def _(window=window):
                    pltpu.make_async_copy(obuf.at[slot], window(), osem.at[slot]).start()

            @pl.when(step == n_steps - 1)
            def _():
                drain(slot)
                if n_steps > 1:
                    drain(1 - slot)

        return pl.pallas_call(
            kern, grid=(mt, nt), name=name + "_fwd",
            in_specs=[pl.BlockSpec((tm, k), lambda i, j: (i, 0)), pl.BlockSpec((k, tn), lambda i, j: (0, j))],
            out_specs=[hbm] * ng,
            out_shape=[jax.ShapeDtypeStruct((m, wd), F32) for wd in widths],
            scratch_shapes=[pltpu.VMEM((2, tm, tn), F32), pltpu.SemaphoreType.DMA((2,))]
            + ([pltpu.VMEM((tm, k), BF16)] if cast_a else []),
            compiler_params=_params(10 * tm * k + 4 * k * tn + 8 * tm * tn),
        )(a, w.astype(BF16))

    def prefetched(gs, gbuf, gsem, rngs, row_tile, col_tile, step, n_steps, tile_of):
        slot = lax.rem(step, 2)

        def start(s_idx, sl):
            ii, jj = tile_of(s_idx)
            for pred, window in _group_tile(gs, rngs, row_tile, col_tile, ii, jj):
                @pl.when(pred)
                def _(window=window):
                    pltpu.make_async_copy(window(), gbuf.at[sl], gsem.at[sl]).start()

        @pl.when(step == 0)
        def _():
            start(step, slot)

        @pl.when(step + 1 < n_steps)
        def _():
            start(step + 1, 1 - slot)

        pltpu.make_async_copy(gs[0].at[pl.ds(0, row_tile), pl.ds(0, col_tile)], gbuf.at[slot], gsem.at[slot]).wait()
        return slot

    def dgrad_call(w, gouts):
        def kern(w_ref, *rest):
            gs, da, gbuf, gsem = rest[:ng], rest[ng], rest[ng + 1], rest[ng + 2]
            i, j = pl.program_id(0), pl.program_id(1)
            slot = prefetched(gs, gbuf, gsem, ranges, tm, tn, i * nt + j, mt * nt, lambda s: (s // nt, lax.rem(s, nt)))

            @pl.when(j == 0)
            def _():
                da[...] = jnp.zeros_like(da)

            da[...] += lax.dot_general(gbuf[slot].astype(BF16), w_ref[...], (((1,), (1,)), ((), ())), preferred_element_type=F32)

        return pl.pallas_call(
            kern, grid=(mt, nt), name=name + "_dgrad",
            in_specs=[pl.BlockSpec((k, tn), lambda i, j: (0, j))] + [hbm] * ng,
            out_specs=pl.BlockSpec((tm, k), lambda i, j: (i, 0)),
            out_shape=jax.ShapeDtypeStruct((m, k), F32),
            scratch_shapes=[pltpu.VMEM((2, tm, tn), BF16), pltpu.SemaphoreType.DMA((2,))],
            compiler_params=_params(12 * tm * k + 4 * k * tn + 10 * tm * tn),
        )(w.astype(BF16), *[g.astype(BF16) for g in gouts])

    def wgrad_call(a, gouts):
        def kern(a_ref, *rest):
            gs, dw, gbuf, gsem = rest[:ng], rest[ng], rest[ng + 1], rest[ng + 2]
            j, i = pl.program_id(0), pl.program_id(1)
            slot = prefetched(gs, gbuf, gsem, ranges_w, tm_w, tn_w, j * mt_w + i, mt_w * nt_w,
                              lambda s: (lax.rem(s, mt_w), s // mt_w))

            @pl.when(i == 0)
            def _():
                dw[...] = jnp.zeros_like(dw)

            dw[...] += lax.dot_general(a_ref[...].astype(BF16), gbuf[slot].astype(BF16), (((0,), (0,)), ((), ())),
                                       preferred_element_type=F32)

        return pl.pallas_call(
            kern, grid=(nt_w, mt_w), name=name + "_wgrad",
            in_specs=[pl.BlockSpec((tm_w, k), lambda j, i: (i, 0))] + [hbm] * ng,
            out_specs=pl.BlockSpec((k, tn_w), lambda j, i: (0, j)),
            out_shape=jax.ShapeDtypeStruct((k, n), F32),
            scratch_shapes=[pltpu.VMEM((2, tm_w, tn_w), BF16), pltpu.SemaphoreType.DMA((2,))],
            compiler_params=_params(12 * tm_w * k + 12 * k * tn_w + 10 * tm_w * tn_w),
        )(a, *[g.astype(BF16) for g in gouts])

    return fwd_call, dgrad_call, wgrad_call, widths


def _linear(name, a, w, widths=None):
    fwd_call, dgrad_call, wgrad_call, _ = _linear_calls(name, a.shape[0], a.shape[1], w.shape[1], widths, a.dtype)

    @jax.custom_vjp
    def op(a, w):
        return tuple(fwd_call(a, w))

    op.defvjp(lambda a, w: (tuple(fwd_call(a, w)), (a, w)),
              lambda res, g: (dgrad_call(res[1], g), wgrad_call(res[0], g)))
    out = op(a, w)
    return out[0] if widths is None else out


def _conv_taps(x_ext, w, n_ext):
    xm2 = pltpu.roll(x_ext, 2, 0)
    xm1 = pltpu.roll(x_ext, 1, 0)
    xp1 = pltpu.roll(x_ext, n_ext - 1, 0)
    return xm2, xm1, xp1


def _dwconv(name, x, w, b, act):
    n_rows, ch = x.shape
    tt = _row_tile(n_rows, 4 * 6 * ch, budget=12 << 20, cap=256)
    nt = n_rows // tt
    n_ext = tt + 2 * SUBLANES
    per8 = tt // SUBLANES
    last8 = n_rows // SUBLANES - 1
    main = pl.BlockSpec((tt, ch), lambda i: (i, 0))
    prev = pl.BlockSpec((SUBLANES, ch), lambda i: (jnp.maximum(i * per8 - 1, 0), 0))
    nxt = pl.BlockSpec((SUBLANES, ch), lambda i: (jnp.minimum((i + 1) * per8, last8), 0))
    wspec = pl.BlockSpec((4, ch), lambda i: (0, 0))
    bspec = pl.BlockSpec((1, ch), lambda i: (0, 0))
    vmem = 4 * n_ext * ch * 14

    def ext(main_ref, prev_ref, next_ref):
        i = pl.program_id(0)
        p = jnp.where(i > 0, prev_ref[...], 0.0)
        q = jnp.where(i < nt - 1, next_ref[...], 0.0)
        return jnp.concatenate([p, main_ref[...], q], axis=0)

    def pre_of(x_ext, wv, bv):
        xm2, xm1, xp1 = _conv_taps(x_ext, wv, n_ext)
        pre = wv[0:1] * xm2 + wv[1:2] * xm1 + wv[2:3] * x_ext + wv[3:4] * xp1 + bv
        return pre, (xm2, xm1, xp1)

    def fwd_call(x, w, b):
        def kern(xm, xp, xn, w_ref, b_ref, o_ref):
            pre, _ = pre_of(ext(xm, xp, xn), w_ref[...], b_ref[...])
            pre = pre[SUBLANES:SUBLANES + tt]
            o_ref[...] = pre * jax.nn.sigmoid(pre) if act else pre

        return pl.pallas_call(
            kern, grid=(nt,), name=name + "_fwd", in_specs=[main, prev, nxt, wspec, bspec], out_specs=main,
            out_shape=jax.ShapeDtypeStruct((n_rows, ch), F32), compiler_params=_params(vmem),
        )(x, x, x, w, b)

    def bwd_call(x, w, b, dy):
        def kern(xm, xp, xn, gm, gp, gn, w_ref, b_ref, dx_ref, dw_ref, db_ref):
            wv = w_ref[...]
            x_ext = ext(xm, xp, xn)
            pre, (xm2, xm1, xp1) = pre_of(x_ext, wv, b_ref[...])
            dpre = ext(gm, gp, gn)
            if act:
                sg = jax.nn.sigmoid(pre)
                dpre = dpre * (sg + pre * sg * (1.0 - sg))
            dx = (wv[0:1] * pltpu.roll(dpre, n_ext - 2, 0) + wv[1:2] * pltpu.roll(dpre, n_ext - 1, 0)
                  + wv[2:3] * dpre + wv[3:4] * pltpu.roll(dpre, 1, 0))
            sl = slice(SUBLANES, SUBLANES + tt)
            dx_ref[...] = dx[sl].astype(dx_ref.dtype)
            dm = dpre[sl]

            @pl.when(pl.program_id(0) == 0)
            def _():
                dw_ref[...] = jnp.zeros_like(dw_ref)
                db_ref[...] = jnp.zeros_like(db_ref)

            dw_ref[...] += jnp.concatenate(
                [jnp.sum(dm * t[sl], axis=0, keepdims=True) for t in (xm2, xm1, x_ext, xp1)], axis=0)
            db_ref[...] += jnp.sum(dm, axis=0, keepdims=True)

        return pl.pallas_call(
            kern, grid=(nt,), name=name + "_bwd", in_specs=[main, prev, nxt, main, prev, nxt, wspec, bspec],
            out_specs=[main, wspec, bspec],
            out_shape=[jax.ShapeDtypeStruct((n_rows, ch), BF16), jax.ShapeDtypeStruct((4, ch), F32),
                       jax.ShapeDtypeStruct((1, ch), F32)],
            compiler_params=_params(vmem),
        )(x, x, x, dy, dy, dy, w, b)

    @jax.custom_vjp
    def op(x, w, b):
        return fwd_call(x, w, b)

    def op_bwd(res, g):
        dx, dw, db = bwd_call(*res, g)
        return dx.astype(F32), dw, db

    op.defvjp(lambda x, w, b: (fwd_call(x, w, b), (x, w, b)), op_bwd)
    return op(x, w, b)


def _scan_groups(tt, ch, reverse, load, store, carry_ref):
    row = lax.broadcasted_iota(jnp.int32, (SUBLANES, ch), 0)
    ng = tt // SUBLANES

    def body(k, carry):
        g = (ng - 1 - k) if reverse else k
        sl = pl.ds(pl.multiple_of(g * SUBLANES, SUBLANES), SUBLANES)
        a, b, extra = load(sl)
        for s in (1, 2, 4):
            if reverse:
                a_sh, b_sh, valid = pltpu.roll(a, SUBLANES - s, 0), pltpu.roll(b, SUBLANES - s, 0), row < SUBLANES - s
            else:
                a_sh, b_sh, valid = pltpu.roll(a, s, 0), pltpu.roll(b, s, 0), row >= s
            b = jnp.where(valid, b + a * b_sh, b)
            a = jnp.where(valid, a * a_sh, a)
        h = b + a * carry
        if reverse:
            h_prev = jnp.where(row == SUBLANES - 1, carry, pltpu.roll(h, SUBLANES - 1, 0))
            last = h[0:1]
        else:
            h_prev = jnp.where(row == 0, carry, pltpu.roll(h, 1, 0))
            last = h[SUBLANES - 1:SUBLANES]
        store(sl, h, h_prev, extra)
        return jnp.broadcast_to(last, (SUBLANES, ch))

    carry_ref[...] = lax.fori_loop(0, ng, body, carry_ref[...])


def _lin_scan(name, a, b, h0, reverse):
    n_rows, ch = a.shape
    tt = _row_tile(n_rows, 0, cap=256)
    nt = n_rows // tt
    vmem = 2 * 4 * tt * ch * 5

    def tile_spec(rev):
        return pl.BlockSpec((tt, ch), (lambda i: (nt - 1 - i, 0)) if rev else (lambda i: (i, 0)))

    vec = pl.BlockSpec((1, ch), lambda i: (0, 0))

    def fwd_call(a, b, h0):
        def kern(a_ref, b_ref, h0_ref, h_ref, hp_ref, last_ref, carry):
            @pl.when(pl.program_id(0) == 0)
            def _():
                carry[...] = jnp.broadcast_to(h0_ref[...], carry.shape)

            def load(sl):
                return a_ref[sl, :], b_ref[sl, :], None

            def store(sl, h, h_prev, _):
                h_ref[sl, :] = h
                hp_ref[sl, :] = h_prev

            _scan_groups(tt, ch, reverse, load, store, carry)
            last_ref[...] = carry[0:1]

        return pl.pallas_call(
            kern, grid=(nt,), name=name + "_fwd", in_specs=[tile_spec(reverse), tile_spec(reverse), vec],
            out_specs=[tile_spec(reverse), tile_spec(reverse), vec],
            out_shape=[jax.ShapeDtypeStruct((n_rows, ch), F32)] * 2 + [jax.ShapeDtypeStruct((1, ch), F32)],
            scratch_shapes=[pltpu.VMEM((SUBLANES, ch), F32)], compiler_params=_params(vmem),
        )(a, b, h0)

    def bwd_call(a, h_prev, dh, dlast):
        rev = not reverse

        def kern(a_ref, hp_ref, dh_ref, dl_ref, da_ref, db_ref, d0_ref, carry):
            @pl.when(pl.program_id(0) == 0)
            def _():
                carry[...] = jnp.broadcast_to(dl_ref[...], carry.shape)

            def load(sl):
                av, dv = a_ref[sl, :], dh_ref[sl, :]
                return av, av * dv, dv

            def store(sl, u, u_next, dv):
                g = dv + u_next
                db_ref[sl, :] = g
                da_ref[sl, :] = g * hp_ref[sl, :]

            _scan_groups(tt, ch, rev, load, store, carry)
            d0_ref[...] = carry[0:1]

        return pl.pallas_call(
            kern, grid=(nt,), name=name + "_bwd", in_specs=[tile_spec(rev)] * 3 + [vec],
            out_specs=[tile_spec(rev), tile_spec(rev), vec],
            out_shape=[jax.ShapeDtypeStruct((n_rows, ch), F32)] * 2 + [jax.ShapeDtypeStruct((1, ch), F32)],
            scratch_shapes=[pltpu.VMEM((SUBLANES, ch), F32)], compiler_params=_params(vmem),
        )(a, h_prev, dh, dlast)

    @jax.custom_vjp
    def op(a, b, h0):
        h, _, last = fwd_call(a, b, h0)
        return h, last

    def op_fwd(a, b, h0):
        h, h_prev, last = fwd_call(a, b, h0)
        return (h, last), (a, h_prev)

    def op_bwd(res, g):
        da, db, d0 = bwd_call(res[0], res[1], g[0], g[1])
        return da, db, d0

    op.defvjp(op_fwd, op_bwd)
    return op(a, b, h0)


def _tri(reverse):
    q = lax.broadcasted_iota(jnp.int32, (CHUNK, CHUNK), 0)
    s = lax.broadcasted_iota(jnp.int32, (CHUNK, CHUNK), 1)
    return (q <= s) if reverse else (q >= s)


def _pick_col(x, lane):
    idx = lax.broadcasted_iota(jnp.int32, x.shape, 1)
    return jnp.sum(jnp.where(idx == lane, x, 0.0), axis=1, keepdims=True)


def _pick_row(x, row):
    idx = lax.broadcasted_iota(jnp.int32, x.shape, 0)
    return jnp.sum(jnp.where(idx == row, x, 0.0), axis=0, keepdims=True)


def _ssd_shared(small, bias_row, alog_row, reverse):
    delta_all = jax.nn.softplus(small + bias_row)
    acs_all = _exact_dot(_tri(reverse).astype(F32), delta_all * (-jnp.exp(alog_row)))
    return delta_all, acs_all, acs_all.T


def _ssd_group(xs, bm, cm, state, delta_all, acs_all, acs_t, g, direction, reverse):
    mask = _tri(reverse)
    last = 0 if reverse else CHUNK - 1
    hd = SSD_HEADDIM
    rowi = lax.broadcasted_iota(jnp.int32, (CHUNK, 1), 0)
    a_cols, a_rows, deltas, tots = [], [], [], []
    for r in range(SSD_HPG):
        lane = _DT_LANE + 32 * direction + SSD_HPG * g + r
        a_col = _pick_col(acs_all, lane)
        a_cols.append(a_col)
        a_rows.append(_pick_row(acs_t, lane))
        deltas.append(_pick_col(delta_all, lane))
        tots.append(jnp.sum(jnp.where(rowi == last, a_col, 0.0), axis=0, keepdims=True))

    def wide(cols, rows):
        return jnp.concatenate([jnp.broadcast_to(c, (rows, hd)) for c in cols], axis=1)

    a_w = wide(a_cols, CHUNK)
    x_w = xs * wide(deltas, CHUNK)
    st = _mm_tn(x_w * jnp.exp(wide(tots, 1) - a_w), bm)
    y_off = _mm_nt(cm, state) * jnp.exp(a_w)
    grow = jnp.concatenate([jnp.broadcast_to(jnp.exp(t), (hd, 1)) for t in tots], axis=0)
    cb = _mm_nt(cm, bm)
    m_cat = jnp.concatenate([cb * jnp.exp(jnp.where(mask, a_cols[r] - a_rows[r], -jnp.inf)) for r in range(SSD_HPG)], axis=1)
    lane_head = lax.broadcasted_iota(jnp.int32, (1, SSD_HPG * hd), 1) // hd
    x_bd = jnp.concatenate([jnp.where(lane_head == r, x_w, 0.0) for r in range(SSD_HPG)], axis=0)
    return _mm_nn(m_cat, x_bd) + y_off, grow * state + st


def _ssd_calls(name, xs, bm, cm, s0, direction, reverse):
    n_rows = xs.shape[0]
    nc = n_rows // CHUNK
    gw = SSD_HPG * SSD_HEADDIM
    vmem = 4 * CHUNK * (gw + 3 * 128) * 8 + 4 * gw * 128 * 12 + (8 << 20)

    n_state = SSD_GROUPS * gw
    shared_scratch = [pltpu.VMEM((CHUNK, LANES), F32), pltpu.VMEM((CHUNK, LANES), F32), pltpu.VMEM((LANES, CHUNK), F32)]

    def specs(order, gps=1):
        def cidx(c):
            return (nc - 1 - c) if order else c

        return dict(
            xs=pl.BlockSpec((CHUNK, gps * gw), lambda c, g: (cidx(c), g)),
            bc=pl.BlockSpec((CHUNK, gps * SSD_STATE), lambda c, g: (cidx(c), g)),
            small=pl.BlockSpec((CHUNK, LANES), lambda c, g: (cidx(c), 0)),
            row=pl.BlockSpec((1, LANES), lambda c, g: (0, 0)),
            state=pl.BlockSpec((n_state, SSD_STATE), lambda c, g: (0, 0)),
            enter=pl.BlockSpec((1, gps * gw, SSD_STATE), lambda c, g: (cidx(c), g, 0)),
        )

    gps_fwd, gps_bwd = 2, 1

    def group_rows(g, gps):
        return pl.ds(pl.multiple_of(g * gps * gw, gps * gw), gps * gw)

    def step_fn(g, gps):
        def fn(xs_v, bm_v, cm_v, st_v, d_all, a_all, a_t):
            ys, sts = [], []
            for u in range(gps):
                y_u, s_u = _ssd_group(xs_v[:, u * gw:(u + 1) * gw], bm_v[:, u * SSD_STATE:(u + 1) * SSD_STATE],
                                      cm_v[:, u * SSD_STATE:(u + 1) * SSD_STATE], st_v[u * gw:(u + 1) * gw],
                                      d_all, a_all, a_t, gps * g + u, direction, reverse)
                ys.append(y_u)
                sts.append(s_u)
            return jnp.concatenate(ys, axis=1), jnp.concatenate(sts, axis=0)

        return fn

    def fwd_call(xs, bm, cm, small, bias_row, alog_row, s0):
        gps = gps_fwd
        sp = specs(reverse, gps)

        def kern(xs_r, bm_r, cm_r, sm_r, br_r, ar_r, s0_r, y_r, sf_r, se_r, st, sh_d, sh_a, sh_t):
            c, g = pl.program_id(0), pl.program_id(1)

            @pl.when((c == 0) & (g == 0))
            def _():
                st[...] = s0_r[...]

            @pl.when(g == 0)
            def _():
                sh_d[...], sh_a[...], sh_t[...] = _ssd_shared(sm_r[...], br_r[...], ar_r[...], reverse)

            rows = group_rows(g, gps)
            s_in = st[rows, :]
            se_r[0] = s_in
            y_r[...], s_new = step_fn(g, gps)(xs_r[...], bm_r[...], cm_r[...], s_in, sh_d[...], sh_a[...], sh_t[...])
            st[rows, :] = s_new
            sf_r[rows, :] = s_new

        return pl.pallas_call(
            kern, grid=(nc, SSD_GROUPS // gps), name=name + "_fwd",
            in_specs=[sp['xs'], sp['bc'], sp['bc'], sp['small'], sp['row'], sp['row'], sp['state']],
            out_specs=[sp['xs'], sp['state'], sp['enter']],
            out_shape=[jax.ShapeDtypeStruct((n_rows, SSD_INNER), F32), jax.ShapeDtypeStruct((n_state, SSD_STATE), F32),
                       jax.ShapeDtypeStruct((nc, n_state, SSD_STATE), F32)],
            scratch_shapes=[pltpu.VMEM((n_state, SSD_STATE), F32)] + shared_scratch, compiler_params=_params(vmem),
        )(xs, bm, cm, small, bias_row, alog_row, s0)

    def bwd_call(xs, bm, cm, small, bias_row, alog_row, enter, dy, dsf, acc=()):
        gps = gps_bwd
        sp = specs(not reverse, gps)

        def kern(*refs):
            xs_r, bm_r, cm_r, sm_r, br_r, ar_r, se_r, dy_r, dsf_r = refs[:9]
            acc_r = refs[9:9 + len(acc)]
            dxs_r, dbm_r, dcm_r, dsm_r, dbr_r, dar_r, ds0_r, ds, sh_d, sh_a, sh_t, gd, ga, gt = refs[9 + len(acc):]
            c, g = pl.program_id(0), pl.program_id(1)

            @pl.when((c == 0) & (g == 0))
            def _():
                ds[...] = dsf_r[...]
                dbr_r[...] = jnp.zeros_like(dbr_r)
                dar_r[...] = jnp.zeros_like(dar_r)

            @pl.when(g == 0)
            def _():
                sh_d[...], sh_a[...], sh_t[...] = _ssd_shared(sm_r[...], br_r[...], ar_r[...], reverse)
                gd[...] = jnp.zeros_like(gd)
                ga[...] = jnp.zeros_like(ga)
                gt[...] = jnp.zeros_like(gt)

            rows = group_rows(g, gps)
            _, vjp = jax.vjp(step_fn(g, gps), xs_r[...], bm_r[...], cm_r[...], se_r[0], sh_d[...], sh_a[...], sh_t[...])
            dxs, dbm, dcm, ds_in, dd, da, dt = vjp((dy_r[...], ds[rows, :]))
            dxs_r[...] = dxs + acc_r[0][...] if acc else dxs
            dbm_r[...] = dbm + acc_r[1][...] if acc else dbm
            dcm_r[...] = dcm + acc_r[2][...] if acc else dcm
            ds[rows, :] = ds_in
            ds0_r[rows, :] = ds_in
            gd[...] += dd
            ga[...] += da
            gt[...] += dt

            @pl.when(g == SSD_GROUPS // gps - 1)
            def _():
                shared = functools.partial(_ssd_shared, reverse=reverse)
                dsm, dbr, dar = jax.vjp(shared, sm_r[...], br_r[...], ar_r[...])[1]((gd[...], ga[...], gt[...]))
                dsm_r[...] = dsm + acc_r[3][...] if acc else dsm
                dbr_r[...] += dbr
                dar_r[...] += dar

        return pl.pallas_call(
            kern, grid=(nc, SSD_GROUPS // gps), name=name + "_bwd",
            in_specs=[sp['xs'], sp['bc'], sp['bc'], sp['small'], sp['row'], sp['row'], sp['enter'], sp['xs'], sp['state']]
            + ([sp['xs'], sp['bc'], sp['bc'], sp['small']] if acc else []),
            out_specs=[sp['xs'], sp['bc'], sp['bc'], sp['small'], sp['row'], sp['row'], sp['state']],
            out_shape=[jax.ShapeDtypeStruct(xs.shape, F32), jax.ShapeDtypeStruct(bm.shape, F32),
                       jax.ShapeDtypeStruct(cm.shape, F32), jax.ShapeDtypeStruct((n_rows, LANES), F32),
                       jax.ShapeDtypeStruct((1, LANES), F32), jax.ShapeDtypeStruct((1, LANES), F32),
                       jax.ShapeDtypeStruct(s0.shape, F32)],
            scratch_shapes=[pltpu.VMEM((n_state, SSD_STATE), F32)] + shared_scratch + shared_scratch,
            compiler_params=_params(vmem),
        )(xs, bm, cm, small, bias_row, alog_row, enter, dy, dsf, *acc)

    return fwd_call, bwd_call


def _ssd_pair(name, xs, bm, cm, small, rows_f, rows_b, s0_f, s0_b):
    calls = [_ssd_calls(name + "%d" % d, xs, bm, cm, s0_f, d, d == 1) for d in range(2)]

    def run_fwd(xs, bm, cm, small, rows_f, rows_b, s0_f, s0_b):
        y_f, sf_f, en_f = calls[0][0](xs, bm, cm, small, *rows_f, s0_f)
        y_b, sf_b, en_b = calls[1][0](xs, bm, cm, small, *rows_b, s0_b)
        return (y_f, y_b, sf_f, sf_b), (en_f, en_b)

    @jax.custom_vjp
    def op(*args):
        return run_fwd(*args)[0]

    def op_fwd(*args):
        outs, enters = run_fwd(*args)
        return outs, (args[:6], enters)

    def op_bwd(res, g):
        (xs, bm, cm, small, rows_f, rows_b), (en_f, en_b) = res
        dy_f, dy_b, dsf_f, dsf_b = g
        dxs, dbm, dcm, dsm, dbr_f, dar_f, ds0_f = calls[0][1](xs, bm, cm, small, *rows_f, en_f, dy_f, dsf_f)
        dxs, dbm, dcm, dsm, dbr_b, dar_b, ds0_b = calls[1][1](xs, bm, cm, small, *rows_b, en_b, dy_b, dsf_b, acc=(dxs, dbm, dcm, dsm))
        return dxs, dbm, dcm, dsm, (dbr_f, dar_f), (dbr_b, dar_b), ds0_f, ds0_b

    op.defvjp(op_fwd, op_bwd)
    return op(xs, bm, cm, small, tuple(rows_f), tuple(rows_b), s0_f, s0_b)


def _ml_shared(small, gate_row, reverse):
    gates = small + gate_row
    b_all = _exact_dot(_tri(reverse).astype(F32), jax.nn.log_sigmoid(gates))
    return gates, b_all, gates.T, b_all.T


def _ml_head(q, k, v, c_st, n_st, m_st, gates, b_all, gates_t, b_t, h, direction, reverse):
    mask = _tri(reverse)
    last = 0 if reverse else CHUNK - 1
    lane_i = _MG_LANE + 8 * direction + h
    lane_f = lane_i + ML_HEADS
    b_col = _pick_col(b_all, lane_f)
    b_row = _pick_row(b_t, lane_f)
    li_col = _pick_col(gates, lane_i)
    li_row = _pick_row(gates_t, lane_i)
    rowi = lax.broadcasted_iota(jnp.int32, (CHUNK, 1), 0)
    g_tot = jnp.sum(jnp.where(rowi == last, b_col, 0.0), axis=0, keepdims=True)
    m_in = m_st[:, 0:1]
    q = q * (ML_HD ** -0.5)
    w = g_tot - b_col + li_col
    m_loc = lax.stop_gradient(jnp.max(w, axis=0, keepdims=True))
    kw = k * jnp.exp(w - m_loc)
    c_loc = _mm_tn(kw, v)
    n_loc = jnp.sum(kw, axis=0, keepdims=True)
    m_new = lax.stop_gradient(jnp.maximum(g_tot + m_in, m_loc))
    s_old = jnp.exp(g_tot + m_in - m_new)
    s_loc = jnp.exp(m_loc - m_new)
    c_new = s_old * c_st + s_loc * c_loc
    n_new = s_old * n_st + s_loc * n_loc
    log_d = jnp.where(mask, b_col - b_row + li_row, -jnp.inf)
    inter = b_col + m_in
    m_t = lax.stop_gradient(jnp.maximum(inter, jnp.max(log_d, axis=1, keepdims=True)))
    dmat = jnp.exp(log_d - m_t)
    wi = jnp.exp(inter - m_t)
    s = _mm_nt(q, k) * dmat
    num = _mm_nn(s, v) + wi * _mm_nn(q, c_st)
    den = jnp.sum(s, axis=1, keepdims=True) + wi * jnp.sum(_round_bf16(q) * _round_bf16(n_st), axis=1, keepdims=True)
    out = num / jnp.maximum(jnp.abs(den), jnp.exp(-m_t))
    return out, c_new, n_new, jnp.broadcast_to(m_new, (1, LANES))


def _ml_calls(name, q, direction, reverse):
    n_rows = q.shape[0]
    nc = n_rows // CHUNK
    vmem = 4 * CHUNK * (4 * ML_HD + 128) * 8 + 4 * ML_HD * ML_HD * 12 + (8 << 20)

    def specs(order):
        def cidx(c):
            return (nc - 1 - c) if order else c

        return dict(
            qkv=pl.BlockSpec((CHUNK, ML_HD), lambda c, h: (cidx(c), h)),
            small=pl.BlockSpec((CHUNK, LANES), lambda c, h: (cidx(c), 0)),
            row=pl.BlockSpec((1, LANES), lambda c, h: (0, 0)),
            c=pl.BlockSpec((ML_HEADS * ML_HD, ML_HD), lambda c, h: (0, 0)),
            n=pl.BlockSpec((ML_HEADS, 1, ML_HD), lambda c, h: (0, 0, 0)),
            m=pl.BlockSpec((ML_HEADS, 1, LANES), lambda c, h: (0, 0, 0)),
            ec=pl.BlockSpec((1, ML_HD, ML_HD), lambda c, h: (cidx(c), h, 0)),
            en=pl.BlockSpec((1, 1, 1, ML_HD), lambda c, h: (cidx(c), h, 0, 0)),
            em=pl.BlockSpec((1, 1, 1, LANES), lambda c, h: (cidx(c), h, 0, 0)),
        )

    st_shapes = [jax.ShapeDtypeStruct((ML_HEADS * ML_HD, ML_HD), F32), jax.ShapeDtypeStruct((ML_HEADS, 1, ML_HD), F32),
                 jax.ShapeDtypeStruct((ML_HEADS, 1, LANES), F32)]
    scratch = [pltpu.VMEM((ML_HEADS * ML_HD, ML_HD), F32), pltpu.VMEM((ML_HEADS, 1, ML_HD), F32),
               pltpu.VMEM((ML_HEADS, 1, LANES), F32)]
    shared_scratch = [pltpu.VMEM((CHUNK, LANES), F32), pltpu.VMEM((CHUNK, LANES), F32),
                      pltpu.VMEM((LANES, CHUNK), F32), pltpu.VMEM((LANES, CHUNK), F32)]

    def head_rows(h):
        return pl.ds(pl.multiple_of(h * ML_HD, ML_HD), ML_HD)

    def fwd_call(q, k, v, small, gate_row, c0, n0, m0):
        sp = specs(reverse)

        def kern(q_r, k_r, v_r, sm_r, gr_r, c0_r, n0_r, m0_r, o_r, cf_r, nf_r, mf_r, ec_r, en_r, em_r, cs, ns, ms, *sh):
            c, h = pl.program_id(0), pl.program_id(1)

            @pl.when((c == 0) & (h == 0))
            def _():
                cs[...] = c0_r[...]
                ns[...] = n0_r[...]
                ms[...] = m0_r[...]

            @pl.when(h == 0)
            def _():
                for ref, val in zip(sh, _ml_shared(sm_r[...], gr_r[...], reverse)):
                    ref[...] = val

            rows = head_rows(h)
            c_in, n_in, m_in = cs[rows, :], ns[h], ms[h]
            ec_r[0] = c_in
            en_r[0, 0] = n_in
            em_r[0, 0] = m_in
            out, c_new, n_new, m_new = _ml_head(q_r[...], k_r[...], v_r[...], c_in, n_in, m_in, *[r[...] for r in sh],
                                                h, direction, reverse)
            o_r[...] = out
            cs[rows, :] = c_new
            ns[h] = n_new
            ms[h] = m_new
            cf_r[rows, :] = c_new
            nf_r[h] = n_new
            mf_r[h] = m_new

        return pl.pallas_call(
            kern, grid=(nc, ML_HEADS), name=name + "_fwd",
            in_specs=[sp['qkv']] * 3 + [sp['small'], sp['row'], sp['c'], sp['n'], sp['m']],
            out_specs=[sp['qkv'], sp['c'], sp['n'], sp['m'], sp['ec'], sp['en'], sp['em']],
            out_shape=[jax.ShapeDtypeStruct((n_rows, ML_HEADS * ML_HD), F32)] + st_shapes + [
                jax.ShapeDtypeStruct((nc, ML_HEADS * ML_HD, ML_HD), F32),
                jax.ShapeDtypeStruct((nc, ML_HEADS, 1, ML_HD), F32), jax.ShapeDtypeStruct((nc, ML_HEADS, 1, LANES), F32)],
            scratch_shapes=scratch + shared_scratch, compiler_params=_params(vmem),
        )(q, k, v, small, gate_row, c0, n0, m0)

    def bwd_call(q, k, v, small, gate_row, ec, en, em, do, dcf, dnf, dmf, acc=()):
        sp = specs(not reverse)
        n_sh = len(shared_scratch)

        def kern(*refs):
            q_r, k_r, v_r, sm_r, gr_r, ec_r, en_r, em_r, do_r, dcf_r, dnf_r, dmf_r = refs[:12]
            acc_r = refs[12:12 + len(acc)]
            dq_r, dk_r, dv_r, dsm_r, dgr_r, dc0_r, dn0_r, dm0_r, dcs, dns, dms = refs[12 + len(acc):23 + len(acc)]
            rest = refs[23 + len(acc):]
            sh, gsh = rest[:n_sh], rest[n_sh:]
            c, h = pl.program_id(0), pl.program_id(1)

            @pl.when((c == 0) & (h == 0))
            def _():
                dcs[...] = dcf_r[...]
                dns[...] = dnf_r[...]
                dms[...] = dmf_r[...]
                dgr_r[...] = jnp.zeros_like(dgr_r)

            @pl.when(h == 0)
            def _():
                for ref, val in zip(sh, _ml_shared(sm_r[...], gr_r[...], reverse)):
                    ref[...] = val
                for ref in gsh:
                    ref[...] = jnp.zeros_like(ref)

            rows = head_rows(h)
            fn = functools.partial(_ml_head, h=h, direction=direction, reverse=reverse)
            _, vjp = jax.vjp(fn, q_r[...], k_r[...], v_r[...], ec_r[0], en_r[0, 0], em_r[0, 0], *[r[...] for r in sh])
            grads = vjp((do_r[...], dcs[rows, :], dns[h], dms[h]))
            dq, dk, dv, dc, dn, dm = grads[:6]
            dq_r[...] = dq + acc_r[0][...] if acc else dq
            dk_r[...] = dk + acc_r[1][...] if acc else dk
            dv_r[...] = dv + acc_r[2][...] if acc else dv
            for ref, val in zip(gsh, grads[6:]):
                ref[...] += val
            dm = jnp.broadcast_to(jnp.sum(dm, axis=1, keepdims=True), (1, LANES)) * (1.0 / LANES)
            dcs[rows, :] = dc
            dns[h] = dn
            dms[h] = dm
            dc0_r[rows, :] = dc
            dn0_r[h] = dn
            dm0_r[h] = dm

            @pl.when(h == ML_HEADS - 1)
            def _():
                shared = functools.partial(_ml_shared, reverse=reverse)
                dsm, dgr = jax.vjp(shared, sm_r[...], gr_r[...])[1](tuple(r[...] for r in gsh))
                dsm_r[...] = dsm + acc_r[3][...] if acc else dsm
                dgr_r[...] += dgr

        return pl.pallas_call(
            kern, grid=(nc, ML_HEADS), name=name + "_bwd",
            in_specs=[sp['qkv']] * 3 + [sp['small'], sp['row'], sp['ec'], sp['en'], sp['em'], sp['qkv'], sp['c'], sp['n'], sp['m']]
            + ([sp['qkv']] * 3 + [sp['small']] if acc else []),
            out_specs=[sp['qkv']] * 3 + [sp['small'], sp['row'], sp['c'], sp['n'], sp['m']],
            out_shape=[jax.ShapeDtypeStruct(q.shape, F32)] * 3 + [jax.ShapeDtypeStruct((n_rows, LANES), F32),
                                                                  jax.ShapeDtypeStruct((1, LANES), F32)] + st_shapes,
            scratch_shapes=scratch + shared_scratch + shared_scratch, compiler_params=_params(vmem),
        )(q, k, v, small, gate_row, ec, en, em, do, dcf, dnf, dmf, *acc)

    return fwd_call, bwd_call


def _ml_pair(name, q, k, v, small, gate_row, state_f, state_b):
    calls = [_ml_calls(name + "%d" % d, q, d, d == 1) for d in range(2)]

    def run_fwd(q, k, v, small, gate_row, state_f, state_b):
        res_f = calls[0][0](q, k, v, small, gate_row, *state_f)
        res_b = calls[1][0](q, k, v, small, gate_row, *state_b)
        return (res_f[0], res_b[0], tuple(res_f[1:4]), tuple(res_b[1:4])), (tuple(res_f[4:]), tuple(res_b[4:]))

    @jax.custom_vjp
    def op(*args):
        return run_fwd(*args)[0]

    def op_fwd(*args):
        outs, enters = run_fwd(*args)
        return outs, (args[:5], enters)

    def op_bwd(res, g):
        (q, k, v, small, gate_row), (en_f, en_b) = res
        do_f, do_b, ds_f, ds_b = g
        dq, dk, dv, dsm, dgr_f, *d0_f = calls[0][1](q, k, v, small, gate_row, *en_f, do_f, *ds_f)
        dq, dk, dv, dsm, dgr_b, *d0_b = calls[1][1](q, k, v, small, gate_row, *en_b, do_b, *ds_b, acc=(dq, dk, dv, dsm))
        return dq, dk, dv, dsm, dgr_f + dgr_b, tuple(d0_f), tuple(d0_b)

    op.defvjp(op_fwd, op_bwd)
    return op(q, k, v, small, gate_row, tuple(state_f), tuple(state_b))


def _f_modulate(x, shift, scale):
    return (_layernorm_rows(x) * (1.0 + scale) + shift,)


def _f_resid_ln(x, o, gate, bias, ln_g, ln_b):
    return (_layernorm_rows(DN_ALPHA * x + gate * (o + bias)) * ln_g + ln_b,)


def _f_lru_gates(xc, w_r, b_r, w_i, b_i, lam):
    outs = []
    for d in range(2):
        def blockdiag(w):
            return jnp.concatenate(
                [_mm_nn(xc[:, n * LRU_BS:(n + 1) * LRU_BS], w[(d * LRU_BLOCKS + n) * LRU_BS:(d * LRU_BLOCKS + n + 1) * LRU_BS, :])
                 for n in range(LRU_BLOCKS)], axis=1)

        r = jax.nn.sigmoid(blockdiag(w_r) + b_r[d:d + 1])
        i = jax.nn.sigmoid(blockdiag(w_i) + b_i[d:d + 1])
        log_a = -LRU_C * r * jax.nn.softplus(-lam[d:d + 1])
        outs += [jnp.exp(log_a), jnp.sqrt(1.0 - jnp.exp(2.0 * log_a)) * i * xc]
    return tuple(outs)


def _f_lru_out(h_f, h_b, ly):
    return ((h_f + h_b) * jax.nn.gelu(ly),)


def _f_ssd_post(y_f, y_b, xs, z, d_exp, norm_w):
    y = (y_f + y_b + xs * d_exp) * jax.nn.silu(z)
    gw = SSD_INNER // SSD_GROUPS
    parts = []
    for g in range(SSD_GROUPS):
        yg = y[:, g * gw:(g + 1) * gw]
        parts.append(yg * lax.rsqrt(jnp.mean(jnp.square(yg), -1, keepdims=True) + LN_EPS))
    return (jnp.concatenate(parts, axis=1) * norm_w,)


def _f_ml_post(h_f, h_b, o, norm_w):
    h = h_f + h_b
    parts = [_layernorm_rows(h[:, i * ML_HD:(i + 1) * ML_HD]) for i in range(ML_HEADS)]
    return (jnp.concatenate(parts, axis=1) * norm_w * jax.nn.sigmoid(o),)


def _f_merge(ga, gb, gc, pa, pb, pc):
    return (jax.nn.sigmoid(ga) * pa + jax.nn.sigmoid(gb) * pb + jax.nn.sigmoid(gc) * pc,)


def _f_relu2(pre, bias):
    return (jnp.square(jax.nn.relu(pre + bias)),)


def _lane_row(vec, start):
    return jnp.pad(vec[None], ((0, 0), (start, LANES - start - vec.shape[0])))


def _mixer(tag, x_tok, shift, scale, p, states):
    (lru_s, ssd_s, ml_s) = states
    lx, ly, sz, xs, bm, cm, mq, mk, mv, mo, ga, gb, gc, small = _rowwise_linear(
        tag + "in", _f_modulate, [x_tok], [shift, scale], [(p['w_in_main'], _IN_MAIN_WIDTHS), (p['w_in_small'], None)])

    xc = _dwconv(tag + "lruconv", lx, p['lru_conv_w'], p['lru_conv_b'][None], False)
    a_f, b_f, a_b, b_b = _rowwise(
        tag + "lrugate", _f_lru_gates, [xc],
        [p['lru_w_r'].reshape(2 * LRU_BLOCKS * LRU_BS, LRU_BS), p['lru_b_r'], p['lru_w_i'].reshape(2 * LRU_BLOCKS * LRU_BS, LRU_BS),
         p['lru_b_i'], p['lru_lambda']], [D_MODEL] * 4)
    h_f, s_f = _lin_scan(tag + "lruscanf", a_f, b_f, lru_s[0], False)
    h_b, s_b = _lin_scan(tag + "lruscanb", a_b, b_b, lru_s[1], True)
    (pa,) = _rowwise_linear(tag + "bra", _f_lru_out, [h_f, h_b, ly], [], [(p['w_br_a'], None)], to_linear=(2,))

    cw, cb_ = p['ssd_conv_w'], p['ssd_conv_b'][None]
    xs_c = _dwconv(tag + "ssdconvx", xs, cw[:, :2048], cb_[:, :2048], True)
    bm_c = _dwconv(tag + "ssdconvb", bm, cw[:, 2048:3072], cb_[:, 2048:3072], True)
    cm_c = _dwconv(tag + "ssdconvc", cm, cw[:, 3072:], cb_[:, 3072:], True)
    dir_rows = [(_lane_row(p['ssd_dt_bias'][d], _DT_LANE + 32 * d), _lane_row(p['ssd_a_log'][d], _DT_LANE + 32 * d)) for d in range(2)]
    *ys, st_f, st_b = _ssd_pair(tag + "ssd", xs_c, bm_c, cm_c, small, dir_rows[0], dir_rows[1], ssd_s[0], ssd_s[1])
    ssd_new = (st_f, st_b)
    (pb,) = _rowwise_linear(tag + "brb", _f_ssd_post, [ys[0], ys[1], xs_c, sz],
                            [jnp.repeat(p['ssd_d'], SSD_HEADDIM)[None], p['ssd_norm_w'][None]], [(p['w_br_b'], None)], to_linear=(3,))

    mw, mb = p['ml_conv_w'], p['ml_conv_b'][None]
    q_c = _dwconv(tag + "mlconvq", mq, mw[:, :1024], mb[:, :1024], True)
    k_c = _dwconv(tag + "mlconvk", mk, mw[:, 1024:], mb[:, 1024:], True)
    gate_row = _lane_row(p['ml_gate_b'].reshape(4 * ML_HEADS), _MG_LANE)
    *hs, ml_f, ml_b = _ml_pair(tag + "ml", q_c, k_c, mv, small, gate_row, ml_s[0], ml_s[1])
    ml_new = (ml_f, ml_b)
    (pc,) = _rowwise_linear(tag + "brc", _f_ml_post, [hs[0], hs[1], mo], [p['ml_norm_w'][None]], [(p['w_br_c'], None)], to_linear=(2,))
    return (ga, gb, gc, pa, pb, pc), ((s_f, s_b), tuple(ssd_new), tuple(ml_new))


def _merge(tag, br, p):
    return _rowwise_linear(tag + "out", _f_merge, list(br), [], [(p['w_out'], None)], to_linear=tuple(range(6)))[0]


def _sublayers(tag, xin, o, mods, p, l):
    sh2, sc2, g1, g2 = mods
    (x1,) = _rowwise(tag + "ln1", _f_resid_ln, [xin, o], [g1, p['b_out'][None], p['ln1_g'][None], p['ln1_b'][None]], [D_MODEL], to_linear=(1,))
    (pre,) = _rowwise_linear(tag + "ff1", _f_modulate, [x1], [sh2, sc2], [(p['w_ff1'], None)])
    (o2,) = _rowwise_linear(tag + "ff2", _f_relu2, [pre], [p['b_ff1'][None]], [(p['w_ff2'], None)], to_linear=(0,))
    (x2,) = _rowwise(tag + "ln2", _f_resid_ln, [x1, o2], [g2, p['b_ff2'][None], p['ln2_g'][None], p['ln2_b'][None]], [D_MODEL], to_linear=(1,))
    return x2


def _to_col_major(h):
    s, d = h.shape
    return h.reshape(s // GRID_W, GRID_W, d).swapaxes(0, 1).reshape(s, d)


def _from_col_major(h):
    s, d = h.shape
    return h.reshape(GRID_W, s // GRID_W, d).swapaxes(0, 1).reshape(s, d)


def _forward(x, wts, mods, ctx):
    zeros = lambda *s: jnp.zeros(s, F32)
    ctx_init = ((zeros(1, D_MODEL), zeros(1, D_MODEL)),
                (zeros(SSD_INNER, SSD_STATE), zeros(SSD_INNER, SSD_STATE)),
                tuple((zeros(ML_HEADS * ML_HD, ML_HD), zeros(ML_HEADS, 1, ML_HD), zeros(ML_HEADS, 1, LANES)) for _ in range(2)))
    for l in range(DEPTH):
        p = {n: wts[n][l] for n in wts}
        tag = "l%d" % l
        sh1x, sc1x, g1x, sh2x, sc2x, g2x = [mods[l][0][:, i * D_MODEL:(i + 1) * D_MODEL] for i in range(6)]
        sh1c, sc1c, g1c, sh2c, sc2c, g2c = [mods[l][1][:, i * D_MODEL:(i + 1) * D_MODEL] for i in range(6)]
        br_c, ctx_states = _mixer(tag + "c", ctx, sh1c, sc1c, p, ctx_init)
        br_x, _ = _mixer(tag + "x", _to_col_major(x) if l % 2 == 1 else x, sh1x, sc1x, p, ctx_states)
        ox = _merge(tag + "x", br_x, p)
        if l % 2 == 1:
            ox = _from_col_major(ox)
        x = _sublayers(tag + "x", x, ox, (sh2x, sc2x, g1x, g2x), p, l)
        if l < DEPTH - 1:
            ctx = _sublayers(tag + "c", ctx, _merge(tag + "c", br_c, p), (sh2c, sc2c, g1c, g2c), p, l)
    return x


_ADA_ROWS = 2 * SUBLANES


def _ada_forward(c, c_ctx, w_ada, b_ada, me):
    c_all = _exchange("gather_c", jnp.broadcast_to(c, (SUBLANES, D_MODEL)), True)[:, 0]

    def rows_of(c_ctx_):
        pad = jnp.zeros((_ADA_ROWS - N_DEV - 1, D_MODEL), F32)
        return jax.nn.silu(jnp.concatenate([c_all, c_ctx_[None], pad], axis=0))

    rows, vjp_rows = jax.vjp(rows_of, c_ctx)
    cols, vjp_cols = jax.vjp(lambda r, w: jnp.stack([_linear("ada%d" % l, r, w[l]) for l in range(DEPTH)]), rows, w_ada)
    full = _exchange("gather_mod", cols, True).transpose(1, 2, 0, 3).reshape(DEPTH, _ADA_ROWS, 6 * D_MODEL) + b_ada[:, None, :]
    mods = [(lax.dynamic_slice_in_dim(full[l], me, 1, axis=0), full[l][N_DEV:N_DEV + 1]) for l in range(DEPTH)]
    return mods, (vjp_rows, vjp_cols)


def _ada_backward(saved, dmods):
    vjp_rows, vjp_cols = saved
    wcol = 6 * D_MODEL // N_DEV
    pad = jnp.zeros((SUBLANES - 2, 6 * D_MODEL), F32)
    both = jnp.stack([jnp.concatenate([dx, dc, pad], axis=0) for dx, dc in dmods])
    send = both.reshape(DEPTH, SUBLANES, N_DEV, wcol).transpose(2, 0, 1, 3)
    recv = _exchange("scatter_dmod", send, False)
    ctx_row = recv[0, :, 1]
    for k in range(1, N_DEV):
        ctx_row = ctx_row + recv[k, :, 1]
    g = jnp.concatenate([recv[:, :, 0].transpose(1, 0, 2), ctx_row[:, None],
                         jnp.zeros((DEPTH, _ADA_ROWS - N_DEV - 1, wcol), F32)], axis=1)
    d_rows, d_w = vjp_cols(g)
    (d_c_ctx,) = vjp_rows(d_rows)
    d_b = jnp.stack([(dx + dc)[0] for dx, dc in dmods])
    return d_w, d_b, d_c_ctx


def _loss_and_cotangent(y, target):
    n_rows, d = y.shape
    tt = _row_tile(n_rows, 0, cap=256)

    def kern(y_ref, t_ref, dy_ref, acc_ref):
        @pl.when(pl.program_id(0) == 0)
        def _():
            acc_ref[...] = jnp.zeros_like(acc_ref)

        err = y_ref[...] - t_ref[...]
        dy_ref[...] = err * (1.0 / d)
        acc_ref[...] += jnp.sum(jnp.square(err))

    spec = pl.BlockSpec((tt, d), lambda i: (i, 0))
    dy, acc = pl.pallas_call(
        kern, grid=(n_rows // tt,), name="loss", in_specs=[spec, spec],
        out_specs=[spec, pl.BlockSpec((SUBLANES, LANES), lambda i: (0, 0))],
        out_shape=[jax.ShapeDtypeStruct((n_rows, d), F32), jax.ShapeDtypeStruct((SUBLANES, LANES), F32)],
    )(y, target)
    return acc[0, 0] * (0.5 / d), dy


def _exchange(name, src, gather):
    slab = src.shape if gather else src.shape[1:]

    def body(src_ref, out_ref, send_sems, recv_sems, local_sem):
        x, y, c = lax.axis_index("x"), lax.axis_index("y"), lax.axis_index("c")
        me = 4 * x + 2 * y + c
        local = pltpu.make_async_copy(src_ref if gather else src_ref.at[me], out_ref.at[me], local_sem)
        local.start()
        copies = []
        for d in range(1, N_DEV):
            px, py, pc = lax.rem(x + (d >> 2), 2), lax.rem(y + ((d >> 1) & 1), 2), lax.rem(c + (d & 1), 2)
            peer = 4 * px + 2 * py + pc
            cp = pltpu.make_async_remote_copy(
                src_ref=src_ref if gather else src_ref.at[peer], dst_ref=out_ref.at[me],
                send_sem=send_sems.at[d - 1], recv_sem=recv_sems.at[d - 1],
                device_id=(px, py, pc), device_id_type=pl.DeviceIdType.MESH)
            cp.start()
            copies.append(cp)
        for cp in copies:
            cp.wait()
        local.wait()

    return pl.pallas_call(
        body, name=name, out_shape=jax.ShapeDtypeStruct((N_DEV,) + tuple(slab), src.dtype),
        in_specs=[pl.BlockSpec(memory_space=pl.ANY)], out_specs=pl.BlockSpec(memory_space=pl.ANY),
        scratch_shapes=[pltpu.SemaphoreType.DMA((N_DEV - 1,)), pltpu.SemaphoreType.DMA((N_DEV - 1,)), pltpu.SemaphoreType.DMA],
    )(src)


_HBM = pl.BlockSpec(memory_space=pl.ANY)
_CHIPS = ((0, 0), (0, 1), (1, 0), (1, 1))


def _gather_two_level(name, srcs):
    n = len(srcs)

    def body(*refs):
        src_refs, out_refs = refs[:n], refs[n:2 * n]
        send_sems, recv_sems, local_sems = refs[2 * n:]
        x, y, c = lax.axis_index("x"), lax.axis_index("y"), lax.axis_index("c")
        me, sibling = (x, y, c), (x, y, 1 - c)
        chips = [(1 - x, y), (x, 1 - y), (1 - x, 1 - y)]

        def slab(a, px, py, pc):
            return out_refs[a].at[4 * px + 2 * py + pc]

        def copy(a, k, block, to, own=False):
            return pltpu.make_async_remote_copy(
                src_ref=src_refs[a] if own else slab(a, *block), dst_ref=slab(a, *block), send_sem=send_sems.at[a, k],
                recv_sem=recv_sems.at[a, k], device_id=to, device_id_type=pl.DeviceIdType.MESH)

        mine = [pltpu.make_async_copy(src_refs[a], slab(a, *me), local_sems.at[a]) for a in range(n)]
        first = []
        for a in range(n):
            mine[a].start()
            first += [copy(a, 0, me, sibling, own=True)] + [copy(a, 1 + j, me, (*chip, c), own=True) for j, chip in enumerate(chips)]
        for cp in first:
            cp.start()
        passed = []
        for j, chip in enumerate(chips):
            for a in range(n):
                copy(a, 1 + j, (*chip, c), me).wait_recv()
                passed.append(copy(a, 4 + j, (*chip, c), sibling))
                passed[-1].start()
        for a in range(n):
            copy(a, 0, sibling, me).wait_recv()
        for j, chip in enumerate(chips):
            for a in range(n):
                copy(a, 4 + j, (*chip, 1 - c), me).wait_recv()
        for cp in first + passed:
            cp.wait_send()
        for cp in mine:
            cp.wait()

    return pl.pallas_call(
        body, name=name, out_shape=[jax.ShapeDtypeStruct((N_DEV,) + tuple(s.shape), s.dtype) for s in srcs],
        in_specs=[_HBM] * n, out_specs=[_HBM] * n,
        scratch_shapes=[pltpu.SemaphoreType.DMA((n, N_DEV - 1)), pltpu.SemaphoreType.DMA((n, N_DEV - 1)), pltpu.SemaphoreType.DMA((n,))],
    )(*srcs)


def _scatter_to_sibling(name, parts_list):
    n = len(parts_list)

    def body(*refs):
        p_refs, out_refs = refs[:n], refs[n:2 * n]
        send_sems, recv_sems = refs[2 * n:]
        x, y, c = lax.axis_index("x"), lax.axis_index("y"), lax.axis_index("c")
        copies = []
        for a in range(n):
            for j, (px, py) in enumerate(_CHIPS):
                cp = pltpu.make_async_remote_copy(
                    src_ref=p_refs[a].at[4 * px + 2 * py + (1 - c)], dst_ref=out_refs[a].at[j], send_sem=send_sems.at[a, j],
                    recv_sem=recv_sems.at[a, j], device_id=(x, y, 1 - c), device_id_type=pl.DeviceIdType.MESH)
                cp.start()
                copies.append(cp)
        for cp in copies:
            cp.wait()

    return pl.pallas_call(
        body, name=name, out_shape=[jax.ShapeDtypeStruct((4,) + tuple(p.shape[1:]), p.dtype) for p in parts_list],
        in_specs=[_HBM] * n, out_specs=[_HBM] * n,
        scratch_shapes=[pltpu.SemaphoreType.DMA((n, 4)), pltpu.SemaphoreType.DMA((n, 4))],
    )(*parts_list)


def _chip_sum(name, parts, from_sibling):
    _, rows, cols = parts.shape
    lanes = -(-cols // LANES) * LANES
    tr = _row_tile(rows, 4 * lanes * 4 * 2, budget=12 << 20)

    def kern(p_ref, s_ref, o_ref):
        c = lax.axis_index("c")
        o_ref[0] = (jnp.where(c == 0, p_ref[0, 0], p_ref[0, 1]) + s_ref[0]).astype(o_ref.dtype)

    return pl.pallas_call(
        kern, grid=(4, rows // tr), name=name,
        in_specs=[pl.BlockSpec((1, 2, tr, cols), lambda j, i: (j, 0, i, 0)), pl.BlockSpec((1, tr, cols), lambda j, i: (j, i, 0))],
        out_specs=pl.BlockSpec((1, tr, cols), lambda j, i: (j, i, 0)),
        out_shape=jax.ShapeDtypeStruct((4, rows, cols), BF16),
        compiler_params=_params(4 * lanes * tr * 4 * 2),
    )(parts.reshape(4, 2, rows, cols), from_sibling)


def _scatter_across_chips(name, sums_list):
    n = len(sums_list)

    def body(*refs):
        q_refs, out_refs = refs[:n], refs[n:2 * n]
        send_sems, recv_sems, local_sems = refs[2 * n:]
        x, y, c = lax.axis_index("x"), lax.axis_index("y"), lax.axis_index("c")
        own = 2 * x + y
        copies = []
        for a in range(n):
            local = pltpu.make_async_copy(q_refs[a].at[own], out_refs[a].at[own], local_sems.at[a])
            local.start()
            copies.append(local)
            for d in range(1, 4):
                px, py = lax.rem(x + (d >> 1), 2), lax.rem(y + (d & 1), 2)
                cp = pltpu.make_async_remote_copy(
                    src_ref=q_refs[a].at[2 * px + py], dst_ref=out_refs[a].at[own], send_sem=send_sems.at[a, d - 1],
                    recv_sem=recv_sems.at[a, d - 1], device_id=(px, py, c), device_id_type=pl.DeviceIdType.MESH)
                cp.start()
                copies.append(cp)
        for cp in copies:
            cp.wait()

    return pl.pallas_call(
        body, name=name, out_shape=[jax.ShapeDtypeStruct(s.shape, s.dtype) for s in sums_list],
        in_specs=[_HBM] * n, out_specs=[_HBM] * n,
        scratch_shapes=[pltpu.SemaphoreType.DMA((n, 3)), pltpu.SemaphoreType.DMA((n, 3)), pltpu.SemaphoreType.DMA((n,))],
    )(*sums_list)


def _sum_parts(name, parts):
    n_parts, rows, cols = parts.shape
    tr = _row_tile(rows, 4 * cols * (n_parts + 1) * 2)

    def kern(p_ref, o_ref):
        acc = p_ref[0]
        for k in range(1, n_parts):
            acc = acc + p_ref[k]
        o_ref[...] = acc

    return pl.pallas_call(
        kern, grid=(rows // tr,), name=name, in_specs=[pl.BlockSpec((n_parts, tr, cols), lambda i: (0, i, 0))],
        out_specs=pl.BlockSpec((tr, cols), lambda i: (i, 0)), out_shape=jax.ShapeDtypeStruct((rows, cols), F32),
    )(parts)


def _adamw(name, w, m, v, parts):
    n_parts, rows, cols = parts.shape
    lanes = -(-cols // LANES) * LANES
    tr = _row_tile(rows, 4 * lanes * (n_parts + 7) * 2, budget=16 << 20)
    c1 = np.float32(1.0 - ADAM_B1 ** ADAM_STEP)
    c2 = np.float32(1.0 - ADAM_B2 ** ADAM_STEP)

    def kern(w_ref, m_ref, v_ref, p_ref, g_ref, d_ref, nm_ref, nv_ref):
        g = p_ref[0].astype(F32)
        for k in range(1, n_parts):
            g = g + p_ref[k].astype(F32)
        m_new = ADAM_B1 * m_ref[...] + (1.0 - ADAM_B1) * g
        v_new = ADAM_B2 * v_ref[...] + (1.0 - ADAM_B2) * jnp.square(g)
        g_ref[...] = g
        nm_ref[...] = m_new
        nv_ref[...] = v_new
        d_ref[...] = -ADAM_LR * ((m_new / c1) / (jnp.sqrt(v_new / c2) + ADAM_EPS) + ADAM_WD * w_ref[...])

    spec = pl.BlockSpec((tr, cols), lambda i: (i, 0))
    return pl.pallas_call(
        kern, grid=(rows // tr,), name=name,
        in_specs=[spec, spec, spec, pl.BlockSpec((n_parts, tr, cols), lambda i: (0, i, 0))], out_specs=[spec] * 4,
        out_shape=[jax.ShapeDtypeStruct((rows, cols), F32)] * 4,
        compiler_params=_params(4 * lanes * tr * (n_parts + 7) * 2),
    )(w, m, v, parts)


def _packed_rows(shape):
    return -(-int(np.prod(shape)) // (SUBLANES * LANES)) * SUBLANES


def _pack(arrays, row_multiple):
    parts = []
    for a in arrays:
        n = int(np.prod(a.shape))
        r = _packed_rows(a.shape)
        parts.append(jnp.pad(a.reshape(-1), (0, r * LANES - n)).reshape(r, LANES))
    rows = sum(p.shape[0] for p in parts)
    total = -(-rows // row_multiple) * row_multiple
    if total > rows:
        parts.append(jnp.zeros((total - rows, LANES), arrays[0].dtype))
    return jnp.concatenate(parts, axis=0)


def _unpack(packed, shapes):
    out, off = [], 0
    for s in shapes:
        r = _packed_rows(s)
        out.append(packed[off:off + r].reshape(-1)[:int(np.prod(s))].reshape(s))
        off += r
    return out


def _split_w_in(w_in):
    main = jnp.concatenate([w_in[:, :, s:e] for s, e in _IN_MAIN], axis=2)
    pad = jnp.zeros(w_in.shape[:2] + (LANES - 80,), w_in.dtype)
    small = jnp.concatenate([w_in[:, :, s:e] for s, e in _IN_SMALL] + [pad], axis=2)
    return main, small


def _join_w_in(main, small):
    return jnp.concatenate([main[:, :, 0:8192], small[:, :, 0:64], main[:, :, 8192:12288], small[:, :, 64:80],
                            main[:, :, 12288:15360]], axis=2)


def _unshard(gathered, axis):
    nd, nl, r, c = gathered.shape
    if axis == 1:
        return gathered.transpose(1, 0, 2, 3).reshape(nl, nd * r, c)
    return gathered.transpose(1, 2, 0, 3).reshape(nl, r, nd * c)


def _reshard(full, axis):
    nl, r, c = full.shape
    if axis == 1:
        return full.reshape(nl, N_DEV, r // N_DEV, c).transpose(1, 0, 2, 3)
    return full.reshape(nl, r, N_DEV, c // N_DEV).transpose(2, 0, 1, 3)


def kernel(x, c, ctx, c_ctx, w_ada, b_ada, w_in, lru_conv_w, lru_conv_b, lru_w_r, lru_b_r, lru_w_i, lru_b_i, lru_lambda, ssd_conv_w, ssd_conv_b, ssd_dt_bias, ssd_a_log, ssd_d, ssd_norm_w, ml_conv_w, ml_conv_b, ml_gate_b, ml_norm_w, w_br_a, w_br_b, w_br_c, w_out, b_out, ln1_g, ln1_b, w_ff1, b_ff1, w_ff2, b_ff2, ln2_g, ln2_b, loss_target, m_c_ctx, m_w_ada, m_b_ada, m_w_in, m_lru_conv_w, m_lru_conv_b, m_lru_w_r, m_lru_b_r, m_lru_w_i, m_lru_b_i, m_lru_lambda, m_ssd_conv_w, m_ssd_conv_b, m_ssd_dt_bias, m_ssd_a_log, m_ssd_d, m_ssd_norm_w, m_ml_conv_w, m_ml_conv_b, m_ml_gate_b, m_ml_norm_w, m_w_br_a, m_w_br_b, m_w_br_c, m_w_out, m_b_out, m_ln1_g, m_ln1_b, m_w_ff1, m_b_ff1, m_w_ff2, m_b_ff2, m_ln2_g, m_ln2_b, v_c_ctx, v_w_ada, v_b_ada, v_w_in, v_lru_conv_w, v_lru_conv_b, v_lru_w_r, v_lru_b_r, v_lru_w_i, v_lru_b_i, v_lru_lambda, v_ssd_conv_w, v_ssd_conv_b, v_ssd_dt_bias, v_ssd_a_log, v_ssd_d, v_ssd_norm_w, v_ml_conv_w, v_ml_conv_b, v_ml_gate_b, v_ml_norm_w, v_w_br_a, v_w_br_b, v_w_br_c, v_w_out, v_b_out, v_ln1_g, v_ln1_b, v_w_ff1, v_b_ff1, v_w_ff2, v_b_ff2, v_ln2_g, v_ln2_b):
    a = dict(locals())
    me = 4 * lax.axis_index("x") + 2 * lax.axis_index("y") + lax.axis_index("c")

    wts = {n: a[n] for n in _REPLICATED if n not in ('c_ctx', 'b_ada')}
    exchanged = [n for n in _BIG if n != 'w_ada']
    gathered = _gather_two_level("gather_weights", [a[n].astype(BF16) for n in exchanged])
    for n, g in zip(exchanged, gathered):
        full = _unshard(g, _BIG[n])
        if n == 'w_in':
            main, small = _split_w_in(full)
            wts['w_in_main'], wts['w_in_small'] = main.astype(F32), small.astype(F32)
        else:
            wts[n] = full.astype(F32)
    small_shapes = [a[n].shape for n in _SMALL_SHARDED]
    small_all = _exchange("gather_small", _pack([a[n] for n in _SMALL_SHARDED], SUBLANES), True)
    per_dev = [_unpack(small_all[k], small_shapes) for k in range(N_DEV)]
    for i, n in enumerate(_SMALL_SHARDED):
        wts[n] = jnp.concatenate([per_dev[k][i] for k in range(N_DEV)], axis=-1)

    mods, ada_saved = _ada_forward(c, c_ctx, w_ada, b_ada, me)
    y, vjp = jax.vjp(functools.partial(_forward, ctx=ctx[0]), x[0], wts, mods)
    loss_local, dy = _loss_and_cotangent(y, loss_target[0])
    grad_x, grads, dmods = vjp(dy)
    grads['w_in'] = _join_w_in(grads.pop('w_in_main'), grads.pop('w_in_small'))
    grad_w_ada, grads['b_ada'], grads['c_ctx'] = _ada_backward(ada_saved, dmods)
    loss = lax.psum(loss_local, ("x", "y", "c"))

    out = {}

    def put(n, res, shape):
        for kind, r in zip(("grad_", "delta_", "new_m_", "new_v_"), res):
            out[kind + n] = r.reshape(shape)

    flat = {n: (a[n].shape[0] * a[n].shape[1], a[n].shape[2]) for n in _BIG}
    by_dest = [_reshard(grads[n], _BIG[n]).reshape(N_DEV, *flat[n]) for n in exchanged]
    from_sibling = _scatter_to_sibling("scatter_d2d", by_dest)
    chip_sums = [_chip_sum("chipsum_" + n, p, s) for n, p, s in zip(exchanged, by_dest, from_sibling)]
    summed = dict(zip(exchanged, _scatter_across_chips("scatter_ici", chip_sums)))
    summed['w_ada'] = grad_w_ada.reshape(1, *flat['w_ada'])
    for n in _BIG:
        shp = a[n].shape
        rows, cols = flat[n]
        parts = summed[n]
        put(n, _adamw("adamw_" + n, a[n].reshape(rows, cols), a["m_" + n].reshape(rows, cols), a["v_" + n].reshape(rows, cols), parts), shp)

    rep_names = _REPLICATED + _SMALL_SHARDED
    chunk_rows = SUBLANES * N_DEV
    g_pack = _pack([grads[n] for n in rep_names], chunk_rows * N_DEV)
    rows = g_pack.shape[0]
    parts = _exchange("scatter_rep", g_pack.reshape(N_DEV, rows // N_DEV, LANES), False)
    mine = _sum_parts("sum_rep", parts)
    g_all = _exchange("gather_rep", mine, True).reshape(rows, LANES)
    g_full = _unpack(g_all, [grads[n].shape for n in rep_names])
    g_local = []
    for n, g in zip(rep_names, g_full):
        if n in _SMALL_SHARDED:
            width = a[n].shape[-1]
            g = lax.dynamic_slice_in_dim(g, me * width, width, axis=g.ndim - 1)
        g_local.append(g)
    shapes = [a[n].shape for n in rep_names]
    res = _adamw("adamw_rep", _pack([a[n] for n in rep_names], chunk_rows), _pack([a["m_" + n] for n in rep_names], chunk_rows),
                 _pack([a["v_" + n] for n in rep_names], chunk_rows), _pack(g_local, chunk_rows)[None])
    unpacked = [_unpack(r, shapes) for r in res]
    for i, n in enumerate(rep_names):
        put(n, [u[i] for u in unpacked], shapes[i])

    outs = [loss, grad_x[None]]
    for kind in ("grad_", "delta_", "new_m_", "new_v_"):
        outs += [out[kind + n] for n in _WEIGHTS]
    return tuple(outs)
```

```python
import functools

import numpy as np
import jax
import jax.numpy as jnp
from jax import lax
from jax.experimental import pallas as pl
from jax.experimental.pallas import tpu as pltpu

F32 = jnp.float32
BF16 = jnp.bfloat16

N_DEV = 8
D_MODEL = 1024
DEPTH = 2
GRID_W = 64
CHUNK = 128
LN_EPS = 1e-6
LRU_BLOCKS = 8
LRU_BS = 128
LRU_C = 8.0
SSD_INNER = 2048
SSD_GROUPS = 8
SSD_HPG = 4
SSD_HEADDIM = 64
SSD_STATE = 128
ML_HEADS = 4
ML_HD = 256
D_FF = 4096
DN_ALPHA = (2 * DEPTH) ** 0.25
ADAM_LR, ADAM_B1, ADAM_B2, ADAM_EPS, ADAM_WD, ADAM_STEP = 0.001, 0.9, 0.999, 1e-08, 0.01, 10

VMEM_CAP = 60 * 1024 * 1024
SUBLANES = 8
LANES = 128

_IN_MAIN = ((0, 8192), (8256, 12352), (12368, 15440))
_IN_MAIN_WIDTHS = (1024, 1024, 2048, 2048, 1024, 1024, 1024, 1024, 1024, 1024, 1024, 1024, 1024)
_IN_SMALL = ((8192, 8256), (12352, 12368))
_DT_LANE = 0
_MG_LANE = 64

_WEIGHTS = ['c_ctx', 'w_ada', 'b_ada', 'w_in', 'lru_conv_w', 'lru_conv_b', 'lru_w_r', 'lru_b_r', 'lru_w_i', 'lru_b_i',
            'lru_lambda', 'ssd_conv_w', 'ssd_conv_b', 'ssd_dt_bias', 'ssd_a_log', 'ssd_d', 'ssd_norm_w', 'ml_conv_w',
            'ml_conv_b', 'ml_gate_b', 'ml_norm_w', 'w_br_a', 'w_br_b', 'w_br_c', 'w_out', 'b_out', 'ln1_g', 'ln1_b',
            'w_ff1', 'b_ff1', 'w_ff2', 'b_ff2', 'ln2_g', 'ln2_b']
_BIG = {'w_ada': 2, 'w_in': 2, 'w_ff1': 2, 'w_br_a': 1, 'w_br_b': 1, 'w_br_c': 1, 'w_out': 1, 'w_ff2': 1}
_SMALL_SHARDED = ['lru_conv_w', 'lru_b_r', 'lru_b_i', 'lru_lambda', 'ssd_conv_w', 'ml_conv_w']
_REPLICATED = [n for n in _WEIGHTS if n not in _BIG and n not in _SMALL_SHARDED]


def _params(vmem_bytes):
    return pltpu.CompilerParams(vmem_limit_bytes=int(min(max(2 * vmem_bytes, 32 << 20), VMEM_CAP)))


def _row_tile(n_rows, bytes_per_row, budget=6 << 20, cap=512):
    t = cap
    while t > SUBLANES and (t * bytes_per_row > budget or n_rows % t):
        t //= 2
    assert n_rows % t == 0, (n_rows, t)
    return t


def _dg(a, b, ca, cb):
    return lax.dot_general(a.astype(BF16), b.astype(BF16), (((ca,), (cb,)), ((), ())), preferred_element_type=F32)


def _make_bdot(ca, cb):
    @jax.custom_vjp
    def f(a, b):
        return _dg(a, b, ca, cb)

    def fwd(a, b):
        return _dg(a, b, ca, cb), (a, b)

    def bwd(res, g):
        a, b = res
        da = _dg(g, b, 1, 1 - cb) if ca == 1 else _dg(b, g, 1 - cb, 1)
        db = _dg(a, g, 1 - ca, 0) if cb == 0 else _dg(g, a, 0, 1 - ca)
        return da, db

    f.defvjp(fwd, bwd)
    return f


_mm_nn = _make_bdot(1, 0)
_mm_nt = _make_bdot(1, 1)
_mm_tn = _make_bdot(0, 0)


@jax.custom_vjp
def _round_bf16(x):
    return x.astype(BF16).astype(F32)


_round_bf16.defvjp(lambda x: (_round_bf16(x), None), lambda _, g: (g,))


def _exact_dot(a, b):
    return jnp.dot(a, b, precision=lax.Precision.HIGHEST, preferred_element_type=F32)


def _layernorm_rows(x):
    mu = jnp.mean(x, -1, keepdims=True)
    var = jnp.mean(jnp.square(x - mu), -1, keepdims=True)
    return (x - mu) * lax.rsqrt(var + LN_EPS)


def _rowwise_calls(name, f, rows, params, out_widths, out_dtype=F32, to_linear=()):
    drow_dtypes = [BF16 if i in to_linear else F32 for i in range(len(rows))]
    nr, npar, no = len(rows), len(params), len(out_widths)
    n_rows = rows[0].shape[0]
    row_w = [r.shape[1] for r in rows]
    par_bytes = sum(int(np.prod(p.shape)) * 4 for p in params)
    tile = _row_tile(n_rows, 4 * (2 * sum(row_w) + 2 * sum(out_widths)), budget=12 << 20)
    grid = (n_rows // tile,)

    def row_spec(w):
        return pl.BlockSpec((tile, w), lambda i: (i, 0))

    def par_spec(p):
        return pl.BlockSpec(p.shape, lambda i: (0, 0))

    vmem = 2 * tile * 4 * (2 * sum(row_w) + 3 * sum(out_widths)) + 4 * par_bytes

    def fwd_call(rows, params):
        def kern(*refs):
            outs = f(*[r[...] for r in refs[:nr + npar]])
            for r, o in zip(refs[nr + npar:], outs):
                r[...] = o.astype(out_dtype)

        return pl.pallas_call(
            kern, grid=grid, name=name + "_fwd",
            in_specs=[row_spec(w) for w in row_w] + [par_spec(p) for p in params],
            out_specs=[row_spec(w) for w in out_widths],
            out_shape=[jax.ShapeDtypeStruct((n_rows, w), out_dtype) for w in out_widths],
            compiler_params=_params(vmem),
        )(*rows, *params)

    def bwd_call(rows, params, gouts):
        def kern(*refs):
            ins = [r[...] for r in refs[:nr + npar]]
            gs = tuple(r[...] for r in refs[nr + npar:nr + npar + no])
            grads = jax.vjp(f, *ins)[1](gs)
            drefs = refs[nr + npar + no:]
            for k in range(nr):
                drefs[k][...] = grads[k].astype(drefs[k].dtype)

            @pl.when(pl.program_id(0) == 0)
            def _():
                for k in range(npar):
                    drefs[nr + k][...] = jnp.zeros_like(drefs[nr + k])

            for k in range(npar):
                drefs[nr + k][...] += grads[nr + k]

        res = pl.pallas_call(
            kern, grid=grid, name=name + "_bwd",
            in_specs=[row_spec(w) for w in row_w] + [par_spec(p) for p in params] + [row_spec(w) for w in out_widths],
            out_specs=[row_spec(w) for w in row_w] + [par_spec(p) for p in params],
            out_shape=[jax.ShapeDtypeStruct(r.shape, dt) for r, dt in zip(rows, drow_dtypes)]
            + [jax.ShapeDtypeStruct(p.shape, F32) for p in params],
            compiler_params=_params(vmem),
        )(*rows, *params, *gouts)
        return tuple(r.astype(F32) for r in res[:nr]), tuple(res[nr:])

    return fwd_call, bwd_call


def _rowwise(name, f, rows, params, out_widths, to_linear=()):
    rows, params = tuple(rows), tuple(params)
    fwd_call, bwd_call = _rowwise_calls(name, f, rows, params, out_widths, to_linear=to_linear)

    @jax.custom_vjp
    def op(rows, params):
        return tuple(fwd_call(rows, params))

    op.defvjp(lambda r, p: (tuple(fwd_call(r, p)), (r, p)), lambda res, g: bwd_call(res[0], res[1], g))
    return op(rows, params)


def _rowwise_linear(name, f, rows, params, weights, to_linear=()):
    rows, params = tuple(rows), tuple(params)
    ws = tuple(w for w, _ in weights)
    m, k = rows[0].shape[0], ws[0].shape[0]
    row_fwd, row_bwd = _rowwise_calls(name, f, rows, params, [k], BF16, to_linear)
    lin = [_linear_calls(name + "lin%d" % i, m, k, w.shape[1], wd, BF16) for i, (w, wd) in enumerate(weights)]
    counts = [len(c[3]) for c in lin]

    def fwd(rows, params, ws):
        (a,) = row_fwd(rows, params)
        outs = []
        for (fwd_call, _, _, _), w in zip(lin, ws):
            outs += list(fwd_call(a, w))
        return tuple(outs), a

    @jax.custom_vjp
    def op(rows, params, ws):
        return fwd(rows, params, ws)[0]

    def op_fwd(rows, params, ws):
        outs, a = fwd(rows, params, ws)
        return outs, (rows, params, ws, a)

    def op_bwd(res, g):
        rows, params, ws, a = res
        da, dws, off = None, [], 0
        for (_, dgrad_call, wgrad_call, _), w, cnt in zip(lin, ws, counts):
            gk = g[off:off + cnt]
            off += cnt
            d = dgrad_call(w, gk)
            da = d if da is None else da + d
            dws.append(wgrad_call(a, gk))
        drows, dparams = row_bwd(rows, params, (da,))
        return drows, dparams, tuple(dws)

    op.defvjp(op_fwd, op_bwd)
    return op(rows, params, ws)


def _group_ranges(widths, tn):
    starts, s = [], 0
    for w in widths:
        assert w % tn == 0, (w, tn)
        starts.append((s // tn, (s + w) // tn))
        s += w
    return starts, s // tn


def _group_tile(refs, ranges, row_tile, col_tile, i, j):
    out = []
    for ref, (s, e) in zip(refs, ranges):
        cols = pl.ds(pl.multiple_of((j - s) * col_tile, col_tile), col_tile)
        out.append(((j >= s) & (j < e), ref, cols))
    return [(p, lambda r=r, c=c: r.at[pl.ds(pl.multiple_of(i * row_tile, row_tile), row_tile), c]) for p, r, c in out]


def _linear_calls(name, m, k, n, widths, a_dtype):
    widths = (n,) if widths is None else tuple(widths)
    ng = len(widths)
    cast_a = a_dtype != BF16
    wide = all(wd % 1024 == 0 for wd in widths)
    tn = 128 if n < 256 else (1024 if wide else (256 if k > 2048 or n % 512 else 512))
    tm = _row_tile(m, 0, cap=1024 if k <= 2048 else 512)
    ranges, nt = _group_ranges(widths, tn)
    mt = m // tm
    tn_w = 1024 if (wide and k <= 1024) else (512 if all(wd % 512 == 0 for wd in widths) else min(tn, 256))
    tm_w = _row_tile(m, 0, cap=1024 if k <= 1024 else 512)
    ranges_w, nt_w = _group_ranges(widths, tn_w)
    mt_w = m // tm_w
    hbm = pl.BlockSpec(memory_space=pl.ANY)

    def fwd_call(a, w):
        n_steps = mt * nt

        def kern(a_ref, w_ref, *rest):
            outs, obuf, osem = rest[:ng], rest[ng], rest[ng + 1]
            a_bf = rest[ng + 2] if cast_a else a_ref
            i, j = pl.program_id(0), pl.program_id(1)
            step = i * nt + j
            slot = lax.rem(step, 2)

            def drain(sl):
                pltpu.make_async_copy(obuf.at[sl], outs[0].at[pl.ds(0, tm), pl.ds(0, tn)], osem.at[sl]).wait()

            if cast_a:
                @pl.when(j == 0)
                def _():
                    a_bf[...] = a_ref[...].astype(BF16)

            @pl.when(step >= 2)
            def _():
                drain(slot)

            obuf[slot] = jnp.dot(a_bf[...], w_ref[...], preferred_element_type=F32)
            for pred, window in _group_tile(outs, ranges, tm, tn, i, j):
                @pl.when(pred)
                def _(window=window):
                    pltpu.make_async_copy(obuf.at[slot], window(), osem.at[slot]).start()

            @pl.when(step == n_steps - 1)
            def _():
                drain(slot)
                if n_steps > 1:
                    drain(1 - slot)

        return pl.pallas_call(
            kern, grid=(mt, nt), name=name + "_fwd",
            in_specs=[pl.BlockSpec((tm, k), lambda i, j: (i, 0)), pl.BlockSpec((k, tn), lambda i, j: (0, j))],
            out_specs=[hbm] * ng,
            out_shape=[jax.ShapeDtypeStruct((m, wd), F32) for wd in widths],
            scratch_shapes=[pltpu.VMEM((2, tm, tn), F32), pltpu.SemaphoreType.DMA((2,))]
            + ([pltpu.VMEM((tm, k), BF16)] if cast_a else []),
            compiler_params=_params(10 * tm * k + 4 * k * tn + 8 * tm * tn),
        )(a, w.astype(BF16))

    def prefetched(gs, gbuf, gsem, rngs, row_tile, col_tile, step, n_steps, tile_of):
        slot = lax.rem(step, 2)

        def start(s_idx, sl):
            ii, jj = tile_of(s_idx)
            for pred, window in _group_tile(gs, rngs, row_tile, col_tile, ii, jj):
                @pl.when(pred)
                def _(window=window):
                    pltpu.make_async_copy(window(), gbuf.at[sl], gsem.at[sl]).start()

        @pl.when(step == 0)
        def _():
            start(step, slot)

        @pl.when(step + 1 < n_steps)
        def _():
            start(step + 1, 1 - slot)

        pltpu.make_async_copy(gs[0].at[pl.ds(0, row_tile), pl.ds(0, col_tile)], gbuf.at[slot], gsem.at[slot]).wait()
        return slot

    def dgrad_call(w, gouts):
        def kern(w_ref, *rest):
            gs, da, gbuf, gsem = rest[:ng], rest[ng], rest[ng + 1], rest[ng + 2]
            i, j = pl.program_id(0), pl.program_id(1)
            slot = prefetched(gs, gbuf, gsem, ranges, tm, tn, i * nt + j, mt * nt, lambda s: (s // nt, lax.rem(s, nt)))

            @pl.when(j == 0)
            def _():
                da[...] = jnp.zeros_like(da)

            da[...] += lax.dot_general(gbuf[slot].astype(BF16), w_ref[...], (((1,), (1,)), ((), ())), preferred_element_type=F32)

        return pl.pallas_call(
            kern, grid=(mt, nt), name=name + "_dgrad",
            in_specs=[pl.BlockSpec((k, tn), lambda i, j: (0, j))] + [hbm] * ng,
            out_specs=pl.BlockSpec((tm, k), lambda i, j: (i, 0)),
            out_shape=jax.ShapeDtypeStruct((m, k), F32),
            scratch_shapes=[pltpu.VMEM((2, tm, tn), BF16), pltpu.SemaphoreType.DMA((2,))],
            compiler_params=_params(12 * tm * k + 4 * k * tn + 10 * tm * tn),
        )(w.astype(BF16), *[g.astype(BF16) for g in gouts])

    def wgrad_call(a, gouts):
        def kern(a_ref, *rest):
            gs, dw, gbuf, gsem = rest[:ng], rest[ng], rest[ng + 1], rest[ng + 2]
            j, i = pl.program_id(0), pl.program_id(1)
            slot = prefetched(gs, gbuf, gsem, ranges_w, tm_w, tn_w, j * mt_w + i, mt_w * nt_w,
                              lambda s: (lax.rem(s, mt_w), s // mt_w))

            @pl.when(i == 0)
            def _():
                dw[...] = jnp.zeros_like(dw)

            dw[...] += lax.dot_general(a_ref[...].astype(BF16), gbuf[slot].astype(BF16), (((0,), (0,)), ((), ())),
                                       preferred_element_type=F32)

        return pl.pallas_call(
            kern, grid=(nt_w, mt_w), name=name + "_wgrad",
            in_specs=[pl.BlockSpec((tm_w, k), lambda j, i: (i, 0))] + [hbm] * ng,
            out_specs=pl.BlockSpec((k, tn_w), lambda j, i: (0, j)),
            out_shape=jax.ShapeDtypeStruct((k, n), F32),
            scratch_shapes=[pltpu.VMEM((2, tm_w, tn_w), BF16), pltpu.SemaphoreType.DMA((2,))],
            compiler_params=_params(12 * tm_w * k + 12 * k * tn_w + 10 * tm_w * tn_w),
        )(a, *[g.astype(BF16) for g in gouts])

    return fwd_call, dgrad_call, wgrad_call, widths


def _linear(name, a, w, widths=None):
    fwd_call, dgrad_call, wgrad_call, _ = _linear_calls(name, a.shape[0], a.shape[1], w.shape[1], widths, a.dtype)

    @jax.custom_vjp
    def op(a, w):
        return tuple(fwd_call(a, w))

    op.defvjp(lambda a, w: (tuple(fwd_call(a, w)), (a, w)),
              lambda res, g: (dgrad_call(res[1], g), wgrad_call(res[0], g)))
    out = op(a, w)
    return out[0] if widths is None else out


def _conv_taps(x_ext, w, n_ext):
    xm2 = pltpu.roll(x_ext, 2, 0)
    xm1 = pltpu.roll(x_ext, 1, 0)
    xp1 = pltpu.roll(x_ext, n_ext - 1, 0)
    return xm2, xm1, xp1


def _dwconv(name, x, w, b, act):
    n_rows, ch = x.shape
    tt = _row_tile(n_rows, 4 * 6 * ch, budget=12 << 20, cap=256)
    nt = n_rows // tt
    n_ext = tt + 2 * SUBLANES
    per8 = tt // SUBLANES
    last8 = n_rows // SUBLANES - 1
    main = pl.BlockSpec((tt, ch), lambda i: (i, 0))
    prev = pl.BlockSpec((SUBLANES, ch), lambda i: (jnp.maximum(i * per8 - 1, 0), 0))
    nxt = pl.BlockSpec((SUBLANES, ch), lambda i: (jnp.minimum((i + 1) * per8, last8), 0))
    wspec = pl.BlockSpec((4, ch), lambda i: (0, 0))
    bspec = pl.BlockSpec((1, ch), lambda i: (0, 0))
    vmem = 4 * n_ext * ch * 14

    def ext(main_ref, prev_ref, next_ref):
        i = pl.program_id(0)
        p = jnp.where(i > 0, prev_ref[...], 0.0)
        q = jnp.where(i < nt - 1, next_ref[...], 0.0)
        return jnp.concatenate([p, main_ref[...], q], axis=0)

    def pre_of(x_ext, wv, bv):
        xm2, xm1, xp1 = _conv_taps(x_ext, wv, n_ext)
        pre = wv[0:1] * xm2 + wv[1:2] * xm1 + wv[2:3] * x_ext + wv[3:4] * xp1 + bv
        return pre, (xm2, xm1, xp1)

    def fwd_call(x, w, b):
        def kern(xm, xp, xn, w_ref, b_ref, o_ref):
            pre, _ = pre_of(ext(xm, xp, xn), w_ref[...], b_ref[...])
            pre = pre[SUBLANES:SUBLANES + tt]
            o_ref[...] = pre * jax.nn.sigmoid(pre) if act else pre

        return pl.pallas_call(
            kern, grid=(nt,), name=name + "_fwd", in_specs=[main, prev, nxt, wspec, bspec], out_specs=main,
            out_shape=jax.ShapeDtypeStruct((n_rows, ch), F32), compiler_params=_params(vmem),
        )(x, x, x, w, b)

    def bwd_call(x, w, b, dy):
        def kern(xm, xp, xn, gm, gp, gn, w_ref, b_ref, dx_ref, dw_ref, db_ref):
            wv = w_ref[...]
            x_ext = ext(xm, xp, xn)
            pre, (xm2, xm1, xp1) = pre_of(x_ext, wv, b_ref[...])
            dpre = ext(gm, gp, gn)
            if act:
                sg = jax.nn.sigmoid(pre)
                dpre = dpre * (sg + pre * sg * (1.0 - sg))
            dx = (wv[0:1] * pltpu.roll(dpre, n_ext - 2, 0) + wv[1:2] * pltpu.roll(dpre, n_ext - 1, 0)
                  + wv[2:3] * dpre + wv[3:4] * pltpu.roll(dpre, 1, 0))
            sl = slice(SUBLANES, SUBLANES + tt)
            dx_ref[...] = dx[sl].astype(dx_ref.dtype)
            dm = dpre[sl]

            @pl.when(pl.program_id(0) == 0)
            def _():
                dw_ref[...] = jnp.zeros_like(dw_ref)
                db_ref[...] = jnp.zeros_like(db_ref)

            dw_ref[...] += jnp.concatenate(
                [jnp.sum(dm * t[sl], axis=0, keepdims=True) for t in (xm2, xm1, x_ext, xp1)], axis=0)
            db_ref[...] += jnp.sum(dm, axis=0, keepdims=True)

        return pl.pallas_call(
            kern, grid=(nt,), name=name + "_bwd", in_specs=[main, prev, nxt, main, prev, nxt, wspec, bspec],
            out_specs=[main, wspec, bspec],
            out_shape=[jax.ShapeDtypeStruct((n_rows, ch), BF16), jax.ShapeDtypeStruct((4, ch), F32),
                       jax.ShapeDtypeStruct((1, ch), F32)],
            compiler_params=_params(vmem),
        )(x, x, x, dy, dy, dy, w, b)

    @jax.custom_vjp
    def op(x, w, b):
        return fwd_call(x, w, b)

    def op_bwd(res, g):
        dx, dw, db = bwd_call(*res, g)
        return dx.astype(F32), dw, db

    op.defvjp(lambda x, w, b: (fwd_call(x, w, b), (x, w, b)), op_bwd)
    return op(x, w, b)


def _scan_groups(tt, ch, reverse, load, store, carry_ref):
    row = lax.broadcasted_iota(jnp.int32, (SUBLANES, ch), 0)
    ng = tt // SUBLANES

    def body(k, carry):
        g = (ng - 1 - k) if reverse else k
        sl = pl.ds(pl.multiple_of(g * SUBLANES, SUBLANES), SUBLANES)
        a, b, extra = load(sl)
        for s in (1, 2, 4):
            if reverse:
                a_sh, b_sh, valid = pltpu.roll(a, SUBLANES - s, 0), pltpu.roll(b, SUBLANES - s, 0), row < SUBLANES - s
            else:
                a_sh, b_sh, valid = pltpu.roll(a, s, 0), pltpu.roll(b, s, 0), row >= s
            b = jnp.where(valid, b + a * b_sh, b)
            a = jnp.where(valid, a * a_sh, a)
        h = b + a * carry
        if reverse:
            h_prev = jnp.where(row == SUBLANES - 1, carry, pltpu.roll(h, SUBLANES - 1, 0))
            last = h[0:1]
        else:
            h_prev = jnp.where(row == 0, carry, pltpu.roll(h, 1, 0))
            last = h[SUBLANES - 1:SUBLANES]
        store(sl, h, h_prev, extra)
        return jnp.broadcast_to(last, (SUBLANES, ch))

    carry_ref[...] = lax.fori_loop(0, ng, body, carry_ref[...])


def _lin_scan(name, a, b, h0, reverse):
    n_rows, ch = a.shape
    tt = _row_tile(n_rows, 0, cap=256)
    nt = n_rows // tt
    vmem = 2 * 4 * tt * ch * 5

    def tile_spec(rev):
        return pl.BlockSpec((tt, ch), (lambda i: (nt - 1 - i, 0)) if rev else (lambda i: (i, 0)))

    vec = pl.BlockSpec((1, ch), lambda i: (0, 0))

    def fwd_call(a, b, h0):
        def kern(a_ref, b_ref, h0_ref, h_ref, hp_ref, last_ref, carry):
            @pl.when(pl.program_id(0) == 0)
            def _():
                carry[...] = jnp.broadcast_to(h0_ref[...], carry.shape)

            def load(sl):
                return a_ref[sl, :], b_ref[sl, :], None

            def store(sl, h, h_prev, _):
                h_ref[sl, :] = h
                hp_ref[sl, :] = h_prev

            _scan_groups(tt, ch, reverse, load, store, carry)
            last_ref[...] = carry[0:1]

        return pl.pallas_call(
            kern, grid=(nt,), name=name + "_fwd", in_specs=[tile_spec(reverse), tile_spec(reverse), vec],
            out_specs=[tile_spec(reverse), tile_spec(reverse), vec],
            out_shape=[jax.ShapeDtypeStruct((n_rows, ch), F32)] * 2 + [jax.ShapeDtypeStruct((1, ch), F32)],
            scratch_shapes=[pltpu.VMEM((SUBLANES, ch), F32)], compiler_params=_params(vmem),
        )(a, b, h0)

    def bwd_call(a, h_prev, dh, dlast):
        rev = not reverse

        def kern(a_ref, hp_ref, dh_ref, dl_ref, da_ref, db_ref, d0_ref, carry):
            @pl.when(pl.program_id(0) == 0)
            def _():
                carry[...] = jnp.broadcast_to(dl_ref[...], carry.shape)

            def load(sl):
                av, dv = a_ref[sl, :], dh_ref[sl, :]
                return av, av * dv, dv

            def store(sl, u, u_next, dv):
                g = dv + u_next
                db_ref[sl, :] = g
                da_ref[sl, :] = g * hp_ref[sl, :]

            _scan_groups(tt, ch, rev, load, store, carry)
            d0_ref[...] = carry[0:1]

        return pl.pallas_call(
            kern, grid=(nt,), name=name + "_bwd", in_specs=[tile_spec(rev)] * 3 + [vec],
            out_specs=[tile_spec(rev), tile_spec(rev), vec],
            out_shape=[jax.ShapeDtypeStruct((n_rows, ch), F32)] * 2 + [jax.ShapeDtypeStruct((1, ch), F32)],
            scratch_shapes=[pltpu.VMEM((SUBLANES, ch), F32)], compiler_params=_params(vmem),
        )(a, h_prev, dh, dlast)

    @jax.custom_vjp
    def op(a, b, h0):
        h, _, last = fwd_call(a, b, h0)
        return h, last

    def op_fwd(a, b, h0):
        h, h_prev, last = fwd_call(a, b, h0)
        return (h, last), (a, h_prev)

    def op_bwd(res, g):
        da, db, d0 = bwd_call(res[0], res[1], g[0], g[1])
        return da, db, d0

    op.defvjp(op_fwd, op_bwd)
    return op(a, b, h0)


def _tri(reverse):
    q = lax.broadcasted_iota(jnp.int32, (CHUNK, CHUNK), 0)
    s = lax.broadcasted_iota(jnp.int32, (CHUNK, CHUNK), 1)
    return (q <= s) if reverse else (q >= s)


def _pick_col(x, lane):
    idx = lax.broadcasted_iota(jnp.int32, x.shape, 1)
    return jnp.sum(jnp.where(idx == lane, x, 0.0), axis=1, keepdims=True)


def _pick_row(x, row):
    idx = lax.broadcasted_iota(jnp.int32, x.shape, 0)
    return jnp.sum(jnp.where(idx == row, x, 0.0), axis=0, keepdims=True)


def _ssd_shared(small, bias_row, alog_row, reverse):
    delta_all = jax.nn.softplus(small + bias_row)
    acs_all = _exact_dot(_tri(reverse).astype(F32), delta_all * (-jnp.exp(alog_row)))
    return delta_all, acs_all, acs_all.T


def _ssd_group(xs, bm, cm, state, delta_all, acs_all, acs_t, g, direction, reverse):
    mask = _tri(reverse)
    last = 0 if reverse else CHUNK - 1
    hd = SSD_HEADDIM
    rowi = lax.broadcasted_iota(jnp.int32, (CHUNK, 1), 0)
    a_cols, a_rows, deltas, tots = [], [], [], []
    for r in range(SSD_HPG):
        lane = _DT_LANE + 32 * direction + SSD_HPG * g + r
        a_col = _pick_col(acs_all, lane)
        a_cols.append(a_col)
        a_rows.append(_pick_row(acs_t, lane))
        deltas.append(_pick_col(delta_all, lane))
        tots.append(jnp.sum(jnp.where(rowi == last, a_col, 0.0), axis=0, keepdims=True))

    def wide(cols, rows):
        return jnp.concatenate([jnp.broadcast_to(c, (rows, hd)) for c in cols], axis=1)

    a_w = wide(a_cols, CHUNK)
    x_w = xs * wide(deltas, CHUNK)
    st = _mm_tn(x_w * jnp.exp(wide(tots, 1) - a_w), bm)
    y_off = _mm_nt(cm, state) * jnp.exp(a_w)
    grow = jnp.concatenate([jnp.broadcast_to(jnp.exp(t), (hd, 1)) for t in tots], axis=0)
    cb = _mm_nt(cm, bm)
    m_cat = jnp.concatenate([cb * jnp.exp(jnp.where(mask, a_cols[r] - a_rows[r], -jnp.inf)) for r in range(SSD_HPG)], axis=1)
    lane_head = lax.broadcasted_iota(jnp.int32, (1, SSD_HPG * hd), 1) // hd
    x_bd = jnp.concatenate([jnp.where(lane_head == r, x_w, 0.0) for r in range(SSD_HPG)], axis=0)
    return _mm_nn(m_cat, x_bd) + y_off, grow * state + st


def _ssd_calls(name, xs, bm, cm, s0, direction, reverse):
    n_rows = xs.shape[0]
    nc = n_rows // CHUNK
    gw = SSD_HPG * SSD_HEADDIM
    vmem = 4 * CHUNK * (gw + 3 * 128) * 8 + 4 * gw * 128 * 12 + (8 << 20)

    n_state = SSD_GROUPS * gw
    shared_scratch = [pltpu.VMEM((CHUNK, LANES), F32), pltpu.VMEM((CHUNK, LANES), F32), pltpu.VMEM((LANES, CHUNK), F32)]

    def specs(order, gps=1):
        def cidx(c):
            return (nc - 1 - c) if order else c

        return dict(
            xs=pl.BlockSpec((CHUNK, gps * gw), lambda c, g: (cidx(c), g)),
            bc=pl.BlockSpec((CHUNK, gps * SSD_STATE), lambda c, g: (cidx(c), g)),
            small=pl.BlockSpec((CHUNK, LANES), lambda c, g: (cidx(c), 0)),
            row=pl.BlockSpec((1, LANES), lambda c, g: (0, 0)),
            state=pl.BlockSpec((n_state, SSD_STATE), lambda c, g: (0, 0)),
            enter=pl.BlockSpec((1, gps * gw, SSD_STATE), lambda c, g: (cidx(c), g, 0)),
        )

    gps_fwd, gps_bwd = 4, 1

    def group_rows(g, gps):
        return pl.ds(pl.multiple_of(g * gps * gw, gps * gw), gps * gw)

    def step_fn(g, gps):
        def fn(xs_v, bm_v, cm_v, st_v, d_all, a_all, a_t):
            ys, sts = [], []
            for u in range(gps):
                y_u, s_u = _ssd_group(xs_v[:, u * gw:(u + 1) * gw], bm_v[:, u * SSD_STATE:(u + 1) * SSD_STATE],
                                      cm_v[:, u * SSD_STATE:(u + 1) * SSD_STATE], st_v[u * gw:(u + 1) * gw],
                                      d_all, a_all, a_t, gps * g + u, direction, reverse)
                ys.append(y_u)
                sts.append(s_u)
            return jnp.concatenate(ys, axis=1), jnp.concatenate(sts, axis=0)

        return fn

    def fwd_call(xs, bm, cm, small, bias_row, alog_row, s0):
        gps = gps_fwd
        sp = specs(reverse, gps)

        def kern(xs_r, bm_r, cm_r, sm_r, br_r, ar_r, s0_r, y_r, sf_r, se_r, st, sh_d, sh_a, sh_t):
            c, g = pl.program_id(0), pl.program_id(1)

            @pl.when((c == 0) & (g == 0))
            def _():
                st[...] = s0_r[...]

            @pl.when(g == 0)
            def _():
                sh_d[...], sh_a[...], sh_t[...] = _ssd_shared(sm_r[...], br_r[...], ar_r[...], reverse)

            rows = group_rows(g, gps)
            s_in = st[rows, :]
            se_r[0] = s_in
            y_r[...], s_new = step_fn(g, gps)(xs_r[...], bm_r[...], cm_r[...], s_in, sh_d[...], sh_a[...], sh_t[...])
            st[rows, :] = s_new
            sf_r[rows, :] = s_new

        return pl.pallas_call(
            kern, grid=(nc, SSD_GROUPS // gps), name=name + "_fwd",
            in_specs=[sp['xs'], sp['bc'], sp['bc'], sp['small'], sp['row'], sp['row'], sp['state']],
            out_specs=[sp['xs'], sp['state'], sp['enter']],
            out_shape=[jax.ShapeDtypeStruct((n_rows, SSD_INNER), F32), jax.ShapeDtypeStruct((n_state, SSD_STATE), F32),
                       jax.ShapeDtypeStruct((nc, n_state, SSD_STATE), F32)],
            scratch_shapes=[pltpu.VMEM((n_state, SSD_STATE), F32)] + shared_scratch, compiler_params=_params(vmem),
        )(xs, bm, cm, small, bias_row, alog_row, s0)

    def bwd_call(xs, bm, cm, small, bias_row, alog_row, enter, dy, dsf, acc=()):
        gps = gps_bwd
        sp = specs(not reverse, gps)

        def kern(*refs):
            xs_r, bm_r, cm_r, sm_r, br_r, ar_r, se_r, dy_r, dsf_r = refs[:9]
            acc_r = refs[9:9 + len(acc)]
            dxs_r, dbm_r, dcm_r, dsm_r, dbr_r, dar_r, ds0_r, ds, sh_d, sh_a, sh_t, gd, ga, gt = refs[9 + len(acc):]
            c, g = pl.program_id(0), pl.program_id(1)

            @pl.when((c == 0) & (g == 0))
            def _():
                ds[...] = dsf_r[...]
                dbr_r[...] = jnp.zeros_like(dbr_r)
                dar_r[...] = jnp.zeros_like(dar_r)

            @pl.when(g == 0)
            def _():
                sh_d[...], sh_a[...], sh_t[...] = _ssd_shared(sm_r[...], br_r[...], ar_r[...], reverse)
                gd[...] = jnp.zeros_like(gd)
                ga[...] = jnp.zeros_like(ga)
                gt[...] = jnp.zeros_like(gt)

            rows = group_rows(g, gps)
            _, vjp = jax.vjp(step_fn(g, gps), xs_r[...], bm_r[...], cm_r[...], se_r[0], sh_d[...], sh_a[...], sh_t[...])
            dxs, dbm, dcm, ds_in, dd, da, dt = vjp((dy_r[...], ds[rows, :]))
            dxs_r[...] = dxs + acc_r[0][...] if acc else dxs
            dbm_r[...] = dbm + acc_r[1][...] if acc else dbm
            dcm_r[...] = dcm + acc_r[2][...] if acc else dcm
            ds[rows, :] = ds_in
            ds0_r[rows, :] = ds_in
            gd[...] += dd
            ga[...] += da
            gt[...] += dt

            @pl.when(g == SSD_GROUPS // gps - 1)
            def _():
                shared = functools.partial(_ssd_shared, reverse=reverse)
                dsm, dbr, dar = jax.vjp(shared, sm_r[...], br_r[...], ar_r[...])[1]((gd[...], ga[...], gt[...]))
                dsm_r[...] = dsm + acc_r[3][...] if acc else dsm
                dbr_r[...] += dbr
                dar_r[...] += dar

        return pl.pallas_call(
            kern, grid=(nc, SSD_GROUPS // gps), name=name + "_bwd",
            in_specs=[sp['xs'], sp['bc'], sp['bc'], sp['small'], sp['row'], sp['row'], sp['enter'], sp['xs'], sp['state']]
            + ([sp['xs'], sp['bc'], sp['bc'], sp['small']] if acc else []),
            out_specs=[sp['xs'], sp['bc'], sp['bc'], sp['small'], sp['row'], sp['row'], sp['state']],
            out_shape=[jax.ShapeDtypeStruct(xs.shape, F32), jax.ShapeDtypeStruct(bm.shape, F32),
                       jax.ShapeDtypeStruct(cm.shape, F32), jax.ShapeDtypeStruct((n_rows, LANES), F32),
                       jax.ShapeDtypeStruct((1, LANES), F32), jax.ShapeDtypeStruct((1, LANES), F32),
                       jax.ShapeDtypeStruct(s0.shape, F32)],
            scratch_shapes=[pltpu.VMEM((n_state, SSD_STATE), F32)] + shared_scratch + shared_scratch,
            compiler_params=_params(vmem),
        )(xs, bm, cm, small, bias_row, alog_row, enter, dy, dsf, *acc)

    return fwd_call, bwd_call


def _ssd_pair(name, xs, bm, cm, small, rows_f, rows_b, s0_f, s0_b):
    calls = [_ssd_calls(name + "%d" % d, xs, bm, cm, s0_f, d, d == 1) for d in range(2)]

    def run_fwd(xs, bm, cm, small, rows_f, rows_b, s0_f, s0_b):
        y_f, sf_f, en_f = calls[0][0](xs, bm, cm, small, *rows_f, s0_f)
        y_b, sf_b, en_b = calls[1][0](xs, bm, cm, small, *rows_b, s0_b)
        return (y_f, y_b, sf_f, sf_b), (en_f, en_b)

    @jax.custom_vjp
    def op(*args):
        return run_fwd(*args)[0]

    def op_fwd(*args):
        outs, enters = run_fwd(*args)
        return outs, (args[:6], enters)

    def op_bwd(res, g):
        (xs, bm, cm, small, rows_f, rows_b), (en_f, en_b) = res
        dy_f, dy_b, dsf_f, dsf_b = g
        dxs, dbm, dcm, dsm, dbr_f, dar_f, ds0_f = calls[0][1](xs, bm, cm, small, *rows_f, en_f, dy_f, dsf_f)
        dxs, dbm, dcm, dsm, dbr_b, dar_b, ds0_b = calls[1][1](xs, bm, cm, small, *rows_b, en_b, dy_b, dsf_b, acc=(dxs, dbm, dcm, dsm))
        return dxs, dbm, dcm, dsm, (dbr_f, dar_f), (dbr_b, dar_b), ds0_f, ds0_b

    op.defvjp(op_fwd, op_bwd)
    return op(xs, bm, cm, small, tuple(rows_f), tuple(rows_b), s0_f, s0_b)


def _ml_shared(small, gate_row, reverse):
    gates = small + gate_row
    b_all = _exact_dot(_tri(reverse).astype(F32), jax.nn.log_sigmoid(gates))
    return gates, b_all, gates.T, b_all.T


def _ml_head(q, k, v, c_st, n_st, m_st, gates, b_all, gates_t, b_t, h, direction, reverse):
    mask = _tri(reverse)
    last = 0 if reverse else CHUNK - 1
    lane_i = _MG_LANE + 8 * direction + h
    lane_f = lane_i + ML_HEADS
    b_col = _pick_col(b_all, lane_f)
    b_row = _pick_row(b_t, lane_f)
    li_col = _pick_col(gates, lane_i)
    li_row = _pick_row(gates_t, lane_i)
    rowi = lax.broadcasted_iota(jnp.int32, (CHUNK, 1), 0)
    g_tot = jnp.sum(jnp.where(rowi == last, b_col, 0.0), axis=0, keepdims=True)
    m_in = m_st[:, 0:1]
    q = q * (ML_HD ** -0.5)
    w = g_tot - b_col + li_col
    m_loc = lax.stop_gradient(jnp.max(w, axis=0, keepdims=True))
    kw = k * jnp.exp(w - m_loc)
    c_loc = _mm_tn(kw, v)
    n_loc = jnp.sum(kw, axis=0, keepdims=True)
    m_new = lax.stop_gradient(jnp.maximum(g_tot + m_in, m_loc))
    s_old = jnp.exp(g_tot + m_in - m_new)
    s_loc = jnp.exp(m_loc - m_new)
    c_new = s_old * c_st + s_loc * c_loc
    n_new = s_old * n_st + s_loc * n_loc
    log_d = jnp.where(mask, b_col - b_row + li_row, -jnp.inf)
    inter = b_col + m_in
    m_t = lax.stop_gradient(jnp.maximum(inter, jnp.max(log_d, axis=1, keepdims=True)))
    dmat = jnp.exp(log_d - m_t)
    wi = jnp.exp(inter - m_t)
    s = _mm_nt(q, k) * dmat
    num = _mm_nn(s, v) + wi * _mm_nn(q, c_st)
    den = jnp.sum(s, axis=1, keepdims=True) + wi * jnp.sum(_round_bf16(q) * _round_bf16(n_st), axis=1, keepdims=True)
    out = num / jnp.maximum(jnp.abs(den), jnp.exp(-m_t))
    return out, c_new, n_new, jnp.broadcast_to(m_new, (1, LANES))


def _ml_calls(name, q, direction, reverse):
    n_rows = q.shape[0]
    nc = n_rows // CHUNK
    vmem = 4 * CHUNK * (4 * ML_HD + 128) * 8 + 4 * ML_HD * ML_HD * 12 + (8 << 20)

    def specs(order, hps=1):
        def cidx(c):
            return (nc - 1 - c) if order else c

        return dict(
            qkv=pl.BlockSpec((CHUNK, hps * ML_HD), lambda c, h: (cidx(c), h)),
            small=pl.BlockSpec((CHUNK, LANES), lambda c, h: (cidx(c), 0)),
            row=pl.BlockSpec((1, LANES), lambda c, h: (0, 0)),
            c=pl.BlockSpec((ML_HEADS * ML_HD, ML_HD), lambda c, h: (0, 0)),
            n=pl.BlockSpec((ML_HEADS, 1, ML_HD), lambda c, h: (0, 0, 0)),
            m=pl.BlockSpec((ML_HEADS, 1, LANES), lambda c, h: (0, 0, 0)),
            ec=pl.BlockSpec((1, hps * ML_HD, ML_HD), lambda c, h: (cidx(c), h, 0)),
            en=pl.BlockSpec((1, hps, 1, ML_HD), lambda c, h: (cidx(c), h, 0, 0)),
            em=pl.BlockSpec((1, hps, 1, LANES), lambda c, h: (cidx(c), h, 0, 0)),
        )

    st_shapes = [jax.ShapeDtypeStruct((ML_HEADS * ML_HD, ML_HD), F32), jax.ShapeDtypeStruct((ML_HEADS, 1, ML_HD), F32),
                 jax.ShapeDtypeStruct((ML_HEADS, 1, LANES), F32)]
    scratch = [pltpu.VMEM((ML_HEADS * ML_HD, ML_HD), F32), pltpu.VMEM((ML_HEADS, 1, ML_HD), F32),
               pltpu.VMEM((ML_HEADS, 1, LANES), F32)]
    shared_scratch = [pltpu.VMEM((CHUNK, LANES), F32), pltpu.VMEM((CHUNK, LANES), F32),
                      pltpu.VMEM((LANES, CHUNK), F32), pltpu.VMEM((LANES, CHUNK), F32)]

    def head_rows(h):
        return pl.ds(pl.multiple_of(h * ML_HD, ML_HD), ML_HD)

    def fwd_call(q, k, v, small, gate_row, c0, n0, m0):
        hps = ML_HEADS
        sp = specs(reverse, hps)

        def kern(q_r, k_r, v_r, sm_r, gr_r, c0_r, n0_r, m0_r, o_r, cf_r, nf_r, mf_r, ec_r, en_r, em_r, cs, ns, ms, *sh):
            c, hh = pl.program_id(0), pl.program_id(1)

            @pl.when((c == 0) & (hh == 0))
            def _():
                cs[...] = c0_r[...]
                ns[...] = n0_r[...]
                ms[...] = m0_r[...]

            @pl.when(hh == 0)
            def _():
                for ref, val in zip(sh, _ml_shared(sm_r[...], gr_r[...], reverse)):
                    ref[...] = val

            q_v, k_v, v_v = q_r[...], k_r[...], v_r[...]
            shared = [r[...] for r in sh]
            heads = [hps * hh + u for u in range(hps)]
            states = [(cs[head_rows(h), :], ns[h], ms[h]) for h in heads]
            results = []
            for u, (h, (c_in, n_in, m_in)) in enumerate(zip(heads, states)):
                cols = slice(u * ML_HD, (u + 1) * ML_HD)
                ec_r[0, cols, :] = c_in
                en_r[0, u] = n_in
                em_r[0, u] = m_in
                results.append(_ml_head(q_v[:, cols], k_v[:, cols], v_v[:, cols], c_in, n_in, m_in, *shared,
                                        h, direction, reverse))
            o_r[...] = jnp.concatenate([r[0] for r in results], axis=1)
            for h, (_, c_new, n_new, m_new) in zip(heads, results):
                cs[head_rows(h), :] = c_new
                ns[h] = n_new
                ms[h] = m_new
                cf_r[head_rows(h), :] = c_new
                nf_r[h] = n_new
                mf_r[h] = m_new

        return pl.pallas_call(
            kern, grid=(nc, ML_HEADS // hps), name=name + "_fwd",
            in_specs=[sp['qkv']] * 3 + [sp['small'], sp['row'], sp['c'], sp['n'], sp['m']],
            out_specs=[sp['qkv'], sp['c'], sp['n'], sp['m'], sp['ec'], sp['en'], sp['em']],
            out_shape=[jax.ShapeDtypeStruct((n_rows, ML_HEADS * ML_HD), F32)] + st_shapes + [
                jax.ShapeDtypeStruct((nc, ML_HEADS * ML_HD, ML_HD), F32),
                jax.ShapeDtypeStruct((nc, ML_HEADS, 1, ML_HD), F32), jax.ShapeDtypeStruct((nc, ML_HEADS, 1, LANES), F32)],
            scratch_shapes=scratch + shared_scratch, compiler_params=_params(vmem),
        )(q, k, v, small, gate_row, c0, n0, m0)

    def bwd_call(q, k, v, small, gate_row, ec, en, em, do, dcf, dnf, dmf, acc=()):
        sp = specs(not reverse)
        n_sh = len(shared_scratch)

        def kern(*refs):
            q_r, k_r, v_r, sm_r, gr_r, ec_r, en_r, em_r, do_r, dcf_r, dnf_r, dmf_r = refs[:12]
            acc_r = refs[12:12 + len(acc)]
            dq_r, dk_r, dv_r, dsm_r, dgr_r, dc0_r, dn0_r, dm0_r, dcs, dns, dms = refs[12 + len(acc):23 + len(acc)]
            rest = refs[23 + len(acc):]
            sh, gsh = rest[:n_sh], rest[n_sh:]
            c, h = pl.program_id(0), pl.program_id(1)

            @pl.when((c == 0) & (h == 0))
            def _():
                dcs[...] = dcf_r[...]
                dns[...] = dnf_r[...]
                dms[...] = dmf_r[...]
                dgr_r[...] = jnp.zeros_like(dgr_r)

            @pl.when(h == 0)
            def _():
                for ref, val in zip(sh, _ml_shared(sm_r[...], gr_r[...], reverse)):
                    ref[...] = val
                for ref in gsh:
                    ref[...] = jnp.zeros_like(ref)

            rows = head_rows(h)
            fn = functools.partial(_ml_head, h=h, direction=direction, reverse=reverse)
            _, vjp = jax.vjp(fn, q_r[...], k_r[...], v_r[...], ec_r[0], en_r[0, 0], em_r[0, 0], *[r[...] for r in sh])
            grads = vjp((do_r[...], dcs[rows, :], dns[h], dms[h]))
            dq, dk, dv, dc, dn, dm = grads[:6]
            dq_r[...] = dq + acc_r[0][...] if acc else dq
            dk_r[...] = dk + acc_r[1][...] if acc else dk
            dv_r[...] = dv + acc_r[2][...] if acc else dv
            for ref, val in zip(gsh, grads[6:]):
                ref[...] += val
            dm = jnp.broadcast_to(jnp.sum(dm, axis=1, keepdims=True), (1, LANES)) * (1.0 / LANES)
            dcs[rows, :] = dc
            dns[h] = dn
            dms[h] = dm
            dc0_r[rows, :] = dc
            dn0_r[h] = dn
            dm0_r[h] = dm

            @pl.when(h == ML_HEADS - 1)
            def _():
                shared = functools.partial(_ml_shared, reverse=reverse)
                dsm, dgr = jax.vjp(shared, sm_r[...], gr_r[...])[1](tuple(r[...] for r in gsh))
                dsm_r[...] = dsm + acc_r[3][...] if acc else dsm
                dgr_r[...] += dgr

        return pl.pallas_call(
            kern, grid=(nc, ML_HEADS), name=name + "_bwd",
            in_specs=[sp['qkv']] * 3 + [sp['small'], sp['row'], sp['ec'], sp['en'], sp['em'], sp['qkv'], sp['c'], sp['n'], sp['m']]
            + ([sp['qkv']] * 3 + [sp['small']] if acc else []),
            out_specs=[sp['qkv']] * 3 + [sp['small'], sp['row'], sp['c'], sp['n'], sp['m']],
            out_shape=[jax.ShapeDtypeStruct(q.shape, F32)] * 3 + [jax.ShapeDtypeStruct((n_rows, LANES), F32),
                                                                  jax.ShapeDtypeStruct((1, LANES), F32)] + st_shapes,
            scratch_shapes=scratch + shared_scratch + shared_scratch, compiler_params=_params(vmem),
        )(q, k, v, small, gate_row, ec, en, em, do, dcf, dnf, dmf, *acc)

    return fwd_call, bwd_call


def _ml_pair(name, q, k, v, small, gate_row, state_f, state_b):
    calls = [_ml_calls(name + "%d" % d, q, d, d == 1) for d in range(2)]

    def run_fwd(q, k, v, small, gate_row, state_f, state_b):
        res_f = calls[0][0](q, k, v, small, gate_row, *state_f)
        res_b = calls[1][0](q, k, v, small, gate_row, *state_b)
        return (res_f[0], res_b[0], tuple(res_f[1:4]), tuple(res_b[1:4])), (tuple(res_f[4:]), tuple(res_b[4:]))

    @jax.custom_vjp
    def op(*args):
        return run_fwd(*args)[0]

    def op_fwd(*args):
        outs, enters = run_fwd(*args)
        return outs, (args[:5], enters)

    def op_bwd(res, g):
        (q, k, v, small, gate_row), (en_f, en_b) = res
        do_f, do_b, ds_f, ds_b = g
        dq, dk, dv, dsm, dgr_f, *d0_f = calls[0][1](q, k, v, small, gate_row, *en_f, do_f, *ds_f)
        dq, dk, dv, dsm, dgr_b, *d0_b = calls[1][1](q, k, v, small, gate_row, *en_b, do_b, *ds_b, acc=(dq, dk, dv, dsm))
        return dq, dk, dv, dsm, dgr_f + dgr_b, tuple(d0_f), tuple(d0_b)

    op.defvjp(op_fwd, op_bwd)
    return op(q, k, v, small, gate_row, tuple(state_f), tuple(state_b))


def _f_modulate(x, shift, scale):
    return (_layernorm_rows(x) * (1.0 + scale) + shift,)


def _f_resid_ln(x, o, gate, bias, ln_g, ln_b):
    return (_layernorm_rows(DN_ALPHA * x + gate * (o + bias)) * ln_g + ln_b,)


def _f_lru_gates(xc, w_r, b_r, w_i, b_i, lam):
    outs = []
    for d in range(2):
        def blockdiag(w):
            return jnp.concatenate(
                [_mm_nn(xc[:, n * LRU_BS:(n + 1) * LRU_BS], w[(d * LRU_BLOCKS + n) * LRU_BS:(d * LRU_BLOCKS + n + 1) * LRU_BS, :])
                 for n in range(LRU_BLOCKS)], axis=1)

        r = jax.nn.sigmoid(blockdiag(w_r) + b_r[d:d + 1])
        i = jax.nn.sigmoid(blockdiag(w_i) + b_i[d:d + 1])
        log_a = -LRU_C * r * jax.nn.softplus(-lam[d:d + 1])
        outs += [jnp.exp(log_a), jnp.sqrt(1.0 - jnp.exp(2.0 * log_a)) * i * xc]
    return tuple(outs)


def _f_lru_out(h_f, h_b, ly):
    return ((h_f + h_b) * jax.nn.gelu(ly),)


def _f_ssd_post(y_f, y_b, xs, z, d_exp, norm_w):
    y = (y_f + y_b + xs * d_exp) * jax.nn.silu(z)
    gw = SSD_INNER // SSD_GROUPS
    parts = []
    for g in range(SSD_GROUPS):
        yg = y[:, g * gw:(g + 1) * gw]
        parts.append(yg * lax.rsqrt(jnp.mean(jnp.square(yg), -1, keepdims=True) + LN_EPS))
    return (jnp.concatenate(parts, axis=1) * norm_w,)


def _f_ml_post(h_f, h_b, o, norm_w):
    h = h_f + h_b
    parts = [_layernorm_rows(h[:, i * ML_HD:(i + 1) * ML_HD]) for i in range(ML_HEADS)]
    return (jnp.concatenate(parts, axis=1) * norm_w * jax.nn.sigmoid(o),)


def _f_merge(ga, gb, gc, pa, pb, pc):
    return (jax.nn.sigmoid(ga) * pa + jax.nn.sigmoid(gb) * pb + jax.nn.sigmoid(gc) * pc,)


def _f_relu2(pre, bias):
    return (jnp.square(jax.nn.relu(pre + bias)),)


def _lane_row(vec, start):
    return jnp.pad(vec[None], ((0, 0), (start, LANES - start - vec.shape[0])))


def _mixer(tag, x_tok, shift, scale, p, states):
    (lru_s, ssd_s, ml_s) = states
    lx, ly, sz, xs, bm, cm, mq, mk, mv, mo, ga, gb, gc, small = _rowwise_linear(
        tag + "in", _f_modulate, [x_tok], [shift, scale], [(p['w_in_main'], _IN_MAIN_WIDTHS), (p['w_in_small'], None)])

    xc = _dwconv(tag + "lruconv", lx, p['lru_conv_w'], p['lru_conv_b'][None], False)
    a_f, b_f, a_b, b_b = _rowwise(
        tag + "lrugate", _f_lru_gates, [xc],
        [p['lru_w_r'].reshape(2 * LRU_BLOCKS * LRU_BS, LRU_BS), p['lru_b_r'], p['lru_w_i'].reshape(2 * LRU_BLOCKS * LRU_BS, LRU_BS),
         p['lru_b_i'], p['lru_lambda']], [D_MODEL] * 4)
    h_f, s_f = _lin_scan(tag + "lruscanf", a_f, b_f, lru_s[0], False)
    h_b, s_b = _lin_scan(tag + "lruscanb", a_b, b_b, lru_s[1], True)
    (pa,) = _rowwise_linear(tag + "bra", _f_lru_out, [h_f, h_b, ly], [], [(p['w_br_a'], None)], to_linear=(2,))

    cw, cb_ = p['ssd_conv_w'], p['ssd_conv_b'][None]
    xs_c = _dwconv(tag + "ssdconvx", xs, cw[:, :2048], cb_[:, :2048], True)
    bm_c = _dwconv(tag + "ssdconvb", bm, cw[:, 2048:3072], cb_[:, 2048:3072], True)
    cm_c = _dwconv(tag + "ssdconvc", cm, cw[:, 3072:], cb_[:, 3072:], True)
    dir_rows = [(_lane_row(p['ssd_dt_bias'][d], _DT_LANE + 32 * d), _lane_row(p['ssd_a_log'][d], _DT_LANE + 32 * d)) for d in range(2)]
    *ys, st_f, st_b = _ssd_pair(tag + "ssd", xs_c, bm_c, cm_c, small, dir_rows[0], dir_rows[1], ssd_s[0], ssd_s[1])
    ssd_new = (st_f, st_b)
    (pb,) = _rowwise_linear(tag + "brb", _f_ssd_post, [ys[0], ys[1], xs_c, sz],
                            [jnp.repeat(p['ssd_d'], SSD_HEADDIM)[None], p['ssd_norm_w'][None]], [(p['w_br_b'], None)], to_linear=(3,))

    mw, mb = p['ml_conv_w'], p['ml_conv_b'][None]
    q_c = _dwconv(tag + "mlconvq", mq, mw[:, :1024], mb[:, :1024], True)
    k_c = _dwconv(tag + "mlconvk", mk, mw[:, 1024:], mb[:, 1024:], True)
    gate_row = _lane_row(p['ml_gate_b'].reshape(4 * ML_HEADS), _MG_LANE)
    *hs, ml_f, ml_b = _ml_pair(tag + "ml", q_c, k_c, mv, small, gate_row, ml_s[0], ml_s[1])
    ml_new = (ml_f, ml_b)
    (pc,) = _rowwise_linear(tag + "brc", _f_ml_post, [hs[0], hs[1], mo], [p['ml_norm_w'][None]], [(p['w_br_c'], None)], to_linear=(2,))
    return (ga, gb, gc, pa, pb, pc), ((s_f, s_b), tuple(ssd_new), tuple(ml_new))


def _merge(tag, br, p):
    return _rowwise_linear(tag + "out", _f_merge, list(br), [], [(p['w_out'], None)], to_linear=tuple(range(6)))[0]


def _sublayers(tag, xin, o, mods, p, l):
    sh2, sc2, g1, g2 = mods
    (x1,) = _rowwise(tag + "ln1", _f_resid_ln, [xin, o], [g1, p['b_out'][None], p['ln1_g'][None], p['ln1_b'][None]], [D_MODEL], to_linear=(1,))
    (pre,) = _rowwise_linear(tag + "ff1", _f_modulate, [x1], [sh2, sc2], [(p['w_ff1'], None)])
    (o2,) = _rowwise_linear(tag + "ff2", _f_relu2, [pre], [p['b_ff1'][None]], [(p['w_ff2'], None)], to_linear=(0,))
    (x2,) = _rowwise(tag + "ln2", _f_resid_ln, [x1, o2], [g2, p['b_ff2'][None], p['ln2_g'][None], p['ln2_b'][None]], [D_MODEL], to_linear=(1,))
    return x2


def _to_col_major(h):
    s, d = h.shape
    return h.reshape(s // GRID_W, GRID_W, d).swapaxes(0, 1).reshape(s, d)


def _from_col_major(h):
    s, d = h.shape
    return h.reshape(GRID_W, s // GRID_W, d).swapaxes(0, 1).reshape(s, d)


def _forward(x, wts, mods, ctx):
    zeros = lambda *s: jnp.zeros(s, F32)
    ctx_init = ((zeros(1, D_MODEL), zeros(1, D_MODEL)),
                (zeros(SSD_INNER, SSD_STATE), zeros(SSD_INNER, SSD_STATE)),
                tuple((zeros(ML_HEADS * ML_HD, ML_HD), zeros(ML_HEADS, 1, ML_HD), zeros(ML_HEADS, 1, LANES)) for _ in range(2)))
    for l in range(DEPTH):
        p = {n: wts[n][l] for n in wts}
        tag = "l%d" % l
        sh1x, sc1x, g1x, sh2x, sc2x, g2x = [mods[l][0][:, i * D_MODEL:(i + 1) * D_MODEL] for i in range(6)]
        sh1c, sc1c, g1c, sh2c, sc2c, g2c = [mods[l][1][:, i * D_MODEL:(i + 1) * D_MODEL] for i in range(6)]
        br_c, ctx_states = _mixer(tag + "c", ctx, sh1c, sc1c, p, ctx_init)
        br_x, _ = _mixer(tag + "x", _to_col_major(x) if l % 2 == 1 else x, sh1x, sc1x, p, ctx_states)
        ox = _merge(tag + "x", br_x, p)
        if l % 2 == 1:
            ox = _from_col_major(ox)
        x = _sublayers(tag + "x", x, ox, (sh2x, sc2x, g1x, g2x), p, l)
        if l < DEPTH - 1:
            ctx = _sublayers(tag + "c", ctx, _merge(tag + "c", br_c, p), (sh2c, sc2c, g1c, g2c), p, l)
    return x


_ADA_ROWS = 2 * SUBLANES


def _ada_forward(c, c_ctx, w_ada, b_ada, me):
    c_all = _exchange("gather_c", jnp.broadcast_to(c, (SUBLANES, D_MODEL)), True)[:, 0]

    def rows_of(c_ctx_):
        pad = jnp.zeros((_ADA_ROWS - N_DEV - 1, D_MODEL), F32)
        return jax.nn.silu(jnp.concatenate([c_all, c_ctx_[None], pad], axis=0))

    rows, vjp_rows = jax.vjp(rows_of, c_ctx)
    cols, vjp_cols = jax.vjp(lambda r, w: jnp.stack([_linear("ada%d" % l, r, w[l]) for l in range(DEPTH)]), rows, w_ada)
    full = _exchange("gather_mod", cols, True).transpose(1, 2, 0, 3).reshape(DEPTH, _ADA_ROWS, 6 * D_MODEL) + b_ada[:, None, :]
    mods = [(lax.dynamic_slice_in_dim(full[l], me, 1, axis=0), full[l][N_DEV:N_DEV + 1]) for l in range(DEPTH)]
    return mods, (vjp_rows, vjp_cols)


def _ada_backward(saved, dmods):
    vjp_rows, vjp_cols = saved
    wcol = 6 * D_MODEL // N_DEV
    pad = jnp.zeros((SUBLANES - 2, 6 * D_MODEL), F32)
    both = jnp.stack([jnp.concatenate([dx, dc, pad], axis=0) for dx, dc in dmods])
    send = both.reshape(DEPTH, SUBLANES, N_DEV, wcol).transpose(2, 0, 1, 3)
    recv = _exchange("scatter_dmod", send, False)
    ctx_row = recv[0, :, 1]
    for k in range(1, N_DEV):
        ctx_row = ctx_row + recv[k, :, 1]
    g = jnp.concatenate([recv[:, :, 0].transpose(1, 0, 2), ctx_row[:, None],
                         jnp.zeros((DEPTH, _ADA_ROWS - N_DEV - 1, wcol), F32)], axis=1)
    d_rows, d_w = vjp_cols(g)
    (d_c_ctx,) = vjp_rows(d_rows)
    d_b = jnp.stack([(dx + dc)[0] for dx, dc in dmods])
    return d_w, d_b, d_c_ctx


def _loss_and_cotangent(y, target):
    n_rows, d = y.shape
    tt = _row_tile(n_rows, 0, cap=256)

    def kern(y_ref, t_ref, dy_ref, acc_ref):
        @pl.when(pl.program_id(0) == 0)
        def _():
            acc_ref[...] = jnp.zeros_like(acc_ref)

        err = y_ref[...] - t_ref[...]
        dy_ref[...] = err * (1.0 / d)
        acc_ref[...] += jnp.sum(jnp.square(err))

    spec = pl.BlockSpec((tt, d), lambda i: (i, 0))
    dy, acc = pl.pallas_call(
        kern, grid=(n_rows // tt,), name="loss", in_specs=[spec, spec],
        out_specs=[spec, pl.BlockSpec((SUBLANES, LANES), lambda i: (0, 0))],
        out_shape=[jax.ShapeDtypeStruct((n_rows, d), F32), jax.ShapeDtypeStruct((SUBLANES, LANES), F32)],
    )(y, target)
    return acc[0, 0] * (0.5 / d), dy


def _exchange(name, src, gather):
    slab = src.shape if gather else src.shape[1:]

    def body(src_ref, out_ref, send_sems, recv_sems, local_sem):
        x, y, c = lax.axis_index("x"), lax.axis_index("y"), lax.axis_index("c")
        me = 4 * x + 2 * y + c
        local = pltpu.make_async_copy(src_ref if gather else src_ref.at[me], out_ref.at[me], local_sem)
        local.start()
        copies = []
        for d in range(1, N_DEV):
            px, py, pc = lax.rem(x + (d >> 2), 2), lax.rem(y + ((d >> 1) & 1), 2), lax.rem(c + (d & 1), 2)
            peer = 4 * px + 2 * py + pc
            cp = pltpu.make_async_remote_copy(
                src_ref=src_ref if gather else src_ref.at[peer], dst_ref=out_ref.at[me],
                send_sem=send_sems.at[d - 1], recv_sem=recv_sems.at[d - 1],
                device_id=(px, py, pc), device_id_type=pl.DeviceIdType.MESH)
            cp.start()
            copies.append(cp)
        for cp in copies:
            cp.wait()
        local.wait()

    return pl.pallas_call(
        body, name=name, out_shape=jax.ShapeDtypeStruct((N_DEV,) + tuple(slab), src.dtype),
        in_specs=[pl.BlockSpec(memory_space=pl.ANY)], out_specs=pl.BlockSpec(memory_space=pl.ANY),
        scratch_shapes=[pltpu.SemaphoreType.DMA((N_DEV - 1,)), pltpu.SemaphoreType.DMA((N_DEV - 1,)), pltpu.SemaphoreType.DMA],
    )(src)


_HBM = pl.BlockSpec(memory_space=pl.ANY)
_CHIPS = ((0, 0), (0, 1), (1, 0), (1, 1))


def _gather_two_level(name, srcs):
    n = len(srcs)

    def body(*refs):
        src_refs, out_refs = refs[:n], refs[n:2 * n]
        send_sems, recv_sems, local_sems = refs[2 * n:]
        x, y, c = lax.axis_index("x"), lax.axis_index("y"), lax.axis_index("c")
        me, sibling = (x, y, c), (x, y, 1 - c)
        chips = [(1 - x, y), (x, 1 - y), (1 - x, 1 - y)]

        def slab(a, px, py, pc):
            return out_refs[a].at[4 * px + 2 * py + pc]

        def copy(a, k, block, to, own=False):
            return pltpu.make_async_remote_copy(
                src_ref=src_refs[a] if own else slab(a, *block), dst_ref=slab(a, *block), send_sem=send_sems.at[a, k],
                recv_sem=recv_sems.at[a, k], device_id=to, device_id_type=pl.DeviceIdType.MESH)

        mine = [pltpu.make_async_copy(src_refs[a], slab(a, *me), local_sems.at[a]) for a in range(n)]
        first = []
        for a in range(n):
            mine[a].start()
            first += [copy(a, 0, me, sibling, own=True)] + [copy(a, 1 + j, me, (*chip, c), own=True) for j, chip in enumerate(chips)]
        for cp in first:
            cp.start()
        passed = []
        for j, chip in enumerate(chips):
            for a in range(n):
                copy(a, 1 + j, (*chip, c), me).wait_recv()
                passed.append(copy(a, 4 + j, (*chip, c), sibling))
                passed[-1].start()
        for a in range(n):
            copy(a, 0, sibling, me).wait_recv()
        for j, chip in enumerate(chips):
            for a in range(n):
                copy(a, 4 + j, (*chip, 1 - c), me).wait_recv()
        for cp in first + passed:
            cp.wait_send()
        for cp in mine:
            cp.wait()

    return pl.pallas_call(
        body, name=name, out_shape=[jax.ShapeDtypeStruct((N_DEV,) + tuple(s.shape), s.dtype) for s in srcs],
        in_specs=[_HBM] * n, out_specs=[_HBM] * n,
        scratch_shapes=[pltpu.SemaphoreType.DMA((n, N_DEV - 1)), pltpu.SemaphoreType.DMA((n, N_DEV - 1)), pltpu.SemaphoreType.DMA((n,))],
    )(*srcs)


def _scatter_to_sibling(name, parts_list):
    n = len(parts_list)

    def body(*refs):
        p_refs, out_refs = refs[:n], refs[n:2 * n]
        send_sems, recv_sems = refs[2 * n:]
        x, y, c = lax.axis_index("x"), lax.axis_index("y"), lax.axis_index("c")
        copies = []
        for a in range(n):
            for j, (px, py) in enumerate(_CHIPS):
                cp = pltpu.make_async_remote_copy(
                    src_ref=p_refs[a].at[4 * px + 2 * py + (1 - c)], dst_ref=out_refs[a].at[j], send_sem=send_sems.at[a, j],
                    recv_sem=recv_sems.at[a, j], device_id=(x, y, 1 - c), device_id_type=pl.DeviceIdType.MESH)
                cp.start()
                copies.append(cp)
        for cp in copies:
            cp.wait()

    return pl.pallas_call(
        body, name=name, out_shape=[jax.ShapeDtypeStruct((4,) + tuple(p.shape[1:]), p.dtype) for p in parts_list],
        in_specs=[_HBM] * n, out_specs=[_HBM] * n,
        scratch_shapes=[pltpu.SemaphoreType.DMA((n, 4)), pltpu.SemaphoreType.DMA((n, 4))],
    )(*parts_list)


def _chip_sum(name, parts, from_sibling):
    _, rows, cols = parts.shape
    lanes = -(-cols // LANES) * LANES
    tr = _row_tile(rows, 4 * lanes * 4 * 2, budget=12 << 20)

    def kern(p_ref, s_ref, o_ref):
        c = lax.axis_index("c")
        o_ref[0] = (jnp.where(c == 0, p_ref[0, 0], p_ref[0, 1]) + s_ref[0]).astype(o_ref.dtype)

    return pl.pallas_call(
        kern, grid=(4, rows // tr), name=name,
        in_specs=[pl.BlockSpec((1, 2, tr, cols), lambda j, i: (j, 0, i, 0)), pl.BlockSpec((1, tr, cols), lambda j, i: (j, i, 0))],
        out_specs=pl.BlockSpec((1, tr, cols), lambda j, i: (j, i, 0)),
        out_shape=jax.ShapeDtypeStruct((4, rows, cols), BF16),
        compiler_params=_params(4 * lanes * tr * 4 * 2),
    )(parts.reshape(4, 2, rows, cols), from_sibling)


def _scatter_across_chips(name, sums_list):
    n = len(sums_list)

    def body(*refs):
        q_refs, out_refs = refs[:n], refs[n:2 * n]
        send_sems, recv_sems, local_sems = refs[2 * n:]
        x, y, c = lax.axis_index("x"), lax.axis_index("y"), lax.axis_index("c")
        own = 2 * x + y
        copies = []
        for a in range(n):
            local = pltpu.make_async_copy(q_refs[a].at[own], out_refs[a].at[own], local_sems.at[a])
            local.start()
            copies.append(local)
            for d in range(1, 4):
                px, py = lax.rem(x + (d >> 1), 2), lax.rem(y + (d & 1), 2)
                cp = pltpu.make_async_remote_copy(
                    src_ref=q_refs[a].at[2 * px + py], dst_ref=out_refs[a].at[own], send_sem=send_sems.at[a, d - 1],
                    recv_sem=recv_sems.at[a, d - 1], device_id=(px, py, c), device_id_type=pl.DeviceIdType.MESH)
                cp.start()
                copies.append(cp)
        for cp in copies:
            cp.wait()

    return pl.pallas_call(
        body, name=name, out_shape=[jax.ShapeDtypeStruct(s.shape, s.dtype) for s in sums_list],
        in_specs=[_HBM] * n, out_specs=[_HBM] * n,
        scratch_shapes=[pltpu.SemaphoreType.DMA((n, 3)), pltpu.SemaphoreType.DMA((n, 3)), pltpu.SemaphoreType.DMA((n,))],
    )(*sums_list)


def _sum_parts(name, parts):
    n_parts, rows, cols = parts.shape
    tr = _row_tile(rows, 4 * cols * (n_parts + 1) * 2)

    def kern(p_ref, o_ref):
        acc = p_ref[0]
        for k in range(1, n_parts):
            acc = acc + p_ref[k]
        o_ref[...] = acc

    return pl.pallas_call(
        kern, grid=(rows // tr,), name=name, in_specs=[pl.BlockSpec((n_parts, tr, cols), lambda i: (0, i, 0))],
        out_specs=pl.BlockSpec((tr, cols), lambda i: (i, 0)), out_shape=jax.ShapeDtypeStruct((rows, cols), F32),
    )(parts)


def _adamw(name, w, m, v, parts):
    n_parts, rows, cols = parts.shape
    lanes = -(-cols // LANES) * LANES
    tr = _row_tile(rows, 4 * lanes * (n_parts + 7) * 2, budget=16 << 20)
    c1 = np.float32(1.0 - ADAM_B1 ** ADAM_STEP)
    c2 = np.float32(1.0 - ADAM_B2 ** ADAM_STEP)

    def kern(w_ref, m_ref, v_ref, p_ref, g_ref, d_ref, nm_ref, nv_ref):
        g = p_ref[0].astype(F32)
        for k in range(1, n_parts):
            g = g + p_ref[k].astype(F32)
        m_new = ADAM_B1 * m_ref[...] + (1.0 - ADAM_B1) * g
        v_new = ADAM_B2 * v_ref[...] + (1.0 - ADAM_B2) * jnp.square(g)
        g_ref[...] = g
        nm_ref[...] = m_new
        nv_ref[...] = v_new
        d_ref[...] = -ADAM_LR * ((m_new / c1) / (jnp.sqrt(v_new / c2) + ADAM_EPS) + ADAM_WD * w_ref[...])

    spec = pl.BlockSpec((tr, cols), lambda i: (i, 0))
    return pl.pallas_call(
        kern, grid=(rows // tr,), name=name,
        in_specs=[spec, spec, spec, pl.BlockSpec((n_parts, tr, cols), lambda i: (0, i, 0))], out_specs=[spec] * 4,
        out_shape=[jax.ShapeDtypeStruct((rows, cols), F32)] * 4,
        compiler_params=_params(4 * lanes * tr * (n_parts + 7) * 2),
    )(w, m, v, parts)


def _packed_rows(shape):
    return -(-int(np.prod(shape)) // (SUBLANES * LANES)) * SUBLANES


def _pack(arrays, row_multiple):
    parts = []
    for a in arrays:
        n = int(np.prod(a.shape))
        r = _packed_rows(a.shape)
        parts.append(jnp.pad(a.reshape(-1), (0, r * LANES - n)).reshape(r, LANES))
    rows = sum(p.shape[0] for p in parts)
    total = -(-rows // row_multiple) * row_multiple
    if total > rows:
        parts.append(jnp.zeros((total - rows, LANES), arrays[0].dtype))
    return jnp.concatenate(parts, axis=0)


def _unpack(packed, shapes):
    out, off = [], 0
    for s in shapes:
        r = _packed_rows(s)
        out.append(packed[off:off + r].reshape(-1)[:int(np.prod(s))].reshape(s))
        off += r
    return out


def _split_w_in(w_in):
    main = jnp.concatenate([w_in[:, :, s:e] for s, e in _IN_MAIN], axis=2)
    pad = jnp.zeros(w_in.shape[:2] + (LANES - 80,), w_in.dtype)
    small = jnp.concatenate([w_in[:, :, s:e] for s, e in _IN_SMALL] + [pad], axis=2)
    return main, small


def _join_w_in(main, small):
    return jnp.concatenate([main[:, :, 0:8192], small[:, :, 0:64], main[:, :, 8192:12288], small[:, :, 64:80],
                            main[:, :, 12288:15360]], axis=2)


def _unshard(gathered, axis):
    nd, nl, r, c = gathered.shape
    if axis == 1:
        return gathered.transpose(1, 0, 2, 3).reshape(nl, nd * r, c)
    return gathered.transpose(1, 2, 0, 3).reshape(nl, r, nd * c)


def _reshard(full, axis):
    nl, r, c = full.shape
    if axis == 1:
        return full.reshape(nl, N_DEV, r // N_DEV, c).transpose(1, 0, 2, 3)
    return full.reshape(nl, r, N_DEV, c // N_DEV).transpose(2, 0, 1, 3)


def kernel(x, c, ctx, c_ctx, w_ada, b_ada, w_in, lru_conv_w, lru_conv_b, lru_w_r, lru_b_r, lru_w_i, lru_b_i, lru_lambda, ssd_conv_w, ssd_conv_b, ssd_dt_bias, ssd_a_log, ssd_d, ssd_norm_w, ml_conv_w, ml_conv_b, ml_gate_b, ml_norm_w, w_br_a, w_br_b, w_br_c, w_out, b_out, ln1_g, ln1_b, w_ff1, b_ff1, w_ff2, b_ff2, ln2_g, ln2_b, loss_target, m_c_ctx, m_w_ada, m_b_ada, m_w_in, m_lru_conv_w, m_lru_conv_b, m_lru_w_r, m_lru_b_r, m_lru_w_i, m_lru_b_i, m_lru_lambda, m_ssd_conv_w, m_ssd_conv_b, m_ssd_dt_bias, m_ssd_a_log, m_ssd_d, m_ssd_norm_w, m_ml_conv_w, m_ml_conv_b, m_ml_gate_b, m_ml_norm_w, m_w_br_a, m_w_br_b, m_w_br_c, m_w_out, m_b_out, m_ln1_g, m_ln1_b, m_w_ff1, m_b_ff1, m_w_ff2, m_b_ff2, m_ln2_g, m_ln2_b, v_c_ctx, v_w_ada, v_b_ada, v_w_in, v_lru_conv_w, v_lru_conv_b, v_lru_w_r, v_lru_b_r, v_lru_w_i, v_lru_b_i, v_lru_lambda, v_ssd_conv_w, v_ssd_conv_b, v_ssd_dt_bias, v_ssd_a_log, v_ssd_d, v_ssd_norm_w, v_ml_conv_w, v_ml_conv_b, v_ml_gate_b, v_ml_norm_w, v_w_br_a, v_w_br_b, v_w_br_c, v_w_out, v_b_out, v_ln1_g, v_ln1_b, v_w_ff1, v_b_ff1, v_w_ff2, v_b_ff2, v_ln2_g, v_ln2_b):
    a = dict(locals())
    me = 4 * lax.axis_index("x") + 2 * lax.axis_index("y") + lax.axis_index("c")

    wts = {n: a[n] for n in _REPLICATED if n not in ('c_ctx', 'b_ada')}
    exchanged = [n for n in _BIG if n != 'w_ada']
    gathered = _gather_two_level("gather_weights", [a[n].astype(BF16) for n in exchanged])
    for n, g in zip(exchanged, gathered):
        full = _unshard(g, _BIG[n])
        if n == 'w_in':
            main, small = _split_w_in(full)
            wts['w_in_main'], wts['w_in_small'] = main.astype(F32), small.astype(F32)
        else:
            wts[n] = full.astype(F32)
    small_shapes = [a[n].shape for n in _SMALL_SHARDED]
    small_all = _exchange("gather_small", _pack([a[n] for n in _SMALL_SHARDED], SUBLANES), True)
    per_dev = [_unpack(small_all[k], small_shapes) for k in range(N_DEV)]
    for i, n in enumerate(_SMALL_SHARDED):
        wts[n] = jnp.concatenate([per_dev[k][i] for k in range(N_DEV)], axis=-1)

    mods, ada_saved = _ada_forward(c, c_ctx, w_ada, b_ada, me)
    y, vjp = jax.vjp(functools.partial(_forward, ctx=ctx[0]), x[0], wts, mods)
    loss_local, dy = _loss_and_cotangent(y, loss_target[0])
    grad_x, grads, dmods = vjp(dy)
    grads['w_in'] = _join_w_in(grads.pop('w_in_main'), grads.pop('w_in_small'))
    grad_w_ada, grads['b_ada'], grads['c_ctx'] = _ada_backward(ada_saved, dmods)
    loss = lax.psum(loss_local, ("x", "y", "c"))

    out = {}

    def put(n, res, shape):
        for kind, r in zip(("grad_", "delta_", "new_m_", "new_v_"), res):
            out[kind + n] = r.reshape(shape)

    flat = {n: (a[n].shape[0] * a[n].shape[1], a[n].shape[2]) for n in _BIG}
    by_dest = [_reshard(grads[n], _BIG[n]).reshape(N_DEV, *flat[n]) for n in exchanged]
    from_sibling = _scatter_to_sibling("scatter_d2d", by_dest)
    chip_sums = [_chip_sum("chipsum_" + n, p, s) for n, p, s in zip(exchanged, by_dest, from_sibling)]
    summed = dict(zip(exchanged, _scatter_across_chips("scatter_ici", chip_sums)))
    summed['w_ada'] = grad_w_ada.reshape(1, *flat['w_ada'])
    for n in _BIG:
        shp = a[n].shape
        rows, cols = flat[n]
        parts = summed[n]
        put(n, _adamw("adamw_" + n, a[n].reshape(rows, cols), a["m_" + n].reshape(rows, cols), a["v_" + n].reshape(rows, cols), parts), shp)

    rep_names = _REPLICATED + _SMALL_SHARDED
    chunk_rows = SUBLANES * N_DEV
    g_pack = _pack([grads[n] for n in rep_names], chunk_rows * N_DEV)
    rows = g_pack.shape[0]
    parts = _exchange("scatter_rep", g_pack.reshape(N_DEV, rows // N_DEV, LANES), False)
    mine = _sum_parts("sum_rep", parts)
    g_all = _exchange("gather_rep", mine, True).reshape(rows, LANES)
    g_full = _unpack(g_all, [grads[n].shape for n in rep_names])
    g_local = []
    for n, g in zip(rep_names, g_full):
        if n in _SMALL_SHARDED:
            width = a[n].shape[-1]
            g = lax.dynamic_slice_in_dim(g, me * width, width, axis=g.ndim - 1)
        g_local.append(g)
    shapes = [a[n].shape for n in rep_names]
    res = _adamw("adamw_rep", _pack([a[n] for n in rep_names], chunk_rows), _pack([a["m_" + n] for n in rep_names], chunk_rows),
                 _pack([a["v_" + n] for n in rep_names], chunk_rows), _pack(g_local, chunk_rows)[None])
    unpacked = [_unpack(r, shapes) for r in res]
    for i, n in enumerate(rep_names):
        put(n, [u[i] for u in unpacked], shapes[i])

    outs = [loss, grad_x[None]]
    for kind in ("grad_", "delta_", "new_m_", "new_v_"):
        outs += [out[kind + n] for n in _WEIGHTS]
    return tuple(outs)
```

```python
import functools

import numpy as np
import jax
import jax.numpy as jnp
from jax import lax
from jax.experimental import pallas as pl
from jax.experimental.pallas import tpu as pltpu

F32 = jnp.float32
BF16 = jnp.bfloat16

N_DEV = 8
D_MODEL = 1024
DEPTH = 2
GRID_W = 64
CHUNK = 128
LN_EPS = 1e-6
LRU_BLOCKS = 8
LRU_BS = 128
LRU_C = 8.0
SSD_INNER = 2048
SSD_GROUPS = 8
SSD_HPG = 4
SSD_HEADDIM = 64
SSD_STATE = 128
ML_HEADS = 4
ML_HD = 256
D_FF = 4096
DN_ALPHA = (2 * DEPTH) ** 0.25
ADAM_LR, ADAM_B1, ADAM_B2, ADAM_EPS, ADAM_WD, ADAM_STEP = 0.001, 0.9, 0.999, 1e-08, 0.01, 10

VMEM_CAP = 60 * 1024 * 1024
SUBLANES = 8
LANES = 128

_IN_MAIN = ((0, 8192), (8256, 12352), (12368, 15440))
_IN_MAIN_WIDTHS = (1024, 1024, 2048, 2048, 1024, 1024, 1024, 1024, 1024, 1024, 1024, 1024, 1024)
_IN_SMALL = ((8192, 8256), (12352, 12368))
_DT_LANE = 0
_MG_LANE = 64

_WEIGHTS = ['c_ctx', 'w_ada', 'b_ada', 'w_in', 'lru_conv_w', 'lru_conv_b', 'lru_w_r', 'lru_b_r', 'lru_w_i', 'lru_b_i',
            'lru_lambda', 'ssd_conv_w', 'ssd_conv_b', 'ssd_dt_bias', 'ssd_a_log', 'ssd_d', 'ssd_norm_w', 'ml_conv_w',
            'ml_conv_b', 'ml_gate_b', 'ml_norm_w', 'w_br_a', 'w_br_b', 'w_br_c', 'w_out', 'b_out', 'ln1_g', 'ln1_b',
            'w_ff1', 'b_ff1', 'w_ff2', 'b_ff2', 'ln2_g', 'ln2_b']
_BIG = {'w_ada': 2, 'w_in': 2, 'w_ff1': 2, 'w_br_a': 1, 'w_br_b': 1, 'w_br_c': 1, 'w_out': 1, 'w_ff2': 1}
_SMALL_SHARDED = ['lru_conv_w', 'lru_b_r', 'lru_b_i', 'lru_lambda', 'ssd_conv_w', 'ml_conv_w']
_REPLICATED = [n for n in _WEIGHTS if n not in _BIG and n not in _SMALL_SHARDED]


def _params(vmem_bytes):
    return pltpu.CompilerParams(vmem_limit_bytes=int(min(max(2 * vmem_bytes, 32 << 20), VMEM_CAP)))


def _row_tile(n_rows, bytes_per_row, budget=6 << 20, cap=512):
    t = cap
    while t > SUBLANES and (t * bytes_per_row > budget or n_rows % t):
        t //= 2
    assert n_rows % t == 0, (n_rows, t)
    return t


def _dg(a, b, ca, cb):
    return lax.dot_general(a.astype(BF16), b.astype(BF16), (((ca,), (cb,)), ((), ())), preferred_element_type=F32)


def _make_bdot(ca, cb):
    @jax.custom_vjp
    def f(a, b):
        return _dg(a, b, ca, cb)

    def fwd(a, b):
        return _dg(a, b, ca, cb), (a, b)

    def bwd(res, g):
        a, b = res
        da = _dg(g, b, 1, 1 - cb) if ca == 1 else _dg(b, g, 1 - cb, 1)
        db = _dg(a, g, 1 - ca, 0) if cb == 0 else _dg(g, a, 0, 1 - ca)
        return da, db

    f.defvjp(fwd, bwd)
    return f


_mm_nn = _make_bdot(1, 0)
_mm_nt = _make_bdot(1, 1)
_mm_tn = _make_bdot(0, 0)


@jax.custom_vjp
def _round_bf16(x):
    return x.astype(BF16).astype(F32)


_round_bf16.defvjp(lambda x: (_round_bf16(x), None), lambda _, g: (g,))


def _exact_dot(a, b):
    return jnp.dot(a, b, precision=lax.Precision.HIGHEST, preferred_element_type=F32)


def _layernorm_rows(x):
    mu = jnp.mean(x, -1, keepdims=True)
    var = jnp.mean(jnp.square(x - mu), -1, keepdims=True)
    return (x - mu) * lax.rsqrt(var + LN_EPS)


def _rowwise_calls(name, f, rows, params, out_widths, out_dtype=F32, to_linear=()):
    drow_dtypes = [BF16 if i in to_linear else F32 for i in range(len(rows))]
    nr, npar, no = len(rows), len(params), len(out_widths)
    n_rows = rows[0].shape[0]
    row_w = [r.shape[1] for r in rows]
    par_bytes = sum(int(np.prod(p.shape)) * 4 for p in params)
    tile = _row_tile(n_rows, 4 * (2 * sum(row_w) + 2 * sum(out_widths)), budget=12 << 20)
    grid = (n_rows // tile,)

    def row_spec(w):
        return pl.BlockSpec((tile, w), lambda i: (i, 0))

    def par_spec(p):
        return pl.BlockSpec(p.shape, lambda i: (0, 0))

    vmem = 2 * tile * 4 * (2 * sum(row_w) + 3 * sum(out_widths)) + 4 * par_bytes

    def fwd_call(rows, params):
        def kern(*refs):
            outs = f(*[r[...] for r in refs[:nr + npar]])
            for r, o in zip(refs[nr + npar:], outs):
                r[...] = o.astype(out_dtype)

        return pl.pallas_call(
            kern, grid=grid, name=name + "_fwd",
            in_specs=[row_spec(w) for w in row_w] + [par_spec(p) for p in params],
            out_specs=[row_spec(w) for w in out_widths],
            out_shape=[jax.ShapeDtypeStruct((n_rows, w), out_dtype) for w in out_widths],
            compiler_params=_params(vmem),
        )(*rows, *params)

    def bwd_call(rows, params, gouts):
        def kern(*refs):
            ins = [r[...] for r in refs[:nr + npar]]
            gs = tuple(r[...] for r in refs[nr + npar:nr + npar + no])
            grads = jax.vjp(f, *ins)[1](gs)
            drefs = refs[nr + npar + no:]
            for k in range(nr):
                drefs[k][...] = grads[k].astype(drefs[k].dtype)

            @pl.when(pl.program_id(0) == 0)
            def _():
                for k in range(npar):
                    drefs[nr + k][...] = jnp.zeros_like(drefs[nr + k])

            for k in range(npar):
                drefs[nr + k][...] += grads[nr + k]

        res = pl.pallas_call(
            kern, grid=grid, name=name + "_bwd",
            in_specs=[row_spec(w) for w in row_w] + [par_spec(p) for p in params] + [row_spec(w) for w in out_widths],
            out_specs=[row_spec(w) for w in row_w] + [par_spec(p) for p in params],
            out_shape=[jax.ShapeDtypeStruct(r.shape, dt) for r, dt in zip(rows, drow_dtypes)]
            + [jax.ShapeDtypeStruct(p.shape, F32) for p in params],
            compiler_params=_params(vmem),
        )(*rows, *params, *gouts)
        return tuple(r.astype(F32) for r in res[:nr]), tuple(res[nr:])

    return fwd_call, bwd_call


def _rowwise(name, f, rows, params, out_widths, to_linear=()):
    rows, params = tuple(rows), tuple(params)
    fwd_call, bwd_call = _rowwise_calls(name, f, rows, params, out_widths, to_linear=to_linear)

    @jax.custom_vjp
    def op(rows, params):
        return tuple(fwd_call(rows, params))

    op.defvjp(lambda r, p: (tuple(fwd_call(r, p)), (r, p)), lambda res, g: bwd_call(res[0], res[1], g))
    return op(rows, params)


def _rowwise_linear(name, f, rows, params, weights, to_linear=()):
    rows, params = tuple(rows), tuple(params)
    ws = tuple(w for w, _ in weights)
    m, k = rows[0].shape[0], ws[0].shape[0]
    row_fwd, row_bwd = _rowwise_calls(name, f, rows, params, [k], BF16, to_linear)
    lin = [_linear_calls(name + "lin%d" % i, m, k, w.shape[1], wd, BF16) for i, (w, wd) in enumerate(weights)]
    counts = [len(c[3]) for c in lin]

    def fwd(rows, params, ws):
        (a,) = row_fwd(rows, params)
        outs = []
        for (fwd_call, _, _, _), w in zip(lin, ws):
            outs += list(fwd_call(a, w))
        return tuple(outs), a

    @jax.custom_vjp
    def op(rows, params, ws):
        return fwd(rows, params, ws)[0]

    def op_fwd(rows, params, ws):
        outs, a = fwd(rows, params, ws)
        return outs, (rows, params, ws, a)

    def op_bwd(res, g):
        rows, params, ws, a = res
        da, dws, off = None, [], 0
        for (_, dgrad_call, wgrad_call, _), w, cnt in zip(lin, ws, counts):
            gk = g[off:off + cnt]
            off += cnt
            d = dgrad_call(w, gk)
            da = d if da is None else da + d
            dws.append(wgrad_call(a, gk))
        drows, dparams = row_bwd(rows, params, (da,))
        return drows, dparams, tuple(dws)

    op.defvjp(op_fwd, op_bwd)
    return op(rows, params, ws)


def _group_ranges(widths, tn):
    starts, s = [], 0
    for w in widths:
        assert w % tn == 0, (w, tn)
        starts.append((s // tn, (s + w) // tn))
        s += w
    return starts, s // tn


def _group_tile(refs, ranges, row_tile, col_tile, i, j):
    out = []
    for ref, (s, e) in zip(refs, ranges):
        cols = pl.ds(pl.multiple_of((j - s) * col_tile, col_tile), col_tile)
        out.append(((j >= s) & (j < e), ref, cols))
    return [(p, lambda r=r, c=c: r.at[pl.ds(pl.multiple_of(i * row_tile, row_tile), row_tile), c]) for p, r, c in out]


def _linear_calls(name, m, k, n, widths, a_dtype):
    widths = (n,) if widths is None else tuple(widths)
    ng = len(widths)
    cast_a = a_dtype != BF16
    wide = all(wd % 1024 == 0 for wd in widths)
    tn = 128 if n < 256 else (1024 if wide else (256 if k > 2048 or n % 512 else 512))
    tm = _row_tile(m, 0, cap=1024 if k <= 2048 else 512)
    ranges, nt = _group_ranges(widths, tn)
    mt = m // tm
    tn_w = 1024 if (wide and k <= 1024) else (512 if all(wd % 512 == 0 for wd in widths) else min(tn, 256))
    tm_w = _row_tile(m, 0, cap=1024 if k <= 1024 else 512)
    ranges_w, nt_w = _group_ranges(widths, tn_w)
    mt_w = m // tm_w
    hbm = pl.BlockSpec(memory_space=pl.ANY)

    def fwd_call(a, w):
        n_steps = mt * nt

        def kern(a_ref, w_ref, *rest):
            outs, obuf, osem = rest[:ng], rest[ng], rest[ng + 1]
            a_bf = rest[ng + 2] if cast_a else a_ref
            i, j = pl.program_id(0), pl.program_id(1)
            step = i * nt + j
            slot = lax.rem(step, 2)

            def drain(sl):
                pltpu.make_async_copy(obuf.at[sl], outs[0].at[pl.ds(0, tm), pl.ds(0, tn)], osem.at[sl]).wait()

            if cast_a:
                @pl.when(j == 0)
                def _():
                    a_bf[...] = a_ref[...].astype(BF16)

            @pl.when(step >= 2)
            def _():
                drain(slot)

            obuf[slot] = jnp.dot(a_bf[...], w_ref[...], preferred_element_type=F32)
            for pred, window in _group_tile(outs, ranges, tm, tn, i, j):
                @pl.when(pred)
                def _(window=window):
                    pltpu.make_async_copy(obuf.at[slot], window(), osem.at[slot]).start()

            @pl.when(step == n_steps - 1)
            def _():
                drain(slot)
                if n_steps > 1:
                    drain(1 - slot)

        return pl.pallas_call(
            kern, grid=(mt, nt), name=name + "_fwd",
            in_specs=[pl.BlockSpec((tm, k), lambda i, j: (i, 0)), pl.BlockSpec((k, tn), lambda i, j: (0, j))],
            out_specs=[hbm] * ng,
            out_shape=[jax.ShapeDtypeStruct((m, wd), F32) for wd in widths],
            scratch_shapes=[pltpu.VMEM((2, tm, tn), F32), pltpu.SemaphoreType.DMA((2,))]
            + ([pltpu.VMEM((tm, k), BF16)] if cast_a else []),
            compiler_params=_params(10 * tm * k + 4 * k * tn + 8 * tm * tn),
        )(a, w.astype(BF16))

    def prefetched(gs, gbuf, gsem, rngs, row_tile, col_tile, step, n_steps, tile_of):
        slot = lax.rem(step, 2)

        def start(s_idx, sl):
            ii, jj = tile_of(s_idx)
            for pred, window in _group_tile(gs, rngs, row_tile, col_tile, ii, jj):
                @pl.when(pred)
                def _(window=window):
                    pltpu.make_async_copy(window(), gbuf.at[sl], gsem.at[sl]).start()

        @pl.when(step == 0)
        def _():
            start(step, slot)

        @pl.when(step + 1 < n_steps)
        def _():
            start(step + 1, 1 - slot)

        pltpu.make_async_copy(gs[0].at[pl.ds(0, row_tile), pl.ds(0, col_tile)], gbuf.at[slot], gsem.at[slot]).wait()
        return slot

    def dgrad_call(w, gouts):
        def kern(w_ref, *rest):
            gs, da, gbuf, gsem = rest[:ng], rest[ng], rest[ng + 1], rest[ng + 2]
            i, j = pl.program_id(0), pl.program_id(1)
            slot = prefetched(gs, gbuf, gsem, ranges, tm, tn, i * nt + j, mt * nt, lambda s: (s // nt, lax.rem(s, nt)))

            @pl.when(j == 0)
            def _():
                da[...] = jnp.zeros_like(da)

            da[...] += lax.dot_general(gbuf[slot].astype(BF16), w_ref[...], (((1,), (1,)), ((), ())), preferred_element_type=F32)

        return pl.pallas_call(
            kern, grid=(mt, nt), name=name + "_dgrad",
            in_specs=[pl.BlockSpec((k, tn), lambda i, j: (0, j))] + [hbm] * ng,
            out_specs=pl.BlockSpec((tm, k), lambda i, j: (i, 0)),
            out_shape=jax.ShapeDtypeStruct((m, k), F32),
            scratch_shapes=[pltpu.VMEM((2, tm, tn), BF16), pltpu.SemaphoreType.DMA((2,))],
            compiler_params=_params(12 * tm * k + 4 * k * tn + 10 * tm * tn),
        )(w.astype(BF16), *[g.astype(BF16) for g in gouts])

    def wgrad_call(a, gouts):
        def kern(a_ref, *rest):
            gs, dw, gbuf, gsem = rest[:ng], rest[ng], rest[ng + 1], rest[ng + 2]
            j, i = pl.program_id(0), pl.program_id(1)
            slot = prefetched(gs, gbuf, gsem, ranges_w, tm_w, tn_w, j * mt_w + i, mt_w * nt_w,
                              lambda s: (lax.rem(s, mt_w), s // mt_w))

            @pl.when(i == 0)
            def _():
                dw[...] = jnp.zeros_like(dw)

            dw[...] += lax.dot_general(a_ref[...].astype(BF16), gbuf[slot].astype(BF16), (((0,), (0,)), ((), ())),
                                       preferred_element_type=F32)

        return pl.pallas_call(
            kern, grid=(nt_w, mt_w), name=name + "_wgrad",
            in_specs=[pl.BlockSpec((tm_w, k), lambda j, i: (i, 0))] + [hbm] * ng,
            out_specs=pl.BlockSpec((k, tn_w), lambda j, i: (0, j)),
            out_shape=jax.ShapeDtypeStruct((k, n), F32),
            scratch_shapes=[pltpu.VMEM((2, tm_w, tn_w), BF16), pltpu.SemaphoreType.DMA((2,))],
            compiler_params=_params(12 * tm_w * k + 12 * k * tn_w + 10 * tm_w * tn_w),
        )(a, *[g.astype(BF16) for g in gouts])

    return fwd_call, dgrad_call, wgrad_call, widths


def _linear(name, a, w, widths=None):
    fwd_call, dgrad_call, wgrad_call, _ = _linear_calls(name, a.shape[0], a.shape[1], w.shape[1], widths, a.dtype)

    @jax.custom_vjp
    def op(a, w):
        return tuple(fwd_call(a, w))

    op.defvjp(lambda a, w: (tuple(fwd_call(a, w)), (a, w)),
              lambda res, g: (dgrad_call(res[1], g), wgrad_call(res[0], g)))
    out = op(a, w)
    return out[0] if widths is None else out


def _conv_taps(x_ext, w, n_ext):
    xm2 = pltpu.roll(x_ext, 2, 0)
    xm1 = pltpu.roll(x_ext, 1, 0)
    xp1 = pltpu.roll(x_ext, n_ext - 1, 0)
    return xm2, xm1, xp1


def _dwconv(name, x, w, b, act):
    n_rows, ch = x.shape
    tt = _row_tile(n_rows, 4 * 6 * ch, budget=12 << 20, cap=256)
    nt = n_rows // tt
    n_ext = tt + 2 * SUBLANES
    per8 = tt // SUBLANES
    last8 = n_rows // SUBLANES - 1
    main = pl.BlockSpec((tt, ch), lambda i: (i, 0))
    prev = pl.BlockSpec((SUBLANES, ch), lambda i: (jnp.maximum(i * per8 - 1, 0), 0))
    nxt = pl.BlockSpec((SUBLANES, ch), lambda i: (jnp.minimum((i + 1) * per8, last8), 0))
    wspec = pl.BlockSpec((4, ch), lambda i: (0, 0))
    bspec = pl.BlockSpec((1, ch), lambda i: (0, 0))
    vmem = 4 * n_ext * ch * 14

    def ext(main_ref, prev_ref, next_ref):
        i = pl.program_id(0)
        p = jnp.where(i > 0, prev_ref[...], 0.0)
        q = jnp.where(i < nt - 1, next_ref[...], 0.0)
        return jnp.concatenate([p, main_ref[...], q], axis=0)

    def pre_of(x_ext, wv, bv):
        xm2, xm1, xp1 = _conv_taps(x_ext, wv, n_ext)
        pre = wv[0:1] * xm2 + wv[1:2] * xm1 + wv[2:3] * x_ext + wv[3:4] * xp1 + bv
        return pre, (xm2, xm1, xp1)

    def fwd_call(x, w, b):
        def kern(xm, xp, xn, w_ref, b_ref, o_ref):
            pre, _ = pre_of(ext(xm, xp, xn), w_ref[...], b_ref[...])
            pre = pre[SUBLANES:SUBLANES + tt]
            o_ref[...] = pre * jax.nn.sigmoid(pre) if act else pre

        return pl.pallas_call(
            kern, grid=(nt,), name=name + "_fwd", in_specs=[main, prev, nxt, wspec, bspec], out_specs=main,
            out_shape=jax.ShapeDtypeStruct((n_rows, ch), F32), compiler_params=_params(vmem),
        )(x, x, x, w, b)

    def bwd_call(x, w, b, dy):
        def kern(xm, xp, xn, gm, gp, gn, w_ref, b_ref, dx_ref, dw_ref, db_ref):
            wv = w_ref[...]
            x_ext = ext(xm, xp, xn)
            pre, (xm2, xm1, xp1) = pre_of(x_ext, wv, b_ref[...])
            dpre = ext(gm, gp, gn)
            if act:
                sg = jax.nn.sigmoid(pre)
                dpre = dpre * (sg + pre * sg * (1.0 - sg))
            dx = (wv[0:1] * pltpu.roll(dpre, n_ext - 2, 0) + wv[1:2] * pltpu.roll(dpre, n_ext - 1, 0)
                  + wv[2:3] * dpre + wv[3:4] * pltpu.roll(dpre, 1, 0))
            sl = slice(SUBLANES, SUBLANES + tt)
            dx_ref[...] = dx[sl].astype(dx_ref.dtype)
            dm = dpre[sl]

            @pl.when(pl.program_id(0) == 0)
            def _():
                dw_ref[...] = jnp.zeros_like(dw_ref)
                db_ref[...] = jnp.zeros_like(db_ref)

            dw_ref[...] += jnp.concatenate(
                [jnp.sum(dm * t[sl], axis=0, keepdims=True) for t in (xm2, xm1, x_ext, xp1)], axis=0)
            db_ref[...] += jnp.sum(dm, axis=0, keepdims=True)

        return pl.pallas_call(
            kern, grid=(nt,), name=name + "_bwd", in_specs=[main, prev, nxt, main, prev, nxt, wspec, bspec],
            out_specs=[main, wspec, bspec],
            out_shape=[jax.ShapeDtypeStruct((n_rows, ch), BF16), jax.ShapeDtypeStruct((4, ch), F32),
                       jax.ShapeDtypeStruct((1, ch), F32)],
            compiler_params=_params(vmem),
        )(x, x, x, dy, dy, dy, w, b)

    @jax.custom_vjp
    def op(x, w, b):
        return fwd_call(x, w, b)

    def op_bwd(res, g):
        dx, dw, db = bwd_call(*res, g)
        return dx.astype(F32), dw, db

    op.defvjp(lambda x, w, b: (fwd_call(x, w, b), (x, w, b)), op_bwd)
    return op(x, w, b)


def _scan_groups(tt, ch, reverse, load, store, carry_ref):
    row = lax.broadcasted_iota(jnp.int32, (SUBLANES, ch), 0)
    ng = tt // SUBLANES

    def body(k, carry):
        g = (ng - 1 - k) if reverse else k
        sl = pl.ds(pl.multiple_of(g * SUBLANES, SUBLANES), SUBLANES)
        a, b, extra = load(sl)
        for s in (1, 2, 4):
            if reverse:
                a_sh, b_sh, valid = pltpu.roll(a, SUBLANES - s, 0), pltpu.roll(b, SUBLANES - s, 0), row < SUBLANES - s
            else:
                a_sh, b_sh, valid = pltpu.roll(a, s, 0), pltpu.roll(b, s, 0), row >= s
            b = jnp.where(valid, b + a * b_sh, b)
            a = jnp.where(valid, a * a_sh, a)
        h = b + a * carry
        if reverse:
            h_prev = jnp.where(row == SUBLANES - 1, carry, pltpu.roll(h, SUBLANES - 1, 0))
            last = h[0:1]
        else:
            h_prev = jnp.where(row == 0, carry, pltpu.roll(h, 1, 0))
            last = h[SUBLANES - 1:SUBLANES]
        store(sl, h, h_prev, extra)
        return jnp.broadcast_to(last, (SUBLANES, ch))

    carry_ref[...] = lax.fori_loop(0, ng, body, carry_ref[...])


def _lin_scan(name, a, b, h0, reverse):
    n_rows, ch = a.shape
    tt = _row_tile(n_rows, 0, cap=256)
    nt = n_rows // tt
    vmem = 2 * 4 * tt * ch * 5

    def tile_spec(rev):
        return pl.BlockSpec((tt, ch), (lambda i: (nt - 1 - i, 0)) if rev else (lambda i: (i, 0)))

    vec = pl.BlockSpec((1, ch), lambda i: (0, 0))

    def fwd_call(a, b, h0):
        def kern(a_ref, b_ref, h0_ref, h_ref, hp_ref, last_ref, carry):
            @pl.when(pl.program_id(0) == 0)
            def _():
                carry[...] = jnp.broadcast_to(h0_ref[...], carry.shape)

            def load(sl):
                return a_ref[sl, :], b_ref[sl, :], None

            def store(sl, h, h_prev, _):
                h_ref[sl, :] = h
                hp_ref[sl, :] = h_prev

            _scan_groups(tt, ch, reverse, load, store, carry)
            last_ref[...] = carry[0:1]

        return pl.pallas_call(
            kern, grid=(nt,), name=name + "_fwd", in_specs=[tile_spec(reverse), tile_spec(reverse), vec],
            out_specs=[tile_spec(reverse), tile_spec(reverse), vec],
            out_shape=[jax.ShapeDtypeStruct((n_rows, ch), F32)] * 2 + [jax.ShapeDtypeStruct((1, ch), F32)],
            scratch_shapes=[pltpu.VMEM((SUBLANES, ch), F32)], compiler_params=_params(vmem),
        )(a, b, h0)

    def bwd_call(a, h_prev, dh, dlast):
        rev = not reverse

        def kern(a_ref, hp_ref, dh_ref, dl_ref, da_ref, db_ref, d0_ref, carry):
            @pl.when(pl.program_id(0) == 0)
            def _():
                carry[...] = jnp.broadcast_to(dl_ref[...], carry.shape)

            def load(sl):
                av, dv = a_ref[sl, :], dh_ref[sl, :]
                return av, av * dv, dv

            def store(sl, u, u_next, dv):
                g = dv + u_next
                db_ref[sl, :] = g
                da_ref[sl, :] = g * hp_ref[sl, :]

            _scan_groups(tt, ch, rev, load, store, carry)
            d0_ref[...] = carry[0:1]

        return pl.pallas_call(
            kern, grid=(nt,), name=name + "_bwd", in_specs=[tile_spec(rev)] * 3 + [vec],
            out_specs=[tile_spec(rev), tile_spec(rev), vec],
            out_shape=[jax.ShapeDtypeStruct((n_rows, ch), F32)] * 2 + [jax.ShapeDtypeStruct((1, ch), F32)],
            scratch_shapes=[pltpu.VMEM((SUBLANES, ch), F32)], compiler_params=_params(vmem),
        )(a, h_prev, dh, dlast)

    @jax.custom_vjp
    def op(a, b, h0):
        h, _, last = fwd_call(a, b, h0)
        return h, last

    def op_fwd(a, b, h0):
        h, h_prev, last = fwd_call(a, b, h0)
        return (h, last), (a, h_prev)

    def op_bwd(res, g):
        da, db, d0 = bwd_call(res[0], res[1], g[0], g[1])
        return da, db, d0

    op.defvjp(op_fwd, op_bwd)
    return op(a, b, h0)


def _tri(reverse):
    q = lax.broadcasted_iota(jnp.int32, (CHUNK, CHUNK), 0)
    s = lax.broadcasted_iota(jnp.int32, (CHUNK, CHUNK), 1)
    return (q <= s) if reverse else (q >= s)


def _pick_col(x, lane):
    idx = lax.broadcasted_iota(jnp.int32, x.shape, 1)
    return jnp.sum(jnp.where(idx == lane, x, 0.0), axis=1, keepdims=True)


def _pick_row(x, row):
    idx = lax.broadcasted_iota(jnp.int32, x.shape, 0)
    return jnp.sum(jnp.where(idx == row, x, 0.0), axis=0, keepdims=True)


def _ssd_shared(small, bias_row, alog_row, reverse):
    delta_all = jax.nn.softplus(small + bias_row)
    acs_all = _exact_dot(_tri(reverse).astype(F32), delta_all * (-jnp.exp(alog_row)))
    return delta_all, acs_all, acs_all.T


def _ssd_group(xs, bm, cm, state, delta_all, acs_all, acs_t, g, direction, reverse):
    mask = _tri(reverse)
    last = 0 if reverse else CHUNK - 1
    hd = SSD_HEADDIM
    rowi = lax.broadcasted_iota(jnp.int32, (CHUNK, 1), 0)
    a_cols, a_rows, deltas, tots = [], [], [], []
    for r in range(SSD_HPG):
        lane = _DT_LANE + 32 * direction + SSD_HPG * g + r
        a_col = _pick_col(acs_all, lane)
        a_cols.append(a_col)
        a_rows.append(_pick_row(acs_t, lane))
        deltas.append(_pick_col(delta_all, lane))
        tots.append(jnp.sum(jnp.where(rowi == last, a_col, 0.0), axis=0, keepdims=True))

    def wide(cols, rows):
        return jnp.concatenate([jnp.broadcast_to(c, (rows, hd)) for c in cols], axis=1)

    a_w = wide(a_cols, CHUNK)
    x_w = xs * wide(deltas, CHUNK)
    st = _mm_tn(x_w * jnp.exp(wide(tots, 1) - a_w), bm)
    y_off = _mm_nt(cm, state) * jnp.exp(a_w)
    grow = jnp.concatenate([jnp.broadcast_to(jnp.exp(t), (hd, 1)) for t in tots], axis=0)
    cb = _mm_nt(cm, bm)
    m_cat = jnp.concatenate([cb * jnp.exp(jnp.where(mask, a_cols[r] - a_rows[r], -jnp.inf)) for r in range(SSD_HPG)], axis=1)
    lane_head = lax.broadcasted_iota(jnp.int32, (1, SSD_HPG * hd), 1) // hd
    x_bd = jnp.concatenate([jnp.where(lane_head == r, x_w, 0.0) for r in range(SSD_HPG)], axis=0)
    return _mm_nn(m_cat, x_bd) + y_off, grow * state + st


def _ssd_calls(name, xs, bm, cm, s0, direction, reverse):
    n_rows = xs.shape[0]
    nc = n_rows // CHUNK
    gw = SSD_HPG * SSD_HEADDIM
    vmem = 4 * CHUNK * (gw + 3 * 128) * 8 + 4 * gw * 128 * 12 + (8 << 20)

    n_state = SSD_GROUPS * gw
    shared_scratch = [pltpu.VMEM((CHUNK, LANES), F32), pltpu.VMEM((CHUNK, LANES), F32), pltpu.VMEM((LANES, CHUNK), F32)]

    def specs(order, gps=1):
        def cidx(c):
            return (nc - 1 - c) if order else c

        return dict(
            xs=pl.BlockSpec((CHUNK, gps * gw), lambda c, g: (cidx(c), g)),
            bc=pl.BlockSpec((CHUNK, gps * SSD_STATE), lambda c, g: (cidx(c), g)),
            small=pl.BlockSpec((CHUNK, LANES), lambda c, g: (cidx(c), 0)),
            row=pl.BlockSpec((1, LANES), lambda c, g: (0, 0)),
            state=pl.BlockSpec((n_state, SSD_STATE), lambda c, g: (0, 0)),
            enter=pl.BlockSpec((1, gps * gw, SSD_STATE), lambda c, g: (cidx(c), g, 0)),
        )

    gps_fwd, gps_bwd = SSD_GROUPS, 1

    def group_rows(g, gps):
        return pl.ds(pl.multiple_of(g * gps * gw, gps * gw), gps * gw)

    def step_fn(g, gps):
        def fn(xs_v, bm_v, cm_v, st_v, d_all, a_all, a_t):
            ys, sts = [], []
            for u in range(gps):
                y_u, s_u = _ssd_group(xs_v[:, u * gw:(u + 1) * gw], bm_v[:, u * SSD_STATE:(u + 1) * SSD_STATE],
                                      cm_v[:, u * SSD_STATE:(u + 1) * SSD_STATE], st_v[u * gw:(u + 1) * gw],
                                      d_all, a_all, a_t, gps * g + u, direction, reverse)
                ys.append(y_u)
                sts.append(s_u)
            return jnp.concatenate(ys, axis=1), jnp.concatenate(sts, axis=0)

        return fn

    def fwd_call(xs, bm, cm, small, bias_row, alog_row, s0):
        gps = gps_fwd
        sp = specs(reverse, gps)

        def kern(xs_r, bm_r, cm_r, sm_r, br_r, ar_r, s0_r, y_r, sf_r, se_r, st, sh_d, sh_a, sh_t):
            c, g = pl.program_id(0), pl.program_id(1)

            @pl.when((c == 0) & (g == 0))
            def _():
                st[...] = s0_r[...]

            @pl.when(g == 0)
            def _():
                sh_d[...], sh_a[...], sh_t[...] = _ssd_shared(sm_r[...], br_r[...], ar_r[...], reverse)

            rows = group_rows(g, gps)
            s_in = st[rows, :]
            se_r[0] = s_in
            y_r[...], s_new = step_fn(g, gps)(xs_r[...], bm_r[...], cm_r[...], s_in, sh_d[...], sh_a[...], sh_t[...])
            st[rows, :] = s_new
            sf_r[rows, :] = s_new

        return pl.pallas_call(
            kern, grid=(nc, SSD_GROUPS // gps), name=name + "_fwd",
            in_specs=[sp['xs'], sp['bc'], sp['bc'], sp['small'], sp['row'], sp['row'], sp['state']],
            out_specs=[sp['xs'], sp['state'], sp['enter']],
            out_shape=[jax.ShapeDtypeStruct((n_rows, SSD_INNER), F32), jax.ShapeDtypeStruct((n_state, SSD_STATE), F32),
                       jax.ShapeDtypeStruct((nc, n_state, SSD_STATE), F32)],
            scratch_shapes=[pltpu.VMEM((n_state, SSD_STATE), F32)] + shared_scratch, compiler_params=_params(vmem),
        )(xs, bm, cm, small, bias_row, alog_row, s0)

    def bwd_call(xs, bm, cm, small, bias_row, alog_row, enter, dy, dsf, acc=()):
        gps = gps_bwd
        sp = specs(not reverse, gps)

        def kern(*refs):
            xs_r, bm_r, cm_r, sm_r, br_r, ar_r, se_r, dy_r, dsf_r = refs[:9]
            acc_r = refs[9:9 + len(acc)]
            dxs_r, dbm_r, dcm_r, dsm_r, dbr_r, dar_r, ds0_r, ds, sh_d, sh_a, sh_t, gd, ga, gt = refs[9 + len(acc):]
            c, g = pl.program_id(0), pl.program_id(1)

            @pl.when((c == 0) & (g == 0))
            def _():
                ds[...] = dsf_r[...]
                dbr_r[...] = jnp.zeros_like(dbr_r)
                dar_r[...] = jnp.zeros_like(dar_r)

            @pl.when(g == 0)
            def _():
                sh_d[...], sh_a[...], sh_t[...] = _ssd_shared(sm_r[...], br_r[...], ar_r[...], reverse)
                gd[...] = jnp.zeros_like(gd)
                ga[...] = jnp.zeros_like(ga)
                gt[...] = jnp.zeros_like(gt)

            rows = group_rows(g, gps)
            _, vjp = jax.vjp(step_fn(g, gps), xs_r[...], bm_r[...], cm_r[...], se_r[0], sh_d[...], sh_a[...], sh_t[...])
            dxs, dbm, dcm, ds_in, dd, da, dt = vjp((dy_r[...], ds[rows, :]))
            dxs_r[...] = dxs + acc_r[0][...] if acc else dxs
            dbm_r[...] = dbm + acc_r[1][...] if acc else dbm
            dcm_r[...] = dcm + acc_r[2][...] if acc else dcm
            ds[rows, :] = ds_in
            ds0_r[rows, :] = ds_in
            gd[...] += dd
            ga[...] += da
            gt[...] += dt

            @pl.when(g == SSD_GROUPS // gps - 1)
            def _():
                shared = functools.partial(_ssd_shared, reverse=reverse)
                dsm, dbr, dar = jax.vjp(shared, sm_r[...], br_r[...], ar_r[...])[1]((gd[...], ga[...], gt[...]))
                dsm_r[...] = dsm + acc_r[3][...] if acc else dsm
                dbr_r[...] += dbr
                dar_r[...] += dar

        return pl.pallas_call(
            kern, grid=(nc, SSD_GROUPS // gps), name=name + "_bwd",
            in_specs=[sp['xs'], sp['bc'], sp['bc'], sp['small'], sp['row'], sp['row'], sp['enter'], sp['xs'], sp['state']]
            + ([sp['xs'], sp['bc'], sp['bc'], sp['small']] if acc else []),
            out_specs=[sp['xs'], sp['bc'], sp['bc'], sp['small'], sp['row'], sp['row'], sp['state']],
            out_shape=[jax.ShapeDtypeStruct(xs.shape, F32), jax.ShapeDtypeStruct(bm.shape, F32),
                       jax.ShapeDtypeStruct(cm.shape, F32), jax.ShapeDtypeStruct((n_rows, LANES), F32),
                       jax.ShapeDtypeStruct((1, LANES), F32), jax.ShapeDtypeStruct((1, LANES), F32),
                       jax.ShapeDtypeStruct(s0.shape, F32)],
            scratch_shapes=[pltpu.VMEM((n_state, SSD_STATE), F32)] + shared_scratch + shared_scratch,
            compiler_params=_params(vmem),
        )(xs, bm, cm, small, bias_row, alog_row, enter, dy, dsf, *acc)

    return fwd_call, bwd_call


def _ssd_pair(name, xs, bm, cm, small, rows_f, rows_b, s0_f, s0_b):
    calls = [_ssd_calls(name + "%d" % d, xs, bm, cm, s0_f, d, d == 1) for d in range(2)]

    def run_fwd(xs, bm, cm, small, rows_f, rows_b, s0_f, s0_b):
        y_f, sf_f, en_f = calls[0][0](xs, bm, cm, small, *rows_f, s0_f)
        y_b, sf_b, en_b = calls[1][0](xs, bm, cm, small, *rows_b, s0_b)
        return (y_f, y_b, sf_f, sf_b), (en_f, en_b)

    @jax.custom_vjp
    def op(*args):
        return run_fwd(*args)[0]

    def op_fwd(*args):
        outs, enters = run_fwd(*args)
        return outs, (args[:6], enters)

    def op_bwd(res, g):
        (xs, bm, cm, small, rows_f, rows_b), (en_f, en_b) = res
        dy_f, dy_b, dsf_f, dsf_b = g
        dxs, dbm, dcm, dsm, dbr_f, dar_f, ds0_f = calls[0][1](xs, bm, cm, small, *rows_f, en_f, dy_f, dsf_f)
        dxs, dbm, dcm, dsm, dbr_b, dar_b, ds0_b = calls[1][1](xs, bm, cm, small, *rows_b, en_b, dy_b, dsf_b, acc=(dxs, dbm, dcm, dsm))
        return dxs, dbm, dcm, dsm, (dbr_f, dar_f), (dbr_b, dar_b), ds0_f, ds0_b

    op.defvjp(op_fwd, op_bwd)
    return op(xs, bm, cm, small, tuple(rows_f), tuple(rows_b), s0_f, s0_b)


def _ml_shared(small, gate_row, reverse):
    gates = small + gate_row
    b_all = _exact_dot(_tri(reverse).astype(F32), jax.nn.log_sigmoid(gates))
    return gates, b_all, gates.T, b_all.T


def _ml_head(q, k, v, c_st, n_st, m_st, gates, b_all, gates_t, b_t, h, direction, reverse):
    mask = _tri(reverse)
    last = 0 if reverse else CHUNK - 1
    lane_i = _MG_LANE + 8 * direction + h
    lane_f = lane_i + ML_HEADS
    b_col = _pick_col(b_all, lane_f)
    b_row = _pick_row(b_t, lane_f)
    li_col = _pick_col(gates, lane_i)
    li_row = _pick_row(gates_t, lane_i)
    rowi = lax.broadcasted_iota(jnp.int32, (CHUNK, 1), 0)
    g_tot = jnp.sum(jnp.where(rowi == last, b_col, 0.0), axis=0, keepdims=True)
    m_in = m_st[:, 0:1]
    q = q * (ML_HD ** -0.5)
    w = g_tot - b_col + li_col
    m_loc = lax.stop_gradient(jnp.max(w, axis=0, keepdims=True))
    kw = k * jnp.exp(w - m_loc)
    c_loc = _mm_tn(kw, v)
    n_loc = jnp.sum(kw, axis=0, keepdims=True)
    m_new = lax.stop_gradient(jnp.maximum(g_tot + m_in, m_loc))
    s_old = jnp.exp(g_tot + m_in - m_new)
    s_loc = jnp.exp(m_loc - m_new)
    c_new = s_old * c_st + s_loc * c_loc
    n_new = s_old * n_st + s_loc * n_loc
    log_d = jnp.where(mask, b_col - b_row + li_row, -jnp.inf)
    inter = b_col + m_in
    m_t = lax.stop_gradient(jnp.maximum(inter, jnp.max(log_d, axis=1, keepdims=True)))
    dmat = jnp.exp(log_d - m_t)
    wi = jnp.exp(inter - m_t)
    s = _mm_nt(q, k) * dmat
    num = _mm_nn(s, v) + wi * _mm_nn(q, c_st)
    den = jnp.sum(s, axis=1, keepdims=True) + wi * jnp.sum(_round_bf16(q) * _round_bf16(n_st), axis=1, keepdims=True)
    out = num / jnp.maximum(jnp.abs(den), jnp.exp(-m_t))
    return out, c_new, n_new, jnp.broadcast_to(m_new, (1, LANES))


def _ml_calls(name, q, direction, reverse):
    n_rows = q.shape[0]
    nc = n_rows // CHUNK
    vmem = 4 * CHUNK * (4 * ML_HD + 128) * 8 + 4 * ML_HD * ML_HD * 12 + (8 << 20)

    def specs(order, hps=1):
        def cidx(c):
            return (nc - 1 - c) if order else c

        return dict(
            qkv=pl.BlockSpec((CHUNK, hps * ML_HD), lambda c, h: (cidx(c), h)),
            small=pl.BlockSpec((CHUNK, LANES), lambda c, h: (cidx(c), 0)),
            row=pl.BlockSpec((1, LANES), lambda c, h: (0, 0)),
            c=pl.BlockSpec((ML_HEADS * ML_HD, ML_HD), lambda c, h: (0, 0)),
            n=pl.BlockSpec((ML_HEADS, 1, ML_HD), lambda c, h: (0, 0, 0)),
            m=pl.BlockSpec((ML_HEADS, 1, LANES), lambda c, h: (0, 0, 0)),
            ec=pl.BlockSpec((1, hps * ML_HD, ML_HD), lambda c, h: (cidx(c), h, 0)),
            en=pl.BlockSpec((1, hps, 1, ML_HD), lambda c, h: (cidx(c), h, 0, 0)),
            em=pl.BlockSpec((1, hps, 1, LANES), lambda c, h: (cidx(c), h, 0, 0)),
        )

    st_shapes = [jax.ShapeDtypeStruct((ML_HEADS * ML_HD, ML_HD), F32), jax.ShapeDtypeStruct((ML_HEADS, 1, ML_HD), F32),
                 jax.ShapeDtypeStruct((ML_HEADS, 1, LANES), F32)]
    scratch = [pltpu.VMEM((ML_HEADS * ML_HD, ML_HD), F32), pltpu.VMEM((ML_HEADS, 1, ML_HD), F32),
               pltpu.VMEM((ML_HEADS, 1, LANES), F32)]
    shared_scratch = [pltpu.VMEM((CHUNK, LANES), F32), pltpu.VMEM((CHUNK, LANES), F32),
                      pltpu.VMEM((LANES, CHUNK), F32), pltpu.VMEM((LANES, CHUNK), F32)]

    def head_rows(h):
        return pl.ds(pl.multiple_of(h * ML_HD, ML_HD), ML_HD)

    def fwd_call(q, k, v, small, gate_row, c0, n0, m0):
        hps = ML_HEADS
        sp = specs(reverse, hps)

        def kern(q_r, k_r, v_r, sm_r, gr_r, c0_r, n0_r, m0_r, o_r, cf_r, nf_r, mf_r, ec_r, en_r, em_r, cs, ns, ms, *sh):
            c, hh = pl.program_id(0), pl.program_id(1)

            @pl.when((c == 0) & (hh == 0))
            def _():
                cs[...] = c0_r[...]
                ns[...] = n0_r[...]
                ms[...] = m0_r[...]

            @pl.when(hh == 0)
            def _():
                for ref, val in zip(sh, _ml_shared(sm_r[...], gr_r[...], reverse)):
                    ref[...] = val

            q_v, k_v, v_v = q_r[...], k_r[...], v_r[...]
            shared = [r[...] for r in sh]
            heads = [hps * hh + u for u in range(hps)]
            states = [(cs[head_rows(h), :], ns[h], ms[h]) for h in heads]
            results = []
            for u, (h, (c_in, n_in, m_in)) in enumerate(zip(heads, states)):
                cols = slice(u * ML_HD, (u + 1) * ML_HD)
                ec_r[0, cols, :] = c_in
                en_r[0, u] = n_in
                em_r[0, u] = m_in
                results.append(_ml_head(q_v[:, cols], k_v[:, cols], v_v[:, cols], c_in, n_in, m_in, *shared,
                                        h, direction, reverse))
            o_r[...] = jnp.concatenate([r[0] for r in results], axis=1)
            for h, (_, c_new, n_new, m_new) in zip(heads, results):
                cs[head_rows(h), :] = c_new
                ns[h] = n_new
                ms[h] = m_new
                cf_r[head_rows(h), :] = c_new
                nf_r[h] = n_new
                mf_r[h] = m_new

        return pl.pallas_call(
            kern, grid=(nc, ML_HEADS // hps), name=name + "_fwd",
            in_specs=[sp['qkv']] * 3 + [sp['small'], sp['row'], sp['c'], sp['n'], sp['m']],
            out_specs=[sp['qkv'], sp['c'], sp['n'], sp['m'], sp['ec'], sp['en'], sp['em']],
            out_shape=[jax.ShapeDtypeStruct((n_rows, ML_HEADS * ML_HD), F32)] + st_shapes + [
                jax.ShapeDtypeStruct((nc, ML_HEADS * ML_HD, ML_HD), F32),
                jax.ShapeDtypeStruct((nc, ML_HEADS, 1, ML_HD), F32), jax.ShapeDtypeStruct((nc, ML_HEADS, 1, LANES), F32)],
            scratch_shapes=scratch + shared_scratch, compiler_params=_params(vmem),
        )(q, k, v, small, gate_row, c0, n0, m0)

    def bwd_call(q, k, v, small, gate_row, ec, en, em, do, dcf, dnf, dmf, acc=()):
        sp = specs(not reverse)
        n_sh = len(shared_scratch)

        def kern(*refs):
            q_r, k_r, v_r, sm_r, gr_r, ec_r, en_r, em_r, do_r, dcf_r, dnf_r, dmf_r = refs[:12]
            acc_r = refs[12:12 + len(acc)]
            dq_r, dk_r, dv_r, dsm_r, dgr_r, dc0_r, dn0_r, dm0_r, dcs, dns, dms = refs[12 + len(acc):23 + len(acc)]
            rest = refs[23 + len(acc):]
            sh, gsh = rest[:n_sh], rest[n_sh:]
            c, h = pl.program_id(0), pl.program_id(1)

            @pl.when((c == 0) & (h == 0))
            def _():
                dcs[...] = dcf_r[...]
                dns[...] = dnf_r[...]
                dms[...] = dmf_r[...]
                dgr_r[...] = jnp.zeros_like(dgr_r)

            @pl.when(h == 0)
            def _():
                for ref, val in zip(sh, _ml_shared(sm_r[...], gr_r[...], reverse)):
                    ref[...] = val
                for ref in gsh:
                    ref[...] = jnp.zeros_like(ref)

            rows = head_rows(h)
            fn = functools.partial(_ml_head, h=h, direction=direction, reverse=reverse)
            _, vjp = jax.vjp(fn, q_r[...], k_r[...], v_r[...], ec_r[0], en_r[0, 0], em_r[0, 0], *[r[...] for r in sh])
            grads = vjp((do_r[...], dcs[rows, :], dns[h], dms[h]))
            dq, dk, dv, dc, dn, dm = grads[:6]
            dq_r[...] = dq + acc_r[0][...] if acc else dq
            dk_r[...] = dk + acc_r[1][...] if acc else dk
            dv_r[...] = dv + acc_r[2][...] if acc else dv
            for ref, val in zip(gsh, grads[6:]):
                ref[...] += val
            dm = jnp.broadcast_to(jnp.sum(dm, axis=1, keepdims=True), (1, LANES)) * (1.0 / LANES)
            dcs[rows, :] = dc
            dns[h] = dn
            dms[h] = dm
            dc0_r[rows, :] = dc
            dn0_r[h] = dn
            dm0_r[h] = dm

            @pl.when(h == ML_HEADS - 1)
            def _():
                shared = functools.partial(_ml_shared, reverse=reverse)
                dsm, dgr = jax.vjp(shared, sm_r[...], gr_r[...])[1](tuple(r[...] for r in gsh))
                dsm_r[...] = dsm + acc_r[3][...] if acc else dsm
                dgr_r[...] += dgr

        return pl.pallas_call(
            kern, grid=(nc, ML_HEADS), name=name + "_bwd",
            in_specs=[sp['qkv']] * 3 + [sp['small'], sp['row'], sp['ec'], sp['en'], sp['em'], sp['qkv'], sp['c'], sp['n'], sp['m']]
            + ([sp['qkv']] * 3 + [sp['small']] if acc else []),
            out_specs=[sp['qkv']] * 3 + [sp['small'], sp['row'], sp['c'], sp['n'], sp['m']],
            out_shape=[jax.ShapeDtypeStruct(q.shape, F32)] * 3 + [jax.ShapeDtypeStruct((n_rows, LANES), F32),
                                                                  jax.ShapeDtypeStruct((1, LANES), F32)] + st_shapes,
            scratch_shapes=scratch + shared_scratch + shared_scratch, compiler_params=_params(vmem),
        )(q, k, v, small, gate_row, ec, en, em, do, dcf, dnf, dmf, *acc)

    return fwd_call, bwd_call


def _ml_pair(name, q, k, v, small, gate_row, state_f, state_b):
    calls = [_ml_calls(name + "%d" % d, q, d, d == 1) for d in range(2)]

    def run_fwd(q, k, v, small, gate_row, state_f, state_b):
        res_f = calls[0][0](q, k, v, small, gate_row, *state_f)
        res_b = calls[1][0](q, k, v, small, gate_row, *state_b)
        return (res_f[0], res_b[0], tuple(res_f[1:4]), tuple(res_b[1:4])), (tuple(res_f[4:]), tuple(res_b[4:]))

    @jax.custom_vjp
    def op(*args):
        return run_fwd(*args)[0]

    def op_fwd(*args):
        outs, enters = run_fwd(*args)
        return outs, (args[:5], enters)

    def op_bwd(res, g):
        (q, k, v, small, gate_row), (en_f, en_b) = res
        do_f, do_b, ds_f, ds_b = g
        dq, dk, dv, dsm, dgr_f, *d0_f = calls[0][1](q, k, v, small, gate_row, *en_f, do_f, *ds_f)
        dq, dk, dv, dsm, dgr_b, *d0_b = calls[1][1](q, k, v, small, gate_row, *en_b, do_b, *ds_b, acc=(dq, dk, dv, dsm))
        return dq, dk, dv, dsm, dgr_f + dgr_b, tuple(d0_f), tuple(d0_b)

    op.defvjp(op_fwd, op_bwd)
    return op(q, k, v, small, gate_row, tuple(state_f), tuple(state_b))


def _f_modulate(x, shift, scale):
    return (_layernorm_rows(x) * (1.0 + scale) + shift,)


def _f_resid_ln(x, o, gate, bias, ln_g, ln_b):
    return (_layernorm_rows(DN_ALPHA * x + gate * (o + bias)) * ln_g + ln_b,)


def _f_lru_gates(xc, w_r, b_r, w_i, b_i, lam):
    outs = []
    for d in range(2):
        def blockdiag(w):
            return jnp.concatenate(
                [_mm_nn(xc[:, n * LRU_BS:(n + 1) * LRU_BS], w[(d * LRU_BLOCKS + n) * LRU_BS:(d * LRU_BLOCKS + n + 1) * LRU_BS, :])
                 for n in range(LRU_BLOCKS)], axis=1)

        r = jax.nn.sigmoid(blockdiag(w_r) + b_r[d:d + 1])
        i = jax.nn.sigmoid(blockdiag(w_i) + b_i[d:d + 1])
        log_a = -LRU_C * r * jax.nn.softplus(-lam[d:d + 1])
        outs += [jnp.exp(log_a), jnp.sqrt(1.0 - jnp.exp(2.0 * log_a)) * i * xc]
    return tuple(outs)


def _f_lru_out(h_f, h_b, ly):
    return ((h_f + h_b) * jax.nn.gelu(ly),)


def _f_ssd_post(y_f, y_b, xs, z, d_exp, norm_w):
    y = (y_f + y_b + xs * d_exp) * jax.nn.silu(z)
    gw = SSD_INNER // SSD_GROUPS
    parts = []
    for g in range(SSD_GROUPS):
        yg = y[:, g * gw:(g + 1) * gw]
        parts.append(yg * lax.rsqrt(jnp.mean(jnp.square(yg), -1, keepdims=True) + LN_EPS))
    return (jnp.concatenate(parts, axis=1) * norm_w,)


def _f_ml_post(h_f, h_b, o, norm_w):
    h = h_f + h_b
    parts = [_layernorm_rows(h[:, i * ML_HD:(i + 1) * ML_HD]) for i in range(ML_HEADS)]
    return (jnp.concatenate(parts, axis=1) * norm_w * jax.nn.sigmoid(o),)


def _f_merge(ga, gb, gc, pa, pb, pc):
    return (jax.nn.sigmoid(ga) * pa + jax.nn.sigmoid(gb) * pb + jax.nn.sigmoid(gc) * pc,)


def _f_relu2(pre, bias):
    return (jnp.square(jax.nn.relu(pre + bias)),)


def _lane_row(vec, start):
    return jnp.pad(vec[None], ((0, 0), (start, LANES - start - vec.shape[0])))


def _mixer(tag, x_tok, shift, scale, p, states):
    (lru_s, ssd_s, ml_s) = states
    lx, ly, sz, xs, bm, cm, mq, mk, mv, mo, ga, gb, gc, small = _rowwise_linear(
        tag + "in", _f_modulate, [x_tok], [shift, scale], [(p['w_in_main'], _IN_MAIN_WIDTHS), (p['w_in_small'], None)])

    xc = _dwconv(tag + "lruconv", lx, p['lru_conv_w'], p['lru_conv_b'][None], False)
    a_f, b_f, a_b, b_b = _rowwise(
        tag + "lrugate", _f_lru_gates, [xc],
        [p['lru_w_r'].reshape(2 * LRU_BLOCKS * LRU_BS, LRU_BS), p['lru_b_r'], p['lru_w_i'].reshape(2 * LRU_BLOCKS * LRU_BS, LRU_BS),
         p['lru_b_i'], p['lru_lambda']], [D_MODEL] * 4)
    h_f, s_f = _lin_scan(tag + "lruscanf", a_f, b_f, lru_s[0], False)
    h_b, s_b = _lin_scan(tag + "lruscanb", a_b, b_b, lru_s[1], True)
    (pa,) = _rowwise_linear(tag + "bra", _f_lru_out, [h_f, h_b, ly], [], [(p['w_br_a'], None)], to_linear=(2,))

    cw, cb_ = p['ssd_conv_w'], p['ssd_conv_b'][None]
    xs_c = _dwconv(tag + "ssdconvx", xs, cw[:, :2048], cb_[:, :2048], True)
    bm_c = _dwconv(tag + "ssdconvb", bm, cw[:, 2048:3072], cb_[:, 2048:3072], True)
    cm_c = _dwconv(tag + "ssdconvc", cm, cw[:, 3072:], cb_[:, 3072:], True)
    dir_rows = [(_lane_row(p['ssd_dt_bias'][d], _DT_LANE + 32 * d), _lane_row(p['ssd_a_log'][d], _DT_LANE + 32 * d)) for d in range(2)]
    *ys, st_f, st_b = _ssd_pair(tag + "ssd", xs_c, bm_c, cm_c, small, dir_rows[0], dir_rows[1], ssd_s[0], ssd_s[1])
    ssd_new = (st_f, st_b)
    (pb,) = _rowwise_linear(tag + "brb", _f_ssd_post, [ys[0], ys[1], xs_c, sz],
                            [jnp.repeat(p['ssd_d'], SSD_HEADDIM)[None], p['ssd_norm_w'][None]], [(p['w_br_b'], None)], to_linear=(3,))

    mw, mb = p['ml_conv_w'], p['ml_conv_b'][None]
    q_c = _dwconv(tag + "mlconvq", mq, mw[:, :1024], mb[:, :1024], True)
    k_c = _dwconv(tag + "mlconvk", mk, mw[:, 1024:], mb[:, 1024:], True)
    gate_row = _lane_row(p['ml_gate_b'].reshape(4 * ML_HEADS), _MG_LANE)
    *hs, ml_f, ml_b = _ml_pair(tag + "ml", q_c, k_c, mv, small, gate_row, ml_s[0], ml_s[1])
    ml_new = (ml_f, ml_b)
    (pc,) = _rowwise_linear(tag + "brc", _f_ml_post, [hs[0], hs[1], mo], [p['ml_norm_w'][None]], [(p['w_br_c'], None)], to_linear=(2,))
    return (ga, gb, gc, pa, pb, pc), ((s_f, s_b), tuple(ssd_new), tuple(ml_new))


def _merge(tag, br, p):
    return _rowwise_linear(tag + "out", _f_merge, list(br), [], [(p['w_out'], None)], to_linear=tuple(range(6)))[0]


def _sublayers(tag, xin, o, mods, p, l):
    sh2, sc2, g1, g2 = mods
    (x1,) = _rowwise(tag + "ln1", _f_resid_ln, [xin, o], [g1, p['b_out'][None], p['ln1_g'][None], p['ln1_b'][None]], [D_MODEL], to_linear=(1,))
    (pre,) = _rowwise_linear(tag + "ff1", _f_modulate, [x1], [sh2, sc2], [(p['w_ff1'], None)])
    (o2,) = _rowwise_linear(tag + "ff2", _f_relu2, [pre], [p['b_ff1'][None]], [(p['w_ff2'], None)], to_linear=(0,))
    (x2,) = _rowwise(tag + "ln2", _f_resid_ln, [x1, o2], [g2, p['b_ff2'][None], p['ln2_g'][None], p['ln2_b'][None]], [D_MODEL], to_linear=(1,))
    return x2


def _to_col_major(h):
    s, d = h.shape
    return h.reshape(s // GRID_W, GRID_W, d).swapaxes(0, 1).reshape(s, d)


def _from_col_major(h):
    s, d = h.shape
    return h.reshape(GRID_W, s // GRID_W, d).swapaxes(0, 1).reshape(s, d)


def _forward(x, wts, mods, ctx):
    zeros = lambda *s: jnp.zeros(s, F32)
    ctx_init = ((zeros(1, D_MODEL), zeros(1, D_MODEL)),
                (zeros(SSD_INNER, SSD_STATE), zeros(SSD_INNER, SSD_STATE)),
                tuple((zeros(ML_HEADS * ML_HD, ML_HD), zeros(ML_HEADS, 1, ML_HD), zeros(ML_HEADS, 1, LANES)) for _ in range(2)))
    for l in range(DEPTH):
        p = {n: wts[n][l] for n in wts}
        tag = "l%d" % l
        sh1x, sc1x, g1x, sh2x, sc2x, g2x = [mods[l][0][:, i * D_MODEL:(i + 1) * D_MODEL] for i in range(6)]
        sh1c, sc1c, g1c, sh2c, sc2c, g2c = [mods[l][1][:, i * D_MODEL:(i + 1) * D_MODEL] for i in range(6)]
        br_c, ctx_states = _mixer(tag + "c", ctx, sh1c, sc1c, p, ctx_init)
        br_x, _ = _mixer(tag + "x", _to_col_major(x) if l % 2 == 1 else x, sh1x, sc1x, p, ctx_states)
        ox = _merge(tag + "x", br_x, p)
        if l % 2 == 1:
            ox = _from_col_major(ox)
        x = _sublayers(tag + "x", x, ox, (sh2x, sc2x, g1x, g2x), p, l)
        if l < DEPTH - 1:
            ctx = _sublayers(tag + "c", ctx, _merge(tag + "c", br_c, p), (sh2c, sc2c, g1c, g2c), p, l)
    return x


_ADA_ROWS = 2 * SUBLANES


def _ada_forward(c, c_ctx, w_ada, b_ada, me):
    c_all = _exchange("gather_c", jnp.broadcast_to(c, (SUBLANES, D_MODEL)), True)[:, 0]

    def rows_of(c_ctx_):
        pad = jnp.zeros((_ADA_ROWS - N_DEV - 1, D_MODEL), F32)
        return jax.nn.silu(jnp.concatenate([c_all, c_ctx_[None], pad], axis=0))

    rows, vjp_rows = jax.vjp(rows_of, c_ctx)
    cols, vjp_cols = jax.vjp(lambda r, w: jnp.stack([_linear("ada%d" % l, r, w[l]) for l in range(DEPTH)]), rows, w_ada)
    full = _exchange("gather_mod", cols, True).transpose(1, 2, 0, 3).reshape(DEPTH, _ADA_ROWS, 6 * D_MODEL) + b_ada[:, None, :]
    mods = [(lax.dynamic_slice_in_dim(full[l], me, 1, axis=0), full[l][N_DEV:N_DEV + 1]) for l in range(DEPTH)]
    return mods, (vjp_rows, vjp_cols)


def _ada_backward(saved, dmods):
    vjp_rows, vjp_cols = saved
    wcol = 6 * D_MODEL // N_DEV
    pad = jnp.zeros((SUBLANES - 2, 6 * D_MODEL), F32)
    both = jnp.stack([jnp.concatenate([dx, dc, pad], axis=0) for dx, dc in dmods])
    send = both.reshape(DEPTH, SUBLANES, N_DEV, wcol).transpose(2, 0, 1, 3)
    recv = _exchange("scatter_dmod", send, False)
    ctx_row = recv[0, :, 1]
    for k in range(1, N_DEV):
        ctx_row = ctx_row + recv[k, :, 1]
    g = jnp.concatenate([recv[:, :, 0].transpose(1, 0, 2), ctx_row[:, None],
                         jnp.zeros((DEPTH, _ADA_ROWS - N_DEV - 1, wcol), F32)], axis=1)
    d_rows, d_w = vjp_cols(g)
    (d_c_ctx,) = vjp_rows(d_rows)
    d_b = jnp.stack([(dx + dc)[0] for dx, dc in dmods])
    return d_w, d_b, d_c_ctx


def _loss_and_cotangent(y, target):
    n_rows, d = y.shape
    tt = _row_tile(n_rows, 0, cap=256)

    def kern(y_ref, t_ref, dy_ref, acc_ref):
        @pl.when(pl.program_id(0) == 0)
        def _():
            acc_ref[...] = jnp.zeros_like(acc_ref)

        err = y_ref[...] - t_ref[...]
        dy_ref[...] = err * (1.0 / d)
        acc_ref[...] += jnp.sum(jnp.square(err))

    spec = pl.BlockSpec((tt, d), lambda i: (i, 0))
    dy, acc = pl.pallas_call(
        kern, grid=(n_rows // tt,), name="loss", in_specs=[spec, spec],
        out_specs=[spec, pl.BlockSpec((SUBLANES, LANES), lambda i: (0, 0))],
        out_shape=[jax.ShapeDtypeStruct((n_rows, d), F32), jax.ShapeDtypeStruct((SUBLANES, LANES), F32)],
    )(y, target)
    return acc[0, 0] * (0.5 / d), dy


def _exchange(name, src, gather):
    slab = src.shape if gather else src.shape[1:]

    def body(src_ref, out_ref, send_sems, recv_sems, local_sem):
        x, y, c = lax.axis_index("x"), lax.axis_index("y"), lax.axis_index("c")
        me = 4 * x + 2 * y + c
        local = pltpu.make_async_copy(src_ref if gather else src_ref.at[me], out_ref.at[me], local_sem)
        local.start()
        copies = []
        for d in range(1, N_DEV):
            px, py, pc = lax.rem(x + (d >> 2), 2), lax.rem(y + ((d >> 1) & 1), 2), lax.rem(c + (d & 1), 2)
            peer = 4 * px + 2 * py + pc
            cp = pltpu.make_async_remote_copy(
                src_ref=src_ref if gather else src_ref.at[peer], dst_ref=out_ref.at[me],
                send_sem=send_sems.at[d - 1], recv_sem=recv_sems.at[d - 1],
                device_id=(px, py, pc), device_id_type=pl.DeviceIdType.MESH)
            cp.start()
            copies.append(cp)
        for cp in copies:
            cp.wait()
        local.wait()

    return pl.pallas_call(
        body, name=name, out_shape=jax.ShapeDtypeStruct((N_DEV,) + tuple(slab), src.dtype),
        in_specs=[pl.BlockSpec(memory_space=pl.ANY)], out_specs=pl.BlockSpec(memory_space=pl.ANY),
        scratch_shapes=[pltpu.SemaphoreType.DMA((N_DEV - 1,)), pltpu.SemaphoreType.DMA((N_DEV - 1,)), pltpu.SemaphoreType.DMA],
    )(src)


_HBM = pl.BlockSpec(memory_space=pl.ANY)
_CHIPS = ((0, 0), (0, 1), (1, 0), (1, 1))


def _gather_two_level(name, srcs):
    n = len(srcs)

    def body(*refs):
        src_refs, out_refs = refs[:n], refs[n:2 * n]
        send_sems, recv_sems, local_sems = refs[2 * n:]
        x, y, c = lax.axis_index("x"), lax.axis_index("y"), lax.axis_index("c")
        me, sibling = (x, y, c), (x, y, 1 - c)
        chips = [(1 - x, y), (x, 1 - y), (1 - x, 1 - y)]

        def slab(a, px, py, pc):
            return out_refs[a].at[4 * px + 2 * py + pc]

        def copy(a, k, block, to, own=False):
            return pltpu.make_async_remote_copy(
                src_ref=src_refs[a] if own else slab(a, *block), dst_ref=slab(a, *block), send_sem=send_sems.at[a, k],
                recv_sem=recv_sems.at[a, k], device_id=to, device_id_type=pl.DeviceIdType.MESH)

        mine = [pltpu.make_async_copy(src_refs[a], slab(a, *me), local_sems.at[a]) for a in range(n)]
        first = []
        for a in range(n):
            mine[a].start()
            first += [copy(a, 0, me, sibling, own=True)] + [copy(a, 1 + j, me, (*chip, c), own=True) for j, chip in enumerate(chips)]
        for cp in first:
            cp.start()
        passed = []
        for j, chip in enumerate(chips):
            for a in range(n):
                copy(a, 1 + j, (*chip, c), me).wait_recv()
                passed.append(copy(a, 4 + j, (*chip, c), sibling))
                passed[-1].start()
        for a in range(n):
            copy(a, 0, sibling, me).wait_recv()
        for j, chip in enumerate(chips):
            for a in range(n):
                copy(a, 4 + j, (*chip, 1 - c), me).wait_recv()
        for cp in first + passed:
            cp.wait_send()
        for cp in mine:
            cp.wait()

    return pl.pallas_call(
        body, name=name, out_shape=[jax.ShapeDtypeStruct((N_DEV,) + tuple(s.shape), s.dtype) for s in srcs],
        in_specs=[_HBM] * n, out_specs=[_HBM] * n,
        scratch_shapes=[pltpu.SemaphoreType.DMA((n, N_DEV - 1)), pltpu.SemaphoreType.DMA((n, N_DEV - 1)), pltpu.SemaphoreType.DMA((n,))],
    )(*srcs)


def _scatter_to_sibling(name, parts_list):
    n = len(parts_list)

    def body(*refs):
        p_refs, out_refs = refs[:n], refs[n:2 * n]
        send_sems, recv_sems = refs[2 * n:]
        x, y, c = lax.axis_index("x"), lax.axis_index("y"), lax.axis_index("c")
        copies = []
        for a in range(n):
            for j, (px, py) in enumerate(_CHIPS):
                cp = pltpu.make_async_remote_copy(
                    src_ref=p_refs[a].at[4 * px + 2 * py + (1 - c)], dst_ref=out_refs[a].at[j], send_sem=send_sems.at[a, j],
                    recv_sem=recv_sems.at[a, j], device_id=(x, y, 1 - c), device_id_type=pl.DeviceIdType.MESH)
                cp.start()
                copies.append(cp)
        for cp in copies:
            cp.wait()

    return pl.pallas_call(
        body, name=name, out_shape=[jax.ShapeDtypeStruct((4,) + tuple(p.shape[1:]), p.dtype) for p in parts_list],
        in_specs=[_HBM] * n, out_specs=[_HBM] * n,
        scratch_shapes=[pltpu.SemaphoreType.DMA((n, 4)), pltpu.SemaphoreType.DMA((n, 4))],
    )(*parts_list)


def _chip_sum(name, parts, from_sibling):
    _, rows, cols = parts.shape
    lanes = -(-cols // LANES) * LANES
    tr = _row_tile(rows, 4 * lanes * 4 * 2, budget=12 << 20)

    def kern(p_ref, s_ref, o_ref):
        c = lax.axis_index("c")
        o_ref[0] = (jnp.where(c == 0, p_ref[0, 0], p_ref[0, 1]) + s_ref[0]).astype(o_ref.dtype)

    return pl.pallas_call(
        kern, grid=(4, rows // tr), name=name,
        in_specs=[pl.BlockSpec((1, 2, tr, cols), lambda j, i: (j, 0, i, 0)), pl.BlockSpec((1, tr, cols), lambda j, i: (j, i, 0))],
        out_specs=pl.BlockSpec((1, tr, cols), lambda j, i: (j, i, 0)),
        out_shape=jax.ShapeDtypeStruct((4, rows, cols), BF16),
        compiler_params=_params(4 * lanes * tr * 4 * 2),
    )(parts.reshape(4, 2, rows, cols), from_sibling)


def _scatter_across_chips(name, sums_list):
    n = len(sums_list)

    def body(*refs):
        q_refs, out_refs = refs[:n], refs[n:2 * n]
        send_sems, recv_sems, local_sems = refs[2 * n:]
        x, y, c = lax.axis_index("x"), lax.axis_index("y"), lax.axis_index("c")
        own = 2 * x + y
        copies = []
        for a in range(n):
            local = pltpu.make_async_copy(q_refs[a].at[own], out_refs[a].at[own], local_sems.at[a])
            local.start()
            copies.append(local)
            for d in range(1, 4):
                px, py = lax.rem(x + (d >> 1), 2), lax.rem(y + (d & 1), 2)
                cp = pltpu.make_async_remote_copy(
                    src_ref=q_refs[a].at[2 * px + py], dst_ref=out_refs[a].at[own], send_sem=send_sems.at[a, d - 1],
                    recv_sem=recv_sems.at[a, d - 1], device_id=(px, py, c), device_id_type=pl.DeviceIdType.MESH)
                cp.start()
                copies.append(cp)
        for cp in copies:
            cp.wait()

    return pl.pallas_call(
        body, name=name, out_shape=[jax.ShapeDtypeStruct(s.shape, s.dtype) for s in sums_list],
        in_specs=[_HBM] * n, out_specs=[_HBM] * n,
        scratch_shapes=[pltpu.SemaphoreType.DMA((n, 3)), pltpu.SemaphoreType.DMA((n, 3)), pltpu.SemaphoreType.DMA((n,))],
    )(*sums_list)


def _sum_parts(name, parts):
    n_parts, rows, cols = parts.shape
    tr = _row_tile(rows, 4 * cols * (n_parts + 1) * 2)

    def kern(p_ref, o_ref):
        acc = p_ref[0]
        for k in range(1, n_parts):
            acc = acc + p_ref[k]
        o_ref[...] = acc

    return pl.pallas_call(
        kern, grid=(rows // tr,), name=name, in_specs=[pl.BlockSpec((n_parts, tr, cols), lambda i: (0, i, 0))],
        out_specs=pl.BlockSpec((tr, cols), lambda i: (i, 0)), out_shape=jax.ShapeDtypeStruct((rows, cols), F32),
    )(parts)


def _adamw(name, w, m, v, parts):
    n_parts, rows, cols = parts.shape
    lanes = -(-cols // LANES) * LANES
    tr = _row_tile(rows, 4 * lanes * (n_parts + 7) * 2, budget=16 << 20)
    c1 = np.float32(1.0 - ADAM_B1 ** ADAM_STEP)
    c2 = np.float32(1.0 - ADAM_B2 ** ADAM_STEP)

    def kern(w_ref, m_ref, v_ref, p_ref, g_ref, d_ref, nm_ref, nv_ref):
        g = p_ref[0].astype(F32)
        for k in range(1, n_parts):
            g = g + p_ref[k].astype(F32)
        m_new = ADAM_B1 * m_ref[...] + (1.0 - ADAM_B1) * g
        v_new = ADAM_B2 * v_ref[...] + (1.0 - ADAM_B2) * jnp.square(g)
        g_ref[...] = g
        nm_ref[...] = m_new
        nv_ref[...] = v_new
        d_ref[...] = -ADAM_LR * ((m_new / c1) / (jnp.sqrt(v_new / c2) + ADAM_EPS) + ADAM_WD * w_ref[...])

    spec = pl.BlockSpec((tr, cols), lambda i: (i, 0))
    return pl.pallas_call(
        kern, grid=(rows // tr,), name=name,
        in_specs=[spec, spec, spec, pl.BlockSpec((n_parts, tr, cols), lambda i: (0, i, 0))], out_specs=[spec] * 4,
        out_shape=[jax.ShapeDtypeStruct((rows, cols), F32)] * 4,
        compiler_params=_params(4 * lanes * tr * (n_parts + 7) * 2),
    )(w, m, v, parts)


def _packed_rows(shape):
    return -(-int(np.prod(shape)) // (SUBLANES * LANES)) * SUBLANES


def _pack(arrays, row_multiple):
    parts = []
    for a in arrays:
        n = int(np.prod(a.shape))
        r = _packed_rows(a.shape)
        parts.append(jnp.pad(a.reshape(-1), (0, r * LANES - n)).reshape(r, LANES))
    rows = sum(p.shape[0] for p in parts)
    total = -(-rows // row_multiple) * row_multiple
    if total > rows:
        parts.append(jnp.zeros((total - rows, LANES), arrays[0].dtype))
    return jnp.concatenate(parts, axis=0)


def _unpack(packed, shapes):
    out, off = [], 0
    for s in shapes:
        r = _packed_rows(s)
        out.append(packed[off:off + r].reshape(-1)[:int(np.prod(s))].reshape(s))
        off += r
    return out


def _split_w_in(w_in):
    main = jnp.concatenate([w_in[:, :, s:e] for s, e in _IN_MAIN], axis=2)
    pad = jnp.zeros(w_in.shape[:2] + (LANES - 80,), w_in.dtype)
    small = jnp.concatenate([w_in[:, :, s:e] for s, e in _IN_SMALL] + [pad], axis=2)
    return main, small


def _join_w_in(main, small):
    return jnp.concatenate([main[:, :, 0:8192], small[:, :, 0:64], main[:, :, 8192:12288], small[:, :, 64:80],
                            main[:, :, 12288:15360]], axis=2)


def _unshard(gathered, axis):
    nd, nl, r, c = gathered.shape
    if axis == 1:
        return gathered.transpose(1, 0, 2, 3).reshape(nl, nd * r, c)
    return gathered.transpose(1, 2, 0, 3).reshape(nl, r, nd * c)


def _reshard(full, axis):
    nl, r, c = full.shape
    if axis == 1:
        return full.reshape(nl, N_DEV, r // N_DEV, c).transpose(1, 0, 2, 3)
    return full.reshape(nl, r, N_DEV, c // N_DEV).transpose(2, 0, 1, 3)


def kernel(x, c, ctx, c_ctx, w_ada, b_ada, w_in, lru_conv_w, lru_conv_b, lru_w_r, lru_b_r, lru_w_i, lru_b_i, lru_lambda, ssd_conv_w, ssd_conv_b, ssd_dt_bias, ssd_a_log, ssd_d, ssd_norm_w, ml_conv_w, ml_conv_b, ml_gate_b, ml_norm_w, w_br_a, w_br_b, w_br_c, w_out, b_out, ln1_g, ln1_b, w_ff1, b_ff1, w_ff2, b_ff2, ln2_g, ln2_b, loss_target, m_c_ctx, m_w_ada, m_b_ada, m_w_in, m_lru_conv_w, m_lru_conv_b, m_lru_w_r, m_lru_b_r, m_lru_w_i, m_lru_b_i, m_lru_lambda, m_ssd_conv_w, m_ssd_conv_b, m_ssd_dt_bias, m_ssd_a_log, m_ssd_d, m_ssd_norm_w, m_ml_conv_w, m_ml_conv_b, m_ml_gate_b, m_ml_norm_w, m_w_br_a, m_w_br_b, m_w_br_c, m_w_out, m_b_out, m_ln1_g, m_ln1_b, m_w_ff1, m_b_ff1, m_w_ff2, m_b_ff2, m_ln2_g, m_ln2_b, v_c_ctx, v_w_ada, v_b_ada, v_w_in, v_lru_conv_w, v_lru_conv_b, v_lru_w_r, v_lru_b_r, v_lru_w_i, v_lru_b_i, v_lru_lambda, v_ssd_conv_w, v_ssd_conv_b, v_ssd_dt_bias, v_ssd_a_log, v_ssd_d, v_ssd_norm_w, v_ml_conv_w, v_ml_conv_b, v_ml_gate_b, v_ml_norm_w, v_w_br_a, v_w_br_b, v_w_br_c, v_w_out, v_b_out, v_ln1_g, v_ln1_b, v_w_ff1, v_b_ff1, v_w_ff2, v_b_ff2, v_ln2_g, v_ln2_b):
    a = dict(locals())
    me = 4 * lax.axis_index("x") + 2 * lax.axis_index("y") + lax.axis_index("c")

    wts = {n: a[n] for n in _REPLICATED if n not in ('c_ctx', 'b_ada')}
    exchanged = [n for n in _BIG if n != 'w_ada']
    gathered = _gather_two_level("gather_weights", [a[n].astype(BF16) for n in exchanged])
    for n, g in zip(exchanged, gathered):
        full = _unshard(g, _BIG[n])
        if n == 'w_in':
            main, small = _split_w_in(full)
            wts['w_in_main'], wts['w_in_small'] = main.astype(F32), small.astype(F32)
        else:
            wts[n] = full.astype(F32)
    small_shapes = [a[n].shape for n in _SMALL_SHARDED]
    small_all = _exchange("gather_small", _pack([a[n] for n in _SMALL_SHARDED], SUBLANES), True)
    per_dev = [_unpack(small_all[k], small_shapes) for k in range(N_DEV)]
    for i, n in enumerate(_SMALL_SHARDED):
        wts[n] = jnp.concatenate([per_dev[k][i] for k in range(N_DEV)], axis=-1)

    mods, ada_saved = _ada_forward(c, c_ctx, w_ada, b_ada, me)
    y, vjp = jax.vjp(functools.partial(_forward, ctx=ctx[0]), x[0], wts, mods)
    loss_local, dy = _loss_and_cotangent(y, loss_target[0])
    grad_x, grads, dmods = vjp(dy)
    grads['w_in'] = _join_w_in(grads.pop('w_in_main'), grads.pop('w_in_small'))
    grad_w_ada, grads['b_ada'], grads['c_ctx'] = _ada_backward(ada_saved, dmods)
    loss = lax.psum(loss_local, ("x", "y", "c"))

    out = {}

    def put(n, res, shape):
        for kind, r in zip(("grad_", "delta_", "new_m_", "new_v_"), res):
            out[kind + n] = r.reshape(shape)

    flat = {n: (a[n].shape[0] * a[n].shape[1], a[n].shape[2]) for n in _BIG}
    by_dest = [_reshard(grads[n], _BIG[n]).reshape(N_DEV, *flat[n]) for n in exchanged]
    from_sibling = _scatter_to_sibling("scatter_d2d", by_dest)
    chip_sums = [_chip_sum("chipsum_" + n, p, s) for n, p, s in zip(exchanged, by_dest, from_sibling)]
    summed = dict(zip(exchanged, _scatter_across_chips("scatter_ici", chip_sums)))
    summed['w_ada'] = grad_w_ada.reshape(1, *flat['w_ada'])
    for n in _BIG:
        shp = a[n].shape
        rows, cols = flat[n]
        parts = summed[n]
        put(n, _adamw("adamw_" + n, a[n].reshape(rows, cols), a["m_" + n].reshape(rows, cols), a["v_" + n].reshape(rows, cols), parts), shp)

    rep_names = _REPLICATED + _SMALL_SHARDED
    chunk_rows = SUBLANES * N_DEV
    g_pack = _pack([grads[n] for n in rep_names], chunk_rows * N_DEV)
    rows = g_pack.shape[0]
    parts = _exchange("scatter_rep", g_pack.reshape(N_DEV, rows // N_DEV, LANES), False)
    mine = _sum_parts("sum_rep", parts)
    g_all = _exchange("gather_rep", mine, True).reshape(rows, LANES)
    g_full = _unpack(g_all, [grads[n].shape for n in rep_names])
    g_local = []
    for n, g in zip(rep_names, g_full):
        if n in _SMALL_SHARDED:
            width = a[n].shape[-1]
            g = lax.dynamic_slice_in_dim(g, me * width, width, axis=g.ndim - 1)
        g_local.append(g)
    shapes = [a[n].shape for n in rep_names]
    res = _adamw("adamw_rep", _pack([a[n] for n in rep_names], chunk_rows), _pack([a["m_" + n] for n in rep_names], chunk_rows),
                 _pack([a["v_" + n] for n in rep_names], chunk_rows), _pack(g_local, chunk_rows)[None])
    unpacked = [_unpack(r, shapes) for r in res]
    for i, n in enumerate(rep_names):
        put(n, [u[i] for u in unpacked], shapes[i])

    outs = [loss, grad_x[None]]
    for kind in ("grad_", "delta_", "new_m_", "new_v_"):
        outs += [out[kind + n] for n in _WEIGHTS]
    return tuple(outs)
```

```python
import functools

import numpy as np
import jax
import jax.numpy as jnp
from jax import lax
from jax.experimental import pallas as pl
from jax.experimental.pallas import tpu as pltpu

F32 = jnp.float32
BF16 = jnp.bfloat16

N_DEV = 8
D_MODEL = 1024
DEPTH = 2
GRID_W = 64
CHUNK = 128
LN_EPS = 1e-6
LRU_BLOCKS = 8
LRU_BS = 128
LRU_C = 8.0
SSD_INNER = 2048
SSD_GROUPS = 8
SSD_HPG = 4
SSD_HEADDIM = 64
SSD_STATE = 128
ML_HEADS = 4
ML_HD = 256
D_FF = 4096
DN_ALPHA = (2 * DEPTH) ** 0.25
ADAM_LR, ADAM_B1, ADAM_B2, ADAM_EPS, ADAM_WD, ADAM_STEP = 0.001, 0.9, 0.999, 1e-08, 0.01, 10

VMEM_CAP = 60 * 1024 * 1024
SUBLANES = 8
LANES = 128

_IN_MAIN = ((0, 8192), (8256, 12352), (12368, 15440))
_IN_MAIN_WIDTHS = (1024, 1024, 2048, 2048, 1024, 1024, 1024, 1024, 1024, 1024, 1024, 1024, 1024)
_IN_SMALL = ((8192, 8256), (12352, 12368))
_DT_LANE = 0
_MG_LANE = 64

_WEIGHTS = ['c_ctx', 'w_ada', 'b_ada', 'w_in', 'lru_conv_w', 'lru_conv_b', 'lru_w_r', 'lru_b_r', 'lru_w_i', 'lru_b_i',
            'lru_lambda', 'ssd_conv_w', 'ssd_conv_b', 'ssd_dt_bias', 'ssd_a_log', 'ssd_d', 'ssd_norm_w', 'ml_conv_w',
            'ml_conv_b', 'ml_gate_b', 'ml_norm_w', 'w_br_a', 'w_br_b', 'w_br_c', 'w_out', 'b_out', 'ln1_g', 'ln1_b',
            'w_ff1', 'b_ff1', 'w_ff2', 'b_ff2', 'ln2_g', 'ln2_b']
_BIG = {'w_ada': 2, 'w_in': 2, 'w_ff1': 2, 'w_br_a': 1, 'w_br_b': 1, 'w_br_c': 1, 'w_out': 1, 'w_ff2': 1}
_SMALL_SHARDED = ['lru_conv_w', 'lru_b_r', 'lru_b_i', 'lru_lambda', 'ssd_conv_w', 'ml_conv_w']
_REPLICATED = [n for n in _WEIGHTS if n not in _BIG and n not in _SMALL_SHARDED]


def _params(vmem_bytes):
    return pltpu.CompilerParams(vmem_limit_bytes=int(min(max(2 * vmem_bytes, 32 << 20), VMEM_CAP)))


def _row_tile(n_rows, bytes_per_row, budget=6 << 20, cap=512):
    t = cap
    while t > SUBLANES and (t * bytes_per_row > budget or n_rows % t):
        t //= 2
    assert n_rows % t == 0, (n_rows, t)
    return t


def _dg(a, b, ca, cb):
    return lax.dot_general(a.astype(BF16), b.astype(BF16), (((ca,), (cb,)), ((), ())), preferred_element_type=F32)


def _make_bdot(ca, cb):
    @jax.custom_vjp
    def f(a, b):
        return _dg(a, b, ca, cb)

    def fwd(a, b):
        return _dg(a, b, ca, cb), (a, b)

    def bwd(res, g):
        a, b = res
        da = _dg(g, b, 1, 1 - cb) if ca == 1 else _dg(b, g, 1 - cb, 1)
        db = _dg(a, g, 1 - ca, 0) if cb == 0 else _dg(g, a, 0, 1 - ca)
        return da, db

    f.defvjp(fwd, bwd)
    return f


_mm_nn = _make_bdot(1, 0)
_mm_nt = _make_bdot(1, 1)
_mm_tn = _make_bdot(0, 0)


@jax.custom_vjp
def _round_bf16(x):
    return x.astype(BF16).astype(F32)


_round_bf16.defvjp(lambda x: (_round_bf16(x), None), lambda _, g: (g,))


def _exact_dot(a, b):
    return jnp.dot(a, b, precision=lax.Precision.HIGHEST, preferred_element_type=F32)


def _layernorm_rows(x):
    mu = jnp.mean(x, -1, keepdims=True)
    var = jnp.mean(jnp.square(x - mu), -1, keepdims=True)
    return (x - mu) * lax.rsqrt(var + LN_EPS)


def _rowwise_calls(name, f, rows, params, out_widths, out_dtype=F32, to_linear=(), tile_cap=512):
    drow_dtypes = [BF16 if i in to_linear else F32 for i in range(len(rows))]
    nr, npar, no = len(rows), len(params), len(out_widths)
    n_rows = rows[0].shape[0]
    row_w = [r.shape[1] for r in rows]
    par_bytes = sum(int(np.prod(p.shape)) * 4 for p in params)
    tile = _row_tile(n_rows, 4 * (2 * sum(row_w) + 2 * sum(out_widths)), budget=16 << 20, cap=tile_cap)
    grid = (n_rows // tile,)

    def row_spec(w):
        return pl.BlockSpec((tile, w), lambda i: (i, 0))

    def par_spec(p):
        return pl.BlockSpec(p.shape, lambda i: (0, 0))

    vmem = 2 * tile * 4 * (2 * sum(row_w) + 3 * sum(out_widths)) + 4 * par_bytes

    def fwd_call(rows, params):
        def kern(*refs):
            outs = f(*[r[...] for r in refs[:nr + npar]])
            for r, o in zip(refs[nr + npar:], outs):
                r[...] = o.astype(out_dtype)

        return pl.pallas_call(
            kern, grid=grid, name=name + "_fwd",
            in_specs=[row_spec(w) for w in row_w] + [par_spec(p) for p in params],
            out_specs=[row_spec(w) for w in out_widths],
            out_shape=[jax.ShapeDtypeStruct((n_rows, w), out_dtype) for w in out_widths],
            compiler_params=_params(vmem),
        )(*rows, *params)

    def bwd_call(rows, params, gouts):
        def kern(*refs):
            ins = [r[...] for r in refs[:nr + npar]]
            gs = tuple(r[...] for r in refs[nr + npar:nr + npar + no])
            grads = jax.vjp(f, *ins)[1](gs)
            drefs = refs[nr + npar + no:]
            for k in range(nr):
                drefs[k][...] = grads[k].astype(drefs[k].dtype)

            @pl.when(pl.program_id(0) == 0)
            def _():
                for k in range(npar):
                    drefs[nr + k][...] = jnp.zeros_like(drefs[nr + k])

            for k in range(npar):
                drefs[nr + k][...] += grads[nr + k]

        res = pl.pallas_call(
            kern, grid=grid, name=name + "_bwd",
            in_specs=[row_spec(w) for w in row_w] + [par_spec(p) for p in params] + [row_spec(w) for w in out_widths],
            out_specs=[row_spec(w) for w in row_w] + [par_spec(p) for p in params],
            out_shape=[jax.ShapeDtypeStruct(r.shape, dt) for r, dt in zip(rows, drow_dtypes)]
            + [jax.ShapeDtypeStruct(p.shape, F32) for p in params],
            compiler_params=_params(vmem),
        )(*rows, *params, *gouts)
        return tuple(r.astype(F32) for r in res[:nr]), tuple(res[nr:])

    return fwd_call, bwd_call


def _rowwise(name, f, rows, params, out_widths, to_linear=(), tile_cap=512):
    rows, params = tuple(rows), tuple(params)
    fwd_call, bwd_call = _rowwise_calls(name, f, rows, params, out_widths, to_linear=to_linear, tile_cap=tile_cap)

    @jax.custom_vjp
    def op(rows, params):
        return tuple(fwd_call(rows, params))

    op.defvjp(lambda r, p: (tuple(fwd_call(r, p)), (r, p)), lambda res, g: bwd_call(res[0], res[1], g))
    return op(rows, params)


def _rowwise_linear(name, f, rows, params, weights, to_linear=()):
    rows, params = tuple(rows), tuple(params)
    ws = tuple(w for w, _ in weights)
    m, k = rows[0].shape[0], ws[0].shape[0]
    row_fwd, row_bwd = _rowwise_calls(name, f, rows, params, [k], BF16, to_linear)
    lin = [_linear_calls(name + "lin%d" % i, m, k, w.shape[1], wd, BF16) for i, (w, wd) in enumerate(weights)]
    counts = [len(c[3]) for c in lin]

    def fwd(rows, params, ws):
        (a,) = row_fwd(rows, params)
        outs = []
        for (fwd_call, _, _, _), w in zip(lin, ws):
            outs += list(fwd_call(a, w))
        return tuple(outs), a

    @jax.custom_vjp
    def op(rows, params, ws):
        return fwd(rows, params, ws)[0]

    def op_fwd(rows, params, ws):
        outs, a = fwd(rows, params, ws)
        return outs, (rows, params, ws, a)

    def op_bwd(res, g):
        rows, params, ws, a = res
        da, dws, off = None, [], 0
        for (_, dgrad_call, wgrad_call, _), w, cnt in zip(lin, ws, counts):
            gk = g[off:off + cnt]
            off += cnt
            d = dgrad_call(w, gk)
            da = d if da is None else da + d
            dws.append(wgrad_call(a, gk))
        drows, dparams = row_bwd(rows, params, (da,))
        return drows, dparams, tuple(dws)

    op.defvjp(op_fwd, op_bwd)
    return op(rows, params, ws)


def _group_ranges(widths, tn):
    starts, s = [], 0
    for w in widths:
        assert w % tn == 0, (w, tn)
        starts.append((s // tn, (s + w) // tn))
        s += w
    return starts, s // tn


def _group_tile(refs, ranges, row_tile, col_tile, i, j):
    out = []
    for ref, (s, e) in zip(refs, ranges):
        cols = pl.ds(pl.multiple_of((j - s) * col_tile, col_tile), col_tile)
        out.append(((j >= s) & (j < e), ref, cols))
    return [(p, lambda r=r, c=c: r.at[pl.ds(pl.multiple_of(i * row_tile, row_tile), row_tile), c]) for p, r, c in out]


def _linear_calls(name, m, k, n, widths, a_dtype):
    widths = (n,) if widths is None else tuple(widths)
    ng = len(widths)
    cast_a = a_dtype != BF16
    wide = all(wd % 1024 == 0 for wd in widths)
    tn = 128 if n < 256 else (1024 if wide else (256 if k > 2048 or n % 512 else 512))
    tm = _row_tile(m, 0, cap=1024 if k <= 2048 else 512)
    ranges, nt = _group_ranges(widths, tn)
    mt = m // tm
    tn_w = 1024 if (wide and k <= 1024) else (512 if all(wd % 512 == 0 for wd in widths) else min(tn, 256))
    tm_w = _row_tile(m, 0, cap=1024 if k <= 1024 else 512)
    ranges_w, nt_w = _group_ranges(widths, tn_w)
    mt_w = m // tm_w
    hbm = pl.BlockSpec(memory_space=pl.ANY)

    def fwd_call(a, w):
        n_steps = mt * nt

        def kern(a_ref, w_ref, *rest):
            outs, obuf, osem = rest[:ng], rest[ng], rest[ng + 1]
            a_bf = rest[ng + 2] if cast_a else a_ref
            i, j = pl.program_id(0), pl.program_id(1)
            step = i * nt + j
            slot = lax.rem(step, 2)

            def drain(sl):
                pltpu.make_async_copy(obuf.at[sl], outs[0].at[pl.ds(0, tm), pl.ds(0, tn)], osem.at[sl]).wait()

            if cast_a:
                @pl.when(j == 0)
                def _():
                    a_bf[...] = a_ref[...].astype(BF16)

            @pl.when(step >= 2)
            def _():
                drain(slot)

            obuf[slot] = jnp.dot(a_bf[...], w_ref[...], preferred_element_type=F32)
            for pred, window in _group_tile(outs, ranges, tm, tn, i, j):
                @pl.when(pred)
                def _(window=window):
                    pltpu.make_async_copy(obuf.at[slot], window(), osem.at[slot]).start()

            @pl.when(step == n_steps - 1)
            def _():
                drain(slot)
                if n_steps > 1:
                    drain(1 - slot)

        return pl.pallas_call(
            kern, grid=(mt, nt), name=name + "_fwd",
            in_specs=[pl.BlockSpec((tm, k), lambda i, j: (i, 0)), pl.BlockSpec((k, tn), lambda i, j: (0, j))],
            out_specs=[hbm] * ng,
            out_shape=[jax.ShapeDtypeStruct((m, wd), F32) for wd in widths],
            scratch_shapes=[pltpu.VMEM((2, tm, tn), F32), pltpu.SemaphoreType.DMA((2,))]
            + ([pltpu.VMEM((tm, k), BF16)] if cast_a else []),
            compiler_params=_params(10 * tm * k + 4 * k * tn + 8 * tm * tn),
        )(a, w.astype(BF16))

    def prefetched(gs, gbuf, gsem, rngs, row_tile, col_tile, step, n_steps, tile_of):
        slot = lax.rem(step, 2)

        def start(s_idx, sl):
            ii, jj = tile_of(s_idx)
            for pred, window in _group_tile(gs, rngs, row_tile, col_tile, ii, jj):
                @pl.when(pred)
                def _(window=window):
                    pltpu.make_async_copy(window(), gbuf.at[sl], gsem.at[sl]).start()

        @pl.when(step == 0)
        def _():
            start(step, slot)

        @pl.when(step + 1 < n_steps)
        def _():
            start(step + 1, 1 - slot)

        pltpu.make_async_copy(gs[0].at[pl.ds(0, row_tile), pl.ds(0, col_tile)], gbuf.at[slot], gsem.at[slot]).wait()
        return slot

    def dgrad_call(w, gouts):
        def kern(w_ref, *rest):
            gs, da, gbuf, gsem = rest[:ng], rest[ng], rest[ng + 1], rest[ng + 2]
            i, j = pl.program_id(0), pl.program_id(1)
            slot = prefetched(gs, gbuf, gsem, ranges, tm, tn, i * nt + j, mt * nt, lambda s: (s // nt, lax.rem(s, nt)))

            @pl.when(j == 0)
            def _():
                da[...] = jnp.zeros_like(da)

            da[...] += lax.dot_general(gbuf[slot].astype(BF16), w_ref[...], (((1,), (1,)), ((), ())), preferred_element_type=F32)

        return pl.pallas_call(
            kern, grid=(mt, nt), name=name + "_dgrad",
            in_specs=[pl.BlockSpec((k, tn), lambda i, j: (0, j))] + [hbm] * ng,
            out_specs=pl.BlockSpec((tm, k), lambda i, j: (i, 0)),
            out_shape=jax.ShapeDtypeStruct((m, k), F32),
            scratch_shapes=[pltpu.VMEM((2, tm, tn), BF16), pltpu.SemaphoreType.DMA((2,))],
            compiler_params=_params(12 * tm * k + 4 * k * tn + 10 * tm * tn),
        )(w.astype(BF16), *[g.astype(BF16) for g in gouts])

    def wgrad_call(a, gouts):
        def kern(a_ref, *rest):
            gs, dw, gbuf, gsem = rest[:ng], rest[ng], rest[ng + 1], rest[ng + 2]
            j, i = pl.program_id(0), pl.program_id(1)
            slot = prefetched(gs, gbuf, gsem, ranges_w, tm_w, tn_w, j * mt_w + i, mt_w * nt_w,
                              lambda s: (lax.rem(s, mt_w), s // mt_w))

            @pl.when(i == 0)
            def _():
                dw[...] = jnp.zeros_like(dw)

            dw[...] += lax.dot_general(a_ref[...].astype(BF16), gbuf[slot].astype(BF16), (((0,), (0,)), ((), ())),
                                       preferred_element_type=F32)

        return pl.pallas_call(
            kern, grid=(nt_w, mt_w), name=name + "_wgrad",
            in_specs=[pl.BlockSpec((tm_w, k), lambda j, i: (i, 0))] + [hbm] * ng,
            out_specs=pl.BlockSpec((k, tn_w), lambda j, i: (0, j)),
            out_shape=jax.ShapeDtypeStruct((k, n), F32),
            scratch_shapes=[pltpu.VMEM((2, tm_w, tn_w), BF16), pltpu.SemaphoreType.DMA((2,))],
            compiler_params=_params(12 * tm_w * k + 12 * k * tn_w + 10 * tm_w * tn_w),
        )(a, *[g.astype(BF16) for g in gouts])

    return fwd_call, dgrad_call, wgrad_call, widths


def _linear(name, a, w, widths=None):
    fwd_call, dgrad_call, wgrad_call, _ = _linear_calls(name, a.shape[0], a.shape[1], w.shape[1], widths, a.dtype)

    @jax.custom_vjp
    def op(a, w):
        return tuple(fwd_call(a, w))

    op.defvjp(lambda a, w: (tuple(fwd_call(a, w)), (a, w)),
              lambda res, g: (dgrad_call(res[1], g), wgrad_call(res[0], g)))
    out = op(a, w)
    return out[0] if widths is None else out


def _conv_taps(x_ext, w, n_ext):
    xm2 = pltpu.roll(x_ext, 2, 0)
    xm1 = pltpu.roll(x_ext, 1, 0)
    xp1 = pltpu.roll(x_ext, n_ext - 1, 0)
    return xm2, xm1, xp1


def _dwconv(name, x, w, b, act):
    n_rows, ch = x.shape
    tt = _row_tile(n_rows, 4 * 6 * ch, budget=12 << 20, cap=256)
    nt = n_rows // tt
    n_ext = tt + 2 * SUBLANES
    per8 = tt // SUBLANES
    last8 = n_rows // SUBLANES - 1
    main = pl.BlockSpec((tt, ch), lambda i: (i, 0))
    prev = pl.BlockSpec((SUBLANES, ch), lambda i: (jnp.maximum(i * per8 - 1, 0), 0))
    nxt = pl.BlockSpec((SUBLANES, ch), lambda i: (jnp.minimum((i + 1) * per8, last8), 0))
    wspec = pl.BlockSpec((4, ch), lambda i: (0, 0))
    bspec = pl.BlockSpec((1, ch), lambda i: (0, 0))
    vmem = 4 * n_ext * ch * 14

    def ext(main_ref, prev_ref, next_ref):
        i = pl.program_id(0)
        p = jnp.where(i > 0, prev_ref[...], 0.0)
        q = jnp.where(i < nt - 1, next_ref[...], 0.0)
        return jnp.concatenate([p, main_ref[...], q], axis=0)

    def pre_of(x_ext, wv, bv):
        xm2, xm1, xp1 = _conv_taps(x_ext, wv, n_ext)
        pre = wv[0:1] * xm2 + wv[1:2] * xm1 + wv[2:3] * x_ext + wv[3:4] * xp1 + bv
        return pre, (xm2, xm1, xp1)

    def fwd_call(x, w, b):
        def kern(xm, xp, xn, w_ref, b_ref, o_ref):
            pre, _ = pre_of(ext(xm, xp, xn), w_ref[...], b_ref[...])
            pre = pre[SUBLANES:SUBLANES + tt]
            o_ref[...] = pre * jax.nn.sigmoid(pre) if act else pre

        return pl.pallas_call(
            kern, grid=(nt,), name=name + "_fwd", in_specs=[main, prev, nxt, wspec, bspec], out_specs=main,
            out_shape=jax.ShapeDtypeStruct((n_rows, ch), F32), compiler_params=_params(vmem),
        )(x, x, x, w, b)

    def bwd_call(x, w, b, dy):
        def kern(xm, xp, xn, gm, gp, gn, w_ref, b_ref, dx_ref, dw_ref, db_ref):
            wv = w_ref[...]
            x_ext = ext(xm, xp, xn)
            pre, (xm2, xm1, xp1) = pre_of(x_ext, wv, b_ref[...])
            dpre = ext(gm, gp, gn)
            if act:
                sg = jax.nn.sigmoid(pre)
                dpre = dpre * (sg + pre * sg * (1.0 - sg))
            dx = (wv[0:1] * pltpu.roll(dpre, n_ext - 2, 0) + wv[1:2] * pltpu.roll(dpre, n_ext - 1, 0)
                  + wv[2:3] * dpre + wv[3:4] * pltpu.roll(dpre, 1, 0))
            sl = slice(SUBLANES, SUBLANES + tt)
            dx_ref[...] = dx[sl].astype(dx_ref.dtype)
            dm = dpre[sl]

            @pl.when(pl.program_id(0) == 0)
            def _():
                dw_ref[...] = jnp.zeros_like(dw_ref)
                db_ref[...] = jnp.zeros_like(db_ref)

            dw_ref[...] += jnp.concatenate(
                [jnp.sum(dm * t[sl], axis=0, keepdims=True) for t in (xm2, xm1, x_ext, xp1)], axis=0)
            db_ref[...] += jnp.sum(dm, axis=0, keepdims=True)

        return pl.pallas_call(
            kern, grid=(nt,), name=name + "_bwd", in_specs=[main, prev, nxt, main, prev, nxt, wspec, bspec],
            out_specs=[main, wspec, bspec],
            out_shape=[jax.ShapeDtypeStruct((n_rows, ch), BF16), jax.ShapeDtypeStruct((4, ch), F32),
                       jax.ShapeDtypeStruct((1, ch), F32)],
            compiler_params=_params(vmem),
        )(x, x, x, dy, dy, dy, w, b)

    @jax.custom_vjp
    def op(x, w, b):
        return fwd_call(x, w, b)

    def op_bwd(res, g):
        dx, dw, db = bwd_call(*res, g)
        return dx.astype(F32), dw, db

    op.defvjp(lambda x, w, b: (fwd_call(x, w, b), (x, w, b)), op_bwd)
    return op(x, w, b)


def _scan_groups(tt, ch, reverse, load, store, carry_ref):
    row = lax.broadcasted_iota(jnp.int32, (SUBLANES, ch), 0)
    ng = tt // SUBLANES

    def body(k, carry):
        g = (ng - 1 - k) if reverse else k
        sl = pl.ds(pl.multiple_of(g * SUBLANES, SUBLANES), SUBLANES)
        a, b, extra = load(sl)
        for s in (1, 2, 4):
            if reverse:
                a_sh, b_sh, valid = pltpu.roll(a, SUBLANES - s, 0), pltpu.roll(b, SUBLANES - s, 0), row < SUBLANES - s
            else:
                a_sh, b_sh, valid = pltpu.roll(a, s, 0), pltpu.roll(b, s, 0), row >= s
            b = jnp.where(valid, b + a * b_sh, b)
            a = jnp.where(valid, a * a_sh, a)
        h = b + a * carry
        if reverse:
            h_prev = jnp.where(row == SUBLANES - 1, carry, pltpu.roll(h, SUBLANES - 1, 0))
            last = h[0:1]
        else:
            h_prev = jnp.where(row == 0, carry, pltpu.roll(h, 1, 0))
            last = h[SUBLANES - 1:SUBLANES]
        store(sl, h, h_prev, extra)
        return jnp.broadcast_to(last, (SUBLANES, ch))

    carry_ref[...] = lax.fori_loop(0, ng, body, carry_ref[...])


def _lin_scan(name, a, b, h0, reverse):
    n_rows, ch = a.shape
    tt = _row_tile(n_rows, 0, cap=256)
    nt = n_rows // tt
    vmem = 2 * 4 * tt * ch * 5

    def tile_spec(rev):
        return pl.BlockSpec((tt, ch), (lambda i: (nt - 1 - i, 0)) if rev else (lambda i: (i, 0)))

    vec = pl.BlockSpec((1, ch), lambda i: (0, 0))

    def fwd_call(a, b, h0):
        def kern(a_ref, b_ref, h0_ref, h_ref, hp_ref, last_ref, carry):
            @pl.when(pl.program_id(0) == 0)
            def _():
                carry[...] = jnp.broadcast_to(h0_ref[...], carry.shape)

            def load(sl):
                return a_ref[sl, :], b_ref[sl, :], None

            def store(sl, h, h_prev, _):
                h_ref[sl, :] = h
                hp_ref[sl, :] = h_prev

            _scan_groups(tt, ch, reverse, load, store, carry)
            last_ref[...] = carry[0:1]

        return pl.pallas_call(
            kern, grid=(nt,), name=name + "_fwd", in_specs=[tile_spec(reverse), tile_spec(reverse), vec],
            out_specs=[tile_spec(reverse), tile_spec(reverse), vec],
            out_shape=[jax.ShapeDtypeStruct((n_rows, ch), F32)] * 2 + [jax.ShapeDtypeStruct((1, ch), F32)],
            scratch_shapes=[pltpu.VMEM((SUBLANES, ch), F32)], compiler_params=_params(vmem),
        )(a, b, h0)

    def bwd_call(a, h_prev, dh, dlast):
        rev = not reverse

        def kern(a_ref, hp_ref, dh_ref, dl_ref, da_ref, db_ref, d0_ref, carry):
            @pl.when(pl.program_id(0) == 0)
            def _():
                carry[...] = jnp.broadcast_to(dl_ref[...], carry.shape)

            def load(sl):
                av, dv = a_ref[sl, :], dh_ref[sl, :]
                return av, av * dv, dv

            def store(sl, u, u_next, dv):
                g = dv + u_next
                db_ref[sl, :] = g
                da_ref[sl, :] = g * hp_ref[sl, :]

            _scan_groups(tt, ch, rev, load, store, carry)
            d0_ref[...] = carry[0:1]

        return pl.pallas_call(
            kern, grid=(nt,), name=name + "_bwd", in_specs=[tile_spec(rev)] * 3 + [vec],
            out_specs=[tile_spec(rev), tile_spec(rev), vec],
            out_shape=[jax.ShapeDtypeStruct((n_rows, ch), F32)] * 2 + [jax.ShapeDtypeStruct((1, ch), F32)],
            scratch_shapes=[pltpu.VMEM((SUBLANES, ch), F32)], compiler_params=_params(vmem),
        )(a, h_prev, dh, dlast)

    @jax.custom_vjp
    def op(a, b, h0):
        h, _, last = fwd_call(a, b, h0)
        return h, last

    def op_fwd(a, b, h0):
        h, h_prev, last = fwd_call(a, b, h0)
        return (h, last), (a, h_prev)

    def op_bwd(res, g):
        da, db, d0 = bwd_call(res[0], res[1], g[0], g[1])
        return da, db, d0

    op.defvjp(op_fwd, op_bwd)
    return op(a, b, h0)


def _tri(reverse):
    q = lax.broadcasted_iota(jnp.int32, (CHUNK, CHUNK), 0)
    s = lax.broadcasted_iota(jnp.int32, (CHUNK, CHUNK), 1)
    return (q <= s) if reverse else (q >= s)


def _pick_col(x, lane):
    idx = lax.broadcasted_iota(jnp.int32, x.shape, 1)
    return jnp.sum(jnp.where(idx == lane, x, 0.0), axis=1, keepdims=True)


def _pick_row(x, row):
    idx = lax.broadcasted_iota(jnp.int32, x.shape, 0)
    return jnp.sum(jnp.where(idx == row, x, 0.0), axis=0, keepdims=True)


def _ssd_shared(small, bias_row, alog_row, reverse):
    delta_all = jax.nn.softplus(small + bias_row)
    acs_all = _exact_dot(_tri(reverse).astype(F32), delta_all * (-jnp.exp(alog_row)))
    return delta_all, acs_all, acs_all.T


def _ssd_group(xs, bm, cm, state, delta_all, acs_all, acs_t, g, direction, reverse):
    mask = _tri(reverse)
    last = 0 if reverse else CHUNK - 1
    hd = SSD_HEADDIM
    rowi = lax.broadcasted_iota(jnp.int32, (CHUNK, 1), 0)
    a_cols, a_rows, deltas, tots = [], [], [], []
    for r in range(SSD_HPG):
        lane = _DT_LANE + 32 * direction + SSD_HPG * g + r
        a_col = _pick_col(acs_all, lane)
        a_cols.append(a_col)
        a_rows.append(_pick_row(acs_t, lane))
        deltas.append(_pick_col(delta_all, lane))
        tots.append(jnp.sum(jnp.where(rowi == last, a_col, 0.0), axis=0, keepdims=True))

    def wide(cols, rows):
        return jnp.concatenate([jnp.broadcast_to(c, (rows, hd)) for c in cols], axis=1)

    a_w = wide(a_cols, CHUNK)
    x_w = xs * wide(deltas, CHUNK)
    st = _mm_tn(x_w * jnp.exp(wide(tots, 1) - a_w), bm)
    y_off = _mm_nt(cm, state) * jnp.exp(a_w)
    grow = jnp.concatenate([jnp.broadcast_to(jnp.exp(t), (hd, 1)) for t in tots], axis=0)
    cb = _mm_nt(cm, bm)
    m_cat = jnp.concatenate([cb * jnp.exp(jnp.where(mask, a_cols[r] - a_rows[r], -jnp.inf)) for r in range(SSD_HPG)], axis=1)
    lane_head = lax.broadcasted_iota(jnp.int32, (1, SSD_HPG * hd), 1) // hd
    x_bd = jnp.concatenate([jnp.where(lane_head == r, x_w, 0.0) for r in range(SSD_HPG)], axis=0)
    return _mm_nn(m_cat, x_bd) + y_off, grow * state + st


def _ssd_calls(name, xs, bm, cm, s0, direction, reverse):
    n_rows = xs.shape[0]
    nc = n_rows // CHUNK
    gw = SSD_HPG * SSD_HEADDIM
    vmem = 4 * CHUNK * (gw + 3 * 128) * 8 + 4 * gw * 128 * 12 + (8 << 20)

    n_state = SSD_GROUPS * gw
    shared_scratch = [pltpu.VMEM((CHUNK, LANES), F32), pltpu.VMEM((CHUNK, LANES), F32), pltpu.VMEM((LANES, CHUNK), F32)]

    def specs(order, gps=1):
        def cidx(c):
            return (nc - 1 - c) if order else c

        return dict(
            xs=pl.BlockSpec((CHUNK, gps * gw), lambda c, g: (cidx(c), g)),
            bc=pl.BlockSpec((CHUNK, gps * SSD_STATE), lambda c, g: (cidx(c), g)),
            small=pl.BlockSpec((CHUNK, LANES), lambda c, g: (cidx(c), 0)),
            row=pl.BlockSpec((1, LANES), lambda c, g: (0, 0)),
            state=pl.BlockSpec((n_state, SSD_STATE), lambda c, g: (0, 0)),
            enter=pl.BlockSpec((1, gps * gw, SSD_STATE), lambda c, g: (cidx(c), g, 0)),
        )

    gps_fwd, gps_bwd = SSD_GROUPS, 1

    def group_rows(g, gps):
        return pl.ds(pl.multiple_of(g * gps * gw, gps * gw), gps * gw)

    def step_fn(g, gps):
        def fn(xs_v, bm_v, cm_v, st_v, d_all, a_all, a_t):
            ys, sts = [], []
            for u in range(gps):
                y_u, s_u = _ssd_group(xs_v[:, u * gw:(u + 1) * gw], bm_v[:, u * SSD_STATE:(u + 1) * SSD_STATE],
                                      cm_v[:, u * SSD_STATE:(u + 1) * SSD_STATE], st_v[u * gw:(u + 1) * gw],
                                      d_all, a_all, a_t, gps * g + u, direction, reverse)
                ys.append(y_u)
                sts.append(s_u)
            return jnp.concatenate(ys, axis=1), jnp.concatenate(sts, axis=0)

        return fn

    def fwd_call(xs, bm, cm, small, bias_row, alog_row, s0):
        gps = gps_fwd
        sp = specs(reverse, gps)

        def kern(xs_r, bm_r, cm_r, sm_r, br_r, ar_r, s0_r, y_r, sf_r, se_r, st, sh_d, sh_a, sh_t):
            c, g = pl.program_id(0), pl.program_id(1)

            @pl.when((c == 0) & (g == 0))
            def _():
                st[...] = s0_r[...]

            @pl.when(g == 0)
            def _():
                sh_d[...], sh_a[...], sh_t[...] = _ssd_shared(sm_r[...], br_r[...], ar_r[...], reverse)

            rows = group_rows(g, gps)
            s_in = st[rows, :]
            se_r[0] = s_in
            y_r[...], s_new = step_fn(g, gps)(xs_r[...], bm_r[...], cm_r[...], s_in, sh_d[...], sh_a[...], sh_t[...])
            st[rows, :] = s_new
            sf_r[rows, :] = s_new

        return pl.pallas_call(
            kern, grid=(nc, SSD_GROUPS // gps), name=name + "_fwd",
            in_specs=[sp['xs'], sp['bc'], sp['bc'], sp['small'], sp['row'], sp['row'], sp['state']],
            out_specs=[sp['xs'], sp['state'], sp['enter']],
            out_shape=[jax.ShapeDtypeStruct((n_rows, SSD_INNER), F32), jax.ShapeDtypeStruct((n_state, SSD_STATE), F32),
                       jax.ShapeDtypeStruct((nc, n_state, SSD_STATE), F32)],
            scratch_shapes=[pltpu.VMEM((n_state, SSD_STATE), F32)] + shared_scratch, compiler_params=_params(vmem),
        )(xs, bm, cm, small, bias_row, alog_row, s0)

    def bwd_call(xs, bm, cm, small, bias_row, alog_row, enter, dy, dsf, acc=()):
        gps = gps_bwd
        sp = specs(not reverse, gps)

        def kern(*refs):
            xs_r, bm_r, cm_r, sm_r, br_r, ar_r, se_r, dy_r, dsf_r = refs[:9]
            acc_r = refs[9:9 + len(acc)]
            dxs_r, dbm_r, dcm_r, dsm_r, dbr_r, dar_r, ds0_r, ds, sh_d, sh_a, sh_t, gd, ga, gt = refs[9 + len(acc):]
            c, g = pl.program_id(0), pl.program_id(1)

            @pl.when((c == 0) & (g == 0))
            def _():
                ds[...] = dsf_r[...]
                dbr_r[...] = jnp.zeros_like(dbr_r)
                dar_r[...] = jnp.zeros_like(dar_r)

            @pl.when(g == 0)
            def _():
                sh_d[...], sh_a[...], sh_t[...] = _ssd_shared(sm_r[...], br_r[...], ar_r[...], reverse)
                gd[...] = jnp.zeros_like(gd)
                ga[...] = jnp.zeros_like(ga)
                gt[...] = jnp.zeros_like(gt)

            rows = group_rows(g, gps)
            _, vjp = jax.vjp(step_fn(g, gps), xs_r[...], bm_r[...], cm_r[...], se_r[0], sh_d[...], sh_a[...], sh_t[...])
            dxs, dbm, dcm, ds_in, dd, da, dt = vjp((dy_r[...], ds[rows, :]))
            dxs_r[...] = dxs + acc_r[0][...] if acc else dxs
            dbm_r[...] = dbm + acc_r[1][...] if acc else dbm
            dcm_r[...] = dcm + acc_r[2][...] if acc else dcm
            ds[rows, :] = ds_in
            ds0_r[rows, :] = ds_in
            gd[...] += dd
            ga[...] += da
            gt[...] += dt

            @pl.when(g == SSD_GROUPS // gps - 1)
            def _():
                shared = functools.partial(_ssd_shared, reverse=reverse)
                dsm, dbr, dar = jax.vjp(shared, sm_r[...], br_r[...], ar_r[...])[1]((gd[...], ga[...], gt[...]))
                dsm_r[...] = dsm + acc_r[3][...] if acc else dsm
                dbr_r[...] += dbr
                dar_r[...] += dar

        return pl.pallas_call(
            kern, grid=(nc, SSD_GROUPS // gps), name=name + "_bwd",
            in_specs=[sp['xs'], sp['bc'], sp['bc'], sp['small'], sp['row'], sp['row'], sp['enter'], sp['xs'], sp['state']]
            + ([sp['xs'], sp['bc'], sp['bc'], sp['small']] if acc else []),
            out_specs=[sp['xs'], sp['bc'], sp['bc'], sp['small'], sp['row'], sp['row'], sp['state']],
            out_shape=[jax.ShapeDtypeStruct(xs.shape, F32), jax.ShapeDtypeStruct(bm.shape, F32),
                       jax.ShapeDtypeStruct(cm.shape, F32), jax.ShapeDtypeStruct((n_rows, LANES), F32),
                       jax.ShapeDtypeStruct((1, LANES), F32), jax.ShapeDtypeStruct((1, LANES), F32),
                       jax.ShapeDtypeStruct(s0.shape, F32)],
            scratch_shapes=[pltpu.VMEM((n_state, SSD_STATE), F32)] + shared_scratch + shared_scratch,
            compiler_params=_params(vmem),
        )(xs, bm, cm, small, bias_row, alog_row, enter, dy, dsf, *acc)

    return fwd_call, bwd_call


def _ssd_pair(name, xs, bm, cm, small, rows_f, rows_b, s0_f, s0_b):
    calls = [_ssd_calls(name + "%d" % d, xs, bm, cm, s0_f, d, d == 1) for d in range(2)]

    def run_fwd(xs, bm, cm, small, rows_f, rows_b, s0_f, s0_b):
        y_f, sf_f, en_f = calls[0][0](xs, bm, cm, small, *rows_f, s0_f)
        y_b, sf_b, en_b = calls[1][0](xs, bm, cm, small, *rows_b, s0_b)
        return (y_f, y_b, sf_f, sf_b), (en_f, en_b)

    @jax.custom_vjp
    def op(*args):
        return run_fwd(*args)[0]

    def op_fwd(*args):
        outs, enters = run_fwd(*args)
        return outs, (args[:6], enters)

    def op_bwd(res, g):
        (xs, bm, cm, small, rows_f, rows_b), (en_f, en_b) = res
        dy_f, dy_b, dsf_f, dsf_b = g
        dxs, dbm, dcm, dsm, dbr_f, dar_f, ds0_f = calls[0][1](xs, bm, cm, small, *rows_f, en_f, dy_f, dsf_f)
        dxs, dbm, dcm, dsm, dbr_b, dar_b, ds0_b = calls[1][1](xs, bm, cm, small, *rows_b, en_b, dy_b, dsf_b, acc=(dxs, dbm, dcm, dsm))
        return dxs, dbm, dcm, dsm, (dbr_f, dar_f), (dbr_b, dar_b), ds0_f, ds0_b

    op.defvjp(op_fwd, op_bwd)
    return op(xs, bm, cm, small, tuple(rows_f), tuple(rows_b), s0_f, s0_b)


def _ml_shared(small, gate_row, reverse):
    gates = small + gate_row
    b_all = _exact_dot(_tri(reverse).astype(F32), jax.nn.log_sigmoid(gates))
    return gates, b_all, gates.T, b_all.T


def _ml_head(q, k, v, c_st, n_st, m_st, gates, b_all, gates_t, b_t, h, direction, reverse):
    mask = _tri(reverse)
    last = 0 if reverse else CHUNK - 1
    lane_i = _MG_LANE + 8 * direction + h
    lane_f = lane_i + ML_HEADS
    b_col = _pick_col(b_all, lane_f)
    b_row = _pick_row(b_t, lane_f)
    li_col = _pick_col(gates, lane_i)
    li_row = _pick_row(gates_t, lane_i)
    rowi = lax.broadcasted_iota(jnp.int32, (CHUNK, 1), 0)
    g_tot = jnp.sum(jnp.where(rowi == last, b_col, 0.0), axis=0, keepdims=True)
    m_in = m_st[:, 0:1]
    q = q * (ML_HD ** -0.5)
    w = g_tot - b_col + li_col
    m_loc = lax.stop_gradient(jnp.max(w, axis=0, keepdims=True))
    kw = k * jnp.exp(w - m_loc)
    c_loc = _mm_tn(kw, v)
    n_loc = jnp.sum(kw, axis=0, keepdims=True)
    m_new = lax.stop_gradient(jnp.maximum(g_tot + m_in, m_loc))
    s_old = jnp.exp(g_tot + m_in - m_new)
    s_loc = jnp.exp(m_loc - m_new)
    c_new = s_old * c_st + s_loc * c_loc
    n_new = s_old * n_st + s_loc * n_loc
    log_d = jnp.where(mask, b_col - b_row + li_row, -jnp.inf)
    inter = b_col + m_in
    m_t = lax.stop_gradient(jnp.maximum(inter, jnp.max(log_d, axis=1, keepdims=True)))
    dmat = jnp.exp(log_d - m_t)
    wi = jnp.exp(inter - m_t)
    s = _mm_nt(q, k) * dmat
    num = _mm_nn(s, v) + wi * _mm_nn(q, c_st)
    den = jnp.sum(s, axis=1, keepdims=True) + wi * jnp.sum(_round_bf16(q) * _round_bf16(n_st), axis=1, keepdims=True)
    out = num / jnp.maximum(jnp.abs(den), jnp.exp(-m_t))
    return out, c_new, n_new, jnp.broadcast_to(m_new, (1, LANES))


def _ml_calls(name, q, direction, reverse):
    n_rows = q.shape[0]
    nc = n_rows // CHUNK
    vmem = 4 * CHUNK * (4 * ML_HD + 128) * 8 + 4 * ML_HD * ML_HD * 12 + (8 << 20)

    def specs(order, hps=1):
        def cidx(c):
            return (nc - 1 - c) if order else c

        return dict(
            qkv=pl.BlockSpec((CHUNK, hps * ML_HD), lambda c, h: (cidx(c), h)),
            small=pl.BlockSpec((CHUNK, LANES), lambda c, h: (cidx(c), 0)),
            row=pl.BlockSpec((1, LANES), lambda c, h: (0, 0)),
            c=pl.BlockSpec((ML_HEADS * ML_HD, ML_HD), lambda c, h: (0, 0)),
            n=pl.BlockSpec((ML_HEADS, 1, ML_HD), lambda c, h: (0, 0, 0)),
            m=pl.BlockSpec((ML_HEADS, 1, LANES), lambda c, h: (0, 0, 0)),
            ec=pl.BlockSpec((1, hps * ML_HD, ML_HD), lambda c, h: (cidx(c), h, 0)),
            en=pl.BlockSpec((1, hps, 1, ML_HD), lambda c, h: (cidx(c), h, 0, 0)),
            em=pl.BlockSpec((1, hps, 1, LANES), lambda c, h: (cidx(c), h, 0, 0)),
        )

    st_shapes = [jax.ShapeDtypeStruct((ML_HEADS * ML_HD, ML_HD), F32), jax.ShapeDtypeStruct((ML_HEADS, 1, ML_HD), F32),
                 jax.ShapeDtypeStruct((ML_HEADS, 1, LANES), F32)]
    scratch = [pltpu.VMEM((ML_HEADS * ML_HD, ML_HD), F32), pltpu.VMEM((ML_HEADS, 1, ML_HD), F32),
               pltpu.VMEM((ML_HEADS, 1, LANES), F32)]
    shared_scratch = [pltpu.VMEM((CHUNK, LANES), F32), pltpu.VMEM((CHUNK, LANES), F32),
                      pltpu.VMEM((LANES, CHUNK), F32), pltpu.VMEM((LANES, CHUNK), F32)]

    def head_rows(h):
        return pl.ds(pl.multiple_of(h * ML_HD, ML_HD), ML_HD)

    def fwd_call(q, k, v, small, gate_row, c0, n0, m0):
        hps = ML_HEADS
        sp = specs(reverse, hps)

        def kern(q_r, k_r, v_r, sm_r, gr_r, c0_r, n0_r, m0_r, o_r, cf_r, nf_r, mf_r, ec_r, en_r, em_r, cs, ns, ms, *sh):
            c, hh = pl.program_id(0), pl.program_id(1)

            @pl.when((c == 0) & (hh == 0))
            def _():
                cs[...] = c0_r[...]
                ns[...] = n0_r[...]
                ms[...] = m0_r[...]

            @pl.when(hh == 0)
            def _():
                for ref, val in zip(sh, _ml_shared(sm_r[...], gr_r[...], reverse)):
                    ref[...] = val

            q_v, k_v, v_v = q_r[...], k_r[...], v_r[...]
            shared = [r[...] for r in sh]
            heads = [hps * hh + u for u in range(hps)]
            states = [(cs[head_rows(h), :], ns[h], ms[h]) for h in heads]
            results = []
            for u, (h, (c_in, n_in, m_in)) in enumerate(zip(heads, states)):
                cols = slice(u * ML_HD, (u + 1) * ML_HD)
                ec_r[0, cols, :] = c_in
                en_r[0, u] = n_in
                em_r[0, u] = m_in
                results.append(_ml_head(q_v[:, cols], k_v[:, cols], v_v[:, cols], c_in, n_in, m_in, *shared,
                                        h, direction, reverse))
            o_r[...] = jnp.concatenate([r[0] for r in results], axis=1)
            for h, (_, c_new, n_new, m_new) in zip(heads, results):
                cs[head_rows(h), :] = c_new
                ns[h] = n_new
                ms[h] = m_new
                cf_r[head_rows(h), :] = c_new
                nf_r[h] = n_new
                mf_r[h] = m_new

        return pl.pallas_call(
            kern, grid=(nc, ML_HEADS // hps), name=name + "_fwd",
            in_specs=[sp['qkv']] * 3 + [sp['small'], sp['row'], sp['c'], sp['n'], sp['m']],
            out_specs=[sp['qkv'], sp['c'], sp['n'], sp['m'], sp['ec'], sp['en'], sp['em']],
            out_shape=[jax.ShapeDtypeStruct((n_rows, ML_HEADS * ML_HD), F32)] + st_shapes + [
                jax.ShapeDtypeStruct((nc, ML_HEADS * ML_HD, ML_HD), F32),
                jax.ShapeDtypeStruct((nc, ML_HEADS, 1, ML_HD), F32), jax.ShapeDtypeStruct((nc, ML_HEADS, 1, LANES), F32)],
            scratch_shapes=scratch + shared_scratch, compiler_params=_params(vmem),
        )(q, k, v, small, gate_row, c0, n0, m0)

    def bwd_call(q, k, v, small, gate_row, ec, en, em, do, dcf, dnf, dmf, acc=()):
        sp = specs(not reverse)
        n_sh = len(shared_scratch)

        def kern(*refs):
            q_r, k_r, v_r, sm_r, gr_r, ec_r, en_r, em_r, do_r, dcf_r, dnf_r, dmf_r = refs[:12]
            acc_r = refs[12:12 + len(acc)]
            dq_r, dk_r, dv_r, dsm_r, dgr_r, dc0_r, dn0_r, dm0_r, dcs, dns, dms = refs[12 + len(acc):23 + len(acc)]
            rest = refs[23 + len(acc):]
            sh, gsh = rest[:n_sh], rest[n_sh:]
            c, h = pl.program_id(0), pl.program_id(1)

            @pl.when((c == 0) & (h == 0))
            def _():
                dcs[...] = dcf_r[...]
                dns[...] = dnf_r[...]
                dms[...] = dmf_r[...]
                dgr_r[...] = jnp.zeros_like(dgr_r)

            @pl.when(h == 0)
            def _():
                for ref, val in zip(sh, _ml_shared(sm_r[...], gr_r[...], reverse)):
                    ref[...] = val
                for ref in gsh:
                    ref[...] = jnp.zeros_like(ref)

            rows = head_rows(h)
            fn = functools.partial(_ml_head, h=h, direction=direction, reverse=reverse)
            _, vjp = jax.vjp(fn, q_r[...], k_r[...], v_r[...], ec_r[0], en_r[0, 0], em_r[0, 0], *[r[...] for r in sh])
            grads = vjp((do_r[...], dcs[rows, :], dns[h], dms[h]))
            dq, dk, dv, dc, dn, dm = grads[:6]
            dq_r[...] = dq + acc_r[0][...] if acc else dq
            dk_r[...] = dk + acc_r[1][...] if acc else dk
            dv_r[...] = dv + acc_r[2][...] if acc else dv
            for ref, val in zip(gsh, grads[6:]):
                ref[...] += val
            dm = jnp.broadcast_to(jnp.sum(dm, axis=1, keepdims=True), (1, LANES)) * (1.0 / LANES)
            dcs[rows, :] = dc
            dns[h] = dn
            dms[h] = dm
            dc0_r[rows, :] = dc
            dn0_r[h] = dn
            dm0_r[h] = dm

            @pl.when(h == ML_HEADS - 1)
            def _():
                shared = functools.partial(_ml_shared, reverse=reverse)
                dsm, dgr = jax.vjp(shared, sm_r[...], gr_r[...])[1](tuple(r[...] for r in gsh))
                dsm_r[...] = dsm + acc_r[3][...] if acc else dsm
                dgr_r[...] += dgr

        return pl.pallas_call(
            kern, grid=(nc, ML_HEADS), name=name + "_bwd",
            in_specs=[sp['qkv']] * 3 + [sp['small'], sp['row'], sp['ec'], sp['en'], sp['em'], sp['qkv'], sp['c'], sp['n'], sp['m']]
            + ([sp['qkv']] * 3 + [sp['small']] if acc else []),
            out_specs=[sp['qkv']] * 3 + [sp['small'], sp['row'], sp['c'], sp['n'], sp['m']],
            out_shape=[jax.ShapeDtypeStruct(q.shape, F32)] * 3 + [jax.ShapeDtypeStruct((n_rows, LANES), F32),
                                                                  jax.ShapeDtypeStruct((1, LANES), F32)] + st_shapes,
            scratch_shapes=scratch + shared_scratch + shared_scratch, compiler_params=_params(vmem),
        )(q, k, v, small, gate_row, ec, en, em, do, dcf, dnf, dmf, *acc)

    return fwd_call, bwd_call


def _ml_pair(name, q, k, v, small, gate_row, state_f, state_b):
    calls = [_ml_calls(name + "%d" % d, q, d, d == 1) for d in range(2)]

    def run_fwd(q, k, v, small, gate_row, state_f, state_b):
        res_f = calls[0][0](q, k, v, small, gate_row, *state_f)
        res_b = calls[1][0](q, k, v, small, gate_row, *state_b)
        return (res_f[0], res_b[0], tuple(res_f[1:4]), tuple(res_b[1:4])), (tuple(res_f[4:]), tuple(res_b[4:]))

    @jax.custom_vjp
    def op(*args):
        return run_fwd(*args)[0]

    def op_fwd(*args):
        outs, enters = run_fwd(*args)
        return outs, (args[:5], enters)

    def op_bwd(res, g):
        (q, k, v, small, gate_row), (en_f, en_b) = res
        do_f, do_b, ds_f, ds_b = g
        dq, dk, dv, dsm, dgr_f, *d0_f = calls[0][1](q, k, v, small, gate_row, *en_f, do_f, *ds_f)
        dq, dk, dv, dsm, dgr_b, *d0_b = calls[1][1](q, k, v, small, gate_row, *en_b, do_b, *ds_b, acc=(dq, dk, dv, dsm))
        return dq, dk, dv, dsm, dgr_f + dgr_b, tuple(d0_f), tuple(d0_b)

    op.defvjp(op_fwd, op_bwd)
    return op(q, k, v, small, gate_row, tuple(state_f), tuple(state_b))


def _f_modulate(x, shift, scale):
    return (_layernorm_rows(x) * (1.0 + scale) + shift,)


def _f_resid_ln(x, o, gate, bias, ln_g, ln_b):
    return (_layernorm_rows(DN_ALPHA * x + gate * (o + bias)) * ln_g + ln_b,)


def _f_lru_gates(xc, w_r, b_r, w_i, b_i, lam):
    outs = []
    for d in range(2):
        def blockdiag(w):
            return jnp.concatenate(
                [_mm_nn(xc[:, n * LRU_BS:(n + 1) * LRU_BS], w[(d * LRU_BLOCKS + n) * LRU_BS:(d * LRU_BLOCKS + n + 1) * LRU_BS, :])
                 for n in range(LRU_BLOCKS)], axis=1)

        r = jax.nn.sigmoid(blockdiag(w_r) + b_r[d:d + 1])
        i = jax.nn.sigmoid(blockdiag(w_i) + b_i[d:d + 1])
        log_a = -LRU_C * r * jax.nn.softplus(-lam[d:d + 1])
        outs += [jnp.exp(log_a), jnp.sqrt(1.0 - jnp.exp(2.0 * log_a)) * i * xc]
    return tuple(outs)


def _f_lru_out(h_f, h_b, ly):
    return ((h_f + h_b) * jax.nn.gelu(ly),)


def _f_ssd_post(y_f, y_b, xs, z, d_exp, norm_w):
    y = (y_f + y_b + xs * d_exp) * jax.nn.silu(z)
    gw = SSD_INNER // SSD_GROUPS
    parts = []
    for g in range(SSD_GROUPS):
        yg = y[:, g * gw:(g + 1) * gw]
        parts.append(yg * lax.rsqrt(jnp.mean(jnp.square(yg), -1, keepdims=True) + LN_EPS))
    return (jnp.concatenate(parts, axis=1) * norm_w,)


def _f_ml_post(h_f, h_b, o, norm_w):
    h = h_f + h_b
    parts = [_layernorm_rows(h[:, i * ML_HD:(i + 1) * ML_HD]) for i in range(ML_HEADS)]
    return (jnp.concatenate(parts, axis=1) * norm_w * jax.nn.sigmoid(o),)


def _f_merge(ga, gb, gc, pa, pb, pc):
    return (jax.nn.sigmoid(ga) * pa + jax.nn.sigmoid(gb) * pb + jax.nn.sigmoid(gc) * pc,)


def _f_relu2(pre, bias):
    return (jnp.square(jax.nn.relu(pre + bias)),)


def _lane_row(vec, start):
    return jnp.pad(vec[None], ((0, 0), (start, LANES - start - vec.shape[0])))


def _mixer(tag, x_tok, shift, scale, p, states):
    (lru_s, ssd_s, ml_s) = states
    lx, ly, sz, xs, bm, cm, mq, mk, mv, mo, ga, gb, gc, small = _rowwise_linear(
        tag + "in", _f_modulate, [x_tok], [shift, scale], [(p['w_in_main'], _IN_MAIN_WIDTHS), (p['w_in_small'], None)])

    xc = _dwconv(tag + "lruconv", lx, p['lru_conv_w'], p['lru_conv_b'][None], False)
    a_f, b_f, a_b, b_b = _rowwise(
        tag + "lrugate", _f_lru_gates, [xc],
        [p['lru_w_r'].reshape(2 * LRU_BLOCKS * LRU_BS, LRU_BS), p['lru_b_r'], p['lru_w_i'].reshape(2 * LRU_BLOCKS * LRU_BS, LRU_BS),
         p['lru_b_i'], p['lru_lambda']], [D_MODEL] * 4, tile_cap=128)
    h_f, s_f = _lin_scan(tag + "lruscanf", a_f, b_f, lru_s[0], False)
    h_b, s_b = _lin_scan(tag + "lruscanb", a_b, b_b, lru_s[1], True)
    (pa,) = _rowwise_linear(tag + "bra", _f_lru_out, [h_f, h_b, ly], [], [(p['w_br_a'], None)], to_linear=(2,))

    cw, cb_ = p['ssd_conv_w'], p['ssd_conv_b'][None]
    xs_c = _dwconv(tag + "ssdconvx", xs, cw[:, :2048], cb_[:, :2048], True)
    bm_c = _dwconv(tag + "ssdconvb", bm, cw[:, 2048:3072], cb_[:, 2048:3072], True)
    cm_c = _dwconv(tag + "ssdconvc", cm, cw[:, 3072:], cb_[:, 3072:], True)
    dir_rows = [(_lane_row(p['ssd_dt_bias'][d], _DT_LANE + 32 * d), _lane_row(p['ssd_a_log'][d], _DT_LANE + 32 * d)) for d in range(2)]
    *ys, st_f, st_b = _ssd_pair(tag + "ssd", xs_c, bm_c, cm_c, small, dir_rows[0], dir_rows[1], ssd_s[0], ssd_s[1])
    ssd_new = (st_f, st_b)
    (pb,) = _rowwise_linear(tag + "brb", _f_ssd_post, [ys[0], ys[1], xs_c, sz],
                            [jnp.repeat(p['ssd_d'], SSD_HEADDIM)[None], p['ssd_norm_w'][None]], [(p['w_br_b'], None)], to_linear=(3,))

    mw, mb = p['ml_conv_w'], p['ml_conv_b'][None]
    q_c = _dwconv(tag + "mlconvq", mq, mw[:, :1024], mb[:, :1024], True)
    k_c = _dwconv(tag + "mlconvk", mk, mw[:, 1024:], mb[:, 1024:], True)
    gate_row = _lane_row(p['ml_gate_b'].reshape(4 * ML_HEADS), _MG_LANE)
    *hs, ml_f, ml_b = _ml_pair(tag + "ml", q_c, k_c, mv, small, gate_row, ml_s[0], ml_s[1])
    ml_new = (ml_f, ml_b)
    (pc,) = _rowwise_linear(tag + "brc", _f_ml_post, [hs[0], hs[1], mo], [p['ml_norm_w'][None]], [(p['w_br_c'], None)], to_linear=(2,))
    return (ga, gb, gc, pa, pb, pc), ((s_f, s_b), tuple(ssd_new), tuple(ml_new))


def _merge(tag, br, p):
    return _rowwise_linear(tag + "out", _f_merge, list(br), [], [(p['w_out'], None)], to_linear=tuple(range(6)))[0]


def _sublayers(tag, xin, o, mods, p, l):
    sh2, sc2, g1, g2 = mods
    (x1,) = _rowwise(tag + "ln1", _f_resid_ln, [xin, o], [g1, p['b_out'][None], p['ln1_g'][None], p['ln1_b'][None]], [D_MODEL], to_linear=(1,))
    (pre,) = _rowwise_linear(tag + "ff1", _f_modulate, [x1], [sh2, sc2], [(p['w_ff1'], None)])
    (o2,) = _rowwise_linear(tag + "ff2", _f_relu2, [pre], [p['b_ff1'][None]], [(p['w_ff2'], None)], to_linear=(0,))
    (x2,) = _rowwise(tag + "ln2", _f_resid_ln, [x1, o2], [g2, p['b_ff2'][None], p['ln2_g'][None], p['ln2_b'][None]], [D_MODEL], to_linear=(1,))
    return x2


def _to_col_major(h):
    s, d = h.shape
    return h.reshape(s // GRID_W, GRID_W, d).swapaxes(0, 1).reshape(s, d)


def _from_col_major(h):
    s, d = h.shape
    return h.reshape(GRID_W, s // GRID_W, d).swapaxes(0, 1).reshape(s, d)


def _forward(x, wts, mods, ctx):
    zeros = lambda *s: jnp.zeros(s, F32)
    ctx_init = ((zeros(1, D_MODEL), zeros(1, D_MODEL)),
                (zeros(SSD_INNER, SSD_STATE), zeros(SSD_INNER, SSD_STATE)),
                tuple((zeros(ML_HEADS * ML_HD, ML_HD), zeros(ML_HEADS, 1, ML_HD), zeros(ML_HEADS, 1, LANES)) for _ in range(2)))
    for l in range(DEPTH):
        p = {n: wts[n][l] for n in wts}
        tag = "l%d" % l
        sh1x, sc1x, g1x, sh2x, sc2x, g2x = [mods[l][0][:, i * D_MODEL:(i + 1) * D_MODEL] for i in range(6)]
        sh1c, sc1c, g1c, sh2c, sc2c, g2c = [mods[l][1][:, i * D_MODEL:(i + 1) * D_MODEL] for i in range(6)]
        br_c, ctx_states = _mixer(tag + "c", ctx, sh1c, sc1c, p, ctx_init)
        br_x, _ = _mixer(tag + "x", _to_col_major(x) if l % 2 == 1 else x, sh1x, sc1x, p, ctx_states)
        ox = _merge(tag + "x", br_x, p)
        if l % 2 == 1:
            ox = _from_col_major(ox)
        x = _sublayers(tag + "x", x, ox, (sh2x, sc2x, g1x, g2x), p, l)
        if l < DEPTH - 1:
            ctx = _sublayers(tag + "c", ctx, _merge(tag + "c", br_c, p), (sh2c, sc2c, g1c, g2c), p, l)
    return x


_ADA_ROWS = 2 * SUBLANES


def _ada_forward(c, c_ctx, w_ada, b_ada, me):
    c_all = _exchange("gather_c", jnp.broadcast_to(c, (SUBLANES, D_MODEL)), True)[:, 0]

    def rows_of(c_ctx_):
        pad = jnp.zeros((_ADA_ROWS - N_DEV - 1, D_MODEL), F32)
        return jax.nn.silu(jnp.concatenate([c_all, c_ctx_[None], pad], axis=0))

    rows, vjp_rows = jax.vjp(rows_of, c_ctx)
    cols, vjp_cols = jax.vjp(lambda r, w: jnp.stack([_linear("ada%d" % l, r, w[l]) for l in range(DEPTH)]), rows, w_ada)
    full = _exchange("gather_mod", cols, True).transpose(1, 2, 0, 3).reshape(DEPTH, _ADA_ROWS, 6 * D_MODEL) + b_ada[:, None, :]
    mods = [(lax.dynamic_slice_in_dim(full[l], me, 1, axis=0), full[l][N_DEV:N_DEV + 1]) for l in range(DEPTH)]
    return mods, (vjp_rows, vjp_cols)


def _ada_backward(saved, dmods):
    vjp_rows, vjp_cols = saved
    wcol = 6 * D_MODEL // N_DEV
    pad = jnp.zeros((SUBLANES - 2, 6 * D_MODEL), F32)
    both = jnp.stack([jnp.concatenate([dx, dc, pad], axis=0) for dx, dc in dmods])
    send = both.reshape(DEPTH, SUBLANES, N_DEV, wcol).transpose(2, 0, 1, 3)
    recv = _exchange("scatter_dmod", send, False)
    ctx_row = recv[0, :, 1]
    for k in range(1, N_DEV):
        ctx_row = ctx_row + recv[k, :, 1]
    g = jnp.concatenate([recv[:, :, 0].transpose(1, 0, 2), ctx_row[:, None],
                         jnp.zeros((DEPTH, _ADA_ROWS - N_DEV - 1, wcol), F32)], axis=1)
    d_rows, d_w = vjp_cols(g)
    (d_c_ctx,) = vjp_rows(d_rows)
    d_b = jnp.stack([(dx + dc)[0] for dx, dc in dmods])
    return d_w, d_b, d_c_ctx


def _loss_and_cotangent(y, target):
    n_rows, d = y.shape
    tt = _row_tile(n_rows, 0, cap=256)

    def kern(y_ref, t_ref, dy_ref, acc_ref):
        @pl.when(pl.program_id(0) == 0)
        def _():
            acc_ref[...] = jnp.zeros_like(acc_ref)

        err = y_ref[...] - t_ref[...]
        dy_ref[...] = err * (1.0 / d)
        acc_ref[...] += jnp.sum(jnp.square(err))

    spec = pl.BlockSpec((tt, d), lambda i: (i, 0))
    dy, acc = pl.pallas_call(
        kern, grid=(n_rows // tt,), name="loss", in_specs=[spec, spec],
        out_specs=[spec, pl.BlockSpec((SUBLANES, LANES), lambda i: (0, 0))],
        out_shape=[jax.ShapeDtypeStruct((n_rows, d), F32), jax.ShapeDtypeStruct((SUBLANES, LANES), F32)],
    )(y, target)
    return acc[0, 0] * (0.5 / d), dy


def _exchange(name, src, gather):
    slab = src.shape if gather else src.shape[1:]

    def body(src_ref, out_ref, send_sems, recv_sems, local_sem):
        x, y, c = lax.axis_index("x"), lax.axis_index("y"), lax.axis_index("c")
        me = 4 * x + 2 * y + c
        local = pltpu.make_async_copy(src_ref if gather else src_ref.at[me], out_ref.at[me], local_sem)
        local.start()
        copies = []
        for d in range(1, N_DEV):
            px, py, pc = lax.rem(x + (d >> 2), 2), lax.rem(y + ((d >> 1) & 1), 2), lax.rem(c + (d & 1), 2)
            peer = 4 * px + 2 * py + pc
            cp = pltpu.make_async_remote_copy(
                src_ref=src_ref if gather else src_ref.at[peer], dst_ref=out_ref.at[me],
                send_sem=send_sems.at[d - 1], recv_sem=recv_sems.at[d - 1],
                device_id=(px, py, pc), device_id_type=pl.DeviceIdType.MESH)
            cp.start()
            copies.append(cp)
        for cp in copies:
            cp.wait()
        local.wait()

    return pl.pallas_call(
        body, name=name, out_shape=jax.ShapeDtypeStruct((N_DEV,) + tuple(slab), src.dtype),
        in_specs=[pl.BlockSpec(memory_space=pl.ANY)], out_specs=pl.BlockSpec(memory_space=pl.ANY),
        scratch_shapes=[pltpu.SemaphoreType.DMA((N_DEV - 1,)), pltpu.SemaphoreType.DMA((N_DEV - 1,)), pltpu.SemaphoreType.DMA],
    )(src)


_HBM = pl.BlockSpec(memory_space=pl.ANY)
_CHIPS = ((0, 0), (0, 1), (1, 0), (1, 1))


def _gather_two_level(name, srcs):
    n = len(srcs)

    def body(*refs):
        src_refs, out_refs = refs[:n], refs[n:2 * n]
        send_sems, recv_sems, local_sems = refs[2 * n:]
        x, y, c = lax.axis_index("x"), lax.axis_index("y"), lax.axis_index("c")
        me, sibling = (x, y, c), (x, y, 1 - c)
        chips = [(1 - x, y), (x, 1 - y), (1 - x, 1 - y)]

        def slab(a, px, py, pc):
            return out_refs[a].at[4 * px + 2 * py + pc]

        def copy(a, k, block, to, own=False):
            return pltpu.make_async_remote_copy(
                src_ref=src_refs[a] if own else slab(a, *block), dst_ref=slab(a, *block), send_sem=send_sems.at[a, k],
                recv_sem=recv_sems.at[a, k], device_id=to, device_id_type=pl.DeviceIdType.MESH)

        mine = [pltpu.make_async_copy(src_refs[a], slab(a, *me), local_sems.at[a]) for a in range(n)]
        first = []
        for a in range(n):
            mine[a].start()
            first += [copy(a, 0, me, sibling, own=True)] + [copy(a, 1 + j, me, (*chip, c), own=True) for j, chip in enumerate(chips)]
        for cp in first:
            cp.start()
        passed = []
        for j, chip in enumerate(chips):
            for a in range(n):
                copy(a, 1 + j, (*chip, c), me).wait_recv()
                passed.append(copy(a, 4 + j, (*chip, c), sibling))
                passed[-1].start()
        for a in range(n):
            copy(a, 0, sibling, me).wait_recv()
        for j, chip in enumerate(chips):
            for a in range(n):
                copy(a, 4 + j, (*chip, 1 - c), me).wait_recv()
        for cp in first + passed:
            cp.wait_send()
        for cp in mine:
            cp.wait()

    return pl.pallas_call(
        body, name=name, out_shape=[jax.ShapeDtypeStruct((N_DEV,) + tuple(s.shape), s.dtype) for s in srcs],
        in_specs=[_HBM] * n, out_specs=[_HBM] * n,
        scratch_shapes=[pltpu.SemaphoreType.DMA((n, N_DEV - 1)), pltpu.SemaphoreType.DMA((n, N_DEV - 1)), pltpu.SemaphoreType.DMA((n,))],
    )(*srcs)


def _scatter_to_sibling(name, parts_list):
    n = len(parts_list)

    def body(*refs):
        p_refs, out_refs = refs[:n], refs[n:2 * n]
        send_sems, recv_sems = refs[2 * n:]
        x, y, c = lax.axis_index("x"), lax.axis_index("y"), lax.axis_index("c")
        copies = []
        for a in range(n):
            for j, (px, py) in enumerate(_CHIPS):
                cp = pltpu.make_async_remote_copy(
                    src_ref=p_refs[a].at[4 * px + 2 * py + (1 - c)], dst_ref=out_refs[a].at[j], send_sem=send_sems.at[a, j],
                    recv_sem=recv_sems.at[a, j], device_id=(x, y, 1 - c), device_id_type=pl.DeviceIdType.MESH)
                cp.start()
                copies.append(cp)
        for cp in copies:
            cp.wait()

    return pl.pallas_call(
        body, name=name, out_shape=[jax.ShapeDtypeStruct((4,) + tuple(p.shape[1:]), p.dtype) for p in parts_list],
        in_specs=[_HBM] * n, out_specs=[_HBM] * n,
        scratch_shapes=[pltpu.SemaphoreType.DMA((n, 4)), pltpu.SemaphoreType.DMA((n, 4))],
    )(*parts_list)


def _chip_sum(name, parts, from_sibling):
    _, rows, cols = parts.shape
    lanes = -(-cols // LANES) * LANES
    tr = _row_tile(rows, 4 * lanes * 4 * 2, budget=24 << 20)

    def kern(p_ref, s_ref, o_ref):
        c = lax.axis_index("c")
        o_ref[0] = (jnp.where(c == 0, p_ref[0, 0], p_ref[0, 1]) + s_ref[0]).astype(o_ref.dtype)

    return pl.pallas_call(
        kern, grid=(4, rows // tr), name=name,
        in_specs=[pl.BlockSpec((1, 2, tr, cols), lambda j, i: (j, 0, i, 0)), pl.BlockSpec((1, tr, cols), lambda j, i: (j, i, 0))],
        out_specs=pl.BlockSpec((1, tr, cols), lambda j, i: (j, i, 0)),
        out_shape=jax.ShapeDtypeStruct((4, rows, cols), BF16),
        compiler_params=_params(4 * lanes * tr * 4 * 2),
    )(parts.reshape(4, 2, rows, cols), from_sibling)


def _scatter_across_chips(name, sums_list):
    n = len(sums_list)

    def body(*refs):
        q_refs, out_refs = refs[:n], refs[n:2 * n]
        send_sems, recv_sems, local_sems = refs[2 * n:]
        x, y, c = lax.axis_index("x"), lax.axis_index("y"), lax.axis_index("c")
        own = 2 * x + y
        copies = []
        for a in range(n):
            local = pltpu.make_async_copy(q_refs[a].at[own], out_refs[a].at[own], local_sems.at[a])
            local.start()
            copies.append(local)
            for d in range(1, 4):
                px, py = lax.rem(x + (d >> 1), 2), lax.rem(y + (d & 1), 2)
                cp = pltpu.make_async_remote_copy(
                    src_ref=q_refs[a].at[2 * px + py], dst_ref=out_refs[a].at[own], send_sem=send_sems.at[a, d - 1],
                    recv_sem=recv_sems.at[a, d - 1], device_id=(px, py, c), device_id_type=pl.DeviceIdType.MESH)
                cp.start()
                copies.append(cp)
        for cp in copies:
            cp.wait()

    return pl.pallas_call(
        body, name=name, out_shape=[jax.ShapeDtypeStruct(s.shape, s.dtype) for s in sums_list],
        in_specs=[_HBM] * n, out_specs=[_HBM] * n,
        scratch_shapes=[pltpu.SemaphoreType.DMA((n, 3)), pltpu.SemaphoreType.DMA((n, 3)), pltpu.SemaphoreType.DMA((n,))],
    )(*sums_list)


def _sum_parts(name, parts):
    n_parts, rows, cols = parts.shape
    tr = _row_tile(rows, 4 * cols * (n_parts + 1) * 2)

    def kern(p_ref, o_ref):
        acc = p_ref[0]
        for k in range(1, n_parts):
            acc = acc + p_ref[k]
        o_ref[...] = acc

    return pl.pallas_call(
        kern, grid=(rows // tr,), name=name, in_specs=[pl.BlockSpec((n_parts, tr, cols), lambda i: (0, i, 0))],
        out_specs=pl.BlockSpec((tr, cols), lambda i: (i, 0)), out_shape=jax.ShapeDtypeStruct((rows, cols), F32),
    )(parts)


def _adamw(name, w, m, v, parts):
    n_parts, rows, cols = parts.shape
    lanes = -(-cols // LANES) * LANES
    tr = _row_tile(rows, 4 * lanes * (n_parts + 7) * 2, budget=28 << 20)
    c1 = np.float32(1.0 - ADAM_B1 ** ADAM_STEP)
    c2 = np.float32(1.0 - ADAM_B2 ** ADAM_STEP)

    def kern(w_ref, m_ref, v_ref, p_ref, g_ref, d_ref, nm_ref, nv_ref):
        g = p_ref[0].astype(F32)
        for k in range(1, n_parts):
            g = g + p_ref[k].astype(F32)
        m_new = ADAM_B1 * m_ref[...] + (1.0 - ADAM_B1) * g
        v_new = ADAM_B2 * v_ref[...] + (1.0 - ADAM_B2) * jnp.square(g)
        g_ref[...] = g
        nm_ref[...] = m_new
        nv_ref[...] = v_new
        d_ref[...] = -ADAM_LR * ((m_new / c1) / (jnp.sqrt(v_new / c2) + ADAM_EPS) + ADAM_WD * w_ref[...])

    spec = pl.BlockSpec((tr, cols), lambda i: (i, 0))
    return pl.pallas_call(
        kern, grid=(rows // tr,), name=name,
        in_specs=[spec, spec, spec, pl.BlockSpec((n_parts, tr, cols), lambda i: (0, i, 0))], out_specs=[spec] * 4,
        out_shape=[jax.ShapeDtypeStruct((rows, cols), F32)] * 4,
        compiler_params=_params(4 * lanes * tr * (n_parts + 7) * 2),
    )(w, m, v, parts)


def _packed_rows(shape):
    return -(-int(np.prod(shape)) // (SUBLANES * LANES)) * SUBLANES


def _pack(arrays, row_multiple):
    parts = []
    for a in arrays:
        n = int(np.prod(a.shape))
        r = _packed_rows(a.shape)
        parts.append(jnp.pad(a.reshape(-1), (0, r * LANES - n)).reshape(r, LANES))
    rows = sum(p.shape[0] for p in parts)
    total = -(-rows // row_multiple) * row_multiple
    if total > rows:
        parts.append(jnp.zeros((total - rows, LANES), arrays[0].dtype))
    return jnp.concatenate(parts, axis=0)


def _unpack(packed, shapes):
    out, off = [], 0
    for s in shapes:
        r = _packed_rows(s)
        out.append(packed[off:off + r].reshape(-1)[:int(np.prod(s))].reshape(s))
        off += r
    return out


def _split_w_in(w_in):
    main = jnp.concatenate([w_in[:, :, s:e] for s, e in _IN_MAIN], axis=2)
    pad = jnp.zeros(w_in.shape[:2] + (LANES - 80,), w_in.dtype)
    small = jnp.concatenate([w_in[:, :, s:e] for s, e in _IN_SMALL] + [pad], axis=2)
    return main, small


def _join_w_in(main, small):
    return jnp.concatenate([main[:, :, 0:8192], small[:, :, 0:64], main[:, :, 8192:12288], small[:, :, 64:80],
                            main[:, :, 12288:15360]], axis=2)


def _unshard(gathered, axis):
    nd, nl, r, c = gathered.shape
    if axis == 1:
        return gathered.transpose(1, 0, 2, 3).reshape(nl, nd * r, c)
    return gathered.transpose(1, 2, 0, 3).reshape(nl, r, nd * c)


def _reshard(full, axis):
    nl, r, c = full.shape
    if axis == 1:
        return full.reshape(nl, N_DEV, r // N_DEV, c).transpose(1, 0, 2, 3)
    return full.reshape(nl, r, N_DEV, c // N_DEV).transpose(2, 0, 1, 3)


def kernel(x, c, ctx, c_ctx, w_ada, b_ada, w_in, lru_conv_w, lru_conv_b, lru_w_r, lru_b_r, lru_w_i, lru_b_i, lru_lambda, ssd_conv_w, ssd_conv_b, ssd_dt_bias, ssd_a_log, ssd_d, ssd_norm_w, ml_conv_w, ml_conv_b, ml_gate_b, ml_norm_w, w_br_a, w_br_b, w_br_c, w_out, b_out, ln1_g, ln1_b, w_ff1, b_ff1, w_ff2, b_ff2, ln2_g, ln2_b, loss_target, m_c_ctx, m_w_ada, m_b_ada, m_w_in, m_lru_conv_w, m_lru_conv_b, m_lru_w_r, m_lru_b_r, m_lru_w_i, m_lru_b_i, m_lru_lambda, m_ssd_conv_w, m_ssd_conv_b, m_ssd_dt_bias, m_ssd_a_log, m_ssd_d, m_ssd_norm_w, m_ml_conv_w, m_ml_conv_b, m_ml_gate_b, m_ml_norm_w, m_w_br_a, m_w_br_b, m_w_br_c, m_w_out, m_b_out, m_ln1_g, m_ln1_b, m_w_ff1, m_b_ff1, m_w_ff2, m_b_ff2, m_ln2_g, m_ln2_b, v_c_ctx, v_w_ada, v_b_ada, v_w_in, v_lru_conv_w, v_lru_conv_b, v_lru_w_r, v_lru_b_r, v_lru_w_i, v_lru_b_i, v_lru_lambda, v_ssd_conv_w, v_ssd_conv_b, v_ssd_dt_bias, v_ssd_a_log, v_ssd_d, v_ssd_norm_w, v_ml_conv_w, v_ml_conv_b, v_ml_gate_b, v_ml_norm_w, v_w_br_a, v_w_br_b, v_w_br_c, v_w_out, v_b_out, v_ln1_g, v_ln1_b, v_w_ff1, v_b_ff1, v_w_ff2, v_b_ff2, v_ln2_g, v_ln2_b):
    a = dict(locals())
    me = 4 * lax.axis_index("x") + 2 * lax.axis_index("y") + lax.axis_index("c")

    wts = {n: a[n] for n in _REPLICATED if n not in ('c_ctx', 'b_ada')}
    exchanged = [n for n in _BIG if n != 'w_ada']
    gathered = _gather_two_level("gather_weights", [a[n].astype(BF16) for n in exchanged])
    for n, g in zip(exchanged, gathered):
        full = _unshard(g, _BIG[n])
        if n == 'w_in':
            main, small = _split_w_in(full)
            wts['w_in_main'], wts['w_in_small'] = main.astype(F32), small.astype(F32)
        else:
            wts[n] = full.astype(F32)
    small_shapes = [a[n].shape for n in _SMALL_SHARDED]
    small_all = _exchange("gather_small", _pack([a[n] for n in _SMALL_SHARDED], SUBLANES), True)
    per_dev = [_unpack(small_all[k], small_shapes) for k in range(N_DEV)]
    for i, n in enumerate(_SMALL_SHARDED):
        wts[n] = jnp.concatenate([per_dev[k][i] for k in range(N_DEV)], axis=-1)

    mods, ada_saved = _ada_forward(c, c_ctx, w_ada, b_ada, me)
    y, vjp = jax.vjp(functools.partial(_forward, ctx=ctx[0]), x[0], wts, mods)
    loss_local, dy = _loss_and_cotangent(y, loss_target[0])
    grad_x, grads, dmods = vjp(dy)
    grads['w_in'] = _join_w_in(grads.pop('w_in_main'), grads.pop('w_in_small'))
    grad_w_ada, grads['b_ada'], grads['c_ctx'] = _ada_backward(ada_saved, dmods)
    loss = lax.psum(loss_local, ("x", "y", "c"))

    out = {}

    def put(n, res, shape):
        for kind, r in zip(("grad_", "delta_", "new_m_", "new_v_"), res):
            out[kind + n] = r.reshape(shape)

    flat = {n: (a[n].shape[0] * a[n].shape[1], a[n].shape[2]) for n in _BIG}
    by_dest = [_reshard(grads[n], _BIG[n]).reshape(N_DEV, *flat[n]) for n in exchanged]
    from_sibling = _scatter_to_sibling("scatter_d2d", by_dest)
    chip_sums = [_chip_sum("chipsum_" + n, p, s) for n, p, s in zip(exchanged, by_dest, from_sibling)]
    summed = dict(zip(exchanged, _scatter_across_chips("scatter_ici", chip_sums)))
    summed['w_ada'] = grad_w_ada.reshape(1, *flat['w_ada'])
    for n in _BIG:
        shp = a[n].shape
        rows, cols = flat[n]
        parts = summed[n]
        put(n, _adamw("adamw_" + n, a[n].reshape(rows, cols), a["m_" + n].reshape(rows, cols), a["v_" + n].reshape(rows, cols), parts), shp)

    rep_names = _REPLICATED + _SMALL_SHARDED
    chunk_rows = SUBLANES * N_DEV
    g_pack = _pack([grads[n] for n in rep_names], chunk_rows * N_DEV)
    rows = g_pack.shape[0]
    parts = _exchange("scatter_rep", g_pack.reshape(N_DEV, rows // N_DEV, LANES), False)
    mine = _sum_parts("sum_rep", parts)
    g_all = _exchange("gather_rep", mine, True).reshape(rows, LANES)
    g_full = _unpack(g_all, [grads[n].shape for n in rep_names])
    g_local = []
    for n, g in zip(rep_names, g_full):
        if n in _SMALL_SHARDED:
            width = a[n].shape[-1]
            g = lax.dynamic_slice_in_dim(g, me * width, width, axis=g.ndim - 1)
        g_local.append(g)
    shapes = [a[n].shape for n in rep_names]
    res = _adamw("adamw_rep", _pack([a[n] for n in rep_names], chunk_rows), _pack([a["m_" + n] for n in rep_names], chunk_rows),
                 _pack([a["v_" + n] for n in rep_names], chunk_rows), _pack(g_local, chunk_rows)[None])
    unpacked = [_unpack(r, shapes) for r in res]
    for i, n in enumerate(rep_names):
        put(n, [u[i] for u in unpacked], shapes[i])

    outs = [loss, grad_x[None]]
    for kind in ("grad_", "delta_", "new_m_", "new_v_"):
        outs += [out[kind + n] for n in _WEIGHTS]
    return tuple(outs)
```

```python
import functools

import numpy as np
import jax
import jax.numpy as jnp
from jax import lax
from jax.experimental import pallas as pl
from jax.experimental.pallas import tpu as pltpu

F32 = jnp.float32
BF16 = jnp.bfloat16

N_DEV = 8
D_MODEL = 1024
DEPTH = 2
GRID_W = 64
CHUNK = 128
LN_EPS = 1e-6
LRU_BLOCKS = 8
LRU_BS = 128
LRU_C = 8.0
SSD_INNER = 2048
SSD_GROUPS = 8
SSD_HPG = 4
SSD_HEADDIM = 64
SSD_STATE = 128
ML_HEADS = 4
ML_HD = 256
D_FF = 4096
DN_ALPHA = (2 * DEPTH) ** 0.25
ADAM_LR, ADAM_B1, ADAM_B2, ADAM_EPS, ADAM_WD, ADAM_STEP = 0.001, 0.9, 0.999, 1e-08, 0.01, 10

VMEM_CAP = 60 * 1024 * 1024
SUBLANES = 8
LANES = 128

_IN_MAIN = ((0, 8192), (8256, 12352), (12368, 15440))
_IN_MAIN_WIDTHS = (1024, 1024, 2048, 2048, 1024, 1024, 1024, 1024, 1024, 1024, 1024, 1024, 1024)
_IN_SMALL = ((8192, 8256), (12352, 12368))
_DT_LANE = 0
_MG_LANE = 64

_WEIGHTS = ['c_ctx', 'w_ada', 'b_ada', 'w_in', 'lru_conv_w', 'lru_conv_b', 'lru_w_r', 'lru_b_r', 'lru_w_i', 'lru_b_i',
            'lru_lambda', 'ssd_conv_w', 'ssd_conv_b', 'ssd_dt_bias', 'ssd_a_log', 'ssd_d', 'ssd_norm_w', 'ml_conv_w',
            'ml_conv_b', 'ml_gate_b', 'ml_norm_w', 'w_br_a', 'w_br_b', 'w_br_c', 'w_out', 'b_out', 'ln1_g', 'ln1_b',
            'w_ff1', 'b_ff1', 'w_ff2', 'b_ff2', 'ln2_g', 'ln2_b']
_BIG = {'w_ada': 2, 'w_in': 2, 'w_ff1': 2, 'w_br_a': 1, 'w_br_b': 1, 'w_br_c': 1, 'w_out': 1, 'w_ff2': 1}
_SMALL_SHARDED = ['lru_conv_w', 'lru_b_r', 'lru_b_i', 'lru_lambda', 'ssd_conv_w', 'ml_conv_w']
_REPLICATED = [n for n in _WEIGHTS if n not in _BIG and n not in _SMALL_SHARDED]


def _params(vmem_bytes):
    return pltpu.CompilerParams(vmem_limit_bytes=int(min(max(2 * vmem_bytes, 32 << 20), VMEM_CAP)))


def _row_tile(n_rows, bytes_per_row, budget=6 << 20, cap=512):
    t = cap
    while t > SUBLANES and (t * bytes_per_row > budget or n_rows % t):
        t //= 2
    assert n_rows % t == 0, (n_rows, t)
    return t


def _dg(a, b, ca, cb):
    return lax.dot_general(a.astype(BF16), b.astype(BF16), (((ca,), (cb,)), ((), ())), preferred_element_type=F32)


def _make_bdot(ca, cb):
    @jax.custom_vjp
    def f(a, b):
        return _dg(a, b, ca, cb)

    def fwd(a, b):
        return _dg(a, b, ca, cb), (a, b)

    def bwd(res, g):
        a, b = res
        da = _dg(g, b, 1, 1 - cb) if ca == 1 else _dg(b, g, 1 - cb, 1)
        db = _dg(a, g, 1 - ca, 0) if cb == 0 else _dg(g, a, 0, 1 - ca)
        return da, db

    f.defvjp(fwd, bwd)
    return f


_mm_nn = _make_bdot(1, 0)
_mm_nt = _make_bdot(1, 1)
_mm_tn = _make_bdot(0, 0)


@jax.custom_vjp
def _round_bf16(x):
    return x.astype(BF16).astype(F32)


_round_bf16.defvjp(lambda x: (_round_bf16(x), None), lambda _, g: (g,))


def _exact_dot(a, b):
    return jnp.dot(a, b, precision=lax.Precision.HIGHEST, preferred_element_type=F32)


def _layernorm_rows(x):
    mu = jnp.mean(x, -1, keepdims=True)
    var = jnp.mean(jnp.square(x - mu), -1, keepdims=True)
    return (x - mu) * lax.rsqrt(var + LN_EPS)


def _rowwise_calls(name, f, rows, params, out_widths, out_dtype=F32, to_linear=(), tile_cap=512):
    drow_dtypes = [BF16 if i in to_linear else F32 for i in range(len(rows))]
    nr, npar, no = len(rows), len(params), len(out_widths)
    n_rows = rows[0].shape[0]
    row_w = [r.shape[1] for r in rows]
    par_bytes = sum(int(np.prod(p.shape)) * 4 for p in params)
    tile = _row_tile(n_rows, 4 * (2 * sum(row_w) + 2 * sum(out_widths)), budget=16 << 20, cap=tile_cap)
    grid = (n_rows // tile,)

    def row_spec(w):
        return pl.BlockSpec((tile, w), lambda i: (i, 0))

    def par_spec(p):
        return pl.BlockSpec(p.shape, lambda i: (0, 0))

    vmem = 2 * tile * 4 * (2 * sum(row_w) + 3 * sum(out_widths)) + 4 * par_bytes

    def fwd_call(rows, params):
        def kern(*refs):
            outs = f(*[r[...] for r in refs[:nr + npar]])
            for r, o in zip(refs[nr + npar:], outs):
                r[...] = o.astype(out_dtype)

        return pl.pallas_call(
            kern, grid=grid, name=name + "_fwd",
            in_specs=[row_spec(w) for w in row_w] + [par_spec(p) for p in params],
            out_specs=[row_spec(w) for w in out_widths],
            out_shape=[jax.ShapeDtypeStruct((n_rows, w), out_dtype) for w in out_widths],
            compiler_params=_params(vmem),
        )(*rows, *params)

    def bwd_call(rows, params, gouts):
        def kern(*refs):
            ins = [r[...] for r in refs[:nr + npar]]
            gs = tuple(r[...] for r in refs[nr + npar:nr + npar + no])
            grads = jax.vjp(f, *ins)[1](gs)
            drefs = refs[nr + npar + no:]
            for k in range(nr):
                drefs[k][...] = grads[k].astype(drefs[k].dtype)

            @pl.when(pl.program_id(0) == 0)
            def _():
                for k in range(npar):
                    drefs[nr + k][...] = jnp.zeros_like(drefs[nr + k])

            for k in range(npar):
                drefs[nr + k][...] += grads[nr + k]

        res = pl.pallas_call(
            kern, grid=grid, name=name + "_bwd",
            in_specs=[row_spec(w) for w in row_w] + [par_spec(p) for p in params] + [row_spec(w) for w in out_widths],
            out_specs=[row_spec(w) for w in row_w] + [par_spec(p) for p in params],
            out_shape=[jax.ShapeDtypeStruct(r.shape, dt) for r, dt in zip(rows, drow_dtypes)]
            + [jax.ShapeDtypeStruct(p.shape, F32) for p in params],
            compiler_params=_params(vmem),
        )(*rows, *params, *gouts)
        return tuple(r.astype(F32) for r in res[:nr]), tuple(res[nr:])

    return fwd_call, bwd_call


def _rowwise(name, f, rows, params, out_widths, to_linear=(), tile_cap=512):
    rows, params = tuple(rows), tuple(params)
    fwd_call, bwd_call = _rowwise_calls(name, f, rows, params, out_widths, to_linear=to_linear, tile_cap=tile_cap)

    @jax.custom_vjp
    def op(rows, params):
        return tuple(fwd_call(rows, params))

    op.defvjp(lambda r, p: (tuple(fwd_call(r, p)), (r, p)), lambda res, g: bwd_call(res[0], res[1], g))
    return op(rows, params)


def _rowwise_linear(name, f, rows, params, weights, to_linear=()):
    rows, params = tuple(rows), tuple(params)
    ws = tuple(w for w, _ in weights)
    m, k = rows[0].shape[0], ws[0].shape[0]
    row_fwd, row_bwd = _rowwise_calls(name, f, rows, params, [k], BF16, to_linear)
    lin = [_linear_calls(name + "lin%d" % i, m, k, w.shape[1], wd, BF16) for i, (w, wd) in enumerate(weights)]
    counts = [len(c[3]) for c in lin]

    def fwd(rows, params, ws):
        (a,) = row_fwd(rows, params)
        outs = []
        for (fwd_call, _, _, _), w in zip(lin, ws):
            outs += list(fwd_call(a, w))
        return tuple(outs), a

    @jax.custom_vjp
    def op(rows, params, ws):
        return fwd(rows, params, ws)[0]

    def op_fwd(rows, params, ws):
        outs, a = fwd(rows, params, ws)
        return outs, (rows, params, ws, a)

    def op_bwd(res, g):
        rows, params, ws, a = res
        da, dws, off = None, [], 0
        for (_, dgrad_call, wgrad_call, _), w, cnt in zip(lin, ws, counts):
            gk = g[off:off + cnt]
            off += cnt
            d = dgrad_call(w, gk)
            da = d if da is None else da + d
            dws.append(wgrad_call(a, gk))
        drows, dparams = row_bwd(rows, params, (da,))
        return drows, dparams, tuple(dws)

    op.defvjp(op_fwd, op_bwd)
    return op(rows, params, ws)


def _group_ranges(widths, tn):
    starts, s = [], 0
    for w in widths:
        assert w % tn == 0, (w, tn)
        starts.append((s // tn, (s + w) // tn))
        s += w
    return starts, s // tn


def _group_tile(refs, ranges, row_tile, col_tile, i, j):
    out = []
    for ref, (s, e) in zip(refs, ranges):
        cols = pl.ds(pl.multiple_of((j - s) * col_tile, col_tile), col_tile)
        out.append(((j >= s) & (j < e), ref, cols))
    return [(p, lambda r=r, c=c: r.at[pl.ds(pl.multiple_of(i * row_tile, row_tile), row_tile), c]) for p, r, c in out]


def _linear_calls(name, m, k, n, widths, a_dtype):
    widths = (n,) if widths is None else tuple(widths)
    ng = len(widths)
    cast_a = a_dtype != BF16
    wide = all(wd % 1024 == 0 for wd in widths)
    tn = 128 if n < 256 else (1024 if wide else (256 if k > 2048 or n % 512 else 512))
    tm = _row_tile(m, 0, cap=1024 if k <= 2048 else 512)
    ranges, nt = _group_ranges(widths, tn)
    mt = m // tm
    tn_w = 1024 if (wide and k <= 1024) else (512 if all(wd % 512 == 0 for wd in widths) else min(tn, 256))
    tm_w = _row_tile(m, 0, cap=1024 if k <= 1024 else 512)
    ranges_w, nt_w = _group_ranges(widths, tn_w)
    mt_w = m // tm_w
    hbm = pl.BlockSpec(memory_space=pl.ANY)

    def fwd_call(a, w):
        n_steps = mt * nt

        def kern(a_ref, w_ref, *rest):
            outs, obuf, osem = rest[:ng], rest[ng], rest[ng + 1]
            a_bf = rest[ng + 2] if cast_a else a_ref
            i, j = pl.program_id(0), pl.program_id(1)
            step = i * nt + j
            slot = lax.rem(step, 2)

            def drain(sl):
                pltpu.make_async_copy(obuf.at[sl], outs[0].at[pl.ds(0, tm), pl.ds(0, tn)], osem.at[sl]).wait()

            if cast_a:
                @pl.when(j == 0)
                def _():
                    a_bf[...] = a_ref[...].astype(BF16)

            @pl.when(step >= 2)
            def _():
                drain(slot)

            obuf[slot] = jnp.dot(a_bf[...], w_ref[...], preferred_element_type=F32)
            for pred, window in _group_tile(outs, ranges, tm, tn, i, j):
                @pl.when(pred)
                def _(window=window):
                    pltpu.make_async_copy(obuf.at[slot], window(), osem.at[slot]).start()

            @pl.when(step == n_steps - 1)
            def _():
                drain(slot)
                if n_steps > 1:
                    drain(1 - slot)

        return pl.pallas_call(
            kern, grid=(mt, nt), name=name + "_fwd",
            in_specs=[pl.BlockSpec((tm, k), lambda i, j: (i, 0)), pl.BlockSpec((k, tn), lambda i, j: (0, j))],
            out_specs=[hbm] * ng,
            out_shape=[jax.ShapeDtypeStruct((m, wd), F32) for wd in widths],
            scratch_shapes=[pltpu.VMEM((2, tm, tn), F32), pltpu.SemaphoreType.DMA((2,))]
            + ([pltpu.VMEM((tm, k), BF16)] if cast_a else []),
            compiler_params=_params(10 * tm * k + 4 * k * tn + 8 * tm * tn),
        )(a, w.astype(BF16))

    def prefetched(gs, gbuf, gsem, rngs, row_tile, col_tile, step, n_steps, tile_of):
        slot = lax.rem(step, 2)

        def start(s_idx, sl):
            ii, jj = tile_of(s_idx)
            for pred, window in _group_tile(gs, rngs, row_tile, col_tile, ii, jj):
                @pl.when(pred)
                def _(window=window):
                    pltpu.make_async_copy(window(), gbuf.at[sl], gsem.at[sl]).start()

        @pl.when(step == 0)
        def _():
            start(step, slot)

        @pl.when(step + 1 < n_steps)
        def _():
            start(step + 1, 1 - slot)

        pltpu.make_async_copy(gs[0].at[pl.ds(0, row_tile), pl.ds(0, col_tile)], gbuf.at[slot], gsem.at[slot]).wait()
        return slot

    def dgrad_call(w, gouts):
        def kern(w_ref, *rest):
            gs, da, gbuf, gsem = rest[:ng], rest[ng], rest[ng + 1], rest[ng + 2]
            i, j = pl.program_id(0), pl.program_id(1)
            slot = prefetched(gs, gbuf, gsem, ranges, tm, tn, i * nt + j, mt * nt, lambda s: (s // nt, lax.rem(s, nt)))

            @pl.when(j == 0)
            def _():
                da[...] = jnp.zeros_like(da)

            da[...] += lax.dot_general(gbuf[slot].astype(BF16), w_ref[...], (((1,), (1,)), ((), ())), preferred_element_type=F32)

        return pl.pallas_call(
            kern, grid=(mt, nt), name=name + "_dgrad",
            in_specs=[pl.BlockSpec((k, tn), lambda i, j: (0, j))] + [hbm] * ng,
            out_specs=pl.BlockSpec((tm, k), lambda i, j: (i, 0)),
            out_shape=jax.ShapeDtypeStruct((m, k), F32),
            scratch_shapes=[pltpu.VMEM((2, tm, tn), BF16), pltpu.SemaphoreType.DMA((2,))],
            compiler_params=_params(12 * tm * k + 4 * k * tn + 10 * tm * tn),
        )(w.astype(BF16), *[g.astype(BF16) for g in gouts])

    def wgrad_call(a, gouts):
        def kern(a_ref, *rest):
            gs, dw, gbuf, gsem = rest[:ng], rest[ng], rest[ng + 1], rest[ng + 2]
            j, i = pl.program_id(0), pl.program_id(1)
            slot = prefetched(gs, gbuf, gsem, ranges_w, tm_w, tn_w, j * mt_w + i, mt_w * nt_w,
                              lambda s: (lax.rem(s, mt_w), s // mt_w))

            @pl.when(i == 0)
            def _():
                dw[...] = jnp.zeros_like(dw)

            dw[...] += lax.dot_general(a_ref[...].astype(BF16), gbuf[slot].astype(BF16), (((0,), (0,)), ((), ())),
                                       preferred_element_type=F32)

        return pl.pallas_call(
            kern, grid=(nt_w, mt_w), name=name + "_wgrad",
            in_specs=[pl.BlockSpec((tm_w, k), lambda j, i: (i, 0))] + [hbm] * ng,
            out_specs=pl.BlockSpec((k, tn_w), lambda j, i: (0, j)),
            out_shape=jax.ShapeDtypeStruct((k, n), F32),
            scratch_shapes=[pltpu.VMEM((2, tm_w, tn_w), BF16), pltpu.SemaphoreType.DMA((2,))],
            compiler_params=_params(12 * tm_w * k + 12 * k * tn_w + 10 * tm_w * tn_w),
        )(a, *[g.astype(BF16) for g in gouts])

    return fwd_call, dgrad_call, wgrad_call, widths


def _linear(name, a, w, widths=None):
    fwd_call, dgrad_call, wgrad_call, _ = _linear_calls(name, a.shape[0], a.shape[1], w.shape[1], widths, a.dtype)

    @jax.custom_vjp
    def op(a, w):
        return tuple(fwd_call(a, w))

    op.defvjp(lambda a, w: (tuple(fwd_call(a, w)), (a, w)),
              lambda res, g: (dgrad_call(res[1], g), wgrad_call(res[0], g)))
    out = op(a, w)
    return out[0] if widths is None else out


def _conv_taps(x_ext, w, n_ext):
    xm2 = pltpu.roll(x_ext, 2, 0)
    xm1 = pltpu.roll(x_ext, 1, 0)
    xp1 = pltpu.roll(x_ext, n_ext - 1, 0)
    return xm2, xm1, xp1


def _dwconv(name, x, w, b, act):
    n_rows, ch = x.shape
    tt = _row_tile(n_rows, 4 * 6 * ch, budget=12 << 20, cap=256)
    nt = n_rows // tt
    n_ext = tt + 2 * SUBLANES
    per8 = tt // SUBLANES
    last8 = n_rows // SUBLANES - 1
    main = pl.BlockSpec((tt, ch), lambda i: (i, 0))
    prev = pl.BlockSpec((SUBLANES, ch), lambda i: (jnp.maximum(i * per8 - 1, 0), 0))
    nxt = pl.BlockSpec((SUBLANES, ch), lambda i: (jnp.minimum((i + 1) * per8, last8), 0))
    wspec = pl.BlockSpec((4, ch), lambda i: (0, 0))
    bspec = pl.BlockSpec((1, ch), lambda i: (0, 0))
    vmem = 4 * n_ext * ch * 14

    def ext(main_ref, prev_ref, next_ref):
        i = pl.program_id(0)
        p = jnp.where(i > 0, prev_ref[...], 0.0)
        q = jnp.where(i < nt - 1, next_ref[...], 0.0)
        return jnp.concatenate([p, main_ref[...], q], axis=0)

    def pre_of(x_ext, wv, bv):
        xm2, xm1, xp1 = _conv_taps(x_ext, wv, n_ext)
        pre = wv[0:1] * xm2 + wv[1:2] * xm1 + wv[2:3] * x_ext + wv[3:4] * xp1 + bv
        return pre, (xm2, xm1, xp1)

    def fwd_call(x, w, b):
        def kern(xm, xp, xn, w_ref, b_ref, o_ref):
            pre, _ = pre_of(ext(xm, xp, xn), w_ref[...], b_ref[...])
            pre = pre[SUBLANES:SUBLANES + tt]
            o_ref[...] = pre * jax.nn.sigmoid(pre) if act else pre

        return pl.pallas_call(
            kern, grid=(nt,), name=name + "_fwd", in_specs=[main, prev, nxt, wspec, bspec], out_specs=main,
            out_shape=jax.ShapeDtypeStruct((n_rows, ch), F32), compiler_params=_params(vmem),
        )(x, x, x, w, b)

    def bwd_call(x, w, b, dy):
        def kern(xm, xp, xn, gm, gp, gn, w_ref, b_ref, dx_ref, dw_ref, db_ref):
            wv = w_ref[...]
            x_ext = ext(xm, xp, xn)
            pre, (xm2, xm1, xp1) = pre_of(x_ext, wv, b_ref[...])
            dpre = ext(gm, gp, gn)
            if act:
                sg = jax.nn.sigmoid(pre)
                dpre = dpre * (sg + pre * sg * (1.0 - sg))
            dx = (wv[0:1] * pltpu.roll(dpre, n_ext - 2, 0) + wv[1:2] * pltpu.roll(dpre, n_ext - 1, 0)
                  + wv[2:3] * dpre + wv[3:4] * pltpu.roll(dpre, 1, 0))
            sl = slice(SUBLANES, SUBLANES + tt)
            dx_ref[...] = dx[sl].astype(dx_ref.dtype)
            dm = dpre[sl]

            @pl.when(pl.program_id(0) == 0)
            def _():
                dw_ref[...] = jnp.zeros_like(dw_ref)
                db_ref[...] = jnp.zeros_like(db_ref)

            dw_ref[...] += jnp.concatenate(
                [jnp.sum(dm * t[sl], axis=0, keepdims=True) for t in (xm2, xm1, x_ext, xp1)], axis=0)
            db_ref[...] += jnp.sum(dm, axis=0, keepdims=True)

        return pl.pallas_call(
            kern, grid=(nt,), name=name + "_bwd", in_specs=[main, prev, nxt, main, prev, nxt, wspec, bspec],
            out_specs=[main, wspec, bspec],
            out_shape=[jax.ShapeDtypeStruct((n_rows, ch), BF16), jax.ShapeDtypeStruct((4, ch), F32),
                       jax.ShapeDtypeStruct((1, ch), F32)],
            compiler_params=_params(vmem),
        )(x, x, x, dy, dy, dy, w, b)

    @jax.custom_vjp
    def op(x, w, b):
        return fwd_call(x, w, b)

    def op_bwd(res, g):
        dx, dw, db = bwd_call(*res, g)
        return dx.astype(F32), dw, db

    op.defvjp(lambda x, w, b: (fwd_call(x, w, b), (x, w, b)), op_bwd)
    return op(x, w, b)


def _scan_groups(tt, ch, reverse, load, store, carry_ref):
    row = lax.broadcasted_iota(jnp.int32, (SUBLANES, ch), 0)
    ng = tt // SUBLANES

    def body(k, carry):
        g = (ng - 1 - k) if reverse else k
        sl = pl.ds(pl.multiple_of(g * SUBLANES, SUBLANES), SUBLANES)
        a, b, extra = load(sl)
        for s in (1, 2, 4):
            if reverse:
                a_sh, b_sh, valid = pltpu.roll(a, SUBLANES - s, 0), pltpu.roll(b, SUBLANES - s, 0), row < SUBLANES - s
            else:
                a_sh, b_sh, valid = pltpu.roll(a, s, 0), pltpu.roll(b, s, 0), row >= s
            b = jnp.where(valid, b + a * b_sh, b)
            a = jnp.where(valid, a * a_sh, a)
        h = b + a * carry
        if reverse:
            h_prev = jnp.where(row == SUBLANES - 1, carry, pltpu.roll(h, SUBLANES - 1, 0))
            last = h[0:1]
        else:
            h_prev = jnp.where(row == 0, carry, pltpu.roll(h, 1, 0))
            last = h[SUBLANES - 1:SUBLANES]
        store(sl, h, h_prev, extra)
        return jnp.broadcast_to(last, (SUBLANES, ch))

    carry_ref[...] = lax.fori_loop(0, ng, body, carry_ref[...])


def _lin_scan(name, a, b, h0, reverse):
    n_rows, ch = a.shape
    tt = _row_tile(n_rows, 0, cap=256)
    nt = n_rows // tt
    vmem = 2 * 4 * tt * ch * 5

    def tile_spec(rev):
        return pl.BlockSpec((tt, ch), (lambda i: (nt - 1 - i, 0)) if rev else (lambda i: (i, 0)))

    vec = pl.BlockSpec((1, ch), lambda i: (0, 0))

    def fwd_call(a, b, h0):
        def kern(a_ref, b_ref, h0_ref, h_ref, hp_ref, last_ref, carry):
            @pl.when(pl.program_id(0) == 0)
            def _():
                carry[...] = jnp.broadcast_to(h0_ref[...], carry.shape)

            def load(sl):
                return a_ref[sl, :], b_ref[sl, :], None

            def store(sl, h, h_prev, _):
                h_ref[sl, :] = h
                hp_ref[sl, :] = h_prev

            _scan_groups(tt, ch, reverse, load, store, carry)
            last_ref[...] = carry[0:1]

        return pl.pallas_call(
            kern, grid=(nt,), name=name + "_fwd", in_specs=[tile_spec(reverse), tile_spec(reverse), vec],
            out_specs=[tile_spec(reverse), tile_spec(reverse), vec],
            out_shape=[jax.ShapeDtypeStruct((n_rows, ch), F32)] * 2 + [jax.ShapeDtypeStruct((1, ch), F32)],
            scratch_shapes=[pltpu.VMEM((SUBLANES, ch), F32)], compiler_params=_params(vmem),
        )(a, b, h0)

    def bwd_call(a, h_prev, dh, dlast):
        rev = not reverse

        def kern(a_ref, hp_ref, dh_ref, dl_ref, da_ref, db_ref, d0_ref, carry):
            @pl.when(pl.program_id(0) == 0)
            def _():
                carry[...] = jnp.broadcast_to(dl_ref[...], carry.shape)

            def load(sl):
                av, dv = a_ref[sl, :], dh_ref[sl, :]
                return av, av * dv, dv

            def store(sl, u, u_next, dv):
                g = dv + u_next
                db_ref[sl, :] = g
                da_ref[sl, :] = g * hp_ref[sl, :]

            _scan_groups(tt, ch, rev, load, store, carry)
            d0_ref[...] = carry[0:1]

        return pl.pallas_call(
            kern, grid=(nt,), name=name + "_bwd", in_specs=[tile_spec(rev)] * 3 + [vec],
            out_specs=[tile_spec(rev), tile_spec(rev), vec],
            out_shape=[jax.ShapeDtypeStruct((n_rows, ch), F32)] * 2 + [jax.ShapeDtypeStruct((1, ch), F32)],
            scratch_shapes=[pltpu.VMEM((SUBLANES, ch), F32)], compiler_params=_params(vmem),
        )(a, h_prev, dh, dlast)

    @jax.custom_vjp
    def op(a, b, h0):
        h, _, last = fwd_call(a, b, h0)
        return h, last

    def op_fwd(a, b, h0):
        h, h_prev, last = fwd_call(a, b, h0)
        return (h, last), (a, h_prev)

    def op_bwd(res, g):
        da, db, d0 = bwd_call(res[0], res[1], g[0], g[1])
        return da, db, d0

    op.defvjp(op_fwd, op_bwd)
    return op(a, b, h0)


def _tri(reverse):
    q = lax.broadcasted_iota(jnp.int32, (CHUNK, CHUNK), 0)
    s = lax.broadcasted_iota(jnp.int32, (CHUNK, CHUNK), 1)
    return (q <= s) if reverse else (q >= s)


def _pick_col(x, lane):
    idx = lax.broadcasted_iota(jnp.int32, x.shape, 1)
    return jnp.sum(jnp.where(idx == lane, x, 0.0), axis=1, keepdims=True)


def _pick_row(x, row):
    idx = lax.broadcasted_iota(jnp.int32, x.shape, 0)
    return jnp.sum(jnp.where(idx == row, x, 0.0), axis=0, keepdims=True)


def _ssd_shared(small, bias_row, alog_row, reverse):
    delta_all = jax.nn.softplus(small + bias_row)
    acs_all = _exact_dot(_tri(reverse).astype(F32), delta_all * (-jnp.exp(alog_row)))
    return delta_all, acs_all, acs_all.T


def _ssd_group(xs, bm, cm, state, delta_all, acs_all, acs_t, g, direction, reverse):
    mask = _tri(reverse)
    last = 0 if reverse else CHUNK - 1
    hd = SSD_HEADDIM
    rowi = lax.broadcasted_iota(jnp.int32, (CHUNK, 1), 0)
    a_cols, a_rows, deltas, tots = [], [], [], []
    for r in range(SSD_HPG):
        lane = _DT_LANE + 32 * direction + SSD_HPG * g + r
        a_col = _pick_col(acs_all, lane)
        a_cols.append(a_col)
        a_rows.append(_pick_row(acs_t, lane))
        deltas.append(_pick_col(delta_all, lane))
        tots.append(jnp.sum(jnp.where(rowi == last, a_col, 0.0), axis=0, keepdims=True))

    def wide(cols, rows):
        return jnp.concatenate([jnp.broadcast_to(c, (rows, hd)) for c in cols], axis=1)

    a_w = wide(a_cols, CHUNK)
    x_w = xs * wide(deltas, CHUNK)
    st = _mm_tn(x_w * jnp.exp(wide(tots, 1) - a_w), bm)
    y_off = _mm_nt(cm, state) * jnp.exp(a_w)
    grow = jnp.concatenate([jnp.broadcast_to(jnp.exp(t), (hd, 1)) for t in tots], axis=0)
    cb = _mm_nt(cm, bm)
    m_cat = jnp.concatenate([cb * jnp.exp(jnp.where(mask, a_cols[r] - a_rows[r], -jnp.inf)) for r in range(SSD_HPG)], axis=1)
    lane_head = lax.broadcasted_iota(jnp.int32, (1, SSD_HPG * hd), 1) // hd
    x_bd = jnp.concatenate([jnp.where(lane_head == r, x_w, 0.0) for r in range(SSD_HPG)], axis=0)
    return _mm_nn(m_cat, x_bd) + y_off, grow * state + st


def _ssd_calls(name, xs, bm, cm, s0, direction, reverse):
    n_rows = xs.shape[0]
    nc = n_rows // CHUNK
    gw = SSD_HPG * SSD_HEADDIM
    vmem = 4 * CHUNK * (gw + 3 * 128) * 8 + 4 * gw * 128 * 12 + (8 << 20)

    n_state = SSD_GROUPS * gw
    shared_scratch = [pltpu.VMEM((CHUNK, LANES), F32), pltpu.VMEM((CHUNK, LANES), F32), pltpu.VMEM((LANES, CHUNK), F32)]

    def specs(order, gps=1):
        def cidx(c):
            return (nc - 1 - c) if order else c

        return dict(
            xs=pl.BlockSpec((CHUNK, gps * gw), lambda c, g: (cidx(c), g)),
            bc=pl.BlockSpec((CHUNK, gps * SSD_STATE), lambda c, g: (cidx(c), g)),
            small=pl.BlockSpec((CHUNK, LANES), lambda c, g: (cidx(c), 0)),
            row=pl.BlockSpec((1, LANES), lambda c, g: (0, 0)),
            state=pl.BlockSpec((n_state, SSD_STATE), lambda c, g: (0, 0)),
            enter=pl.BlockSpec((1, gps * gw, SSD_STATE), lambda c, g: (cidx(c), g, 0)),
        )

    gps_fwd, gps_bwd = SSD_GROUPS, 1

    def group_rows(g, gps):
        return pl.ds(pl.multiple_of(g * gps * gw, gps * gw), gps * gw)

    def step_fn(g, gps):
        def fn(xs_v, bm_v, cm_v, st_v, d_all, a_all, a_t):
            ys, sts = [], []
            for u in range(gps):
                y_u, s_u = _ssd_group(xs_v[:, u * gw:(u + 1) * gw], bm_v[:, u * SSD_STATE:(u + 1) * SSD_STATE],
                                      cm_v[:, u * SSD_STATE:(u + 1) * SSD_STATE], st_v[u * gw:(u + 1) * gw],
                                      d_all, a_all, a_t, gps * g + u, direction, reverse)
                ys.append(y_u)
                sts.append(s_u)
            return jnp.concatenate(ys, axis=1), jnp.concatenate(sts, axis=0)

        return fn

    def fwd_call(xs, bm, cm, small, bias_row, alog_row, s0):
        gps = gps_fwd
        sp = specs(reverse, gps)

        def kern(xs_r, bm_r, cm_r, sm_r, br_r, ar_r, s0_r, y_r, sf_r, se_r, st, sh_d, sh_a, sh_t):
            c, g = pl.program_id(0), pl.program_id(1)

            @pl.when((c == 0) & (g == 0))
            def _():
                st[...] = s0_r[...]

            @pl.when(g == 0)
            def _():
                sh_d[...], sh_a[...], sh_t[...] = _ssd_shared(sm_r[...], br_r[...], ar_r[...], reverse)

            rows = group_rows(g, gps)
            s_in = st[rows, :]
            se_r[0] = s_in
            y_r[...], s_new = step_fn(g, gps)(xs_r[...], bm_r[...], cm_r[...], s_in, sh_d[...], sh_a[...], sh_t[...])
            st[rows, :] = s_new
            sf_r[rows, :] = s_new

        return pl.pallas_call(
            kern, grid=(nc, SSD_GROUPS // gps), name=name + "_fwd",
            in_specs=[sp['xs'], sp['bc'], sp['bc'], sp['small'], sp['row'], sp['row'], sp['state']],
            out_specs=[sp['xs'], sp['state'], sp['enter']],
            out_shape=[jax.ShapeDtypeStruct((n_rows, SSD_INNER), F32), jax.ShapeDtypeStruct((n_state, SSD_STATE), F32),
                       jax.ShapeDtypeStruct((nc, n_state, SSD_STATE), F32)],
            scratch_shapes=[pltpu.VMEM((n_state, SSD_STATE), F32)] + shared_scratch, compiler_params=_params(vmem),
        )(xs, bm, cm, small, bias_row, alog_row, s0)

    def bwd_call(xs, bm, cm, small, bias_row, alog_row, enter, dy, dsf, acc=()):
        gps = gps_bwd
        sp = specs(not reverse, gps)

        def kern(*refs):
            xs_r, bm_r, cm_r, sm_r, br_r, ar_r, se_r, dy_r, dsf_r = refs[:9]
            acc_r = refs[9:9 + len(acc)]
            dxs_r, dbm_r, dcm_r, dsm_r, dbr_r, dar_r, ds0_r, ds, sh_d, sh_a, sh_t, gd, ga, gt = refs[9 + len(acc):]
            c, g = pl.program_id(0), pl.program_id(1)

            @pl.when((c == 0) & (g == 0))
            def _():
                ds[...] = dsf_r[...]
                dbr_r[...] = jnp.zeros_like(dbr_r)
                dar_r[...] = jnp.zeros_like(dar_r)

            @pl.when(g == 0)
            def _():
                sh_d[...], sh_a[...], sh_t[...] = _ssd_shared(sm_r[...], br_r[...], ar_r[...], reverse)
                gd[...] = jnp.zeros_like(gd)
                ga[...] = jnp.zeros_like(ga)
                gt[...] = jnp.zeros_like(gt)

            rows = group_rows(g, gps)
            _, vjp = jax.vjp(step_fn(g, gps), xs_r[...], bm_r[...], cm_r[...], se_r[0], sh_d[...], sh_a[...], sh_t[...])
            dxs, dbm, dcm, ds_in, dd, da, dt = vjp((dy_r[...], ds[rows, :]))
            dxs_r[...] = dxs + acc_r[0][...] if acc else dxs
            dbm_r[...] = dbm + acc_r[1][...] if acc else dbm
            dcm_r[...] = dcm + acc_r[2][...] if acc else dcm
            ds[rows, :] = ds_in
            ds0_r[rows, :] = ds_in
            gd[...] += dd
            ga[...] += da
            gt[...] += dt

            @pl.when(g == SSD_GROUPS // gps - 1)
            def _():
                shared = functools.partial(_ssd_shared, reverse=reverse)
                dsm, dbr, dar = jax.vjp(shared, sm_r[...], br_r[...], ar_r[...])[1]((gd[...], ga[...], gt[...]))
                dsm_r[...] = dsm + acc_r[3][...] if acc else dsm
                dbr_r[...] += dbr
                dar_r[...] += dar

        return pl.pallas_call(
            kern, grid=(nc, SSD_GROUPS // gps), name=name + "_bwd",
            in_specs=[sp['xs'], sp['bc'], sp['bc'], sp['small'], sp['row'], sp['row'], sp['enter'], sp['xs'], sp['state']]
            + ([sp['xs'], sp['bc'], sp['bc'], sp['small']] if acc else []),
            out_specs=[sp['xs'], sp['bc'], sp['bc'], sp['small'], sp['row'], sp['row'], sp['state']],
            out_shape=[jax.ShapeDtypeStruct(xs.shape, F32), jax.ShapeDtypeStruct(bm.shape, F32),
                       jax.ShapeDtypeStruct(cm.shape, F32), jax.ShapeDtypeStruct((n_rows, LANES), F32),
                       jax.ShapeDtypeStruct((1, LANES), F32), jax.ShapeDtypeStruct((1, LANES), F32),
                       jax.ShapeDtypeStruct(s0.shape, F32)],
            scratch_shapes=[pltpu.VMEM((n_state, SSD_STATE), F32)] + shared_scratch + shared_scratch,
            compiler_params=_params(vmem),
        )(xs, bm, cm, small, bias_row, alog_row, enter, dy, dsf, *acc)

    return fwd_call, bwd_call


def _ssd_pair(name, xs, bm, cm, small, rows_f, rows_b, s0_f, s0_b):
    calls = [_ssd_calls(name + "%d" % d, xs, bm, cm, s0_f, d, d == 1) for d in range(2)]

    def run_fwd(xs, bm, cm, small, rows_f, rows_b, s0_f, s0_b):
        y_f, sf_f, en_f = calls[0][0](xs, bm, cm, small, *rows_f, s0_f)
        y_b, sf_b, en_b = calls[1][0](xs, bm, cm, small, *rows_b, s0_b)
        return (y_f, y_b, sf_f, sf_b), (en_f, en_b)

    @jax.custom_vjp
    def op(*args):
        return run_fwd(*args)[0]

    def op_fwd(*args):
        outs, enters = run_fwd(*args)
        return outs, (args[:6], enters)

    def op_bwd(res, g):
        (xs, bm, cm, small, rows_f, rows_b), (en_f, en_b) = res
        dy_f, dy_b, dsf_f, dsf_b = g
        dxs, dbm, dcm, dsm, dbr_f, dar_f, ds0_f = calls[0][1](xs, bm, cm, small, *rows_f, en_f, dy_f, dsf_f)
        dxs, dbm, dcm, dsm, dbr_b, dar_b, ds0_b = calls[1][1](xs, bm, cm, small, *rows_b, en_b, dy_b, dsf_b, acc=(dxs, dbm, dcm, dsm))
        return dxs, dbm, dcm, dsm, (dbr_f, dar_f), (dbr_b, dar_b), ds0_f, ds0_b

    op.defvjp(op_fwd, op_bwd)
    return op(xs, bm, cm, small, tuple(rows_f), tuple(rows_b), s0_f, s0_b)


def _ml_shared(small, gate_row, reverse):
    gates = small + gate_row
    b_all = _exact_dot(_tri(reverse).astype(F32), jax.nn.log_sigmoid(gates))
    return gates, b_all, gates.T, b_all.T


def _ml_head(q, k, v, c_st, n_st, m_st, gates, b_all, gates_t, b_t, h, direction, reverse):
    mask = _tri(reverse)
    last = 0 if reverse else CHUNK - 1
    lane_i = _MG_LANE + 8 * direction + h
    lane_f = lane_i + ML_HEADS
    b_col = _pick_col(b_all, lane_f)
    b_row = _pick_row(b_t, lane_f)
    li_col = _pick_col(gates, lane_i)
    li_row = _pick_row(gates_t, lane_i)
    rowi = lax.broadcasted_iota(jnp.int32, (CHUNK, 1), 0)
    g_tot = jnp.sum(jnp.where(rowi == last, b_col, 0.0), axis=0, keepdims=True)
    m_in = m_st[:, 0:1]
    q = q * (ML_HD ** -0.5)
    w = g_tot - b_col + li_col
    m_loc = lax.stop_gradient(jnp.max(w, axis=0, keepdims=True))
    kw = k * jnp.exp(w - m_loc)
    c_loc = _mm_tn(kw, v)
    n_loc = jnp.sum(kw, axis=0, keepdims=True)
    m_new = lax.stop_gradient(jnp.maximum(g_tot + m_in, m_loc))
    s_old = jnp.exp(g_tot + m_in - m_new)
    s_loc = jnp.exp(m_loc - m_new)
    c_new = s_old * c_st + s_loc * c_loc
    n_new = s_old * n_st + s_loc * n_loc
    log_d = jnp.where(mask, b_col - b_row + li_row, -jnp.inf)
    inter = b_col + m_in
    m_t = lax.stop_gradient(jnp.maximum(inter, jnp.max(log_d, axis=1, keepdims=True)))
    dmat = jnp.exp(log_d - m_t)
    wi = jnp.exp(inter - m_t)
    s = _mm_nt(q, k) * dmat
    num = _mm_nn(s, v) + wi * _mm_nn(q, c_st)
    den = jnp.sum(s, axis=1, keepdims=True) + wi * jnp.sum(_round_bf16(q) * _round_bf16(n_st), axis=1, keepdims=True)
    out = num / jnp.maximum(jnp.abs(den), jnp.exp(-m_t))
    return out, c_new, n_new, jnp.broadcast_to(m_new, (1, LANES))


def _ml_calls(name, q, direction, reverse):
    n_rows = q.shape[0]
    nc = n_rows // CHUNK
    vmem = 4 * CHUNK * (4 * ML_HD + 128) * 8 + 4 * ML_HD * ML_HD * 12 + (8 << 20)

    def specs(order, hps=1):
        def cidx(c):
            return (nc - 1 - c) if order else c

        return dict(
            qkv=pl.BlockSpec((CHUNK, hps * ML_HD), lambda c, h: (cidx(c), h)),
            small=pl.BlockSpec((CHUNK, LANES), lambda c, h: (cidx(c), 0)),
            row=pl.BlockSpec((1, LANES), lambda c, h: (0, 0)),
            c=pl.BlockSpec((ML_HEADS * ML_HD, ML_HD), lambda c, h: (0, 0)),
            n=pl.BlockSpec((ML_HEADS, 1, ML_HD), lambda c, h: (0, 0, 0)),
            m=pl.BlockSpec((ML_HEADS, 1, LANES), lambda c, h: (0, 0, 0)),
            ec=pl.BlockSpec((1, hps * ML_HD, ML_HD), lambda c, h: (cidx(c), h, 0)),
            en=pl.BlockSpec((1, hps, 1, ML_HD), lambda c, h: (cidx(c), h, 0, 0)),
            em=pl.BlockSpec((1, hps, 1, LANES), lambda c, h: (cidx(c), h, 0, 0)),
        )

    st_shapes = [jax.ShapeDtypeStruct((ML_HEADS * ML_HD, ML_HD), F32), jax.ShapeDtypeStruct((ML_HEADS, 1, ML_HD), F32),
                 jax.ShapeDtypeStruct((ML_HEADS, 1, LANES), F32)]
    scratch = [pltpu.VMEM((ML_HEADS * ML_HD, ML_HD), F32), pltpu.VMEM((ML_HEADS, 1, ML_HD), F32),
               pltpu.VMEM((ML_HEADS, 1, LANES), F32)]
    shared_scratch = [pltpu.VMEM((CHUNK, LANES), F32), pltpu.VMEM((CHUNK, LANES), F32),
                      pltpu.VMEM((LANES, CHUNK), F32), pltpu.VMEM((LANES, CHUNK), F32)]

    def head_rows(h):
        return pl.ds(pl.multiple_of(h * ML_HD, ML_HD), ML_HD)

    def fwd_call(q, k, v, small, gate_row, c0, n0, m0):
        hps = ML_HEADS
        sp = specs(reverse, hps)

        def kern(q_r, k_r, v_r, sm_r, gr_r, c0_r, n0_r, m0_r, o_r, cf_r, nf_r, mf_r, ec_r, en_r, em_r, cs, ns, ms, *sh):
            c, hh = pl.program_id(0), pl.program_id(1)

            @pl.when((c == 0) & (hh == 0))
            def _():
                cs[...] = c0_r[...]
                ns[...] = n0_r[...]
                ms[...] = m0_r[...]

            @pl.when(hh == 0)
            def _():
                for ref, val in zip(sh, _ml_shared(sm_r[...], gr_r[...], reverse)):
                    ref[...] = val

            q_v, k_v, v_v = q_r[...], k_r[...], v_r[...]
            shared = [r[...] for r in sh]
            heads = [hps * hh + u for u in range(hps)]
            states = [(cs[head_rows(h), :], ns[h], ms[h]) for h in heads]
            results = []
            for u, (h, (c_in, n_in, m_in)) in enumerate(zip(heads, states)):
                cols = slice(u * ML_HD, (u + 1) * ML_HD)
                ec_r[0, cols, :] = c_in
                en_r[0, u] = n_in
                em_r[0, u] = m_in
                results.append(_ml_head(q_v[:, cols], k_v[:, cols], v_v[:, cols], c_in, n_in, m_in, *shared,
                                        h, direction, reverse))
            o_r[...] = jnp.concatenate([r[0] for r in results], axis=1)
            for h, (_, c_new, n_new, m_new) in zip(heads, results):
                cs[head_rows(h), :] = c_new
                ns[h] = n_new
                ms[h] = m_new
                cf_r[head_rows(h), :] = c_new
                nf_r[h] = n_new
                mf_r[h] = m_new

        return pl.pallas_call(
            kern, grid=(nc, ML_HEADS // hps), name=name + "_fwd",
            in_specs=[sp['qkv']] * 3 + [sp['small'], sp['row'], sp['c'], sp['n'], sp['m']],
            out_specs=[sp['qkv'], sp['c'], sp['n'], sp['m'], sp['ec'], sp['en'], sp['em']],
            out_shape=[jax.ShapeDtypeStruct((n_rows, ML_HEADS * ML_HD), F32)] + st_shapes + [
                jax.ShapeDtypeStruct((nc, ML_HEADS * ML_HD, ML_HD), F32),
                jax.ShapeDtypeStruct((nc, ML_HEADS, 1, ML_HD), F32), jax.ShapeDtypeStruct((nc, ML_HEADS, 1, LANES), F32)],
            scratch_shapes=scratch + shared_scratch, compiler_params=_params(vmem),
        )(q, k, v, small, gate_row, c0, n0, m0)

    def bwd_call(q, k, v, small, gate_row, ec, en, em, do, dcf, dnf, dmf, acc=()):
        hps = 2
        sp = specs(not reverse, hps)
        n_sh = len(shared_scratch)

        def kern(*refs):
            q_r, k_r, v_r, sm_r, gr_r, ec_r, en_r, em_r, do_r, dcf_r, dnf_r, dmf_r = refs[:12]
            acc_r = refs[12:12 + len(acc)]
            dq_r, dk_r, dv_r, dsm_r, dgr_r, dc0_r, dn0_r, dm0_r, dcs, dns, dms = refs[12 + len(acc):23 + len(acc)]
            rest = refs[23 + len(acc):]
            sh, gsh = rest[:n_sh], rest[n_sh:]
            c, hh = pl.program_id(0), pl.program_id(1)

            @pl.when((c == 0) & (hh == 0))
            def _():
                dcs[...] = dcf_r[...]
                dns[...] = dnf_r[...]
                dms[...] = dmf_r[...]
                dgr_r[...] = jnp.zeros_like(dgr_r)

            @pl.when(hh == 0)
            def _():
                for ref, val in zip(sh, _ml_shared(sm_r[...], gr_r[...], reverse)):
                    ref[...] = val
                for ref in gsh:
                    ref[...] = jnp.zeros_like(ref)

            heads = [hps * hh + u for u in range(hps)]
            q_v, k_v, v_v, do_v = q_r[...], k_r[...], v_r[...], do_r[...]
            shared_vals = [r[...] for r in sh]
            cots = [(dcs[head_rows(h), :], dns[h], dms[h]) for h in heads]
            grads = []
            for u, h in enumerate(heads):
                cols = slice(u * ML_HD, (u + 1) * ML_HD)
                fn = functools.partial(_ml_head, h=h, direction=direction, reverse=reverse)
                _, vjp = jax.vjp(fn, q_v[:, cols], k_v[:, cols], v_v[:, cols], ec_r[0, cols, :], en_r[0, u], em_r[0, u], *shared_vals)
                grads.append(vjp((do_v[:, cols],) + cots[u]))
            dq, dk, dv = [jnp.concatenate([g[i] for g in grads], axis=1) for i in range(3)]
            dq_r[...] = dq + acc_r[0][...] if acc else dq
            dk_r[...] = dk + acc_r[1][...] if acc else dk
            dv_r[...] = dv + acc_r[2][...] if acc else dv
            for i, ref in enumerate(gsh):
                ref[...] += sum(g[6 + i] for g in grads)
            for h, g in zip(heads, grads):
                dc, dn, dm = g[3:6]
                dm = jnp.broadcast_to(jnp.sum(dm, axis=1, keepdims=True), (1, LANES)) * (1.0 / LANES)
                dcs[head_rows(h), :] = dc
                dns[h] = dn
                dms[h] = dm
                dc0_r[head_rows(h), :] = dc
                dn0_r[h] = dn
                dm0_r[h] = dm

            @pl.when(hh == ML_HEADS // hps - 1)
            def _():
                shared = functools.partial(_ml_shared, reverse=reverse)
                dsm, dgr = jax.vjp(shared, sm_r[...], gr_r[...])[1](tuple(r[...] for r in gsh))
                dsm_r[...] = dsm + acc_r[3][...] if acc else dsm
                dgr_r[...] += dgr

        return pl.pallas_call(
            kern, grid=(nc, ML_HEADS // hps), name=name + "_bwd",
            in_specs=[sp['qkv']] * 3 + [sp['small'], sp['row'], sp['ec'], sp['en'], sp['em'], sp['qkv'], sp['c'], sp['n'], sp['m']]
            + ([sp['qkv']] * 3 + [sp['small']] if acc else []),
            out_specs=[sp['qkv']] * 3 + [sp['small'], sp['row'], sp['c'], sp['n'], sp['m']],
            out_shape=[jax.ShapeDtypeStruct(q.shape, F32)] * 3 + [jax.ShapeDtypeStruct((n_rows, LANES), F32),
                                                                  jax.ShapeDtypeStruct((1, LANES), F32)] + st_shapes,
            scratch_shapes=scratch + shared_scratch + shared_scratch, compiler_params=_params(vmem),
        )(q, k, v, small, gate_row, ec, en, em, do, dcf, dnf, dmf, *acc)

    return fwd_call, bwd_call


def _ml_pair(name, q, k, v, small, gate_row, state_f, state_b):
    calls = [_ml_calls(name + "%d" % d, q, d, d == 1) for d in range(2)]

    def run_fwd(q, k, v, small, gate_row, state_f, state_b):
        res_f = calls[0][0](q, k, v, small, gate_row, *state_f)
        res_b = calls[1][0](q, k, v, small, gate_row, *state_b)
        return (res_f[0], res_b[0], tuple(res_f[1:4]), tuple(res_b[1:4])), (tuple(res_f[4:]), tuple(res_b[4:]))

    @jax.custom_vjp
    def op(*args):
        return run_fwd(*args)[0]

    def op_fwd(*args):
        outs, enters = run_fwd(*args)
        return outs, (args[:5], enters)

    def op_bwd(res, g):
        (q, k, v, small, gate_row), (en_f, en_b) = res
        do_f, do_b, ds_f, ds_b = g
        dq, dk, dv, dsm, dgr_f, *d0_f = calls[0][1](q, k, v, small, gate_row, *en_f, do_f, *ds_f)
        dq, dk, dv, dsm, dgr_b, *d0_b = calls[1][1](q, k, v, small, gate_row, *en_b, do_b, *ds_b, acc=(dq, dk, dv, dsm))
        return dq, dk, dv, dsm, dgr_f + dgr_b, tuple(d0_f), tuple(d0_b)

    op.defvjp(op_fwd, op_bwd)
    return op(q, k, v, small, gate_row, tuple(state_f), tuple(state_b))


def _f_modulate(x, shift, scale):
    return (_layernorm_rows(x) * (1.0 + scale) + shift,)


def _f_resid_ln(x, o, gate, bias, ln_g, ln_b):
    return (_layernorm_rows(DN_ALPHA * x + gate * (o + bias)) * ln_g + ln_b,)


def _f_lru_gates(xc, w_r, b_r, w_i, b_i, lam):
    outs = []
    for d in range(2):
        def blockdiag(w):
            return jnp.concatenate(
                [_mm_nn(xc[:, n * LRU_BS:(n + 1) * LRU_BS], w[(d * LRU_BLOCKS + n) * LRU_BS:(d * LRU_BLOCKS + n + 1) * LRU_BS, :])
                 for n in range(LRU_BLOCKS)], axis=1)

        r = jax.nn.sigmoid(blockdiag(w_r) + b_r[d:d + 1])
        i = jax.nn.sigmoid(blockdiag(w_i) + b_i[d:d + 1])
        log_a = -LRU_C * r * jax.nn.softplus(-lam[d:d + 1])
        outs += [jnp.exp(log_a), jnp.sqrt(1.0 - jnp.exp(2.0 * log_a)) * i * xc]
    return tuple(outs)


def _f_lru_out(h_f, h_b, ly):
    return ((h_f + h_b) * jax.nn.gelu(ly),)


def _f_ssd_post(y_f, y_b, xs, z, d_exp, norm_w):
    y = (y_f + y_b + xs * d_exp) * jax.nn.silu(z)
    gw = SSD_INNER // SSD_GROUPS
    parts = []
    for g in range(SSD_GROUPS):
        yg = y[:, g * gw:(g + 1) * gw]
        parts.append(yg * lax.rsqrt(jnp.mean(jnp.square(yg), -1, keepdims=True) + LN_EPS))
    return (jnp.concatenate(parts, axis=1) * norm_w,)


def _f_ml_post(h_f, h_b, o, norm_w):
    h = h_f + h_b
    parts = [_layernorm_rows(h[:, i * ML_HD:(i + 1) * ML_HD]) for i in range(ML_HEADS)]
    return (jnp.concatenate(parts, axis=1) * norm_w * jax.nn.sigmoid(o),)


def _f_merge(ga, gb, gc, pa, pb, pc):
    return (jax.nn.sigmoid(ga) * pa + jax.nn.sigmoid(gb) * pb + jax.nn.sigmoid(gc) * pc,)


def _f_relu2(pre, bias):
    return (jnp.square(jax.nn.relu(pre + bias)),)


def _lane_row(vec, start):
    return jnp.pad(vec[None], ((0, 0), (start, LANES - start - vec.shape[0])))


def _mixer(tag, x_tok, shift, scale, p, states):
    (lru_s, ssd_s, ml_s) = states
    lx, ly, sz, xs, bm, cm, mq, mk, mv, mo, ga, gb, gc, small = _rowwise_linear(
        tag + "in", _f_modulate, [x_tok], [shift, scale], [(p['w_in_main'], _IN_MAIN_WIDTHS), (p['w_in_small'], None)])

    xc = _dwconv(tag + "lruconv", lx, p['lru_conv_w'], p['lru_conv_b'][None], False)
    a_f, b_f, a_b, b_b = _rowwise(
        tag + "lrugate", _f_lru_gates, [xc],
        [p['lru_w_r'].reshape(2 * LRU_BLOCKS * LRU_BS, LRU_BS), p['lru_b_r'], p['lru_w_i'].reshape(2 * LRU_BLOCKS * LRU_BS, LRU_BS),
         p['lru_b_i'], p['lru_lambda']], [D_MODEL] * 4, tile_cap=128)
    h_f, s_f = _lin_scan(tag + "lruscanf", a_f, b_f, lru_s[0], False)
    h_b, s_b = _lin_scan(tag + "lruscanb", a_b, b_b, lru_s[1], True)
    (pa,) = _rowwise_linear(tag + "bra", _f_lru_out, [h_f, h_b, ly], [], [(p['w_br_a'], None)], to_linear=(2,))

    cw, cb_ = p['ssd_conv_w'], p['ssd_conv_b'][None]
    xs_c = _dwconv(tag + "ssdconvx", xs, cw[:, :2048], cb_[:, :2048], True)
    bm_c = _dwconv(tag + "ssdconvb", bm, cw[:, 2048:3072], cb_[:, 2048:3072], True)
    cm_c = _dwconv(tag + "ssdconvc", cm, cw[:, 3072:], cb_[:, 3072:], True)
    dir_rows = [(_lane_row(p['ssd_dt_bias'][d], _DT_LANE + 32 * d), _lane_row(p['ssd_a_log'][d], _DT_LANE + 32 * d)) for d in range(2)]
    *ys, st_f, st_b = _ssd_pair(tag + "ssd", xs_c, bm_c, cm_c, small, dir_rows[0], dir_rows[1], ssd_s[0], ssd_s[1])
    ssd_new = (st_f, st_b)
    (pb,) = _rowwise_linear(tag + "brb", _f_ssd_post, [ys[0], ys[1], xs_c, sz],
                            [jnp.repeat(p['ssd_d'], SSD_HEADDIM)[None], p['ssd_norm_w'][None]], [(p['w_br_b'], None)], to_linear=(3,))

    mw, mb = p['ml_conv_w'], p['ml_conv_b'][None]
    q_c = _dwconv(tag + "mlconvq", mq, mw[:, :1024], mb[:, :1024], True)
    k_c = _dwconv(tag + "mlconvk", mk, mw[:, 1024:], mb[:, 1024:], True)
    gate_row = _lane_row(p['ml_gate_b'].reshape(4 * ML_HEADS), _MG_LANE)
    *hs, ml_f, ml_b = _ml_pair(tag + "ml", q_c, k_c, mv, small, gate_row, ml_s[0], ml_s[1])
    ml_new = (ml_f, ml_b)
    (pc,) = _rowwise_linear(tag + "brc", _f_ml_post, [hs[0], hs[1], mo], [p['ml_norm_w'][None]], [(p['w_br_c'], None)], to_linear=(2,))
    return (ga, gb, gc, pa, pb, pc), ((s_f, s_b), tuple(ssd_new), tuple(ml_new))


def _merge(tag, br, p):
    return _rowwise_linear(tag + "out", _f_merge, list(br), [], [(p['w_out'], None)], to_linear=tuple(range(6)))[0]


def _sublayers(tag, xin, o, mods, p, l):
    sh2, sc2, g1, g2 = mods
    (x1,) = _rowwise(tag + "ln1", _f_resid_ln, [xin, o], [g1, p['b_out'][None], p['ln1_g'][None], p['ln1_b'][None]], [D_MODEL], to_linear=(1,))
    (pre,) = _rowwise_linear(tag + "ff1", _f_modulate, [x1], [sh2, sc2], [(p['w_ff1'], None)])
    (o2,) = _rowwise_linear(tag + "ff2", _f_relu2, [pre], [p['b_ff1'][None]], [(p['w_ff2'], None)], to_linear=(0,))
    (x2,) = _rowwise(tag + "ln2", _f_resid_ln, [x1, o2], [g2, p['b_ff2'][None], p['ln2_g'][None], p['ln2_b'][None]], [D_MODEL], to_linear=(1,))
    return x2


def _to_col_major(h):
    s, d = h.shape
    return h.reshape(s // GRID_W, GRID_W, d).swapaxes(0, 1).reshape(s, d)


def _from_col_major(h):
    s, d = h.shape
    return h.reshape(GRID_W, s // GRID_W, d).swapaxes(0, 1).reshape(s, d)


def _forward(x, wts, mods, ctx):
    zeros = lambda *s: jnp.zeros(s, F32)
    ctx_init = ((zeros(1, D_MODEL), zeros(1, D_MODEL)),
                (zeros(SSD_INNER, SSD_STATE), zeros(SSD_INNER, SSD_STATE)),
                tuple((zeros(ML_HEADS * ML_HD, ML_HD), zeros(ML_HEADS, 1, ML_HD), zeros(ML_HEADS, 1, LANES)) for _ in range(2)))
    for l in range(DEPTH):
        p = {n: wts[n][l] for n in wts}
        tag = "l%d" % l
        sh1x, sc1x, g1x, sh2x, sc2x, g2x = [mods[l][0][:, i * D_MODEL:(i + 1) * D_MODEL] for i in range(6)]
        sh1c, sc1c, g1c, sh2c, sc2c, g2c = [mods[l][1][:, i * D_MODEL:(i + 1) * D_MODEL] for i in range(6)]
        br_c, ctx_states = _mixer(tag + "c", ctx, sh1c, sc1c, p, ctx_init)
        br_x, _ = _mixer(tag + "x", _to_col_major(x) if l % 2 == 1 else x, sh1x, sc1x, p, ctx_states)
        ox = _merge(tag + "x", br_x, p)
        if l % 2 == 1:
            ox = _from_col_major(ox)
        x = _sublayers(tag + "x", x, ox, (sh2x, sc2x, g1x, g2x), p, l)
        if l < DEPTH - 1:
            ctx = _sublayers(tag + "c", ctx, _merge(tag + "c", br_c, p), (sh2c, sc2c, g1c, g2c), p, l)
    return x


_ADA_ROWS = 2 * SUBLANES


def _ada_forward(c, c_ctx, w_ada, b_ada, me):
    c_all = _exchange("gather_c", jnp.broadcast_to(c, (SUBLANES, D_MODEL)), True)[:, 0]

    def rows_of(c_ctx_):
        pad = jnp.zeros((_ADA_ROWS - N_DEV - 1, D_MODEL), F32)
        return jax.nn.silu(jnp.concatenate([c_all, c_ctx_[None], pad], axis=0))

    rows, vjp_rows = jax.vjp(rows_of, c_ctx)
    cols, vjp_cols = jax.vjp(lambda r, w: jnp.stack([_linear("ada%d" % l, r, w[l]) for l in range(DEPTH)]), rows, w_ada)
    full = _exchange("gather_mod", cols, True).transpose(1, 2, 0, 3).reshape(DEPTH, _ADA_ROWS, 6 * D_MODEL) + b_ada[:, None, :]
    mods = [(lax.dynamic_slice_in_dim(full[l], me, 1, axis=0), full[l][N_DEV:N_DEV + 1]) for l in range(DEPTH)]
    return mods, (vjp_rows, vjp_cols)


def _ada_backward(saved, dmods):
    vjp_rows, vjp_cols = saved
    wcol = 6 * D_MODEL // N_DEV
    pad = jnp.zeros((SUBLANES - 2, 6 * D_MODEL), F32)
    both = jnp.stack([jnp.concatenate([dx, dc, pad], axis=0) for dx, dc in dmods])
    send = both.reshape(DEPTH, SUBLANES, N_DEV, wcol).transpose(2, 0, 1, 3)
    recv = _exchange("scatter_dmod", send, False)
    ctx_row = recv[0, :, 1]
    for k in range(1, N_DEV):
        ctx_row = ctx_row + recv[k, :, 1]
    g = jnp.concatenate([recv[:, :, 0].transpose(1, 0, 2), ctx_row[:, None],
                         jnp.zeros((DEPTH, _ADA_ROWS - N_DEV - 1, wcol), F32)], axis=1)
    d_rows, d_w = vjp_cols(g)
    (d_c_ctx,) = vjp_rows(d_rows)
    d_b = jnp.stack([(dx + dc)[0] for dx, dc in dmods])
    return d_w, d_b, d_c_ctx


def _loss_and_cotangent(y, target):
    n_rows, d = y.shape
    tt = _row_tile(n_rows, 0, cap=256)

    def kern(y_ref, t_ref, dy_ref, acc_ref):
        @pl.when(pl.program_id(0) == 0)
        def _():
            acc_ref[...] = jnp.zeros_like(acc_ref)

        err = y_ref[...] - t_ref[...]
        dy_ref[...] = err * (1.0 / d)
        acc_ref[...] += jnp.sum(jnp.square(err))

    spec = pl.BlockSpec((tt, d), lambda i: (i, 0))
    dy, acc = pl.pallas_call(
        kern, grid=(n_rows // tt,), name="loss", in_specs=[spec, spec],
        out_specs=[spec, pl.BlockSpec((SUBLANES, LANES), lambda i: (0, 0))],
        out_shape=[jax.ShapeDtypeStruct((n_rows, d), F32), jax.ShapeDtypeStruct((SUBLANES, LANES), F32)],
    )(y, target)
    return acc[0, 0] * (0.5 / d), dy


def _exchange(name, src, gather):
    slab = src.shape if gather else src.shape[1:]

    def body(src_ref, out_ref, send_sems, recv_sems, local_sem):
        x, y, c = lax.axis_index("x"), lax.axis_index("y"), lax.axis_index("c")
        me = 4 * x + 2 * y + c
        local = pltpu.make_async_copy(src_ref if gather else src_ref.at[me], out_ref.at[me], local_sem)
        local.start()
        copies = []
        for d in range(1, N_DEV):
            px, py, pc = lax.rem(x + (d >> 2), 2), lax.rem(y + ((d >> 1) & 1), 2), lax.rem(c + (d & 1), 2)
            peer = 4 * px + 2 * py + pc
            cp = pltpu.make_async_remote_copy(
                src_ref=src_ref if gather else src_ref.at[peer], dst_ref=out_ref.at[me],
                send_sem=send_sems.at[d - 1], recv_sem=recv_sems.at[d - 1],
                device_id=(px, py, pc), device_id_type=pl.DeviceIdType.MESH)
            cp.start()
            copies.append(cp)
        for cp in copies:
            cp.wait()
        local.wait()

    return pl.pallas_call(
        body, name=name, out_shape=jax.ShapeDtypeStruct((N_DEV,) + tuple(slab), src.dtype),
        in_specs=[pl.BlockSpec(memory_space=pl.ANY)], out_specs=pl.BlockSpec(memory_space=pl.ANY),
        scratch_shapes=[pltpu.SemaphoreType.DMA((N_DEV - 1,)), pltpu.SemaphoreType.DMA((N_DEV - 1,)), pltpu.SemaphoreType.DMA],
    )(src)


_HBM = pl.BlockSpec(memory_space=pl.ANY)
_CHIPS = ((0, 0), (0, 1), (1, 0), (1, 1))


def _gather_two_level(name, srcs):
    n = len(srcs)

    def body(*refs):
        src_refs, out_refs = refs[:n], refs[n:2 * n]
        send_sems, recv_sems, local_sems = refs[2 * n:]
        x, y, c = lax.axis_index("x"), lax.axis_index("y"), lax.axis_index("c")
        me, sibling = (x, y, c), (x, y, 1 - c)
        chips = [(1 - x, y), (x, 1 - y), (1 - x, 1 - y)]

        def slab(a, px, py, pc):
            return out_refs[a].at[4 * px + 2 * py + pc]

        def copy(a, k, block, to, own=False):
            return pltpu.make_async_remote_copy(
                src_ref=src_refs[a] if own else slab(a, *block), dst_ref=slab(a, *block), send_sem=send_sems.at[a, k],
                recv_sem=recv_sems.at[a, k], device_id=to, device_id_type=pl.DeviceIdType.MESH)

        mine = [pltpu.make_async_copy(src_refs[a], slab(a, *me), local_sems.at[a]) for a in range(n)]
        first = []
        for a in range(n):
            mine[a].start()
            first += [copy(a, 0, me, sibling, own=True)] + [copy(a, 1 + j, me, (*chip, c), own=True) for j, chip in enumerate(chips)]
        for cp in first:
            cp.start()
        passed = []
        for j, chip in enumerate(chips):
            for a in range(n):
                copy(a, 1 + j, (*chip, c), me).wait_recv()
                passed.append(copy(a, 4 + j, (*chip, c), sibling))
                passed[-1].start()
        for a in range(n):
            copy(a, 0, sibling, me).wait_recv()
        for j, chip in enumerate(chips):
            for a in range(n):
                copy(a, 4 + j, (*chip, 1 - c), me).wait_recv()
        for cp in first + passed:
            cp.wait_send()
        for cp in mine:
            cp.wait()

    return pl.pallas_call(
        body, name=name, out_shape=[jax.ShapeDtypeStruct((N_DEV,) + tuple(s.shape), s.dtype) for s in srcs],
        in_specs=[_HBM] * n, out_specs=[_HBM] * n,
        scratch_shapes=[pltpu.SemaphoreType.DMA((n, N_DEV - 1)), pltpu.SemaphoreType.DMA((n, N_DEV - 1)), pltpu.SemaphoreType.DMA((n,))],
    )(*srcs)


def _scatter_to_sibling(name, parts_list):
    n = len(parts_list)

    def body(*refs):
        p_refs, out_refs = refs[:n], refs[n:2 * n]
        send_sems, recv_sems = refs[2 * n:]
        x, y, c = lax.axis_index("x"), lax.axis_index("y"), lax.axis_index("c")
        copies = []
        for a in range(n):
            for j, (px, py) in enumerate(_CHIPS):
                cp = pltpu.make_async_remote_copy(
                    src_ref=p_refs[a].at[4 * px + 2 * py + (1 - c)], dst_ref=out_refs[a].at[j], send_sem=send_sems.at[a, j],
                    recv_sem=recv_sems.at[a, j], device_id=(x, y, 1 - c), device_id_type=pl.DeviceIdType.MESH)
                cp.start()
                copies.append(cp)
        for cp in copies:
            cp.wait()

    return pl.pallas_call(
        body, name=name, out_shape=[jax.ShapeDtypeStruct((4,) + tuple(p.shape[1:]), p.dtype) for p in parts_list],
        in_specs=[_HBM] * n, out_specs=[_HBM] * n,
        scratch_shapes=[pltpu.SemaphoreType.DMA((n, 4)), pltpu.SemaphoreType.DMA((n, 4))],
    )(*parts_list)


def _chip_sum(name, parts, from_sibling):
    _, rows, cols = parts.shape
    lanes = -(-cols // LANES) * LANES
    tr = _row_tile(rows, 4 * lanes * 4 * 2, budget=24 << 20)

    def kern(p_ref, s_ref, o_ref):
        c = lax.axis_index("c")
        o_ref[0] = (jnp.where(c == 0, p_ref[0, 0], p_ref[0, 1]) + s_ref[0]).astype(o_ref.dtype)

    return pl.pallas_call(
        kern, grid=(4, rows // tr), name=name,
        in_specs=[pl.BlockSpec((1, 2, tr, cols), lambda j, i: (j, 0, i, 0)), pl.BlockSpec((1, tr, cols), lambda j, i: (j, i, 0))],
        out_specs=pl.BlockSpec((1, tr, cols), lambda j, i: (j, i, 0)),
        out_shape=jax.ShapeDtypeStruct((4, rows, cols), BF16),
        compiler_params=_params(4 * lanes * tr * 4 * 2),
    )(parts.reshape(4, 2, rows, cols), from_sibling)


def _scatter_across_chips(name, sums_list):
    n = len(sums_list)

    def body(*refs):
        q_refs, out_refs = refs[:n], refs[n:2 * n]
        send_sems, recv_sems, local_sems = refs[2 * n:]
        x, y, c = lax.axis_index("x"), lax.axis_index("y"), lax.axis_index("c")
        own = 2 * x + y
        copies = []
        for a in range(n):
            local = pltpu.make_async_copy(q_refs[a].at[own], out_refs[a].at[own], local_sems.at[a])
            local.start()
            copies.append(local)
            for d in range(1, 4):
                px, py = lax.rem(x + (d >> 1), 2), lax.rem(y + (d & 1), 2)
                cp = pltpu.make_async_remote_copy(
                    src_ref=q_refs[a].at[2 * px + py], dst_ref=out_refs[a].at[own], send_sem=send_sems.at[a, d - 1],
                    recv_sem=recv_sems.at[a, d - 1], device_id=(px, py, c), device_id_type=pl.DeviceIdType.MESH)
                cp.start()
                copies.append(cp)
        for cp in copies:
            cp.wait()

    return pl.pallas_call(
        body, name=name, out_shape=[jax.ShapeDtypeStruct(s.shape, s.dtype) for s in sums_list],
        in_specs=[_HBM] * n, out_specs=[_HBM] * n,
        scratch_shapes=[pltpu.SemaphoreType.DMA((n, 3)), pltpu.SemaphoreType.DMA((n, 3)), pltpu.SemaphoreType.DMA((n,))],
    )(*sums_list)


def _sum_parts(name, parts):
    n_parts, rows, cols = parts.shape
    tr = _row_tile(rows, 4 * cols * (n_parts + 1) * 2)

    def kern(p_ref, o_ref):
        acc = p_ref[0]
        for k in range(1, n_parts):
            acc = acc + p_ref[k]
        o_ref[...] = acc

    return pl.pallas_call(
        kern, grid=(rows // tr,), name=name, in_specs=[pl.BlockSpec((n_parts, tr, cols), lambda i: (0, i, 0))],
        out_specs=pl.BlockSpec((tr, cols), lambda i: (i, 0)), out_shape=jax.ShapeDtypeStruct((rows, cols), F32),
    )(parts)


def _adamw(name, w, m, v, parts):
    n_parts, rows, cols = parts.shape
    lanes = -(-cols // LANES) * LANES
    tr = _row_tile(rows, 4 * lanes * (n_parts + 7) * 2, budget=28 << 20)
    c1 = np.float32(1.0 - ADAM_B1 ** ADAM_STEP)
    c2 = np.float32(1.0 - ADAM_B2 ** ADAM_STEP)

    def kern(w_ref, m_ref, v_ref, p_ref, g_ref, d_ref, nm_ref, nv_ref):
        g = p_ref[0].astype(F32)
        for k in range(1, n_parts):
            g = g + p_ref[k].astype(F32)
        m_new = ADAM_B1 * m_ref[...] + (1.0 - ADAM_B1) * g
        v_new = ADAM_B2 * v_ref[...] + (1.0 - ADAM_B2) * jnp.square(g)
        g_ref[...] = g
        nm_ref[...] = m_new
        nv_ref[...] = v_new
        d_ref[...] = -ADAM_LR * ((m_new / c1) / (jnp.sqrt(v_new / c2) + ADAM_EPS) + ADAM_WD * w_ref[...])

    spec = pl.BlockSpec((tr, cols), lambda i: (i, 0))
    return pl.pallas_call(
        kern, grid=(rows // tr,), name=name,
        in_specs=[spec, spec, spec, pl.BlockSpec((n_parts, tr, cols), lambda i: (0, i, 0))], out_specs=[spec] * 4,
        out_shape=[jax.ShapeDtypeStruct((rows, cols), F32)] * 4,
        compiler_params=_params(4 * lanes * tr * (n_parts + 7) * 2),
    )(w, m, v, parts)


def _packed_rows(shape):
    return -(-int(np.prod(shape)) // (SUBLANES * LANES)) * SUBLANES


def _pack(arrays, row_multiple):
    parts = []
    for a in arrays:
        n = int(np.prod(a.shape))
        r = _packed_rows(a.shape)
        parts.append(jnp.pad(a.reshape(-1), (0, r * LANES - n)).reshape(r, LANES))
    rows = sum(p.shape[0] for p in parts)
    total = -(-rows // row_multiple) * row_multiple
    if total > rows:
        parts.append(jnp.zeros((total - rows, LANES), arrays[0].dtype))
    return jnp.concatenate(parts, axis=0)


def _unpack(packed, shapes):
    out, off = [], 0
    for s in shapes:
        r = _packed_rows(s)
        out.append(packed[off:off + r].reshape(-1)[:int(np.prod(s))].reshape(s))
        off += r
    return out


def _split_w_in(w_in):
    main = jnp.concatenate([w_in[:, :, s:e] for s, e in _IN_MAIN], axis=2)
    pad = jnp.zeros(w_in.shape[:2] + (LANES - 80,), w_in.dtype)
    small = jnp.concatenate([w_in[:, :, s:e] for s, e in _IN_SMALL] + [pad], axis=2)
    return main, small


def _join_w_in(main, small):
    return jnp.concatenate([main[:, :, 0:8192], small[:, :, 0:64], main[:, :, 8192:12288], small[:, :, 64:80],
                            main[:, :, 12288:15360]], axis=2)


def _unshard(gathered, axis):
    nd, nl, r, c = gathered.shape
    if axis == 1:
        return gathered.transpose(1, 0, 2, 3).reshape(nl, nd * r, c)
    return gathered.transpose(1, 2, 0, 3).reshape(nl, r, nd * c)


def _reshard(full, axis):
    nl, r, c = full.shape
    if axis == 1:
        return full.reshape(nl, N_DEV, r // N_DEV, c).transpose(1, 0, 2, 3)
    return full.reshape(nl, r, N_DEV, c // N_DEV).transpose(2, 0, 1, 3)


def kernel(x, c, ctx, c_ctx, w_ada, b_ada, w_in, lru_conv_w, lru_conv_b, lru_w_r, lru_b_r, lru_w_i, lru_b_i, lru_lambda, ssd_conv_w, ssd_conv_b, ssd_dt_bias, ssd_a_log, ssd_d, ssd_norm_w, ml_conv_w, ml_conv_b, ml_gate_b, ml_norm_w, w_br_a, w_br_b, w_br_c, w_out, b_out, ln1_g, ln1_b, w_ff1, b_ff1, w_ff2, b_ff2, ln2_g, ln2_b, loss_target, m_c_ctx, m_w_ada, m_b_ada, m_w_in, m_lru_conv_w, m_lru_conv_b, m_lru_w_r, m_lru_b_r, m_lru_w_i, m_lru_b_i, m_lru_lambda, m_ssd_conv_w, m_ssd_conv_b, m_ssd_dt_bias, m_ssd_a_log, m_ssd_d, m_ssd_norm_w, m_ml_conv_w, m_ml_conv_b, m_ml_gate_b, m_ml_norm_w, m_w_br_a, m_w_br_b, m_w_br_c, m_w_out, m_b_out, m_ln1_g, m_ln1_b, m_w_ff1, m_b_ff1, m_w_ff2, m_b_ff2, m_ln2_g, m_ln2_b, v_c_ctx, v_w_ada, v_b_ada, v_w_in, v_lru_conv_w, v_lru_conv_b, v_lru_w_r, v_lru_b_r, v_lru_w_i, v_lru_b_i, v_lru_lambda, v_ssd_conv_w, v_ssd_conv_b, v_ssd_dt_bias, v_ssd_a_log, v_ssd_d, v_ssd_norm_w, v_ml_conv_w, v_ml_conv_b, v_ml_gate_b, v_ml_norm_w, v_w_br_a, v_w_br_b, v_w_br_c, v_w_out, v_b_out, v_ln1_g, v_ln1_b, v_w_ff1, v_b_ff1, v_w_ff2, v_b_ff2, v_ln2_g, v_ln2_b):
    a = dict(locals())
    me = 4 * lax.axis_index("x") + 2 * lax.axis_index("y") + lax.axis_index("c")

    wts = {n: a[n] for n in _REPLICATED if n not in ('c_ctx', 'b_ada')}
    exchanged = [n for n in _BIG if n != 'w_ada']
    gathered = _gather_two_level("gather_weights", [a[n].astype(BF16) for n in exchanged])
    for n, g in zip(exchanged, gathered):
        full = _unshard(g, _BIG[n])
        if n == 'w_in':
            main, small = _split_w_in(full)
            wts['w_in_main'], wts['w_in_small'] = main.astype(F32), small.astype(F32)
        else:
            wts[n] = full.astype(F32)
    small_shapes = [a[n].shape for n in _SMALL_SHARDED]
    small_all = _exchange("gather_small", _pack([a[n] for n in _SMALL_SHARDED], SUBLANES), True)
    per_dev = [_unpack(small_all[k], small_shapes) for k in range(N_DEV)]
    for i, n in enumerate(_SMALL_SHARDED):
        wts[n] = jnp.concatenate([per_dev[k][i] for k in range(N_DEV)], axis=-1)

    mods, ada_saved = _ada_forward(c, c_ctx, w_ada, b_ada, me)
    y, vjp = jax.vjp(functools.partial(_forward, ctx=ctx[0]), x[0], wts, mods)
    loss_local, dy = _loss_and_cotangent(y, loss_target[0])
    grad_x, grads, dmods = vjp(dy)
    grads['w_in'] = _join_w_in(grads.pop('w_in_main'), grads.pop('w_in_small'))
    grad_w_ada, grads['b_ada'], grads['c_ctx'] = _ada_backward(ada_saved, dmods)
    loss = lax.psum(loss_local, ("x", "y", "c"))

    out = {}

    def put(n, res, shape):
        for kind, r in zip(("grad_", "delta_", "new_m_", "new_v_"), res):
            out[kind + n] = r.reshape(shape)

    flat = {n: (a[n].shape[0] * a[n].shape[1], a[n].shape[2]) for n in _BIG}
    by_dest = [_reshard(grads[n], _BIG[n]).reshape(N_DEV, *flat[n]) for n in exchanged]
    from_sibling = _scatter_to_sibling("scatter_d2d", by_dest)
    chip_sums = [_chip_sum("chipsum_" + n, p, s) for n, p, s in zip(exchanged, by_dest, from_sibling)]
    summed = dict(zip(exchanged, _scatter_across_chips("scatter_ici", chip_sums)))
    summed['w_ada'] = grad_w_ada.reshape(1, *flat['w_ada'])
    for n in _BIG:
        shp = a[n].shape
        rows, cols = flat[n]
        parts = summed[n]
        put(n, _adamw("adamw_" + n, a[n].reshape(rows, cols), a["m_" + n].reshape(rows, cols), a["v_" + n].reshape(rows, cols), parts), shp)

    rep_names = _REPLICATED + _SMALL_SHARDED
    chunk_rows = SUBLANES * N_DEV
    g_pack = _pack([grads[n] for n in rep_names], chunk_rows * N_DEV)
    rows = g_pack.shape[0]
    parts = _exchange("scatter_rep", g_pack.reshape(N_DEV, rows // N_DEV, LANES), False)
    mine = _sum_parts("sum_rep", parts)
    g_all = _exchange("gather_rep", mine, True).reshape(rows, LANES)
    g_full = _unpack(g_all, [grads[n].shape for n in rep_names])
    g_local = []
    for n, g in zip(rep_names, g_full):
        if n in _SMALL_SHARDED:
            width = a[n].shape[-1]
            g = lax.dynamic_slice_in_dim(g, me * width, width, axis=g.ndim - 1)
        g_local.append(g)
    shapes = [a[n].shape for n in rep_names]
    res = _adamw("adamw_rep", _pack([a[n] for n in rep_names], chunk_rows), _pack([a["m_" + n] for n in rep_names], chunk_rows),
                 _pack([a["v_" + n] for n in rep_names], chunk_rows), _pack(g_local, chunk_rows)[None])
    unpacked = [_unpack(r, shapes) for r in res]
    for i, n in enumerate(rep_names):
        put(n, [u[i] for u in unpacked], shapes[i])

    outs = [loss, grad_x[None]]
    for kind in ("grad_", "delta_", "new_m_", "new_v_"):
        outs += [out[kind + n] for n in _WEIGHTS]
    return tuple(outs)
```

```python
import functools

import numpy as np
import jax
import jax.numpy as jnp
from jax import lax
from jax.experimental import pallas as pl
from jax.experimental.pallas import tpu as pltpu

F32 = jnp.float32
BF16 = jnp.bfloat16

N_DEV = 8
D_MODEL = 1024
DEPTH = 2
GRID_W = 64
CHUNK = 128
LN_EPS = 1e-6
LRU_BLOCKS = 8
LRU_BS = 128
LRU_C = 8.0
SSD_INNER = 2048
SSD_GROUPS = 8
SSD_HPG = 4
SSD_HEADDIM = 64
SSD_STATE = 128
ML_HEADS = 4
ML_HD = 256
D_FF = 4096
DN_ALPHA = (2 * DEPTH) ** 0.25
ADAM_LR, ADAM_B1, ADAM_B2, ADAM_EPS, ADAM_WD, ADAM_STEP = 0.001, 0.9, 0.999, 1e-08, 0.01, 10

VMEM_CAP = 60 * 1024 * 1024
SUBLANES = 8
LANES = 128

_IN_MAIN = ((0, 8192), (8256, 12352), (12368, 15440))
_IN_MAIN_WIDTHS = (1024, 1024, 2048, 2048, 1024, 1024, 1024, 1024, 1024, 1024, 1024, 1024, 1024)
_IN_SMALL = ((8192, 8256), (12352, 12368))
_DT_LANE = 0
_MG_LANE = 64

_WEIGHTS = ['c_ctx', 'w_ada', 'b_ada', 'w_in', 'lru_conv_w', 'lru_conv_b', 'lru_w_r', 'lru_b_r', 'lru_w_i', 'lru_b_i',
            'lru_lambda', 'ssd_conv_w', 'ssd_conv_b', 'ssd_dt_bias', 'ssd_a_log', 'ssd_d', 'ssd_norm_w', 'ml_conv_w',
            'ml_conv_b', 'ml_gate_b', 'ml_norm_w', 'w_br_a', 'w_br_b', 'w_br_c', 'w_out', 'b_out', 'ln1_g', 'ln1_b',
            'w_ff1', 'b_ff1', 'w_ff2', 'b_ff2', 'ln2_g', 'ln2_b']
_BIG = {'w_ada': 2, 'w_in': 2, 'w_ff1': 2, 'w_br_a': 1, 'w_br_b': 1, 'w_br_c': 1, 'w_out': 1, 'w_ff2': 1}
_SMALL_SHARDED = ['lru_conv_w', 'lru_b_r', 'lru_b_i', 'lru_lambda', 'ssd_conv_w', 'ml_conv_w']
_REPLICATED = [n for n in _WEIGHTS if n not in _BIG and n not in _SMALL_SHARDED]


def _params(vmem_bytes):
    return pltpu.CompilerParams(vmem_limit_bytes=int(min(max(2 * vmem_bytes, 32 << 20), VMEM_CAP)))


def _row_tile(n_rows, bytes_per_row, budget=6 << 20, cap=512):
    t = cap
    while t > SUBLANES and (t * bytes_per_row > budget or n_rows % t):
        t //= 2
    assert n_rows % t == 0, (n_rows, t)
    return t


def _dg(a, b, ca, cb):
    return lax.dot_general(a.astype(BF16), b.astype(BF16), (((ca,), (cb,)), ((), ())), preferred_element_type=F32)


def _make_bdot(ca, cb):
    @jax.custom_vjp
    def f(a, b):
        return _dg(a, b, ca, cb)

    def fwd(a, b):
        return _dg(a, b, ca, cb), (a, b)

    def bwd(res, g):
        a, b = res
        da = _dg(g, b, 1, 1 - cb) if ca == 1 else _dg(b, g, 1 - cb, 1)
        db = _dg(a, g, 1 - ca, 0) if cb == 0 else _dg(g, a, 0, 1 - ca)
        return da, db

    f.defvjp(fwd, bwd)
    return f


_mm_nn = _make_bdot(1, 0)
_mm_nt = _make_bdot(1, 1)
_mm_tn = _make_bdot(0, 0)


@jax.custom_vjp
def _round_bf16(x):
    return x.astype(BF16).astype(F32)


_round_bf16.defvjp(lambda x: (_round_bf16(x), None), lambda _, g: (g,))


def _exact_dot(a, b):
    return jnp.dot(a, b, precision=lax.Precision.HIGHEST, preferred_element_type=F32)


def _layernorm_rows(x):
    mu = jnp.mean(x, -1, keepdims=True)
    var = jnp.mean(jnp.square(x - mu), -1, keepdims=True)
    return (x - mu) * lax.rsqrt(var + LN_EPS)


def _rowwise_calls(name, f, rows, params, out_widths, out_dtype=F32, to_linear=(), tile_cap=512):
    drow_dtypes = [BF16 if i in to_linear else F32 for i in range(len(rows))]
    nr, npar, no = len(rows), len(params), len(out_widths)
    n_rows = rows[0].shape[0]
    row_w = [r.shape[1] for r in rows]
    par_bytes = sum(int(np.prod(p.shape)) * 4 for p in params)
    tile = _row_tile(n_rows, 4 * (2 * sum(row_w) + 2 * sum(out_widths)), budget=16 << 20, cap=tile_cap)
    grid = (n_rows // tile,)

    def row_spec(w):
        return pl.BlockSpec((tile, w), lambda i: (i, 0))

    def par_spec(p):
        return pl.BlockSpec(p.shape, lambda i: (0, 0))

    vmem = 2 * tile * 4 * (2 * sum(row_w) + 3 * sum(out_widths)) + 4 * par_bytes

    def fwd_call(rows, params):
        def kern(*refs):
            outs = f(*[r[...] for r in refs[:nr + npar]])
            for r, o in zip(refs[nr + npar:], outs):
                r[...] = o.astype(out_dtype)

        return pl.pallas_call(
            kern, grid=grid, name=name + "_fwd",
            in_specs=[row_spec(w) for w in row_w] + [par_spec(p) for p in params],
            out_specs=[row_spec(w) for w in out_widths],
            out_shape=[jax.ShapeDtypeStruct((n_rows, w), out_dtype) for w in out_widths],
            compiler_params=_params(vmem),
        )(*rows, *params)

    def bwd_call(rows, params, gouts):
        def kern(*refs):
            ins = [r[...] for r in refs[:nr + npar]]
            gs = tuple(r[...] for r in refs[nr + npar:nr + npar + no])
            grads = jax.vjp(f, *ins)[1](gs)
            drefs = refs[nr + npar + no:]
            for k in range(nr):
                drefs[k][...] = grads[k].astype(drefs[k].dtype)

            @pl.when(pl.program_id(0) == 0)
            def _():
                for k in range(npar):
                    drefs[nr + k][...] = jnp.zeros_like(drefs[nr + k])

            for k in range(npar):
                drefs[nr + k][...] += grads[nr + k]

        res = pl.pallas_call(
            kern, grid=grid, name=name + "_bwd",
            in_specs=[row_spec(w) for w in row_w] + [par_spec(p) for p in params] + [row_spec(w) for w in out_widths],
            out_specs=[row_spec(w) for w in row_w] + [par_spec(p) for p in params],
            out_shape=[jax.ShapeDtypeStruct(r.shape, dt) for r, dt in zip(rows, drow_dtypes)]
            + [jax.ShapeDtypeStruct(p.shape, F32) for p in params],
            compiler_params=_params(vmem),
        )(*rows, *params, *gouts)
        return tuple(r.astype(F32) for r in res[:nr]), tuple(res[nr:])

    return fwd_call, bwd_call


def _rowwise(name, f, rows, params, out_widths, to_linear=(), tile_cap=512):
    rows, params = tuple(rows), tuple(params)
    fwd_call, bwd_call = _rowwise_calls(name, f, rows, params, out_widths, to_linear=to_linear, tile_cap=tile_cap)

    @jax.custom_vjp
    def op(rows, params):
        return tuple(fwd_call(rows, params))

    op.defvjp(lambda r, p: (tuple(fwd_call(r, p)), (r, p)), lambda res, g: bwd_call(res[0], res[1], g))
    return op(rows, params)


def _rowwise_linear(name, f, rows, params, weights, to_linear=()):
    rows, params = tuple(rows), tuple(params)
    ws = tuple(w for w, _ in weights)
    m, k = rows[0].shape[0], ws[0].shape[0]
    row_fwd, row_bwd = _rowwise_calls(name, f, rows, params, [k], BF16, to_linear)
    lin = [_linear_calls(name + "lin%d" % i, m, k, w.shape[1], wd, BF16) for i, (w, wd) in enumerate(weights)]
    counts = [len(c[3]) for c in lin]

    def fwd(rows, params, ws):
        (a,) = row_fwd(rows, params)
        outs = []
        for (fwd_call, _, _, _), w in zip(lin, ws):
            outs += list(fwd_call(a, w))
        return tuple(outs), a

    @jax.custom_vjp
    def op(rows, params, ws):
        return fwd(rows, params, ws)[0]

    def op_fwd(rows, params, ws):
        outs, a = fwd(rows, params, ws)
        return outs, (rows, params, ws, a)

    def op_bwd(res, g):
        rows, params, ws, a = res
        da, dws, off = None, [], 0
        for (_, dgrad_call, wgrad_call, _), w, cnt in zip(lin, ws, counts):
            gk = g[off:off + cnt]
            off += cnt
            d = dgrad_call(w, gk)
            da = d if da is None else da + d
            dws.append(wgrad_call(a, gk))
        drows, dparams = row_bwd(rows, params, (da,))
        return drows, dparams, tuple(dws)

    op.defvjp(op_fwd, op_bwd)
    return op(rows, params, ws)


def _group_ranges(widths, tn):
    starts, s = [], 0
    for w in widths:
        assert w % tn == 0, (w, tn)
        starts.append((s // tn, (s + w) // tn))
        s += w
    return starts, s // tn


def _group_tile(refs, ranges, row_tile, col_tile, i, j):
    out = []
    for ref, (s, e) in zip(refs, ranges):
        cols = pl.ds(pl.multiple_of((j - s) * col_tile, col_tile), col_tile)
        out.append(((j >= s) & (j < e), ref, cols))
    return [(p, lambda r=r, c=c: r.at[pl.ds(pl.multiple_of(i * row_tile, row_tile), row_tile), c]) for p, r, c in out]


def _linear_calls(name, m, k, n, widths, a_dtype):
    widths = (n,) if widths is None else tuple(widths)
    ng = len(widths)
    cast_a = a_dtype != BF16
    wide = all(wd % 1024 == 0 for wd in widths)
    tn = 128 if n < 256 else (1024 if wide else (256 if k > 2048 or n % 512 else 512))
    tm = _row_tile(m, 0, cap=1024 if k <= 2048 else 512)
    ranges, nt = _group_ranges(widths, tn)
    mt = m // tm
    tn_w = 1024 if (wide and k <= 1024) else (512 if all(wd % 512 == 0 for wd in widths) else min(tn, 256))
    tm_w = _row_tile(m, 0, cap=1024 if k <= 1024 else 512)
    ranges_w, nt_w = _group_ranges(widths, tn_w)
    mt_w = m // tm_w
    hbm = pl.BlockSpec(memory_space=pl.ANY)

    def fwd_call(a, w):
        n_steps = mt * nt

        def kern(a_ref, w_ref, *rest):
            outs, obuf, osem = rest[:ng], rest[ng], rest[ng + 1]
            a_bf = rest[ng + 2] if cast_a else a_ref
            i, j = pl.program_id(0), pl.program_id(1)
            step = i * nt + j
            slot = lax.rem(step, 2)

            def drain(sl):
                pltpu.make_async_copy(obuf.at[sl], outs[0].at[pl.ds(0, tm), pl.ds(0, tn)], osem.at[sl]).wait()

            if cast_a:
                @pl.when(j == 0)
                def _():
                    a_bf[...] = a_ref[...].astype(BF16)

            @pl.when(step >= 2)
            def _():
                drain(slot)

            obuf[slot] = jnp.dot(a_bf[...], w_ref[...], preferred_element_type=F32)
            for pred, window in _group_tile(outs, ranges, tm, tn, i, j):
                @pl.when(pred)
                def _(window=window):
                    pltpu.make_async_copy(obuf.at[slot], window(), osem.at[slot]).start()

            @pl.when(step == n_steps - 1)
            def _():
                drain(slot)
                if n_steps > 1:
                    drain(1 - slot)

        return pl.pallas_call(
            kern, grid=(mt, nt), name=name + "_fwd",
            in_specs=[pl.BlockSpec((tm, k), lambda i, j: (i, 0)), pl.BlockSpec((k, tn), lambda i, j: (0, j))],
            out_specs=[hbm] * ng,
            out_shape=[jax.ShapeDtypeStruct((m, wd), F32) for wd in widths],
            scratch_shapes=[pltpu.VMEM((2, tm, tn), F32), pltpu.SemaphoreType.DMA((2,))]
            + ([pltpu.VMEM((tm, k), BF16)] if cast_a else []),
            compiler_params=_params(10 * tm * k + 4 * k * tn + 8 * tm * tn),
        )(a, w.astype(BF16))

    def prefetched(gs, gbuf, gsem, rngs, row_tile, col_tile, step, n_steps, tile_of):
        slot = lax.rem(step, 2)

        def start(s_idx, sl):
            ii, jj = tile_of(s_idx)
            for pred, window in _group_tile(gs, rngs, row_tile, col_tile, ii, jj):
                @pl.when(pred)
                def _(window=window):
                    pltpu.make_async_copy(window(), gbuf.at[sl], gsem.at[sl]).start()

        @pl.when(step == 0)
        def _():
            start(step, slot)

        @pl.when(step + 1 < n_steps)
        def _():
            start(step + 1, 1 - slot)

        pltpu.make_async_copy(gs[0].at[pl.ds(0, row_tile), pl.ds(0, col_tile)], gbuf.at[slot], gsem.at[slot]).wait()
        return slot

    def dgrad_call(w, gouts):
        def kern(w_ref, *rest):
            gs, da, gbuf, gsem = rest[:ng], rest[ng], rest[ng + 1], rest[ng + 2]
            i, j = pl.program_id(0), pl.program_id(1)
            slot = prefetched(gs, gbuf, gsem, ranges, tm, tn, i * nt + j, mt * nt, lambda s: (s // nt, lax.rem(s, nt)))

            @pl.when(j == 0)
            def _():
                da[...] = jnp.zeros_like(da)

            da[...] += lax.dot_general(gbuf[slot].astype(BF16), w_ref[...], (((1,), (1,)), ((), ())), preferred_element_type=F32)

        return pl.pallas_call(
            kern, grid=(mt, nt), name=name + "_dgrad",
            in_specs=[pl.BlockSpec((k, tn), lambda i, j: (0, j))] + [hbm] * ng,
            out_specs=pl.BlockSpec((tm, k), lambda i, j: (i, 0)),
            out_shape=jax.ShapeDtypeStruct((m, k), F32),
            scratch_shapes=[pltpu.VMEM((2, tm, tn), BF16), pltpu.SemaphoreType.DMA((2,))],
            compiler_params=_params(12 * tm * k + 4 * k * tn + 10 * tm * tn),
        )(w.astype(BF16), *[g.astype(BF16) for g in gouts])

    def wgrad_call(a, gouts):
        def kern(a_ref, *rest):
            gs, dw, gbuf, gsem = rest[:ng], rest[ng], rest[ng + 1], rest[ng + 2]
            j, i = pl.program_id(0), pl.program_id(1)
            slot = prefetched(gs, gbuf, gsem, ranges_w, tm_w, tn_w, j * mt_w + i, mt_w * nt_w,
                              lambda s: (lax.rem(s, mt_w), s // mt_w))

            @pl.when(i == 0)
            def _():
                dw[...] = jnp.zeros_like(dw)

            dw[...] += lax.dot_general(a_ref[...].astype(BF16), gbuf[slot].astype(BF16), (((0,), (0,)), ((), ())),
                                       preferred_element_type=F32)

        return pl.pallas_call(
            kern, grid=(nt_w, mt_w), name=name + "_wgrad",
            in_specs=[pl.BlockSpec((tm_w, k), lambda j, i: (i, 0))] + [hbm] * ng,
            out_specs=pl.BlockSpec((k, tn_w), lambda j, i: (0, j)),
            out_shape=jax.ShapeDtypeStruct((k, n), F32),
            scratch_shapes=[pltpu.VMEM((2, tm_w, tn_w), BF16), pltpu.SemaphoreType.DMA((2,))],
            compiler_params=_params(12 * tm_w * k + 12 * k * tn_w + 10 * tm_w * tn_w),
        )(a, *[g.astype(BF16) for g in gouts])

    return fwd_call, dgrad_call, wgrad_call, widths


def _linear(name, a, w, widths=None):
    fwd_call, dgrad_call, wgrad_call, _ = _linear_calls(name, a.shape[0], a.shape[1], w.shape[1], widths, a.dtype)

    @jax.custom_vjp
    def op(a, w):
        return tuple(fwd_call(a, w))

    op.defvjp(lambda a, w: (tuple(fwd_call(a, w)), (a, w)),
              lambda res, g: (dgrad_call(res[1], g), wgrad_call(res[0], g)))
    out = op(a, w)
    return out[0] if widths is None else out


def _conv_taps(x_ext, w, n_ext):
    xm2 = pltpu.roll(x_ext, 2, 0)
    xm1 = pltpu.roll(x_ext, 1, 0)
    xp1 = pltpu.roll(x_ext, n_ext - 1, 0)
    return xm2, xm1, xp1


def _dwconv(name, x, w, b, act):
    n_rows, ch = x.shape
    tt = _row_tile(n_rows, 4 * 6 * ch, budget=12 << 20, cap=256)
    nt = n_rows // tt
    n_ext = tt + 2 * SUBLANES
    per8 = tt // SUBLANES
    last8 = n_rows // SUBLANES - 1
    main = pl.BlockSpec((tt, ch), lambda i: (i, 0))
    prev = pl.BlockSpec((SUBLANES, ch), lambda i: (jnp.maximum(i * per8 - 1, 0), 0))
    nxt = pl.BlockSpec((SUBLANES, ch), lambda i: (jnp.minimum((i + 1) * per8, last8), 0))
    wspec = pl.BlockSpec((4, ch), lambda i: (0, 0))
    bspec = pl.BlockSpec((1, ch), lambda i: (0, 0))
    vmem = 4 * n_ext * ch * 14

    def ext(main_ref, prev_ref, next_ref):
        i = pl.program_id(0)
        p = jnp.where(i > 0, prev_ref[...], 0.0)
        q = jnp.where(i < nt - 1, next_ref[...], 0.0)
        return jnp.concatenate([p, main_ref[...], q], axis=0)

    def pre_of(x_ext, wv, bv):
        xm2, xm1, xp1 = _conv_taps(x_ext, wv, n_ext)
        pre = wv[0:1] * xm2 + wv[1:2] * xm1 + wv[2:3] * x_ext + wv[3:4] * xp1 + bv
        return pre, (xm2, xm1, xp1)

    def fwd_call(x, w, b):
        def kern(xm, xp, xn, w_ref, b_ref, o_ref):
            pre, _ = pre_of(ext(xm, xp, xn), w_ref[...], b_ref[...])
            pre = pre[SUBLANES:SUBLANES + tt]
            o_ref[...] = pre * jax.nn.sigmoid(pre) if act else pre

        return pl.pallas_call(
            kern, grid=(nt,), name=name + "_fwd", in_specs=[main, prev, nxt, wspec, bspec], out_specs=main,
            out_shape=jax.ShapeDtypeStruct((n_rows, ch), F32), compiler_params=_params(vmem),
        )(x, x, x, w, b)

    def bwd_call(x, w, b, dy):
        def kern(xm, xp, xn, gm, gp, gn, w_ref, b_ref, dx_ref, dw_ref, db_ref):
            wv = w_ref[...]
            x_ext = ext(xm, xp, xn)
            pre, (xm2, xm1, xp1) = pre_of(x_ext, wv, b_ref[...])
            dpre = ext(gm, gp, gn)
            if act:
                sg = jax.nn.sigmoid(pre)
                dpre = dpre * (sg + pre * sg * (1.0 - sg))
            dx = (wv[0:1] * pltpu.roll(dpre, n_ext - 2, 0) + wv[1:2] * pltpu.roll(dpre, n_ext - 1, 0)
                  + wv[2:3] * dpre + wv[3:4] * pltpu.roll(dpre, 1, 0))
            sl = slice(SUBLANES, SUBLANES + tt)
            dx_ref[...] = dx[sl].astype(dx_ref.dtype)
            dm = dpre[sl]

            @pl.when(pl.program_id(0) == 0)
            def _():
                dw_ref[...] = jnp.zeros_like(dw_ref)
                db_ref[...] = jnp.zeros_like(db_ref)

            dw_ref[...] += jnp.concatenate(
                [jnp.sum(dm * t[sl], axis=0, keepdims=True) for t in (xm2, xm1, x_ext, xp1)], axis=0)
            db_ref[...] += jnp.sum(dm, axis=0, keepdims=True)

        return pl.pallas_call(
            kern, grid=(nt,), name=name + "_bwd", in_specs=[main, prev, nxt, main, prev, nxt, wspec, bspec],
            out_specs=[main, wspec, bspec],
            out_shape=[jax.ShapeDtypeStruct((n_rows, ch), BF16), jax.ShapeDtypeStruct((4, ch), F32),
                       jax.ShapeDtypeStruct((1, ch), F32)],
            compiler_params=_params(vmem),
        )(x, x, x, dy, dy, dy, w, b)

    @jax.custom_vjp
    def op(x, w, b):
        return fwd_call(x, w, b)

    def op_bwd(res, g):
        dx, dw, db = bwd_call(*res, g)
        return dx.astype(F32), dw, db

    op.defvjp(lambda x, w, b: (fwd_call(x, w, b), (x, w, b)), op_bwd)
    return op(x, w, b)


def _scan_groups(tt, ch, reverse, load, store, carry_ref):
    row = lax.broadcasted_iota(jnp.int32, (SUBLANES, ch), 0)
    ng = tt // SUBLANES

    def body(k, carry):
        g = (ng - 1 - k) if reverse else k
        sl = pl.ds(pl.multiple_of(g * SUBLANES, SUBLANES), SUBLANES)
        a, b, extra = load(sl)
        for s in (1, 2, 4):
            if reverse:
                a_sh, b_sh, valid = pltpu.roll(a, SUBLANES - s, 0), pltpu.roll(b, SUBLANES - s, 0), row < SUBLANES - s
            else:
                a_sh, b_sh, valid = pltpu.roll(a, s, 0), pltpu.roll(b, s, 0), row >= s
            b = jnp.where(valid, b + a * b_sh, b)
            a = jnp.where(valid, a * a_sh, a)
        h = b + a * carry
        if reverse:
            h_prev = jnp.where(row == SUBLANES - 1, carry, pltpu.roll(h, SUBLANES - 1, 0))
            last = h[0:1]
        else:
            h_prev = jnp.where(row == 0, carry, pltpu.roll(h, 1, 0))
            last = h[SUBLANES - 1:SUBLANES]
        store(sl, h, h_prev, extra)
        return jnp.broadcast_to(last, (SUBLANES, ch))

    carry_ref[...] = lax.fori_loop(0, ng, body, carry_ref[...])


def _lin_scan(name, a, b, h0, reverse):
    n_rows, ch = a.shape
    tt = _row_tile(n_rows, 0, cap=256)
    nt = n_rows // tt
    vmem = 2 * 4 * tt * ch * 5

    def tile_spec(rev):
        return pl.BlockSpec((tt, ch), (lambda i: (nt - 1 - i, 0)) if rev else (lambda i: (i, 0)))

    vec = pl.BlockSpec((1, ch), lambda i: (0, 0))

    def fwd_call(a, b, h0):
        def kern(a_ref, b_ref, h0_ref, h_ref, hp_ref, last_ref, carry):
            @pl.when(pl.program_id(0) == 0)
            def _():
                carry[...] = jnp.broadcast_to(h0_ref[...], carry.shape)

            def load(sl):
                return a_ref[sl, :], b_ref[sl, :], None

            def store(sl, h, h_prev, _):
                h_ref[sl, :] = h
                hp_ref[sl, :] = h_prev

            _scan_groups(tt, ch, reverse, load, store, carry)
            last_ref[...] = carry[0:1]

        return pl.pallas_call(
            kern, grid=(nt,), name=name + "_fwd", in_specs=[tile_spec(reverse), tile_spec(reverse), vec],
            out_specs=[tile_spec(reverse), tile_spec(reverse), vec],
            out_shape=[jax.ShapeDtypeStruct((n_rows, ch), F32)] * 2 + [jax.ShapeDtypeStruct((1, ch), F32)],
            scratch_shapes=[pltpu.VMEM((SUBLANES, ch), F32)], compiler_params=_params(vmem),
        )(a, b, h0)

    def bwd_call(a, h_prev, dh, dlast):
        rev = not reverse

        def kern(a_ref, hp_ref, dh_ref, dl_ref, da_ref, db_ref, d0_ref, carry):
            @pl.when(pl.program_id(0) == 0)
            def _():
                carry[...] = jnp.broadcast_to(dl_ref[...], carry.shape)

            def load(sl):
                av, dv = a_ref[sl, :], dh_ref[sl, :]
                return av, av * dv, dv

            def store(sl, u, u_next, dv):
                g = dv + u_next
                db_ref[sl, :] = g
                da_ref[sl, :] = g * hp_ref[sl, :]

            _scan_groups(tt, ch, rev, load, store, carry)
            d0_ref[...] = carry[0:1]

        return pl.pallas_call(
            kern, grid=(nt,), name=name + "_bwd", in_specs=[tile_spec(rev)] * 3 + [vec],
            out_specs=[tile_spec(rev), tile_spec(rev), vec],
            out_shape=[jax.ShapeDtypeStruct((n_rows, ch), F32)] * 2 + [jax.ShapeDtypeStruct((1, ch), F32)],
            scratch_shapes=[pltpu.VMEM((SUBLANES, ch), F32)], compiler_params=_params(vmem),
        )(a, h_prev, dh, dlast)

    @jax.custom_vjp
    def op(a, b, h0):
        h, _, last = fwd_call(a, b, h0)
        return h, last

    def op_fwd(a, b, h0):
        h, h_prev, last = fwd_call(a, b, h0)
        return (h, last), (a, h_prev)

    def op_bwd(res, g):
        da, db, d0 = bwd_call(res[0], res[1], g[0], g[1])
        return da, db, d0

    op.defvjp(op_fwd, op_bwd)
    return op(a, b, h0)


def _tri(reverse):
    q = lax.broadcasted_iota(jnp.int32, (CHUNK, CHUNK), 0)
    s = lax.broadcasted_iota(jnp.int32, (CHUNK, CHUNK), 1)
    return (q <= s) if reverse else (q >= s)


def _pick_col(x, lane):
    idx = lax.broadcasted_iota(jnp.int32, x.shape, 1)
    return jnp.sum(jnp.where(idx == lane, x, 0.0), axis=1, keepdims=True)


def _pick_row(x, row):
    idx = lax.broadcasted_iota(jnp.int32, x.shape, 0)
    return jnp.sum(jnp.where(idx == row, x, 0.0), axis=0, keepdims=True)


def _ssd_shared(small, bias_row, alog_row, reverse):
    delta_all = jax.nn.softplus(small + bias_row)
    acs_all = _exact_dot(_tri(reverse).astype(F32), delta_all * (-jnp.exp(alog_row)))
    return delta_all, acs_all, acs_all.T


def _ssd_group(xs, bm, cm, state, delta_all, acs_all, acs_t, g, direction, reverse):
    mask = _tri(reverse)
    last = 0 if reverse else CHUNK - 1
    hd = SSD_HEADDIM
    rowi = lax.broadcasted_iota(jnp.int32, (CHUNK, 1), 0)
    a_cols, a_rows, deltas, tots = [], [], [], []
    for r in range(SSD_HPG):
        lane = _DT_LANE + 32 * direction + SSD_HPG * g + r
        a_col = _pick_col(acs_all, lane)
        a_cols.append(a_col)
        a_rows.append(_pick_row(acs_t, lane))
        deltas.append(_pick_col(delta_all, lane))
        tots.append(jnp.sum(jnp.where(rowi == last, a_col, 0.0), axis=0, keepdims=True))

    def wide(cols, rows):
        return jnp.concatenate([jnp.broadcast_to(c, (rows, hd)) for c in cols], axis=1)

    a_w = wide(a_cols, CHUNK)
    x_w = xs * wide(deltas, CHUNK)
    st = _mm_tn(x_w * jnp.exp(wide(tots, 1) - a_w), bm)
    y_off = _mm_nt(cm, state) * jnp.exp(a_w)
    grow = jnp.concatenate([jnp.broadcast_to(jnp.exp(t), (hd, 1)) for t in tots], axis=0)
    cb = _mm_nt(cm, bm)
    m_cat = jnp.concatenate([cb * jnp.exp(jnp.where(mask, a_cols[r] - a_rows[r], -jnp.inf)) for r in range(SSD_HPG)], axis=1)
    lane_head = lax.broadcasted_iota(jnp.int32, (1, SSD_HPG * hd), 1) // hd
    x_bd = jnp.concatenate([jnp.where(lane_head == r, x_w, 0.0) for r in range(SSD_HPG)], axis=0)
    return _mm_nn(m_cat, x_bd) + y_off, grow * state + st


def _ssd_calls(name, xs, bm, cm, s0, direction, reverse):
    n_rows = xs.shape[0]
    nc = n_rows // CHUNK
    gw = SSD_HPG * SSD_HEADDIM
    vmem = 4 * CHUNK * (gw + 3 * 128) * 8 + 4 * gw * 128 * 12 + (8 << 20)

    n_state = SSD_GROUPS * gw
    shared_scratch = [pltpu.VMEM((CHUNK, LANES), F32), pltpu.VMEM((CHUNK, LANES), F32), pltpu.VMEM((LANES, CHUNK), F32)]

    def specs(order, gps=1):
        def cidx(c):
            return (nc - 1 - c) if order else c

        return dict(
            xs=pl.BlockSpec((CHUNK, gps * gw), lambda c, g: (cidx(c), g)),
            bc=pl.BlockSpec((CHUNK, gps * SSD_STATE), lambda c, g: (cidx(c), g)),
            small=pl.BlockSpec((CHUNK, LANES), lambda c, g: (cidx(c), 0)),
            row=pl.BlockSpec((1, LANES), lambda c, g: (0, 0)),
            state=pl.BlockSpec((n_state, SSD_STATE), lambda c, g: (0, 0)),
            enter=pl.BlockSpec((1, gps * gw, SSD_STATE), lambda c, g: (cidx(c), g, 0)),
        )

    gps_fwd, gps_bwd = SSD_GROUPS, 1

    def group_rows(g, gps):
        return pl.ds(pl.multiple_of(g * gps * gw, gps * gw), gps * gw)

    def step_fn(g, gps):
        def fn(xs_v, bm_v, cm_v, st_v, d_all, a_all, a_t):
            ys, sts = [], []
            for u in range(gps):
                y_u, s_u = _ssd_group(xs_v[:, u * gw:(u + 1) * gw], bm_v[:, u * SSD_STATE:(u + 1) * SSD_STATE],
                                      cm_v[:, u * SSD_STATE:(u + 1) * SSD_STATE], st_v[u * gw:(u + 1) * gw],
                                      d_all, a_all, a_t, gps * g + u, direction, reverse)
                ys.append(y_u)
                sts.append(s_u)
            return jnp.concatenate(ys, axis=1), jnp.concatenate(sts, axis=0)

        return fn

    def fwd_call(xs, bm, cm, small, bias_row, alog_row, s0):
        gps = gps_fwd
        sp = specs(reverse, gps)

        def kern(xs_r, bm_r, cm_r, sm_r, br_r, ar_r, s0_r, y_r, sf_r, se_r, st, sh_d, sh_a, sh_t):
            c, g = pl.program_id(0), pl.program_id(1)

            @pl.when((c == 0) & (g == 0))
            def _():
                st[...] = s0_r[...]

            @pl.when(g == 0)
            def _():
                sh_d[...], sh_a[...], sh_t[...] = _ssd_shared(sm_r[...], br_r[...], ar_r[...], reverse)

            rows = group_rows(g, gps)
            s_in = st[rows, :]
            se_r[0] = s_in
            y_r[...], s_new = step_fn(g, gps)(xs_r[...], bm_r[...], cm_r[...], s_in, sh_d[...], sh_a[...], sh_t[...])
            st[rows, :] = s_new
            sf_r[rows, :] = s_new

        return pl.pallas_call(
            kern, grid=(nc, SSD_GROUPS // gps), name=name + "_fwd",
            in_specs=[sp['xs'], sp['bc'], sp['bc'], sp['small'], sp['row'], sp['row'], sp['state']],
            out_specs=[sp['xs'], sp['state'], sp['enter']],
            out_shape=[jax.ShapeDtypeStruct((n_rows, SSD_INNER), F32), jax.ShapeDtypeStruct((n_state, SSD_STATE), F32),
                       jax.ShapeDtypeStruct((nc, n_state, SSD_STATE), F32)],
            scratch_shapes=[pltpu.VMEM((n_state, SSD_STATE), F32)] + shared_scratch, compiler_params=_params(vmem),
        )(xs, bm, cm, small, bias_row, alog_row, s0)

    def bwd_call(xs, bm, cm, small, bias_row, alog_row, enter, dy, dsf, acc=()):
        gps = gps_bwd
        sp = specs(not reverse, gps)

        def kern(*refs):
            xs_r, bm_r, cm_r, sm_r, br_r, ar_r, se_r, dy_r, dsf_r = refs[:9]
            acc_r = refs[9:9 + len(acc)]
            dxs_r, dbm_r, dcm_r, dsm_r, dbr_r, dar_r, ds0_r, ds, sh_d, sh_a, sh_t, gd, ga, gt = refs[9 + len(acc):]
            c, g = pl.program_id(0), pl.program_id(1)

            @pl.when((c == 0) & (g == 0))
            def _():
                ds[...] = dsf_r[...]
                dbr_r[...] = jnp.zeros_like(dbr_r)
                dar_r[...] = jnp.zeros_like(dar_r)

            @pl.when(g == 0)
            def _():
                sh_d[...], sh_a[...], sh_t[...] = _ssd_shared(sm_r[...], br_r[...], ar_r[...], reverse)
                gd[...] = jnp.zeros_like(gd)
                ga[...] = jnp.zeros_like(ga)
                gt[...] = jnp.zeros_like(gt)

            rows = group_rows(g, gps)
            _, vjp = jax.vjp(step_fn(g, gps), xs_r[...], bm_r[...], cm_r[...], se_r[0], sh_d[...], sh_a[...], sh_t[...])
            dxs, dbm, dcm, ds_in, dd, da, dt = vjp((dy_r[...], ds[rows, :]))
            dxs_r[...] = dxs + acc_r[0][...] if acc else dxs
            dbm_r[...] = dbm + acc_r[1][...] if acc else dbm
            dcm_r[...] = dcm + acc_r[2][...] if acc else dcm
            ds[rows, :] = ds_in
            ds0_r[rows, :] = ds_in
            gd[...] += dd
            ga[...] += da
            gt[...] += dt

            @pl.when(g == SSD_GROUPS // gps - 1)
            def _():
                shared = functools.partial(_ssd_shared, reverse=reverse)
                dsm, dbr, dar = jax.vjp(shared, sm_r[...], br_r[...], ar_r[...])[1]((gd[...], ga[...], gt[...]))
                dsm_r[...] = dsm + acc_r[3][...] if acc else dsm
                dbr_r[...] += dbr
                dar_r[...] += dar

        return pl.pallas_call(
            kern, grid=(nc, SSD_GROUPS // gps), name=name + "_bwd",
            in_specs=[sp['xs'], sp['bc'], sp['bc'], sp['small'], sp['row'], sp['row'], sp['enter'], sp['xs'], sp['state']]
            + ([sp['xs'], sp['bc'], sp['bc'], sp['small']] if acc else []),
            out_specs=[sp['xs'], sp['bc'], sp['bc'], sp['small'], sp['row'], sp['row'], sp['state']],
            out_shape=[jax.ShapeDtypeStruct(xs.shape, F32), jax.ShapeDtypeStruct(bm.shape, F32),
                       jax.ShapeDtypeStruct(cm.shape, F32), jax.ShapeDtypeStruct((n_rows, LANES), F32),
                       jax.ShapeDtypeStruct((1, LANES), F32), jax.ShapeDtypeStruct((1, LANES), F32),
                       jax.ShapeDtypeStruct(s0.shape, F32)],
            scratch_shapes=[pltpu.VMEM((n_state, SSD_STATE), F32)] + shared_scratch + shared_scratch,
            compiler_params=_params(vmem),
        )(xs, bm, cm, small, bias_row, alog_row, enter, dy, dsf, *acc)

    return fwd_call, bwd_call


def _ssd_pair(name, xs, bm, cm, small, rows_f, rows_b, s0_f, s0_b):
    calls = [_ssd_calls(name + "%d" % d, xs, bm, cm, s0_f, d, d == 1) for d in range(2)]

    def run_fwd(xs, bm, cm, small, rows_f, rows_b, s0_f, s0_b):
        y_f, sf_f, en_f = calls[0][0](xs, bm, cm, small, *rows_f, s0_f)
        y_b, sf_b, en_b = calls[1][0](xs, bm, cm, small, *rows_b, s0_b)
        return (y_f, y_b, sf_f, sf_b), (en_f, en_b)

    @jax.custom_vjp
    def op(*args):
        return run_fwd(*args)[0]

    def op_fwd(*args):
        outs, enters = run_fwd(*args)
        return outs, (args[:6], enters)

    def op_bwd(res, g):
        (xs, bm, cm, small, rows_f, rows_b), (en_f, en_b) = res
        dy_f, dy_b, dsf_f, dsf_b = g
        dxs, dbm, dcm, dsm, dbr_f, dar_f, ds0_f = calls[0][1](xs, bm, cm, small, *rows_f, en_f, dy_f, dsf_f)
        dxs, dbm, dcm, dsm, dbr_b, dar_b, ds0_b = calls[1][1](xs, bm, cm, small, *rows_b, en_b, dy_b, dsf_b, acc=(dxs, dbm, dcm, dsm))
        return dxs, dbm, dcm, dsm, (dbr_f, dar_f), (dbr_b, dar_b), ds0_f, ds0_b

    op.defvjp(op_fwd, op_bwd)
    return op(xs, bm, cm, small, tuple(rows_f), tuple(rows_b), s0_f, s0_b)


def _ml_shared(small, gate_row, reverse):
    gates = small + gate_row
    b_all = _exact_dot(_tri(reverse).astype(F32), jax.nn.log_sigmoid(gates))
    return gates, b_all, gates.T, b_all.T


def _ml_head(q, k, v, c_st, n_st, m_st, gates, b_all, gates_t, b_t, h, direction, reverse):
    mask = _tri(reverse)
    last = 0 if reverse else CHUNK - 1
    lane_i = _MG_LANE + 8 * direction + h
    lane_f = lane_i + ML_HEADS
    b_col = _pick_col(b_all, lane_f)
    b_row = _pick_row(b_t, lane_f)
    li_col = _pick_col(gates, lane_i)
    li_row = _pick_row(gates_t, lane_i)
    rowi = lax.broadcasted_iota(jnp.int32, (CHUNK, 1), 0)
    g_tot = jnp.sum(jnp.where(rowi == last, b_col, 0.0), axis=0, keepdims=True)
    m_in = m_st[:, 0:1]
    q = q * (ML_HD ** -0.5)
    w = g_tot - b_col + li_col
    m_loc = lax.stop_gradient(jnp.max(w, axis=0, keepdims=True))
    kw = k * jnp.exp(w - m_loc)
    c_loc = _mm_tn(kw, v)
    n_loc = jnp.sum(kw, axis=0, keepdims=True)
    m_new = lax.stop_gradient(jnp.maximum(g_tot + m_in, m_loc))
    s_old = jnp.exp(g_tot + m_in - m_new)
    s_loc = jnp.exp(m_loc - m_new)
    c_new = s_old * c_st + s_loc * c_loc
    n_new = s_old * n_st + s_loc * n_loc
    log_d = jnp.where(mask, b_col - b_row + li_row, -jnp.inf)
    inter = b_col + m_in
    m_t = lax.stop_gradient(jnp.maximum(inter, jnp.max(log_d, axis=1, keepdims=True)))
    dmat = jnp.exp(log_d - m_t)
    wi = jnp.exp(inter - m_t)
    s = _mm_nt(q, k) * dmat
    num = _mm_nn(s, v) + wi * _mm_nn(q, c_st)
    den = jnp.sum(s, axis=1, keepdims=True) + wi * jnp.sum(_round_bf16(q) * _round_bf16(n_st), axis=1, keepdims=True)
    out = num / jnp.maximum(jnp.abs(den), jnp.exp(-m_t))
    return out, c_new, n_new, jnp.broadcast_to(m_new, (1, LANES))


def _ml_calls(name, q, direction, reverse):
    n_rows = q.shape[0]
    nc = n_rows // CHUNK
    vmem = 4 * CHUNK * (4 * ML_HD + 128) * 8 + 4 * ML_HD * ML_HD * 12 + (8 << 20)

    def specs(order, hps=1):
        def cidx(c):
            return (nc - 1 - c) if order else c

        return dict(
            qkv=pl.BlockSpec((CHUNK, hps * ML_HD), lambda c, h: (cidx(c), h)),
            small=pl.BlockSpec((CHUNK, LANES), lambda c, h: (cidx(c), 0)),
            row=pl.BlockSpec((1, LANES), lambda c, h: (0, 0)),
            c=pl.BlockSpec((ML_HEADS * ML_HD, ML_HD), lambda c, h: (0, 0)),
            n=pl.BlockSpec((ML_HEADS, 1, ML_HD), lambda c, h: (0, 0, 0)),
            m=pl.BlockSpec((ML_HEADS, 1, LANES), lambda c, h: (0, 0, 0)),
            ec=pl.BlockSpec((1, hps * ML_HD, ML_HD), lambda c, h: (cidx(c), h, 0)),
            en=pl.BlockSpec((1, hps, 1, ML_HD), lambda c, h: (cidx(c), h, 0, 0)),
            em=pl.BlockSpec((1, hps, 1, LANES), lambda c, h: (cidx(c), h, 0, 0)),
        )

    st_shapes = [jax.ShapeDtypeStruct((ML_HEADS * ML_HD, ML_HD), F32), jax.ShapeDtypeStruct((ML_HEADS, 1, ML_HD), F32),
                 jax.ShapeDtypeStruct((ML_HEADS, 1, LANES), F32)]
    scratch = [pltpu.VMEM((ML_HEADS * ML_HD, ML_HD), F32), pltpu.VMEM((ML_HEADS, 1, ML_HD), F32),
               pltpu.VMEM((ML_HEADS, 1, LANES), F32)]
    shared_scratch = [pltpu.VMEM((CHUNK, LANES), F32), pltpu.VMEM((CHUNK, LANES), F32),
                      pltpu.VMEM((LANES, CHUNK), F32), pltpu.VMEM((LANES, CHUNK), F32)]

    def head_rows(h):
        return pl.ds(pl.multiple_of(h * ML_HD, ML_HD), ML_HD)

    def fwd_call(q, k, v, small, gate_row, c0, n0, m0):
        hps = ML_HEADS
        sp = specs(reverse, hps)

        def kern(q_r, k_r, v_r, sm_r, gr_r, c0_r, n0_r, m0_r, o_r, cf_r, nf_r, mf_r, ec_r, en_r, em_r, cs, ns, ms, *sh):
            c, hh = pl.program_id(0), pl.program_id(1)

            @pl.when((c == 0) & (hh == 0))
            def _():
                cs[...] = c0_r[...]
                ns[...] = n0_r[...]
                ms[...] = m0_r[...]

            @pl.when(hh == 0)
            def _():
                for ref, val in zip(sh, _ml_shared(sm_r[...], gr_r[...], reverse)):
                    ref[...] = val

            q_v, k_v, v_v = q_r[...], k_r[...], v_r[...]
            shared = [r[...] for r in sh]
            heads = [hps * hh + u for u in range(hps)]
            states = [(cs[head_rows(h), :], ns[h], ms[h]) for h in heads]
            results = []
            for u, (h, (c_in, n_in, m_in)) in enumerate(zip(heads, states)):
                cols = slice(u * ML_HD, (u + 1) * ML_HD)
                ec_r[0, cols, :] = c_in
                en_r[0, u] = n_in
                em_r[0, u] = m_in
                results.append(_ml_head(q_v[:, cols], k_v[:, cols], v_v[:, cols], c_in, n_in, m_in, *shared,
                                        h, direction, reverse))
            o_r[...] = jnp.concatenate([r[0] for r in results], axis=1)
            for h, (_, c_new, n_new, m_new) in zip(heads, results):
                cs[head_rows(h), :] = c_new
                ns[h] = n_new
                ms[h] = m_new
                cf_r[head_rows(h), :] = c_new
                nf_r[h] = n_new
                mf_r[h] = m_new

        return pl.pallas_call(
            kern, grid=(nc, ML_HEADS // hps), name=name + "_fwd",
            in_specs=[sp['qkv']] * 3 + [sp['small'], sp['row'], sp['c'], sp['n'], sp['m']],
            out_specs=[sp['qkv'], sp['c'], sp['n'], sp['m'], sp['ec'], sp['en'], sp['em']],
            out_shape=[jax.ShapeDtypeStruct((n_rows, ML_HEADS * ML_HD), F32)] + st_shapes + [
                jax.ShapeDtypeStruct((nc, ML_HEADS * ML_HD, ML_HD), F32),
                jax.ShapeDtypeStruct((nc, ML_HEADS, 1, ML_HD), F32), jax.ShapeDtypeStruct((nc, ML_HEADS, 1, LANES), F32)],
            scratch_shapes=scratch + shared_scratch, compiler_params=_params(vmem),
        )(q, k, v, small, gate_row, c0, n0, m0)

    def bwd_call(q, k, v, small, gate_row, ec, en, em, do, dcf, dnf, dmf, acc=()):
        hps = 4
        sp = specs(not reverse, hps)
        n_sh = len(shared_scratch)

        def kern(*refs):
            q_r, k_r, v_r, sm_r, gr_r, ec_r, en_r, em_r, do_r, dcf_r, dnf_r, dmf_r = refs[:12]
            acc_r = refs[12:12 + len(acc)]
            dq_r, dk_r, dv_r, dsm_r, dgr_r, dc0_r, dn0_r, dm0_r, dcs, dns, dms = refs[12 + len(acc):23 + len(acc)]
            rest = refs[23 + len(acc):]
            sh, gsh = rest[:n_sh], rest[n_sh:]
            c, hh = pl.program_id(0), pl.program_id(1)

            @pl.when((c == 0) & (hh == 0))
            def _():
                dcs[...] = dcf_r[...]
                dns[...] = dnf_r[...]
                dms[...] = dmf_r[...]
                dgr_r[...] = jnp.zeros_like(dgr_r)

            @pl.when(hh == 0)
            def _():
                for ref, val in zip(sh, _ml_shared(sm_r[...], gr_r[...], reverse)):
                    ref[...] = val
                for ref in gsh:
                    ref[...] = jnp.zeros_like(ref)

            heads = [hps * hh + u for u in range(hps)]
            q_v, k_v, v_v, do_v = q_r[...], k_r[...], v_r[...], do_r[...]
            shared_vals = [r[...] for r in sh]
            cots = [(dcs[head_rows(h), :], dns[h], dms[h]) for h in heads]
            grads = []
            for u, h in enumerate(heads):
                cols = slice(u * ML_HD, (u + 1) * ML_HD)
                fn = functools.partial(_ml_head, h=h, direction=direction, reverse=reverse)
                _, vjp = jax.vjp(fn, q_v[:, cols], k_v[:, cols], v_v[:, cols], ec_r[0, cols, :], en_r[0, u], em_r[0, u], *shared_vals)
                grads.append(vjp((do_v[:, cols],) + cots[u]))
            dq, dk, dv = [jnp.concatenate([g[i] for g in grads], axis=1) for i in range(3)]
            dq_r[...] = dq + acc_r[0][...] if acc else dq
            dk_r[...] = dk + acc_r[1][...] if acc else dk
            dv_r[...] = dv + acc_r[2][...] if acc else dv
            for i, ref in enumerate(gsh):
                ref[...] += sum(g[6 + i] for g in grads)
            for h, g in zip(heads, grads):
                dc, dn, dm = g[3:6]
                dm = jnp.broadcast_to(jnp.sum(dm, axis=1, keepdims=True), (1, LANES)) * (1.0 / LANES)
                dcs[head_rows(h), :] = dc
                dns[h] = dn
                dms[h] = dm
                dc0_r[head_rows(h), :] = dc
                dn0_r[h] = dn
                dm0_r[h] = dm

            @pl.when(hh == ML_HEADS // hps - 1)
            def _():
                shared = functools.partial(_ml_shared, reverse=reverse)
                dsm, dgr = jax.vjp(shared, sm_r[...], gr_r[...])[1](tuple(r[...] for r in gsh))
                dsm_r[...] = dsm + acc_r[3][...] if acc else dsm
                dgr_r[...] += dgr

        return pl.pallas_call(
            kern, grid=(nc, ML_HEADS // hps), name=name + "_bwd",
            in_specs=[sp['qkv']] * 3 + [sp['small'], sp['row'], sp['ec'], sp['en'], sp['em'], sp['qkv'], sp['c'], sp['n'], sp['m']]
            + ([sp['qkv']] * 3 + [sp['small']] if acc else []),
            out_specs=[sp['qkv']] * 3 + [sp['small'], sp['row'], sp['c'], sp['n'], sp['m']],
            out_shape=[jax.ShapeDtypeStruct(q.shape, F32)] * 3 + [jax.ShapeDtypeStruct((n_rows, LANES), F32),
                                                                  jax.ShapeDtypeStruct((1, LANES), F32)] + st_shapes,
            scratch_shapes=scratch + shared_scratch + shared_scratch, compiler_params=_params(vmem),
        )(q, k, v, small, gate_row, ec, en, em, do, dcf, dnf, dmf, *acc)

    return fwd_call, bwd_call


def _ml_pair(name, q, k, v, small, gate_row, state_f, state_b):
    calls = [_ml_calls(name + "%d" % d, q, d, d == 1) for d in range(2)]

    def run_fwd(q, k, v, small, gate_row, state_f, state_b):
        res_f = calls[0][0](q, k, v, small, gate_row, *state_f)
        res_b = calls[1][0](q, k, v, small, gate_row, *state_b)
        return (res_f[0], res_b[0], tuple(res_f[1:4]), tuple(res_b[1:4])), (tuple(res_f[4:]), tuple(res_b[4:]))

    @jax.custom_vjp
    def op(*args):
        return run_fwd(*args)[0]

    def op_fwd(*args):
        outs, enters = run_fwd(*args)
        return outs, (args[:5], enters)

    def op_bwd(res, g):
        (q, k, v, small, gate_row), (en_f, en_b) = res
        do_f, do_b, ds_f, ds_b = g
        dq, dk, dv, dsm, dgr_f, *d0_f = calls[0][1](q, k, v, small, gate_row, *en_f, do_f, *ds_f)
        dq, dk, dv, dsm, dgr_b, *d0_b = calls[1][1](q, k, v, small, gate_row, *en_b, do_b, *ds_b, acc=(dq, dk, dv, dsm))
        return dq, dk, dv, dsm, dgr_f + dgr_b, tuple(d0_f), tuple(d0_b)

    op.defvjp(op_fwd, op_bwd)
    return op(q, k, v, small, gate_row, tuple(state_f), tuple(state_b))


def _f_modulate(x, shift, scale):
    return (_layernorm_rows(x) * (1.0 + scale) + shift,)


def _f_resid_ln(x, o, gate, bias, ln_g, ln_b):
    return (_layernorm_rows(DN_ALPHA * x + gate * (o + bias)) * ln_g + ln_b,)


def _f_lru_gates(xc, w_r, b_r, w_i, b_i, lam):
    outs = []
    for d in range(2):
        def blockdiag(w):
            return jnp.concatenate(
                [_mm_nn(xc[:, n * LRU_BS:(n + 1) * LRU_BS], w[(d * LRU_BLOCKS + n) * LRU_BS:(d * LRU_BLOCKS + n + 1) * LRU_BS, :])
                 for n in range(LRU_BLOCKS)], axis=1)

        r = jax.nn.sigmoid(blockdiag(w_r) + b_r[d:d + 1])
        i = jax.nn.sigmoid(blockdiag(w_i) + b_i[d:d + 1])
        log_a = -LRU_C * r * jax.nn.softplus(-lam[d:d + 1])
        outs += [jnp.exp(log_a), jnp.sqrt(1.0 - jnp.exp(2.0 * log_a)) * i * xc]
    return tuple(outs)


def _f_lru_out(h_f, h_b, ly):
    return ((h_f + h_b) * jax.nn.gelu(ly),)


def _f_ssd_post(y_f, y_b, xs, z, d_exp, norm_w):
    y = (y_f + y_b + xs * d_exp) * jax.nn.silu(z)
    gw = SSD_INNER // SSD_GROUPS
    parts = []
    for g in range(SSD_GROUPS):
        yg = y[:, g * gw:(g + 1) * gw]
        parts.append(yg * lax.rsqrt(jnp.mean(jnp.square(yg), -1, keepdims=True) + LN_EPS))
    return (jnp.concatenate(parts, axis=1) * norm_w,)


def _f_ml_post(h_f, h_b, o, norm_w):
    h = h_f + h_b
    parts = [_layernorm_rows(h[:, i * ML_HD:(i + 1) * ML_HD]) for i in range(ML_HEADS)]
    return (jnp.concatenate(parts, axis=1) * norm_w * jax.nn.sigmoid(o),)


def _f_merge(ga, gb, gc, pa, pb, pc):
    return (jax.nn.sigmoid(ga) * pa + jax.nn.sigmoid(gb) * pb + jax.nn.sigmoid(gc) * pc,)


def _f_relu2(pre, bias):
    return (jnp.square(jax.nn.relu(pre + bias)),)


def _lane_row(vec, start):
    return jnp.pad(vec[None], ((0, 0), (start, LANES - start - vec.shape[0])))


def _mixer(tag, x_tok, shift, scale, p, states):
    (lru_s, ssd_s, ml_s) = states
    lx, ly, sz, xs, bm, cm, mq, mk, mv, mo, ga, gb, gc, small = _rowwise_linear(
        tag + "in", _f_modulate, [x_tok], [shift, scale], [(p['w_in_main'], _IN_MAIN_WIDTHS), (p['w_in_small'], None)])

    xc = _dwconv(tag + "lruconv", lx, p['lru_conv_w'], p['lru_conv_b'][None], False)
    a_f, b_f, a_b, b_b = _rowwise(
        tag + "lrugate", _f_lru_gates, [xc],
        [p['lru_w_r'].reshape(2 * LRU_BLOCKS * LRU_BS, LRU_BS), p['lru_b_r'], p['lru_w_i'].reshape(2 * LRU_BLOCKS * LRU_BS, LRU_BS),
         p['lru_b_i'], p['lru_lambda']], [D_MODEL] * 4, tile_cap=128)
    h_f, s_f = _lin_scan(tag + "lruscanf", a_f, b_f, lru_s[0], False)
    h_b, s_b = _lin_scan(tag + "lruscanb", a_b, b_b, lru_s[1], True)
    (pa,) = _rowwise_linear(tag + "bra", _f_lru_out, [h_f, h_b, ly], [], [(p['w_br_a'], None)], to_linear=(2,))

    cw, cb_ = p['ssd_conv_w'], p['ssd_conv_b'][None]
    xs_c = _dwconv(tag + "ssdconvx", xs, cw[:, :2048], cb_[:, :2048], True)
    bm_c = _dwconv(tag + "ssdconvb", bm, cw[:, 2048:3072], cb_[:, 2048:3072], True)
    cm_c = _dwconv(tag + "ssdconvc", cm, cw[:, 3072:], cb_[:, 3072:], True)
    dir_rows = [(_lane_row(p['ssd_dt_bias'][d], _DT_LANE + 32 * d), _lane_row(p['ssd_a_log'][d], _DT_LANE + 32 * d)) for d in range(2)]
    *ys, st_f, st_b = _ssd_pair(tag + "ssd", xs_c, bm_c, cm_c, small, dir_rows[0], dir_rows[1], ssd_s[0], ssd_s[1])
    ssd_new = (st_f, st_b)
    (pb,) = _rowwise_linear(tag + "brb", _f_ssd_post, [ys[0], ys[1], xs_c, sz],
                            [jnp.repeat(p['ssd_d'], SSD_HEADDIM)[None], p['ssd_norm_w'][None]], [(p['w_br_b'], None)], to_linear=(3,))

    mw, mb = p['ml_conv_w'], p['ml_conv_b'][None]
    q_c = _dwconv(tag + "mlconvq", mq, mw[:, :1024], mb[:, :1024], True)
    k_c = _dwconv(tag + "mlconvk", mk, mw[:, 1024:], mb[:, 1024:], True)
    gate_row = _lane_row(p['ml_gate_b'].reshape(4 * ML_HEADS), _MG_LANE)
    *hs, ml_f, ml_b = _ml_pair(tag + "ml", q_c, k_c, mv, small, gate_row, ml_s[0], ml_s[1])
    ml_new = (ml_f, ml_b)
    (pc,) = _rowwise_linear(tag + "brc", _f_ml_post, [hs[0], hs[1], mo], [p['ml_norm_w'][None]], [(p['w_br_c'], None)], to_linear=(2,))
    return (ga, gb, gc, pa, pb, pc), ((s_f, s_b), tuple(ssd_new), tuple(ml_new))


def _merge(tag, br, p):
    return _rowwise_linear(tag + "out", _f_merge, list(br), [], [(p['w_out'], None)], to_linear=tuple(range(6)))[0]


def _sublayers(tag, xin, o, mods, p, l):
    sh2, sc2, g1, g2 = mods
    (x1,) = _rowwise(tag + "ln1", _f_resid_ln, [xin, o], [g1, p['b_out'][None], p['ln1_g'][None], p['ln1_b'][None]], [D_MODEL], to_linear=(1,))
    (pre,) = _rowwise_linear(tag + "ff1", _f_modulate, [x1], [sh2, sc2], [(p['w_ff1'], None)])
    (o2,) = _rowwise_linear(tag + "ff2", _f_relu2, [pre], [p['b_ff1'][None]], [(p['w_ff2'], None)], to_linear=(0,))
    (x2,) = _rowwise(tag + "ln2", _f_resid_ln, [x1, o2], [g2, p['b_ff2'][None], p['ln2_g'][None], p['ln2_b'][None]], [D_MODEL], to_linear=(1,))
    return x2


def _to_col_major(h):
    s, d = h.shape
    return h.reshape(s // GRID_W, GRID_W, d).swapaxes(0, 1).reshape(s, d)


def _from_col_major(h):
    s, d = h.shape
    return h.reshape(GRID_W, s // GRID_W, d).swapaxes(0, 1).reshape(s, d)


def _forward(x, wts, mods, ctx):
    zeros = lambda *s: jnp.zeros(s, F32)
    ctx_init = ((zeros(1, D_MODEL), zeros(1, D_MODEL)),
                (zeros(SSD_INNER, SSD_STATE), zeros(SSD_INNER, SSD_STATE)),
                tuple((zeros(ML_HEADS * ML_HD, ML_HD), zeros(ML_HEADS, 1, ML_HD), zeros(ML_HEADS, 1, LANES)) for _ in range(2)))
    for l in range(DEPTH):
        p = {n: wts[n][l] for n in wts}
        tag = "l%d" % l
        sh1x, sc1x, g1x, sh2x, sc2x, g2x = [mods[l][0][:, i * D_MODEL:(i + 1) * D_MODEL] for i in range(6)]
        sh1c, sc1c, g1c, sh2c, sc2c, g2c = [mods[l][1][:, i * D_MODEL:(i + 1) * D_MODEL] for i in range(6)]
        br_c, ctx_states = _mixer(tag + "c", ctx, sh1c, sc1c, p, ctx_init)
        br_x, _ = _mixer(tag + "x", _to_col_major(x) if l % 2 == 1 else x, sh1x, sc1x, p, ctx_states)
        ox = _merge(tag + "x", br_x, p)
        if l % 2 == 1:
            ox = _from_col_major(ox)
        x = _sublayers(tag + "x", x, ox, (sh2x, sc2x, g1x, g2x), p, l)
        if l < DEPTH - 1:
            ctx = _sublayers(tag + "c", ctx, _merge(tag + "c", br_c, p), (sh2c, sc2c, g1c, g2c), p, l)
    return x


_ADA_ROWS = 2 * SUBLANES


def _ada_forward(c, c_ctx, w_ada, b_ada, me):
    c_all = _exchange("gather_c", jnp.broadcast_to(c, (SUBLANES, D_MODEL)), True)[:, 0]

    def rows_of(c_ctx_):
        pad = jnp.zeros((_ADA_ROWS - N_DEV - 1, D_MODEL), F32)
        return jax.nn.silu(jnp.concatenate([c_all, c_ctx_[None], pad], axis=0))

    rows, vjp_rows = jax.vjp(rows_of, c_ctx)
    cols, vjp_cols = jax.vjp(lambda r, w: jnp.stack([_linear("ada%d" % l, r, w[l]) for l in range(DEPTH)]), rows, w_ada)
    full = _exchange("gather_mod", cols, True).transpose(1, 2, 0, 3).reshape(DEPTH, _ADA_ROWS, 6 * D_MODEL) + b_ada[:, None, :]
    mods = [(lax.dynamic_slice_in_dim(full[l], me, 1, axis=0), full[l][N_DEV:N_DEV + 1]) for l in range(DEPTH)]
    return mods, (vjp_rows, vjp_cols)


def _ada_backward(saved, dmods):
    vjp_rows, vjp_cols = saved
    wcol = 6 * D_MODEL // N_DEV
    pad = jnp.zeros((SUBLANES - 2, 6 * D_MODEL), F32)
    both = jnp.stack([jnp.concatenate([dx, dc, pad], axis=0) for dx, dc in dmods])
    send = both.reshape(DEPTH, SUBLANES, N_DEV, wcol).transpose(2, 0, 1, 3)
    recv = _exchange("scatter_dmod", send, False)
    ctx_row = recv[0, :, 1]
    for k in range(1, N_DEV):
        ctx_row = ctx_row + recv[k, :, 1]
    g = jnp.concatenate([recv[:, :, 0].transpose(1, 0, 2), ctx_row[:, None],
                         jnp.zeros((DEPTH, _ADA_ROWS - N_DEV - 1, wcol), F32)], axis=1)
    d_rows, d_w = vjp_cols(g)
    (d_c_ctx,) = vjp_rows(d_rows)
    d_b = jnp.stack([(dx + dc)[0] for dx, dc in dmods])
    return d_w, d_b, d_c_ctx


def _loss_and_cotangent(y, target):
    n_rows, d = y.shape
    tt = _row_tile(n_rows, 0, cap=256)

    def kern(y_ref, t_ref, dy_ref, acc_ref):
        @pl.when(pl.program_id(0) == 0)
        def _():
            acc_ref[...] = jnp.zeros_like(acc_ref)

        err = y_ref[...] - t_ref[...]
        dy_ref[...] = err * (1.0 / d)
        acc_ref[...] += jnp.sum(jnp.square(err))

    spec = pl.BlockSpec((tt, d), lambda i: (i, 0))
    dy, acc = pl.pallas_call(
        kern, grid=(n_rows // tt,), name="loss", in_specs=[spec, spec],
        out_specs=[spec, pl.BlockSpec((SUBLANES, LANES), lambda i: (0, 0))],
        out_shape=[jax.ShapeDtypeStruct((n_rows, d), F32), jax.ShapeDtypeStruct((SUBLANES, LANES), F32)],
    )(y, target)
    return acc[0, 0] * (0.5 / d), dy


def _exchange(name, src, gather):
    slab = src.shape if gather else src.shape[1:]

    def body(src_ref, out_ref, send_sems, recv_sems, local_sem):
        x, y, c = lax.axis_index("x"), lax.axis_index("y"), lax.axis_index("c")
        me = 4 * x + 2 * y + c
        local = pltpu.make_async_copy(src_ref if gather else src_ref.at[me], out_ref.at[me], local_sem)
        local.start()
        copies = []
        for d in range(1, N_DEV):
            px, py, pc = lax.rem(x + (d >> 2), 2), lax.rem(y + ((d >> 1) & 1), 2), lax.rem(c + (d & 1), 2)
            peer = 4 * px + 2 * py + pc
            cp = pltpu.make_async_remote_copy(
                src_ref=src_ref if gather else src_ref.at[peer], dst_ref=out_ref.at[me],
                send_sem=send_sems.at[d - 1], recv_sem=recv_sems.at[d - 1],
                device_id=(px, py, pc), device_id_type=pl.DeviceIdType.MESH)
            cp.start()
            copies.append(cp)
        for cp in copies:
            cp.wait()
        local.wait()

    return pl.pallas_call(
        body, name=name, out_shape=jax.ShapeDtypeStruct((N_DEV,) + tuple(slab), src.dtype),
        in_specs=[pl.BlockSpec(memory_space=pl.ANY)], out_specs=pl.BlockSpec(memory_space=pl.ANY),
        scratch_shapes=[pltpu.SemaphoreType.DMA((N_DEV - 1,)), pltpu.SemaphoreType.DMA((N_DEV - 1,)), pltpu.SemaphoreType.DMA],
    )(src)


_HBM = pl.BlockSpec(memory_space=pl.ANY)
_CHIPS = ((0, 0), (0, 1), (1, 0), (1, 1))


def _gather_two_level(name, srcs):
    n = len(srcs)

    def body(*refs):
        src_refs, out_refs = refs[:n], refs[n:2 * n]
        send_sems, recv_sems, local_sems = refs[2 * n:]
        x, y, c = lax.axis_index("x"), lax.axis_index("y"), lax.axis_index("c")
        me, sibling = (x, y, c), (x, y, 1 - c)
        chips = [(1 - x, y), (x, 1 - y), (1 - x, 1 - y)]

        def slab(a, px, py, pc):
            return out_refs[a].at[4 * px + 2 * py + pc]

        def copy(a, k, block, to, own=False):
            return pltpu.make_async_remote_copy(
                src_ref=src_refs[a] if own else slab(a, *block), dst_ref=slab(a, *block), send_sem=send_sems.at[a, k],
                recv_sem=recv_sems.at[a, k], device_id=to, device_id_type=pl.DeviceIdType.MESH)

        mine = [pltpu.make_async_copy(src_refs[a], slab(a, *me), local_sems.at[a]) for a in range(n)]
        first = []
        for a in range(n):
            mine[a].start()
            first += [copy(a, 0, me, sibling, own=True)] + [copy(a, 1 + j, me, (*chip, c), own=True) for j, chip in enumerate(chips)]
        for cp in first:
            cp.start()
        passed = []
        for j, chip in enumerate(chips):
            for a in range(n):
                copy(a, 1 + j, (*chip, c), me).wait_recv()
                passed.append(copy(a, 4 + j, (*chip, c), sibling))
                passed[-1].start()
        for a in range(n):
            copy(a, 0, sibling, me).wait_recv()
        for j, chip in enumerate(chips):
            for a in range(n):
                copy(a, 4 + j, (*chip, 1 - c), me).wait_recv()
        for cp in first + passed:
            cp.wait_send()
        for cp in mine:
            cp.wait()

    return pl.pallas_call(
        body, name=name, out_shape=[jax.ShapeDtypeStruct((N_DEV,) + tuple(s.shape), s.dtype) for s in srcs],
        in_specs=[_HBM] * n, out_specs=[_HBM] * n,
        scratch_shapes=[pltpu.SemaphoreType.DMA((n, N_DEV - 1)), pltpu.SemaphoreType.DMA((n, N_DEV - 1)), pltpu.SemaphoreType.DMA((n,))],
    )(*srcs)


def _scatter_to_sibling(name, parts_list):
    n = len(parts_list)

    def body(*refs):
        p_refs, out_refs = refs[:n], refs[n:2 * n]
        send_sems, recv_sems = refs[2 * n:]
        x, y, c = lax.axis_index("x"), lax.axis_index("y"), lax.axis_index("c")
        copies = []
        for a in range(n):
            for j, (px, py) in enumerate(_CHIPS):
                cp = pltpu.make_async_remote_copy(
                    src_ref=p_refs[a].at[4 * px + 2 * py + (1 - c)], dst_ref=out_refs[a].at[j], send_sem=send_sems.at[a, j],
                    recv_sem=recv_sems.at[a, j], device_id=(x, y, 1 - c), device_id_type=pl.DeviceIdType.MESH)
                cp.start()
                copies.append(cp)
        for cp in copies:
            cp.wait()

    return pl.pallas_call(
        body, name=name, out_shape=[jax.ShapeDtypeStruct((4,) + tuple(p.shape[1:]), p.dtype) for p in parts_list],
        in_specs=[_HBM] * n, out_specs=[_HBM] * n,
        scratch_shapes=[pltpu.SemaphoreType.DMA((n, 4)), pltpu.SemaphoreType.DMA((n, 4))],
    )(*parts_list)


def _chip_sum(name, parts, from_sibling):
    _, rows, cols = parts.shape
    lanes = -(-cols // LANES) * LANES
    tr = _row_tile(rows, 4 * lanes * 4 * 2, budget=24 << 20)

    def kern(p_ref, s_ref, o_ref):
        c = lax.axis_index("c")
        o_ref[0] = (jnp.where(c == 0, p_ref[0, 0], p_ref[0, 1]) + s_ref[0]).astype(o_ref.dtype)

    return pl.pallas_call(
        kern, grid=(4, rows // tr), name=name,
        in_specs=[pl.BlockSpec((1, 2, tr, cols), lambda j, i: (j, 0, i, 0)), pl.BlockSpec((1, tr, cols), lambda j, i: (j, i, 0))],
        out_specs=pl.BlockSpec((1, tr, cols), lambda j, i: (j, i, 0)),
        out_shape=jax.ShapeDtypeStruct((4, rows, cols), BF16),
        compiler_params=_params(4 * lanes * tr * 4 * 2),
    )(parts.reshape(4, 2, rows, cols), from_sibling)


def _scatter_across_chips(name, sums_list):
    n = len(sums_list)

    def body(*refs):
        q_refs, out_refs = refs[:n], refs[n:2 * n]
        send_sems, recv_sems, local_sems = refs[2 * n:]
        x, y, c = lax.axis_index("x"), lax.axis_index("y"), lax.axis_index("c")
        own = 2 * x + y
        copies = []
        for a in range(n):
            local = pltpu.make_async_copy(q_refs[a].at[own], out_refs[a].at[own], local_sems.at[a])
            local.start()
            copies.append(local)
            for d in range(1, 4):
                px, py = lax.rem(x + (d >> 1), 2), lax.rem(y + (d & 1), 2)
                cp = pltpu.make_async_remote_copy(
                    src_ref=q_refs[a].at[2 * px + py], dst_ref=out_refs[a].at[own], send_sem=send_sems.at[a, d - 1],
                    recv_sem=recv_sems.at[a, d - 1], device_id=(px, py, c), device_id_type=pl.DeviceIdType.MESH)
                cp.start()
                copies.append(cp)
        for cp in copies:
            cp.wait()

    return pl.pallas_call(
        body, name=name, out_shape=[jax.ShapeDtypeStruct(s.shape, s.dtype) for s in sums_list],
        in_specs=[_HBM] * n, out_specs=[_HBM] * n,
        scratch_shapes=[pltpu.SemaphoreType.DMA((n, 3)), pltpu.SemaphoreType.DMA((n, 3)), pltpu.SemaphoreType.DMA((n,))],
    )(*sums_list)


def _sum_parts(name, parts):
    n_parts, rows, cols = parts.shape
    tr = _row_tile(rows, 4 * cols * (n_parts + 1) * 2)

    def kern(p_ref, o_ref):
        acc = p_ref[0]
        for k in range(1, n_parts):
            acc = acc + p_ref[k]
        o_ref[...] = acc

    return pl.pallas_call(
        kern, grid=(rows // tr,), name=name, in_specs=[pl.BlockSpec((n_parts, tr, cols), lambda i: (0, i, 0))],
        out_specs=pl.BlockSpec((tr, cols), lambda i: (i, 0)), out_shape=jax.ShapeDtypeStruct((rows, cols), F32),
    )(parts)


def _adamw(name, w, m, v, parts):
    n_parts, rows, cols = parts.shape
    lanes = -(-cols // LANES) * LANES
    tr = _row_tile(rows, 4 * lanes * (n_parts + 7) * 2, budget=28 << 20)
    c1 = np.float32(1.0 - ADAM_B1 ** ADAM_STEP)
    c2 = np.float32(1.0 - ADAM_B2 ** ADAM_STEP)

    def kern(w_ref, m_ref, v_ref, p_ref, g_ref, d_ref, nm_ref, nv_ref):
        g = p_ref[0].astype(F32)
        for k in range(1, n_parts):
            g = g + p_ref[k].astype(F32)
        m_new = ADAM_B1 * m_ref[...] + (1.0 - ADAM_B1) * g
        v_new = ADAM_B2 * v_ref[...] + (1.0 - ADAM_B2) * jnp.square(g)
        g_ref[...] = g
        nm_ref[...] = m_new
        nv_ref[...] = v_new
        d_ref[...] = -ADAM_LR * ((m_new / c1) / (jnp.sqrt(v_new / c2) + ADAM_EPS) + ADAM_WD * w_ref[...])

    spec = pl.BlockSpec((tr, cols), lambda i: (i, 0))
    return pl.pallas_call(
        kern, grid=(rows // tr,), name=name,
        in_specs=[spec, spec, spec, pl.BlockSpec((n_parts, tr, cols), lambda i: (0, i, 0))], out_specs=[spec] * 4,
        out_shape=[jax.ShapeDtypeStruct((rows, cols), F32)] * 4,
        compiler_params=_params(4 * lanes * tr * (n_parts + 7) * 2),
    )(w, m, v, parts)


def _packed_rows(shape):
    return -(-int(np.prod(shape)) // (SUBLANES * LANES)) * SUBLANES


def _pack(arrays, row_multiple):
    parts = []
    for a in arrays:
        n = int(np.prod(a.shape))
        r = _packed_rows(a.shape)
        parts.append(jnp.pad(a.reshape(-1), (0, r * LANES - n)).reshape(r, LANES))
    rows = sum(p.shape[0] for p in parts)
    total = -(-rows // row_multiple) * row_multiple
    if total > rows:
        parts.append(jnp.zeros((total - rows, LANES), arrays[0].dtype))
    return jnp.concatenate(parts, axis=0)


def _unpack(packed, shapes):
    out, off = [], 0
    for s in shapes:
        r = _packed_rows(s)
        out.append(packed[off:off + r].reshape(-1)[:int(np.prod(s))].reshape(s))
        off += r
    return out


def _split_w_in(w_in):
    main = jnp.concatenate([w_in[:, :, s:e] for s, e in _IN_MAIN], axis=2)
    pad = jnp.zeros(w_in.shape[:2] + (LANES - 80,), w_in.dtype)
    small = jnp.concatenate([w_in[:, :, s:e] for s, e in _IN_SMALL] + [pad], axis=2)
    return main, small


def _join_w_in(main, small):
    return jnp.concatenate([main[:, :, 0:8192], small[:, :, 0:64], main[:, :, 8192:12288], small[:, :, 64:80],
                            main[:, :, 12288:15360]], axis=2)


def _unshard(gathered, axis):
    nd, nl, r, c = gathered.shape
    if axis == 1:
        return gathered.transpose(1, 0, 2, 3).reshape(nl, nd * r, c)
    return gathered.transpose(1, 2, 0, 3).reshape(nl, r, nd * c)


def _reshard(full, axis):
    nl, r, c = full.shape
    if axis == 1:
        return full.reshape(nl, N_DEV, r // N_DEV, c).transpose(1, 0, 2, 3)
    return full.reshape(nl, r, N_DEV, c // N_DEV).transpose(2, 0, 1, 3)


def kernel(x, c, ctx, c_ctx, w_ada, b_ada, w_in, lru_conv_w, lru_conv_b, lru_w_r, lru_b_r, lru_w_i, lru_b_i, lru_lambda, ssd_conv_w, ssd_conv_b, ssd_dt_bias, ssd_a_log, ssd_d, ssd_norm_w, ml_conv_w, ml_conv_b, ml_gate_b, ml_norm_w, w_br_a, w_br_b, w_br_c, w_out, b_out, ln1_g, ln1_b, w_ff1, b_ff1, w_ff2, b_ff2, ln2_g, ln2_b, loss_target, m_c_ctx, m_w_ada, m_b_ada, m_w_in, m_lru_conv_w, m_lru_conv_b, m_lru_w_r, m_lru_b_r, m_lru_w_i, m_lru_b_i, m_lru_lambda, m_ssd_conv_w, m_ssd_conv_b, m_ssd_dt_bias, m_ssd_a_log, m_ssd_d, m_ssd_norm_w, m_ml_conv_w, m_ml_conv_b, m_ml_gate_b, m_ml_norm_w, m_w_br_a, m_w_br_b, m_w_br_c, m_w_out, m_b_out, m_ln1_g, m_ln1_b, m_w_ff1, m_b_ff1, m_w_ff2, m_b_ff2, m_ln2_g, m_ln2_b, v_c_ctx, v_w_ada, v_b_ada, v_w_in, v_lru_conv_w, v_lru_conv_b, v_lru_w_r, v_lru_b_r, v_lru_w_i, v_lru_b_i, v_lru_lambda, v_ssd_conv_w, v_ssd_conv_b, v_ssd_dt_bias, v_ssd_a_log, v_ssd_d, v_ssd_norm_w, v_ml_conv_w, v_ml_conv_b, v_ml_gate_b, v_ml_norm_w, v_w_br_a, v_w_br_b, v_w_br_c, v_w_out, v_b_out, v_ln1_g, v_ln1_b, v_w_ff1, v_b_ff1, v_w_ff2, v_b_ff2, v_ln2_g, v_ln2_b):
    a = dict(locals())
    me = 4 * lax.axis_index("x") + 2 * lax.axis_index("y") + lax.axis_index("c")

    wts = {n: a[n] for n in _REPLICATED if n not in ('c_ctx', 'b_ada')}
    exchanged = [n for n in _BIG if n != 'w_ada']
    gathered = _gather_two_level("gather_weights", [a[n].astype(BF16) for n in exchanged])
    for n, g in zip(exchanged, gathered):
        full = _unshard(g, _BIG[n])
        if n == 'w_in':
            main, small = _split_w_in(full)
            wts['w_in_main'], wts['w_in_small'] = main.astype(F32), small.astype(F32)
        else:
            wts[n] = full.astype(F32)
    small_shapes = [a[n].shape for n in _SMALL_SHARDED]
    small_all = _exchange("gather_small", _pack([a[n] for n in _SMALL_SHARDED], SUBLANES), True)
    per_dev = [_unpack(small_all[k], small_shapes) for k in range(N_DEV)]
    for i, n in enumerate(_SMALL_SHARDED):
        wts[n] = jnp.concatenate([per_dev[k][i] for k in range(N_DEV)], axis=-1)

    mods, ada_saved = _ada_forward(c, c_ctx, w_ada, b_ada, me)
    y, vjp = jax.vjp(functools.partial(_forward, ctx=ctx[0]), x[0], wts, mods)
    loss_local, dy = _loss_and_cotangent(y, loss_target[0])
    grad_x, grads, dmods = vjp(dy)
    grads['w_in'] = _join_w_in(grads.pop('w_in_main'), grads.pop('w_in_small'))
    grad_w_ada, grads['b_ada'], grads['c_ctx'] = _ada_backward(ada_saved, dmods)
    loss = lax.psum(loss_local, ("x", "y", "c"))

    out = {}

    def put(n, res, shape):
        for kind, r in zip(("grad_", "delta_", "new_m_", "new_v_"), res):
            out[kind + n] = r.reshape(shape)

    flat = {n: (a[n].shape[0] * a[n].shape[1], a[n].shape[2]) for n in _BIG}
    by_dest = [_reshard(grads[n], _BIG[n]).reshape(N_DEV, *flat[n]) for n in exchanged]
    from_sibling = _scatter_to_sibling("scatter_d2d", by_dest)
    chip_sums = [_chip_sum("chipsum_" + n, p, s) for n, p, s in zip(exchanged, by_dest, from_sibling)]
    summed = dict(zip(exchanged, _scatter_across_chips("scatter_ici", chip_sums)))
    summed['w_ada'] = grad_w_ada.reshape(1, *flat['w_ada'])
    for n in _BIG:
        shp = a[n].shape
        rows, cols = flat[n]
        parts = summed[n]
        put(n, _adamw("adamw_" + n, a[n].reshape(rows, cols), a["m_" + n].reshape(rows, cols), a["v_" + n].reshape(rows, cols), parts), shp)

    rep_names = _REPLICATED + _SMALL_SHARDED
    chunk_rows = SUBLANES * N_DEV
    g_pack = _pack([grads[n] for n in rep_names], chunk_rows * N_DEV)
    rows = g_pack.shape[0]
    parts = _exchange("scatter_rep", g_pack.reshape(N_DEV, rows // N_DEV, LANES), False)
    mine = _sum_parts("sum_rep", parts)
    g_all = _exchange("gather_rep", mine, True).reshape(rows, LANES)
    g_full = _unpack(g_all, [grads[n].shape for n in rep_names])
    g_local = []
    for n, g in zip(rep_names, g_full):
        if n in _SMALL_SHARDED:
            width = a[n].shape[-1]
            g = lax.dynamic_slice_in_dim(g, me * width, width, axis=g.ndim - 1)
        g_local.append(g)
    shapes = [a[n].shape for n in rep_names]
    res = _adamw("adamw_rep", _pack([a[n] for n in rep_names], chunk_rows), _pack([a["m_" + n] for n in rep_names], chunk_rows),
                 _pack([a["v_" + n] for n in rep_names], chunk_rows), _pack(g_local, chunk_rows)[None])
    unpacked = [_unpack(r, shapes) for r in res]
    for i, n in enumerate(rep_names):
        put(n, [u[i] for u in unpacked], shapes[i])

    outs = [loss, grad_x[None]]
    for kind in ("grad_", "delta_", "new_m_", "new_v_"):
        outs += [out[kind + n] for n in _WEIGHTS]
    return tuple(outs)
```
